```python
import math
import jax
import jax.numpy as jnp
from jax import lax
import numpy as np

D_MODEL = 1024
BATCH = 8
SEQ = 16384
DEPTH = 2

CHUNK = 64
CONV_WIDTH = 4
EPS = 1e-6

RET_HEADS = 4
RET_DK = 128
RET_DV = 128
RET_QK_W = RET_HEADS * RET_DK
RET_V_W = RET_HEADS * RET_DV
ROPE_THETA = 10000.0

SSD_HEADS = 8
SSD_HEAD_DIM = 64
SSD_GROUPS = 2
SSD_STATE = 128
SSD_INNER = SSD_HEADS * SSD_HEAD_DIM
SSD_XBC_W = SSD_INNER + 2 * SSD_GROUPS * SSD_STATE

GDN_HEADS = 6
GDN_DK = 128
GDN_DV = 128
GDN_QK_W = GDN_HEADS * GDN_DK
GDN_V_W = GDN_HEADS * GDN_DV
GDN_QKV_W = 2 * GDN_QK_W + GDN_V_W

S5_CH = 256
S5_GROUP = 16
S5_GROUPS = S5_CH // S5_GROUP
S5_STATE = 64

D_FF = 4 * D_MODEL

MIX0_W = RET_V_W + SSD_INNER
MIX1_W = GDN_V_W + S5_CH
IN0_W = 2 * RET_QK_W + 2 * RET_V_W + SSD_INNER + SSD_XBC_W + SSD_HEADS
IN1_W = GDN_QKV_W + GDN_V_W + 2 * GDN_HEADS + S5_CH

kernel_name = "hybrid_retention_ssd_gdn_s5_trunk"


def rmsnorm(x, w):
    xf = x.astype(jnp.float32)
    y = xf * lax.rsqrt(jnp.mean(xf * xf, axis=-1, keepdims=True) + EPS)
    return (y * w.astype(jnp.float32)).astype(x.dtype)


def unit_rms(x):
    return x * lax.rsqrt(jnp.mean(x * x, axis=-1, keepdims=True) + EPS)


def l2norm(x):
    return x * lax.rsqrt(jnp.sum(x * x, axis=-1, keepdims=True) + EPS)


def causal_dwconv(x, w):
    k = w.shape[0]
    return lax.conv_general_dilated(
        x, w[:, None, :].astype(x.dtype), window_strides=(1,), padding=[(k - 1, 0)],
        dimension_numbers=("NWC", "WIO", "NWC"), feature_group_count=x.shape[-1])


def rotary(x, pos):
    half = x.shape[-1] // 2
    inv = ROPE_THETA ** (-jnp.arange(half, dtype=jnp.float32) / half)
    ang = pos[:, None] * inv[None, :]
    cos = jnp.cos(ang)[None, :, None, :]
    sin = jnp.sin(ang)[None, :, None, :]
    x1, x2 = x[..., :half], x[..., half:]
    return jnp.concatenate([x1 * cos - x2 * sin, x1 * sin + x2 * cos], axis=-1)


def retention_chunkwise(q, k, v):
    bsz, seqlen, nh, dk = q.shape
    dv = v.shape[-1]
    nc = seqlen // CHUNK
    log_gamma = jnp.log(1.0 - 2.0 ** (-5.0 - jnp.arange(nh, dtype=jnp.float32)))
    q = q.reshape(bsz, nc, CHUNK, nh, dk) * (dk ** -0.5)
    k = k.reshape(bsz, nc, CHUNK, nh, dk)
    v = v.reshape(bsz, nc, CHUNK, nh, dv)
    idx = jnp.arange(CHUNK, dtype=jnp.float32)
    diff = idx[:, None] - idx[None, :]
    causal = diff >= 0
    dmask = jnp.exp(jnp.where(causal[None], log_gamma[:, None, None] * diff[None], -jnp.inf))
    scores = jnp.einsum("bclhd,bcshd->bchls", q, k) * dmask[None, None]
    y_intra = jnp.einsum("bchls,bcshe->bclhe", scores, v)
    k_w = k * jnp.exp(log_gamma[None, :] * (CHUNK - 1.0 - idx)[:, None])[None, None, :, :, None]
    chunk_kv = jnp.einsum("bclhd,bclhe->bchde", k_w, v)
    chunk_decay = jnp.exp(log_gamma * CHUNK)[None, :, None, None]

    def step(state, kv):
        return state * chunk_decay + kv, state

    init = jnp.zeros((bsz, nh, dk, dv), jnp.float32)
    _, prev = lax.scan(step, init, jnp.moveaxis(chunk_kv, 1, 0))
    prev = jnp.moveaxis(prev, 0, 1)
    q_w = q * jnp.exp(log_gamma[None, :] * (idx + 1.0)[:, None])[None, None, :, :, None]
    y_inter = jnp.einsum("bclhd,bchde->bclhe", q_w, prev)
    return (y_intra + y_inter).reshape(bsz, seqlen, nh, dv)


def ssd_chunked(x, dt, a, bm, cm):
    bsz, seqlen, nh, p = x.shape
    ng, n = bm.shape[-2:]
    nj = nh // ng
    nc = seqlen // CHUNK
    xd = (x * dt[..., None]).reshape(bsz, nc, CHUNK, ng, nj, p)
    la = (dt * a).reshape(bsz, nc, CHUNK, ng, nj)
    la_cum = jnp.cumsum(la, axis=2)
    bc = bm.reshape(bsz, nc, CHUNK, ng, n)
    cc = cm.reshape(bsz, nc, CHUNK, ng, n)
    causal = jnp.tril(jnp.ones((CHUNK, CHUNK), bool))[None, None, :, :, None, None]
    seg = la_cum[:, :, :, None] - la_cum[:, :, None, :]
    lmat = jnp.exp(jnp.where(causal, seg, -jnp.inf))
    cb = jnp.einsum("bclgn,bcsgn->bclsg", cc, bc)
    y_diag = jnp.einsum("bclsgj,bcsgjp->bclgjp", cb[..., None] * lmat, xd)
    decay_to_end = jnp.exp(la_cum[:, :, -1:] - la_cum)
    chunk_states = jnp.einsum("bclgn,bclgj,bclgjp->bcgjpn", bc, decay_to_end, xd)
    chunk_decay = jnp.exp(la_cum[:, :, -1])

    def step(state, inp):
        s, d = inp
        return state * d[..., None, None] + s, state

    init = jnp.zeros((bsz, ng, nj, p, n), jnp.float32)
    _, prev = lax.scan(step, init, (jnp.moveaxis(chunk_states, 1, 0), jnp.moveaxis(chunk_decay, 1, 0)))
    prev = jnp.moveaxis(prev, 0, 1)
    y_off = jnp.einsum("bclgn,bcgjpn,bclgj->bclgjp", cc, prev, jnp.exp(la_cum))
    return (y_diag + y_off).reshape(bsz, seqlen, nh, p)


def gated_delta_chunked(q, k, v, g, beta):
    bsz, seqlen, nh, dk = q.shape
    dv = v.shape[-1]
    nc = seqlen // CHUNK

    def to_chunks(t):
        return t.reshape(bsz, nc, CHUNK, nh, t.shape[-1]).transpose(0, 3, 1, 2, 4)

    q = to_chunks(q * (dk ** -0.5))
    k = to_chunks(k)
    v = to_chunks(v)
    g = g.reshape(bsz, nc, CHUNK, nh).transpose(0, 3, 1, 2)
    beta = beta.reshape(bsz, nc, CHUNK, nh).transpose(0, 3, 1, 2)
    g_cum = jnp.cumsum(g, axis=-1)
    causal = jnp.tril(jnp.ones((CHUNK, CHUNK), bool))
    strict = jnp.tril(jnp.ones((CHUNK, CHUNK), bool), k=-1)
    decay = jnp.exp(jnp.where(causal, g_cum[..., :, None] - g_cum[..., None, :], -jnp.inf))
    k_beta = k * beta[..., None]
    v_beta = v * beta[..., None]
    lower = jnp.where(strict, jnp.einsum("bhcld,bhcsd->bhcls", k_beta, k) * decay, 0.0)
    eye = jnp.eye(CHUNK, dtype=jnp.float32)
    t_inv = lax.linalg.triangular_solve(eye + lower, jnp.broadcast_to(eye, lower.shape),
                                        left_side=True, lower=True)
    u = t_inv @ v_beta
    w = t_inv @ (k_beta * jnp.exp(g_cum)[..., None])
    attn = jnp.where(causal, jnp.einsum("bhcld,bhcsd->bhcls", q, k) * decay, 0.0)
    q_g = q * jnp.exp(g_cum)[..., None]
    k_tail = k * jnp.exp(g_cum[..., -1:] - g_cum)[..., None]
    chunk_decay = jnp.exp(g_cum[..., -1])

    def step(s, inp):
        q_i, w_i, u_i, a_i, kt_i, d_i = inp
        v_new = u_i - w_i @ s
        o = q_i @ s + a_i @ v_new
        s = s * d_i[..., None, None] + jnp.einsum("bhcd,bhce->bhde", kt_i, v_new)
        return s, o

    xs = tuple(jnp.moveaxis(t, 2, 0) for t in (q_g, w, u, attn, k_tail, chunk_decay))
    init = jnp.zeros((bsz, nh, dk, dv), jnp.float32)
    _, o = lax.scan(step, init, xs)
    return o.transpose(1, 0, 3, 2, 4).reshape(bsz, seqlen, nh, dv)


def s5_group_ssm(u, a_re, a_im, log_step, b_re, b_im, c_re, c_im, d_skip, w_glu, b_glu):
    f32 = jnp.float32
    bsz, seqlen, _ = u.shape
    ug = u.reshape(bsz, seqlen, S5_GROUPS, S5_GROUP)
    lam = lax.complex(a_re.astype(f32), a_im.astype(f32))
    step = jnp.exp(log_step.astype(f32))[:, None]
    lam_bar = jnp.exp(lam * step)
    b_mat = lax.complex(b_re.astype(f32), b_im.astype(f32))
    b_bar = ((lam_bar - 1.0) / lam)[..., None] * b_mat
    bu = jnp.einsum("blgc,gnc->blgn", ug.astype(jnp.complex64), b_bar)
    a_seq = jnp.broadcast_to(lam_bar, bu.shape)

    def combine(left, right):
        a_l, b_l = left
        a_r, b_r = right
        return a_r * a_l, a_r * b_l + b_r

    _, h = lax.associative_scan(combine, (a_seq, bu), axis=1)
    c_mat = lax.complex(c_re.astype(f32), c_im.astype(f32))
    y = jnp.einsum("blgn,gcn->blgc", h, c_mat).real + d_skip.astype(f32).reshape(S5_GROUPS, S5_GROUP) * ug
    y = jax.nn.gelu(y.reshape(bsz, seqlen, S5_CH))
    return y * jax.nn.sigmoid(y @ w_glu.astype(f32) + b_glu.astype(f32))


def retention_ssd_mixer(h, w_in, ssd_conv_w, ssd_conv_b, ssd_dt_bias, ssd_A_log, ssd_D, ssd_norm_w, w_out):
    f32 = jnp.float32
    bsz, seqlen, _ = h.shape
    proj = h @ w_in
    offs = np.cumsum([RET_QK_W, RET_QK_W, RET_V_W, RET_V_W, SSD_INNER, SSD_XBC_W]).tolist()
    q, k, v, gate, z, xbc, dt = jnp.split(proj, offs, axis=-1)
    pos = jnp.arange(seqlen, dtype=f32)
    q = rotary(q.astype(f32).reshape(bsz, seqlen, RET_HEADS, RET_DK), pos)
    k = rotary(k.astype(f32).reshape(bsz, seqlen, RET_HEADS, RET_DK), pos)
    v = v.astype(f32).reshape(bsz, seqlen, RET_HEADS, RET_DV)
    r = unit_rms(retention_chunkwise(q, k, v)).reshape(bsz, seqlen, RET_V_W)
    ret_out = jax.nn.silu(gate.astype(f32)) * r
    xbc = jax.nn.silu(causal_dwconv(xbc, ssd_conv_w) + ssd_conv_b.astype(xbc.dtype)).astype(f32)
    xs, bm, cm = jnp.split(xbc, [SSD_INNER, SSD_INNER + SSD_GROUPS * SSD_STATE], axis=-1)
    dt = jax.nn.softplus(dt.astype(f32) + ssd_dt_bias.astype(f32))
    a = -jnp.exp(ssd_A_log.astype(f32))
    xs = xs.reshape(bsz, seqlen, SSD_HEADS, SSD_HEAD_DIM)
    y = ssd_chunked(xs, dt, a, bm.reshape(bsz, seqlen, SSD_GROUPS, SSD_STATE),
                    cm.reshape(bsz, seqlen, SSD_GROUPS, SSD_STATE))
    y = (y + ssd_D.astype(f32)[:, None] * xs).reshape(bsz, seqlen, SSD_INNER)
    yg = (y * jax.nn.silu(z.astype(f32))).reshape(bsz, seqlen, SSD_GROUPS, SSD_INNER // SSD_GROUPS)
    ssd_out = unit_rms(yg).reshape(bsz, seqlen, SSD_INNER) * ssd_norm_w.astype(f32)
    mixed = jnp.concatenate([ret_out, ssd_out], axis=-1).astype(h.dtype)
    return mixed @ w_out


def deltanet_s5_mixer(h, w_in, gdn_conv_w, gdn_A_log, gdn_dt_bias, gdn_norm_w,
                      s5_A_re, s5_A_im, s5_log_step, s5_B_re, s5_B_im, s5_C_re, s5_C_im,
                      s5_D, s5_w_glu, s5_b_glu, w_out):
    f32 = jnp.float32
    bsz, seqlen, _ = h.shape
    proj = h @ w_in
    offs = np.cumsum([GDN_QKV_W, GDN_V_W, GDN_HEADS, GDN_HEADS]).tolist()
    qkv, z, b_raw, a_raw, u = jnp.split(proj, offs, axis=-1)
    qkv = jax.nn.silu(causal_dwconv(qkv, gdn_conv_w)).astype(f32)
    q, k, v = jnp.split(qkv, [GDN_QK_W, 2 * GDN_QK_W], axis=-1)
    q = l2norm(q.reshape(bsz, seqlen, GDN_HEADS, GDN_DK))
    k = l2norm(k.reshape(bsz, seqlen, GDN_HEADS, GDN_DK))
    v = v.reshape(bsz, seqlen, GDN_HEADS, GDN_DV)
    beta = jax.nn.sigmoid(b_raw.astype(f32))
    g = -jnp.exp(gdn_A_log.astype(f32)) * jax.nn.softplus(a_raw.astype(f32) + gdn_dt_bias.astype(f32))
    o = gated_delta_chunked(q, k, v, g, beta)
    o = unit_rms(o) * gdn_norm_w.astype(f32) * jax.nn.silu(z.astype(f32).reshape(bsz, seqlen, GDN_HEADS, GDN_DV))
    gdn_out = o.reshape(bsz, seqlen, GDN_V_W)
    s5_out = s5_group_ssm(u.astype(f32), s5_A_re, s5_A_im, s5_log_step, s5_B_re, s5_B_im,
                          s5_C_re, s5_C_im, s5_D, s5_w_glu, s5_b_glu)
    mixed = jnp.concatenate([gdn_out, s5_out], axis=-1).astype(h.dtype)
    return mixed @ w_out


def sqrelu_mlp(h, w_up, w_down):
    a = jax.nn.relu(h @ w_up)
    return (a * a) @ w_down


def _fwd_setup_inputs(seed: int = 0) -> dict:
    key = jax.random.key(seed)
    ks = jax.random.split(key, 34)
    f32 = jnp.float32

    def nrm(i, shape, scale):
        return scale * jax.random.normal(ks[i], shape, f32)

    def gain(i, n):
        return 1.0 + nrm(i, (n,), 0.02)

    def dt_bias(i, n):
        dt = jnp.exp(jax.random.uniform(ks[i], (n,), f32, math.log(1e-3), math.log(1e-1)))
        return dt + jnp.log(-jnp.expm1(-dt))

    def a_log(i, n):
        return jnp.log(jax.random.uniform(ks[i], (n,), f32, 1.0, 16.0))

    dm = D_MODEL ** -0.5
    return {
        "x": nrm(0, (BATCH, SEQ, D_MODEL), 1.0),
        "l0_norm_mix": gain(1, D_MODEL),
        "l0_w_in": nrm(2, (D_MODEL, IN0_W), dm),
        "ssd_conv_w": nrm(3, (CONV_WIDTH, SSD_XBC_W), CONV_WIDTH ** -0.5),
        "ssd_conv_b": nrm(4, (SSD_XBC_W,), 0.02),
        "ssd_dt_bias": dt_bias(5, SSD_HEADS),
        "ssd_A_log": a_log(6, SSD_HEADS),
        "ssd_D": gain(7, SSD_HEADS),
        "ssd_norm_w": gain(8, SSD_INNER),
        "l0_w_out": nrm(9, (MIX0_W, D_MODEL), MIX0_W ** -0.5),
        "l0_norm_mlp": gain(10, D_MODEL),
        "l0_w_up": nrm(11, (D_MODEL, D_FF), dm),
        "l0_w_down": nrm(12, (D_FF, D_MODEL), D_FF ** -0.5),
        "l1_norm_mix": gain(13, D_MODEL),
        "l1_w_in": nrm(14, (D_MODEL, IN1_W), dm),
        "gdn_conv_w": nrm(15, (CONV_WIDTH, GDN_QKV_W), CONV_WIDTH ** -0.5),
        "gdn_A_log": a_log(16, GDN_HEADS),
        "gdn_dt_bias": dt_bias(17, GDN_HEADS),
        "gdn_norm_w": gain(18, GDN_DV),
        "s5_A_re": -0.5 + nrm(19, (S5_GROUPS, S5_STATE), 0.01),
        "s5_A_im": math.pi * jnp.arange(S5_STATE, dtype=f32)[None, :] + nrm(20, (S5_GROUPS, S5_STATE), 0.01),
        "s5_log_step": jax.random.uniform(ks[21], (S5_GROUPS,), f32, math.log(1e-3), math.log(1e-1)),
        "s5_B_re": nrm(22, (S5_GROUPS, S5_STATE, S5_GROUP), (2 * S5_GROUP) ** -0.5),
        "s5_B_im": nrm(23, (S5_GROUPS, S5_STATE, S5_GROUP), (2 * S5_GROUP) ** -0.5),
        "s5_C_re": nrm(24, (S5_GROUPS, S5_GROUP, S5_STATE), (2 * S5_STATE) ** -0.5),
        "s5_C_im": nrm(25, (S5_GROUPS, S5_GROUP, S5_STATE), (2 * S5_STATE) ** -0.5),
        "s5_D": nrm(26, (S5_CH,), 1.0),
        "s5_w_glu": nrm(27, (S5_CH, S5_CH), S5_CH ** -0.5),
        "s5_b_glu": nrm(28, (S5_CH,), 0.02),
        "l1_w_out": nrm(29, (MIX1_W, D_MODEL), MIX1_W ** -0.5),
        "l1_norm_mlp": gain(30, D_MODEL),
        "l1_w_up": nrm(31, (D_MODEL, D_FF), dm),
        "l1_w_down": nrm(32, (D_FF, D_MODEL), D_FF ** -0.5),
        "final_norm": gain(33, D_MODEL),
    }


def _fwd_reference(x, l0_norm_mix, l0_w_in, ssd_conv_w, ssd_conv_b, ssd_dt_bias, ssd_A_log, ssd_D,
              ssd_norm_w, l0_w_out, l0_norm_mlp, l0_w_up, l0_w_down, l1_norm_mix, l1_w_in,
              gdn_conv_w, gdn_A_log, gdn_dt_bias, gdn_norm_w, s5_A_re, s5_A_im, s5_log_step,
              s5_B_re, s5_B_im, s5_C_re, s5_C_im, s5_D, s5_w_glu, s5_b_glu, l1_w_out,
              l1_norm_mlp, l1_w_up, l1_w_down, final_norm):
    for layer in range(DEPTH):
        if layer % 2 == 0:
            x = x + retention_ssd_mixer(rmsnorm(x, l0_norm_mix), l0_w_in, ssd_conv_w, ssd_conv_b,
                                        ssd_dt_bias, ssd_A_log, ssd_D, ssd_norm_w, l0_w_out)
            x = x + sqrelu_mlp(rmsnorm(x, l0_norm_mlp), l0_w_up, l0_w_down)
        else:
            x = x + deltanet_s5_mixer(rmsnorm(x, l1_norm_mix), l1_w_in, gdn_conv_w, gdn_A_log,
                                      gdn_dt_bias, gdn_norm_w, s5_A_re, s5_A_im, s5_log_step,
                                      s5_B_re, s5_B_im, s5_C_re, s5_C_im, s5_D, s5_w_glu,
                                      s5_b_glu, l1_w_out)
            x = x + sqrelu_mlp(rmsnorm(x, l1_norm_mlp), l1_w_up, l1_w_down)
    return rmsnorm(x, final_norm)


import jax as _jax
import jax.numpy as _jnp

TWIN_FORMAT = 'train_step'
FWD_PARAMS = ['x', 'l0_norm_mix', 'l0_w_in', 'ssd_conv_w', 'ssd_conv_b', 'ssd_dt_bias', 'ssd_A_log', 'ssd_D', 'ssd_norm_w', 'l0_w_out', 'l0_norm_mlp', 'l0_w_up', 'l0_w_down', 'l1_norm_mix', 'l1_w_in', 'gdn_conv_w', 'gdn_A_log', 'gdn_dt_bias', 'gdn_norm_w', 's5_A_re', 's5_A_im', 's5_log_step', 's5_B_re', 's5_B_im', 's5_C_re', 's5_C_im', 's5_D', 's5_w_glu', 's5_b_glu', 'l1_w_out', 'l1_norm_mlp', 'l1_w_up', 'l1_w_down', 'final_norm']
TWIN_WEIGHTS = ['l0_norm_mix', 'l0_w_in', 'ssd_conv_w', 'ssd_conv_b', 'ssd_dt_bias', 'ssd_A_log', 'ssd_D', 'ssd_norm_w', 'l0_w_out', 'l0_norm_mlp', 'l0_w_up', 'l0_w_down', 'l1_norm_mix', 'l1_w_in', 'gdn_conv_w', 'gdn_A_log', 'gdn_dt_bias', 'gdn_norm_w', 's5_A_re', 's5_A_im', 's5_log_step', 's5_B_re', 's5_B_im', 's5_C_re', 's5_C_im', 's5_D', 's5_w_glu', 's5_b_glu', 'l1_w_out', 'l1_norm_mlp', 'l1_w_up', 'l1_w_down', 'final_norm']
TWIN_DIFF_INPUT = 'x'
TWIN_INPUTS = ['x', 'l0_norm_mix', 'l0_w_in', 'ssd_conv_w', 'ssd_conv_b', 'ssd_dt_bias', 'ssd_A_log', 'ssd_D', 'ssd_norm_w', 'l0_w_out', 'l0_norm_mlp', 'l0_w_up', 'l0_w_down', 'l1_norm_mix', 'l1_w_in', 'gdn_conv_w', 'gdn_A_log', 'gdn_dt_bias', 'gdn_norm_w', 's5_A_re', 's5_A_im', 's5_log_step', 's5_B_re', 's5_B_im', 's5_C_re', 's5_C_im', 's5_D', 's5_w_glu', 's5_b_glu', 'l1_w_out', 'l1_norm_mlp', 'l1_w_up', 'l1_w_down', 'final_norm', 'loss_target', 'm_l0_norm_mix', 'm_l0_w_in', 'm_ssd_conv_w', 'm_ssd_conv_b', 'm_ssd_dt_bias', 'm_ssd_A_log', 'm_ssd_D', 'm_ssd_norm_w', 'm_l0_w_out', 'm_l0_norm_mlp', 'm_l0_w_up', 'm_l0_w_down', 'm_l1_norm_mix', 'm_l1_w_in', 'm_gdn_conv_w', 'm_gdn_A_log', 'm_gdn_dt_bias', 'm_gdn_norm_w', 'm_s5_A_re', 'm_s5_A_im', 'm_s5_log_step', 'm_s5_B_re', 'm_s5_B_im', 'm_s5_C_re', 'm_s5_C_im', 'm_s5_D', 'm_s5_w_glu', 'm_s5_b_glu', 'm_l1_w_out', 'm_l1_norm_mlp', 'm_l1_w_up', 'm_l1_w_down', 'm_final_norm', 'v_l0_norm_mix', 'v_l0_w_in', 'v_ssd_conv_w', 'v_ssd_conv_b', 'v_ssd_dt_bias', 'v_ssd_A_log', 'v_ssd_D', 'v_ssd_norm_w', 'v_l0_w_out', 'v_l0_norm_mlp', 'v_l0_w_up', 'v_l0_w_down', 'v_l1_norm_mix', 'v_l1_w_in', 'v_gdn_conv_w', 'v_gdn_A_log', 'v_gdn_dt_bias', 'v_gdn_norm_w', 'v_s5_A_re', 'v_s5_A_im', 'v_s5_log_step', 'v_s5_B_re', 'v_s5_B_im', 'v_s5_C_re', 'v_s5_C_im', 'v_s5_D', 'v_s5_w_glu', 'v_s5_b_glu', 'v_l1_w_out', 'v_l1_norm_mlp', 'v_l1_w_up', 'v_l1_w_down', 'v_final_norm']
TWIN_OUTPUTS = ['loss', 'grad_x', 'grad_l0_norm_mix', 'grad_l0_w_in', 'grad_ssd_conv_w', 'grad_ssd_conv_b', 'grad_ssd_dt_bias', 'grad_ssd_A_log', 'grad_ssd_D', 'grad_ssd_norm_w', 'grad_l0_w_out', 'grad_l0_norm_mlp', 'grad_l0_w_up', 'grad_l0_w_down', 'grad_l1_norm_mix', 'grad_l1_w_in', 'grad_gdn_conv_w', 'grad_gdn_A_log', 'grad_gdn_dt_bias', 'grad_gdn_norm_w', 'grad_s5_A_re', 'grad_s5_A_im', 'grad_s5_log_step', 'grad_s5_B_re', 'grad_s5_B_im', 'grad_s5_C_re', 'grad_s5_C_im', 'grad_s5_D', 'grad_s5_w_glu', 'grad_s5_b_glu', 'grad_l1_w_out', 'grad_l1_norm_mlp', 'grad_l1_w_up', 'grad_l1_w_down', 'grad_final_norm', 'delta_l0_norm_mix', 'delta_l0_w_in', 'delta_ssd_conv_w', 'delta_ssd_conv_b', 'delta_ssd_dt_bias', 'delta_ssd_A_log', 'delta_ssd_D', 'delta_ssd_norm_w', 'delta_l0_w_out', 'delta_l0_norm_mlp', 'delta_l0_w_up', 'delta_l0_w_down', 'delta_l1_norm_mix', 'delta_l1_w_in', 'delta_gdn_conv_w', 'delta_gdn_A_log', 'delta_gdn_dt_bias', 'delta_gdn_norm_w', 'delta_s5_A_re', 'delta_s5_A_im', 'delta_s5_log_step', 'delta_s5_B_re', 'delta_s5_B_im', 'delta_s5_C_re', 'delta_s5_C_im', 'delta_s5_D', 'delta_s5_w_glu', 'delta_s5_b_glu', 'delta_l1_w_out', 'delta_l1_norm_mlp', 'delta_l1_w_up', 'delta_l1_w_down', 'delta_final_norm', 'new_m_l0_norm_mix', 'new_m_l0_w_in', 'new_m_ssd_conv_w', 'new_m_ssd_conv_b', 'new_m_ssd_dt_bias', 'new_m_ssd_A_log', 'new_m_ssd_D', 'new_m_ssd_norm_w', 'new_m_l0_w_out', 'new_m_l0_norm_mlp', 'new_m_l0_w_up', 'new_m_l0_w_down', 'new_m_l1_norm_mix', 'new_m_l1_w_in', 'new_m_gdn_conv_w', 'new_m_gdn_A_log', 'new_m_gdn_dt_bias', 'new_m_gdn_norm_w', 'new_m_s5_A_re', 'new_m_s5_A_im', 'new_m_s5_log_step', 'new_m_s5_B_re', 'new_m_s5_B_im', 'new_m_s5_C_re', 'new_m_s5_C_im', 'new_m_s5_D', 'new_m_s5_w_glu', 'new_m_s5_b_glu', 'new_m_l1_w_out', 'new_m_l1_norm_mlp', 'new_m_l1_w_up', 'new_m_l1_w_down', 'new_m_final_norm', 'new_v_l0_norm_mix', 'new_v_l0_w_in', 'new_v_ssd_conv_w', 'new_v_ssd_conv_b', 'new_v_ssd_dt_bias', 'new_v_ssd_A_log', 'new_v_ssd_D', 'new_v_ssd_norm_w', 'new_v_l0_w_out', 'new_v_l0_norm_mlp', 'new_v_l0_w_up', 'new_v_l0_w_down', 'new_v_l1_norm_mix', 'new_v_l1_w_in', 'new_v_gdn_conv_w', 'new_v_gdn_A_log', 'new_v_gdn_dt_bias', 'new_v_gdn_norm_w', 'new_v_s5_A_re', 'new_v_s5_A_im', 'new_v_s5_log_step', 'new_v_s5_B_re', 'new_v_s5_B_im', 'new_v_s5_C_re', 'new_v_s5_C_im', 'new_v_s5_D', 'new_v_s5_w_glu', 'new_v_s5_b_glu', 'new_v_l1_w_out', 'new_v_l1_norm_mlp', 'new_v_l1_w_up', 'new_v_l1_w_down', 'new_v_final_norm']
TWIN_LEAF_KINDS = {'loss': 'loss', 'grad_x': 'grad_x', 'grad_l0_norm_mix': 'grad_w', 'grad_l0_w_in': 'grad_w', 'grad_ssd_conv_w': 'grad_w', 'grad_ssd_conv_b': 'grad_w', 'grad_ssd_dt_bias': 'grad_w', 'grad_ssd_A_log': 'grad_w', 'grad_ssd_D': 'grad_w', 'grad_ssd_norm_w': 'grad_w', 'grad_l0_w_out': 'grad_w', 'grad_l0_norm_mlp': 'grad_w', 'grad_l0_w_up': 'grad_w', 'grad_l0_w_down': 'grad_w', 'grad_l1_norm_mix': 'grad_w', 'grad_l1_w_in': 'grad_w', 'grad_gdn_conv_w': 'grad_w', 'grad_gdn_A_log': 'grad_w', 'grad_gdn_dt_bias': 'grad_w', 'grad_gdn_norm_w': 'grad_w', 'grad_s5_A_re': 'grad_w', 'grad_s5_A_im': 'grad_w', 'grad_s5_log_step': 'grad_w', 'grad_s5_B_re': 'grad_w', 'grad_s5_B_im': 'grad_w', 'grad_s5_C_re': 'grad_w', 'grad_s5_C_im': 'grad_w', 'grad_s5_D': 'grad_w', 'grad_s5_w_glu': 'grad_w', 'grad_s5_b_glu': 'grad_w', 'grad_l1_w_out': 'grad_w', 'grad_l1_norm_mlp': 'grad_w', 'grad_l1_w_up': 'grad_w', 'grad_l1_w_down': 'grad_w', 'grad_final_norm': 'grad_w', 'delta_l0_norm_mix': 'delta_w', 'delta_l0_w_in': 'delta_w', 'delta_ssd_conv_w': 'delta_w', 'delta_ssd_conv_b': 'delta_w', 'delta_ssd_dt_bias': 'delta_w', 'delta_ssd_A_log': 'delta_w', 'delta_ssd_D': 'delta_w', 'delta_ssd_norm_w': 'delta_w', 'delta_l0_w_out': 'delta_w', 'delta_l0_norm_mlp': 'delta_w', 'delta_l0_w_up': 'delta_w', 'delta_l0_w_down': 'delta_w', 'delta_l1_norm_mix': 'delta_w', 'delta_l1_w_in': 'delta_w', 'delta_gdn_conv_w': 'delta_w', 'delta_gdn_A_log': 'delta_w', 'delta_gdn_dt_bias': 'delta_w', 'delta_gdn_norm_w': 'delta_w', 'delta_s5_A_re': 'delta_w', 'delta_s5_A_im': 'delta_w', 'delta_s5_log_step': 'delta_w', 'delta_s5_B_re': 'delta_w', 'delta_s5_B_im': 'delta_w', 'delta_s5_C_re': 'delta_w', 'delta_s5_C_im': 'delta_w', 'delta_s5_D': 'delta_w', 'delta_s5_w_glu': 'delta_w', 'delta_s5_b_glu': 'delta_w', 'delta_l1_w_out': 'delta_w', 'delta_l1_norm_mlp': 'delta_w', 'delta_l1_w_up': 'delta_w', 'delta_l1_w_down': 'delta_w', 'delta_final_norm': 'delta_w', 'new_m_l0_norm_mix': 'new_m', 'new_m_l0_w_in': 'new_m', 'new_m_ssd_conv_w': 'new_m', 'new_m_ssd_conv_b': 'new_m', 'new_m_ssd_dt_bias': 'new_m', 'new_m_ssd_A_log': 'new_m', 'new_m_ssd_D': 'new_m', 'new_m_ssd_norm_w': 'new_m', 'new_m_l0_w_out': 'new_m', 'new_m_l0_norm_mlp': 'new_m', 'new_m_l0_w_up': 'new_m', 'new_m_l0_w_down': 'new_m', 'new_m_l1_norm_mix': 'new_m', 'new_m_l1_w_in': 'new_m', 'new_m_gdn_conv_w': 'new_m', 'new_m_gdn_A_log': 'new_m', 'new_m_gdn_dt_bias': 'new_m', 'new_m_gdn_norm_w': 'new_m', 'new_m_s5_A_re': 'new_m', 'new_m_s5_A_im': 'new_m', 'new_m_s5_log_step': 'new_m', 'new_m_s5_B_re': 'new_m', 'new_m_s5_B_im': 'new_m', 'new_m_s5_C_re': 'new_m', 'new_m_s5_C_im': 'new_m', 'new_m_s5_D': 'new_m', 'new_m_s5_w_glu': 'new_m', 'new_m_s5_b_glu': 'new_m', 'new_m_l1_w_out': 'new_m', 'new_m_l1_norm_mlp': 'new_m', 'new_m_l1_w_up': 'new_m', 'new_m_l1_w_down': 'new_m', 'new_m_final_norm': 'new_m', 'new_v_l0_norm_mix': 'new_v', 'new_v_l0_w_in': 'new_v', 'new_v_ssd_conv_w': 'new_v', 'new_v_ssd_conv_b': 'new_v', 'new_v_ssd_dt_bias': 'new_v', 'new_v_ssd_A_log': 'new_v', 'new_v_ssd_D': 'new_v', 'new_v_ssd_norm_w': 'new_v', 'new_v_l0_w_out': 'new_v', 'new_v_l0_norm_mlp': 'new_v', 'new_v_l0_w_up': 'new_v', 'new_v_l0_w_down': 'new_v', 'new_v_l1_norm_mix': 'new_v', 'new_v_l1_w_in': 'new_v', 'new_v_gdn_conv_w': 'new_v', 'new_v_gdn_A_log': 'new_v', 'new_v_gdn_dt_bias': 'new_v', 'new_v_gdn_norm_w': 'new_v', 'new_v_s5_A_re': 'new_v', 'new_v_s5_A_im': 'new_v', 'new_v_s5_log_step': 'new_v', 'new_v_s5_B_re': 'new_v', 'new_v_s5_B_im': 'new_v', 'new_v_s5_C_re': 'new_v', 'new_v_s5_C_im': 'new_v', 'new_v_s5_D': 'new_v', 'new_v_s5_w_glu': 'new_v', 'new_v_s5_b_glu': 'new_v', 'new_v_l1_w_out': 'new_v', 'new_v_l1_norm_mlp': 'new_v', 'new_v_l1_w_up': 'new_v', 'new_v_l1_w_down': 'new_v', 'new_v_final_norm': 'new_v'}


def _forward(args):
    return _fwd_reference(*[args[k] for k in FWD_PARAMS])


def _output_shape():
    def fwd():
        inp = _fwd_setup_inputs(0)
        return _fwd_reference(*[inp[k] for k in FWD_PARAMS])
    out = _jax.eval_shape(fwd)
    return out.shape, out.dtype

N_MICROBATCH = 1
ADAM_LR = 0.001
ADAM_B1 = 0.9
ADAM_B2 = 0.999
ADAM_EPS = 1e-08
ADAM_WD = 0.01
ADAM_STEP = 10
PER_EXAMPLE_BATCH_AXIS = {'x': 0, 'loss_target': 0}
SHARED_INPUTS = []
_WEIGHT_DTYPES = {'l0_norm_mix': _jnp.float32, 'l0_w_in': _jnp.float32, 'ssd_conv_w': _jnp.float32, 'ssd_conv_b': _jnp.float32, 'ssd_dt_bias': _jnp.float32, 'ssd_A_log': _jnp.float32, 'ssd_D': _jnp.float32, 'ssd_norm_w': _jnp.float32, 'l0_w_out': _jnp.float32, 'l0_norm_mlp': _jnp.float32, 'l0_w_up': _jnp.float32, 'l0_w_down': _jnp.float32, 'l1_norm_mix': _jnp.float32, 'l1_w_in': _jnp.float32, 'gdn_conv_w': _jnp.float32, 'gdn_A_log': _jnp.float32, 'gdn_dt_bias': _jnp.float32, 'gdn_norm_w': _jnp.float32, 's5_A_re': _jnp.float32, 's5_A_im': _jnp.float32, 's5_log_step': _jnp.float32, 's5_B_re': _jnp.float32, 's5_B_im': _jnp.float32, 's5_C_re': _jnp.float32, 's5_C_im': _jnp.float32, 's5_D': _jnp.float32, 's5_w_glu': _jnp.float32, 's5_b_glu': _jnp.float32, 'l1_w_out': _jnp.float32, 'l1_norm_mlp': _jnp.float32, 'l1_w_up': _jnp.float32, 'l1_w_down': _jnp.float32, 'final_norm': _jnp.float32}
MOMENT_SCALE = {'l0_norm_mix': 4.845913e-01, 'l0_w_in': 2.452022e-01, 'ssd_conv_w': 2.564627e-01, 'ssd_conv_b': 4.207457e-01, 'ssd_dt_bias': 2.469968e+00, 'ssd_A_log': 7.685877e-01, 'ssd_D': 1.079766e+00, 'ssd_norm_w': 3.217485e-01, 'l0_w_out': 2.782924e-01, 'l0_norm_mlp': 3.313411e-01, 'l0_w_up': 1.593261e-01, 'l0_w_down': 3.466267e-01, 'l1_norm_mix': 2.231726e-01, 'l1_w_in': 1.013561e-01, 'gdn_conv_w': 9.800795e-02, 'gdn_A_log': 7.871709e-01, 'gdn_dt_bias': 8.030769e-01, 'gdn_norm_w': 4.344844e-01, 's5_A_re': 1.277830e-02, 's5_A_im': 1.883898e-02, 's5_log_step': 2.134827e+00, 's5_B_re': 6.541581e-03, 's5_B_im': 3.842060e-03, 's5_C_re': 7.540398e-03, 's5_C_im': 8.004746e-03, 's5_D': 7.854661e-02, 's5_w_glu': 2.260597e-02, 's5_b_glu': 4.107941e-02, 'l1_w_out': 1.252758e-01, 'l1_norm_mlp': 2.502434e-01, 'l1_w_up': 1.203889e-01, 'l1_w_down': 2.761400e-01, 'final_norm': 1.296750e+02}


def _to_microbatches(a, axis):
    t = _jnp.moveaxis(a, axis, 0)
    t = t.reshape((N_MICROBATCH, t.shape[0] // N_MICROBATCH) + t.shape[1:])
    return _jnp.moveaxis(t, 1, axis + 1)


def setup_inputs(seed: int = 0) -> dict:
    inp = _fwd_setup_inputs(seed)
    key = _jax.random.fold_in(_jax.random.key(seed), 7919)
    shape, _ = _output_shape()
    out = dict(inp)
    out["loss_target"] = _jax.random.normal(_jax.random.fold_in(key, 0), shape, _jnp.float32)
    for i, name in enumerate(TWIN_WEIGHTS):
        w = inp[name].astype(_jnp.float32)
        if MOMENT_SCALE is None:
            s = _jnp.sqrt(_jnp.mean(_jnp.square(w)) + 1e-30)
        else:
            s = MOMENT_SCALE[name]
        km, kv = _jax.random.split(_jax.random.fold_in(key, i + 1))
        out[name] = w
        out["m_" + name] = s * _jax.random.normal(km, w.shape, _jnp.float32)
        out["v_" + name] = (s * s) * _jax.random.uniform(kv, w.shape, _jnp.float32, 0.5, 1.5)
    if N_MICROBATCH > 1:
        for name, axis in PER_EXAMPLE_BATCH_AXIS.items():
            out[name] = _to_microbatches(out[name], axis)
    return {'x': out['x'], 'l0_norm_mix': out['l0_norm_mix'], 'l0_w_in': out['l0_w_in'], 'ssd_conv_w': out['ssd_conv_w'], 'ssd_conv_b': out['ssd_conv_b'], 'ssd_dt_bias': out['ssd_dt_bias'], 'ssd_A_log': out['ssd_A_log'], 'ssd_D': out['ssd_D'], 'ssd_norm_w': out['ssd_norm_w'], 'l0_w_out': out['l0_w_out'], 'l0_norm_mlp': out['l0_norm_mlp'], 'l0_w_up': out['l0_w_up'], 'l0_w_down': out['l0_w_down'], 'l1_norm_mix': out['l1_norm_mix'], 'l1_w_in': out['l1_w_in'], 'gdn_conv_w': out['gdn_conv_w'], 'gdn_A_log': out['gdn_A_log'], 'gdn_dt_bias': out['gdn_dt_bias'], 'gdn_norm_w': out['gdn_norm_w'], 's5_A_re': out['s5_A_re'], 's5_A_im': out['s5_A_im'], 's5_log_step': out['s5_log_step'], 's5_B_re': out['s5_B_re'], 's5_B_im': out['s5_B_im'], 's5_C_re': out['s5_C_re'], 's5_C_im': out['s5_C_im'], 's5_D': out['s5_D'], 's5_w_glu': out['s5_w_glu'], 's5_b_glu': out['s5_b_glu'], 'l1_w_out': out['l1_w_out'], 'l1_norm_mlp': out['l1_norm_mlp'], 'l1_w_up': out['l1_w_up'], 'l1_w_down': out['l1_w_down'], 'final_norm': out['final_norm'], 'loss_target': out['loss_target'], 'm_l0_norm_mix': out['m_l0_norm_mix'], 'm_l0_w_in': out['m_l0_w_in'], 'm_ssd_conv_w': out['m_ssd_conv_w'], 'm_ssd_conv_b': out['m_ssd_conv_b'], 'm_ssd_dt_bias': out['m_ssd_dt_bias'], 'm_ssd_A_log': out['m_ssd_A_log'], 'm_ssd_D': out['m_ssd_D'], 'm_ssd_norm_w': out['m_ssd_norm_w'], 'm_l0_w_out': out['m_l0_w_out'], 'm_l0_norm_mlp': out['m_l0_norm_mlp'], 'm_l0_w_up': out['m_l0_w_up'], 'm_l0_w_down': out['m_l0_w_down'], 'm_l1_norm_mix': out['m_l1_norm_mix'], 'm_l1_w_in': out['m_l1_w_in'], 'm_gdn_conv_w': out['m_gdn_conv_w'], 'm_gdn_A_log': out['m_gdn_A_log'], 'm_gdn_dt_bias': out['m_gdn_dt_bias'], 'm_gdn_norm_w': out['m_gdn_norm_w'], 'm_s5_A_re': out['m_s5_A_re'], 'm_s5_A_im': out['m_s5_A_im'], 'm_s5_log_step': out['m_s5_log_step'], 'm_s5_B_re': out['m_s5_B_re'], 'm_s5_B_im': out['m_s5_B_im'], 'm_s5_C_re': out['m_s5_C_re'], 'm_s5_C_im': out['m_s5_C_im'], 'm_s5_D': out['m_s5_D'], 'm_s5_w_glu': out['m_s5_w_glu'], 'm_s5_b_glu': out['m_s5_b_glu'], 'm_l1_w_out': out['m_l1_w_out'], 'm_l1_norm_mlp': out['m_l1_norm_mlp'], 'm_l1_w_up': out['m_l1_w_up'], 'm_l1_w_down': out['m_l1_w_down'], 'm_final_norm': out['m_final_norm'], 'v_l0_norm_mix': out['v_l0_norm_mix'], 'v_l0_w_in': out['v_l0_w_in'], 'v_ssd_conv_w': out['v_ssd_conv_w'], 'v_ssd_conv_b': out['v_ssd_conv_b'], 'v_ssd_dt_bias': out['v_ssd_dt_bias'], 'v_ssd_A_log': out['v_ssd_A_log'], 'v_ssd_D': out['v_ssd_D'], 'v_ssd_norm_w': out['v_ssd_norm_w'], 'v_l0_w_out': out['v_l0_w_out'], 'v_l0_norm_mlp': out['v_l0_norm_mlp'], 'v_l0_w_up': out['v_l0_w_up'], 'v_l0_w_down': out['v_l0_w_down'], 'v_l1_norm_mix': out['v_l1_norm_mix'], 'v_l1_w_in': out['v_l1_w_in'], 'v_gdn_conv_w': out['v_gdn_conv_w'], 'v_gdn_A_log': out['v_gdn_A_log'], 'v_gdn_dt_bias': out['v_gdn_dt_bias'], 'v_gdn_norm_w': out['v_gdn_norm_w'], 'v_s5_A_re': out['v_s5_A_re'], 'v_s5_A_im': out['v_s5_A_im'], 'v_s5_log_step': out['v_s5_log_step'], 'v_s5_B_re': out['v_s5_B_re'], 'v_s5_B_im': out['v_s5_B_im'], 'v_s5_C_re': out['v_s5_C_re'], 'v_s5_C_im': out['v_s5_C_im'], 'v_s5_D': out['v_s5_D'], 'v_s5_w_glu': out['v_s5_w_glu'], 'v_s5_b_glu': out['v_s5_b_glu'], 'v_l1_w_out': out['v_l1_w_out'], 'v_l1_norm_mlp': out['v_l1_norm_mlp'], 'v_l1_w_up': out['v_l1_w_up'], 'v_l1_w_down': out['v_l1_w_down'], 'v_final_norm': out['v_final_norm']}


def _loss(weights, diff, rest, loss_target):
    with _jax.named_scope("forward"):
        args = {**rest, TWIN_DIFF_INPUT: diff, **{k: w.astype(_WEIGHT_DTYPES[k]) for k, w in weights.items()}}
        y = _forward(args)
    with _jax.named_scope("loss_head"):
        err = _jnp.square(y.astype(_jnp.float32) - loss_target)
        return 0.5 * _jnp.sum(_jnp.mean(err, axis=-1)) if err.ndim else 0.5 * err


def _adamw(w, g, m, v):
    m = ADAM_B1 * m + (1.0 - ADAM_B1) * g
    v = ADAM_B2 * v + (1.0 - ADAM_B2) * _jnp.square(g)
    m_hat = m / (1.0 - ADAM_B1 ** ADAM_STEP)
    v_hat = v / (1.0 - ADAM_B2 ** ADAM_STEP)
    delta = -ADAM_LR * (m_hat / (_jnp.sqrt(v_hat) + ADAM_EPS) + ADAM_WD * w)
    return delta, m, v


def reference(x, l0_norm_mix, l0_w_in, ssd_conv_w, ssd_conv_b, ssd_dt_bias, ssd_A_log, ssd_D, ssd_norm_w, l0_w_out, l0_norm_mlp, l0_w_up, l0_w_down, l1_norm_mix, l1_w_in, gdn_conv_w, gdn_A_log, gdn_dt_bias, gdn_norm_w, s5_A_re, s5_A_im, s5_log_step, s5_B_re, s5_B_im, s5_C_re, s5_C_im, s5_D, s5_w_glu, s5_b_glu, l1_w_out, l1_norm_mlp, l1_w_up, l1_w_down, final_norm, loss_target, m_l0_norm_mix, m_l0_w_in, m_ssd_conv_w, m_ssd_conv_b, m_ssd_dt_bias, m_ssd_A_log, m_ssd_D, m_ssd_norm_w, m_l0_w_out, m_l0_norm_mlp, m_l0_w_up, m_l0_w_down, m_l1_norm_mix, m_l1_w_in, m_gdn_conv_w, m_gdn_A_log, m_gdn_dt_bias, m_gdn_norm_w, m_s5_A_re, m_s5_A_im, m_s5_log_step, m_s5_B_re, m_s5_B_im, m_s5_C_re, m_s5_C_im, m_s5_D, m_s5_w_glu, m_s5_b_glu, m_l1_w_out, m_l1_norm_mlp, m_l1_w_up, m_l1_w_down, m_final_norm, v_l0_norm_mix, v_l0_w_in, v_ssd_conv_w, v_ssd_conv_b, v_ssd_dt_bias, v_ssd_A_log, v_ssd_D, v_ssd_norm_w, v_l0_w_out, v_l0_norm_mlp, v_l0_w_up, v_l0_w_down, v_l1_norm_mix, v_l1_w_in, v_gdn_conv_w, v_gdn_A_log, v_gdn_dt_bias, v_gdn_norm_w, v_s5_A_re, v_s5_A_im, v_s5_log_step, v_s5_B_re, v_s5_B_im, v_s5_C_re, v_s5_C_im, v_s5_D, v_s5_w_glu, v_s5_b_glu, v_l1_w_out, v_l1_norm_mlp, v_l1_w_up, v_l1_w_down, v_final_norm):
    given = dict(x=x, l0_norm_mix=l0_norm_mix, l0_w_in=l0_w_in, ssd_conv_w=ssd_conv_w, ssd_conv_b=ssd_conv_b, ssd_dt_bias=ssd_dt_bias, ssd_A_log=ssd_A_log, ssd_D=ssd_D, ssd_norm_w=ssd_norm_w, l0_w_out=l0_w_out, l0_norm_mlp=l0_norm_mlp, l0_w_up=l0_w_up, l0_w_down=l0_w_down, l1_norm_mix=l1_norm_mix, l1_w_in=l1_w_in, gdn_conv_w=gdn_conv_w, gdn_A_log=gdn_A_log, gdn_dt_bias=gdn_dt_bias, gdn_norm_w=gdn_norm_w, s5_A_re=s5_A_re, s5_A_im=s5_A_im, s5_log_step=s5_log_step, s5_B_re=s5_B_re, s5_B_im=s5_B_im, s5_C_re=s5_C_re, s5_C_im=s5_C_im, s5_D=s5_D, s5_w_glu=s5_w_glu, s5_b_glu=s5_b_glu, l1_w_out=l1_w_out, l1_norm_mlp=l1_norm_mlp, l1_w_up=l1_w_up, l1_w_down=l1_w_down, final_norm=final_norm, loss_target=loss_target, m_l0_norm_mix=m_l0_norm_mix, m_l0_w_in=m_l0_w_in, m_ssd_conv_w=m_ssd_conv_w, m_ssd_conv_b=m_ssd_conv_b, m_ssd_dt_bias=m_ssd_dt_bias, m_ssd_A_log=m_ssd_A_log, m_ssd_D=m_ssd_D, m_ssd_norm_w=m_ssd_norm_w, m_l0_w_out=m_l0_w_out, m_l0_norm_mlp=m_l0_norm_mlp, m_l0_w_up=m_l0_w_up, m_l0_w_down=m_l0_w_down, m_l1_norm_mix=m_l1_norm_mix, m_l1_w_in=m_l1_w_in, m_gdn_conv_w=m_gdn_conv_w, m_gdn_A_log=m_gdn_A_log, m_gdn_dt_bias=m_gdn_dt_bias, m_gdn_norm_w=m_gdn_norm_w, m_s5_A_re=m_s5_A_re, m_s5_A_im=m_s5_A_im, m_s5_log_step=m_s5_log_step, m_s5_B_re=m_s5_B_re, m_s5_B_im=m_s5_B_im, m_s5_C_re=m_s5_C_re, m_s5_C_im=m_s5_C_im, m_s5_D=m_s5_D, m_s5_w_glu=m_s5_w_glu, m_s5_b_glu=m_s5_b_glu, m_l1_w_out=m_l1_w_out, m_l1_norm_mlp=m_l1_norm_mlp, m_l1_w_up=m_l1_w_up, m_l1_w_down=m_l1_w_down, m_final_norm=m_final_norm, v_l0_norm_mix=v_l0_norm_mix, v_l0_w_in=v_l0_w_in, v_ssd_conv_w=v_ssd_conv_w, v_ssd_conv_b=v_ssd_conv_b, v_ssd_dt_bias=v_ssd_dt_bias, v_ssd_A_log=v_ssd_A_log, v_ssd_D=v_ssd_D, v_ssd_norm_w=v_ssd_norm_w, v_l0_w_out=v_l0_w_out, v_l0_norm_mlp=v_l0_norm_mlp, v_l0_w_up=v_l0_w_up, v_l0_w_down=v_l0_w_down, v_l1_norm_mix=v_l1_norm_mix, v_l1_w_in=v_l1_w_in, v_gdn_conv_w=v_gdn_conv_w, v_gdn_A_log=v_gdn_A_log, v_gdn_dt_bias=v_gdn_dt_bias, v_gdn_norm_w=v_gdn_norm_w, v_s5_A_re=v_s5_A_re, v_s5_A_im=v_s5_A_im, v_s5_log_step=v_s5_log_step, v_s5_B_re=v_s5_B_re, v_s5_B_im=v_s5_B_im, v_s5_C_re=v_s5_C_re, v_s5_C_im=v_s5_C_im, v_s5_D=v_s5_D, v_s5_w_glu=v_s5_w_glu, v_s5_b_glu=v_s5_b_glu, v_l1_w_out=v_l1_w_out, v_l1_norm_mlp=v_l1_norm_mlp, v_l1_w_up=v_l1_w_up, v_l1_w_down=v_l1_w_down, v_final_norm=v_final_norm)
    weights = {n: given[n] for n in TWIN_WEIGHTS}
    shared = {n: given[n] for n in SHARED_INPUTS}
    per_example = {n: given[n] for n in ['x']}
    grad_fn = _jax.value_and_grad(_loss, argnums=(0, 1))

    def one_microbatch(ex, loss_target):
        ex = dict(ex)
        diff = ex.pop(TWIN_DIFF_INPUT)
        return grad_fn(weights, diff, {**shared, **ex}, loss_target)

    if N_MICROBATCH == 1:
        loss, (grad_w, grad_x) = one_microbatch(per_example, given["loss_target"])
    else:
        def body(carry, xs):
            loss_sum, grad_sum = carry
            l_k, (gw_k, gx_k) = one_microbatch(xs[0], xs[1])
            with _jax.named_scope("update"):
                return (loss_sum + l_k, _jax.tree.map(_jnp.add, grad_sum, gw_k)), gx_k

        init = (_jnp.zeros((), _jnp.float32), _jax.tree.map(_jnp.zeros_like, weights))
        (loss, grad_w), grad_x = _jax.lax.scan(body, init, (per_example, given["loss_target"]))
    with _jax.named_scope("update"):
        delta_w, new_m, new_v = {}, {}, {}
        for n in TWIN_WEIGHTS:
            delta_w[n], new_m[n], new_v[n] = _adamw(weights[n], grad_w[n], given["m_" + n], given["v_" + n])
    return (loss, grad_x, *[grad_w[n] for n in TWIN_WEIGHTS], *[delta_w[n] for n in TWIN_WEIGHTS],
            *[new_m[n] for n in TWIN_WEIGHTS], *[new_v[n] for n in TWIN_WEIGHTS])
```

```python
import functools
import math

import jax
import jax.numpy as jnp
from jax import lax
from jax.experimental import pallas as pl
from jax.experimental.pallas import tpu as pltpu

F32 = jnp.float32
_MXU_DTYPE = jnp.bfloat16
HI = lax.Precision.HIGHEST

D_MODEL = 1024
CHUNK = 64
EPS = 1e-6
RET_HEADS, RET_DK = 4, 128
ROPE_THETA = 10000.0
SSD_HEADS, SSD_HEAD_DIM = 8, 64
GDN_HEADS, GDN_DK = 6, 128
S5_GROUPS, S5_GROUP, S5_STATE = 16, 16, 64
ADAM_LR, ADAM_B1, ADAM_B2, ADAM_EPS, ADAM_WD, ADAM_STEP = 0.001, 0.9, 0.999, 1e-08, 0.01, 10

IN0_PAD = 4096
IN1_PAD = 3584
LANES = 1024
VMEM_LIMIT = 48 * 1024 * 1024
MESH = pl.DeviceIdType.MESH

PARAMS = (
    ("l0_norm_mix", "rep"), ("l0_w_in", "col"), ("ssd_conv_w", "col"), ("ssd_conv_b", "rep"),
    ("ssd_dt_bias", "rep"), ("ssd_A_log", "rep"), ("ssd_D", "rep"), ("ssd_norm_w", "rep"),
    ("l0_w_out", "row"), ("l0_norm_mlp", "rep"), ("l0_w_up", "col"), ("l0_w_down", "row"),
    ("l1_norm_mix", "rep"), ("l1_w_in", "col"), ("gdn_conv_w", "col"), ("gdn_A_log", "rep"),
    ("gdn_dt_bias", "rep"), ("gdn_norm_w", "rep"), ("s5_A_re", "rep"), ("s5_A_im", "rep"),
    ("s5_log_step", "rep"), ("s5_B_re", "rep"), ("s5_B_im", "rep"), ("s5_C_re", "rep"), ("s5_C_im", "rep"),
    ("s5_D", "rep"), ("s5_w_glu", "row"), ("s5_b_glu", "rep"), ("l1_w_out", "row"), ("l1_norm_mlp", "rep"),
    ("l1_w_up", "col"), ("l1_w_down", "row"), ("final_norm", "rep"),
)
GATHER_BF16 = ("l0_w_in", "l0_w_out", "l0_w_up", "l0_w_down", "l1_w_in", "l1_w_out", "l1_w_up", "l1_w_down", "s5_w_glu")
GATHER_F32 = ("ssd_conv_w", "gdn_conv_w")


def _dg(a, b, ca, cb, prec=None):
    return lax.dot_general(a, b, (((ca,), (cb,)), ((), ())), preferred_element_type=F32, precision=prec)


def _lo(a):
    return a.astype(_MXU_DTYPE)


@jax.custom_vjp
def _mm(a, b):
    return _dg(_lo(a), _lo(b), 1, 0)


def _mm_fwd(a, b):
    return _mm(a, b), (a, b)


def _mm_bwd(res, g):
    a, b = res
    return _dg(_lo(g), _lo(b), 1, 1), _dg(_lo(a), _lo(g), 0, 0)


_mm.defvjp(_mm_fwd, _mm_bwd)


@jax.custom_vjp
def _mm_nt(a, b):
    return _dg(_lo(a), _lo(b), 1, 1)


def _mm_nt_fwd(a, b):
    return _mm_nt(a, b), (a, b)


def _mm_nt_bwd(res, g):
    a, b = res
    return _dg(_lo(g), _lo(b), 1, 0), _dg(_lo(g), _lo(a), 0, 0)


_mm_nt.defvjp(_mm_nt_fwd, _mm_nt_bwd)


@jax.custom_vjp
def _mm_tn(a, b):
    return _dg(_lo(a), _lo(b), 0, 0)


def _mm_tn_fwd(a, b):
    return _mm_tn(a, b), (a, b)


def _mm_tn_bwd(res, g):
    a, b = res
    return _dg(_lo(b), _lo(g), 1, 1), _dg(_lo(a), _lo(g), 1, 0)


_mm_tn.defvjp(_mm_tn_fwd, _mm_tn_bwd)


@jax.custom_vjp
def _mmh(a, b):
    return _dg(a, b, 1, 0, HI)


def _mmh_fwd(a, b):
    return _mmh(a, b), (a, b)


def _mmh_bwd(res, g):
    a, b = res
    return _dg(g, b, 1, 1, HI), _dg(a, g, 0, 0, HI)


_mmh.defvjp(_mmh_fwd, _mmh_bwd)


@functools.lru_cache(maxsize=None)
def _shift(s, axis):
    @jax.custom_vjp
    def sh(x):
        return pltpu.roll(x, s, axis)

    def fwd(x):
        return sh(x), None

    def bwd(_, g):
        n = g.shape[axis]
        return (pltpu.roll(g, (n - s) % n, axis),)

    sh.defvjp(fwd, bwd)
    return sh


def _iota(shape, axis):
    return lax.broadcasted_iota(jnp.int32, shape, axis)


def _silu(x):
    return x * jax.nn.sigmoid(x)


def _unit_rms(x):
    return x * lax.rsqrt(jnp.mean(x * x, axis=-1, keepdims=True) + EPS)


def _l2norm(x):
    return x * lax.rsqrt(jnp.sum(x * x, axis=-1, keepdims=True) + EPS)


def _tri_masks(n):
    r, c = _iota((n, n), 0), _iota((n, n), 1)
    return r >= c, r > c


def _decay_matrix(lt, us, causal, g_col):
    seg = _mmh(lt, g_col * us)
    return jnp.where(causal, jnp.exp(jnp.where(causal, seg, 0.0)), 0.0)


def _conv(x, tail, w):
    rows, width = x.shape
    row = _iota((rows, width), 0)
    acc = x * w[3:4, :]
    pad = jnp.zeros((rows - 8, width), F32)
    for j in range(3):
        s = 3 - j
        prev = jnp.concatenate([_shift(s, 0)(tail), pad], axis=0)
        acc = acc + w[j:j + 1, :] * jnp.where(row < s, prev, _shift(s, 0)(x))
    return acc


def _tri_inv_impl(a):
    n = a.shape[0]
    r, c = _iota((n, n), 0), _iota((n, n), 1)
    eye = (r == c).astype(F32)

    def same_block(b):
        return (r // b) == (c // b)

    a8 = jnp.where(same_block(8), a, 0.0)
    a2 = _dg(a8, a8, 1, 0, HI)
    a4 = _dg(a2, a2, 1, 0, HI)
    x = _dg(_dg(eye - a8, eye + a2, 1, 0, HI), eye + a4, 1, 0, HI)
    for b in (8, 16, 32):
        off = jnp.where(same_block(2 * b) & jnp.logical_not(same_block(b)), a, 0.0)
        x = x - _dg(_dg(x, off, 1, 0, HI), x, 1, 0, HI)
    return x


@jax.custom_vjp
def _tri_inv(a):
    return _tri_inv_impl(a)


def _tri_inv_fwd(a):
    t = _tri_inv_impl(a)
    return t, t


def _tri_inv_bwd(t, g):
    return (-_dg(_dg(t, g, 0, 0, HI), t, 1, 1, HI),)


_tri_inv.defvjp(_tri_inv_fwd, _tri_inv_bwd)


def _f_ret(tabs, consts, xs, xtabs, states):
    dmask, kdec, qdec, cdec = tabs
    q, k, v, gate = xs
    cs, sn = xtabs
    (st,) = states
    swap = _shift(RET_DK // 2, 1)
    outs, new = [], []
    for h in range(RET_HEADS):
        sl = slice(128 * h, 128 * h + 128)
        qh, kh, vh = q[:, sl], k[:, sl], v[:, sl]
        qh = (qh * cs + swap(qh) * sn) * (RET_DK ** -0.5)
        kh = kh * cs + swap(kh) * sn
        sh = st[sl, :]
        scores = _mm_nt(qh, kh) * dmask[64 * h:64 * h + 64, :]
        y = _mm(scores, vh) + _mm(qh * qdec[:, sl], sh)
        new.append(sh * cdec[:, sl] + _mm_tn(kh * kdec[:, sl], vh))
        outs.append(_silu(gate[:, sl]) * _unit_rms(y))
    return (jnp.concatenate(outs, axis=1),), [jnp.concatenate(new, axis=0)]


def _f_ssd(tabs, consts, xs, xtabs, states):
    (expand,) = tabs
    conv_w, conv_b, dtb, alog, dskip, nw = consts
    z, xr, br, cr, dtr = xs
    tx, tb, tc, st = states
    xc = _silu(_conv(xr, tx, conv_w[:, 0:512]) + conv_b[:, 0:512])
    bc = _silu(_conv(br, tb, conv_w[:, 512:768]) + conv_b[:, 512:768])
    cc = _silu(_conv(cr, tc, conv_w[:, 768:1024]) + conv_b[:, 768:1024])
    dt = jax.nn.softplus(_mmh(dtr, expand) + dtb)
    la = dt * (-jnp.exp(alog))
    causal, strict = _tri_masks(CHUNK)
    lt, us = causal.astype(F32), strict.astype(F32)
    lacum = _mmh(lt, la)
    total = jnp.sum(la, axis=0, keepdims=True)
    xd = xc * dt
    dte, ecum, cdec = jnp.exp(total - lacum), jnp.exp(lacum), jnp.exp(total)
    lane = _iota((CHUNK, 128), 1)
    ys, new = [], []
    cb = None
    for p in range(4):
        g = p // 2
        bg, cg = bc[:, 128 * g:128 * g + 128], cc[:, 128 * g:128 * g + 128]
        if p % 2 == 0:
            cb = _mm_nt(cg, bg)
        sl = slice(128 * p, 128 * p + 128)
        xdp, sp = xd[:, sl], st[sl, :]
        yp = _mm(cg, sp) * ecum[:, sl]
        for hh in range(2):
            h = 2 * p + hh
            lm = _decay_matrix(lt, us, causal, la[:, 64 * h:64 * h + 1])
            msk = (lane < 64) if hh == 0 else (lane >= 64)
            yp = yp + _mm(cb * lm, jnp.where(msk, xdp, 0.0))
        new.append(sp * cdec[:, sl] + _mm_tn(bg, xdp * dte[:, sl]))
        ys.append(yp)
    y = jnp.concatenate(ys, axis=1) + dskip * xc
    yg = y * _silu(z)
    out = jnp.concatenate([_unit_rms(yg[:, 0:256]), _unit_rms(yg[:, 256:512])], axis=1) * nw
    return (out,), [xr[CHUNK - 8:, :], br[CHUNK - 8:, :], cr[CHUNK - 8:, :], jnp.concatenate(new, axis=0)]


def _f_gdn(tabs, consts, xs, xtabs, states):
    conv_w, p_alog, p_dtb, nw = consts
    qr, kr, vr, z, ba = xs
    tq, tk, tv, st = states
    qc = _silu(_conv(qr, tq, conv_w[:, 0:768]))
    kc = _silu(_conv(kr, tk, conv_w[:, 768:1536]))
    vc = _silu(_conv(vr, tv, conv_w[:, 1536:2304]))
    gl = -jnp.exp(p_alog) * jax.nn.softplus(ba + p_dtb)
    bl = jax.nn.sigmoid(ba)
    causal, strict = _tri_masks(CHUNK)
    lt, us = causal.astype(F32), strict.astype(F32)
    gcum = _mmh(lt, gl)
    outs, new = [], []
    for h in range(GDN_HEADS):
        sl = slice(128 * h, 128 * h + 128)
        qh = _l2norm(qc[:, sl]) * (GDN_DK ** -0.5)
        kh = _l2norm(kc[:, sl])
        vh = vc[:, sl]
        g_h, b_h, gc = gl[:, 6 + h:7 + h], bl[:, h:h + 1], gcum[:, 6 + h:7 + h]
        tot = jnp.sum(g_h, axis=0, keepdims=True)
        dec = _decay_matrix(lt, us, causal, g_h)
        eg, et, cd = jnp.exp(gc), jnp.exp(tot - gc), jnp.exp(tot)
        kb, vb = kh * b_h, vh * b_h
        tm = _tri_inv(jnp.where(strict, _mm_nt(kb, kh) * dec, 0.0))
        u = _mm(tm, vb)
        w = _mm(tm, kb * eg)
        attn = _mm_nt(qh, kh) * dec
        sh = st[sl, :]
        vn = u - _mm(w, sh)
        o = _mm(qh * eg, sh) + _mm(attn, vn)
        new.append(sh * cd + _mm_tn(kh * et, vn))
        outs.append(_unit_rms(o) * nw * _silu(z[:, sl]))
    return (jnp.concatenate(outs, axis=1),), [qr[CHUNK - 8:, :], kr[CHUNK - 8:, :], vr[CHUNK - 8:, :],
                                             jnp.concatenate(new, axis=0)]


def _f_s5(tabs, consts, xs, xtabs, states):
    lam_re, lam_im, bblk, c_re, c_im, dskip, wglu, bglu = consts
    (u,) = xs
    s_re, s_im = states
    rows = u.shape[0]
    n = lam_re.shape[1]
    bu = _mm(u, bblk)
    hr, hi = bu[:, 0:n], bu[:, n:2 * n]
    row = _iota((rows, n), 0)
    h0r, h0i = s_re[0:1, :], s_im[0:1, :]
    hr = hr + jnp.where(row == 0, lam_re * h0r - lam_im * h0i, 0.0)
    hi = hi + jnp.where(row == 0, lam_re * h0i + lam_im * h0r, 0.0)
    pr, pi = lam_re, lam_im
    d = 1
    while d < rows:
        sr = jnp.where(row >= d, _shift(d, 0)(hr), 0.0)
        si = jnp.where(row >= d, _shift(d, 0)(hi), 0.0)
        hr, hi = hr + pr * sr - pi * si, hi + pr * si + pi * sr
        pr, pi = pr * pr - pi * pi, 2.0 * pr * pi
        d *= 2
    y = _mm(hr, c_re) - _mm(hi, c_im) + dskip * u
    y = jax.nn.gelu(y)
    out = y * jax.nn.sigmoid(_mm(y, wglu) + bglu)
    last_r = jnp.broadcast_to(hr[rows - 1:rows, :], (8, n))
    last_i = jnp.broadcast_to(hi[rows - 1:rows, :], (8, n))
    return (out,), [last_r, last_i]


def _full_spec(a):
    nd = a.ndim
    return pl.BlockSpec(a.shape, lambda i, _nd=nd: (0,) * _nd)


def _scan_fwd(name, f, rows, tabs, consts, xs, xtabs, state_shapes, y_total, y_width, y_cb, y_alias=None):
    seq = xs[0][0].shape[0]
    nc = seq // rows
    nt, ncst, nx, nxt, ns = len(tabs), len(consts), len(xs), len(xtabs), len(state_shapes)
    alias = y_alias is not None

    def body(*refs):
        p = 0
        tab_r = refs[p:p + nt]; p += nt
        c_r = refs[p:p + ncst]; p += ncst
        x_r = refs[p:p + nx]; p += nx
        xt_r = refs[p:p + nxt]; p += nxt
        if alias:
            p += 1
        y_ref = refs[p]; p += 1
        sv_r = refs[p:p + ns]; p += ns
        st_r = refs[p:p + ns]

        @pl.when(pl.program_id(0) == 0)
        def _():
            for s in st_r:
                s[...] = jnp.zeros(s.shape, F32)

        st = [s[...] for s in st_r]
        for r, v in zip(sv_r, st):
            r[...] = v
        (y,), new = f([r[...] for r in tab_r], [r[...] for r in c_r], [r[...] for r in x_r],
                      [r[...] for r in xt_r], st)
        y_ref[...] = y
        for s, v in zip(st_r, new):
            s[...] = v

    win = [pl.BlockSpec((rows, w), lambda i, _cb=cb: (i, _cb)) for (_, w, cb) in list(xs) + list(xtabs)]
    in_specs = [_full_spec(a) for a in list(tabs) + list(consts)] + win
    args = list(tabs) + list(consts) + [a for (a, _, _) in list(xs) + list(xtabs)]
    io_alias = {}
    if alias:
        in_specs.append(pl.BlockSpec(memory_space=pl.ANY))
        io_alias = {len(args): 0}
        args.append(y_alias)
    out_shape = [jax.ShapeDtypeStruct((seq, y_total), F32)]
    out_specs = [pl.BlockSpec((rows, y_width), lambda i: (i, y_cb))]
    for (r, c) in state_shapes:
        out_shape.append(jax.ShapeDtypeStruct((nc * r, c), F32))
        out_specs.append(pl.BlockSpec((r, c), lambda i: (i, 0)))
    res = pl.pallas_call(
        body, name=name, grid=(nc,), in_specs=in_specs, out_specs=out_specs, out_shape=out_shape,
        scratch_shapes=[pltpu.VMEM(s, F32) for s in state_shapes], input_output_aliases=io_alias,
        compiler_params=pltpu.CompilerParams(dimension_semantics=("arbitrary",), vmem_limit_bytes=VMEM_LIMIT),
    )(*args)
    return res[0], list(res[1:])


def _scan_bwd(name, f, rows, tabs, consts, xs, xtabs, saved, state_shapes, dy, dx_total, dx_width, dx_cb,
              assemble, dx_alias=None):
    seq = xs[0][0].shape[0]
    nc = seq // rows
    nt, ncst, nx, nxt, ns = len(tabs), len(consts), len(xs), len(xtabs), len(state_shapes)
    alias = dx_alias is not None

    def body(*refs):
        p = 0
        tab_r = refs[p:p + nt]; p += nt
        c_r = refs[p:p + ncst]; p += ncst
        x_r = refs[p:p + nx]; p += nx
        xt_r = refs[p:p + nxt]; p += nxt
        sv_r = refs[p:p + ns]; p += ns
        dy_ref = refs[p]; p += 1
        if alias:
            p += 1
        dx_ref = refs[p]; p += 1
        dc_r = refs[p:p + ncst]; p += ncst
        ds_r = refs[p:p + ns]

        @pl.when(pl.program_id(0) == 0)
        def _():
            for s in ds_r:
                s[...] = jnp.zeros(s.shape, F32)
            for r in dc_r:
                r[...] = jnp.zeros(r.shape, F32)

        tab_v = [r[...] for r in tab_r]
        xt_v = [r[...] for r in xt_r]

        def g(c, x, s):
            (y,), new = f(tab_v, c, x, xt_v, s)
            return y, new

        _, vjp = jax.vjp(g, [r[...] for r in c_r], [r[...] for r in x_r], [r[...] for r in sv_r])
        dc, dx, ds = vjp((dy_ref[...], [s[...] for s in ds_r]))
        dx_ref[...] = assemble(dx)
        for r, v in zip(dc_r, dc):
            r[...] += v
        for s, v in zip(ds_r, ds):
            s[...] = v

    win = [pl.BlockSpec((rows, w), lambda j, _cb=cb: (nc - 1 - j, _cb)) for (_, w, cb) in list(xs) + list(xtabs)]
    in_specs = [_full_spec(a) for a in list(tabs) + list(consts)] + win
    args = list(tabs) + list(consts) + [a for (a, _, _) in list(xs) + list(xtabs)]
    for (r, c), sv in zip(state_shapes, saved):
        in_specs.append(pl.BlockSpec((r, c), lambda j: (nc - 1 - j, 0)))
        args.append(sv)
    in_specs.append(pl.BlockSpec((rows, dy[1]), lambda j: (nc - 1 - j, dy[2])))
    args.append(dy[0])
    io_alias = {}
    if alias:
        in_specs.append(pl.BlockSpec(memory_space=pl.ANY))
        io_alias = {len(args): 0}
        args.append(dx_alias)
    out_shape = [jax.ShapeDtypeStruct((seq, dx_total), F32)] + [jax.ShapeDtypeStruct(a.shape, F32) for a in consts]
    out_specs = [pl.BlockSpec((rows, dx_width), lambda j: (nc - 1 - j, dx_cb))] + [_full_spec(a) for a in consts]
    res = pl.pallas_call(
        body, name=name, grid=(nc,), in_specs=in_specs, out_specs=out_specs, out_shape=out_shape,
        scratch_shapes=[pltpu.VMEM(s, F32) for s in state_shapes], input_output_aliases=io_alias,
        compiler_params=pltpu.CompilerParams(dimension_semantics=("arbitrary",), vmem_limit_bytes=VMEM_LIMIT),
    )(*args)
    return res[0], list(res[1:])


def _tile(n, want):
    t = min(n, want)
    while n % t:
        t //= 2
    return t


def _matmul(name, a, b, mode, out_dtype=F32, a_pro=None, epi=None, epi_arr=None, tm=512, tn=1024, tk=1024):
    if mode == "nn":
        (m, k), (k2, n) = a.shape, b.shape
    elif mode == "nt":
        (m, k), (n, k2) = a.shape, b.shape
    else:
        (k, m), (k2, n) = a.shape, b.shape
    assert k == k2, (name, a.shape, b.shape)
    tm, tn, tk = _tile(m, tm), _tile(n, tn), _tile(k, tk)
    nk = k // tk
    ca, cb = {"nn": (1, 0), "nt": (1, 1), "tn": (0, 0)}[mode]

    def body(*refs):
        if epi is None:
            a_ref, b_ref, o_ref, acc = refs
            e_ref = None
        else:
            a_ref, b_ref, e_ref, o_ref, acc = refs
        kk = pl.program_id(2)

        @pl.when(kk == 0)
        def _():
            acc[...] = jnp.zeros(acc.shape, F32)

        av = a_ref[...]
        if a_pro == "relu2":
            r = jnp.maximum(av, 0.0)
            av = r * r
        acc[...] += _dg(_lo(av), _lo(b_ref[...]), ca, cb)

        @pl.when(kk == nk - 1)
        def _():
            r = acc[...]
            if epi == "add":
                r = r + e_ref[...]
            elif epi == "drelu2":
                r = r * (2.0 * jnp.maximum(e_ref[...], 0.0))
            o_ref[...] = r.astype(out_dtype)

    if mode == "tn":
        a_spec = pl.BlockSpec((tk, tm), lambda j, i, kk: (kk, i))
    else:
        a_spec = pl.BlockSpec((tm, tk), lambda j, i, kk: (i, kk))
    if mode == "nt":
        b_spec = pl.BlockSpec((tn, tk), lambda j, i, kk: (j, kk))
    else:
        b_spec = pl.BlockSpec((tk, tn), lambda j, i, kk: (kk, j))
    o_spec = pl.BlockSpec((tm, tn), lambda j, i, kk: (i, j))
    in_specs, args = [a_spec, b_spec], [a, b]
    if epi is not None:
        in_specs.append(o_spec)
        args.append(epi_arr)
    return pl.pallas_call(
        body, name=name, grid=(n // tn, m // tm, nk), in_specs=in_specs, out_specs=o_spec,
        out_shape=jax.ShapeDtypeStruct((m, n), out_dtype), scratch_shapes=[pltpu.VMEM((tm, tn), F32)],
        compiler_params=pltpu.CompilerParams(dimension_semantics=("parallel", "parallel", "arbitrary"),
                                             vmem_limit_bytes=VMEM_LIMIT),
    )(*args)


ROW_TILE = 512


def _rmsnorm_fwd(name, x, w):
    seq, d = x.shape
    tr = _tile(seq, ROW_TILE)

    def body(x_ref, w_ref, o_ref):
        xv = x_ref[...]
        o_ref[...] = (_unit_rms(xv) * w_ref[...]).astype(_MXU_DTYPE)

    return pl.pallas_call(
        body, name=name, grid=(seq // tr,),
        in_specs=[pl.BlockSpec((tr, d), lambda i: (i, 0)), pl.BlockSpec((1, d), lambda i: (0, 0))],
        out_specs=pl.BlockSpec((tr, d), lambda i: (i, 0)), out_shape=jax.ShapeDtypeStruct((seq, d), _MXU_DTYPE),
        compiler_params=pltpu.CompilerParams(dimension_semantics=("parallel",), vmem_limit_bytes=VMEM_LIMIT),
    )(x, w)


def _rmsnorm_bwd(name, dh, x, w, dres):
    seq, d = x.shape
    tr = _tile(seq, ROW_TILE)

    def body(dh_ref, x_ref, w_ref, dres_ref, dx_ref, dw_ref):
        @pl.when(pl.program_id(0) == 0)
        def _():
            dw_ref[...] = jnp.zeros(dw_ref.shape, F32)

        xv = x_ref[...]
        rstd = lax.rsqrt(jnp.mean(xv * xv, axis=-1, keepdims=True) + EPS)
        xh = xv * rstd
        dhv = dh_ref[...]
        g = dhv * w_ref[...]
        dx_ref[...] = dres_ref[...] + rstd * (g - xh * jnp.mean(g * xh, axis=-1, keepdims=True))
        dw_ref[...] += jnp.sum(dhv * xh, axis=0, keepdims=True)

    row = pl.BlockSpec((tr, d), lambda i: (i, 0))
    vec = pl.BlockSpec((1, d), lambda i: (0, 0))
    return pl.pallas_call(
        body, name=name, grid=(seq // tr,), in_specs=[row, row, vec, row], out_specs=[row, vec],
        out_shape=[jax.ShapeDtypeStruct((seq, d), F32), jax.ShapeDtypeStruct((1, d), F32)],
        compiler_params=pltpu.CompilerParams(dimension_semantics=("arbitrary",), vmem_limit_bytes=VMEM_LIMIT),
    )(dh, x, w, dres)


def _loss_head(name, x, w, target):
    seq, d = x.shape
    tr = _tile(seq, ROW_TILE)

    def body(x_ref, w_ref, t_ref, loss_ref, dx_ref, dw_ref):
        @pl.when(pl.program_id(0) == 0)
        def _():
            dw_ref[...] = jnp.zeros(dw_ref.shape, F32)
            loss_ref[...] = jnp.zeros(loss_ref.shape, F32)

        xv = x_ref[...]
        rstd = lax.rsqrt(jnp.mean(xv * xv, axis=-1, keepdims=True) + EPS)
        xh = xv * rstd
        err = xh * w_ref[...] - t_ref[...]
        per_row = jnp.mean(err * err, axis=-1, keepdims=True)
        loss_ref[...] += 0.5 * jnp.sum(per_row, axis=0, keepdims=True)
        dy = err * (1.0 / d)
        g = dy * w_ref[...]
        dx_ref[...] = rstd * (g - xh * jnp.mean(g * xh, axis=-1, keepdims=True))
        dw_ref[...] += jnp.sum(dy * xh, axis=0, keepdims=True)

    row = pl.BlockSpec((tr, d), lambda i: (i, 0))
    vec = pl.BlockSpec((1, d), lambda i: (0, 0))
    one = pl.BlockSpec((1, 1), lambda i: (0, 0))
    return pl.pallas_call(
        body, name=name, grid=(seq // tr,), in_specs=[row, vec, row], out_specs=[one, row, vec],
        out_shape=[jax.ShapeDtypeStruct((1, 1), F32), jax.ShapeDtypeStruct((seq, d), F32),
                   jax.ShapeDtypeStruct((1, d), F32)],
        compiler_params=pltpu.CompilerParams(dimension_semantics=("arbitrary",), vmem_limit_bytes=VMEM_LIMIT),
    )(x, w, target)


def _adamw(name, w, g, m, v):
    rows = w.shape[0]
    tr = _tile(rows, 256)

    def body(w_ref, g_ref, m_ref, v_ref, d_ref, nm_ref, nv_ref):
        gv = g_ref[...]
        nm = ADAM_B1 * m_ref[...] + (1.0 - ADAM_B1) * gv
        nv = ADAM_B2 * v_ref[...] + (1.0 - ADAM_B2) * (gv * gv)
        m_hat = nm / (1.0 - ADAM_B1 ** ADAM_STEP)
        v_hat = nv / (1.0 - ADAM_B2 ** ADAM_STEP)
        d_ref[...] = -ADAM_LR * (m_hat / (jnp.sqrt(v_hat) + ADAM_EPS) + ADAM_WD * w_ref[...])
        nm_ref[...] = nm
        nv_ref[...] = nv

    spec = pl.BlockSpec((tr, LANES), lambda i: (i, 0))
    sds = jax.ShapeDtypeStruct(w.shape, F32)
    return pl.pallas_call(
        body, name=name, grid=(rows // tr,), in_specs=[spec] * 4, out_specs=[spec] * 3, out_shape=[sds] * 3,
        compiler_params=pltpu.CompilerParams(dimension_semantics=("parallel",), vmem_limit_bytes=VMEM_LIMIT),
    )(w, g, m, v)


def _add_halves(name, g, t1, c):
    nsec, rows, _ = g.shape
    rh = rows // 2
    tr = _tile(rh, 256)
    nb = rh // tr

    def body(c_ref, g_ref, t_ref, o_ref):
        o_ref[...] = g_ref[...] + t_ref[...]

    gs = pltpu.PrefetchScalarGridSpec(
        num_scalar_prefetch=1, grid=(nsec, nb),
        in_specs=[pl.BlockSpec((1, tr, LANES), lambda s, i, c_ref: (s, c_ref[0] * nb + i, 0)),
                  pl.BlockSpec((1, tr, LANES), lambda s, i, c_ref: (s, i, 0))],
        out_specs=pl.BlockSpec((1, tr, LANES), lambda s, i, c_ref: (s, i, 0)))
    return pl.pallas_call(
        body, name=name, grid_spec=gs, out_shape=jax.ShapeDtypeStruct((nsec, rh, LANES), F32),
        compiler_params=pltpu.CompilerParams(dimension_semantics=("parallel", "parallel"),
                                             vmem_limit_bytes=VMEM_LIMIT),
    )(c, g, t1)


def _add_four(name, t2):
    _, rh, _ = t2.shape
    tr = _tile(rh, 256)

    def body(t_ref, o_ref):
        o_ref[...] = ((t_ref[0] + t_ref[1]) + t_ref[2]) + t_ref[3]

    return pl.pallas_call(
        body, name=name, grid=(rh // tr,), in_specs=[pl.BlockSpec((4, tr, LANES), lambda i: (0, i, 0))],
        out_specs=pl.BlockSpec((tr, LANES), lambda i: (i, 0)), out_shape=jax.ShapeDtypeStruct((rh, LANES), F32),
        compiler_params=pltpu.CompilerParams(dimension_semantics=("parallel",), vmem_limit_bytes=VMEM_LIMIT),
    )(t2)


ANY = pl.BlockSpec(memory_space=pl.ANY)


def _place():
    return lax.axis_index("x"), lax.axis_index("y"), lax.axis_index("c")


def _all_gather_shards(name, slab):
    rows = slab.shape[0]
    rh = rows // 2

    def body(x_ref, out_ref, send_sems, recv_sems, local_sem):
        x, y, c = _place()
        sibling = (x, y, 1 - c)
        chips = [(1 - x, y), (x, 1 - y), (1 - x, 1 - y)]

        def part(px, py, pc):
            return out_ref.at[2 * px + py, pl.ds(pc * rh, rh), :]

        def copy(k, block, to, src=None):
            return pltpu.make_async_remote_copy(
                src_ref=part(*block) if src is None else src, dst_ref=part(*block),
                send_sem=send_sems.at[k], recv_sem=recv_sems.at[k], device_id=to, device_id_type=MESH)

        mine = pltpu.make_async_copy(x_ref, out_ref.at[2 * x + y], local_sem)
        mine.start()
        my_half = x_ref.at[pl.ds(c * rh, rh), :]
        first = [copy(j, (x, y, c), (*chip, c), src=my_half) for j, chip in enumerate(chips)]
        for cp in first:
            cp.start()
        passed = [copy(3 + j, (*chip, c), sibling) for j, chip in enumerate(chips)]
        for j, chip in enumerate(chips):
            copy(j, (*chip, c), (x, y, c)).wait_recv()
            passed[j].start()
        for j, chip in enumerate(chips):
            copy(3 + j, (*chip, 1 - c), (x, y, c)).wait_recv()
        for cp in first + passed:
            cp.wait_send()
        mine.wait()

    return pl.pallas_call(
        body, name=name, in_specs=[ANY], out_specs=ANY,
        out_shape=jax.ShapeDtypeStruct((4, rows, LANES), slab.dtype),
        scratch_shapes=[pltpu.SemaphoreType.DMA((6,)), pltpu.SemaphoreType.DMA((6,)), pltpu.SemaphoreType.DMA],
    )(slab)


def _swap_halves(name, g):
    nsec, rows, _ = g.shape
    rh = rows // 2

    def body(g_ref, t_ref, send_sem, recv_sem):
        x, y, c = _place()
        cp = pltpu.make_async_remote_copy(
            src_ref=g_ref.at[:, pl.ds((1 - c) * rh, rh), :], dst_ref=t_ref, send_sem=send_sem, recv_sem=recv_sem,
            device_id=(x, y, 1 - c), device_id_type=MESH)
        cp.start()
        cp.wait()

    return pl.pallas_call(
        body, name=name, in_specs=[ANY], out_specs=ANY, out_shape=jax.ShapeDtypeStruct((nsec, rh, LANES), F32),
        scratch_shapes=[pltpu.SemaphoreType.DMA, pltpu.SemaphoreType.DMA],
    )(g)


def _scatter_to_owners(name, p):
    _, rh, _ = p.shape

    def body(p_ref, t_ref, send_sems, recv_sems, local_sem):
        x, y, c = _place()
        me = 2 * x + y
        chips = [(1 - x, y), (x, 1 - y), (1 - x, 1 - y)]
        mine = pltpu.make_async_copy(p_ref.at[me], t_ref.at[me], local_sem)
        mine.start()
        sends = []
        for j, (cx, cy) in enumerate(chips):
            sends.append(pltpu.make_async_remote_copy(
                src_ref=p_ref.at[2 * cx + cy], dst_ref=t_ref.at[me], send_sem=send_sems.at[j],
                recv_sem=recv_sems.at[j], device_id=(cx, cy, c), device_id_type=MESH))
        for cp in sends:
            cp.start()
        for j, (cx, cy) in enumerate(chips):
            pltpu.make_async_remote_copy(
                src_ref=p_ref.at[me], dst_ref=t_ref.at[2 * cx + cy], send_sem=send_sems.at[j],
                recv_sem=recv_sems.at[j], device_id=(cx, cy, c), device_id_type=MESH).wait_recv()
        for cp in sends:
            cp.wait_send()
        mine.wait()

    return pl.pallas_call(
        body, name=name, in_specs=[ANY], out_specs=ANY, out_shape=jax.ShapeDtypeStruct((4, rh, LANES), F32),
        scratch_shapes=[pltpu.SemaphoreType.DMA((3,)), pltpu.SemaphoreType.DMA((3,)), pltpu.SemaphoreType.DMA],
    )(p)


def _join_halves(name, r_half):
    rh = r_half.shape[0]

    def body(h_ref, o_ref, send_sem, recv_sem, local_sem):
        x, y, c = _place()
        mine = pltpu.make_async_copy(h_ref, o_ref.at[pl.ds(c * rh, rh), :], local_sem)
        mine.start()
        cp = pltpu.make_async_remote_copy(
            src_ref=h_ref, dst_ref=o_ref.at[pl.ds(c * rh, rh), :], send_sem=send_sem, recv_sem=recv_sem,
            device_id=(x, y, 1 - c), device_id_type=MESH)
        cp.start()
        pltpu.make_async_remote_copy(
            src_ref=h_ref, dst_ref=o_ref.at[pl.ds((1 - c) * rh, rh), :], send_sem=send_sem, recv_sem=recv_sem,
            device_id=(x, y, 1 - c), device_id_type=MESH).wait_recv()
        cp.wait_send()
        mine.wait()

    return pl.pallas_call(
        body, name=name, in_specs=[ANY], out_specs=ANY, out_shape=jax.ShapeDtypeStruct((2 * rh, LANES), F32),
        scratch_shapes=[pltpu.SemaphoreType.DMA, pltpu.SemaphoreType.DMA, pltpu.SemaphoreType.DMA],
    )(r_half)


def _rows_of(n):
    return -(-n // LANES)


def _pack(arrays, dtype, align=32):
    parts = []
    for a in arrays:
        flat = a.reshape(-1).astype(dtype)
        parts.append(jnp.pad(flat, (0, _rows_of(flat.size) * LANES - flat.size)))
    flat = jnp.concatenate(parts)
    rows = flat.size // LANES
    rows_pad = -(-rows // align) * align
    return jnp.pad(flat, (0, (rows_pad - rows) * LANES)).reshape(rows_pad, LANES)


def _unpack(slab, shapes):
    out, r = [], 0
    for shp in shapes:
        n = math.prod(shp)
        out.append(slab[r:r + _rows_of(n)].reshape(-1)[:n].reshape(shp))
        r += _rows_of(n)
    return out


def _unpack_gathered(g, shapes, kinds):
    out, r = [], 0
    for shp, kind in zip(shapes, kinds):
        n = math.prod(shp)
        blk = g[:, r:r + _rows_of(n)].reshape(4, -1)[:, :n].reshape((4,) + tuple(shp))
        r += _rows_of(n)
        if kind == "col":
            out.append(jnp.moveaxis(blk, 0, 1).reshape(shp[0], 4 * shp[1]))
        else:
            out.append(blk.reshape(4 * shp[0], shp[1]))
    return out


def _shard_block(g, kind, s, local_shape):
    if kind == "col":
        return g[:, s * local_shape[1]:(s + 1) * local_shape[1]]
    if kind == "row":
        return g[s * local_shape[0]:(s + 1) * local_shape[0]]
    return g


def _rotary_tables(seq):
    half = RET_DK // 2
    pos = jnp.arange(seq, dtype=F32)
    inv = ROPE_THETA ** (-jnp.arange(half, dtype=F32) / half)
    ang = pos[:, None] * inv[None, :]
    cos, sin = jnp.cos(ang), jnp.sin(ang)
    return jnp.concatenate([cos, cos], axis=1), jnp.concatenate([-sin, sin], axis=1)


def _retention_tables():
    log_gamma = jnp.log(1.0 - 2.0 ** (-5.0 - jnp.arange(RET_HEADS, dtype=F32)))
    idx = jnp.arange(CHUNK, dtype=F32)
    diff = idx[:, None] - idx[None, :]
    dmask = jnp.exp(jnp.where((diff >= 0)[None], log_gamma[:, None, None] * diff[None], -jnp.inf))
    kdec = jnp.exp(log_gamma[None, :] * (CHUNK - 1.0 - idx)[:, None])
    qdec = jnp.exp(log_gamma[None, :] * (idx + 1.0)[:, None])
    cdec = jnp.exp(log_gamma * CHUNK)[None, :]
    lanes = lambda t: jnp.repeat(t, RET_DK, axis=1)
    return dmask.reshape(RET_HEADS * CHUNK, CHUNK), lanes(kdec), lanes(qdec), lanes(cdec)


def _s5_prep(a_re, a_im, log_step, b_re, b_im, c_re, c_im):
    g, n, c = S5_GROUPS, S5_STATE, S5_GROUP
    lam = lax.complex(a_re, a_im)
    step = jnp.exp(log_step)[:, None]
    lam_bar = jnp.exp(lam * step)
    b_bar = ((lam_bar - 1.0) / lam)[..., None] * lax.complex(b_re, b_im)
    eye = jnp.eye(g, dtype=F32)
    bb_re = (jnp.real(b_bar).transpose(0, 2, 1)[:, :, None, :] * eye[:, None, :, None]).reshape(g * c, g * n)
    bb_im = (jnp.imag(b_bar).transpose(0, 2, 1)[:, :, None, :] * eye[:, None, :, None]).reshape(g * c, g * n)
    cc_re = (c_re.transpose(0, 2, 1)[:, :, None, :] * eye[:, None, :, None]).reshape(g * n, g * c)
    cc_im = (c_im.transpose(0, 2, 1)[:, :, None, :] * eye[:, None, :, None]).reshape(g * n, g * c)
    return (jnp.real(lam_bar).reshape(1, g * n), jnp.imag(lam_bar).reshape(1, g * n),
            jnp.concatenate([bb_re, bb_im], axis=1), cc_re, cc_im)


def kernel(x, l0_norm_mix, l0_w_in, ssd_conv_w, ssd_conv_b, ssd_dt_bias, ssd_A_log, ssd_D, ssd_norm_w, l0_w_out, l0_norm_mlp, l0_w_up, l0_w_down, l1_norm_mix, l1_w_in, gdn_conv_w, gdn_A_log, gdn_dt_bias, gdn_norm_w, s5_A_re, s5_A_im, s5_log_step, s5_B_re, s5_B_im, s5_C_re, s5_C_im, s5_D, s5_w_glu, s5_b_glu, l1_w_out, l1_norm_mlp, l1_w_up, l1_w_down, final_norm, loss_target, m_l0_norm_mix, m_l0_w_in, m_ssd_conv_w, m_ssd_conv_b, m_ssd_dt_bias, m_ssd_A_log, m_ssd_D, m_ssd_norm_w, m_l0_w_out, m_l0_norm_mlp, m_l0_w_up, m_l0_w_down, m_l1_norm_mix, m_l1_w_in, m_gdn_conv_w, m_gdn_A_log, m_gdn_dt_bias, m_gdn_norm_w, m_s5_A_re, m_s5_A_im, m_s5_log_step, m_s5_B_re, m_s5_B_im, m_s5_C_re, m_s5_C_im, m_s5_D, m_s5_w_glu, m_s5_b_glu, m_l1_w_out, m_l1_norm_mlp, m_l1_w_up, m_l1_w_down, m_final_norm, v_l0_norm_mix, v_l0_w_in, v_ssd_conv_w, v_ssd_conv_b, v_ssd_dt_bias, v_ssd_A_log, v_ssd_D, v_ssd_norm_w, v_l0_w_out, v_l0_norm_mlp, v_l0_w_up, v_l0_w_down, v_l1_norm_mix, v_l1_w_in, v_gdn_conv_w, v_gdn_A_log, v_gdn_dt_bias, v_gdn_norm_w, v_s5_A_re, v_s5_A_im, v_s5_log_step, v_s5_B_re, v_s5_B_im, v_s5_C_re, v_s5_C_im, v_s5_D, v_s5_w_glu, v_s5_b_glu, v_l1_w_out, v_l1_norm_mlp, v_l1_w_up, v_l1_w_down, v_final_norm):
    given = dict(locals())
    names = [n for n, _ in PARAMS]
    kinds = dict(PARAMS)
    w = {n: given[n] for n in names}
    seq = x.shape[1]
    x0 = x.reshape(seq, D_MODEL)
    target = loss_target.reshape(seq, D_MODEL)

    gb = _all_gather_shards("gather_weights", _pack([w[n] for n in GATHER_BF16], _MXU_DTYPE))
    full = dict(zip(GATHER_BF16, _unpack_gathered(gb, [w[n].shape for n in GATHER_BF16],
                                                  [kinds[n] for n in GATHER_BF16])))
    gf = _all_gather_shards("gather_conv", _pack([w[n] for n in GATHER_F32], F32))
    full.update(zip(GATHER_F32, _unpack_gathered(gf, [w[n].shape for n in GATHER_F32],
                                                 [kinds[n] for n in GATHER_F32])))
    in0 = full["l0_w_in"].shape[1]
    w_in0 = jnp.pad(full["l0_w_in"], ((0, 0), (0, IN0_PAD - in0)))
    wi1 = full["l1_w_in"]
    in1 = wi1.shape[1]
    w_in1 = jnp.concatenate([wi1[:, :3072], wi1[:, 3084:in1], wi1[:, 3072:3084],
                             jnp.zeros((D_MODEL, IN1_PAD - in1), wi1.dtype)], axis=1)

    row = lambda a: a.reshape(1, -1)
    lanes64 = lambda a: jnp.repeat(a, SSD_HEAD_DIM).reshape(1, -1)

    h0 = _rmsnorm_fwd("norm_mix0", x0, row(w["l0_norm_mix"]))
    proj0 = _matmul("in_proj0", h0, w_in0, "nn")
    cos_t, sin_t = _rotary_tables(seq)
    ret_tabs = list(_retention_tables())
    ret_xs = [(proj0, 512, 0), (proj0, 512, 1), (proj0, 512, 2), (proj0, 512, 3)]
    ret_xt = [(cos_t, 128, 0), (sin_t, 128, 0)]
    ret_states = [(512, 128)]
    mixed0, ret_saved = _scan_fwd("ret_fwd", _f_ret, CHUNK, ret_tabs, [], ret_xs, ret_xt, ret_states, D_MODEL, 512, 0)
    expand = jnp.repeat(jnp.eye(128, SSD_HEADS, dtype=F32), SSD_HEAD_DIM, axis=1)
    ssd_consts = [full["ssd_conv_w"], row(w["ssd_conv_b"]), lanes64(w["ssd_dt_bias"]), lanes64(w["ssd_A_log"]),
                  lanes64(w["ssd_D"]), row(w["ssd_norm_w"])]
    ssd_xs = [(proj0, 512, 4), (proj0, 512, 5), (proj0, 256, 12), (proj0, 256, 13), (proj0, 128, 28)]
    ssd_states = [(8, 512), (8, 256), (8, 256), (512, 128)]
    mixed0, ssd_saved = _scan_fwd("ssd_fwd", _f_ssd, CHUNK, [expand], ssd_consts, ssd_xs, [], ssd_states,
                                  D_MODEL, 512, 1, y_alias=mixed0)
    x1 = _matmul("out_proj0", mixed0, full["l0_w_out"], "nn", epi="add", epi_arr=x0)
    h1 = _rmsnorm_fwd("norm_mlp0", x1, row(w["l0_norm_mlp"]))
    u0 = _matmul("up0", h1, full["l0_w_up"], "nn")
    x2 = _matmul("down0", u0, full["l0_w_down"], "nn", a_pro="relu2", epi="add", epi_arr=x1)

    h2 = _rmsnorm_fwd("norm_mix1", x2, row(w["l1_norm_mix"]))
    proj1 = _matmul("in_proj1", h2, w_in1, "nn")
    p_alog = jnp.zeros((1, 128), F32).at[0, 6:12].set(w["gdn_A_log"])
    p_dtb = jnp.zeros((1, 128), F32).at[0, 6:12].set(w["gdn_dt_bias"])
    gdn_consts = [full["gdn_conv_w"], p_alog, p_dtb, row(w["gdn_norm_w"])]
    gdn_xs = [(proj1, 768, 0), (proj1, 768, 1), (proj1, 768, 2), (proj1, 768, 3), (proj1, 128, 26)]
    gdn_states = [(8, 768), (8, 768), (8, 768), (768, 128)]
    mixed1, gdn_saved = _scan_fwd("gdn_fwd", _f_gdn, CHUNK, [], gdn_consts, gdn_xs, [], gdn_states, D_MODEL, 768, 0)
    s5_args = (w["s5_A_re"], w["s5_A_im"], w["s5_log_step"], w["s5_B_re"], w["s5_B_im"], w["s5_C_re"], w["s5_C_im"])
    (lam_re, lam_im, bblk, cc_re, cc_im), s5_prep_vjp = jax.vjp(_s5_prep, *s5_args)
    s5_consts = [lam_re, lam_im, bblk, cc_re, cc_im, row(w["s5_D"]), full["s5_w_glu"].astype(F32), row(w["s5_b_glu"])]
    s5_xs = [(proj1, 256, 12)]
    s5_states = [(8, 1024), (8, 1024)]
    mixed1, s5_saved = _scan_fwd("s5_fwd", _f_s5, CHUNK, [], s5_consts, s5_xs, [], s5_states, D_MODEL, 256, 3,
                                 y_alias=mixed1)
    x3 = _matmul("out_proj1", mixed1, full["l1_w_out"], "nn", epi="add", epi_arr=x2)
    h3 = _rmsnorm_fwd("norm_mlp1", x3, row(w["l1_norm_mlp"]))
    u1 = _matmul("up1", h3, full["l1_w_up"], "nn")
    x4 = _matmul("down1", u1, full["l1_w_down"], "nn", a_pro="relu2", epi="add", epi_arr=x3)

    loss_part, dx4, d_final = _loss_head("loss_head", x4, row(w["final_norm"]), target)
    loss = lax.psum(loss_part[0, 0], ("x", "y", "c"))
    grads = {"final_norm": d_final.reshape(-1)}

    du1 = _matmul("down1_dx", dx4, full["l1_w_down"], "nt", out_dtype=_MXU_DTYPE, epi="drelu2", epi_arr=u1)
    grads["l1_w_down"] = _matmul("down1_dw", u1, dx4, "tn", a_pro="relu2", tm=512, tn=1024, tk=512)
    grads["l1_w_up"] = _matmul("up1_dw", h3, du1, "tn", tm=512, tn=1024, tk=512)
    dh3 = _matmul("up1_dx", du1, full["l1_w_up"], "nt")
    dx3, dwn = _rmsnorm_bwd("norm_mlp1_bwd", dh3, x3, row(w["l1_norm_mlp"]), dx4)
    grads["l1_norm_mlp"] = dwn.reshape(-1)
    grads["l1_w_out"] = _matmul("out_proj1_dw", mixed1, dx3, "tn", tm=512, tn=1024, tk=512)
    dmixed1 = _matmul("out_proj1_dx", dx3, full["l1_w_out"], "nt")

    def gdn_assemble(dx):
        dq, dk, dv, dz, dba = dx
        zeros = lambda n: jnp.zeros((CHUNK, n), F32)
        return jnp.concatenate([dq, dk, dv, dz, zeros(256), dba, zeros(IN1_PAD - 3456)], axis=1)

    dproj1, gdn_dc = _scan_bwd("gdn_bwd", _f_gdn, CHUNK, [], gdn_consts, gdn_xs, [], gdn_saved, gdn_states,
                               (dmixed1, 768, 0), IN1_PAD, IN1_PAD, 0, gdn_assemble)
    dproj1, s5_dc = _scan_bwd("s5_bwd", _f_s5, CHUNK, [], s5_consts, s5_xs, [], s5_saved, s5_states,
                              (dmixed1, 256, 3), IN1_PAD, 256, 12, lambda dx: dx[0], dx_alias=dproj1)
    grads["gdn_conv_w"] = gdn_dc[0]
    grads["gdn_A_log"] = gdn_dc[1][0, 6:12]
    grads["gdn_dt_bias"] = gdn_dc[2][0, 6:12]
    grads["gdn_norm_w"] = gdn_dc[3].reshape(-1)
    s5_pg = s5_prep_vjp(tuple(s5_dc[:5]))
    for n, gval in zip(("s5_A_re", "s5_A_im", "s5_log_step", "s5_B_re", "s5_B_im", "s5_C_re", "s5_C_im"), s5_pg):
        grads[n] = gval
    grads["s5_D"] = s5_dc[5].reshape(-1)
    grads["s5_w_glu"] = s5_dc[6]
    grads["s5_b_glu"] = s5_dc[7].reshape(-1)
    dwi1 = _matmul("in_proj1_dw", h2, dproj1, "tn", tm=512, tn=512, tk=512)
    grads["l1_w_in"] = jnp.concatenate([dwi1[:, :3072], dwi1[:, 3328:3340], dwi1[:, 3072:3328]], axis=1)
    dh2 = _matmul("in_proj1_dx", dproj1, w_in1, "nt")
    dx2, dwn = _rmsnorm_bwd("norm_mix1_bwd", dh2, x2, row(w["l1_norm_mix"]), dx3)
    grads["l1_norm_mix"] = dwn.reshape(-1)

    du0 = _matmul("down0_dx", dx2, full["l0_w_down"], "nt", out_dtype=_MXU_DTYPE, epi="drelu2", epi_arr=u0)
    grads["l0_w_down"] = _matmul("down0_dw", u0, dx2, "tn", a_pro="relu2", tm=512, tn=1024, tk=512)
    grads["l0_w_up"] = _matmul("up0_dw", h1, du0, "tn", tm=512, tn=1024, tk=512)
    dh1 = _matmul("up0_dx", du0, full["l0_w_up"], "nt")
    dx1, dwn = _rmsnorm_bwd("norm_mlp0_bwd", dh1, x1, row(w["l0_norm_mlp"]), dx2)
    grads["l0_norm_mlp"] = dwn.reshape(-1)
    grads["l0_w_out"] = _matmul("out_proj0_dw", mixed0, dx1, "tn", tm=512, tn=1024, tk=512)
    dmixed0 = _matmul("out_proj0_dx", dx1, full["l0_w_out"], "nt")
    dproj0, _ = _scan_bwd("ret_bwd", _f_ret, CHUNK, ret_tabs, [], ret_xs, ret_xt, ret_saved, ret_states,
                          (dmixed0, 512, 0), IN0_PAD, 2048, 0, lambda dx: jnp.concatenate(dx, axis=1))

    def ssd_assemble(dx):
        return jnp.concatenate(list(dx) + [jnp.zeros((CHUNK, 2048 - 1664), F32)], axis=1)

    dproj0, ssd_dc = _scan_bwd("ssd_bwd", _f_ssd, CHUNK, [expand], ssd_consts, ssd_xs, [], ssd_saved, ssd_states,
                               (dmixed0, 512, 1), IN0_PAD, 2048, 1, ssd_assemble, dx_alias=dproj0)
    heads = lambda a: a.reshape(SSD_HEADS, SSD_HEAD_DIM).sum(axis=1)
    grads["ssd_conv_w"] = ssd_dc[0]
    grads["ssd_conv_b"] = ssd_dc[1].reshape(-1)
    grads["ssd_dt_bias"] = heads(ssd_dc[2])
    grads["ssd_A_log"] = heads(ssd_dc[3])
    grads["ssd_D"] = heads(ssd_dc[4])
    grads["ssd_norm_w"] = ssd_dc[5].reshape(-1)
    grads["l0_w_in"] = _matmul("in_proj0_dw", h0, dproj0, "tn", tm=512, tn=1024, tk=512)[:, :in0]
    dh0 = _matmul("in_proj0_dx", dproj0, w_in0, "nt")
    dx0, dwn = _rmsnorm_bwd("norm_mix0_bwd", dh0, x0, row(w["l0_norm_mix"]), dx1)
    grads["l0_norm_mix"] = dwn.reshape(-1)
    grad_x = dx0.reshape(x.shape)

    c_idx = lax.axis_index("c").astype(jnp.int32).reshape(1)
    sections = []
    for s in range(4):
        sections.append(_pack([_shard_block(grads[n].reshape(_full_shape(n, w, kinds)), kinds[n], s, w[n].shape)
                               for n in names], F32))
    gslab = jnp.stack(sections)
    from_sibling = _swap_halves("grads_swap_halves", gslab)
    chip_sum = _add_halves("grads_add_sibling", gslab, from_sibling, c_idx)
    partials = _scatter_to_owners("grads_scatter", chip_sum)
    my_half = _add_four("grads_add_chips", partials)
    gsum = _join_halves("grads_join_halves", my_half)

    shapes = [w[n].shape for n in names]
    delta, new_m, new_v = _adamw("adamw", _pack([w[n] for n in names], F32), gsum,
                                 _pack([given["m_" + n] for n in names], F32),
                                 _pack([given["v_" + n] for n in names], F32))
    return (loss, grad_x, *_unpack(gsum, shapes), *_unpack(delta, shapes), *_unpack(new_m, shapes),
            *_unpack(new_v, shapes))


def _full_shape(name, w, kinds):
    shp = w[name].shape
    if kinds[name] == "col":
        return (shp[0], 4 * shp[1])
    if kinds[name] == "row":
        return (4 * shp[0],) + tuple(shp[1:])
    return shp
```

```python
import functools
import math

import jax
import jax.numpy as jnp
from jax import lax
from jax.experimental import pallas as pl
from jax.experimental.pallas import tpu as pltpu

F32 = jnp.float32
_MXU_DTYPE = jnp.bfloat16

D_MODEL = 1024
CHUNK = 64
EPS = 1e-6
RET_HEADS, RET_DK = 4, 128
ROPE_THETA = 10000.0
SSD_HEADS, SSD_HEAD_DIM = 8, 64
GDN_HEADS, GDN_DK = 6, 128
S5_GROUPS, S5_GROUP, S5_STATE = 16, 16, 64
ADAM_LR, ADAM_B1, ADAM_B2, ADAM_EPS, ADAM_WD, ADAM_STEP = 0.001, 0.9, 0.999, 1e-08, 0.01, 10

IN0_PAD = 4096
IN1_PAD = 3584
LANES = 1024
VMEM_LIMIT = 48 * 1024 * 1024
MESH = pl.DeviceIdType.MESH

PARAMS = (
    ("l0_norm_mix", "rep"), ("l0_w_in", "col"), ("ssd_conv_w", "col"), ("ssd_conv_b", "rep"),
    ("ssd_dt_bias", "rep"), ("ssd_A_log", "rep"), ("ssd_D", "rep"), ("ssd_norm_w", "rep"),
    ("l0_w_out", "row"), ("l0_norm_mlp", "rep"), ("l0_w_up", "col"), ("l0_w_down", "row"),
    ("l1_norm_mix", "rep"), ("l1_w_in", "col"), ("gdn_conv_w", "col"), ("gdn_A_log", "rep"),
    ("gdn_dt_bias", "rep"), ("gdn_norm_w", "rep"), ("s5_A_re", "rep"), ("s5_A_im", "rep"),
    ("s5_log_step", "rep"), ("s5_B_re", "rep"), ("s5_B_im", "rep"), ("s5_C_re", "rep"), ("s5_C_im", "rep"),
    ("s5_D", "rep"), ("s5_w_glu", "row"), ("s5_b_glu", "rep"), ("l1_w_out", "row"), ("l1_norm_mlp", "rep"),
    ("l1_w_up", "col"), ("l1_w_down", "row"), ("final_norm", "rep"),
)
GATHER_BF16 = ("l0_w_in", "l0_w_out", "l0_w_up", "l0_w_down", "l1_w_in", "l1_w_out", "l1_w_up", "l1_w_down", "s5_w_glu")
GATHER_F32 = ("ssd_conv_w", "gdn_conv_w")


def _dg(a, b, ca, cb, prec=None):
    return lax.dot_general(a, b, (((ca,), (cb,)), ((), ())), preferred_element_type=F32, precision=prec)


def _lo(a):
    return a.astype(_MXU_DTYPE)


@jax.custom_vjp
def _mm(a, b):
    return _dg(_lo(a), _lo(b), 1, 0)


def _mm_fwd(a, b):
    return _mm(a, b), (a, b)


def _mm_bwd(res, g):
    a, b = res
    return _dg(_lo(g), _lo(b), 1, 1), _dg(_lo(a), _lo(g), 0, 0)


_mm.defvjp(_mm_fwd, _mm_bwd)


@jax.custom_vjp
def _mm_nt(a, b):
    return _dg(_lo(a), _lo(b), 1, 1)


def _mm_nt_fwd(a, b):
    return _mm_nt(a, b), (a, b)


def _mm_nt_bwd(res, g):
    a, b = res
    return _dg(_lo(g), _lo(b), 1, 0), _dg(_lo(g), _lo(a), 0, 0)


_mm_nt.defvjp(_mm_nt_fwd, _mm_nt_bwd)


@jax.custom_vjp
def _mm_tn(a, b):
    return _dg(_lo(a), _lo(b), 0, 0)


def _mm_tn_fwd(a, b):
    return _mm_tn(a, b), (a, b)


def _mm_tn_bwd(res, g):
    a, b = res
    return _dg(_lo(b), _lo(g), 1, 1), _dg(_lo(a), _lo(g), 1, 0)


_mm_tn.defvjp(_mm_tn_fwd, _mm_tn_bwd)


def _split2(x):
    hi = _lo(x)
    return hi, _lo(x - hi.astype(F32))


def _split3(x):
    h1 = _lo(x)
    r1 = x - h1.astype(F32)
    h2 = _lo(r1)
    return h1, h2, _lo(r1 - h2.astype(F32))


def _tri_cum_dir(m, ca):
    n, w = m.shape
    causal, _ = _tri_masks(n)
    out = _dg(causal.astype(_MXU_DTYPE), jnp.concatenate(_split3(m), axis=1), ca, 0)
    return out[:, :w] + out[:, w:2 * w] + out[:, 2 * w:]


@jax.custom_vjp
def _tri_cum(m):
    return _tri_cum_dir(m, 1)


def _tri_cum_fwd(m):
    return _tri_cum_dir(m, 1), None


def _tri_cum_bwd(_, g):
    return (_tri_cum_dir(g, 0),)


_tri_cum.defvjp(_tri_cum_fwd, _tri_cum_bwd)


@jax.custom_vjp
def _mm_exact_rhs(a, e):
    return _dg(jnp.concatenate(_split3(a), axis=1), jnp.concatenate([_lo(e)] * 3, axis=0), 1, 0)


def _mm_exact_rhs_fwd(a, e):
    return _mm_exact_rhs(a, e), e


def _mm_exact_rhs_bwd(e, g):
    return _dg(jnp.concatenate(_split3(g), axis=1), jnp.concatenate([_lo(e)] * 3, axis=1), 1, 1), jnp.zeros_like(e)


_mm_exact_rhs.defvjp(_mm_exact_rhs_fwd, _mm_exact_rhs_bwd)


def _bd(x):
    left = _iota(x.shape, 1) < (x.shape[1] // 2)
    zero = jnp.zeros_like(x)
    return jnp.concatenate([jnp.where(left, x, zero), jnp.where(left, zero, x)], axis=0)


def _unbd(m):
    half = m.shape[0] // 2
    left = _iota((half, m.shape[1]), 1) < (m.shape[1] // 2)
    return jnp.where(left, m[:half], m[half:])


def _pmm_nn(x, y):
    xh, xl = _split2(x)
    yh, yl = _split2(y)
    return _dg(jnp.concatenate([xh, xl, xh], axis=1), jnp.concatenate([_bd(yh), _bd(yh), _bd(yl)], axis=0), 1, 0)


def _pmm_nt(x, y):
    xh, xl = _split2(x)
    yh, yl = _split2(y)
    return _dg(jnp.concatenate([xh, xl, xh], axis=1), jnp.concatenate([_bd(yh), _bd(yh), _bd(yl)], axis=1), 1, 1)


def _pmm_tn(x, y):
    xh, xl = _split2(x)
    yh, yl = _split2(y)
    return _unbd(_dg(jnp.concatenate([xh, xl, xh], axis=0), jnp.concatenate([yh, yh, yl], axis=0), 0, 0))


@functools.lru_cache(maxsize=None)
def _shift(s, axis):
    @jax.custom_vjp
    def sh(x):
        return pltpu.roll(x, s, axis)

    def fwd(x):
        return sh(x), None

    def bwd(_, g):
        n = g.shape[axis]
        return (pltpu.roll(g, (n - s) % n, axis),)

    sh.defvjp(fwd, bwd)
    return sh


def _iota(shape, axis):
    return lax.broadcasted_iota(jnp.int32, shape, axis)


def _silu(x):
    return x * jax.nn.sigmoid(x)


def _unit_rms(x):
    return x * lax.rsqrt(jnp.mean(x * x, axis=-1, keepdims=True) + EPS)


def _l2norm(x):
    return x * lax.rsqrt(jnp.sum(x * x, axis=-1, keepdims=True) + EPS)


def _tri_masks(n):
    r, c = _iota((n, n), 0), _iota((n, n), 1)
    return r >= c, r > c


def _packed_rc():
    return _iota((CHUNK, 2 * CHUNK), 0), _iota((CHUNK, 2 * CHUNK), 1) & (CHUNK - 1)


def _decay_packed(g_packed):
    r, c = _packed_rc()
    seg = _tri_cum(g_packed * (r > c).astype(F32))
    return jnp.where(r >= c, jnp.exp(jnp.where(r >= c, seg, 0.0)), 0.0)


def _conv(x, tail, w):
    rows, width = x.shape
    row = _iota((rows, width), 0)
    acc = x * w[3:4, :]
    pad = jnp.zeros((rows - 8, width), F32)
    for j in range(3):
        s = 3 - j
        prev = jnp.concatenate([_shift(s, 0)(tail), pad], axis=0)
        acc = acc + w[j:j + 1, :] * jnp.where(row < s, prev, _shift(s, 0)(x))
    return acc


def _tri_inv_impl(a):
    r, c = _packed_rc()
    eye = (r == c).astype(F32)

    def same_block(b):
        return (r // b) == (c // b)

    a8 = jnp.where(same_block(8), a, 0.0)
    a2 = _pmm_nn(a8, a8)
    a4 = _pmm_nn(a2, a2)
    x = _pmm_nn(_pmm_nn(eye - a8, eye + a2), eye + a4)
    for b in (8, 16, 32):
        off = jnp.where(same_block(2 * b) & jnp.logical_not(same_block(b)), a, 0.0)
        x = x - _pmm_nn(_pmm_nn(x, off), x)
    return x


@jax.custom_vjp
def _tri_inv(a):
    return _tri_inv_impl(a)


def _tri_inv_fwd(a):
    t = _tri_inv_impl(a)
    return t, t


def _tri_inv_bwd(t, g):
    return (-_pmm_nt(_pmm_tn(t, g), t),)


_tri_inv.defvjp(_tri_inv_fwd, _tri_inv_bwd)


def _f_ret(tabs, consts, xs, xtabs, states):
    dmask, kdec, qdec, cdec = tabs
    q, k, v, gate = xs
    cs, sn = xtabs
    (st,) = states
    swap = _shift(RET_DK // 2, 1)
    outs, new = [], []
    for h in range(RET_HEADS):
        sl = slice(128 * h, 128 * h + 128)
        qh, kh, vh = q[:, sl], k[:, sl], v[:, sl]
        qh = (qh * cs + swap(qh) * sn) * (RET_DK ** -0.5)
        kh = kh * cs + swap(kh) * sn
        sh = st[sl, :]
        scores = _mm_nt(qh, kh) * dmask[64 * h:64 * h + 64, :]
        y = _mm(scores, vh) + _mm(qh * qdec[:, sl], sh)
        new.append(sh * cdec[:, sl] + _mm_tn(kh * kdec[:, sl], vh))
        outs.append(_silu(gate[:, sl]) * _unit_rms(y))
    return (jnp.concatenate(outs, axis=1),), [jnp.concatenate(new, axis=0)]


def _f_ssd(tabs, consts, xs, xtabs, states):
    (expand,) = tabs
    conv_w, conv_b, dtb, alog, dskip, nw = consts
    z, xr, br, cr, dtr = xs
    tx, tb, tc, st = states
    xc = _silu(_conv(xr, tx, conv_w[:, 0:512]) + conv_b[:, 0:512])
    bc = _silu(_conv(br, tb, conv_w[:, 512:768]) + conv_b[:, 512:768])
    cc = _silu(_conv(cr, tc, conv_w[:, 768:1024]) + conv_b[:, 768:1024])
    dt = jax.nn.softplus(_mm_exact_rhs(dtr, expand) + dtb)
    la = dt * (-jnp.exp(alog))
    lacum = _tri_cum(la)
    total = jnp.sum(la, axis=0, keepdims=True)
    xd = xc * dt
    dte, ecum, cdec = jnp.exp(total - lacum), jnp.exp(lacum), jnp.exp(total)
    ys, new = [], []
    cb2 = None
    for p in range(4):
        g = p // 2
        bg, cg = bc[:, 128 * g:128 * g + 128], cc[:, 128 * g:128 * g + 128]
        if p % 2 == 0:
            cb2 = _mm_nt(cg, jnp.concatenate([bg, bg], axis=0))
        sl = slice(128 * p, 128 * p + 128)
        xdp, sp = xd[:, sl], st[sl, :]
        yp = _mm(cg, sp) * ecum[:, sl] + _mm(cb2 * _decay_packed(la[:, sl]), _bd(xdp))
        new.append(sp * cdec[:, sl] + _mm_tn(bg, xdp * dte[:, sl]))
        ys.append(yp)
    y = jnp.concatenate(ys, axis=1) + dskip * xc
    yg = y * _silu(z)
    out = jnp.concatenate([_unit_rms(yg[:, 0:256]), _unit_rms(yg[:, 256:512])], axis=1) * nw
    return (out,), [xr[CHUNK - 8:, :], br[CHUNK - 8:, :], cr[CHUNK - 8:, :], jnp.concatenate(new, axis=0)]


def _f_gdn(tabs, consts, xs, xtabs, states):
    conv_w, p_alog, p_dtb, nw = consts
    qr, kr, vr, z, ba = xs
    tq, tk, tv, st = states
    qc = _silu(_conv(qr, tq, conv_w[:, 0:768]))
    kc = _silu(_conv(kr, tk, conv_w[:, 768:1536]))
    vc = _silu(_conv(vr, tv, conv_w[:, 1536:2304]))
    gl = -jnp.exp(p_alog) * jax.nn.softplus(ba + p_dtb)
    bl = jax.nn.sigmoid(ba)
    gcum = _tri_cum(gl)
    left128 = _iota((CHUNK, 128), 1) < 64
    left256 = _iota((CHUNK, 256), 1) < 128
    r, c = _packed_rc()
    diag_blocks = (_iota((256, 256), 0) < 128) == (_iota((256, 256), 1) < 128)

    def norm2(t):
        return jnp.concatenate([_l2norm(t[:, 0:128]), _l2norm(t[:, 128:256])], axis=1)

    outs, new = [], []
    for p in range(GDN_HEADS // 2):
        sl = slice(256 * p, 256 * p + 256)

        def pick(arr, off, left):
            return jnp.where(left, arr[:, off + 2 * p:off + 2 * p + 1], arr[:, off + 2 * p + 1:off + 2 * p + 2])

        qn = norm2(qc[:, sl]) * (GDN_DK ** -0.5)
        kn = norm2(kc[:, sl])
        dec = _decay_packed(pick(gl, 6, left128))
        g2, gc2, b2 = pick(gl, 6, left256), pick(gcum, 6, left256), pick(bl, 0, left256)
        tot = jnp.sum(g2, axis=0, keepdims=True)
        eg, et, cd = jnp.exp(gc2), jnp.exp(tot - gc2), jnp.exp(tot)
        kb, vb = kn * b2, vc[:, sl] * b2
        kbd = _bd(kn)
        tm = _tri_inv(jnp.where(r > c, _mm_nt(kb, kbd) * dec, 0.0))
        u = _mm(tm, _bd(vb))
        w = _mm(tm, _bd(kb * eg))
        attn = _mm_nt(qn, kbd) * dec
        sp = st[sl, :]
        vn = u - _mm(w, sp)
        o = _mm(qn * eg, sp) + _mm(attn, _bd(vn))
        new.append(sp * cd + jnp.where(diag_blocks, _mm_tn(kn * et, vn), 0.0))
        for hh in range(2):
            osl = slice(128 * hh, 128 * hh + 128)
            zsl = slice(256 * p + 128 * hh, 256 * p + 128 * hh + 128)
            outs.append(_unit_rms(o[:, osl]) * nw * _silu(z[:, zsl]))
    return (jnp.concatenate(outs, axis=1),), [qr[CHUNK - 8:, :], kr[CHUNK - 8:, :], vr[CHUNK - 8:, :],
                                             jnp.concatenate(new, axis=0)]


def _f_s5(tabs, consts, xs, xtabs, states):
    lam_re, lam_im, bblk, c_re, c_im, dskip, wglu, bglu = consts
    (u,) = xs
    s_re, s_im = states
    rows = u.shape[0]
    n = lam_re.shape[1]
    bu = _mm(u, bblk)
    hr, hi = bu[:, 0:n], bu[:, n:2 * n]
    row = _iota((rows, n), 0)
    h0r, h0i = s_re[0:1, :], s_im[0:1, :]
    hr = hr + jnp.where(row == 0, lam_re * h0r - lam_im * h0i, 0.0)
    hi = hi + jnp.where(row == 0, lam_re * h0i + lam_im * h0r, 0.0)
    pr, pi = lam_re, lam_im
    d = 1
    while d < rows:
        sr = jnp.where(row >= d, _shift(d, 0)(hr), 0.0)
        si = jnp.where(row >= d, _shift(d, 0)(hi), 0.0)
        hr, hi = hr + pr * sr - pi * si, hi + pr * si + pi * sr
        pr, pi = pr * pr - pi * pi, 2.0 * pr * pi
        d *= 2
    y = _mm(hr, c_re) - _mm(hi, c_im) + dskip * u
    y = jax.nn.gelu(y)
    out = y * jax.nn.sigmoid(_mm(y, wglu) + bglu)
    last_r = jnp.broadcast_to(hr[rows - 1:rows, :], (8, n))
    last_i = jnp.broadcast_to(hi[rows - 1:rows, :], (8, n))
    return (out,), [last_r, last_i]


def _full_spec(a):
    nd = a.ndim
    return pl.BlockSpec(a.shape, lambda i, _nd=nd: (0,) * _nd)


def _scan_fwd(name, f, rows, tabs, consts, xs, xtabs, state_shapes, y_total, y_width, y_cb, y_alias=None):
    seq = xs[0][0].shape[0]
    nc = seq // rows
    nt, ncst, nx, nxt, ns = len(tabs), len(consts), len(xs), len(xtabs), len(state_shapes)
    alias = y_alias is not None

    def body(*refs):
        p = 0
        tab_r = refs[p:p + nt]; p += nt
        c_r = refs[p:p + ncst]; p += ncst
        x_r = refs[p:p + nx]; p += nx
        xt_r = refs[p:p + nxt]; p += nxt
        if alias:
            p += 1
        y_ref = refs[p]; p += 1
        sv_r = refs[p:p + ns]; p += ns
        st_r = refs[p:p + ns]

        @pl.when(pl.program_id(0) == 0)
        def _():
            for s in st_r:
                s[...] = jnp.zeros(s.shape, F32)

        st = [s[...] for s in st_r]
        for r, v in zip(sv_r, st):
            r[...] = v
        (y,), new = f([r[...] for r in tab_r], [r[...] for r in c_r], [r[...] for r in x_r],
                      [r[...] for r in xt_r], st)
        y_ref[...] = y
        for s, v in zip(st_r, new):
            s[...] = v

    win = [pl.BlockSpec((rows, w), lambda i, _cb=cb: (i, _cb)) for (_, w, cb) in list(xs) + list(xtabs)]
    in_specs = [_full_spec(a) for a in list(tabs) + list(consts)] + win
    args = list(tabs) + list(consts) + [a for (a, _, _) in list(xs) + list(xtabs)]
    io_alias = {}
    if alias:
        in_specs.append(pl.BlockSpec(memory_space=pl.ANY))
        io_alias = {len(args): 0}
        args.append(y_alias)
    out_shape = [jax.ShapeDtypeStruct((seq, y_total), F32)]
    out_specs = [pl.BlockSpec((rows, y_width), lambda i: (i, y_cb))]
    for (r, c) in state_shapes:
        out_shape.append(jax.ShapeDtypeStruct((nc * r, c), F32))
        out_specs.append(pl.BlockSpec((r, c), lambda i: (i, 0)))
    res = pl.pallas_call(
        body, name=name, grid=(nc,), in_specs=in_specs, out_specs=out_specs, out_shape=out_shape,
        scratch_shapes=[pltpu.VMEM(s, F32) for s in state_shapes], input_output_aliases=io_alias,
        compiler_params=pltpu.CompilerParams(dimension_semantics=("arbitrary",), vmem_limit_bytes=VMEM_LIMIT),
    )(*args)
    return res[0], list(res[1:])


def _scan_bwd(name, f, rows, tabs, consts, xs, xtabs, saved, state_shapes, dy, dx_total, dx_width, dx_cb,
              assemble, dx_alias=None):
    seq = xs[0][0].shape[0]
    nc = seq // rows
    nt, ncst, nx, nxt, ns = len(tabs), len(consts), len(xs), len(xtabs), len(state_shapes)
    alias = dx_alias is not None

    def body(*refs):
        p = 0
        tab_r = refs[p:p + nt]; p += nt
        c_r = refs[p:p + ncst]; p += ncst
        x_r = refs[p:p + nx]; p += nx
        xt_r = refs[p:p + nxt]; p += nxt
        sv_r = refs[p:p + ns]; p += ns
        dy_ref = refs[p]; p += 1
        if alias:
            p += 1
        dx_ref = refs[p]; p += 1
        dc_r = refs[p:p + ncst]; p += ncst
        ds_r = refs[p:p + ns]

        @pl.when(pl.program_id(0) == 0)
        def _():
            for s in ds_r:
                s[...] = jnp.zeros(s.shape, F32)
            for r in dc_r:
                r[...] = jnp.zeros(r.shape, F32)

        tab_v = [r[...] for r in tab_r]
        xt_v = [r[...] for r in xt_r]

        def g(c, x, s):
            (y,), new = f(tab_v, c, x, xt_v, s)
            return y, new

        _, vjp = jax.vjp(g, [r[...] for r in c_r], [r[...] for r in x_r], [r[...] for r in sv_r])
        dc, dx, ds = vjp((dy_ref[...], [s[...] for s in ds_r]))
        dx_ref[...] = assemble(dx)
        for r, v in zip(dc_r, dc):
            r[...] += v
        for s, v in zip(ds_r, ds):
            s[...] = v

    win = [pl.BlockSpec((rows, w), lambda j, _cb=cb: (nc - 1 - j, _cb)) for (_, w, cb) in list(xs) + list(xtabs)]
    in_specs = [_full_spec(a) for a in list(tabs) + list(consts)] + win
    args = list(tabs) + list(consts) + [a for (a, _, _) in list(xs) + list(xtabs)]
    for (r, c), sv in zip(state_shapes, saved):
        in_specs.append(pl.BlockSpec((r, c), lambda j: (nc - 1 - j, 0)))
        args.append(sv)
    in_specs.append(pl.BlockSpec((rows, dy[1]), lambda j: (nc - 1 - j, dy[2])))
    args.append(dy[0])
    io_alias = {}
    if alias:
        in_specs.append(pl.BlockSpec(memory_space=pl.ANY))
        io_alias = {len(args): 0}
        args.append(dx_alias)
    out_shape = [jax.ShapeDtypeStruct((seq, dx_total), F32)] + [jax.ShapeDtypeStruct(a.shape, F32) for a in consts]
    out_specs = [pl.BlockSpec((rows, dx_width), lambda j: (nc - 1 - j, dx_cb))] + [_full_spec(a) for a in consts]
    res = pl.pallas_call(
        body, name=name, grid=(nc,), in_specs=in_specs, out_specs=out_specs, out_shape=out_shape,
        scratch_shapes=[pltpu.VMEM(s, F32) for s in state_shapes], input_output_aliases=io_alias,
        compiler_params=pltpu.CompilerParams(dimension_semantics=("arbitrary",), vmem_limit_bytes=VMEM_LIMIT),
    )(*args)
    return res[0], list(res[1:])


def _tile(n, want):
    t = min(n, want)
    while n % t:
        t //= 2
    return t


def _matmul(name, a, b, mode, out_dtype=F32, a_pro=None, epi=None, epi_arr=None, tm=512, tn=1024, tk=1024):
    if mode == "nn":
        (m, k), (k2, n) = a.shape, b.shape
    elif mode == "nt":
        (m, k), (n, k2) = a.shape, b.shape
    else:
        (k, m), (k2, n) = a.shape, b.shape
    assert k == k2, (name, a.shape, b.shape)
    tm, tn, tk = _tile(m, tm), _tile(n, tn), _tile(k, tk)
    nk = k // tk
    ca, cb = {"nn": (1, 0), "nt": (1, 1), "tn": (0, 0)}[mode]

    def body(*refs):
        if epi is None:
            a_ref, b_ref, o_ref, acc = refs
            e_ref = None
        else:
            a_ref, b_ref, e_ref, o_ref, acc = refs
        kk = pl.program_id(2)

        @pl.when(kk == 0)
        def _():
            acc[...] = jnp.zeros(acc.shape, F32)

        av = a_ref[...]
        if a_pro == "relu2":
            r = jnp.maximum(av, 0.0)
            av = r * r
        acc[...] += _dg(_lo(av), _lo(b_ref[...]), ca, cb)

        @pl.when(kk == nk - 1)
        def _():
            r = acc[...]
            if epi == "add":
                r = r + e_ref[...]
            elif epi == "drelu2":
                r = r * (2.0 * jnp.maximum(e_ref[...], 0.0))
            o_ref[...] = r.astype(out_dtype)

    if mode == "tn":
        a_spec = pl.BlockSpec((tk, tm), lambda j, i, kk: (kk, i))
    else:
        a_spec = pl.BlockSpec((tm, tk), lambda j, i, kk: (i, kk))
    if mode == "nt":
        b_spec = pl.BlockSpec((tn, tk), lambda j, i, kk: (j, kk))
    else:
        b_spec = pl.BlockSpec((tk, tn), lambda j, i, kk: (kk, j))
    o_spec = pl.BlockSpec((tm, tn), lambda j, i, kk: (i, j))
    in_specs, args = [a_spec, b_spec], [a, b]
    if epi is not None:
        in_specs.append(o_spec)
        args.append(epi_arr)
    return pl.pallas_call(
        body, name=name, grid=(n // tn, m // tm, nk), in_specs=in_specs, out_specs=o_spec,
        out_shape=jax.ShapeDtypeStruct((m, n), out_dtype), scratch_shapes=[pltpu.VMEM((tm, tn), F32)],
        compiler_params=pltpu.CompilerParams(dimension_semantics=("parallel", "parallel", "arbitrary"),
                                             vmem_limit_bytes=VMEM_LIMIT),
    )(*args)


ROW_TILE = 512


def _rmsnorm_fwd(name, x, w):
    seq, d = x.shape
    tr = _tile(seq, ROW_TILE)

    def body(x_ref, w_ref, o_ref):
        xv = x_ref[...]
        o_ref[...] = (_unit_rms(xv) * w_ref[...]).astype(_MXU_DTYPE)

    return pl.pallas_call(
        body, name=name, grid=(seq // tr,),
        in_specs=[pl.BlockSpec((tr, d), lambda i: (i, 0)), pl.BlockSpec((1, d), lambda i: (0, 0))],
        out_specs=pl.BlockSpec((tr, d), lambda i: (i, 0)), out_shape=jax.ShapeDtypeStruct((seq, d), _MXU_DTYPE),
        compiler_params=pltpu.CompilerParams(dimension_semantics=("parallel",), vmem_limit_bytes=VMEM_LIMIT),
    )(x, w)


def _rmsnorm_bwd(name, dh, x, w, dres):
    seq, d = x.shape
    tr = _tile(seq, ROW_TILE)

    def body(dh_ref, x_ref, w_ref, dres_ref, dx_ref, dw_ref):
        @pl.when(pl.program_id(0) == 0)
        def _():
            dw_ref[...] = jnp.zeros(dw_ref.shape, F32)

        xv = x_ref[...]
        rstd = lax.rsqrt(jnp.mean(xv * xv, axis=-1, keepdims=True) + EPS)
        xh = xv * rstd
        dhv = dh_ref[...]
        g = dhv * w_ref[...]
        dx_ref[...] = dres_ref[...] + rstd * (g - xh * jnp.mean(g * xh, axis=-1, keepdims=True))
        dw_ref[...] += jnp.sum(dhv * xh, axis=0, keepdims=True)

    row = pl.BlockSpec((tr, d), lambda i: (i, 0))
    vec = pl.BlockSpec((1, d), lambda i: (0, 0))
    return pl.pallas_call(
        body, name=name, grid=(seq // tr,), in_specs=[row, row, vec, row], out_specs=[row, vec],
        out_shape=[jax.ShapeDtypeStruct((seq, d), F32), jax.ShapeDtypeStruct((1, d), F32)],
        compiler_params=pltpu.CompilerParams(dimension_semantics=("arbitrary",), vmem_limit_bytes=VMEM_LIMIT),
    )(dh, x, w, dres)


def _loss_head(name, x, w, target):
    seq, d = x.shape
    tr = _tile(seq, ROW_TILE)

    def body(x_ref, w_ref, t_ref, loss_ref, dx_ref, dw_ref):
        @pl.when(pl.program_id(0) == 0)
        def _():
            dw_ref[...] = jnp.zeros(dw_ref.shape, F32)
            loss_ref[...] = jnp.zeros(loss_ref.shape, F32)

        xv = x_ref[...]
        rstd = lax.rsqrt(jnp.mean(xv * xv, axis=-1, keepdims=True) + EPS)
        xh = xv * rstd
        err = xh * w_ref[...] - t_ref[...]
        per_row = jnp.mean(err * err, axis=-1, keepdims=True)
        loss_ref[...] += 0.5 * jnp.sum(per_row, axis=0, keepdims=True)
        dy = err * (1.0 / d)
        g = dy * w_ref[...]
        dx_ref[...] = rstd * (g - xh * jnp.mean(g * xh, axis=-1, keepdims=True))
        dw_ref[...] += jnp.sum(dy * xh, axis=0, keepdims=True)

    row = pl.BlockSpec((tr, d), lambda i: (i, 0))
    vec = pl.BlockSpec((1, d), lambda i: (0, 0))
    one = pl.BlockSpec((1, 1), lambda i: (0, 0))
    return pl.pallas_call(
        body, name=name, grid=(seq // tr,), in_specs=[row, vec, row], out_specs=[one, row, vec],
        out_shape=[jax.ShapeDtypeStruct((1, 1), F32), jax.ShapeDtypeStruct((seq, d), F32),
                   jax.ShapeDtypeStruct((1, d), F32)],
        compiler_params=pltpu.CompilerParams(dimension_semantics=("arbitrary",), vmem_limit_bytes=VMEM_LIMIT),
    )(x, w, target)


def _adamw(name, w, g, m, v):
    rows = w.shape[0]
    tr = _tile(rows, 256)

    def body(w_ref, g_ref, m_ref, v_ref, d_ref, nm_ref, nv_ref):
        gv = g_ref[...]
        nm = ADAM_B1 * m_ref[...] + (1.0 - ADAM_B1) * gv
        nv = ADAM_B2 * v_ref[...] + (1.0 - ADAM_B2) * (gv * gv)
        m_hat = nm / (1.0 - ADAM_B1 ** ADAM_STEP)
        v_hat = nv / (1.0 - ADAM_B2 ** ADAM_STEP)
        d_ref[...] = -ADAM_LR * (m_hat / (jnp.sqrt(v_hat) + ADAM_EPS) + ADAM_WD * w_ref[...])
        nm_ref[...] = nm
        nv_ref[...] = nv

    spec = pl.BlockSpec((tr, LANES), lambda i: (i, 0))
    sds = jax.ShapeDtypeStruct(w.shape, F32)
    return pl.pallas_call(
        body, name=name, grid=(rows // tr,), in_specs=[spec] * 4, out_specs=[spec] * 3, out_shape=[sds] * 3,
        compiler_params=pltpu.CompilerParams(dimension_semantics=("parallel",), vmem_limit_bytes=VMEM_LIMIT),
    )(w, g, m, v)


def _add_halves(name, g, t1, c):
    nsec, rows, _ = g.shape
    rh = rows // 2
    tr = _tile(rh, 256)
    nb = rh // tr

    def body(c_ref, g_ref, t_ref, o_ref):
        o_ref[...] = g_ref[...] + t_ref[...]

    gs = pltpu.PrefetchScalarGridSpec(
        num_scalar_prefetch=1, grid=(nsec, nb),
        in_specs=[pl.BlockSpec((1, tr, LANES), lambda s, i, c_ref: (s, c_ref[0] * nb + i, 0)),
                  pl.BlockSpec((1, tr, LANES), lambda s, i, c_ref: (s, i, 0))],
        out_specs=pl.BlockSpec((1, tr, LANES), lambda s, i, c_ref: (s, i, 0)))
    return pl.pallas_call(
        body, name=name, grid_spec=gs, out_shape=jax.ShapeDtypeStruct((nsec, rh, LANES), F32),
        compiler_params=pltpu.CompilerParams(dimension_semantics=("parallel", "parallel"),
                                             vmem_limit_bytes=VMEM_LIMIT),
    )(c, g, t1)


def _add_four(name, t2):
    _, rh, _ = t2.shape
    tr = _tile(rh, 256)

    def body(t_ref, o_ref):
        o_ref[...] = ((t_ref[0] + t_ref[1]) + t_ref[2]) + t_ref[3]

    return pl.pallas_call(
        body, name=name, grid=(rh // tr,), in_specs=[pl.BlockSpec((4, tr, LANES), lambda i: (0, i, 0))],
        out_specs=pl.BlockSpec((tr, LANES), lambda i: (i, 0)), out_shape=jax.ShapeDtypeStruct((rh, LANES), F32),
        compiler_params=pltpu.CompilerParams(dimension_semantics=("parallel",), vmem_limit_bytes=VMEM_LIMIT),
    )(t2)


ANY = pl.BlockSpec(memory_space=pl.ANY)


def _place():
    return lax.axis_index("x"), lax.axis_index("y"), lax.axis_index("c")


def _all_gather_shards(name, slab):
    rows = slab.shape[0]
    rh = rows // 2

    def body(x_ref, out_ref, send_sems, recv_sems, local_sem):
        x, y, c = _place()
        sibling = (x, y, 1 - c)
        chips = [(1 - x, y), (x, 1 - y), (1 - x, 1 - y)]

        def part(px, py, pc):
            return out_ref.at[2 * px + py, pl.ds(pc * rh, rh), :]

        def copy(k, block, to, src=None):
            return pltpu.make_async_remote_copy(
                src_ref=part(*block) if src is None else src, dst_ref=part(*block),
                send_sem=send_sems.at[k], recv_sem=recv_sems.at[k], device_id=to, device_id_type=MESH)

        mine = pltpu.make_async_copy(x_ref, out_ref.at[2 * x + y], local_sem)
        mine.start()
        my_half = x_ref.at[pl.ds(c * rh, rh), :]
        first = [copy(j, (x, y, c), (*chip, c), src=my_half) for j, chip in enumerate(chips)]
        for cp in first:
            cp.start()
        passed = [copy(3 + j, (*chip, c), sibling) for j, chip in enumerate(chips)]
        for j, chip in enumerate(chips):
            copy(j, (*chip, c), (x, y, c)).wait_recv()
            passed[j].start()
        for j, chip in enumerate(chips):
            copy(3 + j, (*chip, 1 - c), (x, y, c)).wait_recv()
        for cp in first + passed:
            cp.wait_send()
        mine.wait()

    return pl.pallas_call(
        body, name=name, in_specs=[ANY], out_specs=ANY,
        out_shape=jax.ShapeDtypeStruct((4, rows, LANES), slab.dtype),
        scratch_shapes=[pltpu.SemaphoreType.DMA((6,)), pltpu.SemaphoreType.DMA((6,)), pltpu.SemaphoreType.DMA],
    )(slab)


def _swap_halves(name, g):
    nsec, rows, _ = g.shape
    rh = rows // 2

    def body(g_ref, t_ref, send_sem, recv_sem):
        x, y, c = _place()
        cp = pltpu.make_async_remote_copy(
            src_ref=g_ref.at[:, pl.ds((1 - c) * rh, rh), :], dst_ref=t_ref, send_sem=send_sem, recv_sem=recv_sem,
            device_id=(x, y, 1 - c), device_id_type=MESH)
        cp.start()
        cp.wait()

    return pl.pallas_call(
        body, name=name, in_specs=[ANY], out_specs=ANY, out_shape=jax.ShapeDtypeStruct((nsec, rh, LANES), F32),
        scratch_shapes=[pltpu.SemaphoreType.DMA, pltpu.SemaphoreType.DMA],
    )(g)


def _scatter_to_owners(name, p):
    _, rh, _ = p.shape

    def body(p_ref, t_ref, send_sems, recv_sems, local_sem):
        x, y, c = _place()
        me = 2 * x + y
        chips = [(1 - x, y), (x, 1 - y), (1 - x, 1 - y)]
        mine = pltpu.make_async_copy(p_ref.at[me], t_ref.at[me], local_sem)
        mine.start()
        sends = []
        for j, (cx, cy) in enumerate(chips):
            sends.append(pltpu.make_async_remote_copy(
                src_ref=p_ref.at[2 * cx + cy], dst_ref=t_ref.at[me], send_sem=send_sems.at[j],
                recv_sem=recv_sems.at[j], device_id=(cx, cy, c), device_id_type=MESH))
        for cp in sends:
            cp.start()
        for j, (cx, cy) in enumerate(chips):
            pltpu.make_async_remote_copy(
                src_ref=p_ref.at[me], dst_ref=t_ref.at[2 * cx + cy], send_sem=send_sems.at[j],
                recv_sem=recv_sems.at[j], device_id=(cx, cy, c), device_id_type=MESH).wait_recv()
        for cp in sends:
            cp.wait_send()
        mine.wait()

    return pl.pallas_call(
        body, name=name, in_specs=[ANY], out_specs=ANY, out_shape=jax.ShapeDtypeStruct((4, rh, LANES), F32),
        scratch_shapes=[pltpu.SemaphoreType.DMA((3,)), pltpu.SemaphoreType.DMA((3,)), pltpu.SemaphoreType.DMA],
    )(p)


def _join_halves(name, r_half):
    rh = r_half.shape[0]

    def body(h_ref, o_ref, send_sem, recv_sem, local_sem):
        x, y, c = _place()
        mine = pltpu.make_async_copy(h_ref, o_ref.at[pl.ds(c * rh, rh), :], local_sem)
        mine.start()
        cp = pltpu.make_async_remote_copy(
            src_ref=h_ref, dst_ref=o_ref.at[pl.ds(c * rh, rh), :], send_sem=send_sem, recv_sem=recv_sem,
            device_id=(x, y, 1 - c), device_id_type=MESH)
        cp.start()
        pltpu.make_async_remote_copy(
            src_ref=h_ref, dst_ref=o_ref.at[pl.ds((1 - c) * rh, rh), :], send_sem=send_sem, recv_sem=recv_sem,
            device_id=(x, y, 1 - c), device_id_type=MESH).wait_recv()
        cp.wait_send()
        mine.wait()

    return pl.pallas_call(
        body, name=name, in_specs=[ANY], out_specs=ANY, out_shape=jax.ShapeDtypeStruct((2 * rh, LANES), F32),
        scratch_shapes=[pltpu.SemaphoreType.DMA, pltpu.SemaphoreType.DMA, pltpu.SemaphoreType.DMA],
    )(r_half)


def _rows_of(n):
    return -(-n // LANES)


def _pack(arrays, dtype, align=32):
    parts = []
    for a in arrays:
        flat = a.reshape(-1).astype(dtype)
        parts.append(jnp.pad(flat, (0, _rows_of(flat.size) * LANES - flat.size)))
    flat = jnp.concatenate(parts)
    rows = flat.size // LANES
    rows_pad = -(-rows // align) * align
    return jnp.pad(flat, (0, (rows_pad - rows) * LANES)).reshape(rows_pad, LANES)


def _unpack(slab, shapes):
    out, r = [], 0
    for shp in shapes:
        n = math.prod(shp)
        out.append(slab[r:r + _rows_of(n)].reshape(-1)[:n].reshape(shp))
        r += _rows_of(n)
    return out


def _unpack_gathered(g, shapes, kinds):
    out, r = [], 0
    for shp, kind in zip(shapes, kinds):
        n = math.prod(shp)
        blk = g[:, r:r + _rows_of(n)].reshape(4, -1)[:, :n].reshape((4,) + tuple(shp))
        r += _rows_of(n)
        if kind == "col":
            out.append(jnp.moveaxis(blk, 0, 1).reshape(shp[0], 4 * shp[1]))
        else:
            out.append(blk.reshape(4 * shp[0], shp[1]))
    return out


def _shard_block(g, kind, s, local_shape):
    if kind == "col":
        return g[:, s * local_shape[1]:(s + 1) * local_shape[1]]
    if kind == "row":
        return g[s * local_shape[0]:(s + 1) * local_shape[0]]
    return g


def _rotary_tables(seq):
    half = RET_DK // 2
    pos = jnp.arange(seq, dtype=F32)
    inv = ROPE_THETA ** (-jnp.arange(half, dtype=F32) / half)
    ang = pos[:, None] * inv[None, :]
    cos, sin = jnp.cos(ang), jnp.sin(ang)
    return jnp.concatenate([cos, cos], axis=1), jnp.concatenate([-sin, sin], axis=1)


def _retention_tables():
    log_gamma = jnp.log(1.0 - 2.0 ** (-5.0 - jnp.arange(RET_HEADS, dtype=F32)))
    idx = jnp.arange(CHUNK, dtype=F32)
    diff = idx[:, None] - idx[None, :]
    dmask = jnp.exp(jnp.where((diff >= 0)[None], log_gamma[:, None, None] * diff[None], -jnp.inf))
    kdec = jnp.exp(log_gamma[None, :] * (CHUNK - 1.0 - idx)[:, None])
    qdec = jnp.exp(log_gamma[None, :] * (idx + 1.0)[:, None])
    cdec = jnp.exp(log_gamma * CHUNK)[None, :]
    lanes = lambda t: jnp.repeat(t, RET_DK, axis=1)
    return dmask.reshape(RET_HEADS * CHUNK, CHUNK), lanes(kdec), lanes(qdec), lanes(cdec)


def _s5_prep(a_re, a_im, log_step, b_re, b_im, c_re, c_im):
    g, n, c = S5_GROUPS, S5_STATE, S5_GROUP
    lam = lax.complex(a_re, a_im)
    step = jnp.exp(log_step)[:, None]
    lam_bar = jnp.exp(lam * step)
    b_bar = ((lam_bar - 1.0) / lam)[..., None] * lax.complex(b_re, b_im)
    eye = jnp.eye(g, dtype=F32)
    bb_re = (jnp.real(b_bar).transpose(0, 2, 1)[:, :, None, :] * eye[:, None, :, None]).reshape(g * c, g * n)
    bb_im = (jnp.imag(b_bar).transpose(0, 2, 1)[:, :, None, :] * eye[:, None, :, None]).reshape(g * c, g * n)
    cc_re = (c_re.transpose(0, 2, 1)[:, :, None, :] * eye[:, None, :, None]).reshape(g * n, g * c)
    cc_im = (c_im.transpose(0, 2, 1)[:, :, None, :] * eye[:, None, :, None]).reshape(g * n, g * c)
    return (jnp.real(lam_bar).reshape(1, g * n), jnp.imag(lam_bar).reshape(1, g * n),
            jnp.concatenate([bb_re, bb_im], axis=1), cc_re, cc_im)


def kernel(x, l0_norm_mix, l0_w_in, ssd_conv_w, ssd_conv_b, ssd_dt_bias, ssd_A_log, ssd_D, ssd_norm_w, l0_w_out, l0_norm_mlp, l0_w_up, l0_w_down, l1_norm_mix, l1_w_in, gdn_conv_w, gdn_A_log, gdn_dt_bias, gdn_norm_w, s5_A_re, s5_A_im, s5_log_step, s5_B_re, s5_B_im, s5_C_re, s5_C_im, s5_D, s5_w_glu, s5_b_glu, l1_w_out, l1_norm_mlp, l1_w_up, l1_w_down, final_norm, loss_target, m_l0_norm_mix, m_l0_w_in, m_ssd_conv_w, m_ssd_conv_b, m_ssd_dt_bias, m_ssd_A_log, m_ssd_D, m_ssd_norm_w, m_l0_w_out, m_l0_norm_mlp, m_l0_w_up, m_l0_w_down, m_l1_norm_mix, m_l1_w_in, m_gdn_conv_w, m_gdn_A_log, m_gdn_dt_bias, m_gdn_norm_w, m_s5_A_re, m_s5_A_im, m_s5_log_step, m_s5_B_re, m_s5_B_im, m_s5_C_re, m_s5_C_im, m_s5_D, m_s5_w_glu, m_s5_b_glu, m_l1_w_out, m_l1_norm_mlp, m_l1_w_up, m_l1_w_down, m_final_norm, v_l0_norm_mix, v_l0_w_in, v_ssd_conv_w, v_ssd_conv_b, v_ssd_dt_bias, v_ssd_A_log, v_ssd_D, v_ssd_norm_w, v_l0_w_out, v_l0_norm_mlp, v_l0_w_up, v_l0_w_down, v_l1_norm_mix, v_l1_w_in, v_gdn_conv_w, v_gdn_A_log, v_gdn_dt_bias, v_gdn_norm_w, v_s5_A_re, v_s5_A_im, v_s5_log_step, v_s5_B_re, v_s5_B_im, v_s5_C_re, v_s5_C_im, v_s5_D, v_s5_w_glu, v_s5_b_glu, v_l1_w_out, v_l1_norm_mlp, v_l1_w_up, v_l1_w_down, v_final_norm):
    given = dict(locals())
    names = [n for n, _ in PARAMS]
    kinds = dict(PARAMS)
    w = {n: given[n] for n in names}
    seq = x.shape[1]
    x0 = x.reshape(seq, D_MODEL)
    target = loss_target.reshape(seq, D_MODEL)

    gb = _all_gather_shards("gather_weights", _pack([w[n] for n in GATHER_BF16], _MXU_DTYPE))
    full = dict(zip(GATHER_BF16, _unpack_gathered(gb, [w[n].shape for n in GATHER_BF16],
                                                  [kinds[n] for n in GATHER_BF16])))
    gf = _all_gather_shards("gather_conv", _pack([w[n] for n in GATHER_F32], F32))
    full.update(zip(GATHER_F32, _unpack_gathered(gf, [w[n].shape for n in GATHER_F32],
                                                 [kinds[n] for n in GATHER_F32])))
    in0 = full["l0_w_in"].shape[1]
    w_in0 = jnp.pad(full["l0_w_in"], ((0, 0), (0, IN0_PAD - in0)))
    wi1 = full["l1_w_in"]
    in1 = wi1.shape[1]
    w_in1 = jnp.concatenate([wi1[:, :3072], wi1[:, 3084:in1], wi1[:, 3072:3084],
                             jnp.zeros((D_MODEL, IN1_PAD - in1), wi1.dtype)], axis=1)

    row = lambda a: a.reshape(1, -1)
    lanes64 = lambda a: jnp.repeat(a, SSD_HEAD_DIM).reshape(1, -1)

    h0 = _rmsnorm_fwd("norm_mix0", x0, row(w["l0_norm_mix"]))
    proj0 = _matmul("in_proj0", h0, w_in0, "nn")
    cos_t, sin_t = _rotary_tables(seq)
    ret_tabs = list(_retention_tables())
    ret_xs = [(proj0, 512, 0), (proj0, 512, 1), (proj0, 512, 2), (proj0, 512, 3)]
    ret_xt = [(cos_t, 128, 0), (sin_t, 128, 0)]
    ret_states = [(512, 128)]
    mixed0, ret_saved = _scan_fwd("ret_fwd", _f_ret, CHUNK, ret_tabs, [], ret_xs, ret_xt, ret_states, D_MODEL, 512, 0)
    expand = jnp.repeat(jnp.eye(128, SSD_HEADS, dtype=F32), SSD_HEAD_DIM, axis=1)
    ssd_consts = [full["ssd_conv_w"], row(w["ssd_conv_b"]), lanes64(w["ssd_dt_bias"]), lanes64(w["ssd_A_log"]),
                  lanes64(w["ssd_D"]), row(w["ssd_norm_w"])]
    ssd_xs = [(proj0, 512, 4), (proj0, 512, 5), (proj0, 256, 12), (proj0, 256, 13), (proj0, 128, 28)]
    ssd_states = [(8, 512), (8, 256), (8, 256), (512, 128)]
    mixed0, ssd_saved = _scan_fwd("ssd_fwd", _f_ssd, CHUNK, [expand], ssd_consts, ssd_xs, [], ssd_states,
                                  D_MODEL, 512, 1, y_alias=mixed0)
    x1 = _matmul("out_proj0", mixed0, full["l0_w_out"], "nn", epi="add", epi_arr=x0)
    h1 = _rmsnorm_fwd("norm_mlp0", x1, row(w["l0_norm_mlp"]))
    u0 = _matmul("up0", h1, full["l0_w_up"], "nn")
    x2 = _matmul("down0", u0, full["l0_w_down"], "nn", a_pro="relu2", epi="add", epi_arr=x1)

    h2 = _rmsnorm_fwd("norm_mix1", x2, row(w["l1_norm_mix"]))
    proj1 = _matmul("in_proj1", h2, w_in1, "nn")
    p_alog = jnp.zeros((1, 128), F32).at[0, 6:12].set(w["gdn_A_log"])
    p_dtb = jnp.zeros((1, 128), F32).at[0, 6:12].set(w["gdn_dt_bias"])
    gdn_consts = [full["gdn_conv_w"], p_alog, p_dtb, row(w["gdn_norm_w"])]
    gdn_xs = [(proj1, 768, 0), (proj1, 768, 1), (proj1, 768, 2), (proj1, 768, 3), (proj1, 128, 26)]
    gdn_states = [(8, 768), (8, 768), (8, 768), (768, 256)]
    mixed1, gdn_saved = _scan_fwd("gdn_fwd", _f_gdn, CHUNK, [], gdn_consts, gdn_xs, [], gdn_states, D_MODEL, 768, 0)
    s5_args = (w["s5_A_re"], w["s5_A_im"], w["s5_log_step"], w["s5_B_re"], w["s5_B_im"], w["s5_C_re"], w["s5_C_im"])
    (lam_re, lam_im, bblk, cc_re, cc_im), s5_prep_vjp = jax.vjp(_s5_prep, *s5_args)
    s5_consts = [lam_re, lam_im, bblk, cc_re, cc_im, row(w["s5_D"]), full["s5_w_glu"].astype(F32), row(w["s5_b_glu"])]
    s5_xs = [(proj1, 256, 12)]
    s5_states = [(8, 1024), (8, 1024)]
    mixed1, s5_saved = _scan_fwd("s5_fwd", _f_s5, CHUNK, [], s5_consts, s5_xs, [], s5_states, D_MODEL, 256, 3,
                                 y_alias=mixed1)
    x3 = _matmul("out_proj1", mixed1, full["l1_w_out"], "nn", epi="add", epi_arr=x2)
    h3 = _rmsnorm_fwd("norm_mlp1", x3, row(w["l1_norm_mlp"]))
    u1 = _matmul("up1", h3, full["l1_w_up"], "nn")
    x4 = _matmul("down1", u1, full["l1_w_down"], "nn", a_pro="relu2", epi="add", epi_arr=x3)

    loss_part, dx4, d_final = _loss_head("loss_head", x4, row(w["final_norm"]), target)
    loss = lax.psum(loss_part[0, 0], ("x", "y", "c"))
    grads = {"final_norm": d_final.reshape(-1)}

    du1 = _matmul("down1_dx", dx4, full["l1_w_down"], "nt", out_dtype=_MXU_DTYPE, epi="drelu2", epi_arr=u1)
    grads["l1_w_down"] = _matmul("down1_dw", u1, dx4, "tn", a_pro="relu2", tm=512, tn=1024, tk=512)
    grads["l1_w_up"] = _matmul("up1_dw", h3, du1, "tn", tm=512, tn=1024, tk=512)
    dh3 = _matmul("up1_dx", du1, full["l1_w_up"], "nt")
    dx3, dwn = _rmsnorm_bwd("norm_mlp1_bwd", dh3, x3, row(w["l1_norm_mlp"]), dx4)
    grads["l1_norm_mlp"] = dwn.reshape(-1)
    grads["l1_w_out"] = _matmul("out_proj1_dw", mixed1, dx3, "tn", tm=512, tn=1024, tk=512)
    dmixed1 = _matmul("out_proj1_dx", dx3, full["l1_w_out"], "nt")

    def gdn_assemble(dx):
        dq, dk, dv, dz, dba = dx
        zeros = lambda n: jnp.zeros((CHUNK, n), F32)
        return jnp.concatenate([dq, dk, dv, dz, zeros(256), dba, zeros(IN1_PAD - 3456)], axis=1)

    dproj1, gdn_dc = _scan_bwd("gdn_bwd", _f_gdn, CHUNK, [], gdn_consts, gdn_xs, [], gdn_saved, gdn_states,
                               (dmixed1, 768, 0), IN1_PAD, IN1_PAD, 0, gdn_assemble)
    dproj1, s5_dc = _scan_bwd("s5_bwd", _f_s5, CHUNK, [], s5_consts, s5_xs, [], s5_saved, s5_states,
                              (dmixed1, 256, 3), IN1_PAD, 256, 12, lambda dx: dx[0], dx_alias=dproj1)
    grads["gdn_conv_w"] = gdn_dc[0]
    grads["gdn_A_log"] = gdn_dc[1][0, 6:12]
    grads["gdn_dt_bias"] = gdn_dc[2][0, 6:12]
    grads["gdn_norm_w"] = gdn_dc[3].reshape(-1)
    s5_pg = s5_prep_vjp(tuple(s5_dc[:5]))
    for n, gval in zip(("s5_A_re", "s5_A_im", "s5_log_step", "s5_B_re", "s5_B_im", "s5_C_re", "s5_C_im"), s5_pg):
        grads[n] = gval
    grads["s5_D"] = s5_dc[5].reshape(-1)
    grads["s5_w_glu"] = s5_dc[6]
    grads["s5_b_glu"] = s5_dc[7].reshape(-1)
    dwi1 = _matmul("in_proj1_dw", h2, dproj1, "tn", tm=512, tn=512, tk=512)
    grads["l1_w_in"] = jnp.concatenate([dwi1[:, :3072], dwi1[:, 3328:3340], dwi1[:, 3072:3328]], axis=1)
    dh2 = _matmul("in_proj1_dx", dproj1, w_in1, "nt")
    dx2, dwn = _rmsnorm_bwd("norm_mix1_bwd", dh2, x2, row(w["l1_norm_mix"]), dx3)
    grads["l1_norm_mix"] = dwn.reshape(-1)

    du0 = _matmul("down0_dx", dx2, full["l0_w_down"], "nt", out_dtype=_MXU_DTYPE, epi="drelu2", epi_arr=u0)
    grads["l0_w_down"] = _matmul("down0_dw", u0, dx2, "tn", a_pro="relu2", tm=512, tn=1024, tk=512)
    grads["l0_w_up"] = _matmul("up0_dw", h1, du0, "tn", tm=512, tn=1024, tk=512)
    dh1 = _matmul("up0_dx", du0, full["l0_w_up"], "nt")
    dx1, dwn = _rmsnorm_bwd("norm_mlp0_bwd", dh1, x1, row(w["l0_norm_mlp"]), dx2)
    grads["l0_norm_mlp"] = dwn.reshape(-1)
    grads["l0_w_out"] = _matmul("out_proj0_dw", mixed0, dx1, "tn", tm=512, tn=1024, tk=512)
    dmixed0 = _matmul("out_proj0_dx", dx1, full["l0_w_out"], "nt")
    dproj0, _ = _scan_bwd("ret_bwd", _f_ret, CHUNK, ret_tabs, [], ret_xs, ret_xt, ret_saved, ret_states,
                          (dmixed0, 512, 0), IN0_PAD, 2048, 0, lambda dx: jnp.concatenate(dx, axis=1))

    def ssd_assemble(dx):
        return jnp.concatenate(list(dx) + [jnp.zeros((CHUNK, 2048 - 1664), F32)], axis=1)

    dproj0, ssd_dc = _scan_bwd("ssd_bwd", _f_ssd, CHUNK, [expand], ssd_consts, ssd_xs, [], ssd_saved, ssd_states,
                               (dmixed0, 512, 1), IN0_PAD, 2048, 1, ssd_assemble, dx_alias=dproj0)
    heads = lambda a: a.reshape(SSD_HEADS, SSD_HEAD_DIM).sum(axis=1)
    grads["ssd_conv_w"] = ssd_dc[0]
    grads["ssd_conv_b"] = ssd_dc[1].reshape(-1)
    grads["ssd_dt_bias"] = heads(ssd_dc[2])
    grads["ssd_A_log"] = heads(ssd_dc[3])
    grads["ssd_D"] = heads(ssd_dc[4])
    grads["ssd_norm_w"] = ssd_dc[5].reshape(-1)
    grads["l0_w_in"] = _matmul("in_proj0_dw", h0, dproj0, "tn", tm=512, tn=1024, tk=512)[:, :in0]
    dh0 = _matmul("in_proj0_dx", dproj0, w_in0, "nt")
    dx0, dwn = _rmsnorm_bwd("norm_mix0_bwd", dh0, x0, row(w["l0_norm_mix"]), dx1)
    grads["l0_norm_mix"] = dwn.reshape(-1)
    grad_x = dx0.reshape(x.shape)

    c_idx = lax.axis_index("c").astype(jnp.int32).reshape(1)
    sections = []
    for s in range(4):
        sections.append(_pack([_shard_block(grads[n].reshape(_full_shape(n, w, kinds)), kinds[n], s, w[n].shape)
                               for n in names], F32))
    gslab = jnp.stack(sections)
    from_sibling = _swap_halves("grads_swap_halves", gslab)
    chip_sum = _add_halves("grads_add_sibling", gslab, from_sibling, c_idx)
    partials = _scatter_to_owners("grads_scatter", chip_sum)
    my_half = _add_four("grads_add_chips", partials)
    gsum = _join_halves("grads_join_halves", my_half)

    shapes = [w[n].shape for n in names]
    delta, new_m, new_v = _adamw("adamw", _pack([w[n] for n in names], F32), gsum,
                                 _pack([given["m_" + n] for n in names], F32),
                                 _pack([given["v_" + n] for n in names], F32))
    return (loss, grad_x, *_unpack(gsum, shapes), *_unpack(delta, shapes), *_unpack(new_m, shapes),
            *_unpack(new_v, shapes))


def _full_shape(name, w, kinds):
    shp = w[name].shape
    if kinds[name] == "col":
        return (shp[0], 4 * shp[1])
    if kinds[name] == "row":
        return (4 * shp[0],) + tuple(shp[1:])
    return shp
```

```python
import functools
import math

import jax
import jax.numpy as jnp
from jax import lax
from jax.experimental import pallas as pl
from jax.experimental.pallas import tpu as pltpu

F32 = jnp.float32
_MXU_DTYPE = jnp.bfloat16

D_MODEL = 1024
CHUNK = 64
EPS = 1e-6
RET_HEADS, RET_DK = 4, 128
ROPE_THETA = 10000.0
SSD_HEADS, SSD_HEAD_DIM = 8, 64
GDN_HEADS, GDN_DK = 6, 128
S5_GROUPS, S5_GROUP, S5_STATE = 16, 16, 64
ADAM_LR, ADAM_B1, ADAM_B2, ADAM_EPS, ADAM_WD, ADAM_STEP = 0.001, 0.9, 0.999, 1e-08, 0.01, 10

IN0_PAD = 4096
IN1_PAD = 3584
LANES = 1024
VMEM_LIMIT = 56 * 1024 * 1024
MESH = pl.DeviceIdType.MESH

PARAMS = (
    ("l0_norm_mix", "rep"), ("l0_w_in", "col"), ("ssd_conv_w", "col"), ("ssd_conv_b", "rep"),
    ("ssd_dt_bias", "rep"), ("ssd_A_log", "rep"), ("ssd_D", "rep"), ("ssd_norm_w", "rep"),
    ("l0_w_out", "row"), ("l0_norm_mlp", "rep"), ("l0_w_up", "col"), ("l0_w_down", "row"),
    ("l1_norm_mix", "rep"), ("l1_w_in", "col"), ("gdn_conv_w", "col"), ("gdn_A_log", "rep"),
    ("gdn_dt_bias", "rep"), ("gdn_norm_w", "rep"), ("s5_A_re", "rep"), ("s5_A_im", "rep"),
    ("s5_log_step", "rep"), ("s5_B_re", "rep"), ("s5_B_im", "rep"), ("s5_C_re", "rep"), ("s5_C_im", "rep"),
    ("s5_D", "rep"), ("s5_w_glu", "row"), ("s5_b_glu", "rep"), ("l1_w_out", "row"), ("l1_norm_mlp", "rep"),
    ("l1_w_up", "col"), ("l1_w_down", "row"), ("final_norm", "rep"),
)
GATHER_BF16 = ("l0_w_in", "l0_w_out", "l0_w_up", "l0_w_down", "l1_w_in", "l1_w_out", "l1_w_up", "l1_w_down", "s5_w_glu")
GATHER_F32 = ("ssd_conv_w", "gdn_conv_w")


def _dg(a, b, ca, cb, prec=None):
    return lax.dot_general(a, b, (((ca,), (cb,)), ((), ())), preferred_element_type=F32, precision=prec)


def _lo(a):
    return a.astype(_MXU_DTYPE)


@jax.custom_vjp
def _mm(a, b):
    return _dg(_lo(a), _lo(b), 1, 0)


def _mm_fwd(a, b):
    return _mm(a, b), (a, b)


def _mm_bwd(res, g):
    a, b = res
    return _dg(_lo(g), _lo(b), 1, 1), _dg(_lo(a), _lo(g), 0, 0)


_mm.defvjp(_mm_fwd, _mm_bwd)


@jax.custom_vjp
def _mm_nt(a, b):
    return _dg(_lo(a), _lo(b), 1, 1)


def _mm_nt_fwd(a, b):
    return _mm_nt(a, b), (a, b)


def _mm_nt_bwd(res, g):
    a, b = res
    return _dg(_lo(g), _lo(b), 1, 0), _dg(_lo(g), _lo(a), 0, 0)


_mm_nt.defvjp(_mm_nt_fwd, _mm_nt_bwd)


@jax.custom_vjp
def _mm_tn(a, b):
    return _dg(_lo(a), _lo(b), 0, 0)


def _mm_tn_fwd(a, b):
    return _mm_tn(a, b), (a, b)


def _mm_tn_bwd(res, g):
    a, b = res
    return _dg(_lo(b), _lo(g), 1, 1), _dg(_lo(a), _lo(g), 1, 0)


_mm_tn.defvjp(_mm_tn_fwd, _mm_tn_bwd)


def _split2(x):
    hi = _lo(x)
    return hi, _lo(x - hi.astype(F32))


def _split3(x):
    h1 = _lo(x)
    r1 = x - h1.astype(F32)
    h2 = _lo(r1)
    return h1, h2, _lo(r1 - h2.astype(F32))


def _tri_cum_dir(m, ca):
    n, w = m.shape
    causal, _ = _tri_masks(n)
    out = _dg(causal.astype(_MXU_DTYPE), jnp.concatenate(_split3(m), axis=1), ca, 0)
    return out[:, :w] + out[:, w:2 * w] + out[:, 2 * w:]


@jax.custom_vjp
def _tri_cum(m):
    return _tri_cum_dir(m, 1)


def _tri_cum_fwd(m):
    return _tri_cum_dir(m, 1), None


def _tri_cum_bwd(_, g):
    return (_tri_cum_dir(g, 0),)


_tri_cum.defvjp(_tri_cum_fwd, _tri_cum_bwd)


@jax.custom_vjp
def _mm_exact_rhs(a, e):
    return _dg(jnp.concatenate(_split3(a), axis=1), jnp.concatenate([_lo(e)] * 3, axis=0), 1, 0)


def _mm_exact_rhs_fwd(a, e):
    return _mm_exact_rhs(a, e), e


def _mm_exact_rhs_bwd(e, g):
    return _dg(jnp.concatenate(_split3(g), axis=1), jnp.concatenate([_lo(e)] * 3, axis=1), 1, 1), jnp.zeros_like(e)


_mm_exact_rhs.defvjp(_mm_exact_rhs_fwd, _mm_exact_rhs_bwd)


def _bd(x):
    left = _iota(x.shape, 1) < (x.shape[1] // 2)
    zero = jnp.zeros_like(x)
    return jnp.concatenate([jnp.where(left, x, zero), jnp.where(left, zero, x)], axis=0)


def _unbd(m):
    half = m.shape[0] // 2
    left = _iota((half, m.shape[1]), 1) < (m.shape[1] // 2)
    return jnp.where(left, m[:half], m[half:])


def _pmm_nn(x, y):
    xh, xl = _split2(x)
    yh, yl = _split2(y)
    return _dg(jnp.concatenate([xh, xl, xh], axis=1), jnp.concatenate([_bd(yh), _bd(yh), _bd(yl)], axis=0), 1, 0)


def _pmm_nt(x, y):
    xh, xl = _split2(x)
    yh, yl = _split2(y)
    return _dg(jnp.concatenate([xh, xl, xh], axis=1), jnp.concatenate([_bd(yh), _bd(yh), _bd(yl)], axis=1), 1, 1)


def _pmm_tn(x, y):
    xh, xl = _split2(x)
    yh, yl = _split2(y)
    return _unbd(_dg(jnp.concatenate([xh, xl, xh], axis=0), jnp.concatenate([yh, yh, yl], axis=0), 0, 0))


@functools.lru_cache(maxsize=None)
def _shift(s, axis):
    @jax.custom_vjp
    def sh(x):
        return pltpu.roll(x, s, axis)

    def fwd(x):
        return sh(x), None

    def bwd(_, g):
        n = g.shape[axis]
        return (pltpu.roll(g, (n - s) % n, axis),)

    sh.defvjp(fwd, bwd)
    return sh


def _iota(shape, axis):
    return lax.broadcasted_iota(jnp.int32, shape, axis)


def _silu(x):
    return x * jax.nn.sigmoid(x)


def _unit_rms(x):
    return x * lax.rsqrt(jnp.mean(x * x, axis=-1, keepdims=True) + EPS)


def _l2norm(x):
    return x * lax.rsqrt(jnp.sum(x * x, axis=-1, keepdims=True) + EPS)


def _tri_masks(n):
    r, c = _iota((n, n), 0), _iota((n, n), 1)
    return r >= c, r > c


def _packed_rc():
    return _iota((CHUNK, 2 * CHUNK), 0), _iota((CHUNK, 2 * CHUNK), 1) & (CHUNK - 1)


def _decay_packed(g_packed):
    r, c = _packed_rc()
    seg = _tri_cum(g_packed * (r > c).astype(F32))
    return jnp.where(r >= c, jnp.exp(jnp.where(r >= c, seg, 0.0)), 0.0)


def _conv(x, tail, w):
    rows, width = x.shape
    row = _iota((rows, width), 0)
    acc = x * w[3:4, :]
    pad = jnp.zeros((rows - 8, width), F32)
    for j in range(3):
        s = 3 - j
        prev = jnp.concatenate([_shift(s, 0)(tail), pad], axis=0)
        acc = acc + w[j:j + 1, :] * jnp.where(row < s, prev, _shift(s, 0)(x))
    return acc


def _tri_inv_impl(a):
    r, c = _packed_rc()
    eye = (r == c).astype(F32)

    def same_block(b):
        return (r // b) == (c // b)

    a8 = jnp.where(same_block(8), a, 0.0)
    a2 = _pmm_nn(a8, a8)
    a4 = _pmm_nn(a2, a2)
    x = _pmm_nn(_pmm_nn(eye - a8, eye + a2), eye + a4)
    for b in (8, 16, 32):
        off = jnp.where(same_block(2 * b) & jnp.logical_not(same_block(b)), a, 0.0)
        x = x - _pmm_nn(_pmm_nn(x, off), x)
    return x


@jax.custom_vjp
def _tri_inv(a):
    return _tri_inv_impl(a)


def _tri_inv_fwd(a):
    t = _tri_inv_impl(a)
    return t, t


def _tri_inv_bwd(t, g):
    return (-_pmm_nt(_pmm_tn(t, g), t),)


_tri_inv.defvjp(_tri_inv_fwd, _tri_inv_bwd)


def _f_ret(tabs, consts, xs, xtabs, states):
    dmask, kdec, qdec, cdec = tabs
    q, k, v, gate = xs
    cs, sn = xtabs
    (st,) = states
    swap = _shift(RET_DK // 2, 1)
    outs, new = [], []
    for h in range(RET_HEADS):
        sl = slice(128 * h, 128 * h + 128)
        qh, kh, vh = q[:, sl], k[:, sl], v[:, sl]
        qh = (qh * cs + swap(qh) * sn) * (RET_DK ** -0.5)
        kh = kh * cs + swap(kh) * sn
        sh = st[sl, :]
        scores = _mm_nt(qh, kh) * dmask[64 * h:64 * h + 64, :]
        y = _mm(scores, vh) + _mm(qh * qdec[:, sl], sh)
        new.append(sh * cdec[:, sl] + _mm_tn(kh * kdec[:, sl], vh))
        outs.append(_silu(gate[:, sl]) * _unit_rms(y))
    return (jnp.concatenate(outs, axis=1),), [jnp.concatenate(new, axis=0)]


def _f_ssd(tabs, consts, xs, xtabs, states):
    (expand,) = tabs
    conv_w, conv_b, dtb, alog, dskip, nw = consts
    z, xr, br, cr, dtr = xs
    tx, tb, tc, st = states
    xc = _silu(_conv(xr, tx, conv_w[:, 0:512]) + conv_b[:, 0:512])
    bc = _silu(_conv(br, tb, conv_w[:, 512:768]) + conv_b[:, 512:768])
    cc = _silu(_conv(cr, tc, conv_w[:, 768:1024]) + conv_b[:, 768:1024])
    dt = jax.nn.softplus(_mm_exact_rhs(dtr, expand) + dtb)
    la = dt * (-jnp.exp(alog))
    lacum = _tri_cum(la)
    total = jnp.sum(la, axis=0, keepdims=True)
    xd = xc * dt
    dte, ecum, cdec = jnp.exp(total - lacum), jnp.exp(lacum), jnp.exp(total)
    ys, new = [], []
    cb2 = None
    for p in range(4):
        g = p // 2
        bg, cg = bc[:, 128 * g:128 * g + 128], cc[:, 128 * g:128 * g + 128]
        if p % 2 == 0:
            cb2 = _mm_nt(cg, jnp.concatenate([bg, bg], axis=0))
        sl = slice(128 * p, 128 * p + 128)
        xdp, sp = xd[:, sl], st[sl, :]
        yp = _mm(cg, sp) * ecum[:, sl] + _mm(cb2 * _decay_packed(la[:, sl]), _bd(xdp))
        new.append(sp * cdec[:, sl] + _mm_tn(bg, xdp * dte[:, sl]))
        ys.append(yp)
    y = jnp.concatenate(ys, axis=1) + dskip * xc
    yg = y * _silu(z)
    out = jnp.concatenate([_unit_rms(yg[:, 0:256]), _unit_rms(yg[:, 256:512])], axis=1) * nw
    return (out,), [xr[CHUNK - 8:, :], br[CHUNK - 8:, :], cr[CHUNK - 8:, :], jnp.concatenate(new, axis=0)]


def _f_gdn(tabs, consts, xs, xtabs, states):
    conv_w, p_alog, p_dtb, nw = consts
    qr, kr, vr, z, ba = xs
    tq, tk, tv, st = states
    qc = _silu(_conv(qr, tq, conv_w[:, 0:768]))
    kc = _silu(_conv(kr, tk, conv_w[:, 768:1536]))
    vc = _silu(_conv(vr, tv, conv_w[:, 1536:2304]))
    gl = -jnp.exp(p_alog) * jax.nn.softplus(ba + p_dtb)
    bl = jax.nn.sigmoid(ba)
    gcum = _tri_cum(gl)
    left128 = _iota((CHUNK, 128), 1) < 64
    left256 = _iota((CHUNK, 256), 1) < 128
    r, c = _packed_rc()
    diag_blocks = (_iota((256, 256), 0) < 128) == (_iota((256, 256), 1) < 128)

    def norm2(t):
        return jnp.concatenate([_l2norm(t[:, 0:128]), _l2norm(t[:, 128:256])], axis=1)

    outs, new = [], []
    for p in range(GDN_HEADS // 2):
        sl = slice(256 * p, 256 * p + 256)

        def pick(arr, off, left):
            return jnp.where(left, arr[:, off + 2 * p:off + 2 * p + 1], arr[:, off + 2 * p + 1:off + 2 * p + 2])

        qn = norm2(qc[:, sl]) * (GDN_DK ** -0.5)
        kn = norm2(kc[:, sl])
        dec = _decay_packed(pick(gl, 6, left128))
        g2, gc2, b2 = pick(gl, 6, left256), pick(gcum, 6, left256), pick(bl, 0, left256)
        tot = jnp.sum(g2, axis=0, keepdims=True)
        eg, et, cd = jnp.exp(gc2), jnp.exp(tot - gc2), jnp.exp(tot)
        kb, vb = kn * b2, vc[:, sl] * b2
        kbd = _bd(kn)
        tm = _tri_inv(jnp.where(r > c, _mm_nt(kb, kbd) * dec, 0.0))
        u = _mm(tm, _bd(vb))
        w = _mm(tm, _bd(kb * eg))
        attn = _mm_nt(qn, kbd) * dec
        sp = st[sl, :]
        vn = u - _mm(w, sp)
        o = _mm(qn * eg, sp) + _mm(attn, _bd(vn))
        new.append(sp * cd + jnp.where(diag_blocks, _mm_tn(kn * et, vn), 0.0))
        for hh in range(2):
            osl = slice(128 * hh, 128 * hh + 128)
            zsl = slice(256 * p + 128 * hh, 256 * p + 128 * hh + 128)
            outs.append(_unit_rms(o[:, osl]) * nw * _silu(z[:, zsl]))
    return (jnp.concatenate(outs, axis=1),), [qr[CHUNK - 8:, :], kr[CHUNK - 8:, :], vr[CHUNK - 8:, :],
                                             jnp.concatenate(new, axis=0)]


def _f_s5(tabs, consts, xs, xtabs, states):
    lam_re, lam_im, bblk, c_re, c_im, dskip, wglu, bglu = consts
    (u,) = xs
    s_re, s_im = states
    rows = u.shape[0]
    n = lam_re.shape[1]
    bu = _mm(u, bblk)
    hr, hi = bu[:, 0:n], bu[:, n:2 * n]
    row = _iota((rows, n), 0)
    h0r, h0i = s_re[0:1, :], s_im[0:1, :]
    hr = hr + jnp.where(row == 0, lam_re * h0r - lam_im * h0i, 0.0)
    hi = hi + jnp.where(row == 0, lam_re * h0i + lam_im * h0r, 0.0)
    pr, pi = lam_re, lam_im
    d = 1
    while d < rows:
        sr = jnp.where(row >= d, _shift(d, 0)(hr), 0.0)
        si = jnp.where(row >= d, _shift(d, 0)(hi), 0.0)
        hr, hi = hr + pr * sr - pi * si, hi + pr * si + pi * sr
        pr, pi = pr * pr - pi * pi, 2.0 * pr * pi
        d *= 2
    y = _mm(hr, c_re) - _mm(hi, c_im) + dskip * u
    y = jax.nn.gelu(y)
    out = y * jax.nn.sigmoid(_mm(y, wglu) + bglu)
    last_r = jnp.broadcast_to(hr[rows - 1:rows, :], (8, n))
    last_i = jnp.broadcast_to(hi[rows - 1:rows, :], (8, n))
    return (out,), [last_r, last_i]


def _full_spec(a):
    nd = a.ndim
    return pl.BlockSpec(a.shape, lambda i, _nd=nd: (0,) * _nd)


def _scan_fwd(name, f, rows, tabs, consts, xs, xtabs, state_shapes, y_total, y_width, y_cb, y_alias=None):
    seq = xs[0][0].shape[0]
    nc = seq // rows
    nt, ncst, nx, nxt, ns = len(tabs), len(consts), len(xs), len(xtabs), len(state_shapes)
    alias = y_alias is not None

    def body(*refs):
        p = 0
        tab_r = refs[p:p + nt]; p += nt
        c_r = refs[p:p + ncst]; p += ncst
        x_r = refs[p:p + nx]; p += nx
        xt_r = refs[p:p + nxt]; p += nxt
        if alias:
            p += 1
        y_ref = refs[p]; p += 1
        sv_r = refs[p:p + ns]; p += ns
        st_r = refs[p:p + ns]

        @pl.when(pl.program_id(0) == 0)
        def _():
            for s in st_r:
                s[...] = jnp.zeros(s.shape, F32)

        st = [s[...] for s in st_r]
        for r, v in zip(sv_r, st):
            r[...] = v
        (y,), new = f([r[...] for r in tab_r], [r[...] for r in c_r], [r[...] for r in x_r],
                      [r[...] for r in xt_r], st)
        y_ref[...] = y
        for s, v in zip(st_r, new):
            s[...] = v

    win = [pl.BlockSpec((rows, w), lambda i, _cb=cb: (i, _cb)) for (_, w, cb) in list(xs) + list(xtabs)]
    in_specs = [_full_spec(a) for a in list(tabs) + list(consts)] + win
    args = list(tabs) + list(consts) + [a for (a, _, _) in list(xs) + list(xtabs)]
    io_alias = {}
    if alias:
        in_specs.append(pl.BlockSpec(memory_space=pl.ANY))
        io_alias = {len(args): 0}
        args.append(y_alias)
    out_shape = [jax.ShapeDtypeStruct((seq, y_total), F32)]
    out_specs = [pl.BlockSpec((rows, y_width), lambda i: (i, y_cb))]
    for (r, c) in state_shapes:
        out_shape.append(jax.ShapeDtypeStruct((nc * r, c), F32))
        out_specs.append(pl.BlockSpec((r, c), lambda i: (i, 0)))
    res = pl.pallas_call(
        body, name=name, grid=(nc,), in_specs=in_specs, out_specs=out_specs, out_shape=out_shape,
        scratch_shapes=[pltpu.VMEM(s, F32) for s in state_shapes], input_output_aliases=io_alias,
        compiler_params=pltpu.CompilerParams(dimension_semantics=("arbitrary",), vmem_limit_bytes=VMEM_LIMIT),
    )(*args)
    return res[0], list(res[1:])


def _scan_bwd(name, f, rows, tabs, consts, xs, xtabs, saved, state_shapes, dy, dx_total, dx_width, dx_cb,
              assemble, dx_alias=None):
    seq = xs[0][0].shape[0]
    nc = seq // rows
    nt, ncst, nx, nxt, ns = len(tabs), len(consts), len(xs), len(xtabs), len(state_shapes)
    alias = dx_alias is not None

    def body(*refs):
        p = 0
        tab_r = refs[p:p + nt]; p += nt
        c_r = refs[p:p + ncst]; p += ncst
        x_r = refs[p:p + nx]; p += nx
        xt_r = refs[p:p + nxt]; p += nxt
        sv_r = refs[p:p + ns]; p += ns
        dy_ref = refs[p]; p += 1
        if alias:
            p += 1
        dx_ref = refs[p]; p += 1
        dc_r = refs[p:p + ncst]; p += ncst
        ds_r = refs[p:p + ns]

        @pl.when(pl.program_id(0) == 0)
        def _():
            for s in ds_r:
                s[...] = jnp.zeros(s.shape, F32)
            for r in dc_r:
                r[...] = jnp.zeros(r.shape, F32)

        tab_v = [r[...] for r in tab_r]
        xt_v = [r[...] for r in xt_r]

        def g(c, x, s):
            (y,), new = f(tab_v, c, x, xt_v, s)
            return y, new

        _, vjp = jax.vjp(g, [r[...] for r in c_r], [r[...] for r in x_r], [r[...] for r in sv_r])
        dc, dx, ds = vjp((dy_ref[...], [s[...] for s in ds_r]))
        dx_ref[...] = assemble(dx)
        for r, v in zip(dc_r, dc):
            r[...] += v
        for s, v in zip(ds_r, ds):
            s[...] = v

    win = [pl.BlockSpec((rows, w), lambda j, _cb=cb: (nc - 1 - j, _cb)) for (_, w, cb) in list(xs) + list(xtabs)]
    in_specs = [_full_spec(a) for a in list(tabs) + list(consts)] + win
    args = list(tabs) + list(consts) + [a for (a, _, _) in list(xs) + list(xtabs)]
    for (r, c), sv in zip(state_shapes, saved):
        in_specs.append(pl.BlockSpec((r, c), lambda j: (nc - 1 - j, 0)))
        args.append(sv)
    in_specs.append(pl.BlockSpec((rows, dy[1]), lambda j: (nc - 1 - j, dy[2])))
    args.append(dy[0])
    io_alias = {}
    if alias:
        in_specs.append(pl.BlockSpec(memory_space=pl.ANY))
        io_alias = {len(args): 0}
        args.append(dx_alias)
    out_shape = [jax.ShapeDtypeStruct((seq, dx_total), F32)] + [jax.ShapeDtypeStruct(a.shape, F32) for a in consts]
    out_specs = [pl.BlockSpec((rows, dx_width), lambda j: (nc - 1 - j, dx_cb))] + [_full_spec(a) for a in consts]
    res = pl.pallas_call(
        body, name=name, grid=(nc,), in_specs=in_specs, out_specs=out_specs, out_shape=out_shape,
        scratch_shapes=[pltpu.VMEM(s, F32) for s in state_shapes], input_output_aliases=io_alias,
        compiler_params=pltpu.CompilerParams(dimension_semantics=("arbitrary",), vmem_limit_bytes=VMEM_LIMIT),
    )(*args)
    return res[0], list(res[1:])


def _tile(n, want):
    t = min(n, want)
    while n % t:
        t //= 2
    return t


MATMUL_VMEM_BUDGET = 40 * 1024 * 1024


def _pick_tiles(m, n, k, sa, sb, so, se):
    best = None
    for tn in {_tile(n, 1024), _tile(n, 512)}:
        for tm in {_tile(m, t) for t in (2048, 1024, 512)}:
            for tk in {_tile(k, t) for t in (4096, 2048, 1024, 512)}:
                at, bt, ot = tm * tk * sa, tk * tn * sb, tm * tn * so
                need = 2 * (at + bt + ot + tm * tn * se) + 2 * tm * tn * 4 + (at if sa == 4 else 0) + (bt if sb == 4 else 0)
                if need > MATMUL_VMEM_BUDGET:
                    continue
                key = ((m // tm) * (n // tn) * (k // tk), k // tk, -tm, -tn)
                if best is None or key < best[0]:
                    best = (key, (tm, tn, tk))
    assert best is not None, (m, n, k)
    return best[1]


def _matmul(name, a, b, mode, out_dtype=F32, a_pro=None, epi=None, epi_arr=None):
    if mode == "nn":
        (m, k), (k2, n) = a.shape, b.shape
    elif mode == "nt":
        (m, k), (n, k2) = a.shape, b.shape
    else:
        (k, m), (k2, n) = a.shape, b.shape
    assert k == k2, (name, a.shape, b.shape)
    size = lambda t: jnp.dtype(t).itemsize
    tm, tn, tk = _pick_tiles(m, n, k, size(a.dtype), size(b.dtype), size(out_dtype),
                             0 if epi is None else size(epi_arr.dtype))
    nk = k // tk
    ca, cb = {"nn": (1, 0), "nt": (1, 1), "tn": (0, 0)}[mode]

    def body(*refs):
        refs = list(refs)
        acc = refs.pop() if nk > 1 else None
        a_ref, b_ref = refs[0], refs[1]
        e_ref = refs[2] if epi is not None else None
        o_ref = refs[-1]

        av = a_ref[...]
        if a_pro == "relu2":
            r = jnp.maximum(av, 0.0)
            av = r * r
        part = _dg(_lo(av), _lo(b_ref[...]), ca, cb)

        def finish(r):
            if epi == "add":
                r = r + e_ref[...]
            elif epi == "drelu2":
                r = r * (2.0 * jnp.maximum(e_ref[...], 0.0))
            o_ref[...] = r.astype(out_dtype)

        if nk == 1:
            finish(part)
        else:
            kk = pl.program_id(2)

            @pl.when(kk == 0)
            def _():
                acc[...] = part

            @pl.when(kk > 0)
            def _():
                acc[...] += part

            @pl.when(kk == nk - 1)
            def _():
                finish(acc[...])

    if mode == "tn":
        a_spec = pl.BlockSpec((tk, tm), lambda j, i, kk: (kk, i))
    else:
        a_spec = pl.BlockSpec((tm, tk), lambda j, i, kk: (i, kk))
    if mode == "nt":
        b_spec = pl.BlockSpec((tn, tk), lambda j, i, kk: (j, kk))
    else:
        b_spec = pl.BlockSpec((tk, tn), lambda j, i, kk: (kk, j))
    o_spec = pl.BlockSpec((tm, tn), lambda j, i, kk: (i, j))
    in_specs, args = [a_spec, b_spec], [a, b]
    if epi is not None:
        in_specs.append(o_spec)
        args.append(epi_arr)
    return pl.pallas_call(
        body, name=name, grid=(n // tn, m // tm, nk), in_specs=in_specs, out_specs=o_spec,
        out_shape=jax.ShapeDtypeStruct((m, n), out_dtype),
        scratch_shapes=[pltpu.VMEM((tm, tn), F32)] if nk > 1 else [],
        compiler_params=pltpu.CompilerParams(dimension_semantics=("parallel", "parallel", "arbitrary"),
                                             vmem_limit_bytes=VMEM_LIMIT),
    )(*args)


ROW_TILE = 512


def _rmsnorm_fwd(name, x, w):
    seq, d = x.shape
    tr = _tile(seq, ROW_TILE)

    def body(x_ref, w_ref, o_ref):
        xv = x_ref[...]
        o_ref[...] = (_unit_rms(xv) * w_ref[...]).astype(_MXU_DTYPE)

    return pl.pallas_call(
        body, name=name, grid=(seq // tr,),
        in_specs=[pl.BlockSpec((tr, d), lambda i: (i, 0)), pl.BlockSpec((1, d), lambda i: (0, 0))],
        out_specs=pl.BlockSpec((tr, d), lambda i: (i, 0)), out_shape=jax.ShapeDtypeStruct((seq, d), _MXU_DTYPE),
        compiler_params=pltpu.CompilerParams(dimension_semantics=("parallel",), vmem_limit_bytes=VMEM_LIMIT),
    )(x, w)


def _rmsnorm_bwd(name, dh, x, w, dres):
    seq, d = x.shape
    tr = _tile(seq, ROW_TILE)

    def body(dh_ref, x_ref, w_ref, dres_ref, dx_ref, dw_ref):
        @pl.when(pl.program_id(0) == 0)
        def _():
            dw_ref[...] = jnp.zeros(dw_ref.shape, F32)

        xv = x_ref[...]
        rstd = lax.rsqrt(jnp.mean(xv * xv, axis=-1, keepdims=True) + EPS)
        xh = xv * rstd
        dhv = dh_ref[...]
        g = dhv * w_ref[...]
        dx_ref[...] = dres_ref[...] + rstd * (g - xh * jnp.mean(g * xh, axis=-1, keepdims=True))
        dw_ref[...] += jnp.sum(dhv * xh, axis=0, keepdims=True)

    row = pl.BlockSpec((tr, d), lambda i: (i, 0))
    vec = pl.BlockSpec((1, d), lambda i: (0, 0))
    return pl.pallas_call(
        body, name=name, grid=(seq // tr,), in_specs=[row, row, vec, row], out_specs=[row, vec],
        out_shape=[jax.ShapeDtypeStruct((seq, d), F32), jax.ShapeDtypeStruct((1, d), F32)],
        compiler_params=pltpu.CompilerParams(dimension_semantics=("arbitrary",), vmem_limit_bytes=VMEM_LIMIT),
    )(dh, x, w, dres)


def _loss_head(name, x, w, target):
    seq, d = x.shape
    tr = _tile(seq, ROW_TILE)

    def body(x_ref, w_ref, t_ref, loss_ref, dx_ref, dw_ref):
        @pl.when(pl.program_id(0) == 0)
        def _():
            dw_ref[...] = jnp.zeros(dw_ref.shape, F32)
            loss_ref[...] = jnp.zeros(loss_ref.shape, F32)

        xv = x_ref[...]
        rstd = lax.rsqrt(jnp.mean(xv * xv, axis=-1, keepdims=True) + EPS)
        xh = xv * rstd
        err = xh * w_ref[...] - t_ref[...]
        per_row = jnp.mean(err * err, axis=-1, keepdims=True)
        loss_ref[...] += 0.5 * jnp.sum(per_row, axis=0, keepdims=True)
        dy = err * (1.0 / d)
        g = dy * w_ref[...]
        dx_ref[...] = rstd * (g - xh * jnp.mean(g * xh, axis=-1, keepdims=True))
        dw_ref[...] += jnp.sum(dy * xh, axis=0, keepdims=True)

    row = pl.BlockSpec((tr, d), lambda i: (i, 0))
    vec = pl.BlockSpec((1, d), lambda i: (0, 0))
    one = pl.BlockSpec((1, 1), lambda i: (0, 0))
    return pl.pallas_call(
        body, name=name, grid=(seq // tr,), in_specs=[row, vec, row], out_specs=[one, row, vec],
        out_shape=[jax.ShapeDtypeStruct((1, 1), F32), jax.ShapeDtypeStruct((seq, d), F32),
                   jax.ShapeDtypeStruct((1, d), F32)],
        compiler_params=pltpu.CompilerParams(dimension_semantics=("arbitrary",), vmem_limit_bytes=VMEM_LIMIT),
    )(x, w, target)


SLAB_TILE_ROWS = 1024


def _slab_tile(rows, cap=SLAB_TILE_ROWS):
    return max(t for t in range(8, min(rows, cap) + 1, 8) if rows % t == 0)


def _adamw(name, w, g, m, v):
    rows = w.shape[0]
    tr = _slab_tile(rows, SLAB_TILE_ROWS // 2)

    def body(w_ref, g_ref, m_ref, v_ref, d_ref, nm_ref, nv_ref):
        gv = g_ref[...]
        nm = ADAM_B1 * m_ref[...] + (1.0 - ADAM_B1) * gv
        nv = ADAM_B2 * v_ref[...] + (1.0 - ADAM_B2) * (gv * gv)
        m_hat = nm / (1.0 - ADAM_B1 ** ADAM_STEP)
        v_hat = nv / (1.0 - ADAM_B2 ** ADAM_STEP)
        d_ref[...] = -ADAM_LR * (m_hat / (jnp.sqrt(v_hat) + ADAM_EPS) + ADAM_WD * w_ref[...])
        nm_ref[...] = nm
        nv_ref[...] = nv

    spec = pl.BlockSpec((tr, LANES), lambda i: (i, 0))
    sds = jax.ShapeDtypeStruct(w.shape, F32)
    return pl.pallas_call(
        body, name=name, grid=(rows // tr,), in_specs=[spec] * 4, out_specs=[spec] * 3, out_shape=[sds] * 3,
        compiler_params=pltpu.CompilerParams(dimension_semantics=("parallel",), vmem_limit_bytes=VMEM_LIMIT),
    )(w, g, m, v)


def _add_halves(name, g, t1, c):
    nsec, rows, _ = g.shape
    rh = rows // 2
    tr = _slab_tile(rh)
    nb = rh // tr

    def body(c_ref, g_ref, t_ref, o_ref):
        o_ref[...] = g_ref[...] + t_ref[...]

    gs = pltpu.PrefetchScalarGridSpec(
        num_scalar_prefetch=1, grid=(nsec, nb),
        in_specs=[pl.BlockSpec((1, tr, LANES), lambda s, i, c_ref: (s, c_ref[0] * nb + i, 0)),
                  pl.BlockSpec((1, tr, LANES), lambda s, i, c_ref: (s, i, 0))],
        out_specs=pl.BlockSpec((1, tr, LANES), lambda s, i, c_ref: (s, i, 0)))
    return pl.pallas_call(
        body, name=name, grid_spec=gs, out_shape=jax.ShapeDtypeStruct((nsec, rh, LANES), F32),
        compiler_params=pltpu.CompilerParams(dimension_semantics=("parallel", "parallel"),
                                             vmem_limit_bytes=VMEM_LIMIT),
    )(c, g, t1)


def _add_four(name, t2):
    _, rh, _ = t2.shape
    tr = _slab_tile(rh)

    def body(t_ref, o_ref):
        o_ref[...] = ((t_ref[0] + t_ref[1]) + t_ref[2]) + t_ref[3]

    return pl.pallas_call(
        body, name=name, grid=(rh // tr,), in_specs=[pl.BlockSpec((4, tr, LANES), lambda i: (0, i, 0))],
        out_specs=pl.BlockSpec((tr, LANES), lambda i: (i, 0)), out_shape=jax.ShapeDtypeStruct((rh, LANES), F32),
        compiler_params=pltpu.CompilerParams(dimension_semantics=("parallel",), vmem_limit_bytes=VMEM_LIMIT),
    )(t2)


ANY = pl.BlockSpec(memory_space=pl.ANY)


def _place():
    return lax.axis_index("x"), lax.axis_index("y"), lax.axis_index("c")


def _all_gather_shards(name, slab):
    rows = slab.shape[0]
    rh = rows // 2

    def body(x_ref, out_ref, send_sems, recv_sems, local_sem):
        x, y, c = _place()
        sibling = (x, y, 1 - c)
        chips = [(1 - x, y), (x, 1 - y), (1 - x, 1 - y)]

        def part(px, py, pc):
            return out_ref.at[2 * px + py, pl.ds(pc * rh, rh), :]

        def copy(k, block, to, src=None):
            return pltpu.make_async_remote_copy(
                src_ref=part(*block) if src is None else src, dst_ref=part(*block),
                send_sem=send_sems.at[k], recv_sem=recv_sems.at[k], device_id=to, device_id_type=MESH)

        mine = pltpu.make_async_copy(x_ref, out_ref.at[2 * x + y], local_sem)
        mine.start()
        my_half = x_ref.at[pl.ds(c * rh, rh), :]
        first = [copy(j, (x, y, c), (*chip, c), src=my_half) for j, chip in enumerate(chips)]
        for cp in first:
            cp.start()
        passed = [copy(3 + j, (*chip, c), sibling) for j, chip in enumerate(chips)]
        for j, chip in enumerate(chips):
            copy(j, (*chip, c), (x, y, c)).wait_recv()
            passed[j].start()
        for j, chip in enumerate(chips):
            copy(3 + j, (*chip, 1 - c), (x, y, c)).wait_recv()
        for cp in first + passed:
            cp.wait_send()
        mine.wait()

    return pl.pallas_call(
        body, name=name, in_specs=[ANY], out_specs=ANY,
        out_shape=jax.ShapeDtypeStruct((4, rows, LANES), slab.dtype),
        scratch_shapes=[pltpu.SemaphoreType.DMA((6,)), pltpu.SemaphoreType.DMA((6,)), pltpu.SemaphoreType.DMA],
    )(slab)


def _swap_halves(name, g):
    nsec, rows, _ = g.shape
    rh = rows // 2

    def body(g_ref, t_ref, send_sem, recv_sem):
        x, y, c = _place()
        cp = pltpu.make_async_remote_copy(
            src_ref=g_ref.at[:, pl.ds((1 - c) * rh, rh), :], dst_ref=t_ref, send_sem=send_sem, recv_sem=recv_sem,
            device_id=(x, y, 1 - c), device_id_type=MESH)
        cp.start()
        cp.wait()

    return pl.pallas_call(
        body, name=name, in_specs=[ANY], out_specs=ANY, out_shape=jax.ShapeDtypeStruct((nsec, rh, LANES), F32),
        scratch_shapes=[pltpu.SemaphoreType.DMA, pltpu.SemaphoreType.DMA],
    )(g)


def _scatter_to_owners(name, p):
    _, rh, _ = p.shape

    def body(p_ref, t_ref, send_sems, recv_sems, local_sem):
        x, y, c = _place()
        me = 2 * x + y
        chips = [(1 - x, y), (x, 1 - y), (1 - x, 1 - y)]
        mine = pltpu.make_async_copy(p_ref.at[me], t_ref.at[me], local_sem)
        mine.start()
        sends = []
        for j, (cx, cy) in enumerate(chips):
            sends.append(pltpu.make_async_remote_copy(
                src_ref=p_ref.at[2 * cx + cy], dst_ref=t_ref.at[me], send_sem=send_sems.at[j],
                recv_sem=recv_sems.at[j], device_id=(cx, cy, c), device_id_type=MESH))
        for cp in sends:
            cp.start()
        for j, (cx, cy) in enumerate(chips):
            pltpu.make_async_remote_copy(
                src_ref=p_ref.at[me], dst_ref=t_ref.at[2 * cx + cy], send_sem=send_sems.at[j],
                recv_sem=recv_sems.at[j], device_id=(cx, cy, c), device_id_type=MESH).wait_recv()
        for cp in sends:
            cp.wait_send()
        mine.wait()

    return pl.pallas_call(
        body, name=name, in_specs=[ANY], out_specs=ANY, out_shape=jax.ShapeDtypeStruct((4, rh, LANES), F32),
        scratch_shapes=[pltpu.SemaphoreType.DMA((3,)), pltpu.SemaphoreType.DMA((3,)), pltpu.SemaphoreType.DMA],
    )(p)


def _join_halves(name, r_half):
    rh = r_half.shape[0]

    def body(h_ref, o_ref, send_sem, recv_sem, local_sem):
        x, y, c = _place()
        mine = pltpu.make_async_copy(h_ref, o_ref.at[pl.ds(c * rh, rh), :], local_sem)
        mine.start()
        cp = pltpu.make_async_remote_copy(
            src_ref=h_ref, dst_ref=o_ref.at[pl.ds(c * rh, rh), :], send_sem=send_sem, recv_sem=recv_sem,
            device_id=(x, y, 1 - c), device_id_type=MESH)
        cp.start()
        pltpu.make_async_remote_copy(
            src_ref=h_ref, dst_ref=o_ref.at[pl.ds((1 - c) * rh, rh), :], send_sem=send_sem, recv_sem=recv_sem,
            device_id=(x, y, 1 - c), device_id_type=MESH).wait_recv()
        cp.wait_send()
        mine.wait()

    return pl.pallas_call(
        body, name=name, in_specs=[ANY], out_specs=ANY, out_shape=jax.ShapeDtypeStruct((2 * rh, LANES), F32),
        scratch_shapes=[pltpu.SemaphoreType.DMA, pltpu.SemaphoreType.DMA, pltpu.SemaphoreType.DMA],
    )(r_half)


def _rows_of(n):
    return -(-n // LANES)


def _pack(arrays, dtype, align=32):
    parts = []
    for a in arrays:
        flat = a.reshape(-1).astype(dtype)
        parts.append(jnp.pad(flat, (0, _rows_of(flat.size) * LANES - flat.size)))
    flat = jnp.concatenate(parts)
    rows = flat.size // LANES
    rows_pad = -(-rows // align) * align
    return jnp.pad(flat, (0, (rows_pad - rows) * LANES)).reshape(rows_pad, LANES)


def _unpack(slab, shapes):
    out, r = [], 0
    for shp in shapes:
        n = math.prod(shp)
        out.append(slab[r:r + _rows_of(n)].reshape(-1)[:n].reshape(shp))
        r += _rows_of(n)
    return out


def _unpack_gathered(g, shapes, kinds):
    out, r = [], 0
    for shp, kind in zip(shapes, kinds):
        n = math.prod(shp)
        blk = g[:, r:r + _rows_of(n)].reshape(4, -1)[:, :n].reshape((4,) + tuple(shp))
        r += _rows_of(n)
        if kind == "col":
            out.append(jnp.moveaxis(blk, 0, 1).reshape(shp[0], 4 * shp[1]))
        else:
            out.append(blk.reshape(4 * shp[0], shp[1]))
    return out


def _shard_block(g, kind, s, local_shape):
    if kind == "col":
        return g[:, s * local_shape[1]:(s + 1) * local_shape[1]]
    if kind == "row":
        return g[s * local_shape[0]:(s + 1) * local_shape[0]]
    return g


def _rotary_tables(seq):
    half = RET_DK // 2
    pos = jnp.arange(seq, dtype=F32)
    inv = ROPE_THETA ** (-jnp.arange(half, dtype=F32) / half)
    ang = pos[:, None] * inv[None, :]
    cos, sin = jnp.cos(ang), jnp.sin(ang)
    return jnp.concatenate([cos, cos], axis=1), jnp.concatenate([-sin, sin], axis=1)


def _retention_tables():
    log_gamma = jnp.log(1.0 - 2.0 ** (-5.0 - jnp.arange(RET_HEADS, dtype=F32)))
    idx = jnp.arange(CHUNK, dtype=F32)
    diff = idx[:, None] - idx[None, :]
    dmask = jnp.exp(jnp.where((diff >= 0)[None], log_gamma[:, None, None] * diff[None], -jnp.inf))
    kdec = jnp.exp(log_gamma[None, :] * (CHUNK - 1.0 - idx)[:, None])
    qdec = jnp.exp(log_gamma[None, :] * (idx + 1.0)[:, None])
    cdec = jnp.exp(log_gamma * CHUNK)[None, :]
    lanes = lambda t: jnp.repeat(t, RET_DK, axis=1)
    return dmask.reshape(RET_HEADS * CHUNK, CHUNK), lanes(kdec), lanes(qdec), lanes(cdec)


def _s5_prep(a_re, a_im, log_step, b_re, b_im, c_re, c_im):
    g, n, c = S5_GROUPS, S5_STATE, S5_GROUP
    lam = lax.complex(a_re, a_im)
    step = jnp.exp(log_step)[:, None]
    lam_bar = jnp.exp(lam * step)
    b_bar = ((lam_bar - 1.0) / lam)[..., None] * lax.complex(b_re, b_im)
    eye = jnp.eye(g, dtype=F32)
    bb_re = (jnp.real(b_bar).transpose(0, 2, 1)[:, :, None, :] * eye[:, None, :, None]).reshape(g * c, g * n)
    bb_im = (jnp.imag(b_bar).transpose(0, 2, 1)[:, :, None, :] * eye[:, None, :, None]).reshape(g * c, g * n)
    cc_re = (c_re.transpose(0, 2, 1)[:, :, None, :] * eye[:, None, :, None]).reshape(g * n, g * c)
    cc_im = (c_im.transpose(0, 2, 1)[:, :, None, :] * eye[:, None, :, None]).reshape(g * n, g * c)
    return (jnp.real(lam_bar).reshape(1, g * n), jnp.imag(lam_bar).reshape(1, g * n),
            jnp.concatenate([bb_re, bb_im], axis=1), cc_re, cc_im)


def kernel(x, l0_norm_mix, l0_w_in, ssd_conv_w, ssd_conv_b, ssd_dt_bias, ssd_A_log, ssd_D, ssd_norm_w, l0_w_out, l0_norm_mlp, l0_w_up, l0_w_down, l1_norm_mix, l1_w_in, gdn_conv_w, gdn_A_log, gdn_dt_bias, gdn_norm_w, s5_A_re, s5_A_im, s5_log_step, s5_B_re, s5_B_im, s5_C_re, s5_C_im, s5_D, s5_w_glu, s5_b_glu, l1_w_out, l1_norm_mlp, l1_w_up, l1_w_down, final_norm, loss_target, m_l0_norm_mix, m_l0_w_in, m_ssd_conv_w, m_ssd_conv_b, m_ssd_dt_bias, m_ssd_A_log, m_ssd_D, m_ssd_norm_w, m_l0_w_out, m_l0_norm_mlp, m_l0_w_up, m_l0_w_down, m_l1_norm_mix, m_l1_w_in, m_gdn_conv_w, m_gdn_A_log, m_gdn_dt_bias, m_gdn_norm_w, m_s5_A_re, m_s5_A_im, m_s5_log_step, m_s5_B_re, m_s5_B_im, m_s5_C_re, m_s5_C_im, m_s5_D, m_s5_w_glu, m_s5_b_glu, m_l1_w_out, m_l1_norm_mlp, m_l1_w_up, m_l1_w_down, m_final_norm, v_l0_norm_mix, v_l0_w_in, v_ssd_conv_w, v_ssd_conv_b, v_ssd_dt_bias, v_ssd_A_log, v_ssd_D, v_ssd_norm_w, v_l0_w_out, v_l0_norm_mlp, v_l0_w_up, v_l0_w_down, v_l1_norm_mix, v_l1_w_in, v_gdn_conv_w, v_gdn_A_log, v_gdn_dt_bias, v_gdn_norm_w, v_s5_A_re, v_s5_A_im, v_s5_log_step, v_s5_B_re, v_s5_B_im, v_s5_C_re, v_s5_C_im, v_s5_D, v_s5_w_glu, v_s5_b_glu, v_l1_w_out, v_l1_norm_mlp, v_l1_w_up, v_l1_w_down, v_final_norm):
    given = dict(locals())
    names = [n for n, _ in PARAMS]
    kinds = dict(PARAMS)
    w = {n: given[n] for n in names}
    seq = x.shape[1]
    x0 = x.reshape(seq, D_MODEL)
    target = loss_target.reshape(seq, D_MODEL)

    gb = _all_gather_shards("gather_weights", _pack([w[n] for n in GATHER_BF16], _MXU_DTYPE))
    full = dict(zip(GATHER_BF16, _unpack_gathered(gb, [w[n].shape for n in GATHER_BF16],
                                                  [kinds[n] for n in GATHER_BF16])))
    gf = _all_gather_shards("gather_conv", _pack([w[n] for n in GATHER_F32], F32))
    full.update(zip(GATHER_F32, _unpack_gathered(gf, [w[n].shape for n in GATHER_F32],
                                                 [kinds[n] for n in GATHER_F32])))
    in0 = full["l0_w_in"].shape[1]
    w_in0 = jnp.pad(full["l0_w_in"], ((0, 0), (0, IN0_PAD - in0)))
    wi1 = full["l1_w_in"]
    in1 = wi1.shape[1]
    w_in1 = jnp.concatenate([wi1[:, :3072], wi1[:, 3084:in1], wi1[:, 3072:3084],
                             jnp.zeros((D_MODEL, IN1_PAD - in1), wi1.dtype)], axis=1)

    row = lambda a: a.reshape(1, -1)
    lanes64 = lambda a: jnp.repeat(a, SSD_HEAD_DIM).reshape(1, -1)

    h0 = _rmsnorm_fwd("norm_mix0", x0, row(w["l0_norm_mix"]))
    proj0 = _matmul("in_proj0", h0, w_in0, "nn")
    cos_t, sin_t = _rotary_tables(seq)
    ret_tabs = list(_retention_tables())
    ret_xs = [(proj0, 512, 0), (proj0, 512, 1), (proj0, 512, 2), (proj0, 512, 3)]
    ret_xt = [(cos_t, 128, 0), (sin_t, 128, 0)]
    ret_states = [(512, 128)]
    mixed0, ret_saved = _scan_fwd("ret_fwd", _f_ret, CHUNK, ret_tabs, [], ret_xs, ret_xt, ret_states, D_MODEL, 512, 0)
    expand = jnp.repeat(jnp.eye(128, SSD_HEADS, dtype=F32), SSD_HEAD_DIM, axis=1)
    ssd_consts = [full["ssd_conv_w"], row(w["ssd_conv_b"]), lanes64(w["ssd_dt_bias"]), lanes64(w["ssd_A_log"]),
                  lanes64(w["ssd_D"]), row(w["ssd_norm_w"])]
    ssd_xs = [(proj0, 512, 4), (proj0, 512, 5), (proj0, 256, 12), (proj0, 256, 13), (proj0, 128, 28)]
    ssd_states = [(8, 512), (8, 256), (8, 256), (512, 128)]
    mixed0, ssd_saved = _scan_fwd("ssd_fwd", _f_ssd, CHUNK, [expand], ssd_consts, ssd_xs, [], ssd_states,
                                  D_MODEL, 512, 1, y_alias=mixed0)
    x1 = _matmul("out_proj0", mixed0, full["l0_w_out"], "nn", epi="add", epi_arr=x0)
    h1 = _rmsnorm_fwd("norm_mlp0", x1, row(w["l0_norm_mlp"]))
    u0 = _matmul("up0", h1, full["l0_w_up"], "nn")
    x2 = _matmul("down0", u0, full["l0_w_down"], "nn", a_pro="relu2", epi="add", epi_arr=x1)

    h2 = _rmsnorm_fwd("norm_mix1", x2, row(w["l1_norm_mix"]))
    proj1 = _matmul("in_proj1", h2, w_in1, "nn")
    p_alog = jnp.zeros((1, 128), F32).at[0, 6:12].set(w["gdn_A_log"])
    p_dtb = jnp.zeros((1, 128), F32).at[0, 6:12].set(w["gdn_dt_bias"])
    gdn_consts = [full["gdn_conv_w"], p_alog, p_dtb, row(w["gdn_norm_w"])]
    gdn_xs = [(proj1, 768, 0), (proj1, 768, 1), (proj1, 768, 2), (proj1, 768, 3), (proj1, 128, 26)]
    gdn_states = [(8, 768), (8, 768), (8, 768), (768, 256)]
    mixed1, gdn_saved = _scan_fwd("gdn_fwd", _f_gdn, CHUNK, [], gdn_consts, gdn_xs, [], gdn_states, D_MODEL, 768, 0)
    s5_args = (w["s5_A_re"], w["s5_A_im"], w["s5_log_step"], w["s5_B_re"], w["s5_B_im"], w["s5_C_re"], w["s5_C_im"])
    (lam_re, lam_im, bblk, cc_re, cc_im), s5_prep_vjp = jax.vjp(_s5_prep, *s5_args)
    s5_consts = [lam_re, lam_im, bblk, cc_re, cc_im, row(w["s5_D"]), full["s5_w_glu"].astype(F32), row(w["s5_b_glu"])]
    s5_xs = [(proj1, 256, 12)]
    s5_states = [(8, 1024), (8, 1024)]
    mixed1, s5_saved = _scan_fwd("s5_fwd", _f_s5, CHUNK, [], s5_consts, s5_xs, [], s5_states, D_MODEL, 256, 3,
                                 y_alias=mixed1)
    x3 = _matmul("out_proj1", mixed1, full["l1_w_out"], "nn", epi="add", epi_arr=x2)
    h3 = _rmsnorm_fwd("norm_mlp1", x3, row(w["l1_norm_mlp"]))
    u1 = _matmul("up1", h3, full["l1_w_up"], "nn")
    x4 = _matmul("down1", u1, full["l1_w_down"], "nn", a_pro="relu2", epi="add", epi_arr=x3)

    loss_part, dx4, d_final = _loss_head("loss_head", x4, row(w["final_norm"]), target)
    loss = lax.psum(loss_part[0, 0], ("x", "y", "c"))
    grads = {"final_norm": d_final.reshape(-1)}

    du1 = _matmul("down1_dx", dx4, full["l1_w_down"], "nt", out_dtype=_MXU_DTYPE, epi="drelu2", epi_arr=u1)
    grads["l1_w_down"] = _matmul("down1_dw", u1, dx4, "tn", a_pro="relu2")
    grads["l1_w_up"] = _matmul("up1_dw", h3, du1, "tn")
    dh3 = _matmul("up1_dx", du1, full["l1_w_up"], "nt")
    dx3, dwn = _rmsnorm_bwd("norm_mlp1_bwd", dh3, x3, row(w["l1_norm_mlp"]), dx4)
    grads["l1_norm_mlp"] = dwn.reshape(-1)
    grads["l1_w_out"] = _matmul("out_proj1_dw", mixed1, dx3, "tn")
    dmixed1 = _matmul("out_proj1_dx", dx3, full["l1_w_out"], "nt")

    def gdn_assemble(dx):
        dq, dk, dv, dz, dba = dx
        zeros = lambda n: jnp.zeros((CHUNK, n), F32)
        return jnp.concatenate([dq, dk, dv, dz, zeros(256), dba, zeros(IN1_PAD - 3456)], axis=1)

    dproj1, gdn_dc = _scan_bwd("gdn_bwd", _f_gdn, CHUNK, [], gdn_consts, gdn_xs, [], gdn_saved, gdn_states,
                               (dmixed1, 768, 0), IN1_PAD, IN1_PAD, 0, gdn_assemble)
    dproj1, s5_dc = _scan_bwd("s5_bwd", _f_s5, CHUNK, [], s5_consts, s5_xs, [], s5_saved, s5_states,
                              (dmixed1, 256, 3), IN1_PAD, 256, 12, lambda dx: dx[0], dx_alias=dproj1)
    grads["gdn_conv_w"] = gdn_dc[0]
    grads["gdn_A_log"] = gdn_dc[1][0, 6:12]
    grads["gdn_dt_bias"] = gdn_dc[2][0, 6:12]
    grads["gdn_norm_w"] = gdn_dc[3].reshape(-1)
    s5_pg = s5_prep_vjp(tuple(s5_dc[:5]))
    for n, gval in zip(("s5_A_re", "s5_A_im", "s5_log_step", "s5_B_re", "s5_B_im", "s5_C_re", "s5_C_im"), s5_pg):
        grads[n] = gval
    grads["s5_D"] = s5_dc[5].reshape(-1)
    grads["s5_w_glu"] = s5_dc[6]
    grads["s5_b_glu"] = s5_dc[7].reshape(-1)
    dwi1 = _matmul("in_proj1_dw", h2, dproj1, "tn")
    grads["l1_w_in"] = jnp.concatenate([dwi1[:, :3072], dwi1[:, 3328:3340], dwi1[:, 3072:3328]], axis=1)
    dh2 = _matmul("in_proj1_dx", dproj1, w_in1, "nt")
    dx2, dwn = _rmsnorm_bwd("norm_mix1_bwd", dh2, x2, row(w["l1_norm_mix"]), dx3)
    grads["l1_norm_mix"] = dwn.reshape(-1)

    du0 = _matmul("down0_dx", dx2, full["l0_w_down"], "nt", out_dtype=_MXU_DTYPE, epi="drelu2", epi_arr=u0)
    grads["l0_w_down"] = _matmul("down0_dw", u0, dx2, "tn", a_pro="relu2")
    grads["l0_w_up"] = _matmul("up0_dw", h1, du0, "tn")
    dh1 = _matmul("up0_dx", du0, full["l0_w_up"], "nt")
    dx1, dwn = _rmsnorm_bwd("norm_mlp0_bwd", dh1, x1, row(w["l0_norm_mlp"]), dx2)
    grads["l0_norm_mlp"] = dwn.reshape(-1)
    grads["l0_w_out"] = _matmul("out_proj0_dw", mixed0, dx1, "tn")
    dmixed0 = _matmul("out_proj0_dx", dx1, full["l0_w_out"], "nt")
    dproj0, _ = _scan_bwd("ret_bwd", _f_ret, CHUNK, ret_tabs, [], ret_xs, ret_xt, ret_saved, ret_states,
                          (dmixed0, 512, 0), IN0_PAD, 2048, 0, lambda dx: jnp.concatenate(dx, axis=1))

    def ssd_assemble(dx):
        return jnp.concatenate(list(dx) + [jnp.zeros((CHUNK, 2048 - 1664), F32)], axis=1)

    dproj0, ssd_dc = _scan_bwd("ssd_bwd", _f_ssd, CHUNK, [expand], ssd_consts, ssd_xs, [], ssd_saved, ssd_states,
                               (dmixed0, 512, 1), IN0_PAD, 2048, 1, ssd_assemble, dx_alias=dproj0)
    heads = lambda a: a.reshape(SSD_HEADS, SSD_HEAD_DIM).sum(axis=1)
    grads["ssd_conv_w"] = ssd_dc[0]
    grads["ssd_conv_b"] = ssd_dc[1].reshape(-1)
    grads["ssd_dt_bias"] = heads(ssd_dc[2])
    grads["ssd_A_log"] = heads(ssd_dc[3])
    grads["ssd_D"] = heads(ssd_dc[4])
    grads["ssd_norm_w"] = ssd_dc[5].reshape(-1)
    grads["l0_w_in"] = _matmul("in_proj0_dw", h0, dproj0, "tn")[:, :in0]
    dh0 = _matmul("in_proj0_dx", dproj0, w_in0, "nt")
    dx0, dwn = _rmsnorm_bwd("norm_mix0_bwd", dh0, x0, row(w["l0_norm_mix"]), dx1)
    grads["l0_norm_mix"] = dwn.reshape(-1)
    grad_x = dx0.reshape(x.shape)

    c_idx = lax.axis_index("c").astype(jnp.int32).reshape(1)
    sections = []
    for s in range(4):
        sections.append(_pack([_shard_block(grads[n].reshape(_full_shape(n, w, kinds)), kinds[n], s, w[n].shape)
                               for n in names], F32))
    gslab = jnp.stack(sections)
    from_sibling = _swap_halves("grads_swap_halves", gslab)
    chip_sum = _add_halves("grads_add_sibling", gslab, from_sibling, c_idx)
    partials = _scatter_to_owners("grads_scatter", chip_sum)
    my_half = _add_four("grads_add_chips", partials)
    gsum = _join_halves("grads_join_halves", my_half)

    shapes = [w[n].shape for n in names]
    delta, new_m, new_v = _adamw("adamw", _pack([w[n] for n in names], F32), gsum,
                                 _pack([given["m_" + n] for n in names], F32),
                                 _pack([given["v_" + n] for n in names], F32))
    return (loss, grad_x, *_unpack(gsum, shapes), *_unpack(delta, shapes), *_unpack(new_m, shapes),
            *_unpack(new_v, shapes))


def _full_shape(name, w, kinds):
    shp = w[name].shape
    if kinds[name] == "col":
        return (shp[0], 4 * shp[1])
    if kinds[name] == "row":
        return (4 * shp[0],) + tuple(shp[1:])
    return shp
```

```python
import functools
import math

import jax
import jax.numpy as jnp
from jax import lax
from jax.experimental import pallas as pl
from jax.experimental.pallas import tpu as pltpu

F32 = jnp.float32
_MXU_DTYPE = jnp.bfloat16

D_MODEL = 1024
CHUNK = 64
EPS = 1e-6
RET_HEADS, RET_DK = 4, 128
ROPE_THETA = 10000.0
SSD_HEADS, SSD_HEAD_DIM = 8, 64
GDN_HEADS, GDN_DK = 6, 128
S5_GROUPS, S5_GROUP, S5_STATE = 16, 16, 64
ADAM_LR, ADAM_B1, ADAM_B2, ADAM_EPS, ADAM_WD, ADAM_STEP = 0.001, 0.9, 0.999, 1e-08, 0.01, 10

IN0_PAD = 4096
IN1_PAD = 3584
LANES = 1024
VMEM_LIMIT = 56 * 1024 * 1024
MESH = pl.DeviceIdType.MESH

PARAMS = (
    ("l0_norm_mix", "rep"), ("l0_w_in", "col"), ("ssd_conv_w", "col"), ("ssd_conv_b", "rep"),
    ("ssd_dt_bias", "rep"), ("ssd_A_log", "rep"), ("ssd_D", "rep"), ("ssd_norm_w", "rep"),
    ("l0_w_out", "row"), ("l0_norm_mlp", "rep"), ("l0_w_up", "col"), ("l0_w_down", "row"),
    ("l1_norm_mix", "rep"), ("l1_w_in", "col"), ("gdn_conv_w", "col"), ("gdn_A_log", "rep"),
    ("gdn_dt_bias", "rep"), ("gdn_norm_w", "rep"), ("s5_A_re", "rep"), ("s5_A_im", "rep"),
    ("s5_log_step", "rep"), ("s5_B_re", "rep"), ("s5_B_im", "rep"), ("s5_C_re", "rep"), ("s5_C_im", "rep"),
    ("s5_D", "rep"), ("s5_w_glu", "row"), ("s5_b_glu", "rep"), ("l1_w_out", "row"), ("l1_norm_mlp", "rep"),
    ("l1_w_up", "col"), ("l1_w_down", "row"), ("final_norm", "rep"),
)
GATHER_BF16 = ("l0_w_in", "l0_w_out", "l0_w_up", "l0_w_down", "l1_w_in", "l1_w_out", "l1_w_up", "l1_w_down", "s5_w_glu")
GATHER_F32 = ("ssd_conv_w", "gdn_conv_w")


def _dg(a, b, ca, cb, prec=None):
    return lax.dot_general(a, b, (((ca,), (cb,)), ((), ())), preferred_element_type=F32, precision=prec)


def _lo(a):
    return a.astype(_MXU_DTYPE)


@jax.custom_vjp
def _mm(a, b):
    return _dg(_lo(a), _lo(b), 1, 0)


def _mm_fwd(a, b):
    return _mm(a, b), (a, b)


def _mm_bwd(res, g):
    a, b = res
    return _dg(_lo(g), _lo(b), 1, 1), _dg(_lo(a), _lo(g), 0, 0)


_mm.defvjp(_mm_fwd, _mm_bwd)


@jax.custom_vjp
def _mm_nt(a, b):
    return _dg(_lo(a), _lo(b), 1, 1)


def _mm_nt_fwd(a, b):
    return _mm_nt(a, b), (a, b)


def _mm_nt_bwd(res, g):
    a, b = res
    return _dg(_lo(g), _lo(b), 1, 0), _dg(_lo(g), _lo(a), 0, 0)


_mm_nt.defvjp(_mm_nt_fwd, _mm_nt_bwd)


@jax.custom_vjp
def _mm_tn(a, b):
    return _dg(_lo(a), _lo(b), 0, 0)


def _mm_tn_fwd(a, b):
    return _mm_tn(a, b), (a, b)


def _mm_tn_bwd(res, g):
    a, b = res
    return _dg(_lo(b), _lo(g), 1, 1), _dg(_lo(a), _lo(g), 1, 0)


_mm_tn.defvjp(_mm_tn_fwd, _mm_tn_bwd)


def _split2(x):
    hi = _lo(x)
    return hi, _lo(x - hi.astype(F32))


def _split3(x):
    h1 = _lo(x)
    r1 = x - h1.astype(F32)
    h2 = _lo(r1)
    return h1, h2, _lo(r1 - h2.astype(F32))


def _tri_cum_dir(m, ca):
    n, w = m.shape
    causal, _ = _tri_masks(n)
    out = _dg(causal.astype(_MXU_DTYPE), jnp.concatenate(_split3(m), axis=1), ca, 0)
    return out[:, :w] + out[:, w:2 * w] + out[:, 2 * w:]


@jax.custom_vjp
def _tri_cum(m):
    return _tri_cum_dir(m, 1)


def _tri_cum_fwd(m):
    return _tri_cum_dir(m, 1), None


def _tri_cum_bwd(_, g):
    return (_tri_cum_dir(g, 0),)


_tri_cum.defvjp(_tri_cum_fwd, _tri_cum_bwd)


@jax.custom_vjp
def _mm_exact_rhs(a, e):
    return _dg(jnp.concatenate(_split3(a), axis=1), jnp.concatenate([_lo(e)] * 3, axis=0), 1, 0)


def _mm_exact_rhs_fwd(a, e):
    return _mm_exact_rhs(a, e), e


def _mm_exact_rhs_bwd(e, g):
    return _dg(jnp.concatenate(_split3(g), axis=1), jnp.concatenate([_lo(e)] * 3, axis=1), 1, 1), jnp.zeros_like(e)


_mm_exact_rhs.defvjp(_mm_exact_rhs_fwd, _mm_exact_rhs_bwd)


def _bd(x):
    left = _iota(x.shape, 1) < (x.shape[1] // 2)
    zero = jnp.zeros_like(x)
    return jnp.concatenate([jnp.where(left, x, zero), jnp.where(left, zero, x)], axis=0)


def _unbd(m):
    half = m.shape[0] // 2
    left = _iota((half, m.shape[1]), 1) < (m.shape[1] // 2)
    return jnp.where(left, m[:half], m[half:])


def _pmm_nn(x, y):
    xh, xl = _split2(x)
    yh, yl = _split2(y)
    return _dg(jnp.concatenate([xh, xl, xh], axis=1), jnp.concatenate([_bd(yh), _bd(yh), _bd(yl)], axis=0), 1, 0)


def _pmm_nt(x, y):
    xh, xl = _split2(x)
    yh, yl = _split2(y)
    return _dg(jnp.concatenate([xh, xl, xh], axis=1), jnp.concatenate([_bd(yh), _bd(yh), _bd(yl)], axis=1), 1, 1)


def _pmm_tn(x, y):
    xh, xl = _split2(x)
    yh, yl = _split2(y)
    return _unbd(_dg(jnp.concatenate([xh, xl, xh], axis=0), jnp.concatenate([yh, yh, yl], axis=0), 0, 0))


@functools.lru_cache(maxsize=None)
def _shift(s, axis):
    @jax.custom_vjp
    def sh(x):
        return pltpu.roll(x, s, axis)

    def fwd(x):
        return sh(x), None

    def bwd(_, g):
        n = g.shape[axis]
        return (pltpu.roll(g, (n - s) % n, axis),)

    sh.defvjp(fwd, bwd)
    return sh


def _iota(shape, axis):
    return lax.broadcasted_iota(jnp.int32, shape, axis)


def _silu(x):
    return x * jax.nn.sigmoid(x)


def _unit_rms(x):
    return x * lax.rsqrt(jnp.mean(x * x, axis=-1, keepdims=True) + EPS)


def _l2norm(x):
    return x * lax.rsqrt(jnp.sum(x * x, axis=-1, keepdims=True) + EPS)


def _tri_masks(n):
    r, c = _iota((n, n), 0), _iota((n, n), 1)
    return r >= c, r > c


def _packed_rc():
    return _iota((CHUNK, 2 * CHUNK), 0), _iota((CHUNK, 2 * CHUNK), 1) & (CHUNK - 1)


def _decay_packed(g_packed):
    r, c = _packed_rc()
    seg = _tri_cum(g_packed * (r > c).astype(F32))
    return jnp.where(r >= c, jnp.exp(jnp.where(r >= c, seg, 0.0)), 0.0)


def _conv(x, tail, w):
    rows, width = x.shape
    row = _iota((rows, width), 0)
    acc = x * w[3:4, :]
    pad = jnp.zeros((rows - 8, width), F32)
    for j in range(3):
        s = 3 - j
        prev = jnp.concatenate([_shift(s, 0)(tail), pad], axis=0)
        acc = acc + w[j:j + 1, :] * jnp.where(row < s, prev, _shift(s, 0)(x))
    return acc


def _tri_inv_impl(mats):
    r, c = _packed_rc()
    eye = (r == c).astype(F32)

    def same_block(b):
        return (r // b) == (c // b)

    a8 = [jnp.where(same_block(8), a, 0.0) for a in mats]
    a2 = [_pmm_nn(t, t) for t in a8]
    a4 = [_pmm_nn(t, t) for t in a2]
    x = [_pmm_nn(eye - p, eye + q) for p, q in zip(a8, a2)]
    x = [_pmm_nn(p, eye + q) for p, q in zip(x, a4)]
    for b in (8, 16, 32):
        off = [jnp.where(same_block(2 * b) & jnp.logical_not(same_block(b)), a, 0.0) for a in mats]
        y = [_pmm_nn(p, q) for p, q in zip(x, off)]
        x = [p - _pmm_nn(q, p) for p, q in zip(x, y)]
    return x


@jax.custom_vjp
def _tri_inv(mats):
    return _tri_inv_impl(mats)


def _tri_inv_fwd(mats):
    t = _tri_inv_impl(mats)
    return t, t


def _tri_inv_bwd(t, g):
    m1 = [_pmm_tn(p, q) for p, q in zip(t, g)]
    return ([-_pmm_nt(p, q) for p, q in zip(m1, t)],)


_tri_inv.defvjp(_tri_inv_fwd, _tri_inv_bwd)


def _f_ret(tabs, consts, xs, xtabs, states):
    dmask, kdec, qdec, cdec = tabs
    q, k, v, gate = xs
    cs, sn = xtabs
    (st,) = states
    swap = _shift(RET_DK // 2, 1)
    heads = range(RET_HEADS)
    sls = [slice(128 * h, 128 * h + 128) for h in heads]
    qh = [(q[:, sl] * cs + swap(q[:, sl]) * sn) * (RET_DK ** -0.5) for sl in sls]
    kh = [k[:, sl] * cs + swap(k[:, sl]) * sn for sl in sls]
    sh = [st[sl, :] for sl in sls]
    scores = [_mm_nt(a, b) * dmask[64 * h:64 * h + 64, :] for h, a, b in zip(heads, qh, kh)]
    y = [_mm(s, v[:, sl]) for s, sl in zip(scores, sls)]
    y = [t + _mm(a * qdec[:, sl], s) for t, a, sl, s in zip(y, qh, sls, sh)]
    new = [s * cdec[:, sl] + _mm_tn(b * kdec[:, sl], v[:, sl]) for s, sl, b in zip(sh, sls, kh)]
    outs = [_silu(gate[:, sl]) * _unit_rms(t) for sl, t in zip(sls, y)]
    return (jnp.concatenate(outs, axis=1),), [jnp.concatenate(new, axis=0)]


def _f_ssd(tabs, consts, xs, xtabs, states):
    (expand,) = tabs
    conv_w, conv_b, dtb, alog, dskip, nw = consts
    z, xr, br, cr, dtr = xs
    tx, tb, tc, st = states
    xc = _silu(_conv(xr, tx, conv_w[:, 0:512]) + conv_b[:, 0:512])
    bc = _silu(_conv(br, tb, conv_w[:, 512:768]) + conv_b[:, 512:768])
    cc = _silu(_conv(cr, tc, conv_w[:, 768:1024]) + conv_b[:, 768:1024])
    dt = jax.nn.softplus(_mm_exact_rhs(dtr, expand) + dtb)
    la = dt * (-jnp.exp(alog))
    lacum = _tri_cum(la)
    total = jnp.sum(la, axis=0, keepdims=True)
    xd = xc * dt
    dte, ecum, cdec = jnp.exp(total - lacum), jnp.exp(lacum), jnp.exp(total)
    pairs = range(SSD_HEADS // 2)
    sls = [slice(128 * p, 128 * p + 128) for p in pairs]
    bg = [bc[:, 128 * g:128 * g + 128] for g in range(2)]
    cg = [cc[:, 128 * g:128 * g + 128] for g in range(2)]
    cb2 = [_mm_nt(c, jnp.concatenate([b, b], axis=0)) for b, c in zip(bg, cg)]
    lm = [_decay_packed(la[:, sl]) for sl in sls]
    sp = [st[sl, :] for sl in sls]
    ys = [_mm(cg[p // 2], sp[p]) * ecum[:, sls[p]] for p in pairs]
    ys = [ys[p] + _mm(cb2[p // 2] * lm[p], _bd(xd[:, sls[p]])) for p in pairs]
    new = [sp[p] * cdec[:, sls[p]] + _mm_tn(bg[p // 2], xd[:, sls[p]] * dte[:, sls[p]]) for p in pairs]
    y = jnp.concatenate(ys, axis=1) + dskip * xc
    yg = y * _silu(z)
    out = jnp.concatenate([_unit_rms(yg[:, 0:256]), _unit_rms(yg[:, 256:512])], axis=1) * nw
    return (out,), [xr[CHUNK - 8:, :], br[CHUNK - 8:, :], cr[CHUNK - 8:, :], jnp.concatenate(new, axis=0)]


def _f_gdn(tabs, consts, xs, xtabs, states):
    conv_w, p_alog, p_dtb, nw = consts
    qr, kr, vr, z, ba = xs
    tq, tk, tv, st = states
    qc = _silu(_conv(qr, tq, conv_w[:, 0:768]))
    kc = _silu(_conv(kr, tk, conv_w[:, 768:1536]))
    vc = _silu(_conv(vr, tv, conv_w[:, 1536:2304]))
    gl = -jnp.exp(p_alog) * jax.nn.softplus(ba + p_dtb)
    bl = jax.nn.sigmoid(ba)
    gcum = _tri_cum(gl)
    left128 = _iota((CHUNK, 128), 1) < 64
    left256 = _iota((CHUNK, 256), 1) < 128
    r, c = _packed_rc()
    diag_blocks = (_iota((256, 256), 0) < 128) == (_iota((256, 256), 1) < 128)

    def norm2(t):
        return jnp.concatenate([_l2norm(t[:, 0:128]), _l2norm(t[:, 128:256])], axis=1)

    def pick(arr, off, left, p):
        return jnp.where(left, arr[:, off + 2 * p:off + 2 * p + 1], arr[:, off + 2 * p + 1:off + 2 * p + 2])

    pairs = range(GDN_HEADS // 2)
    sls = [slice(256 * p, 256 * p + 256) for p in pairs]
    qn = [norm2(qc[:, sl]) * (GDN_DK ** -0.5) for sl in sls]
    kn = [norm2(kc[:, sl]) for sl in sls]
    dec = [_decay_packed(pick(gl, 6, left128, p)) for p in pairs]
    g2 = [pick(gl, 6, left256, p) for p in pairs]
    gc2 = [pick(gcum, 6, left256, p) for p in pairs]
    b2 = [pick(bl, 0, left256, p) for p in pairs]
    tot = [jnp.sum(t, axis=0, keepdims=True) for t in g2]
    eg = [jnp.exp(t) for t in gc2]
    et = [jnp.exp(t - s) for t, s in zip(tot, gc2)]
    cd = [jnp.exp(t) for t in tot]
    kb = [k * b for k, b in zip(kn, b2)]
    vb = [vc[:, sl] * b for sl, b in zip(sls, b2)]
    kbd = [_bd(k) for k in kn]
    tm = _tri_inv([jnp.where(r > c, _mm_nt(a, b) * d, 0.0) for a, b, d in zip(kb, kbd, dec)])
    u = [_mm(t, _bd(v)) for t, v in zip(tm, vb)]
    w = [_mm(t, _bd(k * e)) for t, k, e in zip(tm, kb, eg)]
    attn = [_mm_nt(q, k) * d for q, k, d in zip(qn, kbd, dec)]
    sp = [st[sl, :] for sl in sls]
    vn = [a - _mm(b, s) for a, b, s in zip(u, w, sp)]
    o = [_mm(q * e, s) + _mm(a, _bd(v)) for q, e, s, a, v in zip(qn, eg, sp, attn, vn)]
    new = [s * d + jnp.where(diag_blocks, _mm_tn(k * e, v), 0.0) for s, d, k, e, v in zip(sp, cd, kn, et, vn)]
    outs = []
    for p in pairs:
        for hh in range(2):
            osl = slice(128 * hh, 128 * hh + 128)
            zsl = slice(256 * p + 128 * hh, 256 * p + 128 * hh + 128)
            outs.append(_unit_rms(o[p][:, osl]) * nw * _silu(z[:, zsl]))
    return (jnp.concatenate(outs, axis=1),), [qr[CHUNK - 8:, :], kr[CHUNK - 8:, :], vr[CHUNK - 8:, :],
                                             jnp.concatenate(new, axis=0)]


def _f_s5(tabs, consts, xs, xtabs, states):
    lam_re, lam_im, bblk, c_re, c_im, dskip, wglu, bglu = consts
    (u,) = xs
    s_re, s_im = states
    rows = u.shape[0]
    n = lam_re.shape[1]
    bu = _mm(u, bblk)
    hr, hi = bu[:, 0:n], bu[:, n:2 * n]
    row = _iota((rows, n), 0)
    h0r, h0i = s_re[0:1, :], s_im[0:1, :]
    hr = hr + jnp.where(row == 0, lam_re * h0r - lam_im * h0i, 0.0)
    hi = hi + jnp.where(row == 0, lam_re * h0i + lam_im * h0r, 0.0)
    pr, pi = lam_re, lam_im
    d = 1
    while d < rows:
        sr = jnp.where(row >= d, _shift(d, 0)(hr), 0.0)
        si = jnp.where(row >= d, _shift(d, 0)(hi), 0.0)
        hr, hi = hr + pr * sr - pi * si, hi + pr * si + pi * sr
        pr, pi = pr * pr - pi * pi, 2.0 * pr * pi
        d *= 2
    y = _mm(hr, c_re) - _mm(hi, c_im) + dskip * u
    y = jax.nn.gelu(y)
    out = y * jax.nn.sigmoid(_mm(y, wglu) + bglu)
    last_r = jnp.broadcast_to(hr[rows - 1:rows, :], (8, n))
    last_i = jnp.broadcast_to(hi[rows - 1:rows, :], (8, n))
    return (out,), [last_r, last_i]


def _full_spec(a):
    nd = a.ndim
    return pl.BlockSpec(a.shape, lambda i, _nd=nd: (0,) * _nd)


CHUNKS_PER_STEP = 4


def _chunks_per_step(f, rows, n):
    def g(tabs, consts, xs, xtabs, states):
        ys = []
        for i in range(n):
            sl = slice(rows * i, rows * (i + 1))
            (y,), states = f(tabs, consts, [t[sl] for t in xs], [t[sl] for t in xtabs], states)
            ys.append(y)
        return (jnp.concatenate(ys, axis=0),), states

    return g


def _scan_fwd(name, f, rows, tabs, consts, xs, xtabs, state_shapes, y_total, y_width, y_cb, y_alias=None):
    seq = xs[0][0].shape[0]
    per_step = math.gcd(CHUNKS_PER_STEP, seq // rows)
    f = _chunks_per_step(f, rows, per_step)
    rows = rows * per_step
    nc = seq // rows
    nt, ncst, nx, nxt, ns = len(tabs), len(consts), len(xs), len(xtabs), len(state_shapes)
    alias = y_alias is not None

    def body(*refs):
        p = 0
        tab_r = refs[p:p + nt]; p += nt
        c_r = refs[p:p + ncst]; p += ncst
        x_r = refs[p:p + nx]; p += nx
        xt_r = refs[p:p + nxt]; p += nxt
        if alias:
            p += 1
        y_ref = refs[p]; p += 1
        sv_r = refs[p:p + ns]; p += ns
        st_r = refs[p:p + ns]

        @pl.when(pl.program_id(0) == 0)
        def _():
            for s in st_r:
                s[...] = jnp.zeros(s.shape, F32)

        st = [s[...] for s in st_r]
        for r, v in zip(sv_r, st):
            r[...] = v
        (y,), new = f([r[...] for r in tab_r], [r[...] for r in c_r], [r[...] for r in x_r],
                      [r[...] for r in xt_r], st)
        y_ref[...] = y
        for s, v in zip(st_r, new):
            s[...] = v

    win = [pl.BlockSpec((rows, w), lambda i, _cb=cb: (i, _cb)) for (_, w, cb) in list(xs) + list(xtabs)]
    in_specs = [_full_spec(a) for a in list(tabs) + list(consts)] + win
    args = list(tabs) + list(consts) + [a for (a, _, _) in list(xs) + list(xtabs)]
    io_alias = {}
    if alias:
        in_specs.append(pl.BlockSpec(memory_space=pl.ANY))
        io_alias = {len(args): 0}
        args.append(y_alias)
    out_shape = [jax.ShapeDtypeStruct((seq, y_total), F32)]
    out_specs = [pl.BlockSpec((rows, y_width), lambda i: (i, y_cb))]
    for (r, c) in state_shapes:
        out_shape.append(jax.ShapeDtypeStruct((nc * r, c), F32))
        out_specs.append(pl.BlockSpec((r, c), lambda i: (i, 0)))
    res = pl.pallas_call(
        body, name=name, grid=(nc,), in_specs=in_specs, out_specs=out_specs, out_shape=out_shape,
        scratch_shapes=[pltpu.VMEM(s, F32) for s in state_shapes], input_output_aliases=io_alias,
        compiler_params=pltpu.CompilerParams(dimension_semantics=("arbitrary",), vmem_limit_bytes=VMEM_LIMIT),
    )(*args)
    return res[0], list(res[1:])


def _scan_bwd(name, f, rows, tabs, consts, xs, xtabs, saved, state_shapes, dy, dx_total, dx_width, dx_cb,
              assemble, dx_alias=None):
    seq = xs[0][0].shape[0]
    per_step = math.gcd(CHUNKS_PER_STEP, seq // rows)
    f = _chunks_per_step(f, rows, per_step)
    rows = rows * per_step
    nc = seq // rows
    nt, ncst, nx, nxt, ns = len(tabs), len(consts), len(xs), len(xtabs), len(state_shapes)
    alias = dx_alias is not None

    def body(*refs):
        p = 0
        tab_r = refs[p:p + nt]; p += nt
        c_r = refs[p:p + ncst]; p += ncst
        x_r = refs[p:p + nx]; p += nx
        xt_r = refs[p:p + nxt]; p += nxt
        sv_r = refs[p:p + ns]; p += ns
        dy_ref = refs[p]; p += 1
        if alias:
            p += 1
        dx_ref = refs[p]; p += 1
        dc_r = refs[p:p + ncst]; p += ncst
        ds_r = refs[p:p + ns]

        @pl.when(pl.program_id(0) == 0)
        def _():
            for s in ds_r:
                s[...] = jnp.zeros(s.shape, F32)
            for r in dc_r:
                r[...] = jnp.zeros(r.shape, F32)

        tab_v = [r[...] for r in tab_r]
        xt_v = [r[...] for r in xt_r]

        def g(c, x, s):
            (y,), new = f(tab_v, c, x, xt_v, s)
            return y, new

        _, vjp = jax.vjp(g, [r[...] for r in c_r], [r[...] for r in x_r], [r[...] for r in sv_r])
        dc, dx, ds = vjp((dy_ref[...], [s[...] for s in ds_r]))
        dx_ref[...] = assemble(dx)
        for r, v in zip(dc_r, dc):
            r[...] += v
        for s, v in zip(ds_r, ds):
            s[...] = v

    win = [pl.BlockSpec((rows, w), lambda j, _cb=cb: (nc - 1 - j, _cb)) for (_, w, cb) in list(xs) + list(xtabs)]
    in_specs = [_full_spec(a) for a in list(tabs) + list(consts)] + win
    args = list(tabs) + list(consts) + [a for (a, _, _) in list(xs) + list(xtabs)]
    for (r, c), sv in zip(state_shapes, saved):
        in_specs.append(pl.BlockSpec((r, c), lambda j: (nc - 1 - j, 0)))
        args.append(sv)
    in_specs.append(pl.BlockSpec((rows, dy[1]), lambda j: (nc - 1 - j, dy[2])))
    args.append(dy[0])
    io_alias = {}
    if alias:
        in_specs.append(pl.BlockSpec(memory_space=pl.ANY))
        io_alias = {len(args): 0}
        args.append(dx_alias)
    out_shape = [jax.ShapeDtypeStruct((seq, dx_total), F32)] + [jax.ShapeDtypeStruct(a.shape, F32) for a in consts]
    out_specs = [pl.BlockSpec((rows, dx_width), lambda j: (nc - 1 - j, dx_cb))] + [_full_spec(a) for a in consts]
    res = pl.pallas_call(
        body, name=name, grid=(nc,), in_specs=in_specs, out_specs=out_specs, out_shape=out_shape,
        scratch_shapes=[pltpu.VMEM(s, F32) for s in state_shapes], input_output_aliases=io_alias,
        compiler_params=pltpu.CompilerParams(dimension_semantics=("arbitrary",), vmem_limit_bytes=VMEM_LIMIT),
    )(*args)
    return res[0], list(res[1:])


def _tile(n, want):
    t = min(n, want)
    while n % t:
        t //= 2
    return t


MATMUL_VMEM_BUDGET = 40 * 1024 * 1024


def _pick_tiles(m, n, k, sa, sb, so, se):
    best = None
    for tn in {_tile(n, 1024), _tile(n, 512)}:
        for tm in {_tile(m, t) for t in (2048, 1024, 512)}:
            for tk in {_tile(k, t) for t in (4096, 2048, 1024, 512)}:
                at, bt, ot = tm * tk * sa, tk * tn * sb, tm * tn * so
                need = 2 * (at + bt + ot + tm * tn * se) + 2 * tm * tn * 4 + (at if sa == 4 else 0) + (bt if sb == 4 else 0)
                if need > MATMUL_VMEM_BUDGET:
                    continue
                key = ((m // tm) * (n // tn) * (k // tk), k // tk, -tm, -tn)
                if best is None or key < best[0]:
                    best = (key, (tm, tn, tk))
    assert best is not None, (m, n, k)
    return best[1]


def _matmul(name, a, b, mode, out_dtype=F32, a_pro=None, epi=None, epi_arr=None):
    if mode == "nn":
        (m, k), (k2, n) = a.shape, b.shape
    elif mode == "nt":
        (m, k), (n, k2) = a.shape, b.shape
    else:
        (k, m), (k2, n) = a.shape, b.shape
    assert k == k2, (name, a.shape, b.shape)
    size = lambda t: jnp.dtype(t).itemsize
    tm, tn, tk = _pick_tiles(m, n, k, size(a.dtype), size(b.dtype), size(out_dtype),
                             0 if epi is None else size(epi_arr.dtype))
    nk = k // tk
    ca, cb = {"nn": (1, 0), "nt": (1, 1), "tn": (0, 0)}[mode]

    def body(*refs):
        refs = list(refs)
        acc = refs.pop() if nk > 1 else None
        a_ref, b_ref = refs[0], refs[1]
        e_ref = refs[2] if epi is not None else None
        o_ref = refs[-1]

        av = a_ref[...]
        if a_pro == "relu2":
            r = jnp.maximum(av, 0.0)
            av = r * r
        part = _dg(_lo(av), _lo(b_ref[...]), ca, cb)

        def finish(r):
            if epi == "add":
                r = r + e_ref[...]
            elif epi == "drelu2":
                r = r * (2.0 * jnp.maximum(e_ref[...], 0.0))
            o_ref[...] = r.astype(out_dtype)

        if nk == 1:
            finish(part)
        else:
            kk = pl.program_id(2)

            @pl.when(kk == 0)
            def _():
                acc[...] = part

            @pl.when(kk > 0)
            def _():
                acc[...] += part

            @pl.when(kk == nk - 1)
            def _():
                finish(acc[...])

    if mode == "tn":
        a_spec = pl.BlockSpec((tk, tm), lambda j, i, kk: (kk, i))
    else:
        a_spec = pl.BlockSpec((tm, tk), lambda j, i, kk: (i, kk))
    if mode == "nt":
        b_spec = pl.BlockSpec((tn, tk), lambda j, i, kk: (j, kk))
    else:
        b_spec = pl.BlockSpec((tk, tn), lambda j, i, kk: (kk, j))
    o_spec = pl.BlockSpec((tm, tn), lambda j, i, kk: (i, j))
    in_specs, args = [a_spec, b_spec], [a, b]
    if epi is not None:
        in_specs.append(o_spec)
        args.append(epi_arr)
    return pl.pallas_call(
        body, name=name, grid=(n // tn, m // tm, nk), in_specs=in_specs, out_specs=o_spec,
        out_shape=jax.ShapeDtypeStruct((m, n), out_dtype),
        scratch_shapes=[pltpu.VMEM((tm, tn), F32)] if nk > 1 else [],
        compiler_params=pltpu.CompilerParams(dimension_semantics=("parallel", "parallel", "arbitrary"),
                                             vmem_limit_bytes=VMEM_LIMIT),
    )(*args)


ROW_TILE = 512


def _rmsnorm_fwd(name, x, w):
    seq, d = x.shape
    tr = _tile(seq, ROW_TILE)

    def body(x_ref, w_ref, o_ref):
        xv = x_ref[...]
        o_ref[...] = (_unit_rms(xv) * w_ref[...]).astype(_MXU_DTYPE)

    return pl.pallas_call(
        body, name=name, grid=(seq // tr,),
        in_specs=[pl.BlockSpec((tr, d), lambda i: (i, 0)), pl.BlockSpec((1, d), lambda i: (0, 0))],
        out_specs=pl.BlockSpec((tr, d), lambda i: (i, 0)), out_shape=jax.ShapeDtypeStruct((seq, d), _MXU_DTYPE),
        compiler_params=pltpu.CompilerParams(dimension_semantics=("parallel",), vmem_limit_bytes=VMEM_LIMIT),
    )(x, w)


def _rmsnorm_bwd(name, dh, x, w, dres):
    seq, d = x.shape
    tr = _tile(seq, ROW_TILE)

    def body(dh_ref, x_ref, w_ref, dres_ref, dx_ref, dw_ref):
        @pl.when(pl.program_id(0) == 0)
        def _():
            dw_ref[...] = jnp.zeros(dw_ref.shape, F32)

        xv = x_ref[...]
        rstd = lax.rsqrt(jnp.mean(xv * xv, axis=-1, keepdims=True) + EPS)
        xh = xv * rstd
        dhv = dh_ref[...]
        g = dhv * w_ref[...]
        dx_ref[...] = dres_ref[...] + rstd * (g - xh * jnp.mean(g * xh, axis=-1, keepdims=True))
        dw_ref[...] += jnp.sum(dhv * xh, axis=0, keepdims=True)

    row = pl.BlockSpec((tr, d), lambda i: (i, 0))
    vec = pl.BlockSpec((1, d), lambda i: (0, 0))
    return pl.pallas_call(
        body, name=name, grid=(seq // tr,), in_specs=[row, row, vec, row], out_specs=[row, vec],
        out_shape=[jax.ShapeDtypeStruct((seq, d), F32), jax.ShapeDtypeStruct((1, d), F32)],
        compiler_params=pltpu.CompilerParams(dimension_semantics=("arbitrary",), vmem_limit_bytes=VMEM_LIMIT),
    )(dh, x, w, dres)


def _loss_head(name, x, w, target):
    seq, d = x.shape
    tr = _tile(seq, ROW_TILE)

    def body(x_ref, w_ref, t_ref, loss_ref, dx_ref, dw_ref):
        @pl.when(pl.program_id(0) == 0)
        def _():
            dw_ref[...] = jnp.zeros(dw_ref.shape, F32)
            loss_ref[...] = jnp.zeros(loss_ref.shape, F32)

        xv = x_ref[...]
        rstd = lax.rsqrt(jnp.mean(xv * xv, axis=-1, keepdims=True) + EPS)
        xh = xv * rstd
        err = xh * w_ref[...] - t_ref[...]
        per_row = jnp.mean(err * err, axis=-1, keepdims=True)
        loss_ref[...] += 0.5 * jnp.sum(per_row, axis=0, keepdims=True)
        dy = err * (1.0 / d)
        g = dy * w_ref[...]
        dx_ref[...] = rstd * (g - xh * jnp.mean(g * xh, axis=-1, keepdims=True))
        dw_ref[...] += jnp.sum(dy * xh, axis=0, keepdims=True)

    row = pl.BlockSpec((tr, d), lambda i: (i, 0))
    vec = pl.BlockSpec((1, d), lambda i: (0, 0))
    one = pl.BlockSpec((1, 1), lambda i: (0, 0))
    return pl.pallas_call(
        body, name=name, grid=(seq // tr,), in_specs=[row, vec, row], out_specs=[one, row, vec],
        out_shape=[jax.ShapeDtypeStruct((1, 1), F32), jax.ShapeDtypeStruct((seq, d), F32),
                   jax.ShapeDtypeStruct((1, d), F32)],
        compiler_params=pltpu.CompilerParams(dimension_semantics=("arbitrary",), vmem_limit_bytes=VMEM_LIMIT),
    )(x, w, target)


SLAB_TILE_ROWS = 1024


def _slab_tile(rows, cap=SLAB_TILE_ROWS):
    return max(t for t in range(8, min(rows, cap) + 1, 8) if rows % t == 0)


def _adamw(name, w, g, m, v):
    rows = w.shape[0]
    tr = _slab_tile(rows, SLAB_TILE_ROWS // 2)

    def body(w_ref, g_ref, m_ref, v_ref, d_ref, nm_ref, nv_ref):
        gv = g_ref[...]
        nm = ADAM_B1 * m_ref[...] + (1.0 - ADAM_B1) * gv
        nv = ADAM_B2 * v_ref[...] + (1.0 - ADAM_B2) * (gv * gv)
        m_hat = nm / (1.0 - ADAM_B1 ** ADAM_STEP)
        v_hat = nv / (1.0 - ADAM_B2 ** ADAM_STEP)
        d_ref[...] = -ADAM_LR * (m_hat / (jnp.sqrt(v_hat) + ADAM_EPS) + ADAM_WD * w_ref[...])
        nm_ref[...] = nm
        nv_ref[...] = nv

    spec = pl.BlockSpec((tr, LANES), lambda i: (i, 0))
    sds = jax.ShapeDtypeStruct(w.shape, F32)
    return pl.pallas_call(
        body, name=name, grid=(rows // tr,), in_specs=[spec] * 4, out_specs=[spec] * 3, out_shape=[sds] * 3,
        compiler_params=pltpu.CompilerParams(dimension_semantics=("parallel",), vmem_limit_bytes=VMEM_LIMIT),
    )(w, g, m, v)


def _add_halves(name, g, t1, c):
    nsec, rows, _ = g.shape
    rh = rows // 2
    tr = _slab_tile(rh)
    nb = rh // tr

    def body(c_ref, g_ref, t_ref, o_ref):
        o_ref[...] = g_ref[...] + t_ref[...]

    gs = pltpu.PrefetchScalarGridSpec(
        num_scalar_prefetch=1, grid=(nsec, nb),
        in_specs=[pl.BlockSpec((1, tr, LANES), lambda s, i, c_ref: (s, c_ref[0] * nb + i, 0)),
                  pl.BlockSpec((1, tr, LANES), lambda s, i, c_ref: (s, i, 0))],
        out_specs=pl.BlockSpec((1, tr, LANES), lambda s, i, c_ref: (s, i, 0)))
    return pl.pallas_call(
        body, name=name, grid_spec=gs, out_shape=jax.ShapeDtypeStruct((nsec, rh, LANES), F32),
        compiler_params=pltpu.CompilerParams(dimension_semantics=("parallel", "parallel"),
                                             vmem_limit_bytes=VMEM_LIMIT),
    )(c, g, t1)


def _add_four(name, t2):
    _, rh, _ = t2.shape
    tr = _slab_tile(rh)

    def body(t_ref, o_ref):
        o_ref[...] = ((t_ref[0] + t_ref[1]) + t_ref[2]) + t_ref[3]

    return pl.pallas_call(
        body, name=name, grid=(rh // tr,), in_specs=[pl.BlockSpec((4, tr, LANES), lambda i: (0, i, 0))],
        out_specs=pl.BlockSpec((tr, LANES), lambda i: (i, 0)), out_shape=jax.ShapeDtypeStruct((rh, LANES), F32),
        compiler_params=pltpu.CompilerParams(dimension_semantics=("parallel",), vmem_limit_bytes=VMEM_LIMIT),
    )(t2)


ANY = pl.BlockSpec(memory_space=pl.ANY)


def _place():
    return lax.axis_index("x"), lax.axis_index("y"), lax.axis_index("c")


def _all_gather_shards(name, slab):
    rows = slab.shape[0]
    rh = rows // 2

    def body(x_ref, out_ref, send_sems, recv_sems, local_sem):
        x, y, c = _place()
        sibling = (x, y, 1 - c)
        chips = [(1 - x, y), (x, 1 - y), (1 - x, 1 - y)]

        def part(px, py, pc):
            return out_ref.at[2 * px + py, pl.ds(pc * rh, rh), :]

        def copy(k, block, to, src=None):
            return pltpu.make_async_remote_copy(
                src_ref=part(*block) if src is None else src, dst_ref=part(*block),
                send_sem=send_sems.at[k], recv_sem=recv_sems.at[k], device_id=to, device_id_type=MESH)

        mine = pltpu.make_async_copy(x_ref, out_ref.at[2 * x + y], local_sem)
        mine.start()
        my_half = x_ref.at[pl.ds(c * rh, rh), :]
        first = [copy(j, (x, y, c), (*chip, c), src=my_half) for j, chip in enumerate(chips)]
        for cp in first:
            cp.start()
        passed = [copy(3 + j, (*chip, c), sibling) for j, chip in enumerate(chips)]
        for j, chip in enumerate(chips):
            copy(j, (*chip, c), (x, y, c)).wait_recv()
            passed[j].start()
        for j, chip in enumerate(chips):
            copy(3 + j, (*chip, 1 - c), (x, y, c)).wait_recv()
        for cp in first + passed:
            cp.wait_send()
        mine.wait()

    return pl.pallas_call(
        body, name=name, in_specs=[ANY], out_specs=ANY,
        out_shape=jax.ShapeDtypeStruct((4, rows, LANES), slab.dtype),
        scratch_shapes=[pltpu.SemaphoreType.DMA((6,)), pltpu.SemaphoreType.DMA((6,)), pltpu.SemaphoreType.DMA],
    )(slab)


def _swap_halves(name, g):
    nsec, rows, _ = g.shape
    rh = rows // 2

    def body(g_ref, t_ref, send_sem, recv_sem):
        x, y, c = _place()
        cp = pltpu.make_async_remote_copy(
            src_ref=g_ref.at[:, pl.ds((1 - c) * rh, rh), :], dst_ref=t_ref, send_sem=send_sem, recv_sem=recv_sem,
            device_id=(x, y, 1 - c), device_id_type=MESH)
        cp.start()
        cp.wait()

    return pl.pallas_call(
        body, name=name, in_specs=[ANY], out_specs=ANY, out_shape=jax.ShapeDtypeStruct((nsec, rh, LANES), F32),
        scratch_shapes=[pltpu.SemaphoreType.DMA, pltpu.SemaphoreType.DMA],
    )(g)


def _scatter_to_owners(name, p):
    _, rh, _ = p.shape

    def body(p_ref, t_ref, send_sems, recv_sems, local_sem):
        x, y, c = _place()
        me = 2 * x + y
        chips = [(1 - x, y), (x, 1 - y), (1 - x, 1 - y)]
        mine = pltpu.make_async_copy(p_ref.at[me], t_ref.at[me], local_sem)
        mine.start()
        sends = []
        for j, (cx, cy) in enumerate(chips):
            sends.append(pltpu.make_async_remote_copy(
                src_ref=p_ref.at[2 * cx + cy], dst_ref=t_ref.at[me], send_sem=send_sems.at[j],
                recv_sem=recv_sems.at[j], device_id=(cx, cy, c), device_id_type=MESH))
        for cp in sends:
            cp.start()
        for j, (cx, cy) in enumerate(chips):
            pltpu.make_async_remote_copy(
                src_ref=p_ref.at[me], dst_ref=t_ref.at[2 * cx + cy], send_sem=send_sems.at[j],
                recv_sem=recv_sems.at[j], device_id=(cx, cy, c), device_id_type=MESH).wait_recv()
        for cp in sends:
            cp.wait_send()
        mine.wait()

    return pl.pallas_call(
        body, name=name, in_specs=[ANY], out_specs=ANY, out_shape=jax.ShapeDtypeStruct((4, rh, LANES), F32),
        scratch_shapes=[pltpu.SemaphoreType.DMA((3,)), pltpu.SemaphoreType.DMA((3,)), pltpu.SemaphoreType.DMA],
    )(p)


def _join_halves(name, r_half):
    rh = r_half.shape[0]

    def body(h_ref, o_ref, send_sem, recv_sem, local_sem):
        x, y, c = _place()
        mine = pltpu.make_async_copy(h_ref, o_ref.at[pl.ds(c * rh, rh), :], local_sem)
        mine.start()
        cp = pltpu.make_async_remote_copy(
            src_ref=h_ref, dst_ref=o_ref.at[pl.ds(c * rh, rh), :], send_sem=send_sem, recv_sem=recv_sem,
            device_id=(x, y, 1 - c), device_id_type=MESH)
        cp.start()
        pltpu.make_async_remote_copy(
            src_ref=h_ref, dst_ref=o_ref.at[pl.ds((1 - c) * rh, rh), :], send_sem=send_sem, recv_sem=recv_sem,
            device_id=(x, y, 1 - c), device_id_type=MESH).wait_recv()
        cp.wait_send()
        mine.wait()

    return pl.pallas_call(
        body, name=name, in_specs=[ANY], out_specs=ANY, out_shape=jax.ShapeDtypeStruct((2 * rh, LANES), F32),
        scratch_shapes=[pltpu.SemaphoreType.DMA, pltpu.SemaphoreType.DMA, pltpu.SemaphoreType.DMA],
    )(r_half)


def _rows_of(n):
    return -(-n // LANES)


def _pack(arrays, dtype, align=32):
    parts = []
    for a in arrays:
        flat = a.reshape(-1).astype(dtype)
        parts.append(jnp.pad(flat, (0, _rows_of(flat.size) * LANES - flat.size)))
    flat = jnp.concatenate(parts)
    rows = flat.size // LANES
    rows_pad = -(-rows // align) * align
    return jnp.pad(flat, (0, (rows_pad - rows) * LANES)).reshape(rows_pad, LANES)


def _unpack(slab, shapes):
    out, r = [], 0
    for shp in shapes:
        n = math.prod(shp)
        out.append(slab[r:r + _rows_of(n)].reshape(-1)[:n].reshape(shp))
        r += _rows_of(n)
    return out


def _unpack_gathered(g, shapes, kinds):
    out, r = [], 0
    for shp, kind in zip(shapes, kinds):
        n = math.prod(shp)
        blk = g[:, r:r + _rows_of(n)].reshape(4, -1)[:, :n].reshape((4,) + tuple(shp))
        r += _rows_of(n)
        if kind == "col":
            out.append(jnp.moveaxis(blk, 0, 1).reshape(shp[0], 4 * shp[1]))
        else:
            out.append(blk.reshape(4 * shp[0], shp[1]))
    return out


def _shard_block(g, kind, s, local_shape):
    if kind == "col":
        return g[:, s * local_shape[1]:(s + 1) * local_shape[1]]
    if kind == "row":
        return g[s * local_shape[0]:(s + 1) * local_shape[0]]
    return g


def _rotary_tables(seq):
    half = RET_DK // 2
    pos = jnp.arange(seq, dtype=F32)
    inv = ROPE_THETA ** (-jnp.arange(half, dtype=F32) / half)
    ang = pos[:, None] * inv[None, :]
    cos, sin = jnp.cos(ang), jnp.sin(ang)
    return jnp.concatenate([cos, cos], axis=1), jnp.concatenate([-sin, sin], axis=1)


def _retention_tables():
    log_gamma = jnp.log(1.0 - 2.0 ** (-5.0 - jnp.arange(RET_HEADS, dtype=F32)))
    idx = jnp.arange(CHUNK, dtype=F32)
    diff = idx[:, None] - idx[None, :]
    dmask = jnp.exp(jnp.where((diff >= 0)[None], log_gamma[:, None, None] * diff[None], -jnp.inf))
    kdec = jnp.exp(log_gamma[None, :] * (CHUNK - 1.0 - idx)[:, None])
    qdec = jnp.exp(log_gamma[None, :] * (idx + 1.0)[:, None])
    cdec = jnp.exp(log_gamma * CHUNK)[None, :]
    lanes = lambda t: jnp.repeat(t, RET_DK, axis=1)
    return dmask.reshape(RET_HEADS * CHUNK, CHUNK), lanes(kdec), lanes(qdec), lanes(cdec)


def _s5_prep(a_re, a_im, log_step, b_re, b_im, c_re, c_im):
    g, n, c = S5_GROUPS, S5_STATE, S5_GROUP
    lam = lax.complex(a_re, a_im)
    step = jnp.exp(log_step)[:, None]
    lam_bar = jnp.exp(lam * step)
    b_bar = ((lam_bar - 1.0) / lam)[..., None] * lax.complex(b_re, b_im)
    eye = jnp.eye(g, dtype=F32)
    bb_re = (jnp.real(b_bar).transpose(0, 2, 1)[:, :, None, :] * eye[:, None, :, None]).reshape(g * c, g * n)
    bb_im = (jnp.imag(b_bar).transpose(0, 2, 1)[:, :, None, :] * eye[:, None, :, None]).reshape(g * c, g * n)
    cc_re = (c_re.transpose(0, 2, 1)[:, :, None, :] * eye[:, None, :, None]).reshape(g * n, g * c)
    cc_im = (c_im.transpose(0, 2, 1)[:, :, None, :] * eye[:, None, :, None]).reshape(g * n, g * c)
    return (jnp.real(lam_bar).reshape(1, g * n), jnp.imag(lam_bar).reshape(1, g * n),
            jnp.concatenate([bb_re, bb_im], axis=1), cc_re, cc_im)


def kernel(x, l0_norm_mix, l0_w_in, ssd_conv_w, ssd_conv_b, ssd_dt_bias, ssd_A_log, ssd_D, ssd_norm_w, l0_w_out, l0_norm_mlp, l0_w_up, l0_w_down, l1_norm_mix, l1_w_in, gdn_conv_w, gdn_A_log, gdn_dt_bias, gdn_norm_w, s5_A_re, s5_A_im, s5_log_step, s5_B_re, s5_B_im, s5_C_re, s5_C_im, s5_D, s5_w_glu, s5_b_glu, l1_w_out, l1_norm_mlp, l1_w_up, l1_w_down, final_norm, loss_target, m_l0_norm_mix, m_l0_w_in, m_ssd_conv_w, m_ssd_conv_b, m_ssd_dt_bias, m_ssd_A_log, m_ssd_D, m_ssd_norm_w, m_l0_w_out, m_l0_norm_mlp, m_l0_w_up, m_l0_w_down, m_l1_norm_mix, m_l1_w_in, m_gdn_conv_w, m_gdn_A_log, m_gdn_dt_bias, m_gdn_norm_w, m_s5_A_re, m_s5_A_im, m_s5_log_step, m_s5_B_re, m_s5_B_im, m_s5_C_re, m_s5_C_im, m_s5_D, m_s5_w_glu, m_s5_b_glu, m_l1_w_out, m_l1_norm_mlp, m_l1_w_up, m_l1_w_down, m_final_norm, v_l0_norm_mix, v_l0_w_in, v_ssd_conv_w, v_ssd_conv_b, v_ssd_dt_bias, v_ssd_A_log, v_ssd_D, v_ssd_norm_w, v_l0_w_out, v_l0_norm_mlp, v_l0_w_up, v_l0_w_down, v_l1_norm_mix, v_l1_w_in, v_gdn_conv_w, v_gdn_A_log, v_gdn_dt_bias, v_gdn_norm_w, v_s5_A_re, v_s5_A_im, v_s5_log_step, v_s5_B_re, v_s5_B_im, v_s5_C_re, v_s5_C_im, v_s5_D, v_s5_w_glu, v_s5_b_glu, v_l1_w_out, v_l1_norm_mlp, v_l1_w_up, v_l1_w_down, v_final_norm):
    given = dict(locals())
    names = [n for n, _ in PARAMS]
    kinds = dict(PARAMS)
    w = {n: given[n] for n in names}
    seq = x.shape[1]
    x0 = x.reshape(seq, D_MODEL)
    target = loss_target.reshape(seq, D_MODEL)

    gb = _all_gather_shards("gather_weights", _pack([w[n] for n in GATHER_BF16], _MXU_DTYPE))
    full = dict(zip(GATHER_BF16, _unpack_gathered(gb, [w[n].shape for n in GATHER_BF16],
                                                  [kinds[n] for n in GATHER_BF16])))
    gf = _all_gather_shards("gather_conv", _pack([w[n] for n in GATHER_F32], F32))
    full.update(zip(GATHER_F32, _unpack_gathered(gf, [w[n].shape for n in GATHER_F32],
                                                 [kinds[n] for n in GATHER_F32])))
    in0 = full["l0_w_in"].shape[1]
    w_in0 = jnp.pad(full["l0_w_in"], ((0, 0), (0, IN0_PAD - in0)))
    wi1 = full["l1_w_in"]
    in1 = wi1.shape[1]
    w_in1 = jnp.concatenate([wi1[:, :3072], wi1[:, 3084:in1], wi1[:, 3072:3084],
                             jnp.zeros((D_MODEL, IN1_PAD - in1), wi1.dtype)], axis=1)

    row = lambda a: a.reshape(1, -1)
    lanes64 = lambda a: jnp.repeat(a, SSD_HEAD_DIM).reshape(1, -1)

    h0 = _rmsnorm_fwd("norm_mix0", x0, row(w["l0_norm_mix"]))
    proj0 = _matmul("in_proj0", h0, w_in0, "nn")
    cos_t, sin_t = _rotary_tables(seq)
    ret_tabs = list(_retention_tables())
    ret_xs = [(proj0, 512, 0), (proj0, 512, 1), (proj0, 512, 2), (proj0, 512, 3)]
    ret_xt = [(cos_t, 128, 0), (sin_t, 128, 0)]
    ret_states = [(512, 128)]
    mixed0, ret_saved = _scan_fwd("ret_fwd", _f_ret, CHUNK, ret_tabs, [], ret_xs, ret_xt, ret_states, D_MODEL, 512, 0)
    expand = jnp.repeat(jnp.eye(128, SSD_HEADS, dtype=F32), SSD_HEAD_DIM, axis=1)
    ssd_consts = [full["ssd_conv_w"], row(w["ssd_conv_b"]), lanes64(w["ssd_dt_bias"]), lanes64(w["ssd_A_log"]),
                  lanes64(w["ssd_D"]), row(w["ssd_norm_w"])]
    ssd_xs = [(proj0, 512, 4), (proj0, 512, 5), (proj0, 256, 12), (proj0, 256, 13), (proj0, 128, 28)]
    ssd_states = [(8, 512), (8, 256), (8, 256), (512, 128)]
    mixed0, ssd_saved = _scan_fwd("ssd_fwd", _f_ssd, CHUNK, [expand], ssd_consts, ssd_xs, [], ssd_states,
                                  D_MODEL, 512, 1, y_alias=mixed0)
    x1 = _matmul("out_proj0", mixed0, full["l0_w_out"], "nn", epi="add", epi_arr=x0)
    h1 = _rmsnorm_fwd("norm_mlp0", x1, row(w["l0_norm_mlp"]))
    u0 = _matmul("up0", h1, full["l0_w_up"], "nn")
    x2 = _matmul("down0", u0, full["l0_w_down"], "nn", a_pro="relu2", epi="add", epi_arr=x1)

    h2 = _rmsnorm_fwd("norm_mix1", x2, row(w["l1_norm_mix"]))
    proj1 = _matmul("in_proj1", h2, w_in1, "nn")
    p_alog = jnp.zeros((1, 128), F32).at[0, 6:12].set(w["gdn_A_log"])
    p_dtb = jnp.zeros((1, 128), F32).at[0, 6:12].set(w["gdn_dt_bias"])
    gdn_consts = [full["gdn_conv_w"], p_alog, p_dtb, row(w["gdn_norm_w"])]
    gdn_xs = [(proj1, 768, 0), (proj1, 768, 1), (proj1, 768, 2), (proj1, 768, 3), (proj1, 128, 26)]
    gdn_states = [(8, 768), (8, 768), (8, 768), (768, 256)]
    mixed1, gdn_saved = _scan_fwd("gdn_fwd", _f_gdn, CHUNK, [], gdn_consts, gdn_xs, [], gdn_states, D_MODEL, 768, 0)
    s5_args = (w["s5_A_re"], w["s5_A_im"], w["s5_log_step"], w["s5_B_re"], w["s5_B_im"], w["s5_C_re"], w["s5_C_im"])
    (lam_re, lam_im, bblk, cc_re, cc_im), s5_prep_vjp = jax.vjp(_s5_prep, *s5_args)
    s5_consts = [lam_re, lam_im, bblk, cc_re, cc_im, row(w["s5_D"]), full["s5_w_glu"].astype(F32), row(w["s5_b_glu"])]
    s5_xs = [(proj1, 256, 12)]
    s5_states = [(8, 1024), (8, 1024)]
    mixed1, s5_saved = _scan_fwd("s5_fwd", _f_s5, CHUNK, [], s5_consts, s5_xs, [], s5_states, D_MODEL, 256, 3,
                                 y_alias=mixed1)
    x3 = _matmul("out_proj1", mixed1, full["l1_w_out"], "nn", epi="add", epi_arr=x2)
    h3 = _rmsnorm_fwd("norm_mlp1", x3, row(w["l1_norm_mlp"]))
    u1 = _matmul("up1", h3, full["l1_w_up"], "nn")
    x4 = _matmul("down1", u1, full["l1_w_down"], "nn", a_pro="relu2", epi="add", epi_arr=x3)

    loss_part, dx4, d_final = _loss_head("loss_head", x4, row(w["final_norm"]), target)
    loss = lax.psum(loss_part[0, 0], ("x", "y", "c"))
    grads = {"final_norm": d_final.reshape(-1)}

    du1 = _matmul("down1_dx", dx4, full["l1_w_down"], "nt", out_dtype=_MXU_DTYPE, epi="drelu2", epi_arr=u1)
    grads["l1_w_down"] = _matmul("down1_dw", u1, dx4, "tn", a_pro="relu2")
    grads["l1_w_up"] = _matmul("up1_dw", h3, du1, "tn")
    dh3 = _matmul("up1_dx", du1, full["l1_w_up"], "nt")
    dx3, dwn = _rmsnorm_bwd("norm_mlp1_bwd", dh3, x3, row(w["l1_norm_mlp"]), dx4)
    grads["l1_norm_mlp"] = dwn.reshape(-1)
    grads["l1_w_out"] = _matmul("out_proj1_dw", mixed1, dx3, "tn")
    dmixed1 = _matmul("out_proj1_dx", dx3, full["l1_w_out"], "nt")

    def gdn_assemble(dx):
        dq, dk, dv, dz, dba = dx
        zeros = lambda n: jnp.zeros((dq.shape[0], n), F32)
        return jnp.concatenate([dq, dk, dv, dz, zeros(256), dba, zeros(IN1_PAD - 3456)], axis=1)

    dproj1, gdn_dc = _scan_bwd("gdn_bwd", _f_gdn, CHUNK, [], gdn_consts, gdn_xs, [], gdn_saved, gdn_states,
                               (dmixed1, 768, 0), IN1_PAD, IN1_PAD, 0, gdn_assemble)
    dproj1, s5_dc = _scan_bwd("s5_bwd", _f_s5, CHUNK, [], s5_consts, s5_xs, [], s5_saved, s5_states,
                              (dmixed1, 256, 3), IN1_PAD, 256, 12, lambda dx: dx[0], dx_alias=dproj1)
    grads["gdn_conv_w"] = gdn_dc[0]
    grads["gdn_A_log"] = gdn_dc[1][0, 6:12]
    grads["gdn_dt_bias"] = gdn_dc[2][0, 6:12]
    grads["gdn_norm_w"] = gdn_dc[3].reshape(-1)
    s5_pg = s5_prep_vjp(tuple(s5_dc[:5]))
    for n, gval in zip(("s5_A_re", "s5_A_im", "s5_log_step", "s5_B_re", "s5_B_im", "s5_C_re", "s5_C_im"), s5_pg):
        grads[n] = gval
    grads["s5_D"] = s5_dc[5].reshape(-1)
    grads["s5_w_glu"] = s5_dc[6]
    grads["s5_b_glu"] = s5_dc[7].reshape(-1)
    dwi1 = _matmul("in_proj1_dw", h2, dproj1, "tn")
    grads["l1_w_in"] = jnp.concatenate([dwi1[:, :3072], dwi1[:, 3328:3340], dwi1[:, 3072:3328]], axis=1)
    dh2 = _matmul("in_proj1_dx", dproj1, w_in1, "nt")
    dx2, dwn = _rmsnorm_bwd("norm_mix1_bwd", dh2, x2, row(w["l1_norm_mix"]), dx3)
    grads["l1_norm_mix"] = dwn.reshape(-1)

    du0 = _matmul("down0_dx", dx2, full["l0_w_down"], "nt", out_dtype=_MXU_DTYPE, epi="drelu2", epi_arr=u0)
    grads["l0_w_down"] = _matmul("down0_dw", u0, dx2, "tn", a_pro="relu2")
    grads["l0_w_up"] = _matmul("up0_dw", h1, du0, "tn")
    dh1 = _matmul("up0_dx", du0, full["l0_w_up"], "nt")
    dx1, dwn = _rmsnorm_bwd("norm_mlp0_bwd", dh1, x1, row(w["l0_norm_mlp"]), dx2)
    grads["l0_norm_mlp"] = dwn.reshape(-1)
    grads["l0_w_out"] = _matmul("out_proj0_dw", mixed0, dx1, "tn")
    dmixed0 = _matmul("out_proj0_dx", dx1, full["l0_w_out"], "nt")
    dproj0, _ = _scan_bwd("ret_bwd", _f_ret, CHUNK, ret_tabs, [], ret_xs, ret_xt, ret_saved, ret_states,
                          (dmixed0, 512, 0), IN0_PAD, 2048, 0, lambda dx: jnp.concatenate(dx, axis=1))

    def ssd_assemble(dx):
        return jnp.concatenate(list(dx) + [jnp.zeros((dx[0].shape[0], 2048 - 1664), F32)], axis=1)

    dproj0, ssd_dc = _scan_bwd("ssd_bwd", _f_ssd, CHUNK, [expand], ssd_consts, ssd_xs, [], ssd_saved, ssd_states,
                               (dmixed0, 512, 1), IN0_PAD, 2048, 1, ssd_assemble, dx_alias=dproj0)
    heads = lambda a: a.reshape(SSD_HEADS, SSD_HEAD_DIM).sum(axis=1)
    grads["ssd_conv_w"] = ssd_dc[0]
    grads["ssd_conv_b"] = ssd_dc[1].reshape(-1)
    grads["ssd_dt_bias"] = heads(ssd_dc[2])
    grads["ssd_A_log"] = heads(ssd_dc[3])
    grads["ssd_D"] = heads(ssd_dc[4])
    grads["ssd_norm_w"] = ssd_dc[5].reshape(-1)
    grads["l0_w_in"] = _matmul("in_proj0_dw", h0, dproj0, "tn")[:, :in0]
    dh0 = _matmul("in_proj0_dx", dproj0, w_in0, "nt")
    dx0, dwn = _rmsnorm_bwd("norm_mix0_bwd", dh0, x0, row(w["l0_norm_mix"]), dx1)
    grads["l0_norm_mix"] = dwn.reshape(-1)
    grad_x = dx0.reshape(x.shape)

    c_idx = lax.axis_index("c").astype(jnp.int32).reshape(1)
    sections = []
    for s in range(4):
        sections.append(_pack([_shard_block(grads[n].reshape(_full_shape(n, w, kinds)), kinds[n], s, w[n].shape)
                               for n in names], F32))
    gslab = jnp.stack(sections)
    from_sibling = _swap_halves("grads_swap_halves", gslab)
    chip_sum = _add_halves("grads_add_sibling", gslab, from_sibling, c_idx)
    partials = _scatter_to_owners("grads_scatter", chip_sum)
    my_half = _add_four("grads_add_chips", partials)
    gsum = _join_halves("grads_join_halves", my_half)

    shapes = [w[n].shape for n in names]
    delta, new_m, new_v = _adamw("adamw", _pack([w[n] for n in names], F32), gsum,
                                 _pack([given["m_" + n] for n in names], F32),
                                 _pack([given["v_" + n] for n in names], F32))
    return (loss, grad_x, *_unpack(gsum, shapes), *_unpack(delta, shapes), *_unpack(new_m, shapes),
            *_unpack(new_v, shapes))


def _full_shape(name, w, kinds):
    shp = w[name].shape
    if kinds[name] == "col":
        return (shp[0], 4 * shp[1])
    if kinds[name] == "row":
        return (4 * shp[0],) + tuple(shp[1:])
    return shp
```

```python
import functools
import math

import jax
import jax.numpy as jnp
from jax import lax
from jax.experimental import pallas as pl
from jax.experimental.pallas import tpu as pltpu

F32 = jnp.float32
_MXU_DTYPE = jnp.bfloat16

D_MODEL = 1024
CHUNK = 64
EPS = 1e-6
RET_HEADS, RET_DK = 4, 128
ROPE_THETA = 10000.0
SSD_HEADS, SSD_HEAD_DIM = 8, 64
GDN_HEADS, GDN_DK = 6, 128
S5_GROUPS, S5_GROUP, S5_STATE = 16, 16, 64
ADAM_LR, ADAM_B1, ADAM_B2, ADAM_EPS, ADAM_WD, ADAM_STEP = 0.001, 0.9, 0.999, 1e-08, 0.01, 10

IN0_PAD = 4096
IN1_PAD = 3584
LANES = 1024
VMEM_LIMIT = 56 * 1024 * 1024
MESH = pl.DeviceIdType.MESH

PARAMS = (
    ("l0_norm_mix", "rep"), ("l0_w_in", "col"), ("ssd_conv_w", "col"), ("ssd_conv_b", "rep"),
    ("ssd_dt_bias", "rep"), ("ssd_A_log", "rep"), ("ssd_D", "rep"), ("ssd_norm_w", "rep"),
    ("l0_w_out", "row"), ("l0_norm_mlp", "rep"), ("l0_w_up", "col"), ("l0_w_down", "row"),
    ("l1_norm_mix", "rep"), ("l1_w_in", "col"), ("gdn_conv_w", "col"), ("gdn_A_log", "rep"),
    ("gdn_dt_bias", "rep"), ("gdn_norm_w", "rep"), ("s5_A_re", "rep"), ("s5_A_im", "rep"),
    ("s5_log_step", "rep"), ("s5_B_re", "rep"), ("s5_B_im", "rep"), ("s5_C_re", "rep"), ("s5_C_im", "rep"),
    ("s5_D", "rep"), ("s5_w_glu", "row"), ("s5_b_glu", "rep"), ("l1_w_out", "row"), ("l1_norm_mlp", "rep"),
    ("l1_w_up", "col"), ("l1_w_down", "row"), ("final_norm", "rep"),
)
GATHER_BF16 = ("l0_w_in", "l0_w_out", "l0_w_up", "l0_w_down", "l1_w_in", "l1_w_out", "l1_w_up", "l1_w_down", "s5_w_glu")
GATHER_F32 = ("ssd_conv_w", "gdn_conv_w")
LARGE = GATHER_BF16[:8]
SMALL_SHARDED = ("s5_w_glu", "ssd_conv_w", "gdn_conv_w")


def _dg(a, b, ca, cb, prec=None):
    return lax.dot_general(a, b, (((ca,), (cb,)), ((), ())), preferred_element_type=F32, precision=prec)


def _lo(a):
    return a.astype(_MXU_DTYPE)


@jax.custom_vjp
def _mm(a, b):
    return _dg(_lo(a), _lo(b), 1, 0)


def _mm_fwd(a, b):
    return _mm(a, b), (a, b)


def _mm_bwd(res, g):
    a, b = res
    return _dg(_lo(g), _lo(b), 1, 1), _dg(_lo(a), _lo(g), 0, 0)


_mm.defvjp(_mm_fwd, _mm_bwd)


@jax.custom_vjp
def _mm_nt(a, b):
    return _dg(_lo(a), _lo(b), 1, 1)


def _mm_nt_fwd(a, b):
    return _mm_nt(a, b), (a, b)


def _mm_nt_bwd(res, g):
    a, b = res
    return _dg(_lo(g), _lo(b), 1, 0), _dg(_lo(g), _lo(a), 0, 0)


_mm_nt.defvjp(_mm_nt_fwd, _mm_nt_bwd)


@jax.custom_vjp
def _mm_tn(a, b):
    return _dg(_lo(a), _lo(b), 0, 0)


def _mm_tn_fwd(a, b):
    return _mm_tn(a, b), (a, b)


def _mm_tn_bwd(res, g):
    a, b = res
    return _dg(_lo(b), _lo(g), 1, 1), _dg(_lo(a), _lo(g), 1, 0)


_mm_tn.defvjp(_mm_tn_fwd, _mm_tn_bwd)


def _split2(x):
    hi = _lo(x)
    return hi, _lo(x - hi.astype(F32))


def _split3(x):
    h1 = _lo(x)
    r1 = x - h1.astype(F32)
    h2 = _lo(r1)
    return h1, h2, _lo(r1 - h2.astype(F32))


def _tri_cum_dir(m, ca):
    n, w = m.shape
    causal, _ = _tri_masks(n)
    out = _dg(causal.astype(_MXU_DTYPE), jnp.concatenate(_split3(m), axis=1), ca, 0)
    return out[:, :w] + out[:, w:2 * w] + out[:, 2 * w:]


@jax.custom_vjp
def _tri_cum(m):
    return _tri_cum_dir(m, 1)


def _tri_cum_fwd(m):
    return _tri_cum_dir(m, 1), None


def _tri_cum_bwd(_, g):
    return (_tri_cum_dir(g, 0),)


_tri_cum.defvjp(_tri_cum_fwd, _tri_cum_bwd)


@jax.custom_vjp
def _mm_exact_rhs(a, e):
    return _dg(jnp.concatenate(_split3(a), axis=1), jnp.concatenate([_lo(e)] * 3, axis=0), 1, 0)


def _mm_exact_rhs_fwd(a, e):
    return _mm_exact_rhs(a, e), e


def _mm_exact_rhs_bwd(e, g):
    return _dg(jnp.concatenate(_split3(g), axis=1), jnp.concatenate([_lo(e)] * 3, axis=1), 1, 1), jnp.zeros_like(e)


_mm_exact_rhs.defvjp(_mm_exact_rhs_fwd, _mm_exact_rhs_bwd)


def _bd(x):
    left = _iota(x.shape, 1) < (x.shape[1] // 2)
    zero = jnp.zeros_like(x)
    return jnp.concatenate([jnp.where(left, x, zero), jnp.where(left, zero, x)], axis=0)


def _unbd(m):
    half = m.shape[0] // 2
    left = _iota((half, m.shape[1]), 1) < (m.shape[1] // 2)
    return jnp.where(left, m[:half], m[half:])


def _pmm_nn(x, y):
    xh, xl = _split2(x)
    yh, yl = _split2(y)
    return _dg(jnp.concatenate([xh, xl, xh], axis=1), jnp.concatenate([_bd(yh), _bd(yh), _bd(yl)], axis=0), 1, 0)


def _pmm_nt(x, y):
    xh, xl = _split2(x)
    yh, yl = _split2(y)
    return _dg(jnp.concatenate([xh, xl, xh], axis=1), jnp.concatenate([_bd(yh), _bd(yh), _bd(yl)], axis=1), 1, 1)


def _pmm_tn(x, y):
    xh, xl = _split2(x)
    yh, yl = _split2(y)
    return _unbd(_dg(jnp.concatenate([xh, xl, xh], axis=0), jnp.concatenate([yh, yh, yl], axis=0), 0, 0))


@functools.lru_cache(maxsize=None)
def _shift(s, axis):
    @jax.custom_vjp
    def sh(x):
        return pltpu.roll(x, s, axis)

    def fwd(x):
        return sh(x), None

    def bwd(_, g):
        n = g.shape[axis]
        return (pltpu.roll(g, (n - s) % n, axis),)

    sh.defvjp(fwd, bwd)
    return sh


def _iota(shape, axis):
    return lax.broadcasted_iota(jnp.int32, shape, axis)


def _silu(x):
    return x * jax.nn.sigmoid(x)


def _unit_rms(x):
    return x * lax.rsqrt(jnp.mean(x * x, axis=-1, keepdims=True) + EPS)


def _l2norm(x):
    return x * lax.rsqrt(jnp.sum(x * x, axis=-1, keepdims=True) + EPS)


def _tri_masks(n):
    r, c = _iota((n, n), 0), _iota((n, n), 1)
    return r >= c, r > c


def _packed_rc():
    return _iota((CHUNK, 2 * CHUNK), 0), _iota((CHUNK, 2 * CHUNK), 1) & (CHUNK - 1)


def _decay_packed(g_packed):
    r, c = _packed_rc()
    seg = _tri_cum(g_packed * (r > c).astype(F32))
    return jnp.where(r >= c, jnp.exp(jnp.where(r >= c, seg, 0.0)), 0.0)


def _conv(x, tail, w):
    rows, width = x.shape
    row = _iota((rows, width), 0)
    acc = x * w[3:4, :]
    pad = jnp.zeros((rows - 8, width), F32)
    for j in range(3):
        s = 3 - j
        prev = jnp.concatenate([_shift(s, 0)(tail), pad], axis=0)
        acc = acc + w[j:j + 1, :] * jnp.where(row < s, prev, _shift(s, 0)(x))
    return acc


def _tri_inv_impl(mats):
    r, c = _packed_rc()
    eye = (r == c).astype(F32)

    def same_block(b):
        return (r // b) == (c // b)

    a8 = [jnp.where(same_block(8), a, 0.0) for a in mats]
    a2 = [_pmm_nn(t, t) for t in a8]
    a4 = [_pmm_nn(t, t) for t in a2]
    x = [_pmm_nn(eye - p, eye + q) for p, q in zip(a8, a2)]
    x = [_pmm_nn(p, eye + q) for p, q in zip(x, a4)]
    for b in (8, 16, 32):
        off = [jnp.where(same_block(2 * b) & jnp.logical_not(same_block(b)), a, 0.0) for a in mats]
        y = [_pmm_nn(p, q) for p, q in zip(x, off)]
        x = [p - _pmm_nn(q, p) for p, q in zip(x, y)]
    return x


@jax.custom_vjp
def _tri_inv(mats):
    return _tri_inv_impl(mats)


def _tri_inv_fwd(mats):
    t = _tri_inv_impl(mats)
    return t, t


def _tri_inv_bwd(t, g):
    m1 = [_pmm_tn(p, q) for p, q in zip(t, g)]
    return ([-_pmm_nt(p, q) for p, q in zip(m1, t)],)


_tri_inv.defvjp(_tri_inv_fwd, _tri_inv_bwd)


def _f_ret(tabs, consts, xs, xtabs, states):
    dmask, kdec, qdec, cdec = tabs
    q, k, v, gate = xs
    cs, sn = xtabs
    (st,) = states
    swap = _shift(RET_DK // 2, 1)
    heads = range(RET_HEADS)
    sls = [slice(128 * h, 128 * h + 128) for h in heads]
    qh = [(q[:, sl] * cs + swap(q[:, sl]) * sn) * (RET_DK ** -0.5) for sl in sls]
    kh = [k[:, sl] * cs + swap(k[:, sl]) * sn for sl in sls]
    sh = [st[sl, :] for sl in sls]
    scores = [_mm_nt(a, b) * dmask[64 * h:64 * h + 64, :] for h, a, b in zip(heads, qh, kh)]
    y = [_mm(s, v[:, sl]) for s, sl in zip(scores, sls)]
    y = [t + _mm(a * qdec[:, sl], s) for t, a, sl, s in zip(y, qh, sls, sh)]
    new = [s * cdec[:, sl] + _mm_tn(b * kdec[:, sl], v[:, sl]) for s, sl, b in zip(sh, sls, kh)]
    outs = [_silu(gate[:, sl]) * _unit_rms(t) for sl, t in zip(sls, y)]
    return (jnp.concatenate(outs, axis=1),), [jnp.concatenate(new, axis=0)]


def _f_ssd(tabs, consts, xs, xtabs, states):
    (expand,) = tabs
    conv_w, conv_b, dtb, alog, dskip, nw = consts
    z, xr, br, cr, dtr = xs
    tx, tb, tc, st = states
    xc = _silu(_conv(xr, tx, conv_w[:, 0:512]) + conv_b[:, 0:512])
    bc = _silu(_conv(br, tb, conv_w[:, 512:768]) + conv_b[:, 512:768])
    cc = _silu(_conv(cr, tc, conv_w[:, 768:1024]) + conv_b[:, 768:1024])
    dt = jax.nn.softplus(_mm_exact_rhs(dtr, expand) + dtb)
    la = dt * (-jnp.exp(alog))
    lacum = _tri_cum(la)
    total = jnp.sum(la, axis=0, keepdims=True)
    xd = xc * dt
    dte, ecum, cdec = jnp.exp(total - lacum), jnp.exp(lacum), jnp.exp(total)
    pairs = range(SSD_HEADS // 2)
    sls = [slice(128 * p, 128 * p + 128) for p in pairs]
    bg = [bc[:, 128 * g:128 * g + 128] for g in range(2)]
    cg = [cc[:, 128 * g:128 * g + 128] for g in range(2)]
    cb2 = [_mm_nt(c, jnp.concatenate([b, b], axis=0)) for b, c in zip(bg, cg)]
    lm = [_decay_packed(la[:, sl]) for sl in sls]
    sp = [st[sl, :] for sl in sls]
    ys = [_mm(cg[p // 2], sp[p]) * ecum[:, sls[p]] for p in pairs]
    ys = [ys[p] + _mm(cb2[p // 2] * lm[p], _bd(xd[:, sls[p]])) for p in pairs]
    new = [sp[p] * cdec[:, sls[p]] + _mm_tn(bg[p // 2], xd[:, sls[p]] * dte[:, sls[p]]) for p in pairs]
    y = jnp.concatenate(ys, axis=1) + dskip * xc
    yg = y * _silu(z)
    out = jnp.concatenate([_unit_rms(yg[:, 0:256]), _unit_rms(yg[:, 256:512])], axis=1) * nw
    return (out,), [xr[CHUNK - 8:, :], br[CHUNK - 8:, :], cr[CHUNK - 8:, :], jnp.concatenate(new, axis=0)]


def _f_gdn(tabs, consts, xs, xtabs, states):
    conv_w, p_alog, p_dtb, nw = consts
    qr, kr, vr, z, ba = xs
    tq, tk, tv, st = states
    qc = _silu(_conv(qr, tq, conv_w[:, 0:768]))
    kc = _silu(_conv(kr, tk, conv_w[:, 768:1536]))
    vc = _silu(_conv(vr, tv, conv_w[:, 1536:2304]))
    gl = -jnp.exp(p_alog) * jax.nn.softplus(ba + p_dtb)
    bl = jax.nn.sigmoid(ba)
    gcum = _tri_cum(gl)
    left128 = _iota((CHUNK, 128), 1) < 64
    left256 = _iota((CHUNK, 256), 1) < 128
    r, c = _packed_rc()
    diag_blocks = (_iota((256, 256), 0) < 128) == (_iota((256, 256), 1) < 128)

    def norm2(t):
        return jnp.concatenate([_l2norm(t[:, 0:128]), _l2norm(t[:, 128:256])], axis=1)

    def pick(arr, off, left, p):
        return jnp.where(left, arr[:, off + 2 * p:off + 2 * p + 1], arr[:, off + 2 * p + 1:off + 2 * p + 2])

    pairs = range(GDN_HEADS // 2)
    sls = [slice(256 * p, 256 * p + 256) for p in pairs]
    qn = [norm2(qc[:, sl]) * (GDN_DK ** -0.5) for sl in sls]
    kn = [norm2(kc[:, sl]) for sl in sls]
    dec = [_decay_packed(pick(gl, 6, left128, p)) for p in pairs]
    g2 = [pick(gl, 6, left256, p) for p in pairs]
    gc2 = [pick(gcum, 6, left256, p) for p in pairs]
    b2 = [pick(bl, 0, left256, p) for p in pairs]
    tot = [jnp.sum(t, axis=0, keepdims=True) for t in g2]
    eg = [jnp.exp(t) for t in gc2]
    et = [jnp.exp(t - s) for t, s in zip(tot, gc2)]
    cd = [jnp.exp(t) for t in tot]
    kb = [k * b for k, b in zip(kn, b2)]
    vb = [vc[:, sl] * b for sl, b in zip(sls, b2)]
    kbd = [_bd(k) for k in kn]
    tm = _tri_inv([jnp.where(r > c, _mm_nt(a, b) * d, 0.0) for a, b, d in zip(kb, kbd, dec)])
    u = [_mm(t, _bd(v)) for t, v in zip(tm, vb)]
    w = [_mm(t, _bd(k * e)) for t, k, e in zip(tm, kb, eg)]
    attn = [_mm_nt(q, k) * d for q, k, d in zip(qn, kbd, dec)]
    sp = [st[sl, :] for sl in sls]
    vn = [a - _mm(b, s) for a, b, s in zip(u, w, sp)]
    o = [_mm(q * e, s) + _mm(a, _bd(v)) for q, e, s, a, v in zip(qn, eg, sp, attn, vn)]
    new = [s * d + jnp.where(diag_blocks, _mm_tn(k * e, v), 0.0) for s, d, k, e, v in zip(sp, cd, kn, et, vn)]
    outs = []
    for p in pairs:
        for hh in range(2):
            osl = slice(128 * hh, 128 * hh + 128)
            zsl = slice(256 * p + 128 * hh, 256 * p + 128 * hh + 128)
            outs.append(_unit_rms(o[p][:, osl]) * nw * _silu(z[:, zsl]))
    return (jnp.concatenate(outs, axis=1),), [qr[CHUNK - 8:, :], kr[CHUNK - 8:, :], vr[CHUNK - 8:, :],
                                             jnp.concatenate(new, axis=0)]


def _f_s5(tabs, consts, xs, xtabs, states):
    lam_re, lam_im, bblk, c_re, c_im, dskip, wglu, bglu = consts
    (u,) = xs
    s_re, s_im = states
    rows = u.shape[0]
    n = lam_re.shape[1]
    bu = _mm(u, bblk)
    hr, hi = bu[:, 0:n], bu[:, n:2 * n]
    row = _iota((rows, n), 0)
    h0r, h0i = s_re[0:1, :], s_im[0:1, :]
    hr = hr + jnp.where(row == 0, lam_re * h0r - lam_im * h0i, 0.0)
    hi = hi + jnp.where(row == 0, lam_re * h0i + lam_im * h0r, 0.0)
    pr, pi = lam_re, lam_im
    d = 1
    while d < rows:
        sr = jnp.where(row >= d, _shift(d, 0)(hr), 0.0)
        si = jnp.where(row >= d, _shift(d, 0)(hi), 0.0)
        hr, hi = hr + pr * sr - pi * si, hi + pr * si + pi * sr
        pr, pi = pr * pr - pi * pi, 2.0 * pr * pi
        d *= 2
    y = _mm(hr, c_re) - _mm(hi, c_im) + dskip * u
    y = jax.nn.gelu(y)
    out = y * jax.nn.sigmoid(_mm(y, wglu) + bglu)
    last_r = jnp.broadcast_to(hr[rows - 1:rows, :], (8, n))
    last_i = jnp.broadcast_to(hi[rows - 1:rows, :], (8, n))
    return (out,), [last_r, last_i]


def _full_spec(a):
    nd = a.ndim
    return pl.BlockSpec(a.shape, lambda i, _nd=nd: (0,) * _nd)


CHUNKS_PER_STEP = 4


def _chunks_per_step(f, rows, n):
    def g(tabs, consts, xs, xtabs, states):
        ys = []
        for i in range(n):
            sl = slice(rows * i, rows * (i + 1))
            (y,), states = f(tabs, consts, [t[sl] for t in xs], [t[sl] for t in xtabs], states)
            ys.append(y)
        return (jnp.concatenate(ys, axis=0),), states

    return g


def _scan_fwd(name, f, rows, tabs, consts, xs, xtabs, state_shapes, y_total, y_width, y_cb, y_alias=None):
    seq = xs[0][0].shape[0]
    per_step = math.gcd(CHUNKS_PER_STEP, seq // rows)
    f = _chunks_per_step(f, rows, per_step)
    rows = rows * per_step
    nc = seq // rows
    nt, ncst, nx, nxt, ns = len(tabs), len(consts), len(xs), len(xtabs), len(state_shapes)
    alias = y_alias is not None

    def body(*refs):
        p = 0
        tab_r = refs[p:p + nt]; p += nt
        c_r = refs[p:p + ncst]; p += ncst
        x_r = refs[p:p + nx]; p += nx
        xt_r = refs[p:p + nxt]; p += nxt
        if alias:
            p += 1
        y_ref = refs[p]; p += 1
        sv_r = refs[p:p + ns]; p += ns
        st_r = refs[p:p + ns]

        @pl.when(pl.program_id(0) == 0)
        def _():
            for s in st_r:
                s[...] = jnp.zeros(s.shape, F32)

        st = [s[...] for s in st_r]
        for r, v in zip(sv_r, st):
            r[...] = v
        (y,), new = f([r[...] for r in tab_r], [r[...] for r in c_r], [r[...] for r in x_r],
                      [r[...] for r in xt_r], st)
        y_ref[...] = y
        for s, v in zip(st_r, new):
            s[...] = v

    win = [pl.BlockSpec((rows, w), lambda i, _cb=cb: (i, _cb)) for (_, w, cb) in list(xs) + list(xtabs)]
    in_specs = [_full_spec(a) for a in list(tabs) + list(consts)] + win
    args = list(tabs) + list(consts) + [a for (a, _, _) in list(xs) + list(xtabs)]
    io_alias = {}
    if alias:
        in_specs.append(pl.BlockSpec(memory_space=pl.ANY))
        io_alias = {len(args): 0}
        args.append(y_alias)
    out_shape = [jax.ShapeDtypeStruct((seq, y_total), F32)]
    out_specs = [pl.BlockSpec((rows, y_width), lambda i: (i, y_cb))]
    for (r, c) in state_shapes:
        out_shape.append(jax.ShapeDtypeStruct((nc * r, c), F32))
        out_specs.append(pl.BlockSpec((r, c), lambda i: (i, 0)))
    res = pl.pallas_call(
        body, name=name, grid=(nc,), in_specs=in_specs, out_specs=out_specs, out_shape=out_shape,
        scratch_shapes=[pltpu.VMEM(s, F32) for s in state_shapes], input_output_aliases=io_alias,
        compiler_params=pltpu.CompilerParams(dimension_semantics=("arbitrary",), vmem_limit_bytes=VMEM_LIMIT),
    )(*args)
    return res[0], list(res[1:])


def _scan_bwd(name, f, rows, tabs, consts, xs, xtabs, saved, state_shapes, dy, dx_total, dx_width, dx_cb,
              assemble, dx_alias=None):
    seq = xs[0][0].shape[0]
    per_step = math.gcd(CHUNKS_PER_STEP, seq // rows)
    f = _chunks_per_step(f, rows, per_step)
    rows = rows * per_step
    nc = seq // rows
    nt, ncst, nx, nxt, ns = len(tabs), len(consts), len(xs), len(xtabs), len(state_shapes)
    alias = dx_alias is not None

    def body(*refs):
        p = 0
        tab_r = refs[p:p + nt]; p += nt
        c_r = refs[p:p + ncst]; p += ncst
        x_r = refs[p:p + nx]; p += nx
        xt_r = refs[p:p + nxt]; p += nxt
        sv_r = refs[p:p + ns]; p += ns
        dy_ref = refs[p]; p += 1
        if alias:
            p += 1
        dx_ref = refs[p]; p += 1
        dc_r = refs[p:p + ncst]; p += ncst
        ds_r = refs[p:p + ns]

        @pl.when(pl.program_id(0) == 0)
        def _():
            for s in ds_r:
                s[...] = jnp.zeros(s.shape, F32)
            for r in dc_r:
                r[...] = jnp.zeros(r.shape, F32)

        tab_v = [r[...] for r in tab_r]
        xt_v = [r[...] for r in xt_r]

        def g(c, x, s):
            (y,), new = f(tab_v, c, x, xt_v, s)
            return y, new

        _, vjp = jax.vjp(g, [r[...] for r in c_r], [r[...] for r in x_r], [r[...] for r in sv_r])
        dc, dx, ds = vjp((dy_ref[...], [s[...] for s in ds_r]))
        dx_ref[...] = assemble(dx)
        for r, v in zip(dc_r, dc):
            r[...] += v
        for s, v in zip(ds_r, ds):
            s[...] = v

    win = [pl.BlockSpec((rows, w), lambda j, _cb=cb: (nc - 1 - j, _cb)) for (_, w, cb) in list(xs) + list(xtabs)]
    in_specs = [_full_spec(a) for a in list(tabs) + list(consts)] + win
    args = list(tabs) + list(consts) + [a for (a, _, _) in list(xs) + list(xtabs)]
    for (r, c), sv in zip(state_shapes, saved):
        in_specs.append(pl.BlockSpec((r, c), lambda j: (nc - 1 - j, 0)))
        args.append(sv)
    in_specs.append(pl.BlockSpec((rows, dy[1]), lambda j: (nc - 1 - j, dy[2])))
    args.append(dy[0])
    io_alias = {}
    if alias:
        in_specs.append(pl.BlockSpec(memory_space=pl.ANY))
        io_alias = {len(args): 0}
        args.append(dx_alias)
    out_shape = [jax.ShapeDtypeStruct((seq, dx_total), F32)] + [jax.ShapeDtypeStruct(a.shape, F32) for a in consts]
    out_specs = [pl.BlockSpec((rows, dx_width), lambda j: (nc - 1 - j, dx_cb))] + [_full_spec(a) for a in consts]
    res = pl.pallas_call(
        body, name=name, grid=(nc,), in_specs=in_specs, out_specs=out_specs, out_shape=out_shape,
        scratch_shapes=[pltpu.VMEM(s, F32) for s in state_shapes], input_output_aliases=io_alias,
        compiler_params=pltpu.CompilerParams(dimension_semantics=("arbitrary",), vmem_limit_bytes=VMEM_LIMIT),
    )(*args)
    return res[0], list(res[1:])


def _tile(n, want):
    t = min(n, want)
    while n % t:
        t //= 2
    return t


MATMUL_VMEM_BUDGET = 40 * 1024 * 1024


def _pick_tiles(m, n, k, sa, sb, so, se):
    best = None
    for tn in {_tile(n, 1024), _tile(n, 512)}:
        for tm in {_tile(m, t) for t in (2048, 1024, 512)}:
            for tk in {_tile(k, t) for t in (4096, 2048, 1024, 512)}:
                at, bt, ot = tm * tk * sa, tk * tn * sb, tm * tn * so
                need = 2 * (at + bt + ot + tm * tn * se) + 2 * tm * tn * 4 + (at if sa == 4 else 0) + (bt if sb == 4 else 0)
                if need > MATMUL_VMEM_BUDGET:
                    continue
                key = ((m // tm) * (n // tn) * (k // tk), k // tk, -tm, -tn)
                if best is None or key < best[0]:
                    best = (key, (tm, tn, tk))
    assert best is not None, (m, n, k)
    return best[1]


def _matmul(name, a, b, mode, out_dtype=F32, a_pro=None, epi=None, epi_arr=None):
    if mode == "nn":
        (m, k), (k2, n) = a.shape, b.shape
    elif mode == "nt":
        (m, k), (n, k2) = a.shape, b.shape
    else:
        (k, m), (k2, n) = a.shape, b.shape
    assert k == k2, (name, a.shape, b.shape)
    size = lambda t: jnp.dtype(t).itemsize
    tm, tn, tk = _pick_tiles(m, n, k, size(a.dtype), size(b.dtype), size(out_dtype),
                             0 if epi is None else size(epi_arr.dtype))
    nk = k // tk
    ca, cb = {"nn": (1, 0), "nt": (1, 1), "tn": (0, 0)}[mode]

    def body(*refs):
        refs = list(refs)
        acc = refs.pop() if nk > 1 else None
        a_ref, b_ref = refs[0], refs[1]
        e_ref = refs[2] if epi is not None else None
        o_ref = refs[-1]

        av = a_ref[...]
        if a_pro == "relu2":
            r = jnp.maximum(av, 0.0)
            av = r * r
        part = _dg(_lo(av), _lo(b_ref[...]), ca, cb)

        def finish(r):
            if epi == "add":
                r = r + e_ref[...]
            elif epi == "drelu2":
                r = r * (2.0 * jnp.maximum(e_ref[...], 0.0))
            o_ref[...] = r.astype(out_dtype)

        if nk == 1:
            finish(part)
        else:
            kk = pl.program_id(2)

            @pl.when(kk == 0)
            def _():
                acc[...] = part

            @pl.when(kk > 0)
            def _():
                acc[...] += part

            @pl.when(kk == nk - 1)
            def _():
                finish(acc[...])

    if mode == "tn":
        a_spec = pl.BlockSpec((tk, tm), lambda j, i, kk: (kk, i))
    else:
        a_spec = pl.BlockSpec((tm, tk), lambda j, i, kk: (i, kk))
    if mode == "nt":
        b_spec = pl.BlockSpec((tn, tk), lambda j, i, kk: (j, kk))
    else:
        b_spec = pl.BlockSpec((tk, tn), lambda j, i, kk: (kk, j))
    o_spec = pl.BlockSpec((tm, tn), lambda j, i, kk: (i, j))
    in_specs, args = [a_spec, b_spec], [a, b]
    if epi is not None:
        in_specs.append(o_spec)
        args.append(epi_arr)
    return pl.pallas_call(
        body, name=name, grid=(n // tn, m // tm, nk), in_specs=in_specs, out_specs=o_spec,
        out_shape=jax.ShapeDtypeStruct((m, n), out_dtype),
        scratch_shapes=[pltpu.VMEM((tm, tn), F32)] if nk > 1 else [],
        compiler_params=pltpu.CompilerParams(dimension_semantics=("parallel", "parallel", "arbitrary"),
                                             vmem_limit_bytes=VMEM_LIMIT),
    )(*args)


ROW_TILE = 512


def _rmsnorm_fwd(name, x, w):
    seq, d = x.shape
    tr = _tile(seq, ROW_TILE)

    def body(x_ref, w_ref, o_ref):
        xv = x_ref[...]
        o_ref[...] = (_unit_rms(xv) * w_ref[...]).astype(_MXU_DTYPE)

    return pl.pallas_call(
        body, name=name, grid=(seq // tr,),
        in_specs=[pl.BlockSpec((tr, d), lambda i: (i, 0)), pl.BlockSpec((1, d), lambda i: (0, 0))],
        out_specs=pl.BlockSpec((tr, d), lambda i: (i, 0)), out_shape=jax.ShapeDtypeStruct((seq, d), _MXU_DTYPE),
        compiler_params=pltpu.CompilerParams(dimension_semantics=("parallel",), vmem_limit_bytes=VMEM_LIMIT),
    )(x, w)


def _rmsnorm_bwd(name, dh, x, w, dres):
    seq, d = x.shape
    tr = _tile(seq, ROW_TILE)

    def body(dh_ref, x_ref, w_ref, dres_ref, dx_ref, dw_ref):
        @pl.when(pl.program_id(0) == 0)
        def _():
            dw_ref[...] = jnp.zeros(dw_ref.shape, F32)

        xv = x_ref[...]
        rstd = lax.rsqrt(jnp.mean(xv * xv, axis=-1, keepdims=True) + EPS)
        xh = xv * rstd
        dhv = dh_ref[...]
        g = dhv * w_ref[...]
        dx_ref[...] = dres_ref[...] + rstd * (g - xh * jnp.mean(g * xh, axis=-1, keepdims=True))
        dw_ref[...] += jnp.sum(dhv * xh, axis=0, keepdims=True)

    row = pl.BlockSpec((tr, d), lambda i: (i, 0))
    vec = pl.BlockSpec((1, d), lambda i: (0, 0))
    return pl.pallas_call(
        body, name=name, grid=(seq // tr,), in_specs=[row, row, vec, row], out_specs=[row, vec],
        out_shape=[jax.ShapeDtypeStruct((seq, d), F32), jax.ShapeDtypeStruct((1, d), F32)],
        compiler_params=pltpu.CompilerParams(dimension_semantics=("arbitrary",), vmem_limit_bytes=VMEM_LIMIT),
    )(dh, x, w, dres)


def _loss_head(name, x, w, target):
    seq, d = x.shape
    tr = _tile(seq, ROW_TILE)

    def body(x_ref, w_ref, t_ref, loss_ref, dx_ref, dw_ref):
        @pl.when(pl.program_id(0) == 0)
        def _():
            dw_ref[...] = jnp.zeros(dw_ref.shape, F32)
            loss_ref[...] = jnp.zeros(loss_ref.shape, F32)

        xv = x_ref[...]
        rstd = lax.rsqrt(jnp.mean(xv * xv, axis=-1, keepdims=True) + EPS)
        xh = xv * rstd
        err = xh * w_ref[...] - t_ref[...]
        per_row = jnp.mean(err * err, axis=-1, keepdims=True)
        loss_ref[...] += 0.5 * jnp.sum(per_row, axis=0, keepdims=True)
        dy = err * (1.0 / d)
        g = dy * w_ref[...]
        dx_ref[...] = rstd * (g - xh * jnp.mean(g * xh, axis=-1, keepdims=True))
        dw_ref[...] += jnp.sum(dy * xh, axis=0, keepdims=True)

    row = pl.BlockSpec((tr, d), lambda i: (i, 0))
    vec = pl.BlockSpec((1, d), lambda i: (0, 0))
    one = pl.BlockSpec((1, 1), lambda i: (0, 0))
    return pl.pallas_call(
        body, name=name, grid=(seq // tr,), in_specs=[row, vec, row], out_specs=[one, row, vec],
        out_shape=[jax.ShapeDtypeStruct((1, 1), F32), jax.ShapeDtypeStruct((seq, d), F32),
                   jax.ShapeDtypeStruct((1, d), F32)],
        compiler_params=pltpu.CompilerParams(dimension_semantics=("arbitrary",), vmem_limit_bytes=VMEM_LIMIT),
    )(x, w, target)


SLAB_TILE_ROWS = 1024


def _slab_tile(rows, cap=SLAB_TILE_ROWS):
    step = 16 if rows % 16 == 0 else 8
    return max(t for t in range(step, min(rows, cap) + 1, step) if rows % t == 0)


def _adamw(name, w, g, m, v):
    rows, cols = w.shape
    tr = _slab_tile(rows, SLAB_TILE_ROWS // 2) if rows % 8 == 0 else rows

    def body(w_ref, g_ref, m_ref, v_ref, d_ref, nm_ref, nv_ref):
        gv = g_ref[...]
        nm = ADAM_B1 * m_ref[...] + (1.0 - ADAM_B1) * gv
        nv = ADAM_B2 * v_ref[...] + (1.0 - ADAM_B2) * (gv * gv)
        m_hat = nm / (1.0 - ADAM_B1 ** ADAM_STEP)
        v_hat = nv / (1.0 - ADAM_B2 ** ADAM_STEP)
        d_ref[...] = -ADAM_LR * (m_hat / (jnp.sqrt(v_hat) + ADAM_EPS) + ADAM_WD * w_ref[...])
        nm_ref[...] = nm
        nv_ref[...] = nv

    spec = pl.BlockSpec((tr, cols), lambda i: (i, 0))
    sds = jax.ShapeDtypeStruct(w.shape, F32)
    return pl.pallas_call(
        body, name=name, grid=(rows // tr,), in_specs=[spec] * 4, out_specs=[spec] * 3, out_shape=[sds] * 3,
        compiler_params=pltpu.CompilerParams(dimension_semantics=("parallel",), vmem_limit_bytes=VMEM_LIMIT),
    )(w, g, m, v)


WIRE_DTYPE = jnp.bfloat16


def _add_halves(name, g, t1, c):
    nsec, rows, _ = g.shape
    rh = rows // 2
    tr = _slab_tile(rh)
    nb = rh // tr

    def body(c_ref, g_ref, t_ref, o_ref):
        o_ref[...] = (g_ref[...] + t_ref[...]).astype(o_ref.dtype)

    gs = pltpu.PrefetchScalarGridSpec(
        num_scalar_prefetch=1, grid=(nsec, nb),
        in_specs=[pl.BlockSpec((1, tr, LANES), lambda s, i, c_ref: (s, c_ref[0] * nb + i, 0)),
                  pl.BlockSpec((1, tr, LANES), lambda s, i, c_ref: (s, i, 0))],
        out_specs=pl.BlockSpec((1, tr, LANES), lambda s, i, c_ref: (s, i, 0)))
    return pl.pallas_call(
        body, name=name, grid_spec=gs, out_shape=jax.ShapeDtypeStruct((nsec, rh, LANES), WIRE_DTYPE),
        compiler_params=pltpu.CompilerParams(dimension_semantics=("parallel", "parallel"),
                                             vmem_limit_bytes=VMEM_LIMIT),
    )(c, g, t1)


def _add_four(name, t2):
    _, rh, _ = t2.shape
    tr = _slab_tile(rh)

    def body(t_ref, o_ref):
        part = lambda j: t_ref[j].astype(F32)
        o_ref[...] = ((part(0) + part(1)) + part(2)) + part(3)

    return pl.pallas_call(
        body, name=name, grid=(rh // tr,), in_specs=[pl.BlockSpec((4, tr, LANES), lambda i: (0, i, 0))],
        out_specs=pl.BlockSpec((tr, LANES), lambda i: (i, 0)), out_shape=jax.ShapeDtypeStruct((rh, LANES), F32),
        compiler_params=pltpu.CompilerParams(dimension_semantics=("parallel",), vmem_limit_bytes=VMEM_LIMIT),
    )(t2)


ANY = pl.BlockSpec(memory_space=pl.ANY)


def _place():
    return lax.axis_index("x"), lax.axis_index("y"), lax.axis_index("c")


def _all_gather_shards(name, slab):
    rows = slab.shape[0]
    rh = rows // 2

    def body(x_ref, out_ref, send_sems, recv_sems, local_sem):
        x, y, c = _place()
        sibling = (x, y, 1 - c)
        chips = [(1 - x, y), (x, 1 - y), (1 - x, 1 - y)]

        def part(px, py, pc):
            return out_ref.at[2 * px + py, pl.ds(pc * rh, rh), :]

        def copy(k, block, to, src=None):
            return pltpu.make_async_remote_copy(
                src_ref=part(*block) if src is None else src, dst_ref=part(*block),
                send_sem=send_sems.at[k], recv_sem=recv_sems.at[k], device_id=to, device_id_type=MESH)

        mine = pltpu.make_async_copy(x_ref, out_ref.at[2 * x + y], local_sem)
        mine.start()
        my_half = x_ref.at[pl.ds(c * rh, rh), :]
        first = [copy(j, (x, y, c), (*chip, c), src=my_half) for j, chip in enumerate(chips)]
        for cp in first:
            cp.start()
        passed = [copy(3 + j, (*chip, c), sibling) for j, chip in enumerate(chips)]
        for j, chip in enumerate(chips):
            copy(j, (*chip, c), (x, y, c)).wait_recv()
            passed[j].start()
        for j, chip in enumerate(chips):
            copy(3 + j, (*chip, 1 - c), (x, y, c)).wait_recv()
        for cp in first + passed:
            cp.wait_send()
        mine.wait()

    return pl.pallas_call(
        body, name=name, in_specs=[ANY], out_specs=ANY,
        out_shape=jax.ShapeDtypeStruct((4, rows, LANES), slab.dtype),
        scratch_shapes=[pltpu.SemaphoreType.DMA((6,)), pltpu.SemaphoreType.DMA((6,)), pltpu.SemaphoreType.DMA],
    )(slab)


def _swap_halves(name, g):
    nsec, rows, _ = g.shape
    rh = rows // 2

    def body(g_ref, t_ref, send_sem, recv_sem):
        x, y, c = _place()
        cp = pltpu.make_async_remote_copy(
            src_ref=g_ref.at[:, pl.ds((1 - c) * rh, rh), :], dst_ref=t_ref, send_sem=send_sem, recv_sem=recv_sem,
            device_id=(x, y, 1 - c), device_id_type=MESH)
        cp.start()
        cp.wait()

    return pl.pallas_call(
        body, name=name, in_specs=[ANY], out_specs=ANY, out_shape=jax.ShapeDtypeStruct((nsec, rh, LANES), F32),
        scratch_shapes=[pltpu.SemaphoreType.DMA, pltpu.SemaphoreType.DMA],
    )(g)


def _scatter_to_owners(name, p):
    _, rh, _ = p.shape

    def body(p_ref, t_ref, send_sems, recv_sems, local_sem):
        x, y, c = _place()
        me = 2 * x + y
        chips = [(1 - x, y), (x, 1 - y), (1 - x, 1 - y)]
        mine = pltpu.make_async_copy(p_ref.at[me], t_ref.at[me], local_sem)
        mine.start()
        sends = []
        for j, (cx, cy) in enumerate(chips):
            sends.append(pltpu.make_async_remote_copy(
                src_ref=p_ref.at[2 * cx + cy], dst_ref=t_ref.at[me], send_sem=send_sems.at[j],
                recv_sem=recv_sems.at[j], device_id=(cx, cy, c), device_id_type=MESH))
        for cp in sends:
            cp.start()
        for j, (cx, cy) in enumerate(chips):
            pltpu.make_async_remote_copy(
                src_ref=p_ref.at[me], dst_ref=t_ref.at[2 * cx + cy], send_sem=send_sems.at[j],
                recv_sem=recv_sems.at[j], device_id=(cx, cy, c), device_id_type=MESH).wait_recv()
        for cp in sends:
            cp.wait_send()
        mine.wait()

    return pl.pallas_call(
        body, name=name, in_specs=[ANY], out_specs=ANY, out_shape=jax.ShapeDtypeStruct((4, rh, LANES), p.dtype),
        scratch_shapes=[pltpu.SemaphoreType.DMA((3,)), pltpu.SemaphoreType.DMA((3,)), pltpu.SemaphoreType.DMA],
    )(p)


def _join_halves(name, r_half):
    rh = r_half.shape[0]

    def body(h_ref, o_ref, send_sem, recv_sem, local_sem):
        x, y, c = _place()
        mine = pltpu.make_async_copy(h_ref, o_ref.at[pl.ds(c * rh, rh), :], local_sem)
        mine.start()
        cp = pltpu.make_async_remote_copy(
            src_ref=h_ref, dst_ref=o_ref.at[pl.ds(c * rh, rh), :], send_sem=send_sem, recv_sem=recv_sem,
            device_id=(x, y, 1 - c), device_id_type=MESH)
        cp.start()
        pltpu.make_async_remote_copy(
            src_ref=h_ref, dst_ref=o_ref.at[pl.ds((1 - c) * rh, rh), :], send_sem=send_sem, recv_sem=recv_sem,
            device_id=(x, y, 1 - c), device_id_type=MESH).wait_recv()
        cp.wait_send()
        mine.wait()

    return pl.pallas_call(
        body, name=name, in_specs=[ANY], out_specs=ANY, out_shape=jax.ShapeDtypeStruct((2 * rh, LANES), F32),
        scratch_shapes=[pltpu.SemaphoreType.DMA, pltpu.SemaphoreType.DMA, pltpu.SemaphoreType.DMA],
    )(r_half)


def _rows_of(n):
    return -(-n // LANES)


SLAB_ROW_ALIGN = 512


def _flat_rows(arrays, dtype):
    parts = []
    for a in arrays:
        flat = a.reshape(-1).astype(dtype)
        parts.append(jnp.pad(flat, (0, _rows_of(flat.size) * LANES - flat.size)))
    return jnp.concatenate(parts).reshape(-1, LANES)


def _align_rows(slab):
    rows = slab.shape[0]
    return jnp.pad(slab, ((0, -(-rows // SLAB_ROW_ALIGN) * SLAB_ROW_ALIGN - rows), (0, 0)))


def _pack(arrays, dtype):
    return _align_rows(_flat_rows(arrays, dtype))


def _unpack(slab, shapes):
    out, r = [], 0
    for shp in shapes:
        n = math.prod(shp)
        out.append(slab[r:r + _rows_of(n)].reshape(-1)[:n].reshape(shp))
        r += _rows_of(n)
    return out


def _unpack_gathered(g, shapes, kinds):
    out, r = [], 0
    for shp, kind in zip(shapes, kinds):
        n = math.prod(shp)
        blk = g[:, r:r + _rows_of(n)].reshape(4, -1)[:, :n].reshape((4,) + tuple(shp))
        r += _rows_of(n)
        if kind == "col":
            out.append(jnp.moveaxis(blk, 0, 1).reshape(shp[0], 4 * shp[1]))
        else:
            out.append(blk.reshape(4 * shp[0], shp[1]))
    return out


def _shard_block(g, kind, s, local_shape):
    if kind == "col":
        return g[:, s * local_shape[1]:(s + 1) * local_shape[1]]
    if kind == "row":
        return g[s * local_shape[0]:(s + 1) * local_shape[0]]
    return g


def _rotary_tables(seq):
    half = RET_DK // 2
    pos = jnp.arange(seq, dtype=F32)
    inv = ROPE_THETA ** (-jnp.arange(half, dtype=F32) / half)
    ang = pos[:, None] * inv[None, :]
    cos, sin = jnp.cos(ang), jnp.sin(ang)
    return jnp.concatenate([cos, cos], axis=1), jnp.concatenate([-sin, sin], axis=1)


def _retention_tables():
    log_gamma = jnp.log(1.0 - 2.0 ** (-5.0 - jnp.arange(RET_HEADS, dtype=F32)))
    idx = jnp.arange(CHUNK, dtype=F32)
    diff = idx[:, None] - idx[None, :]
    dmask = jnp.exp(jnp.where((diff >= 0)[None], log_gamma[:, None, None] * diff[None], -jnp.inf))
    kdec = jnp.exp(log_gamma[None, :] * (CHUNK - 1.0 - idx)[:, None])
    qdec = jnp.exp(log_gamma[None, :] * (idx + 1.0)[:, None])
    cdec = jnp.exp(log_gamma * CHUNK)[None, :]
    lanes = lambda t: jnp.repeat(t, RET_DK, axis=1)
    return dmask.reshape(RET_HEADS * CHUNK, CHUNK), lanes(kdec), lanes(qdec), lanes(cdec)


def _s5_prep(a_re, a_im, log_step, b_re, b_im, c_re, c_im):
    g, n, c = S5_GROUPS, S5_STATE, S5_GROUP
    lam = lax.complex(a_re, a_im)
    step = jnp.exp(log_step)[:, None]
    lam_bar = jnp.exp(lam * step)
    b_bar = ((lam_bar - 1.0) / lam)[..., None] * lax.complex(b_re, b_im)
    eye = jnp.eye(g, dtype=F32)
    bb_re = (jnp.real(b_bar).transpose(0, 2, 1)[:, :, None, :] * eye[:, None, :, None]).reshape(g * c, g * n)
    bb_im = (jnp.imag(b_bar).transpose(0, 2, 1)[:, :, None, :] * eye[:, None, :, None]).reshape(g * c, g * n)
    cc_re = (c_re.transpose(0, 2, 1)[:, :, None, :] * eye[:, None, :, None]).reshape(g * n, g * c)
    cc_im = (c_im.transpose(0, 2, 1)[:, :, None, :] * eye[:, None, :, None]).reshape(g * n, g * c)
    return (jnp.real(lam_bar).reshape(1, g * n), jnp.imag(lam_bar).reshape(1, g * n),
            jnp.concatenate([bb_re, bb_im], axis=1), cc_re, cc_im)


def kernel(x, l0_norm_mix, l0_w_in, ssd_conv_w, ssd_conv_b, ssd_dt_bias, ssd_A_log, ssd_D, ssd_norm_w, l0_w_out, l0_norm_mlp, l0_w_up, l0_w_down, l1_norm_mix, l1_w_in, gdn_conv_w, gdn_A_log, gdn_dt_bias, gdn_norm_w, s5_A_re, s5_A_im, s5_log_step, s5_B_re, s5_B_im, s5_C_re, s5_C_im, s5_D, s5_w_glu, s5_b_glu, l1_w_out, l1_norm_mlp, l1_w_up, l1_w_down, final_norm, loss_target, m_l0_norm_mix, m_l0_w_in, m_ssd_conv_w, m_ssd_conv_b, m_ssd_dt_bias, m_ssd_A_log, m_ssd_D, m_ssd_norm_w, m_l0_w_out, m_l0_norm_mlp, m_l0_w_up, m_l0_w_down, m_l1_norm_mix, m_l1_w_in, m_gdn_conv_w, m_gdn_A_log, m_gdn_dt_bias, m_gdn_norm_w, m_s5_A_re, m_s5_A_im, m_s5_log_step, m_s5_B_re, m_s5_B_im, m_s5_C_re, m_s5_C_im, m_s5_D, m_s5_w_glu, m_s5_b_glu, m_l1_w_out, m_l1_norm_mlp, m_l1_w_up, m_l1_w_down, m_final_norm, v_l0_norm_mix, v_l0_w_in, v_ssd_conv_w, v_ssd_conv_b, v_ssd_dt_bias, v_ssd_A_log, v_ssd_D, v_ssd_norm_w, v_l0_w_out, v_l0_norm_mlp, v_l0_w_up, v_l0_w_down, v_l1_norm_mix, v_l1_w_in, v_gdn_conv_w, v_gdn_A_log, v_gdn_dt_bias, v_gdn_norm_w, v_s5_A_re, v_s5_A_im, v_s5_log_step, v_s5_B_re, v_s5_B_im, v_s5_C_re, v_s5_C_im, v_s5_D, v_s5_w_glu, v_s5_b_glu, v_l1_w_out, v_l1_norm_mlp, v_l1_w_up, v_l1_w_down, v_final_norm):
    given = dict(locals())
    names = [n for n, _ in PARAMS]
    kinds = dict(PARAMS)
    w = {n: given[n] for n in names}
    seq = x.shape[1]
    x0 = x.reshape(seq, D_MODEL)
    target = loss_target.reshape(seq, D_MODEL)

    gb = _all_gather_shards("gather_weights", _pack([w[n] for n in GATHER_BF16], _MXU_DTYPE))
    full = dict(zip(GATHER_BF16, _unpack_gathered(gb, [w[n].shape for n in GATHER_BF16],
                                                  [kinds[n] for n in GATHER_BF16])))
    gf = _all_gather_shards("gather_conv", _pack([w[n] for n in GATHER_F32], F32))
    full.update(zip(GATHER_F32, _unpack_gathered(gf, [w[n].shape for n in GATHER_F32],
                                                 [kinds[n] for n in GATHER_F32])))
    in0 = full["l0_w_in"].shape[1]
    w_in0 = jnp.pad(full["l0_w_in"], ((0, 0), (0, IN0_PAD - in0)))
    wi1 = full["l1_w_in"]
    in1 = wi1.shape[1]
    w_in1 = jnp.concatenate([wi1[:, :3072], wi1[:, 3084:in1], wi1[:, 3072:3084],
                             jnp.zeros((D_MODEL, IN1_PAD - in1), wi1.dtype)], axis=1)

    row = lambda a: a.reshape(1, -1)
    lanes64 = lambda a: jnp.repeat(a, SSD_HEAD_DIM).reshape(1, -1)

    h0 = _rmsnorm_fwd("norm_mix0", x0, row(w["l0_norm_mix"]))
    proj0 = _matmul("in_proj0", h0, w_in0, "nn")
    cos_t, sin_t = _rotary_tables(seq)
    ret_tabs = list(_retention_tables())
    ret_xs = [(proj0, 512, 0), (proj0, 512, 1), (proj0, 512, 2), (proj0, 512, 3)]
    ret_xt = [(cos_t, 128, 0), (sin_t, 128, 0)]
    ret_states = [(512, 128)]
    mixed0, ret_saved = _scan_fwd("ret_fwd", _f_ret, CHUNK, ret_tabs, [], ret_xs, ret_xt, ret_states, D_MODEL, 512, 0)
    expand = jnp.repeat(jnp.eye(128, SSD_HEADS, dtype=F32), SSD_HEAD_DIM, axis=1)
    ssd_consts = [full["ssd_conv_w"], row(w["ssd_conv_b"]), lanes64(w["ssd_dt_bias"]), lanes64(w["ssd_A_log"]),
                  lanes64(w["ssd_D"]), row(w["ssd_norm_w"])]
    ssd_xs = [(proj0, 512, 4), (proj0, 512, 5), (proj0, 256, 12), (proj0, 256, 13), (proj0, 128, 28)]
    ssd_states = [(8, 512), (8, 256), (8, 256), (512, 128)]
    mixed0, ssd_saved = _scan_fwd("ssd_fwd", _f_ssd, CHUNK, [expand], ssd_consts, ssd_xs, [], ssd_states,
                                  D_MODEL, 512, 1, y_alias=mixed0)
    x1 = _matmul("out_proj0", mixed0, full["l0_w_out"], "nn", epi="add", epi_arr=x0)
    h1 = _rmsnorm_fwd("norm_mlp0", x1, row(w["l0_norm_mlp"]))
    u0 = _matmul("up0", h1, full["l0_w_up"], "nn")
    x2 = _matmul("down0", u0, full["l0_w_down"], "nn", a_pro="relu2", epi="add", epi_arr=x1)

    h2 = _rmsnorm_fwd("norm_mix1", x2, row(w["l1_norm_mix"]))
    proj1 = _matmul("in_proj1", h2, w_in1, "nn")
    p_alog = jnp.zeros((1, 128), F32).at[0, 6:12].set(w["gdn_A_log"])
    p_dtb = jnp.zeros((1, 128), F32).at[0, 6:12].set(w["gdn_dt_bias"])
    gdn_consts = [full["gdn_conv_w"], p_alog, p_dtb, row(w["gdn_norm_w"])]
    gdn_xs = [(proj1, 768, 0), (proj1, 768, 1), (proj1, 768, 2), (proj1, 768, 3), (proj1, 128, 26)]
    gdn_states = [(8, 768), (8, 768), (8, 768), (768, 256)]
    mixed1, gdn_saved = _scan_fwd("gdn_fwd", _f_gdn, CHUNK, [], gdn_consts, gdn_xs, [], gdn_states, D_MODEL, 768, 0)
    s5_args = (w["s5_A_re"], w["s5_A_im"], w["s5_log_step"], w["s5_B_re"], w["s5_B_im"], w["s5_C_re"], w["s5_C_im"])
    (lam_re, lam_im, bblk, cc_re, cc_im), s5_prep_vjp = jax.vjp(_s5_prep, *s5_args)
    s5_consts = [lam_re, lam_im, bblk, cc_re, cc_im, row(w["s5_D"]), full["s5_w_glu"].astype(F32), row(w["s5_b_glu"])]
    s5_xs = [(proj1, 256, 12)]
    s5_states = [(8, 1024), (8, 1024)]
    mixed1, s5_saved = _scan_fwd("s5_fwd", _f_s5, CHUNK, [], s5_consts, s5_xs, [], s5_states, D_MODEL, 256, 3,
                                 y_alias=mixed1)
    x3 = _matmul("out_proj1", mixed1, full["l1_w_out"], "nn", epi="add", epi_arr=x2)
    h3 = _rmsnorm_fwd("norm_mlp1", x3, row(w["l1_norm_mlp"]))
    u1 = _matmul("up1", h3, full["l1_w_up"], "nn")
    x4 = _matmul("down1", u1, full["l1_w_down"], "nn", a_pro="relu2", epi="add", epi_arr=x3)

    loss_part, dx4, d_final = _loss_head("loss_head", x4, row(w["final_norm"]), target)
    loss = lax.psum(loss_part[0, 0], ("x", "y", "c"))
    grads = {"final_norm": d_final.reshape(-1)}

    du1 = _matmul("down1_dx", dx4, full["l1_w_down"], "nt", out_dtype=_MXU_DTYPE, epi="drelu2", epi_arr=u1)
    grads["l1_w_down"] = _matmul("down1_dw", u1, dx4, "tn", a_pro="relu2")
    grads["l1_w_up"] = _matmul("up1_dw", h3, du1, "tn")
    dh3 = _matmul("up1_dx", du1, full["l1_w_up"], "nt")
    dx3, dwn = _rmsnorm_bwd("norm_mlp1_bwd", dh3, x3, row(w["l1_norm_mlp"]), dx4)
    grads["l1_norm_mlp"] = dwn.reshape(-1)
    grads["l1_w_out"] = _matmul("out_proj1_dw", mixed1, dx3, "tn")
    dmixed1 = _matmul("out_proj1_dx", dx3, full["l1_w_out"], "nt")

    def gdn_assemble(dx):
        dq, dk, dv, dz, dba = dx
        zeros = lambda n: jnp.zeros((dq.shape[0], n), F32)
        return jnp.concatenate([dq, dk, dv, dz, zeros(256), dba, zeros(IN1_PAD - 3456)], axis=1)

    dproj1, gdn_dc = _scan_bwd("gdn_bwd", _f_gdn, CHUNK, [], gdn_consts, gdn_xs, [], gdn_saved, gdn_states,
                               (dmixed1, 768, 0), IN1_PAD, IN1_PAD, 0, gdn_assemble)
    dproj1, s5_dc = _scan_bwd("s5_bwd", _f_s5, CHUNK, [], s5_consts, s5_xs, [], s5_saved, s5_states,
                              (dmixed1, 256, 3), IN1_PAD, 256, 12, lambda dx: dx[0], dx_alias=dproj1)
    grads["gdn_conv_w"] = gdn_dc[0]
    grads["gdn_A_log"] = gdn_dc[1][0, 6:12]
    grads["gdn_dt_bias"] = gdn_dc[2][0, 6:12]
    grads["gdn_norm_w"] = gdn_dc[3].reshape(-1)
    s5_pg = s5_prep_vjp(tuple(s5_dc[:5]))
    for n, gval in zip(("s5_A_re", "s5_A_im", "s5_log_step", "s5_B_re", "s5_B_im", "s5_C_re", "s5_C_im"), s5_pg):
        grads[n] = gval
    grads["s5_D"] = s5_dc[5].reshape(-1)
    grads["s5_w_glu"] = s5_dc[6]
    grads["s5_b_glu"] = s5_dc[7].reshape(-1)
    dwi1 = _matmul("in_proj1_dw", h2, dproj1, "tn")
    grads["l1_w_in"] = jnp.concatenate([dwi1[:, :3072], dwi1[:, 3328:3340], dwi1[:, 3072:3328]], axis=1)
    dh2 = _matmul("in_proj1_dx", dproj1, w_in1, "nt")
    dx2, dwn = _rmsnorm_bwd("norm_mix1_bwd", dh2, x2, row(w["l1_norm_mix"]), dx3)
    grads["l1_norm_mix"] = dwn.reshape(-1)

    du0 = _matmul("down0_dx", dx2, full["l0_w_down"], "nt", out_dtype=_MXU_DTYPE, epi="drelu2", epi_arr=u0)
    grads["l0_w_down"] = _matmul("down0_dw", u0, dx2, "tn", a_pro="relu2")
    grads["l0_w_up"] = _matmul("up0_dw", h1, du0, "tn")
    dh1 = _matmul("up0_dx", du0, full["l0_w_up"], "nt")
    dx1, dwn = _rmsnorm_bwd("norm_mlp0_bwd", dh1, x1, row(w["l0_norm_mlp"]), dx2)
    grads["l0_norm_mlp"] = dwn.reshape(-1)
    grads["l0_w_out"] = _matmul("out_proj0_dw", mixed0, dx1, "tn")
    dmixed0 = _matmul("out_proj0_dx", dx1, full["l0_w_out"], "nt")
    dproj0, _ = _scan_bwd("ret_bwd", _f_ret, CHUNK, ret_tabs, [], ret_xs, ret_xt, ret_saved, ret_states,
                          (dmixed0, 512, 0), IN0_PAD, 2048, 0, lambda dx: jnp.concatenate(dx, axis=1))

    def ssd_assemble(dx):
        return jnp.concatenate(list(dx) + [jnp.zeros((dx[0].shape[0], 2048 - 1664), F32)], axis=1)

    dproj0, ssd_dc = _scan_bwd("ssd_bwd", _f_ssd, CHUNK, [expand], ssd_consts, ssd_xs, [], ssd_saved, ssd_states,
                               (dmixed0, 512, 1), IN0_PAD, 2048, 1, ssd_assemble, dx_alias=dproj0)
    heads = lambda a: a.reshape(SSD_HEADS, SSD_HEAD_DIM).sum(axis=1)
    grads["ssd_conv_w"] = ssd_dc[0]
    grads["ssd_conv_b"] = ssd_dc[1].reshape(-1)
    grads["ssd_dt_bias"] = heads(ssd_dc[2])
    grads["ssd_A_log"] = heads(ssd_dc[3])
    grads["ssd_D"] = heads(ssd_dc[4])
    grads["ssd_norm_w"] = ssd_dc[5].reshape(-1)
    grads["l0_w_in"] = _matmul("in_proj0_dw", h0, dproj0, "tn")[:, :in0]
    dh0 = _matmul("in_proj0_dx", dproj0, w_in0, "nt")
    dx0, dwn = _rmsnorm_bwd("norm_mix0_bwd", dh0, x0, row(w["l0_norm_mix"]), dx1)
    grads["l0_norm_mix"] = dwn.reshape(-1)
    grad_x = dx0.reshape(x.shape)

    c_idx = lax.axis_index("c").astype(jnp.int32).reshape(1)
    small = SMALL_SHARDED + tuple(n for n in names if kinds[n] == "rep")
    order = LARGE + small
    block = lambda n, s: _shard_block(grads[n].reshape(_full_shape(n, w, kinds)), kinds[n], s, w[n].shape)
    rep_rows = _flat_rows([grads[n] for n in order[len(LARGE) + len(SMALL_SHARDED):]], F32)
    gslab = jnp.stack([_align_rows(jnp.concatenate(
        [_flat_rows([block(n, s) for n in LARGE + SMALL_SHARDED], F32), rep_rows])) for s in range(4)])
    from_sibling = _swap_halves("grads_swap_halves", gslab)
    chip_sum = _add_halves("grads_add_sibling", gslab, from_sibling, c_idx)
    partials = _scatter_to_owners("grads_scatter", chip_sum)
    my_half = _add_four("grads_add_chips", partials)
    gsum = _join_halves("grads_join_halves", my_half)
    grad = dict(zip(order, _unpack(gsum, [w[n].shape for n in order])))

    delta, new_m, new_v = {}, {}, {}
    for n in LARGE:
        delta[n], new_m[n], new_v[n] = _adamw("adamw_" + n, w[n], grad[n], given["m_" + n], given["v_" + n])
    first_small = sum(_rows_of(math.prod(w[n].shape)) for n in LARGE)
    small_shapes = [w[n].shape for n in small]

    def small_slab(arrays):
        rows = _flat_rows(arrays, F32)
        return jnp.pad(rows, ((0, gsum.shape[0] - first_small - rows.shape[0]), (0, 0)))

    res = _adamw("adamw_small", small_slab([w[n] for n in small]), gsum[first_small:],
                 small_slab([given["m_" + n] for n in small]), small_slab([given["v_" + n] for n in small]))
    for out, slab in zip((delta, new_m, new_v), res):
        out.update(zip(small, _unpack(slab, small_shapes)))
    return (loss, grad_x, *[grad[n] for n in names], *[delta[n] for n in names], *[new_m[n] for n in names],
            *[new_v[n] for n in names])


def _full_shape(name, w, kinds):
    shp = w[name].shape
    if kinds[name] == "col":
        return (shp[0], 4 * shp[1])
    if kinds[name] == "row":
        return (4 * shp[0],) + tuple(shp[1:])
    return shp
```

```python
import functools
import math

import jax
import jax.numpy as jnp
from jax import lax
from jax.experimental import pallas as pl
from jax.experimental.pallas import tpu as pltpu

F32 = jnp.float32
_MXU_DTYPE = jnp.bfloat16

D_MODEL = 1024
CHUNK = 64
EPS = 1e-6
RET_HEADS, RET_DK = 4, 128
ROPE_THETA = 10000.0
SSD_HEADS, SSD_HEAD_DIM = 8, 64
GDN_HEADS, GDN_DK = 6, 128
S5_GROUPS, S5_GROUP, S5_STATE = 16, 16, 64
ADAM_LR, ADAM_B1, ADAM_B2, ADAM_EPS, ADAM_WD, ADAM_STEP = 0.001, 0.9, 0.999, 1e-08, 0.01, 10

IN0_PAD = 4096
IN1_PAD = 3584
LANES = 1024
VMEM_LIMIT = 56 * 1024 * 1024
MESH = pl.DeviceIdType.MESH

PARAMS = (
    ("l0_norm_mix", "rep"), ("l0_w_in", "col"), ("ssd_conv_w", "col"), ("ssd_conv_b", "rep"),
    ("ssd_dt_bias", "rep"), ("ssd_A_log", "rep"), ("ssd_D", "rep"), ("ssd_norm_w", "rep"),
    ("l0_w_out", "row"), ("l0_norm_mlp", "rep"), ("l0_w_up", "col"), ("l0_w_down", "row"),
    ("l1_norm_mix", "rep"), ("l1_w_in", "col"), ("gdn_conv_w", "col"), ("gdn_A_log", "rep"),
    ("gdn_dt_bias", "rep"), ("gdn_norm_w", "rep"), ("s5_A_re", "rep"), ("s5_A_im", "rep"),
    ("s5_log_step", "rep"), ("s5_B_re", "rep"), ("s5_B_im", "rep"), ("s5_C_re", "rep"), ("s5_C_im", "rep"),
    ("s5_D", "rep"), ("s5_w_glu", "row"), ("s5_b_glu", "rep"), ("l1_w_out", "row"), ("l1_norm_mlp", "rep"),
    ("l1_w_up", "col"), ("l1_w_down", "row"), ("final_norm", "rep"),
)
GATHER_BF16 = ("l0_w_in", "l0_w_out", "l0_w_up", "l0_w_down", "l1_w_in", "l1_w_out", "l1_w_up", "l1_w_down", "s5_w_glu")
GATHER_F32 = ("ssd_conv_w", "gdn_conv_w")
LARGE = GATHER_BF16[:8]
SMALL_SHARDED = ("s5_w_glu", "ssd_conv_w", "gdn_conv_w")


def _dg(a, b, ca, cb, prec=None):
    return lax.dot_general(a, b, (((ca,), (cb,)), ((), ())), preferred_element_type=F32, precision=prec)


def _lo(a):
    return a.astype(_MXU_DTYPE)


@jax.custom_vjp
def _mm(a, b):
    return _dg(_lo(a), _lo(b), 1, 0)


def _mm_fwd(a, b):
    return _mm(a, b), (a, b)


def _mm_bwd(res, g):
    a, b = res
    return _dg(_lo(g), _lo(b), 1, 1), _dg(_lo(a), _lo(g), 0, 0)


_mm.defvjp(_mm_fwd, _mm_bwd)


@jax.custom_vjp
def _mm_nt(a, b):
    return _dg(_lo(a), _lo(b), 1, 1)


def _mm_nt_fwd(a, b):
    return _mm_nt(a, b), (a, b)


def _mm_nt_bwd(res, g):
    a, b = res
    return _dg(_lo(g), _lo(b), 1, 0), _dg(_lo(g), _lo(a), 0, 0)


_mm_nt.defvjp(_mm_nt_fwd, _mm_nt_bwd)


@jax.custom_vjp
def _mm_tn(a, b):
    return _dg(_lo(a), _lo(b), 0, 0)


def _mm_tn_fwd(a, b):
    return _mm_tn(a, b), (a, b)


def _mm_tn_bwd(res, g):
    a, b = res
    return _dg(_lo(b), _lo(g), 1, 1), _dg(_lo(a), _lo(g), 1, 0)


_mm_tn.defvjp(_mm_tn_fwd, _mm_tn_bwd)


def _split2(x):
    hi = _lo(x)
    return hi, _lo(x - hi.astype(F32))


def _split3(x):
    h1 = _lo(x)
    r1 = x - h1.astype(F32)
    h2 = _lo(r1)
    return h1, h2, _lo(r1 - h2.astype(F32))


def _tri_cum_dir(m, ca):
    n, w = m.shape
    causal, _ = _tri_masks(n)
    out = _dg(causal.astype(_MXU_DTYPE), jnp.concatenate(_split3(m), axis=1), ca, 0)
    return out[:, :w] + out[:, w:2 * w] + out[:, 2 * w:]


@jax.custom_vjp
def _tri_cum(m):
    return _tri_cum_dir(m, 1)


def _tri_cum_fwd(m):
    return _tri_cum_dir(m, 1), None


def _tri_cum_bwd(_, g):
    return (_tri_cum_dir(g, 0),)


_tri_cum.defvjp(_tri_cum_fwd, _tri_cum_bwd)


@jax.custom_vjp
def _mm_exact_rhs(a, e):
    return _dg(jnp.concatenate(_split3(a), axis=1), jnp.concatenate([_lo(e)] * 3, axis=0), 1, 0)


def _mm_exact_rhs_fwd(a, e):
    return _mm_exact_rhs(a, e), e


def _mm_exact_rhs_bwd(e, g):
    return _dg(jnp.concatenate(_split3(g), axis=1), jnp.concatenate([_lo(e)] * 3, axis=1), 1, 1), jnp.zeros_like(e)


_mm_exact_rhs.defvjp(_mm_exact_rhs_fwd, _mm_exact_rhs_bwd)


def _bd(x):
    left = _iota(x.shape, 1) < (x.shape[1] // 2)
    zero = jnp.zeros_like(x)
    return jnp.concatenate([jnp.where(left, x, zero), jnp.where(left, zero, x)], axis=0)


def _unbd(m):
    half = m.shape[0] // 2
    left = _iota((half, m.shape[1]), 1) < (m.shape[1] // 2)
    return jnp.where(left, m[:half], m[half:])


def _pmm_nn(x, y):
    xh, xl = _split2(x)
    yh, yl = _split2(y)
    return _dg(jnp.concatenate([xh, xl, xh], axis=1), jnp.concatenate([_bd(yh), _bd(yh), _bd(yl)], axis=0), 1, 0)


def _pmm_nt(x, y):
    xh, xl = _split2(x)
    yh, yl = _split2(y)
    return _dg(jnp.concatenate([xh, xl, xh], axis=1), jnp.concatenate([_bd(yh), _bd(yh), _bd(yl)], axis=1), 1, 1)


def _pmm_tn(x, y):
    xh, xl = _split2(x)
    yh, yl = _split2(y)
    return _unbd(_dg(jnp.concatenate([xh, xl, xh], axis=0), jnp.concatenate([yh, yh, yl], axis=0), 0, 0))


@functools.lru_cache(maxsize=None)
def _shift(s, axis):
    @jax.custom_vjp
    def sh(x):
        return pltpu.roll(x, s, axis)

    def fwd(x):
        return sh(x), None

    def bwd(_, g):
        n = g.shape[axis]
        return (pltpu.roll(g, (n - s) % n, axis),)

    sh.defvjp(fwd, bwd)
    return sh


def _iota(shape, axis):
    return lax.broadcasted_iota(jnp.int32, shape, axis)


def _silu(x):
    return x * jax.nn.sigmoid(x)


def _unit_rms(x):
    return x * lax.rsqrt(jnp.mean(x * x, axis=-1, keepdims=True) + EPS)


def _l2norm(x):
    return x * lax.rsqrt(jnp.sum(x * x, axis=-1, keepdims=True) + EPS)


def _tri_masks(n):
    r, c = _iota((n, n), 0), _iota((n, n), 1)
    return r >= c, r > c


def _packed_rc():
    return _iota((CHUNK, 2 * CHUNK), 0), _iota((CHUNK, 2 * CHUNK), 1) & (CHUNK - 1)


def _decay_packed(g_packed):
    r, c = _packed_rc()
    seg = _tri_cum(g_packed * (r > c).astype(F32))
    return jnp.where(r >= c, jnp.exp(jnp.where(r >= c, seg, 0.0)), 0.0)


def _conv(x, tail, w):
    rows, width = x.shape
    row = _iota((rows, width), 0)
    acc = x * w[3:4, :]
    pad = jnp.zeros((rows - 8, width), F32)
    for j in range(3):
        s = 3 - j
        prev = jnp.concatenate([_shift(s, 0)(tail), pad], axis=0)
        acc = acc + w[j:j + 1, :] * jnp.where(row < s, prev, _shift(s, 0)(x))
    return acc


def _tri_inv_impl(mats):
    r, c = _packed_rc()
    eye = (r == c).astype(F32)

    def same_block(b):
        return (r // b) == (c // b)

    a8 = [jnp.where(same_block(8), a, 0.0) for a in mats]
    a2 = [_pmm_nn(t, t) for t in a8]
    a4 = [_pmm_nn(t, t) for t in a2]
    x = [_pmm_nn(eye - p, eye + q) for p, q in zip(a8, a2)]
    x = [_pmm_nn(p, eye + q) for p, q in zip(x, a4)]
    for b in (8, 16, 32):
        off = [jnp.where(same_block(2 * b) & jnp.logical_not(same_block(b)), a, 0.0) for a in mats]
        y = [_pmm_nn(p, q) for p, q in zip(x, off)]
        x = [p - _pmm_nn(q, p) for p, q in zip(x, y)]
    return x


@jax.custom_vjp
def _tri_inv(mats):
    return _tri_inv_impl(mats)


def _tri_inv_fwd(mats):
    t = _tri_inv_impl(mats)
    return t, t


def _tri_inv_bwd(t, g):
    m1 = [_pmm_tn(p, q) for p, q in zip(t, g)]
    return ([-_pmm_nt(p, q) for p, q in zip(m1, t)],)


_tri_inv.defvjp(_tri_inv_fwd, _tri_inv_bwd)


def _f_ret(tabs, consts, xs, xtabs, states):
    dmask, kdec, qdec, cdec = tabs
    q, k, v, gate = xs
    cs, sn = xtabs
    (st,) = states
    swap = _shift(RET_DK // 2, 1)
    heads = range(RET_HEADS)
    sls = [slice(128 * h, 128 * h + 128) for h in heads]
    qh = [(q[:, sl] * cs + swap(q[:, sl]) * sn) * (RET_DK ** -0.5) for sl in sls]
    kh = [k[:, sl] * cs + swap(k[:, sl]) * sn for sl in sls]
    sh = [st[sl, :] for sl in sls]
    scores = [_mm_nt(a, b) * dmask[64 * h:64 * h + 64, :] for h, a, b in zip(heads, qh, kh)]
    y = [_mm(s, v[:, sl]) for s, sl in zip(scores, sls)]
    y = [t + _mm(a * qdec[:, sl], s) for t, a, sl, s in zip(y, qh, sls, sh)]
    new = [s * cdec[:, sl] + _mm_tn(b * kdec[:, sl], v[:, sl]) for s, sl, b in zip(sh, sls, kh)]
    outs = [_silu(gate[:, sl]) * _unit_rms(t) for sl, t in zip(sls, y)]
    return (jnp.concatenate(outs, axis=1),), [jnp.concatenate(new, axis=0)]


def _f_ssd(tabs, consts, xs, xtabs, states):
    (expand,) = tabs
    conv_w, conv_b, dtb, alog, dskip, nw = consts
    z, xr, br, cr, dtr = xs
    tx, tb, tc, st = states
    xc = _silu(_conv(xr, tx, conv_w[:, 0:512]) + conv_b[:, 0:512])
    bc = _silu(_conv(br, tb, conv_w[:, 512:768]) + conv_b[:, 512:768])
    cc = _silu(_conv(cr, tc, conv_w[:, 768:1024]) + conv_b[:, 768:1024])
    dt = jax.nn.softplus(_mm_exact_rhs(dtr, expand) + dtb)
    la = dt * (-jnp.exp(alog))
    lacum = _tri_cum(la)
    total = jnp.sum(la, axis=0, keepdims=True)
    xd = xc * dt
    dte, ecum, cdec = jnp.exp(total - lacum), jnp.exp(lacum), jnp.exp(total)
    pairs = range(SSD_HEADS // 2)
    sls = [slice(128 * p, 128 * p + 128) for p in pairs]
    bg = [bc[:, 128 * g:128 * g + 128] for g in range(2)]
    cg = [cc[:, 128 * g:128 * g + 128] for g in range(2)]
    cb2 = [_mm_nt(c, jnp.concatenate([b, b], axis=0)) for b, c in zip(bg, cg)]
    lm = [_decay_packed(la[:, sl]) for sl in sls]
    sp = [st[sl, :] for sl in sls]
    ys = [_mm(cg[p // 2], sp[p]) * ecum[:, sls[p]] for p in pairs]
    ys = [ys[p] + _mm(cb2[p // 2] * lm[p], _bd(xd[:, sls[p]])) for p in pairs]
    new = [sp[p] * cdec[:, sls[p]] + _mm_tn(bg[p // 2], xd[:, sls[p]] * dte[:, sls[p]]) for p in pairs]
    y = jnp.concatenate(ys, axis=1) + dskip * xc
    yg = y * _silu(z)
    out = jnp.concatenate([_unit_rms(yg[:, 0:256]), _unit_rms(yg[:, 256:512])], axis=1) * nw
    return (out,), [xr[CHUNK - 8:, :], br[CHUNK - 8:, :], cr[CHUNK - 8:, :], jnp.concatenate(new, axis=0)]


def _f_gdn(tabs, consts, xs, xtabs, states):
    conv_w, p_alog, p_dtb, nw = consts
    qr, kr, vr, z, ba = xs
    tq, tk, tv, st = states
    qc = _silu(_conv(qr, tq, conv_w[:, 0:768]))
    kc = _silu(_conv(kr, tk, conv_w[:, 768:1536]))
    vc = _silu(_conv(vr, tv, conv_w[:, 1536:2304]))
    gl = -jnp.exp(p_alog) * jax.nn.softplus(ba + p_dtb)
    bl = jax.nn.sigmoid(ba)
    gcum = _tri_cum(gl)
    left128 = _iota((CHUNK, 128), 1) < 64
    left256 = _iota((CHUNK, 256), 1) < 128
    r, c = _packed_rc()
    diag_blocks = (_iota((256, 256), 0) < 128) == (_iota((256, 256), 1) < 128)

    def norm2(t):
        return jnp.concatenate([_l2norm(t[:, 0:128]), _l2norm(t[:, 128:256])], axis=1)

    def pick(arr, off, left, p):
        return jnp.where(left, arr[:, off + 2 * p:off + 2 * p + 1], arr[:, off + 2 * p + 1:off + 2 * p + 2])

    pairs = range(GDN_HEADS // 2)
    sls = [slice(256 * p, 256 * p + 256) for p in pairs]
    qn = [norm2(qc[:, sl]) * (GDN_DK ** -0.5) for sl in sls]
    kn = [norm2(kc[:, sl]) for sl in sls]
    dec = [_decay_packed(pick(gl, 6, left128, p)) for p in pairs]
    g2 = [pick(gl, 6, left256, p) for p in pairs]
    gc2 = [pick(gcum, 6, left256, p) for p in pairs]
    b2 = [pick(bl, 0, left256, p) for p in pairs]
    tot = [jnp.sum(t, axis=0, keepdims=True) for t in g2]
    eg = [jnp.exp(t) for t in gc2]
    et = [jnp.exp(t - s) for t, s in zip(tot, gc2)]
    cd = [jnp.exp(t) for t in tot]
    kb = [k * b for k, b in zip(kn, b2)]
    vb = [vc[:, sl] * b for sl, b in zip(sls, b2)]
    kbd = [_bd(k) for k in kn]
    tm = _tri_inv([jnp.where(r > c, _mm_nt(a, b) * d, 0.0) for a, b, d in zip(kb, kbd, dec)])
    u = [_mm(t, _bd(v)) for t, v in zip(tm, vb)]
    w = [_mm(t, _bd(k * e)) for t, k, e in zip(tm, kb, eg)]
    attn = [_mm_nt(q, k) * d for q, k, d in zip(qn, kbd, dec)]
    sp = [st[sl, :] for sl in sls]
    vn = [a - _mm(b, s) for a, b, s in zip(u, w, sp)]
    o = [_mm(q * e, s) + _mm(a, _bd(v)) for q, e, s, a, v in zip(qn, eg, sp, attn, vn)]
    new = [s * d + jnp.where(diag_blocks, _mm_tn(k * e, v), 0.0) for s, d, k, e, v in zip(sp, cd, kn, et, vn)]
    outs = []
    for p in pairs:
        for hh in range(2):
            osl = slice(128 * hh, 128 * hh + 128)
            zsl = slice(256 * p + 128 * hh, 256 * p + 128 * hh + 128)
            outs.append(_unit_rms(o[p][:, osl]) * nw * _silu(z[:, zsl]))
    return (jnp.concatenate(outs, axis=1),), [qr[CHUNK - 8:, :], kr[CHUNK - 8:, :], vr[CHUNK - 8:, :],
                                             jnp.concatenate(new, axis=0)]


def _f_s5(tabs, consts, xs, xtabs, states):
    lam_re, lam_im, bblk, c_re, c_im, dskip, wglu, bglu = consts
    (u,) = xs
    s_re, s_im = states
    rows = u.shape[0]
    n = lam_re.shape[1]
    bu = _mm(u, bblk)
    hr, hi = bu[:, 0:n], bu[:, n:2 * n]
    row = _iota((rows, n), 0)
    h0r, h0i = s_re[0:1, :], s_im[0:1, :]
    hr = hr + jnp.where(row == 0, lam_re * h0r - lam_im * h0i, 0.0)
    hi = hi + jnp.where(row == 0, lam_re * h0i + lam_im * h0r, 0.0)
    pr, pi = lam_re, lam_im
    d = 1
    while d < rows:
        sr = jnp.where(row >= d, _shift(d, 0)(hr), 0.0)
        si = jnp.where(row >= d, _shift(d, 0)(hi), 0.0)
        hr, hi = hr + pr * sr - pi * si, hi + pr * si + pi * sr
        pr, pi = pr * pr - pi * pi, 2.0 * pr * pi
        d *= 2
    y = _mm(hr, c_re) - _mm(hi, c_im) + dskip * u
    y = jax.nn.gelu(y)
    out = y * jax.nn.sigmoid(_mm(y, wglu) + bglu)
    last_r = jnp.broadcast_to(hr[rows - 1:rows, :], (8, n))
    last_i = jnp.broadcast_to(hi[rows - 1:rows, :], (8, n))
    return (out,), [last_r, last_i]


def _full_spec(a):
    nd = a.ndim
    return pl.BlockSpec(a.shape, lambda i, _nd=nd: (0,) * _nd)


CHUNKS_PER_STEP = 4


def _chunks_per_step(f, rows, n):
    def g(tabs, consts, xs, xtabs, states):
        ys = []
        for i in range(n):
            sl = slice(rows * i, rows * (i + 1))
            (y,), states = f(tabs, consts, [t[sl] for t in xs], [t[sl] for t in xtabs], states)
            ys.append(y)
        return (jnp.concatenate(ys, axis=0),), states

    return g


def _scan_fwd(name, f, rows, tabs, consts, xs, xtabs, state_shapes, y_total, y_width, y_cb, y_alias=None):
    seq = xs[0][0].shape[0]
    per_step = math.gcd(CHUNKS_PER_STEP, seq // rows)
    f = _chunks_per_step(f, rows, per_step)
    rows = rows * per_step
    nc = seq // rows
    nt, ncst, nx, nxt, ns = len(tabs), len(consts), len(xs), len(xtabs), len(state_shapes)
    alias = y_alias is not None

    def body(*refs):
        p = 0
        tab_r = refs[p:p + nt]; p += nt
        c_r = refs[p:p + ncst]; p += ncst
        x_r = refs[p:p + nx]; p += nx
        xt_r = refs[p:p + nxt]; p += nxt
        if alias:
            p += 1
        y_ref = refs[p]; p += 1
        sv_r = refs[p:p + ns]; p += ns
        st_r = refs[p:p + ns]

        @pl.when(pl.program_id(0) == 0)
        def _():
            for s in st_r:
                s[...] = jnp.zeros(s.shape, F32)

        st = [s[...] for s in st_r]
        for r, v in zip(sv_r, st):
            r[...] = v
        (y,), new = f([r[...] for r in tab_r], [r[...] for r in c_r], [r[...] for r in x_r],
                      [r[...] for r in xt_r], st)
        y_ref[...] = y
        for s, v in zip(st_r, new):
            s[...] = v

    win = [pl.BlockSpec((rows, w), lambda i, _cb=cb: (i, _cb)) for (_, w, cb) in list(xs) + list(xtabs)]
    in_specs = [_full_spec(a) for a in list(tabs) + list(consts)] + win
    args = list(tabs) + list(consts) + [a for (a, _, _) in list(xs) + list(xtabs)]
    io_alias = {}
    if alias:
        in_specs.append(pl.BlockSpec(memory_space=pl.ANY))
        io_alias = {len(args): 0}
        args.append(y_alias)
    out_shape = [jax.ShapeDtypeStruct((seq, y_total), F32)]
    out_specs = [pl.BlockSpec((rows, y_width), lambda i: (i, y_cb))]
    for (r, c) in state_shapes:
        out_shape.append(jax.ShapeDtypeStruct((nc * r, c), F32))
        out_specs.append(pl.BlockSpec((r, c), lambda i: (i, 0)))
    res = pl.pallas_call(
        body, name=name, grid=(nc,), in_specs=in_specs, out_specs=out_specs, out_shape=out_shape,
        scratch_shapes=[pltpu.VMEM(s, F32) for s in state_shapes], input_output_aliases=io_alias,
        compiler_params=pltpu.CompilerParams(dimension_semantics=("arbitrary",), vmem_limit_bytes=VMEM_LIMIT),
    )(*args)
    return res[0], list(res[1:])


def _scan_bwd(name, f, rows, tabs, consts, xs, xtabs, saved, state_shapes, dy, dx_total, dx_width, dx_cb,
              assemble, dx_alias=None):
    seq = xs[0][0].shape[0]
    per_step = math.gcd(CHUNKS_PER_STEP, seq // rows)
    f = _chunks_per_step(f, rows, per_step)
    rows = rows * per_step
    nc = seq // rows
    nt, ncst, nx, nxt, ns = len(tabs), len(consts), len(xs), len(xtabs), len(state_shapes)
    alias = dx_alias is not None

    def body(*refs):
        p = 0
        tab_r = refs[p:p + nt]; p += nt
        c_r = refs[p:p + ncst]; p += ncst
        x_r = refs[p:p + nx]; p += nx
        xt_r = refs[p:p + nxt]; p += nxt
        sv_r = refs[p:p + ns]; p += ns
        dy_ref = refs[p]; p += 1
        if alias:
            p += 1
        dx_ref = refs[p]; p += 1
        dc_r = refs[p:p + ncst]; p += ncst
        ds_r = refs[p:p + ns]

        @pl.when(pl.program_id(0) == 0)
        def _():
            for s in ds_r:
                s[...] = jnp.zeros(s.shape, F32)
            for r in dc_r:
                r[...] = jnp.zeros(r.shape, F32)

        tab_v = [r[...] for r in tab_r]
        xt_v = [r[...] for r in xt_r]

        def g(c, x, s):
            (y,), new = f(tab_v, c, x, xt_v, s)
            return y, new

        _, vjp = jax.vjp(g, [r[...] for r in c_r], [r[...] for r in x_r], [r[...] for r in sv_r])
        dc, dx, ds = vjp((dy_ref[...], [s[...] for s in ds_r]))
        dx_ref[...] = assemble(dx)
        for r, v in zip(dc_r, dc):
            r[...] += v
        for s, v in zip(ds_r, ds):
            s[...] = v

    win = [pl.BlockSpec((rows, w), lambda j, _cb=cb: (nc - 1 - j, _cb)) for (_, w, cb) in list(xs) + list(xtabs)]
    in_specs = [_full_spec(a) for a in list(tabs) + list(consts)] + win
    args = list(tabs) + list(consts) + [a for (a, _, _) in list(xs) + list(xtabs)]
    for (r, c), sv in zip(state_shapes, saved):
        in_specs.append(pl.BlockSpec((r, c), lambda j: (nc - 1 - j, 0)))
        args.append(sv)
    in_specs.append(pl.BlockSpec((rows, dy[1]), lambda j: (nc - 1 - j, dy[2])))
    args.append(dy[0])
    io_alias = {}
    if alias:
        in_specs.append(pl.BlockSpec(memory_space=pl.ANY))
        io_alias = {len(args): 0}
        args.append(dx_alias)
    out_shape = [jax.ShapeDtypeStruct((seq, dx_total), F32)] + [jax.ShapeDtypeStruct(a.shape, F32) for a in consts]
    out_specs = [pl.BlockSpec((rows, dx_width), lambda j: (nc - 1 - j, dx_cb))] + [_full_spec(a) for a in consts]
    res = pl.pallas_call(
        body, name=name, grid=(nc,), in_specs=in_specs, out_specs=out_specs, out_shape=out_shape,
        scratch_shapes=[pltpu.VMEM(s, F32) for s in state_shapes], input_output_aliases=io_alias,
        compiler_params=pltpu.CompilerParams(dimension_semantics=("arbitrary",), vmem_limit_bytes=VMEM_LIMIT),
    )(*args)
    return res[0], list(res[1:])


def _tile(n, want):
    t = min(n, want)
    while n % t:
        t //= 2
    return t


MATMUL_VMEM_BUDGET = 40 * 1024 * 1024


def _pick_tiles(m, n, k, sa, sb, so, se):
    best = None
    for tn in {_tile(n, 1024), _tile(n, 512)}:
        for tm in {_tile(m, t) for t in (2048, 1024, 512)}:
            for tk in {_tile(k, t) for t in (4096, 2048, 1024, 512)}:
                at, bt, ot = tm * tk * sa, tk * tn * sb, tm * tn * so
                need = 2 * (at + bt + ot + tm * tn * se) + 2 * tm * tn * 4 + (at if sa == 4 else 0) + (bt if sb == 4 else 0)
                if need > MATMUL_VMEM_BUDGET:
                    continue
                key = ((m // tm) * (n // tn) * (k // tk), k // tk, -tm, -tn)
                if best is None or key < best[0]:
                    best = (key, (tm, tn, tk))
    assert best is not None, (m, n, k)
    return best[1]


def _matmul(name, a, b, mode, out_dtype=F32, a_pro=None, epi=None, epi_arr=None):
    if mode == "nn":
        (m, k), (k2, n) = a.shape, b.shape
    elif mode == "nt":
        (m, k), (n, k2) = a.shape, b.shape
    else:
        (k, m), (k2, n) = a.shape, b.shape
    assert k == k2, (name, a.shape, b.shape)
    size = lambda t: jnp.dtype(t).itemsize
    tm, tn, tk = _pick_tiles(m, n, k, size(a.dtype), size(b.dtype), size(out_dtype),
                             0 if epi is None else size(epi_arr.dtype))
    nk = k // tk
    ca, cb = {"nn": (1, 0), "nt": (1, 1), "tn": (0, 0)}[mode]

    def body(*refs):
        refs = list(refs)
        acc = refs.pop() if nk > 1 else None
        a_ref, b_ref = refs[0], refs[1]
        e_ref = refs[2] if epi is not None else None
        o_ref = refs[-1]

        av = a_ref[...]
        if a_pro == "relu2":
            r = jnp.maximum(av, 0.0)
            av = r * r
        part = _dg(_lo(av), _lo(b_ref[...]), ca, cb)

        def finish(r):
            if epi == "add":
                r = r + e_ref[...]
            elif epi == "drelu2":
                r = r * (2.0 * jnp.maximum(e_ref[...], 0.0))
            o_ref[...] = r.astype(out_dtype)

        if nk == 1:
            finish(part)
        else:
            kk = pl.program_id(2)

            @pl.when(kk == 0)
            def _():
                acc[...] = part

            @pl.when(kk > 0)
            def _():
                acc[...] += part

            @pl.when(kk == nk - 1)
            def _():
                finish(acc[...])

    if mode == "tn":
        a_spec = pl.BlockSpec((tk, tm), lambda j, i, kk: (kk, i))
    else:
        a_spec = pl.BlockSpec((tm, tk), lambda j, i, kk: (i, kk))
    if mode == "nt":
        b_spec = pl.BlockSpec((tn, tk), lambda j, i, kk: (j, kk))
    else:
        b_spec = pl.BlockSpec((tk, tn), lambda j, i, kk: (kk, j))
    o_spec = pl.BlockSpec((tm, tn), lambda j, i, kk: (i, j))
    in_specs, args = [a_spec, b_spec], [a, b]
    if epi is not None:
        in_specs.append(o_spec)
        args.append(epi_arr)
    return pl.pallas_call(
        body, name=name, grid=(n // tn, m // tm, nk), in_specs=in_specs, out_specs=o_spec,
        out_shape=jax.ShapeDtypeStruct((m, n), out_dtype),
        scratch_shapes=[pltpu.VMEM((tm, tn), F32)] if nk > 1 else [],
        compiler_params=pltpu.CompilerParams(dimension_semantics=("parallel", "parallel", "arbitrary"),
                                             vmem_limit_bytes=VMEM_LIMIT),
    )(*args)


ROW_TILE = 512


def _rmsnorm_fwd(name, x, w):
    seq, d = x.shape
    tr = _tile(seq, ROW_TILE)

    def body(x_ref, w_ref, o_ref):
        xv = x_ref[...]
        o_ref[...] = (_unit_rms(xv) * w_ref[...]).astype(_MXU_DTYPE)

    return pl.pallas_call(
        body, name=name, grid=(seq // tr,),
        in_specs=[pl.BlockSpec((tr, d), lambda i: (i, 0)), pl.BlockSpec((1, d), lambda i: (0, 0))],
        out_specs=pl.BlockSpec((tr, d), lambda i: (i, 0)), out_shape=jax.ShapeDtypeStruct((seq, d), _MXU_DTYPE),
        compiler_params=pltpu.CompilerParams(dimension_semantics=("parallel",), vmem_limit_bytes=VMEM_LIMIT),
    )(x, w)


def _rmsnorm_bwd(name, dh, x, w, dres):
    seq, d = x.shape
    tr = _tile(seq, ROW_TILE)

    def body(dh_ref, x_ref, w_ref, dres_ref, dx_ref, dw_ref):
        @pl.when(pl.program_id(0) == 0)
        def _():
            dw_ref[...] = jnp.zeros(dw_ref.shape, F32)

        xv = x_ref[...]
        rstd = lax.rsqrt(jnp.mean(xv * xv, axis=-1, keepdims=True) + EPS)
        xh = xv * rstd
        dhv = dh_ref[...]
        g = dhv * w_ref[...]
        dx_ref[...] = dres_ref[...] + rstd * (g - xh * jnp.mean(g * xh, axis=-1, keepdims=True))
        dw_ref[...] += jnp.sum(dhv * xh, axis=0, keepdims=True)

    row = pl.BlockSpec((tr, d), lambda i: (i, 0))
    vec = pl.BlockSpec((1, d), lambda i: (0, 0))
    return pl.pallas_call(
        body, name=name, grid=(seq // tr,), in_specs=[row, row, vec, row], out_specs=[row, vec],
        out_shape=[jax.ShapeDtypeStruct((seq, d), F32), jax.ShapeDtypeStruct((1, d), F32)],
        compiler_params=pltpu.CompilerParams(dimension_semantics=("arbitrary",), vmem_limit_bytes=VMEM_LIMIT),
    )(dh, x, w, dres)


def _loss_head(name, x, w, target):
    seq, d = x.shape
    tr = _tile(seq, ROW_TILE)

    def body(x_ref, w_ref, t_ref, loss_ref, dx_ref, dw_ref):
        @pl.when(pl.program_id(0) == 0)
        def _():
            dw_ref[...] = jnp.zeros(dw_ref.shape, F32)
            loss_ref[...] = jnp.zeros(loss_ref.shape, F32)

        xv = x_ref[...]
        rstd = lax.rsqrt(jnp.mean(xv * xv, axis=-1, keepdims=True) + EPS)
        xh = xv * rstd
        err = xh * w_ref[...] - t_ref[...]
        per_row = jnp.mean(err * err, axis=-1, keepdims=True)
        loss_ref[...] += 0.5 * jnp.sum(per_row, axis=0, keepdims=True)
        dy = err * (1.0 / d)
        g = dy * w_ref[...]
        dx_ref[...] = rstd * (g - xh * jnp.mean(g * xh, axis=-1, keepdims=True))
        dw_ref[...] += jnp.sum(dy * xh, axis=0, keepdims=True)

    row = pl.BlockSpec((tr, d), lambda i: (i, 0))
    vec = pl.BlockSpec((1, d), lambda i: (0, 0))
    one = pl.BlockSpec((1, 1), lambda i: (0, 0))
    return pl.pallas_call(
        body, name=name, grid=(seq // tr,), in_specs=[row, vec, row], out_specs=[one, row, vec],
        out_shape=[jax.ShapeDtypeStruct((1, 1), F32), jax.ShapeDtypeStruct((seq, d), F32),
                   jax.ShapeDtypeStruct((1, d), F32)],
        compiler_params=pltpu.CompilerParams(dimension_semantics=("arbitrary",), vmem_limit_bytes=VMEM_LIMIT),
    )(x, w, target)


SLAB_TILE_ROWS = 1024


def _slab_tile(rows, cap=SLAB_TILE_ROWS):
    step = 16 if rows % 16 == 0 else 8
    return max(t for t in range(step, min(rows, cap) + 1, step) if rows % t == 0)


def _adamw(name, w, g, m, v):
    rows, cols = w.shape
    tr = _slab_tile(rows, SLAB_TILE_ROWS // 2) if rows % 8 == 0 else rows

    def body(w_ref, g_ref, m_ref, v_ref, d_ref, nm_ref, nv_ref):
        gv = g_ref[...]
        nm = ADAM_B1 * m_ref[...] + (1.0 - ADAM_B1) * gv
        nv = ADAM_B2 * v_ref[...] + (1.0 - ADAM_B2) * (gv * gv)
        m_hat = nm / (1.0 - ADAM_B1 ** ADAM_STEP)
        v_hat = nv / (1.0 - ADAM_B2 ** ADAM_STEP)
        d_ref[...] = -ADAM_LR * (m_hat / (jnp.sqrt(v_hat) + ADAM_EPS) + ADAM_WD * w_ref[...])
        nm_ref[...] = nm
        nv_ref[...] = nv

    spec = pl.BlockSpec((tr, cols), lambda i: (i, 0))
    sds = jax.ShapeDtypeStruct(w.shape, F32)
    return pl.pallas_call(
        body, name=name, grid=(rows // tr,), in_specs=[spec] * 4, out_specs=[spec] * 3, out_shape=[sds] * 3,
        compiler_params=pltpu.CompilerParams(dimension_semantics=("parallel",), vmem_limit_bytes=VMEM_LIMIT),
    )(w, g, m, v)


WIRE_DTYPE = jnp.bfloat16


def _add_halves(name, g, t1, c):
    nsec, rows, _ = g.shape
    rh = rows // 2
    tr = _slab_tile(rh)
    nb = rh // tr

    def body(c_ref, g_ref, t_ref, o_ref):
        o_ref[...] = (g_ref[...] + t_ref[...]).astype(o_ref.dtype)

    gs = pltpu.PrefetchScalarGridSpec(
        num_scalar_prefetch=1, grid=(nsec, nb),
        in_specs=[pl.BlockSpec((1, tr, LANES), lambda s, i, c_ref: (s, c_ref[0] * nb + i, 0)),
                  pl.BlockSpec((1, tr, LANES), lambda s, i, c_ref: (s, i, 0))],
        out_specs=pl.BlockSpec((1, tr, LANES), lambda s, i, c_ref: (s, i, 0)))
    return pl.pallas_call(
        body, name=name, grid_spec=gs, out_shape=jax.ShapeDtypeStruct((nsec, rh, LANES), WIRE_DTYPE),
        compiler_params=pltpu.CompilerParams(dimension_semantics=("parallel", "parallel"),
                                             vmem_limit_bytes=VMEM_LIMIT),
    )(c, g, t1)


ANY = pl.BlockSpec(memory_space=pl.ANY)


def _place():
    return lax.axis_index("x"), lax.axis_index("y"), lax.axis_index("c")


def _all_gather_shards(name, slab):
    rows = slab.shape[0]
    rh = rows // 2
    rq = rh // 2

    def body(x_ref, out_ref, send_sems, recv_sems, local_sem):
        x, y, c = _place()
        me, sibling = (x, y, c), (x, y, 1 - c)
        xn, yn, dg = (1 - x, y), (x, 1 - y), (1 - x, 1 - y)

        def piece(chip, core, q):
            return out_ref.at[2 * chip[0] + chip[1], pl.ds(core * rh + q * rq, rq), :]

        def copy(k, chip, core, q, to, src=None):
            return pltpu.make_async_remote_copy(
                src_ref=piece(chip, core, q) if src is None else src, dst_ref=piece(chip, core, q),
                send_sem=send_sems.at[k], recv_sem=recv_sems.at[k], device_id=to, device_id_type=MESH)

        mine = pltpu.make_async_copy(x_ref, out_ref.at[2 * x + y], local_sem)
        mine.start()
        own = [x_ref.at[pl.ds(c * rh + q * rq, rq), :] for q in range(2)]
        sends = [copy(0, (x, y), c, 0, (*xn, c), src=own[0]), copy(1, (x, y), c, 1, (*xn, c), src=own[1]),
                 copy(2, (x, y), c, 0, (*yn, c), src=own[0]), copy(3, (x, y), c, 1, (*yn, c), src=own[1])]
        for cp in sends:
            cp.start()
        landed = [(0, xn, 0), (3, yn, 1), (1, xn, 1), (2, yn, 0), (4, dg, 0), (5, dg, 1)]
        onward = {0: (4, (*yn, c)), 3: (5, (*xn, c))}
        for i, (k, chip, q) in enumerate(landed):
            copy(k, chip, c, q, me).wait_recv()
            if k in onward:
                fk, to = onward[k]
                sends.append(copy(fk, chip, c, q, to))
                sends[-1].start()
            sends.append(copy(6 + i, chip, c, q, sibling))
            sends[-1].start()
        for i, (k, chip, q) in enumerate(landed):
            copy(6 + i, chip, 1 - c, q, me).wait_recv()
        for cp in sends:
            cp.wait_send()
        mine.wait()

    return pl.pallas_call(
        body, name=name, in_specs=[ANY], out_specs=ANY,
        out_shape=jax.ShapeDtypeStruct((4, rows, LANES), slab.dtype),
        scratch_shapes=[pltpu.SemaphoreType.DMA((12,)), pltpu.SemaphoreType.DMA((12,)), pltpu.SemaphoreType.DMA],
    )(slab)


def _swap_halves(name, g):
    nsec, rows, _ = g.shape
    rh = rows // 2

    def body(g_ref, t_ref, send_sem, recv_sem):
        x, y, c = _place()
        cp = pltpu.make_async_remote_copy(
            src_ref=g_ref.at[:, pl.ds((1 - c) * rh, rh), :], dst_ref=t_ref, send_sem=send_sem, recv_sem=recv_sem,
            device_id=(x, y, 1 - c), device_id_type=MESH)
        cp.start()
        cp.wait()

    return pl.pallas_call(
        body, name=name, in_specs=[ANY], out_specs=ANY, out_shape=jax.ShapeDtypeStruct((nsec, rh, LANES), F32),
        scratch_shapes=[pltpu.SemaphoreType.DMA, pltpu.SemaphoreType.DMA],
    )(g)


def _exchange_stage1(name, p):
    _, rh, _ = p.shape
    rq = rh // 2

    def body(p_ref, fx_ref, fy_ref, send_sems, recv_sems):
        x, y, c = _place()
        to_x = pltpu.make_async_remote_copy(
            src_ref=p_ref.at[pl.ds(2 * (1 - x), 2), pl.ds(0, rq), :], dst_ref=fx_ref, send_sem=send_sems.at[0],
            recv_sem=recv_sems.at[0], device_id=(1 - x, y, c), device_id_type=MESH)
        to_y = [pltpu.make_async_remote_copy(
            src_ref=p_ref.at[2 * sx + (1 - y), pl.ds(rq, rq), :], dst_ref=fy_ref.at[sx], send_sem=send_sems.at[1 + sx],
            recv_sem=recv_sems.at[1 + sx], device_id=(x, 1 - y, c), device_id_type=MESH) for sx in range(2)]
        for cp in [to_x] + to_y:
            cp.start()
        for cp in [to_x] + to_y:
            cp.wait_recv()
        for cp in [to_x] + to_y:
            cp.wait_send()

    sds = jax.ShapeDtypeStruct((2, rq, LANES), p.dtype)
    return pl.pallas_call(
        body, name=name, in_specs=[ANY], out_specs=[ANY, ANY], out_shape=[sds, sds],
        scratch_shapes=[pltpu.SemaphoreType.DMA((3,)), pltpu.SemaphoreType.DMA((3,))],
    )(p)


def _exchange_add1(name, p, from_x, from_y, place):
    _, rh, _ = p.shape
    rq = rh // 2
    tr = _slab_tile(rq)
    nb = rq // tr

    def body(xy_ref, pa_s, pa_k, pb_s, pb_k, fx_s, fx_k, fy_s, fy_k, sa, ka, sb, kb):
        for mine, theirs, out in ((pa_s, fx_s, sa), (pa_k, fx_k, ka), (pb_s, fy_s, sb), (pb_k, fy_k, kb)):
            out[...] = (mine[0].astype(F32) + theirs[0].astype(F32)).astype(out.dtype)

    blk = lambda fn: pl.BlockSpec((1, tr, LANES), fn)
    gs = pltpu.PrefetchScalarGridSpec(
        num_scalar_prefetch=1, grid=(nb,),
        in_specs=[blk(lambda i, xy: (2 * xy[0] + 1 - xy[1], i, 0)), blk(lambda i, xy: (2 * xy[0] + xy[1], i, 0)),
                  blk(lambda i, xy: (2 * (1 - xy[0]) + xy[1], nb + i, 0)), blk(lambda i, xy: (2 * xy[0] + xy[1], nb + i, 0)),
                  blk(lambda i, xy: (1 - xy[1], i, 0)), blk(lambda i, xy: (xy[1], i, 0)),
                  blk(lambda i, xy: (1 - xy[0], i, 0)), blk(lambda i, xy: (xy[0], i, 0))],
        out_specs=[pl.BlockSpec((tr, LANES), lambda i, xy: (i, 0))] * 4)
    sds = jax.ShapeDtypeStruct((rq, LANES), p.dtype)
    return pl.pallas_call(
        body, name=name, grid_spec=gs, out_shape=[sds] * 4,
        compiler_params=pltpu.CompilerParams(dimension_semantics=("parallel",), vmem_limit_bytes=VMEM_LIMIT),
    )(place, p, p, p, p, from_x, from_x, from_y, from_y)


def _exchange_stage2(name, send_a, send_b):
    def body(a_ref, b_ref, fa_ref, fb_ref, send_sems, recv_sems):
        x, y, c = _place()
        cps = [pltpu.make_async_remote_copy(src_ref=a_ref, dst_ref=fa_ref, send_sem=send_sems.at[0],
                                            recv_sem=recv_sems.at[0], device_id=(x, 1 - y, c), device_id_type=MESH),
               pltpu.make_async_remote_copy(src_ref=b_ref, dst_ref=fb_ref, send_sem=send_sems.at[1],
                                            recv_sem=recv_sems.at[1], device_id=(1 - x, y, c), device_id_type=MESH)]
        for cp in cps:
            cp.start()
        for cp in cps:
            cp.wait_recv()
        for cp in cps:
            cp.wait_send()

    sds = jax.ShapeDtypeStruct(send_a.shape, send_a.dtype)
    return pl.pallas_call(
        body, name=name, in_specs=[ANY, ANY], out_specs=[ANY, ANY], out_shape=[sds, sds],
        scratch_shapes=[pltpu.SemaphoreType.DMA((2,)), pltpu.SemaphoreType.DMA((2,))],
    )(send_a, send_b)


def _exchange_add2(name, keep_a, got_a, keep_b, got_b):
    rq = keep_a.shape[0]
    tr = _slab_tile(rq)

    def body(ka, ga, kb, gb, o_ref):
        o_ref[0] = ka[...].astype(F32) + ga[...].astype(F32)
        o_ref[1] = kb[...].astype(F32) + gb[...].astype(F32)

    spec = pl.BlockSpec((tr, LANES), lambda i: (i, 0))
    out = pl.pallas_call(
        body, name=name, grid=(rq // tr,), in_specs=[spec] * 4,
        out_specs=pl.BlockSpec((2, tr, LANES), lambda i: (0, i, 0)),
        out_shape=jax.ShapeDtypeStruct((2, rq, LANES), F32),
        compiler_params=pltpu.CompilerParams(dimension_semantics=("parallel",), vmem_limit_bytes=VMEM_LIMIT),
    )(keep_a, got_a, keep_b, got_b)
    return out.reshape(2 * rq, LANES)


def _join_halves(name, r_half):
    rh = r_half.shape[0]

    def body(h_ref, o_ref, send_sem, recv_sem, local_sem):
        x, y, c = _place()
        mine = pltpu.make_async_copy(h_ref, o_ref.at[pl.ds(c * rh, rh), :], local_sem)
        mine.start()
        cp = pltpu.make_async_remote_copy(
            src_ref=h_ref, dst_ref=o_ref.at[pl.ds(c * rh, rh), :], send_sem=send_sem, recv_sem=recv_sem,
            device_id=(x, y, 1 - c), device_id_type=MESH)
        cp.start()
        pltpu.make_async_remote_copy(
            src_ref=h_ref, dst_ref=o_ref.at[pl.ds((1 - c) * rh, rh), :], send_sem=send_sem, recv_sem=recv_sem,
            device_id=(x, y, 1 - c), device_id_type=MESH).wait_recv()
        cp.wait_send()
        mine.wait()

    return pl.pallas_call(
        body, name=name, in_specs=[ANY], out_specs=ANY, out_shape=jax.ShapeDtypeStruct((2 * rh, LANES), F32),
        scratch_shapes=[pltpu.SemaphoreType.DMA, pltpu.SemaphoreType.DMA, pltpu.SemaphoreType.DMA],
    )(r_half)


def _rows_of(n):
    return -(-n // LANES)


SLAB_ROW_ALIGN = 512


def _flat_rows(arrays, dtype):
    parts = []
    for a in arrays:
        flat = a.reshape(-1).astype(dtype)
        parts.append(jnp.pad(flat, (0, _rows_of(flat.size) * LANES - flat.size)))
    return jnp.concatenate(parts).reshape(-1, LANES)


def _align_rows(slab):
    rows = slab.shape[0]
    return jnp.pad(slab, ((0, -(-rows // SLAB_ROW_ALIGN) * SLAB_ROW_ALIGN - rows), (0, 0)))


def _pack(arrays, dtype):
    return _align_rows(_flat_rows(arrays, dtype))


def _unpack(slab, shapes):
    out, r = [], 0
    for shp in shapes:
        n = math.prod(shp)
        out.append(slab[r:r + _rows_of(n)].reshape(-1)[:n].reshape(shp))
        r += _rows_of(n)
    return out


def _unpack_gathered(g, shapes, kinds):
    out, r = [], 0
    for shp, kind in zip(shapes, kinds):
        n = math.prod(shp)
        blk = g[:, r:r + _rows_of(n)].reshape(4, -1)[:, :n].reshape((4,) + tuple(shp))
        r += _rows_of(n)
        if kind == "col":
            out.append(jnp.moveaxis(blk, 0, 1).reshape(shp[0], 4 * shp[1]))
        else:
            out.append(blk.reshape(4 * shp[0], shp[1]))
    return out


def _shard_block(g, kind, s, local_shape):
    if kind == "col":
        return g[:, s * local_shape[1]:(s + 1) * local_shape[1]]
    if kind == "row":
        return g[s * local_shape[0]:(s + 1) * local_shape[0]]
    return g


def _rotary_tables(seq):
    half = RET_DK // 2
    pos = jnp.arange(seq, dtype=F32)
    inv = ROPE_THETA ** (-jnp.arange(half, dtype=F32) / half)
    ang = pos[:, None] * inv[None, :]
    cos, sin = jnp.cos(ang), jnp.sin(ang)
    return jnp.concatenate([cos, cos], axis=1), jnp.concatenate([-sin, sin], axis=1)


def _retention_tables():
    log_gamma = jnp.log(1.0 - 2.0 ** (-5.0 - jnp.arange(RET_HEADS, dtype=F32)))
    idx = jnp.arange(CHUNK, dtype=F32)
    diff = idx[:, None] - idx[None, :]
    dmask = jnp.exp(jnp.where((diff >= 0)[None], log_gamma[:, None, None] * diff[None], -jnp.inf))
    kdec = jnp.exp(log_gamma[None, :] * (CHUNK - 1.0 - idx)[:, None])
    qdec = jnp.exp(log_gamma[None, :] * (idx + 1.0)[:, None])
    cdec = jnp.exp(log_gamma * CHUNK)[None, :]
    lanes = lambda t: jnp.repeat(t, RET_DK, axis=1)
    return dmask.reshape(RET_HEADS * CHUNK, CHUNK), lanes(kdec), lanes(qdec), lanes(cdec)


def _s5_prep(a_re, a_im, log_step, b_re, b_im, c_re, c_im):
    g, n, c = S5_GROUPS, S5_STATE, S5_GROUP
    lam = lax.complex(a_re, a_im)
    step = jnp.exp(log_step)[:, None]
    lam_bar = jnp.exp(lam * step)
    b_bar = ((lam_bar - 1.0) / lam)[..., None] * lax.complex(b_re, b_im)
    eye = jnp.eye(g, dtype=F32)
    bb_re = (jnp.real(b_bar).transpose(0, 2, 1)[:, :, None, :] * eye[:, None, :, None]).reshape(g * c, g * n)
    bb_im = (jnp.imag(b_bar).transpose(0, 2, 1)[:, :, None, :] * eye[:, None, :, None]).reshape(g * c, g * n)
    cc_re = (c_re.transpose(0, 2, 1)[:, :, None, :] * eye[:, None, :, None]).reshape(g * n, g * c)
    cc_im = (c_im.transpose(0, 2, 1)[:, :, None, :] * eye[:, None, :, None]).reshape(g * n, g * c)
    return (jnp.real(lam_bar).reshape(1, g * n), jnp.imag(lam_bar).reshape(1, g * n),
            jnp.concatenate([bb_re, bb_im], axis=1), cc_re, cc_im)


def kernel(x, l0_norm_mix, l0_w_in, ssd_conv_w, ssd_conv_b, ssd_dt_bias, ssd_A_log, ssd_D, ssd_norm_w, l0_w_out, l0_norm_mlp, l0_w_up, l0_w_down, l1_norm_mix, l1_w_in, gdn_conv_w, gdn_A_log, gdn_dt_bias, gdn_norm_w, s5_A_re, s5_A_im, s5_log_step, s5_B_re, s5_B_im, s5_C_re, s5_C_im, s5_D, s5_w_glu, s5_b_glu, l1_w_out, l1_norm_mlp, l1_w_up, l1_w_down, final_norm, loss_target, m_l0_norm_mix, m_l0_w_in, m_ssd_conv_w, m_ssd_conv_b, m_ssd_dt_bias, m_ssd_A_log, m_ssd_D, m_ssd_norm_w, m_l0_w_out, m_l0_norm_mlp, m_l0_w_up, m_l0_w_down, m_l1_norm_mix, m_l1_w_in, m_gdn_conv_w, m_gdn_A_log, m_gdn_dt_bias, m_gdn_norm_w, m_s5_A_re, m_s5_A_im, m_s5_log_step, m_s5_B_re, m_s5_B_im, m_s5_C_re, m_s5_C_im, m_s5_D, m_s5_w_glu, m_s5_b_glu, m_l1_w_out, m_l1_norm_mlp, m_l1_w_up, m_l1_w_down, m_final_norm, v_l0_norm_mix, v_l0_w_in, v_ssd_conv_w, v_ssd_conv_b, v_ssd_dt_bias, v_ssd_A_log, v_ssd_D, v_ssd_norm_w, v_l0_w_out, v_l0_norm_mlp, v_l0_w_up, v_l0_w_down, v_l1_norm_mix, v_l1_w_in, v_gdn_conv_w, v_gdn_A_log, v_gdn_dt_bias, v_gdn_norm_w, v_s5_A_re, v_s5_A_im, v_s5_log_step, v_s5_B_re, v_s5_B_im, v_s5_C_re, v_s5_C_im, v_s5_D, v_s5_w_glu, v_s5_b_glu, v_l1_w_out, v_l1_norm_mlp, v_l1_w_up, v_l1_w_down, v_final_norm):
    given = dict(locals())
    names = [n for n, _ in PARAMS]
    kinds = dict(PARAMS)
    w = {n: given[n] for n in names}
    seq = x.shape[1]
    x0 = x.reshape(seq, D_MODEL)
    target = loss_target.reshape(seq, D_MODEL)

    gb = _all_gather_shards("gather_weights", _pack([w[n] for n in GATHER_BF16], _MXU_DTYPE))
    full = dict(zip(GATHER_BF16, _unpack_gathered(gb, [w[n].shape for n in GATHER_BF16],
                                                  [kinds[n] for n in GATHER_BF16])))
    gf = _all_gather_shards("gather_conv", _pack([w[n] for n in GATHER_F32], F32))
    full.update(zip(GATHER_F32, _unpack_gathered(gf, [w[n].shape for n in GATHER_F32],
                                                 [kinds[n] for n in GATHER_F32])))
    in0 = full["l0_w_in"].shape[1]
    w_in0 = jnp.pad(full["l0_w_in"], ((0, 0), (0, IN0_PAD - in0)))
    wi1 = full["l1_w_in"]
    in1 = wi1.shape[1]
    w_in1 = jnp.concatenate([wi1[:, :3072], wi1[:, 3084:in1], wi1[:, 3072:3084],
                             jnp.zeros((D_MODEL, IN1_PAD - in1), wi1.dtype)], axis=1)

    row = lambda a: a.reshape(1, -1)
    lanes64 = lambda a: jnp.repeat(a, SSD_HEAD_DIM).reshape(1, -1)

    h0 = _rmsnorm_fwd("norm_mix0", x0, row(w["l0_norm_mix"]))
    proj0 = _matmul("in_proj0", h0, w_in0, "nn")
    cos_t, sin_t = _rotary_tables(seq)
    ret_tabs = list(_retention_tables())
    ret_xs = [(proj0, 512, 0), (proj0, 512, 1), (proj0, 512, 2), (proj0, 512, 3)]
    ret_xt = [(cos_t, 128, 0), (sin_t, 128, 0)]
    ret_states = [(512, 128)]
    mixed0, ret_saved = _scan_fwd("ret_fwd", _f_ret, CHUNK, ret_tabs, [], ret_xs, ret_xt, ret_states, D_MODEL, 512, 0)
    expand = jnp.repeat(jnp.eye(128, SSD_HEADS, dtype=F32), SSD_HEAD_DIM, axis=1)
    ssd_consts = [full["ssd_conv_w"], row(w["ssd_conv_b"]), lanes64(w["ssd_dt_bias"]), lanes64(w["ssd_A_log"]),
                  lanes64(w["ssd_D"]), row(w["ssd_norm_w"])]
    ssd_xs = [(proj0, 512, 4), (proj0, 512, 5), (proj0, 256, 12), (proj0, 256, 13), (proj0, 128, 28)]
    ssd_states = [(8, 512), (8, 256), (8, 256), (512, 128)]
    mixed0, ssd_saved = _scan_fwd("ssd_fwd", _f_ssd, CHUNK, [expand], ssd_consts, ssd_xs, [], ssd_states,
                                  D_MODEL, 512, 1, y_alias=mixed0)
    x1 = _matmul("out_proj0", mixed0, full["l0_w_out"], "nn", epi="add", epi_arr=x0)
    h1 = _rmsnorm_fwd("norm_mlp0", x1, row(w["l0_norm_mlp"]))
    u0 = _matmul("up0", h1, full["l0_w_up"], "nn")
    x2 = _matmul("down0", u0, full["l0_w_down"], "nn", a_pro="relu2", epi="add", epi_arr=x1)

    h2 = _rmsnorm_fwd("norm_mix1", x2, row(w["l1_norm_mix"]))
    proj1 = _matmul("in_proj1", h2, w_in1, "nn")
    p_alog = jnp.zeros((1, 128), F32).at[0, 6:12].set(w["gdn_A_log"])
    p_dtb = jnp.zeros((1, 128), F32).at[0, 6:12].set(w["gdn_dt_bias"])
    gdn_consts = [full["gdn_conv_w"], p_alog, p_dtb, row(w["gdn_norm_w"])]
    gdn_xs = [(proj1, 768, 0), (proj1, 768, 1), (proj1, 768, 2), (proj1, 768, 3), (proj1, 128, 26)]
    gdn_states = [(8, 768), (8, 768), (8, 768), (768, 256)]
    mixed1, gdn_saved = _scan_fwd("gdn_fwd", _f_gdn, CHUNK, [], gdn_consts, gdn_xs, [], gdn_states, D_MODEL, 768, 0)
    s5_args = (w["s5_A_re"], w["s5_A_im"], w["s5_log_step"], w["s5_B_re"], w["s5_B_im"], w["s5_C_re"], w["s5_C_im"])
    (lam_re, lam_im, bblk, cc_re, cc_im), s5_prep_vjp = jax.vjp(_s5_prep, *s5_args)
    s5_consts = [lam_re, lam_im, bblk, cc_re, cc_im, row(w["s5_D"]), full["s5_w_glu"].astype(F32), row(w["s5_b_glu"])]
    s5_xs = [(proj1, 256, 12)]
    s5_states = [(8, 1024), (8, 1024)]
    mixed1, s5_saved = _scan_fwd("s5_fwd", _f_s5, CHUNK, [], s5_consts, s5_xs, [], s5_states, D_MODEL, 256, 3,
                                 y_alias=mixed1)
    x3 = _matmul("out_proj1", mixed1, full["l1_w_out"], "nn", epi="add", epi_arr=x2)
    h3 = _rmsnorm_fwd("norm_mlp1", x3, row(w["l1_norm_mlp"]))
    u1 = _matmul("up1", h3, full["l1_w_up"], "nn")
    x4 = _matmul("down1", u1, full["l1_w_down"], "nn", a_pro="relu2", epi="add", epi_arr=x3)

    loss_part, dx4, d_final = _loss_head("loss_head", x4, row(w["final_norm"]), target)
    loss = lax.psum(loss_part[0, 0], ("x", "y", "c"))
    grads = {"final_norm": d_final.reshape(-1)}

    du1 = _matmul("down1_dx", dx4, full["l1_w_down"], "nt", out_dtype=_MXU_DTYPE, epi="drelu2", epi_arr=u1)
    grads["l1_w_down"] = _matmul("down1_dw", u1, dx4, "tn", a_pro="relu2")
    grads["l1_w_up"] = _matmul("up1_dw", h3, du1, "tn")
    dh3 = _matmul("up1_dx", du1, full["l1_w_up"], "nt")
    dx3, dwn = _rmsnorm_bwd("norm_mlp1_bwd", dh3, x3, row(w["l1_norm_mlp"]), dx4)
    grads["l1_norm_mlp"] = dwn.reshape(-1)
    grads["l1_w_out"] = _matmul("out_proj1_dw", mixed1, dx3, "tn")
    dmixed1 = _matmul("out_proj1_dx", dx3, full["l1_w_out"], "nt")

    def gdn_assemble(dx):
        dq, dk, dv, dz, dba = dx
        zeros = lambda n: jnp.zeros((dq.shape[0], n), F32)
        return jnp.concatenate([dq, dk, dv, dz, zeros(256), dba, zeros(IN1_PAD - 3456)], axis=1)

    dproj1, gdn_dc = _scan_bwd("gdn_bwd", _f_gdn, CHUNK, [], gdn_consts, gdn_xs, [], gdn_saved, gdn_states,
                               (dmixed1, 768, 0), IN1_PAD, IN1_PAD, 0, gdn_assemble)
    dproj1, s5_dc = _scan_bwd("s5_bwd", _f_s5, CHUNK, [], s5_consts, s5_xs, [], s5_saved, s5_states,
                              (dmixed1, 256, 3), IN1_PAD, 256, 12, lambda dx: dx[0], dx_alias=dproj1)
    grads["gdn_conv_w"] = gdn_dc[0]
    grads["gdn_A_log"] = gdn_dc[1][0, 6:12]
    grads["gdn_dt_bias"] = gdn_dc[2][0, 6:12]
    grads["gdn_norm_w"] = gdn_dc[3].reshape(-1)
    s5_pg = s5_prep_vjp(tuple(s5_dc[:5]))
    for n, gval in zip(("s5_A_re", "s5_A_im", "s5_log_step", "s5_B_re", "s5_B_im", "s5_C_re", "s5_C_im"), s5_pg):
        grads[n] = gval
    grads["s5_D"] = s5_dc[5].reshape(-1)
    grads["s5_w_glu"] = s5_dc[6]
    grads["s5_b_glu"] = s5_dc[7].reshape(-1)
    dwi1 = _matmul("in_proj1_dw", h2, dproj1, "tn")
    grads["l1_w_in"] = jnp.concatenate([dwi1[:, :3072], dwi1[:, 3328:3340], dwi1[:, 3072:3328]], axis=1)
    dh2 = _matmul("in_proj1_dx", dproj1, w_in1, "nt")
    dx2, dwn = _rmsnorm_bwd("norm_mix1_bwd", dh2, x2, row(w["l1_norm_mix"]), dx3)
    grads["l1_norm_mix"] = dwn.reshape(-1)

    du0 = _matmul("down0_dx", dx2, full["l0_w_down"], "nt", out_dtype=_MXU_DTYPE, epi="drelu2", epi_arr=u0)
    grads["l0_w_down"] = _matmul("down0_dw", u0, dx2, "tn", a_pro="relu2")
    grads["l0_w_up"] = _matmul("up0_dw", h1, du0, "tn")
    dh1 = _matmul("up0_dx", du0, full["l0_w_up"], "nt")
    dx1, dwn = _rmsnorm_bwd("norm_mlp0_bwd", dh1, x1, row(w["l0_norm_mlp"]), dx2)
    grads["l0_norm_mlp"] = dwn.reshape(-1)
    grads["l0_w_out"] = _matmul("out_proj0_dw", mixed0, dx1, "tn")
    dmixed0 = _matmul("out_proj0_dx", dx1, full["l0_w_out"], "nt")
    dproj0, _ = _scan_bwd("ret_bwd", _f_ret, CHUNK, ret_tabs, [], ret_xs, ret_xt, ret_saved, ret_states,
                          (dmixed0, 512, 0), IN0_PAD, 2048, 0, lambda dx: jnp.concatenate(dx, axis=1))

    def ssd_assemble(dx):
        return jnp.concatenate(list(dx) + [jnp.zeros((dx[0].shape[0], 2048 - 1664), F32)], axis=1)

    dproj0, ssd_dc = _scan_bwd("ssd_bwd", _f_ssd, CHUNK, [expand], ssd_consts, ssd_xs, [], ssd_saved, ssd_states,
                               (dmixed0, 512, 1), IN0_PAD, 2048, 1, ssd_assemble, dx_alias=dproj0)
    heads = lambda a: a.reshape(SSD_HEADS, SSD_HEAD_DIM).sum(axis=1)
    grads["ssd_conv_w"] = ssd_dc[0]
    grads["ssd_conv_b"] = ssd_dc[1].reshape(-1)
    grads["ssd_dt_bias"] = heads(ssd_dc[2])
    grads["ssd_A_log"] = heads(ssd_dc[3])
    grads["ssd_D"] = heads(ssd_dc[4])
    grads["ssd_norm_w"] = ssd_dc[5].reshape(-1)
    grads["l0_w_in"] = _matmul("in_proj0_dw", h0, dproj0, "tn")[:, :in0]
    dh0 = _matmul("in_proj0_dx", dproj0, w_in0, "nt")
    dx0, dwn = _rmsnorm_bwd("norm_mix0_bwd", dh0, x0, row(w["l0_norm_mix"]), dx1)
    grads["l0_norm_mix"] = dwn.reshape(-1)
    grad_x = dx0.reshape(x.shape)

    c_idx = lax.axis_index("c").astype(jnp.int32).reshape(1)
    small = SMALL_SHARDED + tuple(n for n in names if kinds[n] == "rep")
    order = LARGE + small
    block = lambda n, s: _shard_block(grads[n].reshape(_full_shape(n, w, kinds)), kinds[n], s, w[n].shape)
    rep_rows = _flat_rows([grads[n] for n in order[len(LARGE) + len(SMALL_SHARDED):]], F32)
    gslab = jnp.stack([_align_rows(jnp.concatenate(
        [_flat_rows([block(n, s) for n in LARGE + SMALL_SHARDED], F32), rep_rows])) for s in range(4)])
    from_sibling = _swap_halves("grads_swap_halves", gslab)
    chip_sum = _add_halves("grads_add_sibling", gslab, from_sibling, c_idx)
    place = jnp.stack([lax.axis_index("x"), lax.axis_index("y")]).astype(jnp.int32)
    from_x, from_y = _exchange_stage1("grads_stage1", chip_sum)
    send_a, keep_a, send_b, keep_b = _exchange_add1("grads_add1", chip_sum, from_x, from_y, place)
    got_a, got_b = _exchange_stage2("grads_stage2", send_a, send_b)
    my_half = _exchange_add2("grads_add2", keep_a, got_a, keep_b, got_b)
    gsum = _join_halves("grads_join_halves", my_half)
    grad = dict(zip(order, _unpack(gsum, [w[n].shape for n in order])))

    delta, new_m, new_v = {}, {}, {}
    for n in LARGE:
        delta[n], new_m[n], new_v[n] = _adamw("adamw_" + n, w[n], grad[n], given["m_" + n], given["v_" + n])
    first_small = sum(_rows_of(math.prod(w[n].shape)) for n in LARGE)
    small_shapes = [w[n].shape for n in small]

    def small_slab(arrays):
        rows = _flat_rows(arrays, F32)
        return jnp.pad(rows, ((0, gsum.shape[0] - first_small - rows.shape[0]), (0, 0)))

    res = _adamw("adamw_small", small_slab([w[n] for n in small]), gsum[first_small:],
                 small_slab([given["m_" + n] for n in small]), small_slab([given["v_" + n] for n in small]))
    for out, slab in zip((delta, new_m, new_v), res):
        out.update(zip(small, _unpack(slab, small_shapes)))
    return (loss, grad_x, *[grad[n] for n in names], *[delta[n] for n in names], *[new_m[n] for n in names],
            *[new_v[n] for n in names])


def _full_shape(name, w, kinds):
    shp = w[name].shape
    if kinds[name] == "col":
        return (shp[0], 4 * shp[1])
    if kinds[name] == "row":
        return (4 * shp[0],) + tuple(shp[1:])
    return shp
```

```python
import functools
import math

import jax
import jax.numpy as jnp
from jax import lax
from jax.experimental import pallas as pl
from jax.experimental.pallas import tpu as pltpu

F32 = jnp.float32
_MXU_DTYPE = jnp.bfloat16

D_MODEL = 1024
CHUNK = 64
EPS = 1e-6
RET_HEADS, RET_DK = 4, 128
ROPE_THETA = 10000.0
SSD_HEADS, SSD_HEAD_DIM = 8, 64
GDN_HEADS, GDN_DK = 6, 128
S5_GROUPS, S5_GROUP, S5_STATE = 16, 16, 64
ADAM_LR, ADAM_B1, ADAM_B2, ADAM_EPS, ADAM_WD, ADAM_STEP = 0.001, 0.9, 0.999, 1e-08, 0.01, 10

IN0_PAD = 4096
IN1_PAD = 3584
LANES = 1024
VMEM_LIMIT = 56 * 1024 * 1024
MESH = pl.DeviceIdType.MESH

PARAMS = (
    ("l0_norm_mix", "rep"), ("l0_w_in", "col"), ("ssd_conv_w", "col"), ("ssd_conv_b", "rep"),
    ("ssd_dt_bias", "rep"), ("ssd_A_log", "rep"), ("ssd_D", "rep"), ("ssd_norm_w", "rep"),
    ("l0_w_out", "row"), ("l0_norm_mlp", "rep"), ("l0_w_up", "col"), ("l0_w_down", "row"),
    ("l1_norm_mix", "rep"), ("l1_w_in", "col"), ("gdn_conv_w", "col"), ("gdn_A_log", "rep"),
    ("gdn_dt_bias", "rep"), ("gdn_norm_w", "rep"), ("s5_A_re", "rep"), ("s5_A_im", "rep"),
    ("s5_log_step", "rep"), ("s5_B_re", "rep"), ("s5_B_im", "rep"), ("s5_C_re", "rep"), ("s5_C_im", "rep"),
    ("s5_D", "rep"), ("s5_w_glu", "row"), ("s5_b_glu", "rep"), ("l1_w_out", "row"), ("l1_norm_mlp", "rep"),
    ("l1_w_up", "col"), ("l1_w_down", "row"), ("final_norm", "rep"),
)
GATHER_BF16 = ("l0_w_in", "l0_w_out", "l0_w_up", "l0_w_down", "l1_w_in", "l1_w_out", "l1_w_up", "l1_w_down", "s5_w_glu")
GATHER_F32 = ("ssd_conv_w", "gdn_conv_w")
LARGE = GATHER_BF16[:8]
SMALL_SHARDED = ("s5_w_glu", "ssd_conv_w", "gdn_conv_w")


def _dg(a, b, ca, cb, prec=None):
    return lax.dot_general(a, b, (((ca,), (cb,)), ((), ())), preferred_element_type=F32, precision=prec)


def _lo(a):
    return a.astype(_MXU_DTYPE)


@jax.custom_vjp
def _mm(a, b):
    return _dg(_lo(a), _lo(b), 1, 0)


def _mm_fwd(a, b):
    return _mm(a, b), (a, b)


def _mm_bwd(res, g):
    a, b = res
    return _dg(_lo(g), _lo(b), 1, 1), _dg(_lo(a), _lo(g), 0, 0)


_mm.defvjp(_mm_fwd, _mm_bwd)


@jax.custom_vjp
def _mm_nt(a, b):
    return _dg(_lo(a), _lo(b), 1, 1)


def _mm_nt_fwd(a, b):
    return _mm_nt(a, b), (a, b)


def _mm_nt_bwd(res, g):
    a, b = res
    return _dg(_lo(g), _lo(b), 1, 0), _dg(_lo(g), _lo(a), 0, 0)


_mm_nt.defvjp(_mm_nt_fwd, _mm_nt_bwd)


@jax.custom_vjp
def _mm_tn(a, b):
    return _dg(_lo(a), _lo(b), 0, 0)


def _mm_tn_fwd(a, b):
    return _mm_tn(a, b), (a, b)


def _mm_tn_bwd(res, g):
    a, b = res
    return _dg(_lo(b), _lo(g), 1, 1), _dg(_lo(a), _lo(g), 1, 0)


_mm_tn.defvjp(_mm_tn_fwd, _mm_tn_bwd)


def _split2(x):
    hi = _lo(x)
    return hi, _lo(x - hi.astype(F32))


def _split3(x):
    h1 = _lo(x)
    r1 = x - h1.astype(F32)
    h2 = _lo(r1)
    return h1, h2, _lo(r1 - h2.astype(F32))


def _tri_cum_dir(m, ca):
    n, w = m.shape
    causal, _ = _tri_masks(n)
    out = _dg(causal.astype(_MXU_DTYPE), jnp.concatenate(_split3(m), axis=1), ca, 0)
    return out[:, :w] + out[:, w:2 * w] + out[:, 2 * w:]


@jax.custom_vjp
def _tri_cum(m):
    return _tri_cum_dir(m, 1)


def _tri_cum_fwd(m):
    return _tri_cum_dir(m, 1), None


def _tri_cum_bwd(_, g):
    return (_tri_cum_dir(g, 0),)


_tri_cum.defvjp(_tri_cum_fwd, _tri_cum_bwd)


@jax.custom_vjp
def _mm_exact_rhs(a, e):
    return _dg(jnp.concatenate(_split3(a), axis=1), jnp.concatenate([_lo(e)] * 3, axis=0), 1, 0)


def _mm_exact_rhs_fwd(a, e):
    return _mm_exact_rhs(a, e), e


def _mm_exact_rhs_bwd(e, g):
    return _dg(jnp.concatenate(_split3(g), axis=1), jnp.concatenate([_lo(e)] * 3, axis=1), 1, 1), jnp.zeros_like(e)


_mm_exact_rhs.defvjp(_mm_exact_rhs_fwd, _mm_exact_rhs_bwd)


def _bd(x):
    left = _iota(x.shape, 1) < (x.shape[1] // 2)
    zero = jnp.zeros_like(x)
    return jnp.concatenate([jnp.where(left, x, zero), jnp.where(left, zero, x)], axis=0)


def _unbd(m):
    half = m.shape[0] // 2
    left = _iota((half, m.shape[1]), 1) < (m.shape[1] // 2)
    return jnp.where(left, m[:half], m[half:])


def _pmm_nn(x, y):
    xh, xl = _split2(x)
    yh, yl = _split2(y)
    return _dg(jnp.concatenate([xh, xl, xh], axis=1), jnp.concatenate([_bd(yh), _bd(yh), _bd(yl)], axis=0), 1, 0)


def _pmm_nt(x, y):
    xh, xl = _split2(x)
    yh, yl = _split2(y)
    return _dg(jnp.concatenate([xh, xl, xh], axis=1), jnp.concatenate([_bd(yh), _bd(yh), _bd(yl)], axis=1), 1, 1)


def _pmm_tn(x, y):
    xh, xl = _split2(x)
    yh, yl = _split2(y)
    return _unbd(_dg(jnp.concatenate([xh, xl, xh], axis=0), jnp.concatenate([yh, yh, yl], axis=0), 0, 0))


@functools.lru_cache(maxsize=None)
def _shift(s, axis):
    @jax.custom_vjp
    def sh(x):
        return pltpu.roll(x, s, axis)

    def fwd(x):
        return sh(x), None

    def bwd(_, g):
        n = g.shape[axis]
        return (pltpu.roll(g, (n - s) % n, axis),)

    sh.defvjp(fwd, bwd)
    return sh


def _iota(shape, axis):
    return lax.broadcasted_iota(jnp.int32, shape, axis)


def _silu(x):
    return x * jax.nn.sigmoid(x)


def _unit_rms(x):
    return x * lax.rsqrt(jnp.mean(x * x, axis=-1, keepdims=True) + EPS)


def _l2norm(x):
    return x * lax.rsqrt(jnp.sum(x * x, axis=-1, keepdims=True) + EPS)


def _tri_masks(n):
    r, c = _iota((n, n), 0), _iota((n, n), 1)
    return r >= c, r > c


def _packed_rc():
    return _iota((CHUNK, 2 * CHUNK), 0), _iota((CHUNK, 2 * CHUNK), 1) & (CHUNK - 1)


def _decay_packed(g_packed):
    r, c = _packed_rc()
    seg = _tri_cum(g_packed * (r > c).astype(F32))
    return jnp.where(r >= c, jnp.exp(jnp.where(r >= c, seg, 0.0)), 0.0)


def _conv(x, tail, w):
    rows, width = x.shape
    row = _iota((rows, width), 0)
    acc = x * w[3:4, :]
    pad = jnp.zeros((rows - 8, width), F32)
    for j in range(3):
        s = 3 - j
        prev = jnp.concatenate([_shift(s, 0)(tail), pad], axis=0)
        acc = acc + w[j:j + 1, :] * jnp.where(row < s, prev, _shift(s, 0)(x))
    return acc


def _tri_inv_impl(mats):
    r, c = _packed_rc()
    eye = (r == c).astype(F32)

    def same_block(b):
        return (r // b) == (c // b)

    a8 = [jnp.where(same_block(8), a, 0.0) for a in mats]
    a2 = [_pmm_nn(t, t) for t in a8]
    a4 = [_pmm_nn(t, t) for t in a2]
    x = [_pmm_nn(eye - p, eye + q) for p, q in zip(a8, a2)]
    x = [_pmm_nn(p, eye + q) for p, q in zip(x, a4)]
    for b in (8, 16, 32):
        off = [jnp.where(same_block(2 * b) & jnp.logical_not(same_block(b)), a, 0.0) for a in mats]
        y = [_pmm_nn(p, q) for p, q in zip(x, off)]
        x = [p - _pmm_nn(q, p) for p, q in zip(x, y)]
    return x


@jax.custom_vjp
def _tri_inv(mats):
    return _tri_inv_impl(mats)


def _tri_inv_fwd(mats):
    t = _tri_inv_impl(mats)
    return t, t


def _tri_inv_bwd(t, g):
    m1 = [_pmm_tn(p, q) for p, q in zip(t, g)]
    return ([-_pmm_nt(p, q) for p, q in zip(m1, t)],)


_tri_inv.defvjp(_tri_inv_fwd, _tri_inv_bwd)


def _f_ret(tabs, consts, xs, xtabs, states):
    dmask, kdec, qdec, cdec = tabs
    q, k, v, gate = xs
    cs, sn = xtabs
    (st,) = states
    swap = _shift(RET_DK // 2, 1)
    heads = range(RET_HEADS)
    sls = [slice(128 * h, 128 * h + 128) for h in heads]
    qh = [(q[:, sl] * cs + swap(q[:, sl]) * sn) * (RET_DK ** -0.5) for sl in sls]
    kh = [k[:, sl] * cs + swap(k[:, sl]) * sn for sl in sls]
    sh = [st[sl, :] for sl in sls]
    scores = [_mm_nt(a, b) * dmask[64 * h:64 * h + 64, :] for h, a, b in zip(heads, qh, kh)]
    y = [_mm(s, v[:, sl]) for s, sl in zip(scores, sls)]
    y = [t + _mm(a * qdec[:, sl], s) for t, a, sl, s in zip(y, qh, sls, sh)]
    new = [s * cdec[:, sl] + _mm_tn(b * kdec[:, sl], v[:, sl]) for s, sl, b in zip(sh, sls, kh)]
    outs = [_silu(gate[:, sl]) * _unit_rms(t) for sl, t in zip(sls, y)]
    return (jnp.concatenate(outs, axis=1),), [jnp.concatenate(new, axis=0)]


def _f_ssd(tabs, consts, xs, xtabs, states):
    (expand,) = tabs
    conv_w, conv_b, dtb, alog, dskip, nw = consts
    z, xr, br, cr, dtr = xs
    tx, tb, tc, st = states
    xc = _silu(_conv(xr, tx, conv_w[:, 0:512]) + conv_b[:, 0:512])
    bc = _silu(_conv(br, tb, conv_w[:, 512:768]) + conv_b[:, 512:768])
    cc = _silu(_conv(cr, tc, conv_w[:, 768:1024]) + conv_b[:, 768:1024])
    dt = jax.nn.softplus(_mm_exact_rhs(dtr, expand) + dtb)
    la = dt * (-jnp.exp(alog))
    lacum = _tri_cum(la)
    total = jnp.sum(la, axis=0, keepdims=True)
    xd = xc * dt
    dte, ecum, cdec = jnp.exp(total - lacum), jnp.exp(lacum), jnp.exp(total)
    pairs = range(SSD_HEADS // 2)
    sls = [slice(128 * p, 128 * p + 128) for p in pairs]
    bg = [bc[:, 128 * g:128 * g + 128] for g in range(2)]
    cg = [cc[:, 128 * g:128 * g + 128] for g in range(2)]
    cb2 = [_mm_nt(c, jnp.concatenate([b, b], axis=0)) for b, c in zip(bg, cg)]
    lm = [_decay_packed(la[:, sl]) for sl in sls]
    sp = [st[sl, :] for sl in sls]
    ys = [_mm(cg[p // 2], sp[p]) * ecum[:, sls[p]] for p in pairs]
    ys = [ys[p] + _mm(cb2[p // 2] * lm[p], _bd(xd[:, sls[p]])) for p in pairs]
    new = [sp[p] * cdec[:, sls[p]] + _mm_tn(bg[p // 2], xd[:, sls[p]] * dte[:, sls[p]]) for p in pairs]
    y = jnp.concatenate(ys, axis=1) + dskip * xc
    yg = y * _silu(z)
    out = jnp.concatenate([_unit_rms(yg[:, 0:256]), _unit_rms(yg[:, 256:512])], axis=1) * nw
    return (out,), [xr[CHUNK - 8:, :], br[CHUNK - 8:, :], cr[CHUNK - 8:, :], jnp.concatenate(new, axis=0)]


def _f_gdn(tabs, consts, xs, xtabs, states):
    conv_w, p_alog, p_dtb, nw = consts
    qr, kr, vr, z, ba = xs
    tq, tk, tv, st = states
    qc = _silu(_conv(qr, tq, conv_w[:, 0:768]))
    kc = _silu(_conv(kr, tk, conv_w[:, 768:1536]))
    vc = _silu(_conv(vr, tv, conv_w[:, 1536:2304]))
    gl = -jnp.exp(p_alog) * jax.nn.softplus(ba + p_dtb)
    bl = jax.nn.sigmoid(ba)
    gcum = _tri_cum(gl)
    left128 = _iota((CHUNK, 128), 1) < 64
    left256 = _iota((CHUNK, 256), 1) < 128
    r, c = _packed_rc()
    diag_blocks = (_iota((256, 256), 0) < 128) == (_iota((256, 256), 1) < 128)

    def norm2(t):
        return jnp.concatenate([_l2norm(t[:, 0:128]), _l2norm(t[:, 128:256])], axis=1)

    def pick(arr, off, left, p):
        return jnp.where(left, arr[:, off + 2 * p:off + 2 * p + 1], arr[:, off + 2 * p + 1:off + 2 * p + 2])

    pairs = range(GDN_HEADS // 2)
    sls = [slice(256 * p, 256 * p + 256) for p in pairs]
    qn = [norm2(qc[:, sl]) * (GDN_DK ** -0.5) for sl in sls]
    kn = [norm2(kc[:, sl]) for sl in sls]
    dec = [_decay_packed(pick(gl, 6, left128, p)) for p in pairs]
    g2 = [pick(gl, 6, left256, p) for p in pairs]
    gc2 = [pick(gcum, 6, left256, p) for p in pairs]
    b2 = [pick(bl, 0, left256, p) for p in pairs]
    tot = [jnp.sum(t, axis=0, keepdims=True) for t in g2]
    eg = [jnp.exp(t) for t in gc2]
    et = [jnp.exp(t - s) for t, s in zip(tot, gc2)]
    cd = [jnp.exp(t) for t in tot]
    kb = [k * b for k, b in zip(kn, b2)]
    vb = [vc[:, sl] * b for sl, b in zip(sls, b2)]
    kbd = [_bd(k) for k in kn]
    tm = _tri_inv([jnp.where(r > c, _mm_nt(a, b) * d, 0.0) for a, b, d in zip(kb, kbd, dec)])
    u = [_mm(t, _bd(v)) for t, v in zip(tm, vb)]
    w = [_mm(t, _bd(k * e)) for t, k, e in zip(tm, kb, eg)]
    attn = [_mm_nt(q, k) * d for q, k, d in zip(qn, kbd, dec)]
    sp = [st[sl, :] for sl in sls]
    vn = [a - _mm(b, s) for a, b, s in zip(u, w, sp)]
    o = [_mm(q * e, s) + _mm(a, _bd(v)) for q, e, s, a, v in zip(qn, eg, sp, attn, vn)]
    new = [s * d + jnp.where(diag_blocks, _mm_tn(k * e, v), 0.0) for s, d, k, e, v in zip(sp, cd, kn, et, vn)]
    outs = []
    for p in pairs:
        for hh in range(2):
            osl = slice(128 * hh, 128 * hh + 128)
            zsl = slice(256 * p + 128 * hh, 256 * p + 128 * hh + 128)
            outs.append(_unit_rms(o[p][:, osl]) * nw * _silu(z[:, zsl]))
    return (jnp.concatenate(outs, axis=1),), [qr[CHUNK - 8:, :], kr[CHUNK - 8:, :], vr[CHUNK - 8:, :],
                                             jnp.concatenate(new, axis=0)]


def _f_s5(tabs, consts, xs, xtabs, states):
    lam_re, lam_im, bblk, c_re, c_im, dskip, wglu, bglu = consts
    (u,) = xs
    s_re, s_im = states
    rows = u.shape[0]
    n = lam_re.shape[1]
    bu = _mm(u, bblk)
    hr, hi = bu[:, 0:n], bu[:, n:2 * n]
    row = _iota((rows, n), 0)
    h0r, h0i = s_re[0:1, :], s_im[0:1, :]
    hr = hr + jnp.where(row == 0, lam_re * h0r - lam_im * h0i, 0.0)
    hi = hi + jnp.where(row == 0, lam_re * h0i + lam_im * h0r, 0.0)
    pr, pi = lam_re, lam_im
    d = 1
    while d < rows:
        sr = jnp.where(row >= d, _shift(d, 0)(hr), 0.0)
        si = jnp.where(row >= d, _shift(d, 0)(hi), 0.0)
        hr, hi = hr + pr * sr - pi * si, hi + pr * si + pi * sr
        pr, pi = pr * pr - pi * pi, 2.0 * pr * pi
        d *= 2
    y = _mm(hr, c_re) - _mm(hi, c_im) + dskip * u
    y = jax.nn.gelu(y)
    out = y * jax.nn.sigmoid(_mm(y, wglu) + bglu)
    last_r = jnp.broadcast_to(hr[rows - 1:rows, :], (8, n))
    last_i = jnp.broadcast_to(hi[rows - 1:rows, :], (8, n))
    return (out,), [last_r, last_i]


def _full_spec(a):
    nd = a.ndim
    return pl.BlockSpec(a.shape, lambda i, _nd=nd: (0,) * _nd)


CHUNKS_PER_STEP = 4


def _chunks_per_step(f, rows, n):
    def g(tabs, consts, xs, xtabs, states):
        ys = []
        for i in range(n):
            sl = slice(rows * i, rows * (i + 1))
            (y,), states = f(tabs, consts, [t[sl] for t in xs], [t[sl] for t in xtabs], states)
            ys.append(y)
        return (jnp.concatenate(ys, axis=0),), states

    return g


def _scan_fwd(name, f, rows, tabs, consts, xs, xtabs, state_shapes, y_total, y_width, y_cb, y_alias=None):
    seq = xs[0][0].shape[0]
    per_step = math.gcd(CHUNKS_PER_STEP, seq // rows)
    f = _chunks_per_step(f, rows, per_step)
    rows = rows * per_step
    nc = seq // rows
    nt, ncst, nx, nxt, ns = len(tabs), len(consts), len(xs), len(xtabs), len(state_shapes)
    alias = y_alias is not None

    def body(*refs):
        p = 0
        tab_r = refs[p:p + nt]; p += nt
        c_r = refs[p:p + ncst]; p += ncst
        x_r = refs[p:p + nx]; p += nx
        xt_r = refs[p:p + nxt]; p += nxt
        if alias:
            p += 1
        y_ref = refs[p]; p += 1
        sv_r = refs[p:p + ns]; p += ns
        st_r = refs[p:p + ns]

        @pl.when(pl.program_id(0) == 0)
        def _():
            for s in st_r:
                s[...] = jnp.zeros(s.shape, F32)

        st = [s[...] for s in st_r]
        for r, v in zip(sv_r, st):
            r[...] = v
        (y,), new = f([r[...] for r in tab_r], [r[...] for r in c_r], [r[...] for r in x_r],
                      [r[...] for r in xt_r], st)
        y_ref[...] = y
        for s, v in zip(st_r, new):
            s[...] = v

    win = [pl.BlockSpec((rows, w), lambda i, _cb=cb: (i, _cb)) for (_, w, cb) in list(xs) + list(xtabs)]
    in_specs = [_full_spec(a) for a in list(tabs) + list(consts)] + win
    args = list(tabs) + list(consts) + [a for (a, _, _) in list(xs) + list(xtabs)]
    io_alias = {}
    if alias:
        in_specs.append(pl.BlockSpec(memory_space=pl.ANY))
        io_alias = {len(args): 0}
        args.append(y_alias)
    out_shape = [jax.ShapeDtypeStruct((seq, y_total), F32)]
    out_specs = [pl.BlockSpec((rows, y_width), lambda i: (i, y_cb))]
    for (r, c) in state_shapes:
        out_shape.append(jax.ShapeDtypeStruct((nc * r, c), F32))
        out_specs.append(pl.BlockSpec((r, c), lambda i: (i, 0)))
    res = pl.pallas_call(
        body, name=name, grid=(nc,), in_specs=in_specs, out_specs=out_specs, out_shape=out_shape,
        scratch_shapes=[pltpu.VMEM(s, F32) for s in state_shapes], input_output_aliases=io_alias,
        compiler_params=pltpu.CompilerParams(dimension_semantics=("arbitrary",), vmem_limit_bytes=VMEM_LIMIT),
    )(*args)
    return res[0], list(res[1:])


def _scan_bwd(name, f, rows, tabs, consts, xs, xtabs, saved, state_shapes, dy, dx_total, dx_width, dx_cb,
              assemble, dx_alias=None):
    seq = xs[0][0].shape[0]
    per_step = math.gcd(CHUNKS_PER_STEP, seq // rows)
    f = _chunks_per_step(f, rows, per_step)
    rows = rows * per_step
    nc = seq // rows
    nt, ncst, nx, nxt, ns = len(tabs), len(consts), len(xs), len(xtabs), len(state_shapes)
    alias = dx_alias is not None

    def body(*refs):
        p = 0
        tab_r = refs[p:p + nt]; p += nt
        c_r = refs[p:p + ncst]; p += ncst
        x_r = refs[p:p + nx]; p += nx
        xt_r = refs[p:p + nxt]; p += nxt
        sv_r = refs[p:p + ns]; p += ns
        dy_ref = refs[p]; p += 1
        if alias:
            p += 1
        dx_ref = refs[p]; p += 1
        dc_r = refs[p:p + ncst]; p += ncst
        ds_r = refs[p:p + ns]

        @pl.when(pl.program_id(0) == 0)
        def _():
            for s in ds_r:
                s[...] = jnp.zeros(s.shape, F32)
            for r in dc_r:
                r[...] = jnp.zeros(r.shape, F32)

        tab_v = [r[...] for r in tab_r]
        xt_v = [r[...] for r in xt_r]

        def g(c, x, s):
            (y,), new = f(tab_v, c, x, xt_v, s)
            return y, new

        _, vjp = jax.vjp(g, [r[...] for r in c_r], [r[...] for r in x_r], [r[...] for r in sv_r])
        dc, dx, ds = vjp((dy_ref[...], [s[...] for s in ds_r]))
        dx_ref[...] = assemble(dx)
        for r, v in zip(dc_r, dc):
            r[...] += v
        for s, v in zip(ds_r, ds):
            s[...] = v

    win = [pl.BlockSpec((rows, w), lambda j, _cb=cb: (nc - 1 - j, _cb)) for (_, w, cb) in list(xs) + list(xtabs)]
    in_specs = [_full_spec(a) for a in list(tabs) + list(consts)] + win
    args = list(tabs) + list(consts) + [a for (a, _, _) in list(xs) + list(xtabs)]
    for (r, c), sv in zip(state_shapes, saved):
        in_specs.append(pl.BlockSpec((r, c), lambda j: (nc - 1 - j, 0)))
        args.append(sv)
    in_specs.append(pl.BlockSpec((rows, dy[1]), lambda j: (nc - 1 - j, dy[2])))
    args.append(dy[0])
    io_alias = {}
    if alias:
        in_specs.append(pl.BlockSpec(memory_space=pl.ANY))
        io_alias = {len(args): 0}
        args.append(dx_alias)
    out_shape = [jax.ShapeDtypeStruct((seq, dx_total), F32)] + [jax.ShapeDtypeStruct(a.shape, F32) for a in consts]
    out_specs = [pl.BlockSpec((rows, dx_width), lambda j: (nc - 1 - j, dx_cb))] + [_full_spec(a) for a in consts]
    res = pl.pallas_call(
        body, name=name, grid=(nc,), in_specs=in_specs, out_specs=out_specs, out_shape=out_shape,
        scratch_shapes=[pltpu.VMEM(s, F32) for s in state_shapes], input_output_aliases=io_alias,
        compiler_params=pltpu.CompilerParams(dimension_semantics=("arbitrary",), vmem_limit_bytes=VMEM_LIMIT),
    )(*args)
    return res[0], list(res[1:])


def _tile(n, want):
    t = min(n, want)
    while n % t:
        t //= 2
    return t


MATMUL_VMEM_BUDGET = 40 * 1024 * 1024


def _pick_tiles(m, n, k, sa, sb, so, se):
    best = None
    for tn in {_tile(n, 1024), _tile(n, 512)}:
        for tm in {_tile(m, t) for t in (2048, 1024, 512)}:
            for tk in {_tile(k, t) for t in (4096, 2048, 1024, 512)}:
                at, bt, ot = tm * tk * sa, tk * tn * sb, tm * tn * so
                need = 2 * (at + bt + ot + tm * tn * se) + 2 * tm * tn * 4 + (at if sa == 4 else 0) + (bt if sb == 4 else 0)
                if need > MATMUL_VMEM_BUDGET:
                    continue
                key = ((m // tm) * (n // tn) * (k // tk), k // tk, -tm, -tn)
                if best is None or key < best[0]:
                    best = (key, (tm, tn, tk))
    assert best is not None, (m, n, k)
    return best[1]


def _matmul(name, a, b, mode, out_dtype=F32, a_pro=None, epi=None, epi_arr=None):
    if mode == "nn":
        (m, k), (k2, n) = a.shape, b.shape
    elif mode == "nt":
        (m, k), (n, k2) = a.shape, b.shape
    else:
        (k, m), (k2, n) = a.shape, b.shape
    assert k == k2, (name, a.shape, b.shape)
    size = lambda t: jnp.dtype(t).itemsize
    tm, tn, tk = _pick_tiles(m, n, k, size(a.dtype), size(b.dtype), size(out_dtype),
                             0 if epi is None else size(epi_arr.dtype))
    nk = k // tk
    ca, cb = {"nn": (1, 0), "nt": (1, 1), "tn": (0, 0)}[mode]

    def body(*refs):
        refs = list(refs)
        acc = refs.pop() if nk > 1 else None
        a_ref, b_ref = refs[0], refs[1]
        e_ref = refs[2] if epi is not None else None
        o_ref = refs[-1]

        av = a_ref[...]
        if a_pro == "relu2":
            r = jnp.maximum(av, 0.0)
            av = r * r
        part = _dg(_lo(av), _lo(b_ref[...]), ca, cb)

        def finish(r):
            if epi == "add":
                r = r + e_ref[...]
            elif epi == "drelu2":
                r = r * (2.0 * jnp.maximum(e_ref[...], 0.0))
            o_ref[...] = r.astype(out_dtype)

        if nk == 1:
            finish(part)
        else:
            kk = pl.program_id(2)

            @pl.when(kk == 0)
            def _():
                acc[...] = part

            @pl.when(kk > 0)
            def _():
                acc[...] += part

            @pl.when(kk == nk - 1)
            def _():
                finish(acc[...])

    if mode == "tn":
        a_spec = pl.BlockSpec((tk, tm), lambda j, i, kk: (kk, i))
    else:
        a_spec = pl.BlockSpec((tm, tk), lambda j, i, kk: (i, kk))
    if mode == "nt":
        b_spec = pl.BlockSpec((tn, tk), lambda j, i, kk: (j, kk))
    else:
        b_spec = pl.BlockSpec((tk, tn), lambda j, i, kk: (kk, j))
    o_spec = pl.BlockSpec((tm, tn), lambda j, i, kk: (i, j))
    in_specs, args = [a_spec, b_spec], [a, b]
    if epi is not None:
        in_specs.append(o_spec)
        args.append(epi_arr)
    return pl.pallas_call(
        body, name=name, grid=(n // tn, m // tm, nk), in_specs=in_specs, out_specs=o_spec,
        out_shape=jax.ShapeDtypeStruct((m, n), out_dtype),
        scratch_shapes=[pltpu.VMEM((tm, tn), F32)] if nk > 1 else [],
        compiler_params=pltpu.CompilerParams(dimension_semantics=("parallel", "parallel", "arbitrary"),
                                             vmem_limit_bytes=VMEM_LIMIT),
    )(*args)


ROW_TILE = 512


def _rmsnorm_fwd(name, x, w):
    seq, d = x.shape
    tr = _tile(seq, ROW_TILE)

    def body(x_ref, w_ref, o_ref):
        xv = x_ref[...]
        o_ref[...] = (_unit_rms(xv) * w_ref[...]).astype(_MXU_DTYPE)

    return pl.pallas_call(
        body, name=name, grid=(seq // tr,),
        in_specs=[pl.BlockSpec((tr, d), lambda i: (i, 0)), pl.BlockSpec((1, d), lambda i: (0, 0))],
        out_specs=pl.BlockSpec((tr, d), lambda i: (i, 0)), out_shape=jax.ShapeDtypeStruct((seq, d), _MXU_DTYPE),
        compiler_params=pltpu.CompilerParams(dimension_semantics=("parallel",), vmem_limit_bytes=VMEM_LIMIT),
    )(x, w)


def _rmsnorm_bwd(name, dh, x, w, dres):
    seq, d = x.shape
    tr = _tile(seq, ROW_TILE)

    def body(dh_ref, x_ref, w_ref, dres_ref, dx_ref, dw_ref):
        @pl.when(pl.program_id(0) == 0)
        def _():
            dw_ref[...] = jnp.zeros(dw_ref.shape, F32)

        xv = x_ref[...]
        rstd = lax.rsqrt(jnp.mean(xv * xv, axis=-1, keepdims=True) + EPS)
        xh = xv * rstd
        dhv = dh_ref[...]
        g = dhv * w_ref[...]
        dx_ref[...] = dres_ref[...] + rstd * (g - xh * jnp.mean(g * xh, axis=-1, keepdims=True))
        dw_ref[...] += jnp.sum(dhv * xh, axis=0, keepdims=True)

    row = pl.BlockSpec((tr, d), lambda i: (i, 0))
    vec = pl.BlockSpec((1, d), lambda i: (0, 0))
    return pl.pallas_call(
        body, name=name, grid=(seq // tr,), in_specs=[row, row, vec, row], out_specs=[row, vec],
        out_shape=[jax.ShapeDtypeStruct((seq, d), F32), jax.ShapeDtypeStruct((1, d), F32)],
        compiler_params=pltpu.CompilerParams(dimension_semantics=("arbitrary",), vmem_limit_bytes=VMEM_LIMIT),
    )(dh, x, w, dres)


def _loss_head(name, x, w, target):
    seq, d = x.shape
    tr = _tile(seq, ROW_TILE)

    def body(x_ref, w_ref, t_ref, loss_ref, dx_ref, dw_ref):
        @pl.when(pl.program_id(0) == 0)
        def _():
            dw_ref[...] = jnp.zeros(dw_ref.shape, F32)
            loss_ref[...] = jnp.zeros(loss_ref.shape, F32)

        xv = x_ref[...]
        rstd = lax.rsqrt(jnp.mean(xv * xv, axis=-1, keepdims=True) + EPS)
        xh = xv * rstd
        err = xh * w_ref[...] - t_ref[...]
        per_row = jnp.mean(err * err, axis=-1, keepdims=True)
        loss_ref[...] += 0.5 * jnp.sum(per_row, axis=0, keepdims=True)
        dy = err * (1.0 / d)
        g = dy * w_ref[...]
        dx_ref[...] = rstd * (g - xh * jnp.mean(g * xh, axis=-1, keepdims=True))
        dw_ref[...] += jnp.sum(dy * xh, axis=0, keepdims=True)

    row = pl.BlockSpec((tr, d), lambda i: (i, 0))
    vec = pl.BlockSpec((1, d), lambda i: (0, 0))
    one = pl.BlockSpec((1, 1), lambda i: (0, 0))
    return pl.pallas_call(
        body, name=name, grid=(seq // tr,), in_specs=[row, vec, row], out_specs=[one, row, vec],
        out_shape=[jax.ShapeDtypeStruct((1, 1), F32), jax.ShapeDtypeStruct((seq, d), F32),
                   jax.ShapeDtypeStruct((1, d), F32)],
        compiler_params=pltpu.CompilerParams(dimension_semantics=("arbitrary",), vmem_limit_bytes=VMEM_LIMIT),
    )(x, w, target)


SLAB_TILE_ROWS = 1024


def _slab_tile(rows, cap=SLAB_TILE_ROWS):
    step = 16 if rows % 16 == 0 else 8
    return max(t for t in range(step, min(rows, cap) + 1, step) if rows % t == 0)


def _adamw(name, w, g, m, v):
    rows, cols = w.shape
    tr = _slab_tile(rows, SLAB_TILE_ROWS // 2) if rows % 8 == 0 else rows

    def body(w_ref, g_ref, m_ref, v_ref, d_ref, nm_ref, nv_ref):
        gv = g_ref[...]
        nm = ADAM_B1 * m_ref[...] + (1.0 - ADAM_B1) * gv
        nv = ADAM_B2 * v_ref[...] + (1.0 - ADAM_B2) * (gv * gv)
        m_hat = nm / (1.0 - ADAM_B1 ** ADAM_STEP)
        v_hat = nv / (1.0 - ADAM_B2 ** ADAM_STEP)
        d_ref[...] = -ADAM_LR * (m_hat / (jnp.sqrt(v_hat) + ADAM_EPS) + ADAM_WD * w_ref[...])
        nm_ref[...] = nm
        nv_ref[...] = nv

    spec = pl.BlockSpec((tr, cols), lambda i: (i, 0))
    sds = jax.ShapeDtypeStruct(w.shape, F32)
    return pl.pallas_call(
        body, name=name, grid=(rows // tr,), in_specs=[spec] * 4, out_specs=[spec] * 3, out_shape=[sds] * 3,
        compiler_params=pltpu.CompilerParams(dimension_semantics=("parallel",), vmem_limit_bytes=VMEM_LIMIT),
    )(w, g, m, v)


WIRE_DTYPE = jnp.bfloat16


def _add_halves(name, g, t1, c):
    nsec, rows, _ = g.shape
    rh = rows // 2
    tr = _slab_tile(rh)
    nb = rh // tr

    def body(c_ref, g_ref, t_ref, o_ref):
        o_ref[...] = (g_ref[...] + t_ref[...]).astype(o_ref.dtype)

    gs = pltpu.PrefetchScalarGridSpec(
        num_scalar_prefetch=1, grid=(nsec, nb),
        in_specs=[pl.BlockSpec((1, tr, LANES), lambda s, i, c_ref: (s, c_ref[0] * nb + i, 0)),
                  pl.BlockSpec((1, tr, LANES), lambda s, i, c_ref: (s, i, 0))],
        out_specs=pl.BlockSpec((1, tr, LANES), lambda s, i, c_ref: (s, i, 0)))
    return pl.pallas_call(
        body, name=name, grid_spec=gs, out_shape=jax.ShapeDtypeStruct((nsec, rh, LANES), WIRE_DTYPE),
        compiler_params=pltpu.CompilerParams(dimension_semantics=("parallel", "parallel"),
                                             vmem_limit_bytes=VMEM_LIMIT),
    )(c, g, t1)


ANY = pl.BlockSpec(memory_space=pl.ANY)


def _place():
    return lax.axis_index("x"), lax.axis_index("y"), lax.axis_index("c")


def _all_gather_shards(name, slab):
    rows = slab.shape[0]
    rh = rows // 2
    rq = rh // 2

    def body(x_ref, out_ref, send_sems, recv_sems):
        x, y, c = _place()
        me, sibling = (x, y, c), (x, y, 1 - c)
        xn, yn, dg = (1 - x, y), (x, 1 - y), (1 - x, 1 - y)

        def piece(chip, core, q):
            return out_ref.at[2 * chip[0] + chip[1], pl.ds(core * rh + q * rq, rq), :]

        def copy(k, chip, core, q, to, src=None):
            return pltpu.make_async_remote_copy(
                src_ref=piece(chip, core, q) if src is None else src, dst_ref=piece(chip, core, q),
                send_sem=send_sems.at[k], recv_sem=recv_sems.at[k], device_id=to, device_id_type=MESH)

        own = [x_ref.at[pl.ds(c * rh + q * rq, rq), :] for q in range(2)]
        sends = [copy(0, (x, y), c, 0, (*xn, c), src=own[0]), copy(1, (x, y), c, 1, (*xn, c), src=own[1]),
                 copy(2, (x, y), c, 0, (*yn, c), src=own[0]), copy(3, (x, y), c, 1, (*yn, c), src=own[1])]
        for cp in sends:
            cp.start()
        landed = [(0, xn, 0), (3, yn, 1), (1, xn, 1), (2, yn, 0), (4, dg, 0), (5, dg, 1)]
        onward = {0: (4, (*yn, c)), 3: (5, (*xn, c))}
        for i, (k, chip, q) in enumerate(landed):
            copy(k, chip, c, q, me).wait_recv()
            if k in onward:
                fk, to = onward[k]
                sends.append(copy(fk, chip, c, q, to))
                sends[-1].start()
            sends.append(copy(6 + i, chip, c, q, sibling))
            sends[-1].start()
        for i, (k, chip, q) in enumerate(landed):
            copy(6 + i, chip, 1 - c, q, me).wait_recv()
        for cp in sends:
            cp.wait_send()

    got = pl.pallas_call(
        body, name=name, in_specs=[ANY], out_specs=ANY,
        out_shape=jax.ShapeDtypeStruct((4, rows, LANES), slab.dtype),
        scratch_shapes=[pltpu.SemaphoreType.DMA((12,)), pltpu.SemaphoreType.DMA((12,))],
    )(slab)
    return lax.dynamic_update_slice(got, slab[None], (2 * lax.axis_index("x") + lax.axis_index("y"), 0, 0))


def _swap_halves(name, g):
    nsec, rows, _ = g.shape
    rh = rows // 2

    def body(g_ref, t_ref, send_sem, recv_sem):
        x, y, c = _place()
        cp = pltpu.make_async_remote_copy(
            src_ref=g_ref.at[:, pl.ds((1 - c) * rh, rh), :], dst_ref=t_ref, send_sem=send_sem, recv_sem=recv_sem,
            device_id=(x, y, 1 - c), device_id_type=MESH)
        cp.start()
        cp.wait()

    return pl.pallas_call(
        body, name=name, in_specs=[ANY], out_specs=ANY, out_shape=jax.ShapeDtypeStruct((nsec, rh, LANES), F32),
        scratch_shapes=[pltpu.SemaphoreType.DMA, pltpu.SemaphoreType.DMA],
    )(g)


def _exchange_stage1(name, p):
    _, rh, _ = p.shape
    rq = rh // 2

    def body(p_ref, fx_ref, fy_ref, send_sems, recv_sems):
        x, y, c = _place()
        to_x = pltpu.make_async_remote_copy(
            src_ref=p_ref.at[pl.ds(2 * (1 - x), 2), pl.ds(0, rq), :], dst_ref=fx_ref, send_sem=send_sems.at[0],
            recv_sem=recv_sems.at[0], device_id=(1 - x, y, c), device_id_type=MESH)
        to_y = [pltpu.make_async_remote_copy(
            src_ref=p_ref.at[2 * sx + (1 - y), pl.ds(rq, rq), :], dst_ref=fy_ref.at[sx], send_sem=send_sems.at[1 + sx],
            recv_sem=recv_sems.at[1 + sx], device_id=(x, 1 - y, c), device_id_type=MESH) for sx in range(2)]
        for cp in [to_x] + to_y:
            cp.start()
        for cp in [to_x] + to_y:
            cp.wait_recv()
        for cp in [to_x] + to_y:
            cp.wait_send()

    sds = jax.ShapeDtypeStruct((2, rq, LANES), p.dtype)
    return pl.pallas_call(
        body, name=name, in_specs=[ANY], out_specs=[ANY, ANY], out_shape=[sds, sds],
        scratch_shapes=[pltpu.SemaphoreType.DMA((3,)), pltpu.SemaphoreType.DMA((3,))],
    )(p)


def _exchange_add1(name, p, from_x, from_y, place):
    _, rh, _ = p.shape
    rq = rh // 2
    tr = _slab_tile(rq)
    nb = rq // tr

    def body(xy_ref, pa_s, pa_k, pb_s, pb_k, fx_s, fx_k, fy_s, fy_k, sa, ka, sb, kb):
        for mine, theirs, out in ((pa_s, fx_s, sa), (pa_k, fx_k, ka), (pb_s, fy_s, sb), (pb_k, fy_k, kb)):
            out[...] = (mine[0].astype(F32) + theirs[0].astype(F32)).astype(out.dtype)

    blk = lambda fn: pl.BlockSpec((1, tr, LANES), fn)
    gs = pltpu.PrefetchScalarGridSpec(
        num_scalar_prefetch=1, grid=(nb,),
        in_specs=[blk(lambda i, xy: (2 * xy[0] + 1 - xy[1], i, 0)), blk(lambda i, xy: (2 * xy[0] + xy[1], i, 0)),
                  blk(lambda i, xy: (2 * (1 - xy[0]) + xy[1], nb + i, 0)), blk(lambda i, xy: (2 * xy[0] + xy[1], nb + i, 0)),
                  blk(lambda i, xy: (1 - xy[1], i, 0)), blk(lambda i, xy: (xy[1], i, 0)),
                  blk(lambda i, xy: (1 - xy[0], i, 0)), blk(lambda i, xy: (xy[0], i, 0))],
        out_specs=[pl.BlockSpec((tr, LANES), lambda i, xy: (i, 0))] * 4)
    sds = jax.ShapeDtypeStruct((rq, LANES), p.dtype)
    return pl.pallas_call(
        body, name=name, grid_spec=gs, out_shape=[sds] * 4,
        compiler_params=pltpu.CompilerParams(dimension_semantics=("parallel",), vmem_limit_bytes=VMEM_LIMIT),
    )(place, p, p, p, p, from_x, from_x, from_y, from_y)


def _exchange_stage2(name, send_a, send_b):
    def body(a_ref, b_ref, fa_ref, fb_ref, send_sems, recv_sems):
        x, y, c = _place()
        cps = [pltpu.make_async_remote_copy(src_ref=a_ref, dst_ref=fa_ref, send_sem=send_sems.at[0],
                                            recv_sem=recv_sems.at[0], device_id=(x, 1 - y, c), device_id_type=MESH),
               pltpu.make_async_remote_copy(src_ref=b_ref, dst_ref=fb_ref, send_sem=send_sems.at[1],
                                            recv_sem=recv_sems.at[1], device_id=(1 - x, y, c), device_id_type=MESH)]
        for cp in cps:
            cp.start()
        for cp in cps:
            cp.wait_recv()
        for cp in cps:
            cp.wait_send()

    sds = jax.ShapeDtypeStruct(send_a.shape, send_a.dtype)
    return pl.pallas_call(
        body, name=name, in_specs=[ANY, ANY], out_specs=[ANY, ANY], out_shape=[sds, sds],
        scratch_shapes=[pltpu.SemaphoreType.DMA((2,)), pltpu.SemaphoreType.DMA((2,))],
    )(send_a, send_b)


def _exchange_add2(name, keep_a, got_a, keep_b, got_b, c):
    rq = keep_a.shape[0]
    tr = _slab_tile(rq)

    def body(c_ref, ka, ga, kb, gb, o_ref):
        o_ref[0] = ka[...].astype(F32) + ga[...].astype(F32)
        o_ref[1] = kb[...].astype(F32) + gb[...].astype(F32)

    spec = pl.BlockSpec((tr, LANES), lambda i, c_ref: (i, 0))
    gs = pltpu.PrefetchScalarGridSpec(
        num_scalar_prefetch=1, grid=(rq // tr,), in_specs=[spec] * 4,
        out_specs=pl.BlockSpec((2, tr, LANES), lambda i, c_ref: (c_ref[0], i, 0)))
    out = pl.pallas_call(
        body, name=name, grid_spec=gs, out_shape=jax.ShapeDtypeStruct((4, rq, LANES), F32),
        compiler_params=pltpu.CompilerParams(dimension_semantics=("parallel",), vmem_limit_bytes=VMEM_LIMIT),
    )(c, keep_a, got_a, keep_b, got_b)
    return out.reshape(4 * rq, LANES)


def _join_halves(name, full):
    rh = full.shape[0] // 2

    def body(in_ref, o_ref, send_sem, recv_sem):
        x, y, c = _place()
        cp = pltpu.make_async_remote_copy(
            src_ref=in_ref.at[pl.ds(c * rh, rh), :], dst_ref=o_ref.at[pl.ds(c * rh, rh), :], send_sem=send_sem,
            recv_sem=recv_sem, device_id=(x, y, 1 - c), device_id_type=MESH)
        cp.start()
        pltpu.make_async_remote_copy(
            src_ref=in_ref.at[pl.ds(c * rh, rh), :], dst_ref=o_ref.at[pl.ds((1 - c) * rh, rh), :], send_sem=send_sem,
            recv_sem=recv_sem, device_id=(x, y, 1 - c), device_id_type=MESH).wait_recv()
        cp.wait_send()

    return pl.pallas_call(
        body, name=name, in_specs=[ANY], out_specs=ANY, out_shape=jax.ShapeDtypeStruct(full.shape, full.dtype),
        input_output_aliases={0: 0}, scratch_shapes=[pltpu.SemaphoreType.DMA, pltpu.SemaphoreType.DMA],
    )(full)


def _rows_of(n):
    return -(-n // LANES)


SLAB_ROW_ALIGN = 512


def _flat_rows(arrays, dtype):
    parts = []
    for a in arrays:
        flat = a.reshape(-1).astype(dtype)
        parts.append(jnp.pad(flat, (0, _rows_of(flat.size) * LANES - flat.size)))
    return jnp.concatenate(parts).reshape(-1, LANES)


def _align_rows(slab):
    rows = slab.shape[0]
    return jnp.pad(slab, ((0, -(-rows // SLAB_ROW_ALIGN) * SLAB_ROW_ALIGN - rows), (0, 0)))


def _pack(arrays, dtype):
    return _align_rows(_flat_rows(arrays, dtype))


def _unpack(slab, shapes):
    out, r = [], 0
    for shp in shapes:
        n = math.prod(shp)
        out.append(slab[r:r + _rows_of(n)].reshape(-1)[:n].reshape(shp))
        r += _rows_of(n)
    return out


def _unpack_gathered(g, shapes, kinds):
    out, r = [], 0
    for shp, kind in zip(shapes, kinds):
        n = math.prod(shp)
        blk = g[:, r:r + _rows_of(n)].reshape(4, -1)[:, :n].reshape((4,) + tuple(shp))
        r += _rows_of(n)
        if kind == "col":
            out.append(jnp.moveaxis(blk, 0, 1).reshape(shp[0], 4 * shp[1]))
        else:
            out.append(blk.reshape(4 * shp[0], shp[1]))
    return out


def _shard_block(g, kind, s, local_shape):
    if kind == "col":
        return g[:, s * local_shape[1]:(s + 1) * local_shape[1]]
    if kind == "row":
        return g[s * local_shape[0]:(s + 1) * local_shape[0]]
    return g


def _rotary_tables(seq):
    half = RET_DK // 2
    pos = jnp.arange(seq, dtype=F32)
    inv = ROPE_THETA ** (-jnp.arange(half, dtype=F32) / half)
    ang = pos[:, None] * inv[None, :]
    cos, sin = jnp.cos(ang), jnp.sin(ang)
    return jnp.concatenate([cos, cos], axis=1), jnp.concatenate([-sin, sin], axis=1)


def _retention_tables():
    log_gamma = jnp.log(1.0 - 2.0 ** (-5.0 - jnp.arange(RET_HEADS, dtype=F32)))
    idx = jnp.arange(CHUNK, dtype=F32)
    diff = idx[:, None] - idx[None, :]
    dmask = jnp.exp(jnp.where((diff >= 0)[None], log_gamma[:, None, None] * diff[None], -jnp.inf))
    kdec = jnp.exp(log_gamma[None, :] * (CHUNK - 1.0 - idx)[:, None])
    qdec = jnp.exp(log_gamma[None, :] * (idx + 1.0)[:, None])
    cdec = jnp.exp(log_gamma * CHUNK)[None, :]
    lanes = lambda t: jnp.repeat(t, RET_DK, axis=1)
    return dmask.reshape(RET_HEADS * CHUNK, CHUNK), lanes(kdec), lanes(qdec), lanes(cdec)


def _s5_prep(a_re, a_im, log_step, b_re, b_im, c_re, c_im):
    g, n, c = S5_GROUPS, S5_STATE, S5_GROUP
    lam = lax.complex(a_re, a_im)
    step = jnp.exp(log_step)[:, None]
    lam_bar = jnp.exp(lam * step)
    b_bar = ((lam_bar - 1.0) / lam)[..., None] * lax.complex(b_re, b_im)
    eye = jnp.eye(g, dtype=F32)
    bb_re = (jnp.real(b_bar).transpose(0, 2, 1)[:, :, None, :] * eye[:, None, :, None]).reshape(g * c, g * n)
    bb_im = (jnp.imag(b_bar).transpose(0, 2, 1)[:, :, None, :] * eye[:, None, :, None]).reshape(g * c, g * n)
    cc_re = (c_re.transpose(0, 2, 1)[:, :, None, :] * eye[:, None, :, None]).reshape(g * n, g * c)
    cc_im = (c_im.transpose(0, 2, 1)[:, :, None, :] * eye[:, None, :, None]).reshape(g * n, g * c)
    return (jnp.real(lam_bar).reshape(1, g * n), jnp.imag(lam_bar).reshape(1, g * n),
            jnp.concatenate([bb_re, bb_im], axis=1), cc_re, cc_im)


def kernel(x, l0_norm_mix, l0_w_in, ssd_conv_w, ssd_conv_b, ssd_dt_bias, ssd_A_log, ssd_D, ssd_norm_w, l0_w_out, l0_norm_mlp, l0_w_up, l0_w_down, l1_norm_mix, l1_w_in, gdn_conv_w, gdn_A_log, gdn_dt_bias, gdn_norm_w, s5_A_re, s5_A_im, s5_log_step, s5_B_re, s5_B_im, s5_C_re, s5_C_im, s5_D, s5_w_glu, s5_b_glu, l1_w_out, l1_norm_mlp, l1_w_up, l1_w_down, final_norm, loss_target, m_l0_norm_mix, m_l0_w_in, m_ssd_conv_w, m_ssd_conv_b, m_ssd_dt_bias, m_ssd_A_log, m_ssd_D, m_ssd_norm_w, m_l0_w_out, m_l0_norm_mlp, m_l0_w_up, m_l0_w_down, m_l1_norm_mix, m_l1_w_in, m_gdn_conv_w, m_gdn_A_log, m_gdn_dt_bias, m_gdn_norm_w, m_s5_A_re, m_s5_A_im, m_s5_log_step, m_s5_B_re, m_s5_B_im, m_s5_C_re, m_s5_C_im, m_s5_D, m_s5_w_glu, m_s5_b_glu, m_l1_w_out, m_l1_norm_mlp, m_l1_w_up, m_l1_w_down, m_final_norm, v_l0_norm_mix, v_l0_w_in, v_ssd_conv_w, v_ssd_conv_b, v_ssd_dt_bias, v_ssd_A_log, v_ssd_D, v_ssd_norm_w, v_l0_w_out, v_l0_norm_mlp, v_l0_w_up, v_l0_w_down, v_l1_norm_mix, v_l1_w_in, v_gdn_conv_w, v_gdn_A_log, v_gdn_dt_bias, v_gdn_norm_w, v_s5_A_re, v_s5_A_im, v_s5_log_step, v_s5_B_re, v_s5_B_im, v_s5_C_re, v_s5_C_im, v_s5_D, v_s5_w_glu, v_s5_b_glu, v_l1_w_out, v_l1_norm_mlp, v_l1_w_up, v_l1_w_down, v_final_norm):
    given = dict(locals())
    names = [n for n, _ in PARAMS]
    kinds = dict(PARAMS)
    w = {n: given[n] for n in names}
    seq = x.shape[1]
    x0 = x.reshape(seq, D_MODEL)
    target = loss_target.reshape(seq, D_MODEL)

    gb = _all_gather_shards("gather_weights", _pack([w[n] for n in GATHER_BF16], _MXU_DTYPE))
    full = dict(zip(GATHER_BF16, _unpack_gathered(gb, [w[n].shape for n in GATHER_BF16],
                                                  [kinds[n] for n in GATHER_BF16])))
    gf = _all_gather_shards("gather_conv", _pack([w[n] for n in GATHER_F32], F32))
    full.update(zip(GATHER_F32, _unpack_gathered(gf, [w[n].shape for n in GATHER_F32],
                                                 [kinds[n] for n in GATHER_F32])))
    in0 = full["l0_w_in"].shape[1]
    w_in0 = jnp.pad(full["l0_w_in"], ((0, 0), (0, IN0_PAD - in0)))
    wi1 = full["l1_w_in"]
    in1 = wi1.shape[1]
    w_in1 = jnp.concatenate([wi1[:, :3072], wi1[:, 3084:in1], wi1[:, 3072:3084],
                             jnp.zeros((D_MODEL, IN1_PAD - in1), wi1.dtype)], axis=1)

    row = lambda a: a.reshape(1, -1)
    lanes64 = lambda a: jnp.repeat(a, SSD_HEAD_DIM).reshape(1, -1)

    h0 = _rmsnorm_fwd("norm_mix0", x0, row(w["l0_norm_mix"]))
    proj0 = _matmul("in_proj0", h0, w_in0, "nn")
    cos_t, sin_t = _rotary_tables(seq)
    ret_tabs = list(_retention_tables())
    ret_xs = [(proj0, 512, 0), (proj0, 512, 1), (proj0, 512, 2), (proj0, 512, 3)]
    ret_xt = [(cos_t, 128, 0), (sin_t, 128, 0)]
    ret_states = [(512, 128)]
    mixed0, ret_saved = _scan_fwd("ret_fwd", _f_ret, CHUNK, ret_tabs, [], ret_xs, ret_xt, ret_states, D_MODEL, 512, 0)
    expand = jnp.repeat(jnp.eye(128, SSD_HEADS, dtype=F32), SSD_HEAD_DIM, axis=1)
    ssd_consts = [full["ssd_conv_w"], row(w["ssd_conv_b"]), lanes64(w["ssd_dt_bias"]), lanes64(w["ssd_A_log"]),
                  lanes64(w["ssd_D"]), row(w["ssd_norm_w"])]
    ssd_xs = [(proj0, 512, 4), (proj0, 512, 5), (proj0, 256, 12), (proj0, 256, 13), (proj0, 128, 28)]
    ssd_states = [(8, 512), (8, 256), (8, 256), (512, 128)]
    mixed0, ssd_saved = _scan_fwd("ssd_fwd", _f_ssd, CHUNK, [expand], ssd_consts, ssd_xs, [], ssd_states,
                                  D_MODEL, 512, 1, y_alias=mixed0)
    x1 = _matmul("out_proj0", mixed0, full["l0_w_out"], "nn", epi="add", epi_arr=x0)
    h1 = _rmsnorm_fwd("norm_mlp0", x1, row(w["l0_norm_mlp"]))
    u0 = _matmul("up0", h1, full["l0_w_up"], "nn")
    x2 = _matmul("down0", u0, full["l0_w_down"], "nn", a_pro="relu2", epi="add", epi_arr=x1)

    h2 = _rmsnorm_fwd("norm_mix1", x2, row(w["l1_norm_mix"]))
    proj1 = _matmul("in_proj1", h2, w_in1, "nn")
    p_alog = jnp.zeros((1, 128), F32).at[0, 6:12].set(w["gdn_A_log"])
    p_dtb = jnp.zeros((1, 128), F32).at[0, 6:12].set(w["gdn_dt_bias"])
    gdn_consts = [full["gdn_conv_w"], p_alog, p_dtb, row(w["gdn_norm_w"])]
    gdn_xs = [(proj1, 768, 0), (proj1, 768, 1), (proj1, 768, 2), (proj1, 768, 3), (proj1, 128, 26)]
    gdn_states = [(8, 768), (8, 768), (8, 768), (768, 256)]
    mixed1, gdn_saved = _scan_fwd("gdn_fwd", _f_gdn, CHUNK, [], gdn_consts, gdn_xs, [], gdn_states, D_MODEL, 768, 0)
    s5_args = (w["s5_A_re"], w["s5_A_im"], w["s5_log_step"], w["s5_B_re"], w["s5_B_im"], w["s5_C_re"], w["s5_C_im"])
    (lam_re, lam_im, bblk, cc_re, cc_im), s5_prep_vjp = jax.vjp(_s5_prep, *s5_args)
    s5_consts = [lam_re, lam_im, bblk, cc_re, cc_im, row(w["s5_D"]), full["s5_w_glu"].astype(F32), row(w["s5_b_glu"])]
    s5_xs = [(proj1, 256, 12)]
    s5_states = [(8, 1024), (8, 1024)]
    mixed1, s5_saved = _scan_fwd("s5_fwd", _f_s5, CHUNK, [], s5_consts, s5_xs, [], s5_states, D_MODEL, 256, 3,
                                 y_alias=mixed1)
    x3 = _matmul("out_proj1", mixed1, full["l1_w_out"], "nn", epi="add", epi_arr=x2)
    h3 = _rmsnorm_fwd("norm_mlp1", x3, row(w["l1_norm_mlp"]))
    u1 = _matmul("up1", h3, full["l1_w_up"], "nn")
    x4 = _matmul("down1", u1, full["l1_w_down"], "nn", a_pro="relu2", epi="add", epi_arr=x3)

    loss_part, dx4, d_final = _loss_head("loss_head", x4, row(w["final_norm"]), target)
    loss = lax.psum(loss_part[0, 0], ("x", "y", "c"))
    grads = {"final_norm": d_final.reshape(-1)}

    du1 = _matmul("down1_dx", dx4, full["l1_w_down"], "nt", out_dtype=_MXU_DTYPE, epi="drelu2", epi_arr=u1)
    grads["l1_w_down"] = _matmul("down1_dw", u1, dx4, "tn", a_pro="relu2")
    grads["l1_w_up"] = _matmul("up1_dw", h3, du1, "tn")
    dh3 = _matmul("up1_dx", du1, full["l1_w_up"], "nt")
    dx3, dwn = _rmsnorm_bwd("norm_mlp1_bwd", dh3, x3, row(w["l1_norm_mlp"]), dx4)
    grads["l1_norm_mlp"] = dwn.reshape(-1)
    grads["l1_w_out"] = _matmul("out_proj1_dw", mixed1, dx3, "tn")
    dmixed1 = _matmul("out_proj1_dx", dx3, full["l1_w_out"], "nt")

    def gdn_assemble(dx):
        dq, dk, dv, dz, dba = dx
        zeros = lambda n: jnp.zeros((dq.shape[0], n), F32)
        return jnp.concatenate([dq, dk, dv, dz, zeros(256), dba, zeros(IN1_PAD - 3456)], axis=1)

    dproj1, gdn_dc = _scan_bwd("gdn_bwd", _f_gdn, CHUNK, [], gdn_consts, gdn_xs, [], gdn_saved, gdn_states,
                               (dmixed1, 768, 0), IN1_PAD, IN1_PAD, 0, gdn_assemble)
    dproj1, s5_dc = _scan_bwd("s5_bwd", _f_s5, CHUNK, [], s5_consts, s5_xs, [], s5_saved, s5_states,
                              (dmixed1, 256, 3), IN1_PAD, 256, 12, lambda dx: dx[0], dx_alias=dproj1)
    grads["gdn_conv_w"] = gdn_dc[0]
    grads["gdn_A_log"] = gdn_dc[1][0, 6:12]
    grads["gdn_dt_bias"] = gdn_dc[2][0, 6:12]
    grads["gdn_norm_w"] = gdn_dc[3].reshape(-1)
    s5_pg = s5_prep_vjp(tuple(s5_dc[:5]))
    for n, gval in zip(("s5_A_re", "s5_A_im", "s5_log_step", "s5_B_re", "s5_B_im", "s5_C_re", "s5_C_im"), s5_pg):
        grads[n] = gval
    grads["s5_D"] = s5_dc[5].reshape(-1)
    grads["s5_w_glu"] = s5_dc[6]
    grads["s5_b_glu"] = s5_dc[7].reshape(-1)
    dwi1 = _matmul("in_proj1_dw", h2, dproj1, "tn")
    grads["l1_w_in"] = jnp.concatenate([dwi1[:, :3072], dwi1[:, 3328:3340], dwi1[:, 3072:3328]], axis=1)
    dh2 = _matmul("in_proj1_dx", dproj1, w_in1, "nt")
    dx2, dwn = _rmsnorm_bwd("norm_mix1_bwd", dh2, x2, row(w["l1_norm_mix"]), dx3)
    grads["l1_norm_mix"] = dwn.reshape(-1)

    du0 = _matmul("down0_dx", dx2, full["l0_w_down"], "nt", out_dtype=_MXU_DTYPE, epi="drelu2", epi_arr=u0)
    grads["l0_w_down"] = _matmul("down0_dw", u0, dx2, "tn", a_pro="relu2")
    grads["l0_w_up"] = _matmul("up0_dw", h1, du0, "tn")
    dh1 = _matmul("up0_dx", du0, full["l0_w_up"], "nt")
    dx1, dwn = _rmsnorm_bwd("norm_mlp0_bwd", dh1, x1, row(w["l0_norm_mlp"]), dx2)
    grads["l0_norm_mlp"] = dwn.reshape(-1)
    grads["l0_w_out"] = _matmul("out_proj0_dw", mixed0, dx1, "tn")
    dmixed0 = _matmul("out_proj0_dx", dx1, full["l0_w_out"], "nt")
    dproj0, _ = _scan_bwd("ret_bwd", _f_ret, CHUNK, ret_tabs, [], ret_xs, ret_xt, ret_saved, ret_states,
                          (dmixed0, 512, 0), IN0_PAD, 2048, 0, lambda dx: jnp.concatenate(dx, axis=1))

    def ssd_assemble(dx):
        return jnp.concatenate(list(dx) + [jnp.zeros((dx[0].shape[0], 2048 - 1664), F32)], axis=1)

    dproj0, ssd_dc = _scan_bwd("ssd_bwd", _f_ssd, CHUNK, [expand], ssd_consts, ssd_xs, [], ssd_saved, ssd_states,
                               (dmixed0, 512, 1), IN0_PAD, 2048, 1, ssd_assemble, dx_alias=dproj0)
    heads = lambda a: a.reshape(SSD_HEADS, SSD_HEAD_DIM).sum(axis=1)
    grads["ssd_conv_w"] = ssd_dc[0]
    grads["ssd_conv_b"] = ssd_dc[1].reshape(-1)
    grads["ssd_dt_bias"] = heads(ssd_dc[2])
    grads["ssd_A_log"] = heads(ssd_dc[3])
    grads["ssd_D"] = heads(ssd_dc[4])
    grads["ssd_norm_w"] = ssd_dc[5].reshape(-1)
    grads["l0_w_in"] = _matmul("in_proj0_dw", h0, dproj0, "tn")[:, :in0]
    dh0 = _matmul("in_proj0_dx", dproj0, w_in0, "nt")
    dx0, dwn = _rmsnorm_bwd("norm_mix0_bwd", dh0, x0, row(w["l0_norm_mix"]), dx1)
    grads["l0_norm_mix"] = dwn.reshape(-1)
    grad_x = dx0.reshape(x.shape)

    c_idx = lax.axis_index("c").astype(jnp.int32).reshape(1)
    small = SMALL_SHARDED + tuple(n for n in names if kinds[n] == "rep")
    order = LARGE + small
    block = lambda n, s: _shard_block(grads[n].reshape(_full_shape(n, w, kinds)), kinds[n], s, w[n].shape)
    rep_rows = _flat_rows([grads[n] for n in order[len(LARGE) + len(SMALL_SHARDED):]], F32)
    gslab = jnp.stack([_align_rows(jnp.concatenate(
        [_flat_rows([block(n, s) for n in LARGE + SMALL_SHARDED], F32), rep_rows])) for s in range(4)])
    from_sibling = _swap_halves("grads_swap_halves", gslab)
    chip_sum = _add_halves("grads_add_sibling", gslab, from_sibling, c_idx)
    place = jnp.stack([lax.axis_index("x"), lax.axis_index("y")]).astype(jnp.int32)
    from_x, from_y = _exchange_stage1("grads_stage1", chip_sum)
    send_a, keep_a, send_b, keep_b = _exchange_add1("grads_add1", chip_sum, from_x, from_y, place)
    got_a, got_b = _exchange_stage2("grads_stage2", send_a, send_b)
    my_half = _exchange_add2("grads_add2", keep_a, got_a, keep_b, got_b, c_idx)
    gsum = _join_halves("grads_join_halves", my_half)
    grad = dict(zip(order, _unpack(gsum, [w[n].shape for n in order])))

    delta, new_m, new_v = {}, {}, {}
    for n in LARGE:
        delta[n], new_m[n], new_v[n] = _adamw("adamw_" + n, w[n], grad[n], given["m_" + n], given["v_" + n])
    first_small = sum(_rows_of(math.prod(w[n].shape)) for n in LARGE)
    small_shapes = [w[n].shape for n in small]

    def small_slab(arrays):
        rows = _flat_rows(arrays, F32)
        return jnp.pad(rows, ((0, gsum.shape[0] - first_small - rows.shape[0]), (0, 0)))

    res = _adamw("adamw_small", small_slab([w[n] for n in small]), gsum[first_small:],
                 small_slab([given["m_" + n] for n in small]), small_slab([given["v_" + n] for n in small]))
    for out, slab in zip((delta, new_m, new_v), res):
        out.update(zip(small, _unpack(slab, small_shapes)))
    return (loss, grad_x, *[grad[n] for n in names], *[delta[n] for n in names], *[new_m[n] for n in names],
            *[new_v[n] for n in names])


def _full_shape(name, w, kinds):
    shp = w[name].shape
    if kinds[name] == "col":
        return (shp[0], 4 * shp[1])
    if kinds[name] == "row":
        return (4 * shp[0],) + tuple(shp[1:])
    return shp
```

```python
import functools
import math

import jax
import jax.numpy as jnp
from jax import lax
from jax.experimental import pallas as pl
from jax.experimental.pallas import tpu as pltpu

F32 = jnp.float32
_MXU_DTYPE = jnp.bfloat16

D_MODEL = 1024
CHUNK = 64
EPS = 1e-6
RET_HEADS, RET_DK = 4, 128
ROPE_THETA = 10000.0
SSD_HEADS, SSD_HEAD_DIM = 8, 64
GDN_HEADS, GDN_DK = 6, 128
S5_GROUPS, S5_GROUP, S5_STATE = 16, 16, 64
ADAM_LR, ADAM_B1, ADAM_B2, ADAM_EPS, ADAM_WD, ADAM_STEP = 0.001, 0.9, 0.999, 1e-08, 0.01, 10

IN0_PAD = 4096
IN1_PAD = 3584
LANES = 1024
VMEM_LIMIT = 56 * 1024 * 1024
MESH = pl.DeviceIdType.MESH

PARAMS = (
    ("l0_norm_mix", "rep"), ("l0_w_in", "col"), ("ssd_conv_w", "col"), ("ssd_conv_b", "rep"),
    ("ssd_dt_bias", "rep"), ("ssd_A_log", "rep"), ("ssd_D", "rep"), ("ssd_norm_w", "rep"),
    ("l0_w_out", "row"), ("l0_norm_mlp", "rep"), ("l0_w_up", "col"), ("l0_w_down", "row"),
    ("l1_norm_mix", "rep"), ("l1_w_in", "col"), ("gdn_conv_w", "col"), ("gdn_A_log", "rep"),
    ("gdn_dt_bias", "rep"), ("gdn_norm_w", "rep"), ("s5_A_re", "rep"), ("s5_A_im", "rep"),
    ("s5_log_step", "rep"), ("s5_B_re", "rep"), ("s5_B_im", "rep"), ("s5_C_re", "rep"), ("s5_C_im", "rep"),
    ("s5_D", "rep"), ("s5_w_glu", "row"), ("s5_b_glu", "rep"), ("l1_w_out", "row"), ("l1_norm_mlp", "rep"),
    ("l1_w_up", "col"), ("l1_w_down", "row"), ("final_norm", "rep"),
)
GATHER_BF16 = ("l0_w_in", "l0_w_out", "l0_w_up", "l0_w_down", "l1_w_in", "l1_w_out", "l1_w_up", "l1_w_down", "s5_w_glu")
GATHER_F32 = ("ssd_conv_w", "gdn_conv_w")
LARGE = GATHER_BF16[:8]
SMALL_SHARDED = ("s5_w_glu", "ssd_conv_w", "gdn_conv_w")


def _dg(a, b, ca, cb, prec=None):
    return lax.dot_general(a, b, (((ca,), (cb,)), ((), ())), preferred_element_type=F32, precision=prec)


def _lo(a):
    return a.astype(_MXU_DTYPE)


@jax.custom_vjp
def _mm(a, b):
    return _dg(_lo(a), _lo(b), 1, 0)


def _mm_fwd(a, b):
    return _mm(a, b), (a, b)


def _mm_bwd(res, g):
    a, b = res
    return _dg(_lo(g), _lo(b), 1, 1), _dg(_lo(a), _lo(g), 0, 0)


_mm.defvjp(_mm_fwd, _mm_bwd)


@jax.custom_vjp
def _mm_nt(a, b):
    return _dg(_lo(a), _lo(b), 1, 1)


def _mm_nt_fwd(a, b):
    return _mm_nt(a, b), (a, b)


def _mm_nt_bwd(res, g):
    a, b = res
    return _dg(_lo(g), _lo(b), 1, 0), _dg(_lo(g), _lo(a), 0, 0)


_mm_nt.defvjp(_mm_nt_fwd, _mm_nt_bwd)


@jax.custom_vjp
def _mm_tn(a, b):
    return _dg(_lo(a), _lo(b), 0, 0)


def _mm_tn_fwd(a, b):
    return _mm_tn(a, b), (a, b)


def _mm_tn_bwd(res, g):
    a, b = res
    return _dg(_lo(b), _lo(g), 1, 1), _dg(_lo(a), _lo(g), 1, 0)


_mm_tn.defvjp(_mm_tn_fwd, _mm_tn_bwd)


def _split2(x):
    hi = _lo(x)
    return hi, _lo(x - hi.astype(F32))


def _split3(x):
    h1 = _lo(x)
    r1 = x - h1.astype(F32)
    h2 = _lo(r1)
    return h1, h2, _lo(r1 - h2.astype(F32))


def _tri_cum_dir(m, ca):
    n, w = m.shape
    causal, _ = _tri_masks(n)
    out = _dg(causal.astype(_MXU_DTYPE), jnp.concatenate(_split3(m), axis=1), ca, 0)
    return out[:, :w] + out[:, w:2 * w] + out[:, 2 * w:]


@jax.custom_vjp
def _tri_cum(m):
    return _tri_cum_dir(m, 1)


def _tri_cum_fwd(m):
    return _tri_cum_dir(m, 1), None


def _tri_cum_bwd(_, g):
    return (_tri_cum_dir(g, 0),)


_tri_cum.defvjp(_tri_cum_fwd, _tri_cum_bwd)


@jax.custom_vjp
def _mm_exact_rhs(a, e):
    return _dg(jnp.concatenate(_split3(a), axis=1), jnp.concatenate([_lo(e)] * 3, axis=0), 1, 0)


def _mm_exact_rhs_fwd(a, e):
    return _mm_exact_rhs(a, e), e


def _mm_exact_rhs_bwd(e, g):
    return _dg(jnp.concatenate(_split3(g), axis=1), jnp.concatenate([_lo(e)] * 3, axis=1), 1, 1), jnp.zeros_like(e)


_mm_exact_rhs.defvjp(_mm_exact_rhs_fwd, _mm_exact_rhs_bwd)


def _bd(x):
    left = _iota(x.shape, 1) < (x.shape[1] // 2)
    zero = jnp.zeros_like(x)
    return jnp.concatenate([jnp.where(left, x, zero), jnp.where(left, zero, x)], axis=0)


def _unbd(m):
    half = m.shape[0] // 2
    left = _iota((half, m.shape[1]), 1) < (m.shape[1] // 2)
    return jnp.where(left, m[:half], m[half:])


def _pmm_nn(x, y):
    xh, xl = _split2(x)
    yh, yl = _split2(y)
    return _dg(jnp.concatenate([xh, xl, xh], axis=1), jnp.concatenate([_bd(yh), _bd(yh), _bd(yl)], axis=0), 1, 0)


def _pmm_nt(x, y):
    xh, xl = _split2(x)
    yh, yl = _split2(y)
    return _dg(jnp.concatenate([xh, xl, xh], axis=1), jnp.concatenate([_bd(yh), _bd(yh), _bd(yl)], axis=1), 1, 1)


def _pmm_tn(x, y):
    xh, xl = _split2(x)
    yh, yl = _split2(y)
    return _unbd(_dg(jnp.concatenate([xh, xl, xh], axis=0), jnp.concatenate([yh, yh, yl], axis=0), 0, 0))


@functools.lru_cache(maxsize=None)
def _shift(s, axis):
    @jax.custom_vjp
    def sh(x):
        return pltpu.roll(x, s, axis)

    def fwd(x):
        return sh(x), None

    def bwd(_, g):
        n = g.shape[axis]
        return (pltpu.roll(g, (n - s) % n, axis),)

    sh.defvjp(fwd, bwd)
    return sh


def _iota(shape, axis):
    return lax.broadcasted_iota(jnp.int32, shape, axis)


def _silu(x):
    return x * jax.nn.sigmoid(x)


def _unit_rms(x):
    return x * lax.rsqrt(jnp.mean(x * x, axis=-1, keepdims=True) + EPS)


def _l2norm(x):
    return x * lax.rsqrt(jnp.sum(x * x, axis=-1, keepdims=True) + EPS)


def _tri_masks(n):
    r, c = _iota((n, n), 0), _iota((n, n), 1)
    return r >= c, r > c


def _packed_rc():
    return _iota((CHUNK, 2 * CHUNK), 0), _iota((CHUNK, 2 * CHUNK), 1) & (CHUNK - 1)


def _decay_packed(g_packed):
    r, c = _packed_rc()
    seg = _tri_cum(g_packed * (r > c).astype(F32))
    return jnp.where(r >= c, jnp.exp(jnp.where(r >= c, seg, 0.0)), 0.0)


def _conv(x, tail, w):
    rows, width = x.shape
    row = _iota((rows, width), 0)
    acc = x * w[3:4, :]
    pad = jnp.zeros((rows - 8, width), F32)
    for j in range(3):
        s = 3 - j
        prev = jnp.concatenate([_shift(s, 0)(tail), pad], axis=0)
        acc = acc + w[j:j + 1, :] * jnp.where(row < s, prev, _shift(s, 0)(x))
    return acc


def _tri_inv_impl(mats):
    r, c = _packed_rc()
    eye = (r == c).astype(F32)

    def same_block(b):
        return (r // b) == (c // b)

    a8 = [jnp.where(same_block(8), a, 0.0) for a in mats]
    a2 = [_pmm_nn(t, t) for t in a8]
    a4 = [_pmm_nn(t, t) for t in a2]
    x = [_pmm_nn(eye - p, eye + q) for p, q in zip(a8, a2)]
    x = [_pmm_nn(p, eye + q) for p, q in zip(x, a4)]
    for b in (8, 16, 32):
        off = [jnp.where(same_block(2 * b) & jnp.logical_not(same_block(b)), a, 0.0) for a in mats]
        y = [_pmm_nn(p, q) for p, q in zip(x, off)]
        x = [p - _pmm_nn(q, p) for p, q in zip(x, y)]
    return x


@jax.custom_vjp
def _tri_inv(mats):
    return _tri_inv_impl(mats)


def _tri_inv_fwd(mats):
    t = _tri_inv_impl(mats)
    return t, t


def _tri_inv_bwd(t, g):
    m1 = [_pmm_tn(p, q) for p, q in zip(t, g)]
    return ([-_pmm_nt(p, q) for p, q in zip(m1, t)],)


_tri_inv.defvjp(_tri_inv_fwd, _tri_inv_bwd)


def _f_ret(tabs, consts, xs, xtabs, states):
    dmask, kdec, qdec, cdec = tabs
    q, k, v, gate = xs
    cs, sn = xtabs
    (st,) = states
    swap = _shift(RET_DK // 2, 1)
    heads = range(RET_HEADS)
    sls = [slice(128 * h, 128 * h + 128) for h in heads]
    qh = [(q[:, sl] * cs + swap(q[:, sl]) * sn) * (RET_DK ** -0.5) for sl in sls]
    kh = [k[:, sl] * cs + swap(k[:, sl]) * sn for sl in sls]
    sh = [st[sl, :] for sl in sls]
    scores = [_mm_nt(a, b) * dmask[64 * h:64 * h + 64, :] for h, a, b in zip(heads, qh, kh)]
    y = [_mm(s, v[:, sl]) for s, sl in zip(scores, sls)]
    y = [t + _mm(a * qdec[:, sl], s) for t, a, sl, s in zip(y, qh, sls, sh)]
    new = [s * cdec[:, sl] + _mm_tn(b * kdec[:, sl], v[:, sl]) for s, sl, b in zip(sh, sls, kh)]
    outs = [_silu(gate[:, sl]) * _unit_rms(t) for sl, t in zip(sls, y)]
    return (jnp.concatenate(outs, axis=1),), [jnp.concatenate(new, axis=0)]


def _f_ssd(tabs, consts, xs, xtabs, states):
    (expand,) = tabs
    conv_w, conv_b, dtb, alog, dskip, nw = consts
    z, xr, br, cr, dtr = xs
    tx, tb, tc, st = states
    xc = _silu(_conv(xr, tx, conv_w[:, 0:512]) + conv_b[:, 0:512])
    bc = _silu(_conv(br, tb, conv_w[:, 512:768]) + conv_b[:, 512:768])
    cc = _silu(_conv(cr, tc, conv_w[:, 768:1024]) + conv_b[:, 768:1024])
    dt = jax.nn.softplus(_mm_exact_rhs(dtr, expand) + dtb)
    la = dt * (-jnp.exp(alog))
    lacum = _tri_cum(la)
    total = jnp.sum(la, axis=0, keepdims=True)
    xd = xc * dt
    dte, ecum, cdec = jnp.exp(total - lacum), jnp.exp(lacum), jnp.exp(total)
    pairs = range(SSD_HEADS // 2)
    sls = [slice(128 * p, 128 * p + 128) for p in pairs]
    bg = [bc[:, 128 * g:128 * g + 128] for g in range(2)]
    cg = [cc[:, 128 * g:128 * g + 128] for g in range(2)]
    cb2 = [_mm_nt(c, jnp.concatenate([b, b], axis=0)) for b, c in zip(bg, cg)]
    lm = [_decay_packed(la[:, sl]) for sl in sls]
    sp = [st[sl, :] for sl in sls]
    ys = [_mm(cg[p // 2], sp[p]) * ecum[:, sls[p]] for p in pairs]
    ys = [ys[p] + _mm(cb2[p // 2] * lm[p], _bd(xd[:, sls[p]])) for p in pairs]
    new = [sp[p] * cdec[:, sls[p]] + _mm_tn(bg[p // 2], xd[:, sls[p]] * dte[:, sls[p]]) for p in pairs]
    y = jnp.concatenate(ys, axis=1) + dskip * xc
    yg = y * _silu(z)
    out = jnp.concatenate([_unit_rms(yg[:, 0:256]), _unit_rms(yg[:, 256:512])], axis=1) * nw
    return (out,), [xr[CHUNK - 8:, :], br[CHUNK - 8:, :], cr[CHUNK - 8:, :], jnp.concatenate(new, axis=0)]


def _f_gdn(tabs, consts, xs, xtabs, states):
    conv_w, p_alog, p_dtb, nw = consts
    qr, kr, vr, z, ba = xs
    tq, tk, tv, st = states
    qc = _silu(_conv(qr, tq, conv_w[:, 0:768]))
    kc = _silu(_conv(kr, tk, conv_w[:, 768:1536]))
    vc = _silu(_conv(vr, tv, conv_w[:, 1536:2304]))
    gl = -jnp.exp(p_alog) * jax.nn.softplus(ba + p_dtb)
    bl = jax.nn.sigmoid(ba)
    gcum = _tri_cum(gl)
    left128 = _iota((CHUNK, 128), 1) < 64
    left256 = _iota((CHUNK, 256), 1) < 128
    r, c = _packed_rc()
    diag_blocks = (_iota((256, 256), 0) < 128) == (_iota((256, 256), 1) < 128)

    def norm2(t):
        return jnp.concatenate([_l2norm(t[:, 0:128]), _l2norm(t[:, 128:256])], axis=1)

    def pick(arr, off, left, p):
        return jnp.where(left, arr[:, off + 2 * p:off + 2 * p + 1], arr[:, off + 2 * p + 1:off + 2 * p + 2])

    pairs = range(GDN_HEADS // 2)
    sls = [slice(256 * p, 256 * p + 256) for p in pairs]
    qn = [norm2(qc[:, sl]) * (GDN_DK ** -0.5) for sl in sls]
    kn = [norm2(kc[:, sl]) for sl in sls]
    dec = [_decay_packed(pick(gl, 6, left128, p)) for p in pairs]
    g2 = [pick(gl, 6, left256, p) for p in pairs]
    gc2 = [pick(gcum, 6, left256, p) for p in pairs]
    b2 = [pick(bl, 0, left256, p) for p in pairs]
    tot = [jnp.sum(t, axis=0, keepdims=True) for t in g2]
    eg = [jnp.exp(t) for t in gc2]
    et = [jnp.exp(t - s) for t, s in zip(tot, gc2)]
    cd = [jnp.exp(t) for t in tot]
    kb = [k * b for k, b in zip(kn, b2)]
    vb = [vc[:, sl] * b for sl, b in zip(sls, b2)]
    kbd = [_bd(k) for k in kn]
    tm = _tri_inv([jnp.where(r > c, _mm_nt(a, b) * d, 0.0) for a, b, d in zip(kb, kbd, dec)])
    u = [_mm(t, _bd(v)) for t, v in zip(tm, vb)]
    w = [_mm(t, _bd(k * e)) for t, k, e in zip(tm, kb, eg)]
    attn = [_mm_nt(q, k) * d for q, k, d in zip(qn, kbd, dec)]
    sp = [st[sl, :] for sl in sls]
    vn = [a - _mm(b, s) for a, b, s in zip(u, w, sp)]
    o = [_mm(q * e, s) + _mm(a, _bd(v)) for q, e, s, a, v in zip(qn, eg, sp, attn, vn)]
    new = [s * d + jnp.where(diag_blocks, _mm_tn(k * e, v), 0.0) for s, d, k, e, v in zip(sp, cd, kn, et, vn)]
    outs = []
    for p in pairs:
        for hh in range(2):
            osl = slice(128 * hh, 128 * hh + 128)
            zsl = slice(256 * p + 128 * hh, 256 * p + 128 * hh + 128)
            outs.append(_unit_rms(o[p][:, osl]) * nw * _silu(z[:, zsl]))
    return (jnp.concatenate(outs, axis=1),), [qr[CHUNK - 8:, :], kr[CHUNK - 8:, :], vr[CHUNK - 8:, :],
                                             jnp.concatenate(new, axis=0)]


def _f_s5(tabs, consts, xs, xtabs, states):
    lam_re, lam_im, bblk, c_re, c_im, dskip, wglu, bglu = consts
    (u,) = xs
    s_re, s_im = states
    rows = u.shape[0]
    n = lam_re.shape[1]
    bu = _mm(u, bblk)
    hr, hi = bu[:, 0:n], bu[:, n:2 * n]
    row = _iota((rows, n), 0)
    h0r, h0i = s_re[0:1, :], s_im[0:1, :]
    hr = hr + jnp.where(row == 0, lam_re * h0r - lam_im * h0i, 0.0)
    hi = hi + jnp.where(row == 0, lam_re * h0i + lam_im * h0r, 0.0)
    pr, pi = lam_re, lam_im
    d = 1
    while d < rows:
        sr = jnp.where(row >= d, _shift(d, 0)(hr), 0.0)
        si = jnp.where(row >= d, _shift(d, 0)(hi), 0.0)
        hr, hi = hr + pr * sr - pi * si, hi + pr * si + pi * sr
        pr, pi = pr * pr - pi * pi, 2.0 * pr * pi
        d *= 2
    y = _mm(hr, c_re) - _mm(hi, c_im) + dskip * u
    y = jax.nn.gelu(y)
    out = y * jax.nn.sigmoid(_mm(y, wglu) + bglu)
    last_r = jnp.broadcast_to(hr[rows - 1:rows, :], (8, n))
    last_i = jnp.broadcast_to(hi[rows - 1:rows, :], (8, n))
    return (out,), [last_r, last_i]


def _full_spec(a):
    nd = a.ndim
    return pl.BlockSpec(a.shape, lambda i, _nd=nd: (0,) * _nd)


CHUNKS_PER_STEP = 4


def _chunks_per_step(f, rows, n):
    def g(tabs, consts, xs, xtabs, states):
        ys = []
        for i in range(n):
            sl = slice(rows * i, rows * (i + 1))
            (y,), states = f(tabs, consts, [t[sl] for t in xs], [t[sl] for t in xtabs], states)
            ys.append(y)
        return (jnp.concatenate(ys, axis=0),), states

    return g


def _scan_fwd(name, f, rows, tabs, consts, xs, xtabs, state_shapes, y_total, y_width, y_cb, y_alias=None):
    seq = xs[0][0].shape[0]
    per_step = math.gcd(CHUNKS_PER_STEP, seq // rows)
    f = _chunks_per_step(f, rows, per_step)
    rows = rows * per_step
    nc = seq // rows
    nt, ncst, nx, nxt, ns = len(tabs), len(consts), len(xs), len(xtabs), len(state_shapes)
    alias = y_alias is not None

    def body(*refs):
        p = 0
        tab_r = refs[p:p + nt]; p += nt
        c_r = refs[p:p + ncst]; p += ncst
        x_r = refs[p:p + nx]; p += nx
        xt_r = refs[p:p + nxt]; p += nxt
        if alias:
            p += 1
        y_ref = refs[p]; p += 1
        sv_r = refs[p:p + ns]; p += ns
        st_r = refs[p:p + ns]

        @pl.when(pl.program_id(0) == 0)
        def _():
            for s in st_r:
                s[...] = jnp.zeros(s.shape, F32)

        st = [s[...] for s in st_r]
        for r, v in zip(sv_r, st):
            r[...] = v
        (y,), new = f([r[...] for r in tab_r], [r[...] for r in c_r], [r[...] for r in x_r],
                      [r[...] for r in xt_r], st)
        y_ref[...] = y
        for s, v in zip(st_r, new):
            s[...] = v

    win = [pl.BlockSpec((rows, w), lambda i, _cb=cb: (i, _cb)) for (_, w, cb) in list(xs) + list(xtabs)]
    in_specs = [_full_spec(a) for a in list(tabs) + list(consts)] + win
    args = list(tabs) + list(consts) + [a for (a, _, _) in list(xs) + list(xtabs)]
    io_alias = {}
    if alias:
        in_specs.append(pl.BlockSpec(memory_space=pl.ANY))
        io_alias = {len(args): 0}
        args.append(y_alias)
    out_shape = [jax.ShapeDtypeStruct((seq, y_total), F32)]
    out_specs = [pl.BlockSpec((rows, y_width), lambda i: (i, y_cb))]
    for (r, c) in state_shapes:
        out_shape.append(jax.ShapeDtypeStruct((nc * r, c), F32))
        out_specs.append(pl.BlockSpec((r, c), lambda i: (i, 0)))
    res = pl.pallas_call(
        body, name=name, grid=(nc,), in_specs=in_specs, out_specs=out_specs, out_shape=out_shape,
        scratch_shapes=[pltpu.VMEM(s, F32) for s in state_shapes], input_output_aliases=io_alias,
        compiler_params=pltpu.CompilerParams(dimension_semantics=("arbitrary",), vmem_limit_bytes=VMEM_LIMIT),
    )(*args)
    return res[0], list(res[1:])


def _scan_bwd(name, f, rows, tabs, consts, xs, xtabs, saved, state_shapes, dy, dx_total, dx_width, dx_cb,
              assemble, dx_alias=None):
    seq = xs[0][0].shape[0]
    per_step = math.gcd(CHUNKS_PER_STEP, seq // rows)
    f = _chunks_per_step(f, rows, per_step)
    rows = rows * per_step
    nc = seq // rows
    nt, ncst, nx, nxt, ns = len(tabs), len(consts), len(xs), len(xtabs), len(state_shapes)
    alias = dx_alias is not None

    def body(*refs):
        p = 0
        tab_r = refs[p:p + nt]; p += nt
        c_r = refs[p:p + ncst]; p += ncst
        x_r = refs[p:p + nx]; p += nx
        xt_r = refs[p:p + nxt]; p += nxt
        sv_r = refs[p:p + ns]; p += ns
        dy_ref = refs[p]; p += 1
        if alias:
            p += 1
        dx_ref = refs[p]; p += 1
        dc_r = refs[p:p + ncst]; p += ncst
        ds_r = refs[p:p + ns]

        @pl.when(pl.program_id(0) == 0)
        def _():
            for s in ds_r:
                s[...] = jnp.zeros(s.shape, F32)
            for r in dc_r:
                r[...] = jnp.zeros(r.shape, F32)

        tab_v = [r[...] for r in tab_r]
        xt_v = [r[...] for r in xt_r]

        def g(c, x, s):
            (y,), new = f(tab_v, c, x, xt_v, s)
            return y, new

        _, vjp = jax.vjp(g, [r[...] for r in c_r], [r[...] for r in x_r], [r[...] for r in sv_r])
        dc, dx, ds = vjp((dy_ref[...], [s[...] for s in ds_r]))
        dx_ref[...] = assemble(dx)
        for r, v in zip(dc_r, dc):
            r[...] += v
        for s, v in zip(ds_r, ds):
            s[...] = v

    win = [pl.BlockSpec((rows, w), lambda j, _cb=cb: (nc - 1 - j, _cb)) for (_, w, cb) in list(xs) + list(xtabs)]
    in_specs = [_full_spec(a) for a in list(tabs) + list(consts)] + win
    args = list(tabs) + list(consts) + [a for (a, _, _) in list(xs) + list(xtabs)]
    for (r, c), sv in zip(state_shapes, saved):
        in_specs.append(pl.BlockSpec((r, c), lambda j: (nc - 1 - j, 0)))
        args.append(sv)
    in_specs.append(pl.BlockSpec((rows, dy[1]), lambda j: (nc - 1 - j, dy[2])))
    args.append(dy[0])
    io_alias = {}
    if alias:
        in_specs.append(pl.BlockSpec(memory_space=pl.ANY))
        io_alias = {len(args): 0}
        args.append(dx_alias)
    out_shape = [jax.ShapeDtypeStruct((seq, dx_total), F32)] + [jax.ShapeDtypeStruct(a.shape, F32) for a in consts]
    out_specs = [pl.BlockSpec((rows, dx_width), lambda j: (nc - 1 - j, dx_cb))] + [_full_spec(a) for a in consts]
    res = pl.pallas_call(
        body, name=name, grid=(nc,), in_specs=in_specs, out_specs=out_specs, out_shape=out_shape,
        scratch_shapes=[pltpu.VMEM(s, F32) for s in state_shapes], input_output_aliases=io_alias,
        compiler_params=pltpu.CompilerParams(dimension_semantics=("arbitrary",), vmem_limit_bytes=VMEM_LIMIT),
    )(*args)
    return res[0], list(res[1:])


def _tile(n, want):
    t = min(n, want)
    while n % t:
        t //= 2
    return t


MATMUL_VMEM_BUDGET = 40 * 1024 * 1024


def _pick_tiles(m, n, k, sa, sb, so, se, whole_rows=False):
    best = None
    for tn in ({n} if whole_rows else {_tile(n, 1024), _tile(n, 512)}):
        for tm in {_tile(m, t) for t in (2048, 1024, 512)}:
            for tk in {_tile(k, t) for t in (4096, 2048, 1024, 512)}:
                at, bt, ot = tm * tk * sa, tk * tn * sb, tm * tn * so
                need = 2 * (at + bt + ot + tm * tn * se) + 2 * tm * tn * 4 + (at if sa == 4 else 0) + (bt if sb == 4 else 0)
                if need > MATMUL_VMEM_BUDGET:
                    continue
                key = ((m // tm) * (n // tn) * (k // tk), k // tk, -tm, -tn)
                if best is None or key < best[0]:
                    best = (key, (tm, tn, tk))
    assert best is not None, (m, n, k)
    return best[1]


def _matmul(name, a, b, mode, out_dtype=F32, a_pro=None, epi=None, epi_arr=None, norm_w=None, norm_x=None):
    if mode == "nn":
        (m, k), (k2, n) = a.shape, b.shape
    elif mode == "nt":
        (m, k), (n, k2) = a.shape, b.shape
    else:
        (k, m), (k2, n) = a.shape, b.shape
    assert k == k2, (name, a.shape, b.shape)
    size = lambda t: jnp.dtype(t).itemsize
    rows_in = [] if epi is None else [epi_arr] + ([norm_x] if epi == "norm_bwd" else [])
    emit_norm = epi == "add" and norm_w is not None
    extra = sum(size(t.dtype) for t in rows_in) + (size(_MXU_DTYPE) if emit_norm else 0)
    tm, tn, tk = _pick_tiles(m, n, k, size(a.dtype), size(b.dtype), size(out_dtype), extra,
                             whole_rows=norm_w is not None)
    nk = k // tk
    ca, cb = {"nn": (1, 0), "nt": (1, 1), "tn": (0, 0)}[mode]
    n_in = 2 + len(rows_in) + (norm_w is not None)
    n_out = 2 if (emit_norm or epi == "norm_bwd") else 1

    def body(*refs):
        refs = list(refs)
        acc = refs.pop() if nk > 1 else None
        a_ref, b_ref = refs[0], refs[1]
        e_ref = refs[2] if epi is not None else None
        x_ref = refs[3] if epi == "norm_bwd" else None
        w_ref = refs[n_in - 1] if norm_w is not None else None
        o_ref = refs[n_in]
        o2_ref = refs[n_in + 1] if n_out == 2 else None
        kk = pl.program_id(2)

        if epi == "norm_bwd":
            @pl.when((pl.program_id(1) == 0) & (kk == 0))
            def _():
                o2_ref[...] = jnp.zeros(o2_ref.shape, F32)

        av = a_ref[...]
        if a_pro == "relu2":
            r = jnp.maximum(av, 0.0)
            av = r * r
        part = _dg(_lo(av), _lo(b_ref[...]), ca, cb)

        def finish(r):
            if epi == "add":
                r = r + e_ref[...]
                if emit_norm:
                    o2_ref[...] = (_unit_rms(r) * w_ref[...]).astype(_MXU_DTYPE)
            elif epi == "drelu2":
                r = r * (2.0 * jnp.maximum(e_ref[...], 0.0))
            elif epi == "norm_bwd":
                xv = x_ref[...]
                rstd = lax.rsqrt(jnp.mean(xv * xv, axis=-1, keepdims=True) + EPS)
                xh = xv * rstd
                g = r * w_ref[...]
                o2_ref[...] += jnp.sum(r * xh, axis=0, keepdims=True)
                r = e_ref[...] + rstd * (g - xh * jnp.mean(g * xh, axis=-1, keepdims=True))
            o_ref[...] = r.astype(out_dtype)

        if nk == 1:
            finish(part)
        else:
            @pl.when(kk == 0)
            def _():
                acc[...] = part

            @pl.when(kk > 0)
            def _():
                acc[...] += part

            @pl.when(kk == nk - 1)
            def _():
                finish(acc[...])

    if mode == "tn":
        a_spec = pl.BlockSpec((tk, tm), lambda j, i, kk: (kk, i))
    else:
        a_spec = pl.BlockSpec((tm, tk), lambda j, i, kk: (i, kk))
    if mode == "nt":
        b_spec = pl.BlockSpec((tn, tk), lambda j, i, kk: (j, kk))
    else:
        b_spec = pl.BlockSpec((tk, tn), lambda j, i, kk: (kk, j))
    o_spec = pl.BlockSpec((tm, tn), lambda j, i, kk: (i, j))
    vec_spec = pl.BlockSpec((1, tn), lambda j, i, kk: (0, j))
    in_specs, args = [a_spec, b_spec] + [o_spec] * len(rows_in), [a, b] + rows_in
    if norm_w is not None:
        in_specs.append(vec_spec)
        args.append(norm_w)
    out_specs, out_shape = [o_spec], [jax.ShapeDtypeStruct((m, n), out_dtype)]
    if emit_norm:
        out_specs.append(o_spec)
        out_shape.append(jax.ShapeDtypeStruct((m, n), _MXU_DTYPE))
    elif epi == "norm_bwd":
        out_specs.append(vec_spec)
        out_shape.append(jax.ShapeDtypeStruct((1, n), F32))
    sem = ("parallel", "arbitrary" if epi == "norm_bwd" else "parallel", "arbitrary")
    res = pl.pallas_call(
        body, name=name, grid=(n // tn, m // tm, nk), in_specs=in_specs, out_specs=out_specs, out_shape=out_shape,
        scratch_shapes=[pltpu.VMEM((tm, tn), F32)] if nk > 1 else [],
        compiler_params=pltpu.CompilerParams(dimension_semantics=sem, vmem_limit_bytes=VMEM_LIMIT),
    )(*args)
    return res[0] if n_out == 1 else res


ROW_TILE = 512


def _rmsnorm_fwd(name, x, w):
    seq, d = x.shape
    tr = _tile(seq, ROW_TILE)

    def body(x_ref, w_ref, o_ref):
        xv = x_ref[...]
        o_ref[...] = (_unit_rms(xv) * w_ref[...]).astype(_MXU_DTYPE)

    return pl.pallas_call(
        body, name=name, grid=(seq // tr,),
        in_specs=[pl.BlockSpec((tr, d), lambda i: (i, 0)), pl.BlockSpec((1, d), lambda i: (0, 0))],
        out_specs=pl.BlockSpec((tr, d), lambda i: (i, 0)), out_shape=jax.ShapeDtypeStruct((seq, d), _MXU_DTYPE),
        compiler_params=pltpu.CompilerParams(dimension_semantics=("parallel",), vmem_limit_bytes=VMEM_LIMIT),
    )(x, w)


def _loss_head(name, x, w, target):
    seq, d = x.shape
    tr = _tile(seq, ROW_TILE)

    def body(x_ref, w_ref, t_ref, loss_ref, dx_ref, dw_ref):
        @pl.when(pl.program_id(0) == 0)
        def _():
            dw_ref[...] = jnp.zeros(dw_ref.shape, F32)
            loss_ref[...] = jnp.zeros(loss_ref.shape, F32)

        xv = x_ref[...]
        rstd = lax.rsqrt(jnp.mean(xv * xv, axis=-1, keepdims=True) + EPS)
        xh = xv * rstd
        err = xh * w_ref[...] - t_ref[...]
        per_row = jnp.mean(err * err, axis=-1, keepdims=True)
        loss_ref[...] += 0.5 * jnp.sum(per_row, axis=0, keepdims=True)
        dy = err * (1.0 / d)
        g = dy * w_ref[...]
        dx_ref[...] = rstd * (g - xh * jnp.mean(g * xh, axis=-1, keepdims=True))
        dw_ref[...] += jnp.sum(dy * xh, axis=0, keepdims=True)

    row = pl.BlockSpec((tr, d), lambda i: (i, 0))
    vec = pl.BlockSpec((1, d), lambda i: (0, 0))
    one = pl.BlockSpec((1, 1), lambda i: (0, 0))
    return pl.pallas_call(
        body, name=name, grid=(seq // tr,), in_specs=[row, vec, row], out_specs=[one, row, vec],
        out_shape=[jax.ShapeDtypeStruct((1, 1), F32), jax.ShapeDtypeStruct((seq, d), F32),
                   jax.ShapeDtypeStruct((1, d), F32)],
        compiler_params=pltpu.CompilerParams(dimension_semantics=("arbitrary",), vmem_limit_bytes=VMEM_LIMIT),
    )(x, w, target)


SLAB_TILE_ROWS = 1024


def _slab_tile(rows, cap=SLAB_TILE_ROWS):
    step = 16 if rows % 16 == 0 else 8
    return max(t for t in range(step, min(rows, cap) + 1, step) if rows % t == 0)


def _adamw(name, w, g, m, v):
    rows, cols = w.shape
    tr = _slab_tile(rows, SLAB_TILE_ROWS // 2) if rows % 8 == 0 else rows

    def body(w_ref, g_ref, m_ref, v_ref, d_ref, nm_ref, nv_ref):
        gv = g_ref[...]
        nm = ADAM_B1 * m_ref[...] + (1.0 - ADAM_B1) * gv
        nv = ADAM_B2 * v_ref[...] + (1.0 - ADAM_B2) * (gv * gv)
        m_hat = nm / (1.0 - ADAM_B1 ** ADAM_STEP)
        v_hat = nv / (1.0 - ADAM_B2 ** ADAM_STEP)
        d_ref[...] = -ADAM_LR * (m_hat / (jnp.sqrt(v_hat) + ADAM_EPS) + ADAM_WD * w_ref[...])
        nm_ref[...] = nm
        nv_ref[...] = nv

    spec = pl.BlockSpec((tr, cols), lambda i: (i, 0))
    sds = jax.ShapeDtypeStruct(w.shape, F32)
    return pl.pallas_call(
        body, name=name, grid=(rows // tr,), in_specs=[spec] * 4, out_specs=[spec] * 3, out_shape=[sds] * 3,
        compiler_params=pltpu.CompilerParams(dimension_semantics=("parallel",), vmem_limit_bytes=VMEM_LIMIT),
    )(w, g, m, v)


WIRE_DTYPE = jnp.bfloat16


def _add_halves(name, g, t1, c):
    nsec, rows, _ = g.shape
    rh = rows // 2
    tr = _slab_tile(rh)
    nb = rh // tr

    def body(c_ref, g_ref, t_ref, o_ref):
        o_ref[...] = (g_ref[...] + t_ref[...]).astype(o_ref.dtype)

    gs = pltpu.PrefetchScalarGridSpec(
        num_scalar_prefetch=1, grid=(nsec, nb),
        in_specs=[pl.BlockSpec((1, tr, LANES), lambda s, i, c_ref: (s, c_ref[0] * nb + i, 0)),
                  pl.BlockSpec((1, tr, LANES), lambda s, i, c_ref: (s, i, 0))],
        out_specs=pl.BlockSpec((1, tr, LANES), lambda s, i, c_ref: (s, i, 0)))
    return pl.pallas_call(
        body, name=name, grid_spec=gs, out_shape=jax.ShapeDtypeStruct((nsec, rh, LANES), WIRE_DTYPE),
        compiler_params=pltpu.CompilerParams(dimension_semantics=("parallel", "parallel"),
                                             vmem_limit_bytes=VMEM_LIMIT),
    )(c, g, t1)


ANY = pl.BlockSpec(memory_space=pl.ANY)


def _place():
    return lax.axis_index("x"), lax.axis_index("y"), lax.axis_index("c")


def _all_gather_shards(name, slab):
    rows = slab.shape[0]
    rh = rows // 2
    rq = rh // 2

    def body(x_ref, out_ref, send_sems, recv_sems):
        x, y, c = _place()
        me, sibling = (x, y, c), (x, y, 1 - c)
        xn, yn, dg = (1 - x, y), (x, 1 - y), (1 - x, 1 - y)

        def piece(chip, core, q):
            return out_ref.at[2 * chip[0] + chip[1], pl.ds(core * rh + q * rq, rq), :]

        def copy(k, chip, core, q, to, src=None):
            return pltpu.make_async_remote_copy(
                src_ref=piece(chip, core, q) if src is None else src, dst_ref=piece(chip, core, q),
                send_sem=send_sems.at[k], recv_sem=recv_sems.at[k], device_id=to, device_id_type=MESH)

        own = [x_ref.at[pl.ds(c * rh + q * rq, rq), :] for q in range(2)]
        sends = [copy(0, (x, y), c, 0, (*xn, c), src=own[0]), copy(1, (x, y), c, 1, (*xn, c), src=own[1]),
                 copy(2, (x, y), c, 0, (*yn, c), src=own[0]), copy(3, (x, y), c, 1, (*yn, c), src=own[1])]
        for cp in sends:
            cp.start()
        landed = [(0, xn, 0), (3, yn, 1), (1, xn, 1), (2, yn, 0), (4, dg, 0), (5, dg, 1)]
        onward = {0: (4, (*yn, c)), 3: (5, (*xn, c))}
        for i, (k, chip, q) in enumerate(landed):
            copy(k, chip, c, q, me).wait_recv()
            if k in onward:
                fk, to = onward[k]
                sends.append(copy(fk, chip, c, q, to))
                sends[-1].start()
            sends.append(copy(6 + i, chip, c, q, sibling))
            sends[-1].start()
        for i, (k, chip, q) in enumerate(landed):
            copy(6 + i, chip, 1 - c, q, me).wait_recv()
        for cp in sends:
            cp.wait_send()

    got = pl.pallas_call(
        body, name=name, in_specs=[ANY], out_specs=ANY,
        out_shape=jax.ShapeDtypeStruct((4, rows, LANES), slab.dtype),
        scratch_shapes=[pltpu.SemaphoreType.DMA((12,)), pltpu.SemaphoreType.DMA((12,))],
    )(slab)
    return lax.dynamic_update_slice(got, slab[None], (2 * lax.axis_index("x") + lax.axis_index("y"), 0, 0))


def _swap_halves(name, g):
    nsec, rows, _ = g.shape
    rh = rows // 2

    def body(g_ref, t_ref, send_sem, recv_sem):
        x, y, c = _place()
        cp = pltpu.make_async_remote_copy(
            src_ref=g_ref.at[:, pl.ds((1 - c) * rh, rh), :], dst_ref=t_ref, send_sem=send_sem, recv_sem=recv_sem,
            device_id=(x, y, 1 - c), device_id_type=MESH)
        cp.start()
        cp.wait()

    return pl.pallas_call(
        body, name=name, in_specs=[ANY], out_specs=ANY, out_shape=jax.ShapeDtypeStruct((nsec, rh, LANES), F32),
        scratch_shapes=[pltpu.SemaphoreType.DMA, pltpu.SemaphoreType.DMA],
    )(g)


def _exchange_stage1(name, p):
    _, rh, _ = p.shape
    rq = rh // 2

    def body(p_ref, fx_ref, fy_ref, send_sems, recv_sems):
        x, y, c = _place()
        to_x = pltpu.make_async_remote_copy(
            src_ref=p_ref.at[pl.ds(2 * (1 - x), 2), pl.ds(0, rq), :], dst_ref=fx_ref, send_sem=send_sems.at[0],
            recv_sem=recv_sems.at[0], device_id=(1 - x, y, c), device_id_type=MESH)
        to_y = [pltpu.make_async_remote_copy(
            src_ref=p_ref.at[2 * sx + (1 - y), pl.ds(rq, rq), :], dst_ref=fy_ref.at[sx], send_sem=send_sems.at[1 + sx],
            recv_sem=recv_sems.at[1 + sx], device_id=(x, 1 - y, c), device_id_type=MESH) for sx in range(2)]
        for cp in [to_x] + to_y:
            cp.start()
        for cp in [to_x] + to_y:
            cp.wait_recv()
        for cp in [to_x] + to_y:
            cp.wait_send()

    sds = jax.ShapeDtypeStruct((2, rq, LANES), p.dtype)
    return pl.pallas_call(
        body, name=name, in_specs=[ANY], out_specs=[ANY, ANY], out_shape=[sds, sds],
        scratch_shapes=[pltpu.SemaphoreType.DMA((3,)), pltpu.SemaphoreType.DMA((3,))],
    )(p)


def _exchange_add1(name, p, from_x, from_y, place):
    _, rh, _ = p.shape
    rq = rh // 2
    tr = _slab_tile(rq)
    nb = rq // tr

    def body(xy_ref, pa_s, pa_k, pb_s, pb_k, fx_s, fx_k, fy_s, fy_k, sa, ka, sb, kb):
        for mine, theirs, out in ((pa_s, fx_s, sa), (pa_k, fx_k, ka), (pb_s, fy_s, sb), (pb_k, fy_k, kb)):
            out[...] = (mine[0].astype(F32) + theirs[0].astype(F32)).astype(out.dtype)

    blk = lambda fn: pl.BlockSpec((1, tr, LANES), fn)
    gs = pltpu.PrefetchScalarGridSpec(
        num_scalar_prefetch=1, grid=(nb,),
        in_specs=[blk(lambda i, xy: (2 * xy[0] + 1 - xy[1], i, 0)), blk(lambda i, xy: (2 * xy[0] + xy[1], i, 0)),
                  blk(lambda i, xy: (2 * (1 - xy[0]) + xy[1], nb + i, 0)), blk(lambda i, xy: (2 * xy[0] + xy[1], nb + i, 0)),
                  blk(lambda i, xy: (1 - xy[1], i, 0)), blk(lambda i, xy: (xy[1], i, 0)),
                  blk(lambda i, xy: (1 - xy[0], i, 0)), blk(lambda i, xy: (xy[0], i, 0))],
        out_specs=[pl.BlockSpec((tr, LANES), lambda i, xy: (i, 0))] * 4)
    sds = jax.ShapeDtypeStruct((rq, LANES), p.dtype)
    return pl.pallas_call(
        body, name=name, grid_spec=gs, out_shape=[sds] * 4,
        compiler_params=pltpu.CompilerParams(dimension_semantics=("parallel",), vmem_limit_bytes=VMEM_LIMIT),
    )(place, p, p, p, p, from_x, from_x, from_y, from_y)


def _exchange_stage2(name, send_a, send_b):
    def body(a_ref, b_ref, fa_ref, fb_ref, send_sems, recv_sems):
        x, y, c = _place()
        cps = [pltpu.make_async_remote_copy(src_ref=a_ref, dst_ref=fa_ref, send_sem=send_sems.at[0],
                                            recv_sem=recv_sems.at[0], device_id=(x, 1 - y, c), device_id_type=MESH),
               pltpu.make_async_remote_copy(src_ref=b_ref, dst_ref=fb_ref, send_sem=send_sems.at[1],
                                            recv_sem=recv_sems.at[1], device_id=(1 - x, y, c), device_id_type=MESH)]
        for cp in cps:
            cp.start()
        for cp in cps:
            cp.wait_recv()
        for cp in cps:
            cp.wait_send()

    sds = jax.ShapeDtypeStruct(send_a.shape, send_a.dtype)
    return pl.pallas_call(
        body, name=name, in_specs=[ANY, ANY], out_specs=[ANY, ANY], out_shape=[sds, sds],
        scratch_shapes=[pltpu.SemaphoreType.DMA((2,)), pltpu.SemaphoreType.DMA((2,))],
    )(send_a, send_b)


def _exchange_add2(name, keep_a, got_a, keep_b, got_b, c):
    rq = keep_a.shape[0]
    tr = _slab_tile(rq)

    def body(c_ref, ka, ga, kb, gb, o_ref):
        o_ref[0] = ka[...].astype(F32) + ga[...].astype(F32)
        o_ref[1] = kb[...].astype(F32) + gb[...].astype(F32)

    spec = pl.BlockSpec((tr, LANES), lambda i, c_ref: (i, 0))
    gs = pltpu.PrefetchScalarGridSpec(
        num_scalar_prefetch=1, grid=(rq // tr,), in_specs=[spec] * 4,
        out_specs=pl.BlockSpec((2, tr, LANES), lambda i, c_ref: (c_ref[0], i, 0)))
    out = pl.pallas_call(
        body, name=name, grid_spec=gs, out_shape=jax.ShapeDtypeStruct((4, rq, LANES), F32),
        compiler_params=pltpu.CompilerParams(dimension_semantics=("parallel",), vmem_limit_bytes=VMEM_LIMIT),
    )(c, keep_a, got_a, keep_b, got_b)
    return out.reshape(4 * rq, LANES)


def _join_halves(name, full):
    rh = full.shape[0] // 2

    def body(in_ref, o_ref, send_sem, recv_sem):
        x, y, c = _place()
        cp = pltpu.make_async_remote_copy(
            src_ref=in_ref.at[pl.ds(c * rh, rh), :], dst_ref=o_ref.at[pl.ds(c * rh, rh), :], send_sem=send_sem,
            recv_sem=recv_sem, device_id=(x, y, 1 - c), device_id_type=MESH)
        cp.start()
        pltpu.make_async_remote_copy(
            src_ref=in_ref.at[pl.ds(c * rh, rh), :], dst_ref=o_ref.at[pl.ds((1 - c) * rh, rh), :], send_sem=send_sem,
            recv_sem=recv_sem, device_id=(x, y, 1 - c), device_id_type=MESH).wait_recv()
        cp.wait_send()

    return pl.pallas_call(
        body, name=name, in_specs=[ANY], out_specs=ANY, out_shape=jax.ShapeDtypeStruct(full.shape, full.dtype),
        input_output_aliases={0: 0}, scratch_shapes=[pltpu.SemaphoreType.DMA, pltpu.SemaphoreType.DMA],
    )(full)


def _rows_of(n):
    return -(-n // LANES)


SLAB_ROW_ALIGN = 512


def _flat_rows(arrays, dtype):
    parts = []
    for a in arrays:
        flat = a.reshape(-1).astype(dtype)
        parts.append(jnp.pad(flat, (0, _rows_of(flat.size) * LANES - flat.size)))
    return jnp.concatenate(parts).reshape(-1, LANES)


def _align_rows(slab):
    rows = slab.shape[0]
    return jnp.pad(slab, ((0, -(-rows // SLAB_ROW_ALIGN) * SLAB_ROW_ALIGN - rows), (0, 0)))


def _pack(arrays, dtype):
    return _align_rows(_flat_rows(arrays, dtype))


def _unpack(slab, shapes):
    out, r = [], 0
    for shp in shapes:
        n = math.prod(shp)
        out.append(slab[r:r + _rows_of(n)].reshape(-1)[:n].reshape(shp))
        r += _rows_of(n)
    return out


def _unpack_gathered(g, shapes, kinds):
    out, r = [], 0
    for shp, kind in zip(shapes, kinds):
        n = math.prod(shp)
        blk = g[:, r:r + _rows_of(n)].reshape(4, -1)[:, :n].reshape((4,) + tuple(shp))
        r += _rows_of(n)
        if kind == "col":
            out.append(jnp.moveaxis(blk, 0, 1).reshape(shp[0], 4 * shp[1]))
        else:
            out.append(blk.reshape(4 * shp[0], shp[1]))
    return out


def _shard_block(g, kind, s, local_shape):
    if kind == "col":
        return g[:, s * local_shape[1]:(s + 1) * local_shape[1]]
    if kind == "row":
        return g[s * local_shape[0]:(s + 1) * local_shape[0]]
    return g


def _rotary_tables(seq):
    half = RET_DK // 2
    pos = jnp.arange(seq, dtype=F32)
    inv = ROPE_THETA ** (-jnp.arange(half, dtype=F32) / half)
    ang = pos[:, None] * inv[None, :]
    cos, sin = jnp.cos(ang), jnp.sin(ang)
    return jnp.concatenate([cos, cos], axis=1), jnp.concatenate([-sin, sin], axis=1)


def _retention_tables():
    log_gamma = jnp.log(1.0 - 2.0 ** (-5.0 - jnp.arange(RET_HEADS, dtype=F32)))
    idx = jnp.arange(CHUNK, dtype=F32)
    diff = idx[:, None] - idx[None, :]
    dmask = jnp.exp(jnp.where((diff >= 0)[None], log_gamma[:, None, None] * diff[None], -jnp.inf))
    kdec = jnp.exp(log_gamma[None, :] * (CHUNK - 1.0 - idx)[:, None])
    qdec = jnp.exp(log_gamma[None, :] * (idx + 1.0)[:, None])
    cdec = jnp.exp(log_gamma * CHUNK)[None, :]
    lanes = lambda t: jnp.repeat(t, RET_DK, axis=1)
    return dmask.reshape(RET_HEADS * CHUNK, CHUNK), lanes(kdec), lanes(qdec), lanes(cdec)


def _s5_prep(a_re, a_im, log_step, b_re, b_im, c_re, c_im):
    g, n, c = S5_GROUPS, S5_STATE, S5_GROUP
    lam = lax.complex(a_re, a_im)
    step = jnp.exp(log_step)[:, None]
    lam_bar = jnp.exp(lam * step)
    b_bar = ((lam_bar - 1.0) / lam)[..., None] * lax.complex(b_re, b_im)
    eye = jnp.eye(g, dtype=F32)
    bb_re = (jnp.real(b_bar).transpose(0, 2, 1)[:, :, None, :] * eye[:, None, :, None]).reshape(g * c, g * n)
    bb_im = (jnp.imag(b_bar).transpose(0, 2, 1)[:, :, None, :] * eye[:, None, :, None]).reshape(g * c, g * n)
    cc_re = (c_re.transpose(0, 2, 1)[:, :, None, :] * eye[:, None, :, None]).reshape(g * n, g * c)
    cc_im = (c_im.transpose(0, 2, 1)[:, :, None, :] * eye[:, None, :, None]).reshape(g * n, g * c)
    return (jnp.real(lam_bar).reshape(1, g * n), jnp.imag(lam_bar).reshape(1, g * n),
            jnp.concatenate([bb_re, bb_im], axis=1), cc_re, cc_im)


def kernel(x, l0_norm_mix, l0_w_in, ssd_conv_w, ssd_conv_b, ssd_dt_bias, ssd_A_log, ssd_D, ssd_norm_w, l0_w_out, l0_norm_mlp, l0_w_up, l0_w_down, l1_norm_mix, l1_w_in, gdn_conv_w, gdn_A_log, gdn_dt_bias, gdn_norm_w, s5_A_re, s5_A_im, s5_log_step, s5_B_re, s5_B_im, s5_C_re, s5_C_im, s5_D, s5_w_glu, s5_b_glu, l1_w_out, l1_norm_mlp, l1_w_up, l1_w_down, final_norm, loss_target, m_l0_norm_mix, m_l0_w_in, m_ssd_conv_w, m_ssd_conv_b, m_ssd_dt_bias, m_ssd_A_log, m_ssd_D, m_ssd_norm_w, m_l0_w_out, m_l0_norm_mlp, m_l0_w_up, m_l0_w_down, m_l1_norm_mix, m_l1_w_in, m_gdn_conv_w, m_gdn_A_log, m_gdn_dt_bias, m_gdn_norm_w, m_s5_A_re, m_s5_A_im, m_s5_log_step, m_s5_B_re, m_s5_B_im, m_s5_C_re, m_s5_C_im, m_s5_D, m_s5_w_glu, m_s5_b_glu, m_l1_w_out, m_l1_norm_mlp, m_l1_w_up, m_l1_w_down, m_final_norm, v_l0_norm_mix, v_l0_w_in, v_ssd_conv_w, v_ssd_conv_b, v_ssd_dt_bias, v_ssd_A_log, v_ssd_D, v_ssd_norm_w, v_l0_w_out, v_l0_norm_mlp, v_l0_w_up, v_l0_w_down, v_l1_norm_mix, v_l1_w_in, v_gdn_conv_w, v_gdn_A_log, v_gdn_dt_bias, v_gdn_norm_w, v_s5_A_re, v_s5_A_im, v_s5_log_step, v_s5_B_re, v_s5_B_im, v_s5_C_re, v_s5_C_im, v_s5_D, v_s5_w_glu, v_s5_b_glu, v_l1_w_out, v_l1_norm_mlp, v_l1_w_up, v_l1_w_down, v_final_norm):
    given = dict(locals())
    names = [n for n, _ in PARAMS]
    kinds = dict(PARAMS)
    w = {n: given[n] for n in names}
    seq = x.shape[1]
    x0 = x.reshape(seq, D_MODEL)
    target = loss_target.reshape(seq, D_MODEL)

    gb = _all_gather_shards("gather_weights", _pack([w[n] for n in GATHER_BF16], _MXU_DTYPE))
    full = dict(zip(GATHER_BF16, _unpack_gathered(gb, [w[n].shape for n in GATHER_BF16],
                                                  [kinds[n] for n in GATHER_BF16])))
    gf = _all_gather_shards("gather_conv", _pack([w[n] for n in GATHER_F32], F32))
    full.update(zip(GATHER_F32, _unpack_gathered(gf, [w[n].shape for n in GATHER_F32],
                                                 [kinds[n] for n in GATHER_F32])))
    in0 = full["l0_w_in"].shape[1]
    w_in0 = jnp.pad(full["l0_w_in"], ((0, 0), (0, IN0_PAD - in0)))
    wi1 = full["l1_w_in"]
    in1 = wi1.shape[1]
    w_in1 = jnp.concatenate([wi1[:, :3072], wi1[:, 3084:in1], wi1[:, 3072:3084],
                             jnp.zeros((D_MODEL, IN1_PAD - in1), wi1.dtype)], axis=1)

    row = lambda a: a.reshape(1, -1)
    lanes64 = lambda a: jnp.repeat(a, SSD_HEAD_DIM).reshape(1, -1)

    h0 = _rmsnorm_fwd("norm_mix0", x0, row(w["l0_norm_mix"]))
    proj0 = _matmul("in_proj0", h0, w_in0, "nn")
    cos_t, sin_t = _rotary_tables(seq)
    ret_tabs = list(_retention_tables())
    ret_xs = [(proj0, 512, 0), (proj0, 512, 1), (proj0, 512, 2), (proj0, 512, 3)]
    ret_xt = [(cos_t, 128, 0), (sin_t, 128, 0)]
    ret_states = [(512, 128)]
    mixed0, ret_saved = _scan_fwd("ret_fwd", _f_ret, CHUNK, ret_tabs, [], ret_xs, ret_xt, ret_states, D_MODEL, 512, 0)
    expand = jnp.repeat(jnp.eye(128, SSD_HEADS, dtype=F32), SSD_HEAD_DIM, axis=1)
    ssd_consts = [full["ssd_conv_w"], row(w["ssd_conv_b"]), lanes64(w["ssd_dt_bias"]), lanes64(w["ssd_A_log"]),
                  lanes64(w["ssd_D"]), row(w["ssd_norm_w"])]
    ssd_xs = [(proj0, 512, 4), (proj0, 512, 5), (proj0, 256, 12), (proj0, 256, 13), (proj0, 128, 28)]
    ssd_states = [(8, 512), (8, 256), (8, 256), (512, 128)]
    mixed0, ssd_saved = _scan_fwd("ssd_fwd", _f_ssd, CHUNK, [expand], ssd_consts, ssd_xs, [], ssd_states,
                                  D_MODEL, 512, 1, y_alias=mixed0)
    x1, h1 = _matmul("out_proj0", mixed0, full["l0_w_out"], "nn", epi="add", epi_arr=x0, norm_w=row(w["l0_norm_mlp"]))
    u0 = _matmul("up0", h1, full["l0_w_up"], "nn")
    x2, h2 = _matmul("down0", u0, full["l0_w_down"], "nn", a_pro="relu2", epi="add", epi_arr=x1,
                     norm_w=row(w["l1_norm_mix"]))

    proj1 = _matmul("in_proj1", h2, w_in1, "nn")
    p_alog = jnp.zeros((1, 128), F32).at[0, 6:12].set(w["gdn_A_log"])
    p_dtb = jnp.zeros((1, 128), F32).at[0, 6:12].set(w["gdn_dt_bias"])
    gdn_consts = [full["gdn_conv_w"], p_alog, p_dtb, row(w["gdn_norm_w"])]
    gdn_xs = [(proj1, 768, 0), (proj1, 768, 1), (proj1, 768, 2), (proj1, 768, 3), (proj1, 128, 26)]
    gdn_states = [(8, 768), (8, 768), (8, 768), (768, 256)]
    mixed1, gdn_saved = _scan_fwd("gdn_fwd", _f_gdn, CHUNK, [], gdn_consts, gdn_xs, [], gdn_states, D_MODEL, 768, 0)
    s5_args = (w["s5_A_re"], w["s5_A_im"], w["s5_log_step"], w["s5_B_re"], w["s5_B_im"], w["s5_C_re"], w["s5_C_im"])
    (lam_re, lam_im, bblk, cc_re, cc_im), s5_prep_vjp = jax.vjp(_s5_prep, *s5_args)
    s5_consts = [lam_re, lam_im, bblk, cc_re, cc_im, row(w["s5_D"]), full["s5_w_glu"].astype(F32), row(w["s5_b_glu"])]
    s5_xs = [(proj1, 256, 12)]
    s5_states = [(8, 1024), (8, 1024)]
    mixed1, s5_saved = _scan_fwd("s5_fwd", _f_s5, CHUNK, [], s5_consts, s5_xs, [], s5_states, D_MODEL, 256, 3,
                                 y_alias=mixed1)
    x3, h3 = _matmul("out_proj1", mixed1, full["l1_w_out"], "nn", epi="add", epi_arr=x2, norm_w=row(w["l1_norm_mlp"]))
    u1 = _matmul("up1", h3, full["l1_w_up"], "nn")
    x4 = _matmul("down1", u1, full["l1_w_down"], "nn", a_pro="relu2", epi="add", epi_arr=x3)

    loss_part, dx4, d_final = _loss_head("loss_head", x4, row(w["final_norm"]), target)
    loss = lax.psum(loss_part[0, 0], ("x", "y", "c"))
    grads = {"final_norm": d_final.reshape(-1)}

    du1 = _matmul("down1_dx", dx4, full["l1_w_down"], "nt", out_dtype=_MXU_DTYPE, epi="drelu2", epi_arr=u1)
    grads["l1_w_down"] = _matmul("down1_dw", u1, dx4, "tn", a_pro="relu2")
    grads["l1_w_up"] = _matmul("up1_dw", h3, du1, "tn")
    dx3, dwn = _matmul("up1_dx", du1, full["l1_w_up"], "nt", epi="norm_bwd", epi_arr=dx4, norm_x=x3,
                       norm_w=row(w["l1_norm_mlp"]))
    grads["l1_norm_mlp"] = dwn.reshape(-1)
    grads["l1_w_out"] = _matmul("out_proj1_dw", mixed1, dx3, "tn")
    dmixed1 = _matmul("out_proj1_dx", dx3, full["l1_w_out"], "nt")

    def gdn_assemble(dx):
        dq, dk, dv, dz, dba = dx
        zeros = lambda n: jnp.zeros((dq.shape[0], n), F32)
        return jnp.concatenate([dq, dk, dv, dz, zeros(256), dba, zeros(IN1_PAD - 3456)], axis=1)

    dproj1, gdn_dc = _scan_bwd("gdn_bwd", _f_gdn, CHUNK, [], gdn_consts, gdn_xs, [], gdn_saved, gdn_states,
                               (dmixed1, 768, 0), IN1_PAD, IN1_PAD, 0, gdn_assemble)
    dproj1, s5_dc = _scan_bwd("s5_bwd", _f_s5, CHUNK, [], s5_consts, s5_xs, [], s5_saved, s5_states,
                              (dmixed1, 256, 3), IN1_PAD, 256, 12, lambda dx: dx[0], dx_alias=dproj1)
    grads["gdn_conv_w"] = gdn_dc[0]
    grads["gdn_A_log"] = gdn_dc[1][0, 6:12]
    grads["gdn_dt_bias"] = gdn_dc[2][0, 6:12]
    grads["gdn_norm_w"] = gdn_dc[3].reshape(-1)
    s5_pg = s5_prep_vjp(tuple(s5_dc[:5]))
    for n, gval in zip(("s5_A_re", "s5_A_im", "s5_log_step", "s5_B_re", "s5_B_im", "s5_C_re", "s5_C_im"), s5_pg):
        grads[n] = gval
    grads["s5_D"] = s5_dc[5].reshape(-1)
    grads["s5_w_glu"] = s5_dc[6]
    grads["s5_b_glu"] = s5_dc[7].reshape(-1)
    dwi1 = _matmul("in_proj1_dw", h2, dproj1, "tn")
    grads["l1_w_in"] = jnp.concatenate([dwi1[:, :3072], dwi1[:, 3328:3340], dwi1[:, 3072:3328]], axis=1)
    dx2, dwn = _matmul("in_proj1_dx", dproj1, w_in1, "nt", epi="norm_bwd", epi_arr=dx3, norm_x=x2,
                       norm_w=row(w["l1_norm_mix"]))
    grads["l1_norm_mix"] = dwn.reshape(-1)

    du0 = _matmul("down0_dx", dx2, full["l0_w_down"], "nt", out_dtype=_MXU_DTYPE, epi="drelu2", epi_arr=u0)
    grads["l0_w_down"] = _matmul("down0_dw", u0, dx2, "tn", a_pro="relu2")
    grads["l0_w_up"] = _matmul("up0_dw", h1, du0, "tn")
    dx1, dwn = _matmul("up0_dx", du0, full["l0_w_up"], "nt", epi="norm_bwd", epi_arr=dx2, norm_x=x1,
                       norm_w=row(w["l0_norm_mlp"]))
    grads["l0_norm_mlp"] = dwn.reshape(-1)
    grads["l0_w_out"] = _matmul("out_proj0_dw", mixed0, dx1, "tn")
    dmixed0 = _matmul("out_proj0_dx", dx1, full["l0_w_out"], "nt")
    dproj0, _ = _scan_bwd("ret_bwd", _f_ret, CHUNK, ret_tabs, [], ret_xs, ret_xt, ret_saved, ret_states,
                          (dmixed0, 512, 0), IN0_PAD, 2048, 0, lambda dx: jnp.concatenate(dx, axis=1))

    def ssd_assemble(dx):
        return jnp.concatenate(list(dx) + [jnp.zeros((dx[0].shape[0], 2048 - 1664), F32)], axis=1)

    dproj0, ssd_dc = _scan_bwd("ssd_bwd", _f_ssd, CHUNK, [expand], ssd_consts, ssd_xs, [], ssd_saved, ssd_states,
                               (dmixed0, 512, 1), IN0_PAD, 2048, 1, ssd_assemble, dx_alias=dproj0)
    heads = lambda a: a.reshape(SSD_HEADS, SSD_HEAD_DIM).sum(axis=1)
    grads["ssd_conv_w"] = ssd_dc[0]
    grads["ssd_conv_b"] = ssd_dc[1].reshape(-1)
    grads["ssd_dt_bias"] = heads(ssd_dc[2])
    grads["ssd_A_log"] = heads(ssd_dc[3])
    grads["ssd_D"] = heads(ssd_dc[4])
    grads["ssd_norm_w"] = ssd_dc[5].reshape(-1)
    grads["l0_w_in"] = _matmul("in_proj0_dw", h0, dproj0, "tn")[:, :in0]
    dx0, dwn = _matmul("in_proj0_dx", dproj0, w_in0, "nt", epi="norm_bwd", epi_arr=dx1, norm_x=x0,
                       norm_w=row(w["l0_norm_mix"]))
    grads["l0_norm_mix"] = dwn.reshape(-1)
    grad_x = dx0.reshape(x.shape)

    c_idx = lax.axis_index("c").astype(jnp.int32).reshape(1)
    small = SMALL_SHARDED + tuple(n for n in names if kinds[n] == "rep")
    order = LARGE + small
    block = lambda n, s: _shard_block(grads[n].reshape(_full_shape(n, w, kinds)), kinds[n], s, w[n].shape)
    rep_rows = _flat_rows([grads[n] for n in order[len(LARGE) + len(SMALL_SHARDED):]], F32)
    gslab = jnp.stack([_align_rows(jnp.concatenate(
        [_flat_rows([block(n, s) for n in LARGE + SMALL_SHARDED], F32), rep_rows])) for s in range(4)])
    from_sibling = _swap_halves("grads_swap_halves", gslab)
    chip_sum = _add_halves("grads_add_sibling", gslab, from_sibling, c_idx)
    place = jnp.stack([lax.axis_index("x"), lax.axis_index("y")]).astype(jnp.int32)
    from_x, from_y = _exchange_stage1("grads_stage1", chip_sum)
    send_a, keep_a, send_b, keep_b = _exchange_add1("grads_add1", chip_sum, from_x, from_y, place)
    got_a, got_b = _exchange_stage2("grads_stage2", send_a, send_b)
    my_half = _exchange_add2("grads_add2", keep_a, got_a, keep_b, got_b, c_idx)
    gsum = _join_halves("grads_join_halves", my_half)
    grad = dict(zip(order, _unpack(gsum, [w[n].shape for n in order])))

    delta, new_m, new_v = {}, {}, {}
    for n in LARGE:
        delta[n], new_m[n], new_v[n] = _adamw("adamw_" + n, w[n], grad[n], given["m_" + n], given["v_" + n])
    first_small = sum(_rows_of(math.prod(w[n].shape)) for n in LARGE)
    small_shapes = [w[n].shape for n in small]

    def small_slab(arrays):
        rows = _flat_rows(arrays, F32)
        return jnp.pad(rows, ((0, gsum.shape[0] - first_small - rows.shape[0]), (0, 0)))

    res = _adamw("adamw_small", small_slab([w[n] for n in small]), gsum[first_small:],
                 small_slab([given["m_" + n] for n in small]), small_slab([given["v_" + n] for n in small]))
    for out, slab in zip((delta, new_m, new_v), res):
        out.update(zip(small, _unpack(slab, small_shapes)))
    return (loss, grad_x, *[grad[n] for n in names], *[delta[n] for n in names], *[new_m[n] for n in names],
            *[new_v[n] for n in names])


def _full_shape(name, w, kinds):
    shp = w[name].shape
    if kinds[name] == "col":
        return (shp[0], 4 * shp[1])
    if kinds[name] == "row":
        return (4 * shp[0],) + tuple(shp[1:])
    return shp
```

```python
import functools
import math

import jax
import jax.numpy as jnp
from jax import lax
from jax.experimental import pallas as pl
from jax.experimental.pallas import tpu as pltpu

F32 = jnp.float32
_MXU_DTYPE = jnp.bfloat16

D_MODEL = 1024
CHUNK = 64
EPS = 1e-6
RET_HEADS, RET_DK = 4, 128
ROPE_THETA = 10000.0
SSD_HEADS, SSD_HEAD_DIM = 8, 64
GDN_HEADS, GDN_DK = 6, 128
S5_GROUPS, S5_GROUP, S5_STATE = 16, 16, 64
ADAM_LR, ADAM_B1, ADAM_B2, ADAM_EPS, ADAM_WD, ADAM_STEP = 0.001, 0.9, 0.999, 1e-08, 0.01, 10

IN0_PAD = 4096
IN1_PAD = 3584
LANES = 1024
VMEM_LIMIT = 56 * 1024 * 1024
MESH = pl.DeviceIdType.MESH

PARAMS = (
    ("l0_norm_mix", "rep"), ("l0_w_in", "col"), ("ssd_conv_w", "col"), ("ssd_conv_b", "rep"),
    ("ssd_dt_bias", "rep"), ("ssd_A_log", "rep"), ("ssd_D", "rep"), ("ssd_norm_w", "rep"),
    ("l0_w_out", "row"), ("l0_norm_mlp", "rep"), ("l0_w_up", "col"), ("l0_w_down", "row"),
    ("l1_norm_mix", "rep"), ("l1_w_in", "col"), ("gdn_conv_w", "col"), ("gdn_A_log", "rep"),
    ("gdn_dt_bias", "rep"), ("gdn_norm_w", "rep"), ("s5_A_re", "rep"), ("s5_A_im", "rep"),
    ("s5_log_step", "rep"), ("s5_B_re", "rep"), ("s5_B_im", "rep"), ("s5_C_re", "rep"), ("s5_C_im", "rep"),
    ("s5_D", "rep"), ("s5_w_glu", "row"), ("s5_b_glu", "rep"), ("l1_w_out", "row"), ("l1_norm_mlp", "rep"),
    ("l1_w_up", "col"), ("l1_w_down", "row"), ("final_norm", "rep"),
)
GATHER_BF16 = ("l0_w_in", "l0_w_out", "l0_w_up", "l0_w_down", "l1_w_in", "l1_w_out", "l1_w_up", "l1_w_down", "s5_w_glu")
GATHER_F32 = ("ssd_conv_w", "gdn_conv_w")
LARGE = GATHER_BF16[:8]
SMALL_SHARDED = ("s5_w_glu", "ssd_conv_w", "gdn_conv_w")


def _dg(a, b, ca, cb, prec=None):
    return lax.dot_general(a, b, (((ca,), (cb,)), ((), ())), preferred_element_type=F32, precision=prec)


def _lo(a):
    return a.astype(_MXU_DTYPE)


@jax.custom_vjp
def _mm(a, b):
    return _dg(_lo(a), _lo(b), 1, 0)


def _mm_fwd(a, b):
    return _mm(a, b), (a, b)


def _mm_bwd(res, g):
    a, b = res
    return _dg(_lo(g), _lo(b), 1, 1), _dg(_lo(a), _lo(g), 0, 0)


_mm.defvjp(_mm_fwd, _mm_bwd)


@jax.custom_vjp
def _mm_nt(a, b):
    return _dg(_lo(a), _lo(b), 1, 1)


def _mm_nt_fwd(a, b):
    return _mm_nt(a, b), (a, b)


def _mm_nt_bwd(res, g):
    a, b = res
    return _dg(_lo(g), _lo(b), 1, 0), _dg(_lo(g), _lo(a), 0, 0)


_mm_nt.defvjp(_mm_nt_fwd, _mm_nt_bwd)


@jax.custom_vjp
def _mm_tn(a, b):
    return _dg(_lo(a), _lo(b), 0, 0)


def _mm_tn_fwd(a, b):
    return _mm_tn(a, b), (a, b)


def _mm_tn_bwd(res, g):
    a, b = res
    return _dg(_lo(b), _lo(g), 1, 1), _dg(_lo(a), _lo(g), 1, 0)


_mm_tn.defvjp(_mm_tn_fwd, _mm_tn_bwd)


def _split2(x):
    hi = _lo(x)
    return hi, _lo(x - hi.astype(F32))


def _split3(x):
    h1 = _lo(x)
    r1 = x - h1.astype(F32)
    h2 = _lo(r1)
    return h1, h2, _lo(r1 - h2.astype(F32))


def _tri_cum_dir(m, ca):
    n, w = m.shape
    causal, _ = _tri_masks(n)
    out = _dg(causal.astype(_MXU_DTYPE), jnp.concatenate(_split3(m), axis=1), ca, 0)
    return out[:, :w] + out[:, w:2 * w] + out[:, 2 * w:]


@jax.custom_vjp
def _tri_cum(m):
    return _tri_cum_dir(m, 1)


def _tri_cum_fwd(m):
    return _tri_cum_dir(m, 1), None


def _tri_cum_bwd(_, g):
    return (_tri_cum_dir(g, 0),)


_tri_cum.defvjp(_tri_cum_fwd, _tri_cum_bwd)


@jax.custom_vjp
def _mm_exact_rhs(a, e):
    return _dg(jnp.concatenate(_split3(a), axis=1), jnp.concatenate([_lo(e)] * 3, axis=0), 1, 0)


def _mm_exact_rhs_fwd(a, e):
    return _mm_exact_rhs(a, e), e


def _mm_exact_rhs_bwd(e, g):
    return _dg(jnp.concatenate(_split3(g), axis=1), jnp.concatenate([_lo(e)] * 3, axis=1), 1, 1), jnp.zeros_like(e)


_mm_exact_rhs.defvjp(_mm_exact_rhs_fwd, _mm_exact_rhs_bwd)


def _bd(x):
    left = _iota(x.shape, 1) < (x.shape[1] // 2)
    zero = jnp.zeros_like(x)
    return jnp.concatenate([jnp.where(left, x, zero), jnp.where(left, zero, x)], axis=0)


def _unbd(m):
    half = m.shape[0] // 2
    left = _iota((half, m.shape[1]), 1) < (m.shape[1] // 2)
    return jnp.where(left, m[:half], m[half:])


def _pmm_nn(x, y):
    xh, xl = _split2(x)
    yh, yl = _split2(y)
    return _dg(jnp.concatenate([xh, xl, xh], axis=1), jnp.concatenate([_bd(yh), _bd(yh), _bd(yl)], axis=0), 1, 0)


def _pmm_nt(x, y):
    xh, xl = _split2(x)
    yh, yl = _split2(y)
    return _dg(jnp.concatenate([xh, xl, xh], axis=1), jnp.concatenate([_bd(yh), _bd(yh), _bd(yl)], axis=1), 1, 1)


def _pmm_tn(x, y):
    xh, xl = _split2(x)
    yh, yl = _split2(y)
    return _unbd(_dg(jnp.concatenate([xh, xl, xh], axis=0), jnp.concatenate([yh, yh, yl], axis=0), 0, 0))


@functools.lru_cache(maxsize=None)
def _shift(s, axis):
    @jax.custom_vjp
    def sh(x):
        return pltpu.roll(x, s, axis)

    def fwd(x):
        return sh(x), None

    def bwd(_, g):
        n = g.shape[axis]
        return (pltpu.roll(g, (n - s) % n, axis),)

    sh.defvjp(fwd, bwd)
    return sh


def _iota(shape, axis):
    return lax.broadcasted_iota(jnp.int32, shape, axis)


def _silu(x):
    return x * jax.nn.sigmoid(x)


def _unit_rms(x):
    return x * lax.rsqrt(jnp.mean(x * x, axis=-1, keepdims=True) + EPS)


def _l2norm(x):
    return x * lax.rsqrt(jnp.sum(x * x, axis=-1, keepdims=True) + EPS)


def _tri_masks(n):
    r, c = _iota((n, n), 0), _iota((n, n), 1)
    return r >= c, r > c


def _packed_rc():
    return _iota((CHUNK, 2 * CHUNK), 0), _iota((CHUNK, 2 * CHUNK), 1) & (CHUNK - 1)


def _decay_packed(g_packed):
    r, c = _packed_rc()
    seg = _tri_cum(g_packed * (r > c).astype(F32))
    return jnp.where(r >= c, jnp.exp(jnp.where(r >= c, seg, 0.0)), 0.0)


def _conv(x, tail, w):
    rows, width = x.shape
    row = _iota((rows, width), 0)
    acc = x * w[3:4, :]
    pad = jnp.zeros((rows - 8, width), F32)
    for j in range(3):
        s = 3 - j
        prev = jnp.concatenate([_shift(s, 0)(tail), pad], axis=0)
        acc = acc + w[j:j + 1, :] * jnp.where(row < s, prev, _shift(s, 0)(x))
    return acc


def _tri_inv_impl(mats):
    r, c = _packed_rc()
    eye = (r == c).astype(F32)

    def same_block(b):
        return (r // b) == (c // b)

    a8 = [jnp.where(same_block(8), a, 0.0) for a in mats]
    a2 = [_pmm_nn(t, t) for t in a8]
    a4 = [_pmm_nn(t, t) for t in a2]
    x = [_pmm_nn(eye - p, eye + q) for p, q in zip(a8, a2)]
    x = [_pmm_nn(p, eye + q) for p, q in zip(x, a4)]
    for b in (8, 16, 32):
        off = [jnp.where(same_block(2 * b) & jnp.logical_not(same_block(b)), a, 0.0) for a in mats]
        y = [_pmm_nn(p, q) for p, q in zip(x, off)]
        x = [p - _pmm_nn(q, p) for p, q in zip(x, y)]
    return x


@jax.custom_vjp
def _tri_inv(mats):
    return _tri_inv_impl(mats)


def _tri_inv_fwd(mats):
    t = _tri_inv_impl(mats)
    return t, t


def _tri_inv_bwd(t, g):
    m1 = [_pmm_tn(p, q) for p, q in zip(t, g)]
    return ([-_pmm_nt(p, q) for p, q in zip(m1, t)],)


_tri_inv.defvjp(_tri_inv_fwd, _tri_inv_bwd)


def _f_ret(tabs, consts, xs, xtabs, states):
    dmask, kdec, qdec, cdec = tabs
    q, k, v, gate = xs
    cs, sn = xtabs
    (st,) = states
    swap = _shift(RET_DK // 2, 1)
    heads = range(RET_HEADS)
    sls = [slice(128 * h, 128 * h + 128) for h in heads]
    qh = [(q[:, sl] * cs + swap(q[:, sl]) * sn) * (RET_DK ** -0.5) for sl in sls]
    kh = [k[:, sl] * cs + swap(k[:, sl]) * sn for sl in sls]
    sh = [st[sl, :] for sl in sls]
    scores = [_mm_nt(a, b) * dmask[64 * h:64 * h + 64, :] for h, a, b in zip(heads, qh, kh)]
    y = [_mm(s, v[:, sl]) for s, sl in zip(scores, sls)]
    y = [t + _mm(a * qdec[:, sl], s) for t, a, sl, s in zip(y, qh, sls, sh)]
    new = [s * cdec[:, sl] + _mm_tn(b * kdec[:, sl], v[:, sl]) for s, sl, b in zip(sh, sls, kh)]
    outs = [_silu(gate[:, sl]) * _unit_rms(t) for sl, t in zip(sls, y)]
    return (jnp.concatenate(outs, axis=1),), [jnp.concatenate(new, axis=0)]


def _f_ssd(tabs, consts, xs, xtabs, states):
    (expand,) = tabs
    conv_w, conv_b, dtb, alog, dskip, nw = consts
    z, xr, br, cr, dtr = xs
    tx, tb, tc, st = states
    xc = _silu(_conv(xr, tx, conv_w[:, 0:512]) + conv_b[:, 0:512])
    bc = _silu(_conv(br, tb, conv_w[:, 512:768]) + conv_b[:, 512:768])
    cc = _silu(_conv(cr, tc, conv_w[:, 768:1024]) + conv_b[:, 768:1024])
    dt = jax.nn.softplus(_mm_exact_rhs(dtr, expand) + dtb)
    la = dt * (-jnp.exp(alog))
    lacum = _tri_cum(la)
    total = jnp.sum(la, axis=0, keepdims=True)
    xd = xc * dt
    dte, ecum, cdec = jnp.exp(total - lacum), jnp.exp(lacum), jnp.exp(total)
    pairs = range(SSD_HEADS // 2)
    sls = [slice(128 * p, 128 * p + 128) for p in pairs]
    bg = [bc[:, 128 * g:128 * g + 128] for g in range(2)]
    cg = [cc[:, 128 * g:128 * g + 128] for g in range(2)]
    cb2 = [_mm_nt(c, jnp.concatenate([b, b], axis=0)) for b, c in zip(bg, cg)]
    lm = [_decay_packed(la[:, sl]) for sl in sls]
    sp = [st[sl, :] for sl in sls]
    ys = [_mm(cg[p // 2], sp[p]) * ecum[:, sls[p]] for p in pairs]
    ys = [ys[p] + _mm(cb2[p // 2] * lm[p], _bd(xd[:, sls[p]])) for p in pairs]
    new = [sp[p] * cdec[:, sls[p]] + _mm_tn(bg[p // 2], xd[:, sls[p]] * dte[:, sls[p]]) for p in pairs]
    y = jnp.concatenate(ys, axis=1) + dskip * xc
    yg = y * _silu(z)
    out = jnp.concatenate([_unit_rms(yg[:, 0:256]), _unit_rms(yg[:, 256:512])], axis=1) * nw
    return (out,), [xr[CHUNK - 8:, :], br[CHUNK - 8:, :], cr[CHUNK - 8:, :], jnp.concatenate(new, axis=0)]


def _f_gdn(tabs, consts, xs, xtabs, states):
    conv_w, p_alog, p_dtb, nw = consts
    qr, kr, vr, z, ba = xs
    tq, tk, tv, st = states
    qc = _silu(_conv(qr, tq, conv_w[:, 0:768]))
    kc = _silu(_conv(kr, tk, conv_w[:, 768:1536]))
    vc = _silu(_conv(vr, tv, conv_w[:, 1536:2304]))
    gl = -jnp.exp(p_alog) * jax.nn.softplus(ba + p_dtb)
    bl = jax.nn.sigmoid(ba)
    gcum = _tri_cum(gl)
    left128 = _iota((CHUNK, 128), 1) < 64
    left256 = _iota((CHUNK, 256), 1) < 128
    r, c = _packed_rc()
    diag_blocks = (_iota((256, 256), 0) < 128) == (_iota((256, 256), 1) < 128)

    def norm2(t):
        return jnp.concatenate([_l2norm(t[:, 0:128]), _l2norm(t[:, 128:256])], axis=1)

    def pick(arr, off, left, p):
        return jnp.where(left, arr[:, off + 2 * p:off + 2 * p + 1], arr[:, off + 2 * p + 1:off + 2 * p + 2])

    pairs = range(GDN_HEADS // 2)
    sls = [slice(256 * p, 256 * p + 256) for p in pairs]
    qn = [norm2(qc[:, sl]) * (GDN_DK ** -0.5) for sl in sls]
    kn = [norm2(kc[:, sl]) for sl in sls]
    dec = [_decay_packed(pick(gl, 6, left128, p)) for p in pairs]
    g2 = [pick(gl, 6, left256, p) for p in pairs]
    gc2 = [pick(gcum, 6, left256, p) for p in pairs]
    b2 = [pick(bl, 0, left256, p) for p in pairs]
    tot = [jnp.sum(t, axis=0, keepdims=True) for t in g2]
    eg = [jnp.exp(t) for t in gc2]
    et = [jnp.exp(t - s) for t, s in zip(tot, gc2)]
    cd = [jnp.exp(t) for t in tot]
    kb = [k * b for k, b in zip(kn, b2)]
    vb = [vc[:, sl] * b for sl, b in zip(sls, b2)]
    kbd = [_bd(k) for k in kn]
    tm = _tri_inv([jnp.where(r > c, _mm_nt(a, b) * d, 0.0) for a, b, d in zip(kb, kbd, dec)])
    u = [_mm(t, _bd(v)) for t, v in zip(tm, vb)]
    w = [_mm(t, _bd(k * e)) for t, k, e in zip(tm, kb, eg)]
    attn = [_mm_nt(q, k) * d for q, k, d in zip(qn, kbd, dec)]
    sp = [st[sl, :] for sl in sls]
    vn = [a - _mm(b, s) for a, b, s in zip(u, w, sp)]
    o = [_mm(q * e, s) + _mm(a, _bd(v)) for q, e, s, a, v in zip(qn, eg, sp, attn, vn)]
    new = [s * d + jnp.where(diag_blocks, _mm_tn(k * e, v), 0.0) for s, d, k, e, v in zip(sp, cd, kn, et, vn)]
    outs = []
    for p in pairs:
        for hh in range(2):
            osl = slice(128 * hh, 128 * hh + 128)
            zsl = slice(256 * p + 128 * hh, 256 * p + 128 * hh + 128)
            outs.append(_unit_rms(o[p][:, osl]) * nw * _silu(z[:, zsl]))
    return (jnp.concatenate(outs, axis=1),), [qr[CHUNK - 8:, :], kr[CHUNK - 8:, :], vr[CHUNK - 8:, :],
                                             jnp.concatenate(new, axis=0)]


def _f_s5(tabs, consts, xs, xtabs, states):
    lam_re, lam_im, bblk, c_re, c_im, dskip, wglu, bglu = consts
    (u,) = xs
    s_re, s_im = states
    rows = u.shape[0]
    n = lam_re.shape[1]
    bu = _mm(u, bblk)
    hr, hi = bu[:, 0:n], bu[:, n:2 * n]
    row = _iota((rows, n), 0)
    h0r, h0i = s_re[0:1, :], s_im[0:1, :]
    hr = hr + jnp.where(row == 0, lam_re * h0r - lam_im * h0i, 0.0)
    hi = hi + jnp.where(row == 0, lam_re * h0i + lam_im * h0r, 0.0)
    pr, pi = lam_re, lam_im
    d = 1
    while d < rows:
        sr = jnp.where(row >= d, _shift(d, 0)(hr), 0.0)
        si = jnp.where(row >= d, _shift(d, 0)(hi), 0.0)
        hr, hi = hr + pr * sr - pi * si, hi + pr * si + pi * sr
        pr, pi = pr * pr - pi * pi, 2.0 * pr * pi
        d *= 2
    y = _mm(hr, c_re) - _mm(hi, c_im) + dskip * u
    y = jax.nn.gelu(y)
    out = y * jax.nn.sigmoid(_mm(y, wglu) + bglu)
    last_r = jnp.broadcast_to(hr[rows - 1:rows, :], (8, n))
    last_i = jnp.broadcast_to(hi[rows - 1:rows, :], (8, n))
    return (out,), [last_r, last_i]


def _full_spec(a):
    nd = a.ndim
    return pl.BlockSpec(a.shape, lambda i, _nd=nd: (0,) * _nd)


CHUNKS_PER_STEP = 4


def _chunks_per_step(f, rows, n):
    def g(tabs, consts, xs, xtabs, states):
        ys = []
        for i in range(n):
            sl = slice(rows * i, rows * (i + 1))
            (y,), states = f(tabs, consts, [t[sl] for t in xs], [t[sl] for t in xtabs], states)
            ys.append(y)
        return (jnp.concatenate(ys, axis=0),), states

    return g


def _scan_fwd(name, f, rows, tabs, consts, xs, xtabs, state_shapes, y_total, y_width, y_cb, y_alias=None):
    seq = xs[0][0].shape[0]
    per_step = math.gcd(CHUNKS_PER_STEP, seq // rows)
    f = _chunks_per_step(f, rows, per_step)
    rows = rows * per_step
    nc = seq // rows
    nt, ncst, nx, nxt, ns = len(tabs), len(consts), len(xs), len(xtabs), len(state_shapes)
    alias = y_alias is not None

    def body(*refs):
        p = 0
        tab_r = refs[p:p + nt]; p += nt
        c_r = refs[p:p + ncst]; p += ncst
        x_r = refs[p:p + nx]; p += nx
        xt_r = refs[p:p + nxt]; p += nxt
        if alias:
            p += 1
        y_ref = refs[p]; p += 1
        sv_r = refs[p:p + ns]; p += ns
        st_r = refs[p:p + ns]

        @pl.when(pl.program_id(0) == 0)
        def _():
            for s in st_r:
                s[...] = jnp.zeros(s.shape, F32)

        st = [s[...] for s in st_r]
        for r, v in zip(sv_r, st):
            r[...] = v
        (y,), new = f([r[...] for r in tab_r], [r[...] for r in c_r], [r[...].astype(F32) for r in x_r],
                      [r[...] for r in xt_r], st)
        y_ref[...] = y.astype(y_ref.dtype)
        for s, v in zip(st_r, new):
            s[...] = v

    win = [pl.BlockSpec((rows, w), lambda i, _cb=cb: (i, _cb)) for (_, w, cb) in list(xs) + list(xtabs)]
    in_specs = [_full_spec(a) for a in list(tabs) + list(consts)] + win
    args = list(tabs) + list(consts) + [a for (a, _, _) in list(xs) + list(xtabs)]
    io_alias = {}
    if alias:
        in_specs.append(pl.BlockSpec(memory_space=pl.ANY))
        io_alias = {len(args): 0}
        args.append(y_alias)
    out_shape = [jax.ShapeDtypeStruct((seq, y_total), _MXU_DTYPE)]
    out_specs = [pl.BlockSpec((rows, y_width), lambda i: (i, y_cb))]
    for (r, c) in state_shapes:
        out_shape.append(jax.ShapeDtypeStruct((nc * r, c), F32))
        out_specs.append(pl.BlockSpec((r, c), lambda i: (i, 0)))
    res = pl.pallas_call(
        body, name=name, grid=(nc,), in_specs=in_specs, out_specs=out_specs, out_shape=out_shape,
        scratch_shapes=[pltpu.VMEM(s, F32) for s in state_shapes], input_output_aliases=io_alias,
        compiler_params=pltpu.CompilerParams(dimension_semantics=("arbitrary",), vmem_limit_bytes=VMEM_LIMIT),
    )(*args)
    return res[0], list(res[1:])


def _scan_bwd(name, f, rows, tabs, consts, xs, xtabs, saved, state_shapes, dy, dx_total, dx_width, dx_cb,
              assemble, dx_alias=None):
    seq = xs[0][0].shape[0]
    per_step = math.gcd(CHUNKS_PER_STEP, seq // rows)
    f = _chunks_per_step(f, rows, per_step)
    rows = rows * per_step
    nc = seq // rows
    nt, ncst, nx, nxt, ns = len(tabs), len(consts), len(xs), len(xtabs), len(state_shapes)
    alias = dx_alias is not None

    def body(*refs):
        p = 0
        tab_r = refs[p:p + nt]; p += nt
        c_r = refs[p:p + ncst]; p += ncst
        x_r = refs[p:p + nx]; p += nx
        xt_r = refs[p:p + nxt]; p += nxt
        sv_r = refs[p:p + ns]; p += ns
        dy_ref = refs[p]; p += 1
        if alias:
            p += 1
        dx_ref = refs[p]; p += 1
        dc_r = refs[p:p + ncst]; p += ncst
        ds_r = refs[p:p + ns]

        @pl.when(pl.program_id(0) == 0)
        def _():
            for s in ds_r:
                s[...] = jnp.zeros(s.shape, F32)
            for r in dc_r:
                r[...] = jnp.zeros(r.shape, F32)

        tab_v = [r[...] for r in tab_r]
        xt_v = [r[...] for r in xt_r]

        def g(c, x, s):
            (y,), new = f(tab_v, c, x, xt_v, s)
            return y, new

        _, vjp = jax.vjp(g, [r[...] for r in c_r], [r[...].astype(F32) for r in x_r], [r[...] for r in sv_r])
        dc, dx, ds = vjp((dy_ref[...], [s[...] for s in ds_r]))
        dx_ref[...] = assemble(dx).astype(dx_ref.dtype)
        for r, v in zip(dc_r, dc):
            r[...] += v
        for s, v in zip(ds_r, ds):
            s[...] = v

    win = [pl.BlockSpec((rows, w), lambda j, _cb=cb: (nc - 1 - j, _cb)) for (_, w, cb) in list(xs) + list(xtabs)]
    in_specs = [_full_spec(a) for a in list(tabs) + list(consts)] + win
    args = list(tabs) + list(consts) + [a for (a, _, _) in list(xs) + list(xtabs)]
    for (r, c), sv in zip(state_shapes, saved):
        in_specs.append(pl.BlockSpec((r, c), lambda j: (nc - 1 - j, 0)))
        args.append(sv)
    in_specs.append(pl.BlockSpec((rows, dy[1]), lambda j: (nc - 1 - j, dy[2])))
    args.append(dy[0])
    io_alias = {}
    if alias:
        in_specs.append(pl.BlockSpec(memory_space=pl.ANY))
        io_alias = {len(args): 0}
        args.append(dx_alias)
    out_shape = [jax.ShapeDtypeStruct((seq, dx_total), _MXU_DTYPE)] +[jax.ShapeDtypeStruct(a.shape, F32) for a in consts]
    out_specs = [pl.BlockSpec((rows, dx_width), lambda j: (nc - 1 - j, dx_cb))] + [_full_spec(a) for a in consts]
    res = pl.pallas_call(
        body, name=name, grid=(nc,), in_specs=in_specs, out_specs=out_specs, out_shape=out_shape,
        scratch_shapes=[pltpu.VMEM(s, F32) for s in state_shapes], input_output_aliases=io_alias,
        compiler_params=pltpu.CompilerParams(dimension_semantics=("arbitrary",), vmem_limit_bytes=VMEM_LIMIT),
    )(*args)
    return res[0], list(res[1:])


def _tile(n, want):
    t = min(n, want)
    while n % t:
        t //= 2
    return t


MATMUL_VMEM_BUDGET = 40 * 1024 * 1024


def _pick_tiles(m, n, k, sa, sb, so, se, whole_rows=False):
    best = None
    for tn in ({n} if whole_rows else {_tile(n, 1024), _tile(n, 512)}):
        for tm in {_tile(m, t) for t in (2048, 1024, 512)}:
            for tk in {_tile(k, t) for t in (4096, 2048, 1024, 512)}:
                at, bt, ot = tm * tk * sa, tk * tn * sb, tm * tn * so
                need = 2 * (at + bt + ot + tm * tn * se) + 2 * tm * tn * 4 + (at if sa == 4 else 0) + (bt if sb == 4 else 0)
                if need > MATMUL_VMEM_BUDGET:
                    continue
                key = ((m // tm) * (n // tn) * (k // tk), k // tk, -tm, -tn)
                if best is None or key < best[0]:
                    best = (key, (tm, tn, tk))
    assert best is not None, (m, n, k)
    return best[1]


def _matmul(name, a, b, mode, out_dtype=F32, a_pro=None, epi=None, epi_arr=None, norm_w=None, norm_x=None):
    if mode == "nn":
        (m, k), (k2, n) = a.shape, b.shape
    elif mode == "nt":
        (m, k), (n, k2) = a.shape, b.shape
    else:
        (k, m), (k2, n) = a.shape, b.shape
    assert k == k2, (name, a.shape, b.shape)
    size = lambda t: jnp.dtype(t).itemsize
    rows_in = [] if epi is None else [epi_arr] + ([norm_x] if epi == "norm_bwd" else [])
    emit_norm = epi == "add" and norm_w is not None
    extra = sum(size(t.dtype) for t in rows_in) + (size(_MXU_DTYPE) if emit_norm else 0)
    tm, tn, tk = _pick_tiles(m, n, k, size(a.dtype), size(b.dtype), size(out_dtype), extra,
                             whole_rows=norm_w is not None)
    nk = k // tk
    ca, cb = {"nn": (1, 0), "nt": (1, 1), "tn": (0, 0)}[mode]
    n_in = 2 + len(rows_in) + (norm_w is not None)
    n_out = 2 if (emit_norm or epi == "norm_bwd") else 1

    def body(*refs):
        refs = list(refs)
        acc = refs.pop() if nk > 1 else None
        a_ref, b_ref = refs[0], refs[1]
        e_ref = refs[2] if epi is not None else None
        x_ref = refs[3] if epi == "norm_bwd" else None
        w_ref = refs[n_in - 1] if norm_w is not None else None
        o_ref = refs[n_in]
        o2_ref = refs[n_in + 1] if n_out == 2 else None
        kk = pl.program_id(2)

        if epi == "norm_bwd":
            @pl.when((pl.program_id(1) == 0) & (kk == 0))
            def _():
                o2_ref[...] = jnp.zeros(o2_ref.shape, F32)

        av = a_ref[...]
        if a_pro == "relu2":
            r = jnp.maximum(av, 0.0)
            av = r * r
        part = _dg(_lo(av), _lo(b_ref[...]), ca, cb)

        def finish(r):
            if epi == "add":
                r = r + e_ref[...]
                if emit_norm:
                    o2_ref[...] = (_unit_rms(r) * w_ref[...]).astype(_MXU_DTYPE)
            elif epi == "drelu2":
                r = r * (2.0 * jnp.maximum(e_ref[...], 0.0))
            elif epi == "norm_bwd":
                xv = x_ref[...]
                rstd = lax.rsqrt(jnp.mean(xv * xv, axis=-1, keepdims=True) + EPS)
                xh = xv * rstd
                g = r * w_ref[...]
                o2_ref[...] += jnp.sum(r * xh, axis=0, keepdims=True)
                r = e_ref[...] + rstd * (g - xh * jnp.mean(g * xh, axis=-1, keepdims=True))
            o_ref[...] = r.astype(out_dtype)

        if nk == 1:
            finish(part)
        else:
            @pl.when(kk == 0)
            def _():
                acc[...] = part

            @pl.when(kk > 0)
            def _():
                acc[...] += part

            @pl.when(kk == nk - 1)
            def _():
                finish(acc[...])

    if mode == "tn":
        a_spec = pl.BlockSpec((tk, tm), lambda j, i, kk: (kk, i))
    else:
        a_spec = pl.BlockSpec((tm, tk), lambda j, i, kk: (i, kk))
    if mode == "nt":
        b_spec = pl.BlockSpec((tn, tk), lambda j, i, kk: (j, kk))
    else:
        b_spec = pl.BlockSpec((tk, tn), lambda j, i, kk: (kk, j))
    o_spec = pl.BlockSpec((tm, tn), lambda j, i, kk: (i, j))
    vec_spec = pl.BlockSpec((1, tn), lambda j, i, kk: (0, j))
    in_specs, args = [a_spec, b_spec] + [o_spec] * len(rows_in), [a, b] + rows_in
    if norm_w is not None:
        in_specs.append(vec_spec)
        args.append(norm_w)
    out_specs, out_shape = [o_spec], [jax.ShapeDtypeStruct((m, n), out_dtype)]
    if emit_norm:
        out_specs.append(o_spec)
        out_shape.append(jax.ShapeDtypeStruct((m, n), _MXU_DTYPE))
    elif epi == "norm_bwd":
        out_specs.append(vec_spec)
        out_shape.append(jax.ShapeDtypeStruct((1, n), F32))
    sem = ("parallel", "arbitrary" if epi == "norm_bwd" else "parallel", "arbitrary")
    res = pl.pallas_call(
        body, name=name, grid=(n // tn, m // tm, nk), in_specs=in_specs, out_specs=out_specs, out_shape=out_shape,
        scratch_shapes=[pltpu.VMEM((tm, tn), F32)] if nk > 1 else [],
        compiler_params=pltpu.CompilerParams(dimension_semantics=sem, vmem_limit_bytes=VMEM_LIMIT),
    )(*args)
    return res[0] if n_out == 1 else res


ROW_TILE = 512


def _rmsnorm_fwd(name, x, w):
    seq, d = x.shape
    tr = _tile(seq, ROW_TILE)

    def body(x_ref, w_ref, o_ref):
        xv = x_ref[...]
        o_ref[...] = (_unit_rms(xv) * w_ref[...]).astype(_MXU_DTYPE)

    return pl.pallas_call(
        body, name=name, grid=(seq // tr,),
        in_specs=[pl.BlockSpec((tr, d), lambda i: (i, 0)), pl.BlockSpec((1, d), lambda i: (0, 0))],
        out_specs=pl.BlockSpec((tr, d), lambda i: (i, 0)), out_shape=jax.ShapeDtypeStruct((seq, d), _MXU_DTYPE),
        compiler_params=pltpu.CompilerParams(dimension_semantics=("parallel",), vmem_limit_bytes=VMEM_LIMIT),
    )(x, w)


def _loss_head(name, x, w, target):
    seq, d = x.shape
    tr = _tile(seq, ROW_TILE)

    def body(x_ref, w_ref, t_ref, loss_ref, dx_ref, dw_ref):
        @pl.when(pl.program_id(0) == 0)
        def _():
            dw_ref[...] = jnp.zeros(dw_ref.shape, F32)
            loss_ref[...] = jnp.zeros(loss_ref.shape, F32)

        xv = x_ref[...]
        rstd = lax.rsqrt(jnp.mean(xv * xv, axis=-1, keepdims=True) + EPS)
        xh = xv * rstd
        err = xh * w_ref[...] - t_ref[...]
        per_row = jnp.mean(err * err, axis=-1, keepdims=True)
        loss_ref[...] += 0.5 * jnp.sum(per_row, axis=0, keepdims=True)
        dy = err * (1.0 / d)
        g = dy * w_ref[...]
        dx_ref[...] = rstd * (g - xh * jnp.mean(g * xh, axis=-1, keepdims=True))
        dw_ref[...] += jnp.sum(dy * xh, axis=0, keepdims=True)

    row = pl.BlockSpec((tr, d), lambda i: (i, 0))
    vec = pl.BlockSpec((1, d), lambda i: (0, 0))
    one = pl.BlockSpec((1, 1), lambda i: (0, 0))
    return pl.pallas_call(
        body, name=name, grid=(seq // tr,), in_specs=[row, vec, row], out_specs=[one, row, vec],
        out_shape=[jax.ShapeDtypeStruct((1, 1), F32), jax.ShapeDtypeStruct((seq, d), F32),
                   jax.ShapeDtypeStruct((1, d), F32)],
        compiler_params=pltpu.CompilerParams(dimension_semantics=("arbitrary",), vmem_limit_bytes=VMEM_LIMIT),
    )(x, w, target)


SLAB_TILE_ROWS = 1024


def _slab_tile(rows, cap=SLAB_TILE_ROWS):
    step = 16 if rows % 16 == 0 else 8
    return max(t for t in range(step, min(rows, cap) + 1, step) if rows % t == 0)


def _adamw(name, w, g, m, v):
    rows, cols = w.shape
    tr = _slab_tile(rows, SLAB_TILE_ROWS // 2) if rows % 8 == 0 else rows

    def body(w_ref, g_ref, m_ref, v_ref, d_ref, nm_ref, nv_ref):
        gv = g_ref[...]
        nm = ADAM_B1 * m_ref[...] + (1.0 - ADAM_B1) * gv
        nv = ADAM_B2 * v_ref[...] + (1.0 - ADAM_B2) * (gv * gv)
        m_hat = nm / (1.0 - ADAM_B1 ** ADAM_STEP)
        v_hat = nv / (1.0 - ADAM_B2 ** ADAM_STEP)
        d_ref[...] = -ADAM_LR * (m_hat / (jnp.sqrt(v_hat) + ADAM_EPS) + ADAM_WD * w_ref[...])
        nm_ref[...] = nm
        nv_ref[...] = nv

    spec = pl.BlockSpec((tr, cols), lambda i: (i, 0))
    sds = jax.ShapeDtypeStruct(w.shape, F32)
    return pl.pallas_call(
        body, name=name, grid=(rows // tr,), in_specs=[spec] * 4, out_specs=[spec] * 3, out_shape=[sds] * 3,
        compiler_params=pltpu.CompilerParams(dimension_semantics=("parallel",), vmem_limit_bytes=VMEM_LIMIT),
    )(w, g, m, v)


WIRE_DTYPE = jnp.bfloat16


def _add_halves(name, g, t1, c):
    nsec, rows, _ = g.shape
    rh = rows // 2
    tr = _slab_tile(rh)
    nb = rh // tr

    def body(c_ref, g_ref, t_ref, o_ref):
        o_ref[...] = (g_ref[...] + t_ref[...]).astype(o_ref.dtype)

    gs = pltpu.PrefetchScalarGridSpec(
        num_scalar_prefetch=1, grid=(nsec, nb),
        in_specs=[pl.BlockSpec((1, tr, LANES), lambda s, i, c_ref: (s, c_ref[0] * nb + i, 0)),
                  pl.BlockSpec((1, tr, LANES), lambda s, i, c_ref: (s, i, 0))],
        out_specs=pl.BlockSpec((1, tr, LANES), lambda s, i, c_ref: (s, i, 0)))
    return pl.pallas_call(
        body, name=name, grid_spec=gs, out_shape=jax.ShapeDtypeStruct((nsec, rh, LANES), WIRE_DTYPE),
        compiler_params=pltpu.CompilerParams(dimension_semantics=("parallel", "parallel"),
                                             vmem_limit_bytes=VMEM_LIMIT),
    )(c, g, t1)


ANY = pl.BlockSpec(memory_space=pl.ANY)


def _place():
    return lax.axis_index("x"), lax.axis_index("y"), lax.axis_index("c")


def _all_gather_shards(name, slab):
    rows = slab.shape[0]
    rh = rows // 2
    rq = rh // 2

    def body(x_ref, out_ref, send_sems, recv_sems):
        x, y, c = _place()
        me, sibling = (x, y, c), (x, y, 1 - c)
        xn, yn, dg = (1 - x, y), (x, 1 - y), (1 - x, 1 - y)

        def piece(chip, core, q):
            return out_ref.at[2 * chip[0] + chip[1], pl.ds(core * rh + q * rq, rq), :]

        def copy(k, chip, core, q, to, src=None):
            return pltpu.make_async_remote_copy(
                src_ref=piece(chip, core, q) if src is None else src, dst_ref=piece(chip, core, q),
                send_sem=send_sems.at[k], recv_sem=recv_sems.at[k], device_id=to, device_id_type=MESH)

        own = [x_ref.at[pl.ds(c * rh + q * rq, rq), :] for q in range(2)]
        sends = [copy(0, (x, y), c, 0, (*xn, c), src=own[0]), copy(1, (x, y), c, 1, (*xn, c), src=own[1]),
                 copy(2, (x, y), c, 0, (*yn, c), src=own[0]), copy(3, (x, y), c, 1, (*yn, c), src=own[1])]
        for cp in sends:
            cp.start()
        landed = [(0, xn, 0), (3, yn, 1), (1, xn, 1), (2, yn, 0), (4, dg, 0), (5, dg, 1)]
        onward = {0: (4, (*yn, c)), 3: (5, (*xn, c))}
        for i, (k, chip, q) in enumerate(landed):
            copy(k, chip, c, q, me).wait_recv()
            if k in onward:
                fk, to = onward[k]
                sends.append(copy(fk, chip, c, q, to))
                sends[-1].start()
            sends.append(copy(6 + i, chip, c, q, sibling))
            sends[-1].start()
        for i, (k, chip, q) in enumerate(landed):
            copy(6 + i, chip, 1 - c, q, me).wait_recv()
        for cp in sends:
            cp.wait_send()

    got = pl.pallas_call(
        body, name=name, in_specs=[ANY], out_specs=ANY,
        out_shape=jax.ShapeDtypeStruct((4, rows, LANES), slab.dtype),
        scratch_shapes=[pltpu.SemaphoreType.DMA((12,)), pltpu.SemaphoreType.DMA((12,))],
    )(slab)
    return lax.dynamic_update_slice(got, slab[None], (2 * lax.axis_index("x") + lax.axis_index("y"), 0, 0))


def _swap_halves(name, g):
    nsec, rows, _ = g.shape
    rh = rows // 2

    def body(g_ref, t_ref, send_sem, recv_sem):
        x, y, c = _place()
        cp = pltpu.make_async_remote_copy(
            src_ref=g_ref.at[:, pl.ds((1 - c) * rh, rh), :], dst_ref=t_ref, send_sem=send_sem, recv_sem=recv_sem,
            device_id=(x, y, 1 - c), device_id_type=MESH)
        cp.start()
        cp.wait()

    return pl.pallas_call(
        body, name=name, in_specs=[ANY], out_specs=ANY, out_shape=jax.ShapeDtypeStruct((nsec, rh, LANES), F32),
        scratch_shapes=[pltpu.SemaphoreType.DMA, pltpu.SemaphoreType.DMA],
    )(g)


def _exchange_stage1(name, p):
    _, rh, _ = p.shape
    rq = rh // 2

    def body(p_ref, fx_ref, fy_ref, send_sems, recv_sems):
        x, y, c = _place()
        to_x = pltpu.make_async_remote_copy(
            src_ref=p_ref.at[pl.ds(2 * (1 - x), 2), pl.ds(0, rq), :], dst_ref=fx_ref, send_sem=send_sems.at[0],
            recv_sem=recv_sems.at[0], device_id=(1 - x, y, c), device_id_type=MESH)
        to_y = [pltpu.make_async_remote_copy(
            src_ref=p_ref.at[2 * sx + (1 - y), pl.ds(rq, rq), :], dst_ref=fy_ref.at[sx], send_sem=send_sems.at[1 + sx],
            recv_sem=recv_sems.at[1 + sx], device_id=(x, 1 - y, c), device_id_type=MESH) for sx in range(2)]
        for cp in [to_x] + to_y:
            cp.start()
        for cp in [to_x] + to_y:
            cp.wait_recv()
        for cp in [to_x] + to_y:
            cp.wait_send()

    sds = jax.ShapeDtypeStruct((2, rq, LANES), p.dtype)
    return pl.pallas_call(
        body, name=name, in_specs=[ANY], out_specs=[ANY, ANY], out_shape=[sds, sds],
        scratch_shapes=[pltpu.SemaphoreType.DMA((3,)), pltpu.SemaphoreType.DMA((3,))],
    )(p)


def _exchange_add1(name, p, from_x, from_y, place):
    _, rh, _ = p.shape
    rq = rh // 2
    tr = _slab_tile(rq)
    nb = rq // tr

    def body(xy_ref, pa_s, pa_k, pb_s, pb_k, fx_s, fx_k, fy_s, fy_k, sa, ka, sb, kb):
        for mine, theirs, out in ((pa_s, fx_s, sa), (pa_k, fx_k, ka), (pb_s, fy_s, sb), (pb_k, fy_k, kb)):
            out[...] = (mine[0].astype(F32) + theirs[0].astype(F32)).astype(out.dtype)

    blk = lambda fn: pl.BlockSpec((1, tr, LANES), fn)
    gs = pltpu.PrefetchScalarGridSpec(
        num_scalar_prefetch=1, grid=(nb,),
        in_specs=[blk(lambda i, xy: (2 * xy[0] + 1 - xy[1], i, 0)), blk(lambda i, xy: (2 * xy[0] + xy[1], i, 0)),
                  blk(lambda i, xy: (2 * (1 - xy[0]) + xy[1], nb + i, 0)), blk(lambda i, xy: (2 * xy[0] + xy[1], nb + i, 0)),
                  blk(lambda i, xy: (1 - xy[1], i, 0)), blk(lambda i, xy: (xy[1], i, 0)),
                  blk(lambda i, xy: (1 - xy[0], i, 0)), blk(lambda i, xy: (xy[0], i, 0))],
        out_specs=[pl.BlockSpec((tr, LANES), lambda i, xy: (i, 0))] * 4)
    sds = jax.ShapeDtypeStruct((rq, LANES), p.dtype)
    return pl.pallas_call(
        body, name=name, grid_spec=gs, out_shape=[sds] * 4,
        compiler_params=pltpu.CompilerParams(dimension_semantics=("parallel",), vmem_limit_bytes=VMEM_LIMIT),
    )(place, p, p, p, p, from_x, from_x, from_y, from_y)


def _exchange_stage2(name, send_a, send_b):
    def body(a_ref, b_ref, fa_ref, fb_ref, send_sems, recv_sems):
        x, y, c = _place()
        cps = [pltpu.make_async_remote_copy(src_ref=a_ref, dst_ref=fa_ref, send_sem=send_sems.at[0],
                                            recv_sem=recv_sems.at[0], device_id=(x, 1 - y, c), device_id_type=MESH),
               pltpu.make_async_remote_copy(src_ref=b_ref, dst_ref=fb_ref, send_sem=send_sems.at[1],
                                            recv_sem=recv_sems.at[1], device_id=(1 - x, y, c), device_id_type=MESH)]
        for cp in cps:
            cp.start()
        for cp in cps:
            cp.wait_recv()
        for cp in cps:
            cp.wait_send()

    sds = jax.ShapeDtypeStruct(send_a.shape, send_a.dtype)
    return pl.pallas_call(
        body, name=name, in_specs=[ANY, ANY], out_specs=[ANY, ANY], out_shape=[sds, sds],
        scratch_shapes=[pltpu.SemaphoreType.DMA((2,)), pltpu.SemaphoreType.DMA((2,))],
    )(send_a, send_b)


def _exchange_add2(name, keep_a, got_a, keep_b, got_b, c):
    rq = keep_a.shape[0]
    tr = _slab_tile(rq)

    def body(c_ref, ka, ga, kb, gb, o_ref):
        o_ref[0] = ka[...].astype(F32) + ga[...].astype(F32)
        o_ref[1] = kb[...].astype(F32) + gb[...].astype(F32)

    spec = pl.BlockSpec((tr, LANES), lambda i, c_ref: (i, 0))
    gs = pltpu.PrefetchScalarGridSpec(
        num_scalar_prefetch=1, grid=(rq // tr,), in_specs=[spec] * 4,
        out_specs=pl.BlockSpec((2, tr, LANES), lambda i, c_ref: (c_ref[0], i, 0)))
    out = pl.pallas_call(
        body, name=name, grid_spec=gs, out_shape=jax.ShapeDtypeStruct((4, rq, LANES), F32),
        compiler_params=pltpu.CompilerParams(dimension_semantics=("parallel",), vmem_limit_bytes=VMEM_LIMIT),
    )(c, keep_a, got_a, keep_b, got_b)
    return out.reshape(4 * rq, LANES)


def _join_halves(name, full):
    rh = full.shape[0] // 2

    def body(in_ref, o_ref, send_sem, recv_sem):
        x, y, c = _place()
        cp = pltpu.make_async_remote_copy(
            src_ref=in_ref.at[pl.ds(c * rh, rh), :], dst_ref=o_ref.at[pl.ds(c * rh, rh), :], send_sem=send_sem,
            recv_sem=recv_sem, device_id=(x, y, 1 - c), device_id_type=MESH)
        cp.start()
        pltpu.make_async_remote_copy(
            src_ref=in_ref.at[pl.ds(c * rh, rh), :], dst_ref=o_ref.at[pl.ds((1 - c) * rh, rh), :], send_sem=send_sem,
            recv_sem=recv_sem, device_id=(x, y, 1 - c), device_id_type=MESH).wait_recv()
        cp.wait_send()

    return pl.pallas_call(
        body, name=name, in_specs=[ANY], out_specs=ANY, out_shape=jax.ShapeDtypeStruct(full.shape, full.dtype),
        input_output_aliases={0: 0}, scratch_shapes=[pltpu.SemaphoreType.DMA, pltpu.SemaphoreType.DMA],
    )(full)


def _rows_of(n):
    return -(-n // LANES)


SLAB_ROW_ALIGN = 512


def _flat_rows(arrays, dtype):
    parts = []
    for a in arrays:
        flat = a.reshape(-1).astype(dtype)
        parts.append(jnp.pad(flat, (0, _rows_of(flat.size) * LANES - flat.size)))
    return jnp.concatenate(parts).reshape(-1, LANES)


def _align_rows(slab):
    rows = slab.shape[0]
    return jnp.pad(slab, ((0, -(-rows // SLAB_ROW_ALIGN) * SLAB_ROW_ALIGN - rows), (0, 0)))


def _pack(arrays, dtype):
    return _align_rows(_flat_rows(arrays, dtype))


def _unpack(slab, shapes):
    out, r = [], 0
    for shp in shapes:
        n = math.prod(shp)
        out.append(slab[r:r + _rows_of(n)].reshape(-1)[:n].reshape(shp))
        r += _rows_of(n)
    return out


def _unpack_gathered(g, shapes, kinds):
    out, r = [], 0
    for shp, kind in zip(shapes, kinds):
        n = math.prod(shp)
        blk = g[:, r:r + _rows_of(n)].reshape(4, -1)[:, :n].reshape((4,) + tuple(shp))
        r += _rows_of(n)
        if kind == "col":
            out.append(jnp.moveaxis(blk, 0, 1).reshape(shp[0], 4 * shp[1]))
        else:
            out.append(blk.reshape(4 * shp[0], shp[1]))
    return out


def _shard_block(g, kind, s, local_shape):
    if kind == "col":
        return g[:, s * local_shape[1]:(s + 1) * local_shape[1]]
    if kind == "row":
        return g[s * local_shape[0]:(s + 1) * local_shape[0]]
    return g


def _rotary_tables(seq):
    half = RET_DK // 2
    pos = jnp.arange(seq, dtype=F32)
    inv = ROPE_THETA ** (-jnp.arange(half, dtype=F32) / half)
    ang = pos[:, None] * inv[None, :]
    cos, sin = jnp.cos(ang), jnp.sin(ang)
    return jnp.concatenate([cos, cos], axis=1), jnp.concatenate([-sin, sin], axis=1)


def _retention_tables():
    log_gamma = jnp.log(1.0 - 2.0 ** (-5.0 - jnp.arange(RET_HEADS, dtype=F32)))
    idx = jnp.arange(CHUNK, dtype=F32)
    diff = idx[:, None] - idx[None, :]
    dmask = jnp.exp(jnp.where((diff >= 0)[None], log_gamma[:, None, None] * diff[None], -jnp.inf))
    kdec = jnp.exp(log_gamma[None, :] * (CHUNK - 1.0 - idx)[:, None])
    qdec = jnp.exp(log_gamma[None, :] * (idx + 1.0)[:, None])
    cdec = jnp.exp(log_gamma * CHUNK)[None, :]
    lanes = lambda t: jnp.repeat(t, RET_DK, axis=1)
    return dmask.reshape(RET_HEADS * CHUNK, CHUNK), lanes(kdec), lanes(qdec), lanes(cdec)


def _s5_prep(a_re, a_im, log_step, b_re, b_im, c_re, c_im):
    g, n, c = S5_GROUPS, S5_STATE, S5_GROUP
    lam = lax.complex(a_re, a_im)
    step = jnp.exp(log_step)[:, None]
    lam_bar = jnp.exp(lam * step)
    b_bar = ((lam_bar - 1.0) / lam)[..., None] * lax.complex(b_re, b_im)
    eye = jnp.eye(g, dtype=F32)
    bb_re = (jnp.real(b_bar).transpose(0, 2, 1)[:, :, None, :] * eye[:, None, :, None]).reshape(g * c, g * n)
    bb_im = (jnp.imag(b_bar).transpose(0, 2, 1)[:, :, None, :] * eye[:, None, :, None]).reshape(g * c, g * n)
    cc_re = (c_re.transpose(0, 2, 1)[:, :, None, :] * eye[:, None, :, None]).reshape(g * n, g * c)
    cc_im = (c_im.transpose(0, 2, 1)[:, :, None, :] * eye[:, None, :, None]).reshape(g * n, g * c)
    return (jnp.real(lam_bar).reshape(1, g * n), jnp.imag(lam_bar).reshape(1, g * n),
            jnp.concatenate([bb_re, bb_im], axis=1), cc_re, cc_im)


def kernel(x, l0_norm_mix, l0_w_in, ssd_conv_w, ssd_conv_b, ssd_dt_bias, ssd_A_log, ssd_D, ssd_norm_w, l0_w_out, l0_norm_mlp, l0_w_up, l0_w_down, l1_norm_mix, l1_w_in, gdn_conv_w, gdn_A_log, gdn_dt_bias, gdn_norm_w, s5_A_re, s5_A_im, s5_log_step, s5_B_re, s5_B_im, s5_C_re, s5_C_im, s5_D, s5_w_glu, s5_b_glu, l1_w_out, l1_norm_mlp, l1_w_up, l1_w_down, final_norm, loss_target, m_l0_norm_mix, m_l0_w_in, m_ssd_conv_w, m_ssd_conv_b, m_ssd_dt_bias, m_ssd_A_log, m_ssd_D, m_ssd_norm_w, m_l0_w_out, m_l0_norm_mlp, m_l0_w_up, m_l0_w_down, m_l1_norm_mix, m_l1_w_in, m_gdn_conv_w, m_gdn_A_log, m_gdn_dt_bias, m_gdn_norm_w, m_s5_A_re, m_s5_A_im, m_s5_log_step, m_s5_B_re, m_s5_B_im, m_s5_C_re, m_s5_C_im, m_s5_D, m_s5_w_glu, m_s5_b_glu, m_l1_w_out, m_l1_norm_mlp, m_l1_w_up, m_l1_w_down, m_final_norm, v_l0_norm_mix, v_l0_w_in, v_ssd_conv_w, v_ssd_conv_b, v_ssd_dt_bias, v_ssd_A_log, v_ssd_D, v_ssd_norm_w, v_l0_w_out, v_l0_norm_mlp, v_l0_w_up, v_l0_w_down, v_l1_norm_mix, v_l1_w_in, v_gdn_conv_w, v_gdn_A_log, v_gdn_dt_bias, v_gdn_norm_w, v_s5_A_re, v_s5_A_im, v_s5_log_step, v_s5_B_re, v_s5_B_im, v_s5_C_re, v_s5_C_im, v_s5_D, v_s5_w_glu, v_s5_b_glu, v_l1_w_out, v_l1_norm_mlp, v_l1_w_up, v_l1_w_down, v_final_norm):
    given = dict(locals())
    names = [n for n, _ in PARAMS]
    kinds = dict(PARAMS)
    w = {n: given[n] for n in names}
    seq = x.shape[1]
    x0 = x.reshape(seq, D_MODEL)
    target = loss_target.reshape(seq, D_MODEL)

    gb = _all_gather_shards("gather_weights", _pack([w[n] for n in GATHER_BF16], _MXU_DTYPE))
    full = dict(zip(GATHER_BF16, _unpack_gathered(gb, [w[n].shape for n in GATHER_BF16],
                                                  [kinds[n] for n in GATHER_BF16])))
    gf = _all_gather_shards("gather_conv", _pack([w[n] for n in GATHER_F32], F32))
    full.update(zip(GATHER_F32, _unpack_gathered(gf, [w[n].shape for n in GATHER_F32],
                                                 [kinds[n] for n in GATHER_F32])))
    in0 = full["l0_w_in"].shape[1]
    w_in0 = jnp.pad(full["l0_w_in"], ((0, 0), (0, IN0_PAD - in0)))
    wi1 = full["l1_w_in"]
    in1 = wi1.shape[1]
    w_in1 = jnp.concatenate([wi1[:, :3072], wi1[:, 3084:in1], wi1[:, 3072:3084],
                             jnp.zeros((D_MODEL, IN1_PAD - in1), wi1.dtype)], axis=1)

    row = lambda a: a.reshape(1, -1)
    lanes64 = lambda a: jnp.repeat(a, SSD_HEAD_DIM).reshape(1, -1)

    h0 = _rmsnorm_fwd("norm_mix0", x0, row(w["l0_norm_mix"]))
    proj0 = _matmul("in_proj0", h0, w_in0, "nn")
    cos_t, sin_t = _rotary_tables(seq)
    ret_tabs = list(_retention_tables())
    ret_xs = [(proj0, 512, 0), (proj0, 512, 1), (proj0, 512, 2), (proj0, 512, 3)]
    ret_xt = [(cos_t, 128, 0), (sin_t, 128, 0)]
    ret_states = [(512, 128)]
    mixed0, ret_saved = _scan_fwd("ret_fwd", _f_ret, CHUNK, ret_tabs, [], ret_xs, ret_xt, ret_states, D_MODEL, 512, 0)
    expand = jnp.repeat(jnp.eye(128, SSD_HEADS, dtype=F32), SSD_HEAD_DIM, axis=1)
    ssd_consts = [full["ssd_conv_w"], row(w["ssd_conv_b"]), lanes64(w["ssd_dt_bias"]), lanes64(w["ssd_A_log"]),
                  lanes64(w["ssd_D"]), row(w["ssd_norm_w"])]
    ssd_xs = [(proj0, 512, 4), (proj0, 512, 5), (proj0, 256, 12), (proj0, 256, 13), (proj0, 128, 28)]
    ssd_states = [(8, 512), (8, 256), (8, 256), (512, 128)]
    mixed0, ssd_saved = _scan_fwd("ssd_fwd", _f_ssd, CHUNK, [expand], ssd_consts, ssd_xs, [], ssd_states,
                                  D_MODEL, 512, 1, y_alias=mixed0)
    x1, h1 = _matmul("out_proj0", mixed0, full["l0_w_out"], "nn", epi="add", epi_arr=x0, norm_w=row(w["l0_norm_mlp"]))
    u0 = _matmul("up0", h1, full["l0_w_up"], "nn", out_dtype=_MXU_DTYPE)
    x2, h2 = _matmul("down0", u0, full["l0_w_down"], "nn", a_pro="relu2", epi="add", epi_arr=x1,
                     norm_w=row(w["l1_norm_mix"]))

    proj1 = _matmul("in_proj1", h2, w_in1, "nn")
    p_alog = jnp.zeros((1, 128), F32).at[0, 6:12].set(w["gdn_A_log"])
    p_dtb = jnp.zeros((1, 128), F32).at[0, 6:12].set(w["gdn_dt_bias"])
    gdn_consts = [full["gdn_conv_w"], p_alog, p_dtb, row(w["gdn_norm_w"])]
    gdn_xs = [(proj1, 768, 0), (proj1, 768, 1), (proj1, 768, 2), (proj1, 768, 3), (proj1, 128, 26)]
    gdn_states = [(8, 768), (8, 768), (8, 768), (768, 256)]
    mixed1, gdn_saved = _scan_fwd("gdn_fwd", _f_gdn, CHUNK, [], gdn_consts, gdn_xs, [], gdn_states, D_MODEL, 768, 0)
    s5_args = (w["s5_A_re"], w["s5_A_im"], w["s5_log_step"], w["s5_B_re"], w["s5_B_im"], w["s5_C_re"], w["s5_C_im"])
    (lam_re, lam_im, bblk, cc_re, cc_im), s5_prep_vjp = jax.vjp(_s5_prep, *s5_args)
    s5_consts = [lam_re, lam_im, bblk, cc_re, cc_im, row(w["s5_D"]), full["s5_w_glu"].astype(F32), row(w["s5_b_glu"])]
    s5_xs = [(proj1, 256, 12)]
    s5_states = [(8, 1024), (8, 1024)]
    mixed1, s5_saved = _scan_fwd("s5_fwd", _f_s5, CHUNK, [], s5_consts, s5_xs, [], s5_states, D_MODEL, 256, 3,
                                 y_alias=mixed1)
    x3, h3 = _matmul("out_proj1", mixed1, full["l1_w_out"], "nn", epi="add", epi_arr=x2, norm_w=row(w["l1_norm_mlp"]))
    u1 = _matmul("up1", h3, full["l1_w_up"], "nn", out_dtype=_MXU_DTYPE)
    x4 = _matmul("down1", u1, full["l1_w_down"], "nn", a_pro="relu2", epi="add", epi_arr=x3)

    loss_part, dx4, d_final = _loss_head("loss_head", x4, row(w["final_norm"]), target)
    loss = lax.psum(loss_part[0, 0], ("x", "y", "c"))
    grads = {"final_norm": d_final.reshape(-1)}

    du1 = _matmul("down1_dx", dx4, full["l1_w_down"], "nt", out_dtype=_MXU_DTYPE, epi="drelu2", epi_arr=u1)
    grads["l1_w_down"] = _matmul("down1_dw", u1, dx4, "tn", a_pro="relu2")
    grads["l1_w_up"] = _matmul("up1_dw", h3, du1, "tn")
    dx3, dwn = _matmul("up1_dx", du1, full["l1_w_up"], "nt", epi="norm_bwd", epi_arr=dx4, norm_x=x3,
                       norm_w=row(w["l1_norm_mlp"]))
    grads["l1_norm_mlp"] = dwn.reshape(-1)
    grads["l1_w_out"] = _matmul("out_proj1_dw", mixed1, dx3, "tn")
    dmixed1 = _matmul("out_proj1_dx", dx3, full["l1_w_out"], "nt")

    def gdn_assemble(dx):
        dq, dk, dv, dz, dba = dx
        zeros = lambda n: jnp.zeros((dq.shape[0], n), F32)
        return jnp.concatenate([dq, dk, dv, dz, zeros(256), dba, zeros(IN1_PAD - 3456)], axis=1)

    dproj1, gdn_dc = _scan_bwd("gdn_bwd", _f_gdn, CHUNK, [], gdn_consts, gdn_xs, [], gdn_saved, gdn_states,
                               (dmixed1, 768, 0), IN1_PAD, IN1_PAD, 0, gdn_assemble)
    dproj1, s5_dc = _scan_bwd("s5_bwd", _f_s5, CHUNK, [], s5_consts, s5_xs, [], s5_saved, s5_states,
                              (dmixed1, 256, 3), IN1_PAD, 256, 12, lambda dx: dx[0], dx_alias=dproj1)
    grads["gdn_conv_w"] = gdn_dc[0]
    grads["gdn_A_log"] = gdn_dc[1][0, 6:12]
    grads["gdn_dt_bias"] = gdn_dc[2][0, 6:12]
    grads["gdn_norm_w"] = gdn_dc[3].reshape(-1)
    s5_pg = s5_prep_vjp(tuple(s5_dc[:5]))
    for n, gval in zip(("s5_A_re", "s5_A_im", "s5_log_step", "s5_B_re", "s5_B_im", "s5_C_re", "s5_C_im"), s5_pg):
        grads[n] = gval
    grads["s5_D"] = s5_dc[5].reshape(-1)
    grads["s5_w_glu"] = s5_dc[6]
    grads["s5_b_glu"] = s5_dc[7].reshape(-1)
    dwi1 = _matmul("in_proj1_dw", h2, dproj1, "tn")
    grads["l1_w_in"] = jnp.concatenate([dwi1[:, :3072], dwi1[:, 3328:3340], dwi1[:, 3072:3328]], axis=1)
    dx2, dwn = _matmul("in_proj1_dx", dproj1, w_in1, "nt", epi="norm_bwd", epi_arr=dx3, norm_x=x2,
                       norm_w=row(w["l1_norm_mix"]))
    grads["l1_norm_mix"] = dwn.reshape(-1)

    du0 = _matmul("down0_dx", dx2, full["l0_w_down"], "nt", out_dtype=_MXU_DTYPE, epi="drelu2", epi_arr=u0)
    grads["l0_w_down"] = _matmul("down0_dw", u0, dx2, "tn", a_pro="relu2")
    grads["l0_w_up"] = _matmul("up0_dw", h1, du0, "tn")
    dx1, dwn = _matmul("up0_dx", du0, full["l0_w_up"], "nt", epi="norm_bwd", epi_arr=dx2, norm_x=x1,
                       norm_w=row(w["l0_norm_mlp"]))
    grads["l0_norm_mlp"] = dwn.reshape(-1)
    grads["l0_w_out"] = _matmul("out_proj0_dw", mixed0, dx1, "tn")
    dmixed0 = _matmul("out_proj0_dx", dx1, full["l0_w_out"], "nt")
    dproj0, _ = _scan_bwd("ret_bwd", _f_ret, CHUNK, ret_tabs, [], ret_xs, ret_xt, ret_saved, ret_states,
                          (dmixed0, 512, 0), IN0_PAD, 2048, 0, lambda dx: jnp.concatenate(dx, axis=1))

    def ssd_assemble(dx):
        return jnp.concatenate(list(dx) + [jnp.zeros((dx[0].shape[0], 2048 - 1664), F32)], axis=1)

    dproj0, ssd_dc = _scan_bwd("ssd_bwd", _f_ssd, CHUNK, [expand], ssd_consts, ssd_xs, [], ssd_saved, ssd_states,
                               (dmixed0, 512, 1), IN0_PAD, 2048, 1, ssd_assemble, dx_alias=dproj0)
    heads = lambda a: a.reshape(SSD_HEADS, SSD_HEAD_DIM).sum(axis=1)
    grads["ssd_conv_w"] = ssd_dc[0]
    grads["ssd_conv_b"] = ssd_dc[1].reshape(-1)
    grads["ssd_dt_bias"] = heads(ssd_dc[2])
    grads["ssd_A_log"] = heads(ssd_dc[3])
    grads["ssd_D"] = heads(ssd_dc[4])
    grads["ssd_norm_w"] = ssd_dc[5].reshape(-1)
    grads["l0_w_in"] = _matmul("in_proj0_dw", h0, dproj0, "tn")[:, :in0]
    dx0, dwn = _matmul("in_proj0_dx", dproj0, w_in0, "nt", epi="norm_bwd", epi_arr=dx1, norm_x=x0,
                       norm_w=row(w["l0_norm_mix"]))
    grads["l0_norm_mix"] = dwn.reshape(-1)
    grad_x = dx0.reshape(x.shape)

    c_idx = lax.axis_index("c").astype(jnp.int32).reshape(1)
    small = SMALL_SHARDED + tuple(n for n in names if kinds[n] == "rep")
    order = LARGE + small
    block = lambda n, s: _shard_block(grads[n].reshape(_full_shape(n, w, kinds)), kinds[n], s, w[n].shape)
    rep_rows = _flat_rows([grads[n] for n in order[len(LARGE) + len(SMALL_SHARDED):]], F32)
    gslab = jnp.stack([_align_rows(jnp.concatenate(
        [_flat_rows([block(n, s) for n in LARGE + SMALL_SHARDED], F32), rep_rows])) for s in range(4)])
    from_sibling = _swap_halves("grads_swap_halves", gslab)
    chip_sum = _add_halves("grads_add_sibling", gslab, from_sibling, c_idx)
    place = jnp.stack([lax.axis_index("x"), lax.axis_index("y")]).astype(jnp.int32)
    from_x, from_y = _exchange_stage1("grads_stage1", chip_sum)
    send_a, keep_a, send_b, keep_b = _exchange_add1("grads_add1", chip_sum, from_x, from_y, place)
    got_a, got_b = _exchange_stage2("grads_stage2", send_a, send_b)
    my_half = _exchange_add2("grads_add2", keep_a, got_a, keep_b, got_b, c_idx)
    gsum = _join_halves("grads_join_halves", my_half)
    grad = dict(zip(order, _unpack(gsum, [w[n].shape for n in order])))

    delta, new_m, new_v = {}, {}, {}
    for n in LARGE:
        delta[n], new_m[n], new_v[n] = _adamw("adamw_" + n, w[n], grad[n], given["m_" + n], given["v_" + n])
    first_small = sum(_rows_of(math.prod(w[n].shape)) for n in LARGE)
    small_shapes = [w[n].shape for n in small]

    def small_slab(arrays):
        rows = _flat_rows(arrays, F32)
        return jnp.pad(rows, ((0, gsum.shape[0] - first_small - rows.shape[0]), (0, 0)))

    res = _adamw("adamw_small", small_slab([w[n] for n in small]), gsum[first_small:],
                 small_slab([given["m_" + n] for n in small]), small_slab([given["v_" + n] for n in small]))
    for out, slab in zip((delta, new_m, new_v), res):
        out.update(zip(small, _unpack(slab, small_shapes)))
    return (loss, grad_x, *[grad[n] for n in names], *[delta[n] for n in names], *[new_m[n] for n in names],
            *[new_v[n] for n in names])


def _full_shape(name, w, kinds):
    shp = w[name].shape
    if kinds[name] == "col":
        return (shp[0], 4 * shp[1])
    if kinds[name] == "row":
        return (4 * shp[0],) + tuple(shp[1:])
    return shp
```

```python
import functools
import math

import jax
import jax.numpy as jnp
from jax import lax
from jax.experimental import pallas as pl
from jax.experimental.pallas import tpu as pltpu

F32 = jnp.float32
_MXU_DTYPE = jnp.bfloat16

D_MODEL = 1024
CHUNK = 64
EPS = 1e-6
RET_HEADS, RET_DK = 4, 128
ROPE_THETA = 10000.0
SSD_HEADS, SSD_HEAD_DIM = 8, 64
GDN_HEADS, GDN_DK = 6, 128
S5_GROUPS, S5_GROUP, S5_STATE = 16, 16, 64
ADAM_LR, ADAM_B1, ADAM_B2, ADAM_EPS, ADAM_WD, ADAM_STEP = 0.001, 0.9, 0.999, 1e-08, 0.01, 10

IN0_PAD = 4096
IN1_PAD = 3584
LANES = 1024
VMEM_LIMIT = 56 * 1024 * 1024
MESH = pl.DeviceIdType.MESH

PARAMS = (
    ("l0_norm_mix", "rep"), ("l0_w_in", "col"), ("ssd_conv_w", "col"), ("ssd_conv_b", "rep"),
    ("ssd_dt_bias", "rep"), ("ssd_A_log", "rep"), ("ssd_D", "rep"), ("ssd_norm_w", "rep"),
    ("l0_w_out", "row"), ("l0_norm_mlp", "rep"), ("l0_w_up", "col"), ("l0_w_down", "row"),
    ("l1_norm_mix", "rep"), ("l1_w_in", "col"), ("gdn_conv_w", "col"), ("gdn_A_log", "rep"),
    ("gdn_dt_bias", "rep"), ("gdn_norm_w", "rep"), ("s5_A_re", "rep"), ("s5_A_im", "rep"),
    ("s5_log_step", "rep"), ("s5_B_re", "rep"), ("s5_B_im", "rep"), ("s5_C_re", "rep"), ("s5_C_im", "rep"),
    ("s5_D", "rep"), ("s5_w_glu", "row"), ("s5_b_glu", "rep"), ("l1_w_out", "row"), ("l1_norm_mlp", "rep"),
    ("l1_w_up", "col"), ("l1_w_down", "row"), ("final_norm", "rep"),
)
GATHER_BF16 = ("l0_w_in", "l0_w_out", "l0_w_up", "l0_w_down", "l1_w_in", "l1_w_out", "l1_w_up", "l1_w_down", "s5_w_glu")
GATHER_F32 = ("ssd_conv_w", "gdn_conv_w")
LARGE = GATHER_BF16[:8]
SMALL_SHARDED = ("s5_w_glu", "ssd_conv_w", "gdn_conv_w")


def _dg(a, b, ca, cb, prec=None):
    return lax.dot_general(a, b, (((ca,), (cb,)), ((), ())), preferred_element_type=F32, precision=prec)


def _lo(a):
    return a.astype(_MXU_DTYPE)


@jax.custom_vjp
def _mm(a, b):
    return _dg(_lo(a), _lo(b), 1, 0)


def _mm_fwd(a, b):
    return _mm(a, b), (a, b)


def _mm_bwd(res, g):
    a, b = res
    return _dg(_lo(g), _lo(b), 1, 1), _dg(_lo(a), _lo(g), 0, 0)


_mm.defvjp(_mm_fwd, _mm_bwd)


@jax.custom_vjp
def _mm_nt(a, b):
    return _dg(_lo(a), _lo(b), 1, 1)


def _mm_nt_fwd(a, b):
    return _mm_nt(a, b), (a, b)


def _mm_nt_bwd(res, g):
    a, b = res
    return _dg(_lo(g), _lo(b), 1, 0), _dg(_lo(g), _lo(a), 0, 0)


_mm_nt.defvjp(_mm_nt_fwd, _mm_nt_bwd)


@jax.custom_vjp
def _mm_tn(a, b):
    return _dg(_lo(a), _lo(b), 0, 0)


def _mm_tn_fwd(a, b):
    return _mm_tn(a, b), (a, b)


def _mm_tn_bwd(res, g):
    a, b = res
    return _dg(_lo(b), _lo(g), 1, 1), _dg(_lo(a), _lo(g), 1, 0)


_mm_tn.defvjp(_mm_tn_fwd, _mm_tn_bwd)


def _split2(x):
    hi = _lo(x)
    return hi, _lo(x - hi.astype(F32))


def _split3(x):
    h1 = _lo(x)
    r1 = x - h1.astype(F32)
    h2 = _lo(r1)
    return h1, h2, _lo(r1 - h2.astype(F32))


def _tri_cum_dir(m, ca):
    n, w = m.shape
    causal, _ = _tri_masks(n)
    out = _dg(causal.astype(_MXU_DTYPE), jnp.concatenate(_split3(m), axis=1), ca, 0)
    return out[:, :w] + out[:, w:2 * w] + out[:, 2 * w:]


@jax.custom_vjp
def _tri_cum(m):
    return _tri_cum_dir(m, 1)


def _tri_cum_fwd(m):
    return _tri_cum_dir(m, 1), None


def _tri_cum_bwd(_, g):
    return (_tri_cum_dir(g, 0),)


_tri_cum.defvjp(_tri_cum_fwd, _tri_cum_bwd)


@jax.custom_vjp
def _mm_exact_rhs(a, e):
    return _dg(jnp.concatenate(_split3(a), axis=1), jnp.concatenate([_lo(e)] * 3, axis=0), 1, 0)


def _mm_exact_rhs_fwd(a, e):
    return _mm_exact_rhs(a, e), e


def _mm_exact_rhs_bwd(e, g):
    return _dg(jnp.concatenate(_split3(g), axis=1), jnp.concatenate([_lo(e)] * 3, axis=1), 1, 1), jnp.zeros_like(e)


_mm_exact_rhs.defvjp(_mm_exact_rhs_fwd, _mm_exact_rhs_bwd)


def _bd(x):
    left = _iota(x.shape, 1) < (x.shape[1] // 2)
    zero = jnp.zeros_like(x)
    return jnp.concatenate([jnp.where(left, x, zero), jnp.where(left, zero, x)], axis=0)


def _unbd(m):
    half = m.shape[0] // 2
    left = _iota((half, m.shape[1]), 1) < (m.shape[1] // 2)
    return jnp.where(left, m[:half], m[half:])


def _pmm_nn(x, y):
    xh, xl = _split2(x)
    yh, yl = _split2(y)
    return _dg(jnp.concatenate([xh, xl, xh], axis=1), jnp.concatenate([_bd(yh), _bd(yh), _bd(yl)], axis=0), 1, 0)


def _pmm_nt(x, y):
    xh, xl = _split2(x)
    yh, yl = _split2(y)
    return _dg(jnp.concatenate([xh, xl, xh], axis=1), jnp.concatenate([_bd(yh), _bd(yh), _bd(yl)], axis=1), 1, 1)


def _pmm_tn(x, y):
    xh, xl = _split2(x)
    yh, yl = _split2(y)
    return _unbd(_dg(jnp.concatenate([xh, xl, xh], axis=0), jnp.concatenate([yh, yh, yl], axis=0), 0, 0))


@functools.lru_cache(maxsize=None)
def _shift(s, axis):
    @jax.custom_vjp
    def sh(x):
        return pltpu.roll(x, s, axis)

    def fwd(x):
        return sh(x), None

    def bwd(_, g):
        n = g.shape[axis]
        return (pltpu.roll(g, (n - s) % n, axis),)

    sh.defvjp(fwd, bwd)
    return sh


def _iota(shape, axis):
    return lax.broadcasted_iota(jnp.int32, shape, axis)


def _silu(x):
    return x * jax.nn.sigmoid(x)


def _unit_rms(x):
    return x * lax.rsqrt(jnp.mean(x * x, axis=-1, keepdims=True) + EPS)


def _l2norm(x):
    return x * lax.rsqrt(jnp.sum(x * x, axis=-1, keepdims=True) + EPS)


def _tri_masks(n):
    r, c = _iota((n, n), 0), _iota((n, n), 1)
    return r >= c, r > c


def _packed_rc():
    return _iota((CHUNK, 2 * CHUNK), 0), _iota((CHUNK, 2 * CHUNK), 1) & (CHUNK - 1)


def _decay_packed(g_packed):
    r, c = _packed_rc()
    seg = _tri_cum(g_packed * (r > c).astype(F32))
    return jnp.where(r >= c, jnp.exp(jnp.where(r >= c, seg, 0.0)), 0.0)


def _conv(x, tail, w):
    rows, width = x.shape
    row = _iota((rows, width), 0)
    acc = x * w[3:4, :]
    pad = jnp.zeros((rows - 8, width), F32)
    for j in range(3):
        s = 3 - j
        prev = jnp.concatenate([_shift(s, 0)(tail), pad], axis=0)
        acc = acc + w[j:j + 1, :] * jnp.where(row < s, prev, _shift(s, 0)(x))
    return acc


def _tri_inv_impl(mats):
    r, c = _packed_rc()
    eye = (r == c).astype(F32)

    def same_block(b):
        return (r // b) == (c // b)

    a8 = [jnp.where(same_block(8), a, 0.0) for a in mats]
    a2 = [_pmm_nn(t, t) for t in a8]
    a4 = [_pmm_nn(t, t) for t in a2]
    x = [_pmm_nn(eye - p, eye + q) for p, q in zip(a8, a2)]
    x = [_pmm_nn(p, eye + q) for p, q in zip(x, a4)]
    for b in (8, 16, 32):
        off = [jnp.where(same_block(2 * b) & jnp.logical_not(same_block(b)), a, 0.0) for a in mats]
        y = [_pmm_nn(p, q) for p, q in zip(x, off)]
        x = [p - _pmm_nn(q, p) for p, q in zip(x, y)]
    return x


@jax.custom_vjp
def _tri_inv(mats):
    return _tri_inv_impl(mats)


def _tri_inv_fwd(mats):
    t = _tri_inv_impl(mats)
    return t, t


def _tri_inv_bwd(t, g):
    m1 = [_pmm_tn(p, q) for p, q in zip(t, g)]
    return ([-_pmm_nt(p, q) for p, q in zip(m1, t)],)


_tri_inv.defvjp(_tri_inv_fwd, _tri_inv_bwd)


def _f_ret(tabs, consts, xs, xtabs, states):
    dmask, kdec, qdec, cdec = tabs
    q, k, v, gate = xs
    cs, sn = xtabs
    (st,) = states
    swap = _shift(RET_DK // 2, 1)
    heads = range(RET_HEADS)
    sls = [slice(128 * h, 128 * h + 128) for h in heads]
    qh = [(q[:, sl] * cs + swap(q[:, sl]) * sn) * (RET_DK ** -0.5) for sl in sls]
    kh = [k[:, sl] * cs + swap(k[:, sl]) * sn for sl in sls]
    sh = [st[sl, :] for sl in sls]
    scores = [_mm_nt(a, b) * dmask[64 * h:64 * h + 64, :] for h, a, b in zip(heads, qh, kh)]
    y = [_mm(s, v[:, sl]) for s, sl in zip(scores, sls)]
    y = [t + _mm(a * qdec[:, sl], s) for t, a, sl, s in zip(y, qh, sls, sh)]
    new = [s * cdec[:, sl] + _mm_tn(b * kdec[:, sl], v[:, sl]) for s, sl, b in zip(sh, sls, kh)]
    outs = [_silu(gate[:, sl]) * _unit_rms(t) for sl, t in zip(sls, y)]
    return (jnp.concatenate(outs, axis=1),), [jnp.concatenate(new, axis=0)]


def _f_ssd(tabs, consts, xs, xtabs, states):
    (expand,) = tabs
    conv_w, conv_b, dtb, alog, dskip, nw = consts
    z, xr, br, cr, dtr = xs
    tx, tb, tc, st = states
    xc = _silu(_conv(xr, tx, conv_w[:, 0:512]) + conv_b[:, 0:512])
    bc = _silu(_conv(br, tb, conv_w[:, 512:768]) + conv_b[:, 512:768])
    cc = _silu(_conv(cr, tc, conv_w[:, 768:1024]) + conv_b[:, 768:1024])
    dt = jax.nn.softplus(_mm_exact_rhs(dtr, expand) + dtb)
    la = dt * (-jnp.exp(alog))
    lacum = _tri_cum(la)
    total = jnp.sum(la, axis=0, keepdims=True)
    xd = xc * dt
    dte, ecum, cdec = jnp.exp(total - lacum), jnp.exp(lacum), jnp.exp(total)
    pairs = range(SSD_HEADS // 2)
    sls = [slice(128 * p, 128 * p + 128) for p in pairs]
    bg = [bc[:, 128 * g:128 * g + 128] for g in range(2)]
    cg = [cc[:, 128 * g:128 * g + 128] for g in range(2)]
    cb2 = [_mm_nt(c, jnp.concatenate([b, b], axis=0)) for b, c in zip(bg, cg)]
    lm = [_decay_packed(la[:, sl]) for sl in sls]
    sp = [st[sl, :] for sl in sls]
    ys = [_mm(cg[p // 2], sp[p]) * ecum[:, sls[p]] for p in pairs]
    ys = [ys[p] + _mm(cb2[p // 2] * lm[p], _bd(xd[:, sls[p]])) for p in pairs]
    new = [sp[p] * cdec[:, sls[p]] + _mm_tn(bg[p // 2], xd[:, sls[p]] * dte[:, sls[p]]) for p in pairs]
    y = jnp.concatenate(ys, axis=1) + dskip * xc
    yg = y * _silu(z)
    out = jnp.concatenate([_unit_rms(yg[:, 0:256]), _unit_rms(yg[:, 256:512])], axis=1) * nw
    return (out,), [xr[CHUNK - 8:, :], br[CHUNK - 8:, :], cr[CHUNK - 8:, :], jnp.concatenate(new, axis=0)]


def _f_gdn(tabs, consts, xs, xtabs, states):
    conv_w, p_alog, p_dtb, nw = consts
    qr, kr, vr, z, ba = xs
    tq, tk, tv, st = states
    qc = _silu(_conv(qr, tq, conv_w[:, 0:768]))
    kc = _silu(_conv(kr, tk, conv_w[:, 768:1536]))
    vc = _silu(_conv(vr, tv, conv_w[:, 1536:2304]))
    gl = -jnp.exp(p_alog) * jax.nn.softplus(ba + p_dtb)
    bl = jax.nn.sigmoid(ba)
    gcum = _tri_cum(gl)
    left128 = _iota((CHUNK, 128), 1) < 64
    left256 = _iota((CHUNK, 256), 1) < 128
    r, c = _packed_rc()
    diag_blocks = (_iota((256, 256), 0) < 128) == (_iota((256, 256), 1) < 128)

    def norm2(t):
        return jnp.concatenate([_l2norm(t[:, 0:128]), _l2norm(t[:, 128:256])], axis=1)

    def pick(arr, off, left, p):
        return jnp.where(left, arr[:, off + 2 * p:off + 2 * p + 1], arr[:, off + 2 * p + 1:off + 2 * p + 2])

    pairs = range(GDN_HEADS // 2)
    sls = [slice(256 * p, 256 * p + 256) for p in pairs]
    qn = [norm2(qc[:, sl]) * (GDN_DK ** -0.5) for sl in sls]
    kn = [norm2(kc[:, sl]) for sl in sls]
    dec = [_decay_packed(pick(gl, 6, left128, p)) for p in pairs]
    g2 = [pick(gl, 6, left256, p) for p in pairs]
    gc2 = [pick(gcum, 6, left256, p) for p in pairs]
    b2 = [pick(bl, 0, left256, p) for p in pairs]
    tot = [jnp.sum(t, axis=0, keepdims=True) for t in g2]
    eg = [jnp.exp(t) for t in gc2]
    et = [jnp.exp(t - s) for t, s in zip(tot, gc2)]
    cd = [jnp.exp(t) for t in tot]
    kb = [k * b for k, b in zip(kn, b2)]
    vb = [vc[:, sl] * b for sl, b in zip(sls, b2)]
    kbd = [_bd(k) for k in kn]
    tm = _tri_inv([jnp.where(r > c, _mm_nt(a, b) * d, 0.0) for a, b, d in zip(kb, kbd, dec)])
    u = [_mm(t, _bd(v)) for t, v in zip(tm, vb)]
    w = [_mm(t, _bd(k * e)) for t, k, e in zip(tm, kb, eg)]
    attn = [_mm_nt(q, k) * d for q, k, d in zip(qn, kbd, dec)]
    sp = [st[sl, :] for sl in sls]
    vn = [a - _mm(b, s) for a, b, s in zip(u, w, sp)]
    o = [_mm(q * e, s) + _mm(a, _bd(v)) for q, e, s, a, v in zip(qn, eg, sp, attn, vn)]
    new = [s * d + jnp.where(diag_blocks, _mm_tn(k * e, v), 0.0) for s, d, k, e, v in zip(sp, cd, kn, et, vn)]
    outs = []
    for p in pairs:
        for hh in range(2):
            osl = slice(128 * hh, 128 * hh + 128)
            zsl = slice(256 * p + 128 * hh, 256 * p + 128 * hh + 128)
            outs.append(_unit_rms(o[p][:, osl]) * nw * _silu(z[:, zsl]))
    return (jnp.concatenate(outs, axis=1),), [qr[CHUNK - 8:, :], kr[CHUNK - 8:, :], vr[CHUNK - 8:, :],
                                             jnp.concatenate(new, axis=0)]


def _f_s5(tabs, consts, xs, xtabs, states):
    lam_re, lam_im, bblk, c_re, c_im, dskip, wglu, bglu = consts
    (u,) = xs
    s_re, s_im = states
    rows = u.shape[0]
    n = lam_re.shape[1]
    bu = _mm(u, bblk)
    hr, hi = bu[:, 0:n], bu[:, n:2 * n]
    row = _iota((rows, n), 0)
    h0r, h0i = s_re[0:1, :], s_im[0:1, :]
    hr = hr + jnp.where(row == 0, lam_re * h0r - lam_im * h0i, 0.0)
    hi = hi + jnp.where(row == 0, lam_re * h0i + lam_im * h0r, 0.0)
    pr, pi = lam_re, lam_im
    d = 1
    while d < rows:
        sr = jnp.where(row >= d, _shift(d, 0)(hr), 0.0)
        si = jnp.where(row >= d, _shift(d, 0)(hi), 0.0)
        hr, hi = hr + pr * sr - pi * si, hi + pr * si + pi * sr
        pr, pi = pr * pr - pi * pi, 2.0 * pr * pi
        d *= 2
    y = _mm(hr, c_re) - _mm(hi, c_im) + dskip * u
    y = jax.nn.gelu(y)
    out = y * jax.nn.sigmoid(_mm(y, wglu) + bglu)
    last_r = jnp.broadcast_to(hr[rows - 1:rows, :], (8, n))
    last_i = jnp.broadcast_to(hi[rows - 1:rows, :], (8, n))
    return (out,), [last_r, last_i]


def _full_spec(a):
    nd = a.ndim
    return pl.BlockSpec(a.shape, lambda i, _nd=nd: (0,) * _nd)


CHUNKS_PER_STEP = 4


def _chunks_per_step(f, rows, n):
    def g(tabs, consts, xs, xtabs, states):
        ys = []
        for i in range(n):
            sl = slice(rows * i, rows * (i + 1))
            (y,), states = f(tabs, consts, [t[sl] for t in xs], [t[sl] for t in xtabs], states)
            ys.append(y)
        return (jnp.concatenate(ys, axis=0),), states

    return g


def _scan_fwd(name, f, rows, tabs, consts, xs, xtabs, state_shapes, y_total, y_width, y_cb, y_alias=None):
    seq = xs[0][0].shape[0]
    per_step = math.gcd(CHUNKS_PER_STEP, seq // rows)
    f = _chunks_per_step(f, rows, per_step)
    rows = rows * per_step
    nc = seq // rows
    nt, ncst, nx, nxt, ns = len(tabs), len(consts), len(xs), len(xtabs), len(state_shapes)
    alias = y_alias is not None

    def body(*refs):
        p = 0
        tab_r = refs[p:p + nt]; p += nt
        c_r = refs[p:p + ncst]; p += ncst
        x_r = refs[p:p + nx]; p += nx
        xt_r = refs[p:p + nxt]; p += nxt
        if alias:
            p += 1
        y_ref = refs[p]; p += 1
        sv_r = refs[p:p + ns]; p += ns
        st_r = refs[p:p + ns]

        @pl.when(pl.program_id(0) == 0)
        def _():
            for s in st_r:
                s[...] = jnp.zeros(s.shape, F32)

        st = [s[...] for s in st_r]
        for r, v in zip(sv_r, st):
            r[...] = v
        (y,), new = f([r[...] for r in tab_r], [r[...] for r in c_r], [r[...].astype(F32) for r in x_r],
                      [r[...] for r in xt_r], st)
        y_ref[...] = y.astype(y_ref.dtype)
        for s, v in zip(st_r, new):
            s[...] = v

    win = [pl.BlockSpec((rows, w), lambda i, _cb=cb: (i, _cb)) for (_, w, cb) in list(xs) + list(xtabs)]
    in_specs = [_full_spec(a) for a in list(tabs) + list(consts)] + win
    args = list(tabs) + list(consts) + [a for (a, _, _) in list(xs) + list(xtabs)]
    io_alias = {}
    if alias:
        in_specs.append(pl.BlockSpec(memory_space=pl.ANY))
        io_alias = {len(args): 0}
        args.append(y_alias)
    out_shape = [jax.ShapeDtypeStruct((seq, y_total), _MXU_DTYPE)]
    out_specs = [pl.BlockSpec((rows, y_width), lambda i: (i, y_cb))]
    for (r, c) in state_shapes:
        out_shape.append(jax.ShapeDtypeStruct((nc * r, c), F32))
        out_specs.append(pl.BlockSpec((r, c), lambda i: (i, 0)))
    res = pl.pallas_call(
        body, name=name, grid=(nc,), in_specs=in_specs, out_specs=out_specs, out_shape=out_shape,
        scratch_shapes=[pltpu.VMEM(s, F32) for s in state_shapes], input_output_aliases=io_alias,
        compiler_params=pltpu.CompilerParams(dimension_semantics=("arbitrary",), vmem_limit_bytes=VMEM_LIMIT),
    )(*args)
    return res[0], list(res[1:])


def _scan_bwd(name, f, rows, tabs, consts, xs, xtabs, saved, state_shapes, dy, dx_total, dx_width, dx_cb,
              assemble, dx_alias=None):
    seq = xs[0][0].shape[0]
    per_step = math.gcd(CHUNKS_PER_STEP, seq // rows)
    f = _chunks_per_step(f, rows, per_step)
    rows = rows * per_step
    nc = seq // rows
    nt, ncst, nx, nxt, ns = len(tabs), len(consts), len(xs), len(xtabs), len(state_shapes)
    alias = dx_alias is not None

    def body(*refs):
        p = 0
        tab_r = refs[p:p + nt]; p += nt
        c_r = refs[p:p + ncst]; p += ncst
        x_r = refs[p:p + nx]; p += nx
        xt_r = refs[p:p + nxt]; p += nxt
        sv_r = refs[p:p + ns]; p += ns
        dy_ref = refs[p]; p += 1
        if alias:
            p += 1
        dx_ref = refs[p]; p += 1
        dc_r = refs[p:p + ncst]; p += ncst
        ds_r = refs[p:p + ns]

        @pl.when(pl.program_id(0) == 0)
        def _():
            for s in ds_r:
                s[...] = jnp.zeros(s.shape, F32)
            for r in dc_r:
                r[...] = jnp.zeros(r.shape, F32)

        tab_v = [r[...] for r in tab_r]
        xt_v = [r[...] for r in xt_r]

        def g(c, x, s):
            (y,), new = f(tab_v, c, x, xt_v, s)
            return y, new

        _, vjp = jax.vjp(g, [r[...] for r in c_r], [r[...].astype(F32) for r in x_r], [r[...] for r in sv_r])
        dc, dx, ds = vjp((dy_ref[...], [s[...] for s in ds_r]))
        dx_ref[...] = assemble(dx).astype(dx_ref.dtype)
        for r, v in zip(dc_r, dc):
            r[...] += v
        for s, v in zip(ds_r, ds):
            s[...] = v

    win = [pl.BlockSpec((rows, w), lambda j, _cb=cb: (nc - 1 - j, _cb)) for (_, w, cb) in list(xs) + list(xtabs)]
    in_specs = [_full_spec(a) for a in list(tabs) + list(consts)] + win
    args = list(tabs) + list(consts) + [a for (a, _, _) in list(xs) + list(xtabs)]
    for (r, c), sv in zip(state_shapes, saved):
        in_specs.append(pl.BlockSpec((r, c), lambda j: (nc - 1 - j, 0)))
        args.append(sv)
    in_specs.append(pl.BlockSpec((rows, dy[1]), lambda j: (nc - 1 - j, dy[2])))
    args.append(dy[0])
    io_alias = {}
    if alias:
        in_specs.append(pl.BlockSpec(memory_space=pl.ANY))
        io_alias = {len(args): 0}
        args.append(dx_alias)
    out_shape = [jax.ShapeDtypeStruct((seq, dx_total), _MXU_DTYPE)] +[jax.ShapeDtypeStruct(a.shape, F32) for a in consts]
    out_specs = [pl.BlockSpec((rows, dx_width), lambda j: (nc - 1 - j, dx_cb))] + [_full_spec(a) for a in consts]
    res = pl.pallas_call(
        body, name=name, grid=(nc,), in_specs=in_specs, out_specs=out_specs, out_shape=out_shape,
        scratch_shapes=[pltpu.VMEM(s, F32) for s in state_shapes], input_output_aliases=io_alias,
        compiler_params=pltpu.CompilerParams(dimension_semantics=("arbitrary",), vmem_limit_bytes=VMEM_LIMIT),
    )(*args)
    return res[0], list(res[1:])


def _tile(n, want):
    t = min(n, want)
    while n % t:
        t //= 2
    return t


MATMUL_VMEM_BUDGET = 40 * 1024 * 1024


MXU_FLOPS_PER_S = 8.5e14
HBM_BYTES_PER_S = 2.8e12
GRID_STEP_S = 0.35e-6


def _pick_tiles(m, n, k, sa, sb, so, se, whole_rows=False, reduce_rows=False):
    best = None
    for tn in ({n} if whole_rows else {_tile(n, t) for t in (4096, 2048, 1024, 512)}):
        for tm in {_tile(m, t) for t in (2048, 1024, 512)}:
            for tk in {_tile(k, t) for t in (4096, 2048, 1024, 512)}:
                at, bt, ot = tm * tk * sa, tk * tn * sb, tm * tn * so
                need = 2 * (at + bt + ot + tm * tn * se) + 2 * tm * tn * 4 + (at if sa == 4 else 0) + (bt if sb == 4 else 0)
                if need > MATMUL_VMEM_BUDGET:
                    continue
                ni, nj, nk = m // tm, n // tn, k // tk
                b_reads = ni if (reduce_rows or nk > 1) else 1
                moved = m * k * sa * nj + k * n * sb * b_reads + m * n * (so + se)
                cost = max(2 * m * n * k / MXU_FLOPS_PER_S, moved / HBM_BYTES_PER_S) + ni * nj * nk * GRID_STEP_S
                key = (cost, nk, -tm)
                if best is None or key < best[0]:
                    best = (key, (tm, tn, tk))
    assert best is not None, (m, n, k)
    return best[1]


def _matmul(name, a, b, mode, out_dtype=F32, a_pro=None, epi=None, epi_arr=None, norm_w=None, norm_x=None):
    if mode == "nn":
        (m, k), (k2, n) = a.shape, b.shape
    elif mode == "nt":
        (m, k), (n, k2) = a.shape, b.shape
    else:
        (k, m), (k2, n) = a.shape, b.shape
    assert k == k2, (name, a.shape, b.shape)
    size = lambda t: jnp.dtype(t).itemsize
    rows_in = [] if epi is None else [epi_arr] + ([norm_x] if epi == "norm_bwd" else [])
    emit_norm = epi == "add" and norm_w is not None
    extra = sum(size(t.dtype) for t in rows_in) + (size(_MXU_DTYPE) if emit_norm else 0)
    tm, tn, tk = _pick_tiles(m, n, k, size(a.dtype), size(b.dtype), size(out_dtype), extra,
                             whole_rows=norm_w is not None, reduce_rows=mode == "tn")
    nk = k // tk
    ca, cb = {"nn": (1, 0), "nt": (1, 1), "tn": (0, 0)}[mode]
    n_in = 2 + len(rows_in) + (norm_w is not None)
    n_out = 2 if (emit_norm or epi == "norm_bwd") else 1

    def body(*refs):
        refs = list(refs)
        acc = refs.pop() if nk > 1 else None
        a_ref, b_ref = refs[0], refs[1]
        e_ref = refs[2] if epi is not None else None
        x_ref = refs[3] if epi == "norm_bwd" else None
        w_ref = refs[n_in - 1] if norm_w is not None else None
        o_ref = refs[n_in]
        o2_ref = refs[n_in + 1] if n_out == 2 else None
        kk = pl.program_id(2)

        if epi == "norm_bwd":
            @pl.when((pl.program_id(1) == 0) & (kk == 0))
            def _():
                o2_ref[...] = jnp.zeros(o2_ref.shape, F32)

        av = a_ref[...]
        if a_pro == "relu2":
            r = jnp.maximum(av, 0.0)
            av = r * r
        part = _dg(_lo(av), _lo(b_ref[...]), ca, cb)

        def finish(r):
            if epi == "add":
                r = r + e_ref[...]
                if emit_norm:
                    o2_ref[...] = (_unit_rms(r) * w_ref[...]).astype(_MXU_DTYPE)
            elif epi == "drelu2":
                r = r * (2.0 * jnp.maximum(e_ref[...], 0.0))
            elif epi == "norm_bwd":
                xv = x_ref[...]
                rstd = lax.rsqrt(jnp.mean(xv * xv, axis=-1, keepdims=True) + EPS)
                xh = xv * rstd
                g = r * w_ref[...]
                o2_ref[...] += jnp.sum(r * xh, axis=0, keepdims=True)
                r = e_ref[...] + rstd * (g - xh * jnp.mean(g * xh, axis=-1, keepdims=True))
            o_ref[...] = r.astype(out_dtype)

        if nk == 1:
            finish(part)
        else:
            @pl.when(kk == 0)
            def _():
                acc[...] = part

            @pl.when(kk > 0)
            def _():
                acc[...] += part

            @pl.when(kk == nk - 1)
            def _():
                finish(acc[...])

    if mode == "tn":
        a_spec = pl.BlockSpec((tk, tm), lambda j, i, kk: (kk, i))
    else:
        a_spec = pl.BlockSpec((tm, tk), lambda j, i, kk: (i, kk))
    if mode == "nt":
        b_spec = pl.BlockSpec((tn, tk), lambda j, i, kk: (j, kk))
    else:
        b_spec = pl.BlockSpec((tk, tn), lambda j, i, kk: (kk, j))
    o_spec = pl.BlockSpec((tm, tn), lambda j, i, kk: (i, j))
    vec_spec = pl.BlockSpec((1, tn), lambda j, i, kk: (0, j))
    in_specs, args = [a_spec, b_spec] + [o_spec] * len(rows_in), [a, b] + rows_in
    if norm_w is not None:
        in_specs.append(vec_spec)
        args.append(norm_w)
    out_specs, out_shape = [o_spec], [jax.ShapeDtypeStruct((m, n), out_dtype)]
    if emit_norm:
        out_specs.append(o_spec)
        out_shape.append(jax.ShapeDtypeStruct((m, n), _MXU_DTYPE))
    elif epi == "norm_bwd":
        out_specs.append(vec_spec)
        out_shape.append(jax.ShapeDtypeStruct((1, n), F32))
    sem = ("parallel", "arbitrary" if epi == "norm_bwd" else "parallel", "arbitrary")
    res = pl.pallas_call(
        body, name=name, grid=(n // tn, m // tm, nk), in_specs=in_specs, out_specs=out_specs, out_shape=out_shape,
        scratch_shapes=[pltpu.VMEM((tm, tn), F32)] if nk > 1 else [],
        compiler_params=pltpu.CompilerParams(dimension_semantics=sem, vmem_limit_bytes=VMEM_LIMIT),
    )(*args)
    return res[0] if n_out == 1 else res


ROW_TILE = 512


def _rmsnorm_fwd(name, x, w):
    seq, d = x.shape
    tr = _tile(seq, ROW_TILE)

    def body(x_ref, w_ref, o_ref):
        xv = x_ref[...]
        o_ref[...] = (_unit_rms(xv) * w_ref[...]).astype(_MXU_DTYPE)

    return pl.pallas_call(
        body, name=name, grid=(seq // tr,),
        in_specs=[pl.BlockSpec((tr, d), lambda i: (i, 0)), pl.BlockSpec((1, d), lambda i: (0, 0))],
        out_specs=pl.BlockSpec((tr, d), lambda i: (i, 0)), out_shape=jax.ShapeDtypeStruct((seq, d), _MXU_DTYPE),
        compiler_params=pltpu.CompilerParams(dimension_semantics=("parallel",), vmem_limit_bytes=VMEM_LIMIT),
    )(x, w)


def _loss_head(name, x, w, target):
    seq, d = x.shape
    tr = _tile(seq, ROW_TILE)

    def body(x_ref, w_ref, t_ref, loss_ref, dx_ref, dw_ref):
        @pl.when(pl.program_id(0) == 0)
        def _():
            dw_ref[...] = jnp.zeros(dw_ref.shape, F32)
            loss_ref[...] = jnp.zeros(loss_ref.shape, F32)

        xv = x_ref[...]
        rstd = lax.rsqrt(jnp.mean(xv * xv, axis=-1, keepdims=True) + EPS)
        xh = xv * rstd
        err = xh * w_ref[...] - t_ref[...]
        per_row = jnp.mean(err * err, axis=-1, keepdims=True)
        loss_ref[...] += 0.5 * jnp.sum(per_row, axis=0, keepdims=True)
        dy = err * (1.0 / d)
        g = dy * w_ref[...]
        dx_ref[...] = rstd * (g - xh * jnp.mean(g * xh, axis=-1, keepdims=True))
        dw_ref[...] += jnp.sum(dy * xh, axis=0, keepdims=True)

    row = pl.BlockSpec((tr, d), lambda i: (i, 0))
    vec = pl.BlockSpec((1, d), lambda i: (0, 0))
    one = pl.BlockSpec((1, 1), lambda i: (0, 0))
    return pl.pallas_call(
        body, name=name, grid=(seq // tr,), in_specs=[row, vec, row], out_specs=[one, row, vec],
        out_shape=[jax.ShapeDtypeStruct((1, 1), F32), jax.ShapeDtypeStruct((seq, d), F32),
                   jax.ShapeDtypeStruct((1, d), F32)],
        compiler_params=pltpu.CompilerParams(dimension_semantics=("arbitrary",), vmem_limit_bytes=VMEM_LIMIT),
    )(x, w, target)


SLAB_TILE_ROWS = 1024


def _slab_tile(rows, cap=SLAB_TILE_ROWS):
    step = 16 if rows % 16 == 0 else 8
    return max(t for t in range(step, min(rows, cap) + 1, step) if rows % t == 0)


def _adamw(name, w, g, m, v):
    rows, cols = w.shape
    tr = _slab_tile(rows, SLAB_TILE_ROWS // 2) if rows % 8 == 0 else rows

    def body(w_ref, g_ref, m_ref, v_ref, d_ref, nm_ref, nv_ref):
        gv = g_ref[...]
        nm = ADAM_B1 * m_ref[...] + (1.0 - ADAM_B1) * gv
        nv = ADAM_B2 * v_ref[...] + (1.0 - ADAM_B2) * (gv * gv)
        m_hat = nm / (1.0 - ADAM_B1 ** ADAM_STEP)
        v_hat = nv / (1.0 - ADAM_B2 ** ADAM_STEP)
        d_ref[...] = -ADAM_LR * (m_hat / (jnp.sqrt(v_hat) + ADAM_EPS) + ADAM_WD * w_ref[...])
        nm_ref[...] = nm
        nv_ref[...] = nv

    spec = pl.BlockSpec((tr, cols), lambda i: (i, 0))
    sds = jax.ShapeDtypeStruct(w.shape, F32)
    return pl.pallas_call(
        body, name=name, grid=(rows // tr,), in_specs=[spec] * 4, out_specs=[spec] * 3, out_shape=[sds] * 3,
        compiler_params=pltpu.CompilerParams(dimension_semantics=("parallel",), vmem_limit_bytes=VMEM_LIMIT),
    )(w, g, m, v)


WIRE_DTYPE = jnp.bfloat16


def _add_halves(name, g, t1, c):
    nsec, rows, _ = g.shape
    rh = rows // 2
    tr = _slab_tile(rh)
    nb = rh // tr

    def body(c_ref, g_ref, t_ref, o_ref):
        o_ref[...] = (g_ref[...] + t_ref[...]).astype(o_ref.dtype)

    gs = pltpu.PrefetchScalarGridSpec(
        num_scalar_prefetch=1, grid=(nsec, nb),
        in_specs=[pl.BlockSpec((1, tr, LANES), lambda s, i, c_ref: (s, c_ref[0] * nb + i, 0)),
                  pl.BlockSpec((1, tr, LANES), lambda s, i, c_ref: (s, i, 0))],
        out_specs=pl.BlockSpec((1, tr, LANES), lambda s, i, c_ref: (s, i, 0)))
    return pl.pallas_call(
        body, name=name, grid_spec=gs, out_shape=jax.ShapeDtypeStruct((nsec, rh, LANES), WIRE_DTYPE),
        compiler_params=pltpu.CompilerParams(dimension_semantics=("parallel", "parallel"),
                                             vmem_limit_bytes=VMEM_LIMIT),
    )(c, g, t1)


ANY = pl.BlockSpec(memory_space=pl.ANY)


def _place():
    return lax.axis_index("x"), lax.axis_index("y"), lax.axis_index("c")


def _all_gather_shards(name, slab):
    rows = slab.shape[0]
    rh = rows // 2
    rq = rh // 2

    def body(x_ref, out_ref, send_sems, recv_sems):
        x, y, c = _place()
        me, sibling = (x, y, c), (x, y, 1 - c)
        xn, yn, dg = (1 - x, y), (x, 1 - y), (1 - x, 1 - y)

        def piece(chip, core, q):
            return out_ref.at[2 * chip[0] + chip[1], pl.ds(core * rh + q * rq, rq), :]

        def copy(k, chip, core, q, to, src=None):
            return pltpu.make_async_remote_copy(
                src_ref=piece(chip, core, q) if src is None else src, dst_ref=piece(chip, core, q),
                send_sem=send_sems.at[k], recv_sem=recv_sems.at[k], device_id=to, device_id_type=MESH)

        own = [x_ref.at[pl.ds(c * rh + q * rq, rq), :] for q in range(2)]
        sends = [copy(0, (x, y), c, 0, (*xn, c), src=own[0]), copy(1, (x, y), c, 1, (*xn, c), src=own[1]),
                 copy(2, (x, y), c, 0, (*yn, c), src=own[0]), copy(3, (x, y), c, 1, (*yn, c), src=own[1])]
        for cp in sends:
            cp.start()
        landed = [(0, xn, 0), (3, yn, 1), (1, xn, 1), (2, yn, 0), (4, dg, 0), (5, dg, 1)]
        onward = {0: (4, (*yn, c)), 3: (5, (*xn, c))}
        for i, (k, chip, q) in enumerate(landed):
            copy(k, chip, c, q, me).wait_recv()
            if k in onward:
                fk, to = onward[k]
                sends.append(copy(fk, chip, c, q, to))
                sends[-1].start()
            sends.append(copy(6 + i, chip, c, q, sibling))
            sends[-1].start()
        for i, (k, chip, q) in enumerate(landed):
            copy(6 + i, chip, 1 - c, q, me).wait_recv()
        for cp in sends:
            cp.wait_send()

    got = pl.pallas_call(
        body, name=name, in_specs=[ANY], out_specs=ANY,
        out_shape=jax.ShapeDtypeStruct((4, rows, LANES), slab.dtype),
        scratch_shapes=[pltpu.SemaphoreType.DMA((12,)), pltpu.SemaphoreType.DMA((12,))],
    )(slab)
    return lax.dynamic_update_slice(got, slab[None], (2 * lax.axis_index("x") + lax.axis_index("y"), 0, 0))


def _swap_halves(name, g):
    nsec, rows, _ = g.shape
    rh = rows // 2

    def body(g_ref, t_ref, send_sem, recv_sem):
        x, y, c = _place()
        cp = pltpu.make_async_remote_copy(
            src_ref=g_ref.at[:, pl.ds((1 - c) * rh, rh), :], dst_ref=t_ref, send_sem=send_sem, recv_sem=recv_sem,
            device_id=(x, y, 1 - c), device_id_type=MESH)
        cp.start()
        cp.wait()

    return pl.pallas_call(
        body, name=name, in_specs=[ANY], out_specs=ANY, out_shape=jax.ShapeDtypeStruct((nsec, rh, LANES), F32),
        scratch_shapes=[pltpu.SemaphoreType.DMA, pltpu.SemaphoreType.DMA],
    )(g)


def _exchange_stage1(name, p):
    _, rh, _ = p.shape
    rq = rh // 2

    def body(p_ref, fx_ref, fy_ref, send_sems, recv_sems):
        x, y, c = _place()
        to_x = pltpu.make_async_remote_copy(
            src_ref=p_ref.at[pl.ds(2 * (1 - x), 2), pl.ds(0, rq), :], dst_ref=fx_ref, send_sem=send_sems.at[0],
            recv_sem=recv_sems.at[0], device_id=(1 - x, y, c), device_id_type=MESH)
        to_y = [pltpu.make_async_remote_copy(
            src_ref=p_ref.at[2 * sx + (1 - y), pl.ds(rq, rq), :], dst_ref=fy_ref.at[sx], send_sem=send_sems.at[1 + sx],
            recv_sem=recv_sems.at[1 + sx], device_id=(x, 1 - y, c), device_id_type=MESH) for sx in range(2)]
        for cp in [to_x] + to_y:
            cp.start()
        for cp in [to_x] + to_y:
            cp.wait_recv()
        for cp in [to_x] + to_y:
            cp.wait_send()

    sds = jax.ShapeDtypeStruct((2, rq, LANES), p.dtype)
    return pl.pallas_call(
        body, name=name, in_specs=[ANY], out_specs=[ANY, ANY], out_shape=[sds, sds],
        scratch_shapes=[pltpu.SemaphoreType.DMA((3,)), pltpu.SemaphoreType.DMA((3,))],
    )(p)


def _exchange_add1(name, p, from_x, from_y, place):
    _, rh, _ = p.shape
    rq = rh // 2
    tr = _slab_tile(rq)
    nb = rq // tr

    def body(xy_ref, pa_s, pa_k, pb_s, pb_k, fx_s, fx_k, fy_s, fy_k, sa, ka, sb, kb):
        for mine, theirs, out in ((pa_s, fx_s, sa), (pa_k, fx_k, ka), (pb_s, fy_s, sb), (pb_k, fy_k, kb)):
            out[...] = (mine[0].astype(F32) + theirs[0].astype(F32)).astype(out.dtype)

    blk = lambda fn: pl.BlockSpec((1, tr, LANES), fn)
    gs = pltpu.PrefetchScalarGridSpec(
        num_scalar_prefetch=1, grid=(nb,),
        in_specs=[blk(lambda i, xy: (2 * xy[0] + 1 - xy[1], i, 0)), blk(lambda i, xy: (2 * xy[0] + xy[1], i, 0)),
                  blk(lambda i, xy: (2 * (1 - xy[0]) + xy[1], nb + i, 0)), blk(lambda i, xy: (2 * xy[0] + xy[1], nb + i, 0)),
                  blk(lambda i, xy: (1 - xy[1], i, 0)), blk(lambda i, xy: (xy[1], i, 0)),
                  blk(lambda i, xy: (1 - xy[0], i, 0)), blk(lambda i, xy: (xy[0], i, 0))],
        out_specs=[pl.BlockSpec((tr, LANES), lambda i, xy: (i, 0))] * 4)
    sds = jax.ShapeDtypeStruct((rq, LANES), p.dtype)
    return pl.pallas_call(
        body, name=name, grid_spec=gs, out_shape=[sds] * 4,
        compiler_params=pltpu.CompilerParams(dimension_semantics=("parallel",), vmem_limit_bytes=VMEM_LIMIT),
    )(place, p, p, p, p, from_x, from_x, from_y, from_y)


def _exchange_stage2(name, send_a, send_b):
    def body(a_ref, b_ref, fa_ref, fb_ref, send_sems, recv_sems):
        x, y, c = _place()
        cps = [pltpu.make_async_remote_copy(src_ref=a_ref, dst_ref=fa_ref, send_sem=send_sems.at[0],
                                            recv_sem=recv_sems.at[0], device_id=(x, 1 - y, c), device_id_type=MESH),
               pltpu.make_async_remote_copy(src_ref=b_ref, dst_ref=fb_ref, send_sem=send_sems.at[1],
                                            recv_sem=recv_sems.at[1], device_id=(1 - x, y, c), device_id_type=MESH)]
        for cp in cps:
            cp.start()
        for cp in cps:
            cp.wait_recv()
        for cp in cps:
            cp.wait_send()

    sds = jax.ShapeDtypeStruct(send_a.shape, send_a.dtype)
    return pl.pallas_call(
        body, name=name, in_specs=[ANY, ANY], out_specs=[ANY, ANY], out_shape=[sds, sds],
        scratch_shapes=[pltpu.SemaphoreType.DMA((2,)), pltpu.SemaphoreType.DMA((2,))],
    )(send_a, send_b)


def _exchange_add2(name, keep_a, got_a, keep_b, got_b, c):
    rq = keep_a.shape[0]
    tr = _slab_tile(rq)

    def body(c_ref, ka, ga, kb, gb, o_ref):
        o_ref[0] = ka[...].astype(F32) + ga[...].astype(F32)
        o_ref[1] = kb[...].astype(F32) + gb[...].astype(F32)

    spec = pl.BlockSpec((tr, LANES), lambda i, c_ref: (i, 0))
    gs = pltpu.PrefetchScalarGridSpec(
        num_scalar_prefetch=1, grid=(rq // tr,), in_specs=[spec] * 4,
        out_specs=pl.BlockSpec((2, tr, LANES), lambda i, c_ref: (c_ref[0], i, 0)))
    out = pl.pallas_call(
        body, name=name, grid_spec=gs, out_shape=jax.ShapeDtypeStruct((4, rq, LANES), F32),
        compiler_params=pltpu.CompilerParams(dimension_semantics=("parallel",), vmem_limit_bytes=VMEM_LIMIT),
    )(c, keep_a, got_a, keep_b, got_b)
    return out.reshape(4 * rq, LANES)


def _join_halves(name, full):
    rh = full.shape[0] // 2

    def body(in_ref, o_ref, send_sem, recv_sem):
        x, y, c = _place()
        cp = pltpu.make_async_remote_copy(
            src_ref=in_ref.at[pl.ds(c * rh, rh), :], dst_ref=o_ref.at[pl.ds(c * rh, rh), :], send_sem=send_sem,
            recv_sem=recv_sem, device_id=(x, y, 1 - c), device_id_type=MESH)
        cp.start()
        pltpu.make_async_remote_copy(
            src_ref=in_ref.at[pl.ds(c * rh, rh), :], dst_ref=o_ref.at[pl.ds((1 - c) * rh, rh), :], send_sem=send_sem,
            recv_sem=recv_sem, device_id=(x, y, 1 - c), device_id_type=MESH).wait_recv()
        cp.wait_send()

    return pl.pallas_call(
        body, name=name, in_specs=[ANY], out_specs=ANY, out_shape=jax.ShapeDtypeStruct(full.shape, full.dtype),
        input_output_aliases={0: 0}, scratch_shapes=[pltpu.SemaphoreType.DMA, pltpu.SemaphoreType.DMA],
    )(full)


def _rows_of(n):
    return -(-n // LANES)


SLAB_ROW_ALIGN = 512


def _flat_rows(arrays, dtype):
    parts = []
    for a in arrays:
        flat = a.reshape(-1).astype(dtype)
        parts.append(jnp.pad(flat, (0, _rows_of(flat.size) * LANES - flat.size)))
    return jnp.concatenate(parts).reshape(-1, LANES)


def _align_rows(slab):
    rows = slab.shape[0]
    return jnp.pad(slab, ((0, -(-rows // SLAB_ROW_ALIGN) * SLAB_ROW_ALIGN - rows), (0, 0)))


def _pack(arrays, dtype):
    return _align_rows(_flat_rows(arrays, dtype))


def _unpack(slab, shapes):
    out, r = [], 0
    for shp in shapes:
        n = math.prod(shp)
        out.append(slab[r:r + _rows_of(n)].reshape(-1)[:n].reshape(shp))
        r += _rows_of(n)
    return out


def _unpack_gathered(g, shapes, kinds):
    out, r = [], 0
    for shp, kind in zip(shapes, kinds):
        n = math.prod(shp)
        blk = g[:, r:r + _rows_of(n)].reshape(4, -1)[:, :n].reshape((4,) + tuple(shp))
        r += _rows_of(n)
        if kind == "col":
            out.append(jnp.moveaxis(blk, 0, 1).reshape(shp[0], 4 * shp[1]))
        else:
            out.append(blk.reshape(4 * shp[0], shp[1]))
    return out


def _sections(g, kind, local_shape):
    if kind == "col":
        blocks = jnp.moveaxis(g.reshape(local_shape[0], 4, local_shape[1]), 1, 0)
    elif kind == "row":
        blocks = g.reshape((4,) + tuple(local_shape))
    else:
        blocks = jnp.broadcast_to(g, (4,) + tuple(g.shape))
    flat = blocks.reshape(4, -1)
    rows = _rows_of(flat.shape[1])
    return jnp.pad(flat, ((0, 0), (0, rows * LANES - flat.shape[1]))).reshape(4, rows, LANES)


def _rotary_tables(seq):
    half = RET_DK // 2
    pos = jnp.arange(seq, dtype=F32)
    inv = ROPE_THETA ** (-jnp.arange(half, dtype=F32) / half)
    ang = pos[:, None] * inv[None, :]
    cos, sin = jnp.cos(ang), jnp.sin(ang)
    return jnp.concatenate([cos, cos], axis=1), jnp.concatenate([-sin, sin], axis=1)


def _retention_tables():
    log_gamma = jnp.log(1.0 - 2.0 ** (-5.0 - jnp.arange(RET_HEADS, dtype=F32)))
    idx = jnp.arange(CHUNK, dtype=F32)
    diff = idx[:, None] - idx[None, :]
    dmask = jnp.exp(jnp.where((diff >= 0)[None], log_gamma[:, None, None] * diff[None], -jnp.inf))
    kdec = jnp.exp(log_gamma[None, :] * (CHUNK - 1.0 - idx)[:, None])
    qdec = jnp.exp(log_gamma[None, :] * (idx + 1.0)[:, None])
    cdec = jnp.exp(log_gamma * CHUNK)[None, :]
    lanes = lambda t: jnp.repeat(t, RET_DK, axis=1)
    return dmask.reshape(RET_HEADS * CHUNK, CHUNK), lanes(kdec), lanes(qdec), lanes(cdec)


def _s5_prep(a_re, a_im, log_step, b_re, b_im, c_re, c_im):
    g, n, c = S5_GROUPS, S5_STATE, S5_GROUP
    lam = lax.complex(a_re, a_im)
    step = jnp.exp(log_step)[:, None]
    lam_bar = jnp.exp(lam * step)
    b_bar = ((lam_bar - 1.0) / lam)[..., None] * lax.complex(b_re, b_im)
    eye = jnp.eye(g, dtype=F32)
    bb_re = (jnp.real(b_bar).transpose(0, 2, 1)[:, :, None, :] * eye[:, None, :, None]).reshape(g * c, g * n)
    bb_im = (jnp.imag(b_bar).transpose(0, 2, 1)[:, :, None, :] * eye[:, None, :, None]).reshape(g * c, g * n)
    cc_re = (c_re.transpose(0, 2, 1)[:, :, None, :] * eye[:, None, :, None]).reshape(g * n, g * c)
    cc_im = (c_im.transpose(0, 2, 1)[:, :, None, :] * eye[:, None, :, None]).reshape(g * n, g * c)
    return (jnp.real(lam_bar).reshape(1, g * n), jnp.imag(lam_bar).reshape(1, g * n),
            jnp.concatenate([bb_re, bb_im], axis=1), cc_re, cc_im)


def kernel(x, l0_norm_mix, l0_w_in, ssd_conv_w, ssd_conv_b, ssd_dt_bias, ssd_A_log, ssd_D, ssd_norm_w, l0_w_out, l0_norm_mlp, l0_w_up, l0_w_down, l1_norm_mix, l1_w_in, gdn_conv_w, gdn_A_log, gdn_dt_bias, gdn_norm_w, s5_A_re, s5_A_im, s5_log_step, s5_B_re, s5_B_im, s5_C_re, s5_C_im, s5_D, s5_w_glu, s5_b_glu, l1_w_out, l1_norm_mlp, l1_w_up, l1_w_down, final_norm, loss_target, m_l0_norm_mix, m_l0_w_in, m_ssd_conv_w, m_ssd_conv_b, m_ssd_dt_bias, m_ssd_A_log, m_ssd_D, m_ssd_norm_w, m_l0_w_out, m_l0_norm_mlp, m_l0_w_up, m_l0_w_down, m_l1_norm_mix, m_l1_w_in, m_gdn_conv_w, m_gdn_A_log, m_gdn_dt_bias, m_gdn_norm_w, m_s5_A_re, m_s5_A_im, m_s5_log_step, m_s5_B_re, m_s5_B_im, m_s5_C_re, m_s5_C_im, m_s5_D, m_s5_w_glu, m_s5_b_glu, m_l1_w_out, m_l1_norm_mlp, m_l1_w_up, m_l1_w_down, m_final_norm, v_l0_norm_mix, v_l0_w_in, v_ssd_conv_w, v_ssd_conv_b, v_ssd_dt_bias, v_ssd_A_log, v_ssd_D, v_ssd_norm_w, v_l0_w_out, v_l0_norm_mlp, v_l0_w_up, v_l0_w_down, v_l1_norm_mix, v_l1_w_in, v_gdn_conv_w, v_gdn_A_log, v_gdn_dt_bias, v_gdn_norm_w, v_s5_A_re, v_s5_A_im, v_s5_log_step, v_s5_B_re, v_s5_B_im, v_s5_C_re, v_s5_C_im, v_s5_D, v_s5_w_glu, v_s5_b_glu, v_l1_w_out, v_l1_norm_mlp, v_l1_w_up, v_l1_w_down, v_final_norm):
    given = dict(locals())
    names = [n for n, _ in PARAMS]
    kinds = dict(PARAMS)
    w = {n: given[n] for n in names}
    seq = x.shape[1]
    x0 = x.reshape(seq, D_MODEL)
    target = loss_target.reshape(seq, D_MODEL)

    gb = _all_gather_shards("gather_weights", _pack([w[n] for n in GATHER_BF16], _MXU_DTYPE))
    full = dict(zip(GATHER_BF16, _unpack_gathered(gb, [w[n].shape for n in GATHER_BF16],
                                                  [kinds[n] for n in GATHER_BF16])))
    gf = _all_gather_shards("gather_conv", _pack([w[n] for n in GATHER_F32], F32))
    full.update(zip(GATHER_F32, _unpack_gathered(gf, [w[n].shape for n in GATHER_F32],
                                                 [kinds[n] for n in GATHER_F32])))
    in0 = full["l0_w_in"].shape[1]
    w_in0 = jnp.pad(full["l0_w_in"], ((0, 0), (0, IN0_PAD - in0)))
    wi1 = full["l1_w_in"]
    in1 = wi1.shape[1]
    w_in1 = jnp.concatenate([wi1[:, :3072], wi1[:, 3084:in1], wi1[:, 3072:3084],
                             jnp.zeros((D_MODEL, IN1_PAD - in1), wi1.dtype)], axis=1)

    row = lambda a: a.reshape(1, -1)
    lanes64 = lambda a: jnp.repeat(a, SSD_HEAD_DIM).reshape(1, -1)

    h0 = _rmsnorm_fwd("norm_mix0", x0, row(w["l0_norm_mix"]))
    proj0 = _matmul("in_proj0", h0, w_in0, "nn")
    cos_t, sin_t = _rotary_tables(seq)
    ret_tabs = list(_retention_tables())
    ret_xs = [(proj0, 512, 0), (proj0, 512, 1), (proj0, 512, 2), (proj0, 512, 3)]
    ret_xt = [(cos_t, 128, 0), (sin_t, 128, 0)]
    ret_states = [(512, 128)]
    mixed0, ret_saved = _scan_fwd("ret_fwd", _f_ret, CHUNK, ret_tabs, [], ret_xs, ret_xt, ret_states, D_MODEL, 512, 0)
    expand = jnp.repeat(jnp.eye(128, SSD_HEADS, dtype=F32), SSD_HEAD_DIM, axis=1)
    ssd_consts = [full["ssd_conv_w"], row(w["ssd_conv_b"]), lanes64(w["ssd_dt_bias"]), lanes64(w["ssd_A_log"]),
                  lanes64(w["ssd_D"]), row(w["ssd_norm_w"])]
    ssd_xs = [(proj0, 512, 4), (proj0, 512, 5), (proj0, 256, 12), (proj0, 256, 13), (proj0, 128, 28)]
    ssd_states = [(8, 512), (8, 256), (8, 256), (512, 128)]
    mixed0, ssd_saved = _scan_fwd("ssd_fwd", _f_ssd, CHUNK, [expand], ssd_consts, ssd_xs, [], ssd_states,
                                  D_MODEL, 512, 1, y_alias=mixed0)
    x1, h1 = _matmul("out_proj0", mixed0, full["l0_w_out"], "nn", epi="add", epi_arr=x0, norm_w=row(w["l0_norm_mlp"]))
    u0 = _matmul("up0", h1, full["l0_w_up"], "nn", out_dtype=_MXU_DTYPE)
    x2, h2 = _matmul("down0", u0, full["l0_w_down"], "nn", a_pro="relu2", epi="add", epi_arr=x1,
                     norm_w=row(w["l1_norm_mix"]))

    proj1 = _matmul("in_proj1", h2, w_in1, "nn")
    p_alog = jnp.zeros((1, 128), F32).at[0, 6:12].set(w["gdn_A_log"])
    p_dtb = jnp.zeros((1, 128), F32).at[0, 6:12].set(w["gdn_dt_bias"])
    gdn_consts = [full["gdn_conv_w"], p_alog, p_dtb, row(w["gdn_norm_w"])]
    gdn_xs = [(proj1, 768, 0), (proj1, 768, 1), (proj1, 768, 2), (proj1, 768, 3), (proj1, 128, 26)]
    gdn_states = [(8, 768), (8, 768), (8, 768), (768, 256)]
    mixed1, gdn_saved = _scan_fwd("gdn_fwd", _f_gdn, CHUNK, [], gdn_consts, gdn_xs, [], gdn_states, D_MODEL, 768, 0)
    s5_args = (w["s5_A_re"], w["s5_A_im"], w["s5_log_step"], w["s5_B_re"], w["s5_B_im"], w["s5_C_re"], w["s5_C_im"])
    (lam_re, lam_im, bblk, cc_re, cc_im), s5_prep_vjp = jax.vjp(_s5_prep, *s5_args)
    s5_consts = [lam_re, lam_im, bblk, cc_re, cc_im, row(w["s5_D"]), full["s5_w_glu"].astype(F32), row(w["s5_b_glu"])]
    s5_xs = [(proj1, 256, 12)]
    s5_states = [(8, 1024), (8, 1024)]
    mixed1, s5_saved = _scan_fwd("s5_fwd", _f_s5, CHUNK, [], s5_consts, s5_xs, [], s5_states, D_MODEL, 256, 3,
                                 y_alias=mixed1)
    x3, h3 = _matmul("out_proj1", mixed1, full["l1_w_out"], "nn", epi="add", epi_arr=x2, norm_w=row(w["l1_norm_mlp"]))
    u1 = _matmul("up1", h3, full["l1_w_up"], "nn", out_dtype=_MXU_DTYPE)
    x4 = _matmul("down1", u1, full["l1_w_down"], "nn", a_pro="relu2", epi="add", epi_arr=x3)

    loss_part, dx4, d_final = _loss_head("loss_head", x4, row(w["final_norm"]), target)
    loss = lax.psum(loss_part[0, 0], ("x", "y", "c"))
    grads = {"final_norm": d_final.reshape(-1)}

    du1 = _matmul("down1_dx", dx4, full["l1_w_down"], "nt", out_dtype=_MXU_DTYPE, epi="drelu2", epi_arr=u1)
    grads["l1_w_down"] = _matmul("down1_dw", u1, dx4, "tn", a_pro="relu2")
    grads["l1_w_up"] = _matmul("up1_dw", h3, du1, "tn")
    dx3, dwn = _matmul("up1_dx", du1, full["l1_w_up"], "nt", epi="norm_bwd", epi_arr=dx4, norm_x=x3,
                       norm_w=row(w["l1_norm_mlp"]))
    grads["l1_norm_mlp"] = dwn.reshape(-1)
    grads["l1_w_out"] = _matmul("out_proj1_dw", mixed1, dx3, "tn")
    dmixed1 = _matmul("out_proj1_dx", dx3, full["l1_w_out"], "nt")

    def gdn_assemble(dx):
        dq, dk, dv, dz, dba = dx
        zeros = lambda n: jnp.zeros((dq.shape[0], n), F32)
        return jnp.concatenate([dq, dk, dv, dz, zeros(256), dba, zeros(IN1_PAD - 3456)], axis=1)

    dproj1, gdn_dc = _scan_bwd("gdn_bwd", _f_gdn, CHUNK, [], gdn_consts, gdn_xs, [], gdn_saved, gdn_states,
                               (dmixed1, 768, 0), IN1_PAD, IN1_PAD, 0, gdn_assemble)
    dproj1, s5_dc = _scan_bwd("s5_bwd", _f_s5, CHUNK, [], s5_consts, s5_xs, [], s5_saved, s5_states,
                              (dmixed1, 256, 3), IN1_PAD, 256, 12, lambda dx: dx[0], dx_alias=dproj1)
    grads["gdn_conv_w"] = gdn_dc[0]
    grads["gdn_A_log"] = gdn_dc[1][0, 6:12]
    grads["gdn_dt_bias"] = gdn_dc[2][0, 6:12]
    grads["gdn_norm_w"] = gdn_dc[3].reshape(-1)
    s5_pg = s5_prep_vjp(tuple(s5_dc[:5]))
    for n, gval in zip(("s5_A_re", "s5_A_im", "s5_log_step", "s5_B_re", "s5_B_im", "s5_C_re", "s5_C_im"), s5_pg):
        grads[n] = gval
    grads["s5_D"] = s5_dc[5].reshape(-1)
    grads["s5_w_glu"] = s5_dc[6]
    grads["s5_b_glu"] = s5_dc[7].reshape(-1)
    dwi1 = _matmul("in_proj1_dw", h2, dproj1, "tn")
    grads["l1_w_in"] = jnp.concatenate([dwi1[:, :3072], dwi1[:, 3328:3340], dwi1[:, 3072:3328]], axis=1)
    dx2, dwn = _matmul("in_proj1_dx", dproj1, w_in1, "nt", epi="norm_bwd", epi_arr=dx3, norm_x=x2,
                       norm_w=row(w["l1_norm_mix"]))
    grads["l1_norm_mix"] = dwn.reshape(-1)

    du0 = _matmul("down0_dx", dx2, full["l0_w_down"], "nt", out_dtype=_MXU_DTYPE, epi="drelu2", epi_arr=u0)
    grads["l0_w_down"] = _matmul("down0_dw", u0, dx2, "tn", a_pro="relu2")
    grads["l0_w_up"] = _matmul("up0_dw", h1, du0, "tn")
    dx1, dwn = _matmul("up0_dx", du0, full["l0_w_up"], "nt", epi="norm_bwd", epi_arr=dx2, norm_x=x1,
                       norm_w=row(w["l0_norm_mlp"]))
    grads["l0_norm_mlp"] = dwn.reshape(-1)
    grads["l0_w_out"] = _matmul("out_proj0_dw", mixed0, dx1, "tn")
    dmixed0 = _matmul("out_proj0_dx", dx1, full["l0_w_out"], "nt")
    dproj0, _ = _scan_bwd("ret_bwd", _f_ret, CHUNK, ret_tabs, [], ret_xs, ret_xt, ret_saved, ret_states,
                          (dmixed0, 512, 0), IN0_PAD, 2048, 0, lambda dx: jnp.concatenate(dx, axis=1))

    def ssd_assemble(dx):
        return jnp.concatenate(list(dx) + [jnp.zeros((dx[0].shape[0], 2048 - 1664), F32)], axis=1)

    dproj0, ssd_dc = _scan_bwd("ssd_bwd", _f_ssd, CHUNK, [expand], ssd_consts, ssd_xs, [], ssd_saved, ssd_states,
                               (dmixed0, 512, 1), IN0_PAD, 2048, 1, ssd_assemble, dx_alias=dproj0)
    heads = lambda a: a.reshape(SSD_HEADS, SSD_HEAD_DIM).sum(axis=1)
    grads["ssd_conv_w"] = ssd_dc[0]
    grads["ssd_conv_b"] = ssd_dc[1].reshape(-1)
    grads["ssd_dt_bias"] = heads(ssd_dc[2])
    grads["ssd_A_log"] = heads(ssd_dc[3])
    grads["ssd_D"] = heads(ssd_dc[4])
    grads["ssd_norm_w"] = ssd_dc[5].reshape(-1)
    grads["l0_w_in"] = _matmul("in_proj0_dw", h0, dproj0, "tn")[:, :in0]
    dx0, dwn = _matmul("in_proj0_dx", dproj0, w_in0, "nt", epi="norm_bwd", epi_arr=dx1, norm_x=x0,
                       norm_w=row(w["l0_norm_mix"]))
    grads["l0_norm_mix"] = dwn.reshape(-1)
    grad_x = dx0.reshape(x.shape)

    c_idx = lax.axis_index("c").astype(jnp.int32).reshape(1)
    small = SMALL_SHARDED + tuple(n for n in names if kinds[n] == "rep")
    order = LARGE + small
    parts = [_sections(grads[n].reshape(_full_shape(n, w, kinds)), kinds[n], w[n].shape) for n in order]
    used = sum(p.shape[1] for p in parts)
    parts.append(jnp.zeros((4, -(-used // SLAB_ROW_ALIGN) * SLAB_ROW_ALIGN - used, LANES), F32))
    gslab = jnp.concatenate(parts, axis=1)
    from_sibling = _swap_halves("grads_swap_halves", gslab)
    chip_sum = _add_halves("grads_add_sibling", gslab, from_sibling, c_idx)
    place = jnp.stack([lax.axis_index("x"), lax.axis_index("y")]).astype(jnp.int32)
    from_x, from_y = _exchange_stage1("grads_stage1", chip_sum)
    send_a, keep_a, send_b, keep_b = _exchange_add1("grads_add1", chip_sum, from_x, from_y, place)
    got_a, got_b = _exchange_stage2("grads_stage2", send_a, send_b)
    my_half = _exchange_add2("grads_add2", keep_a, got_a, keep_b, got_b, c_idx)
    gsum = _join_halves("grads_join_halves", my_half)
    grad = dict(zip(order, _unpack(gsum, [w[n].shape for n in order])))

    delta, new_m, new_v = {}, {}, {}
    for n in LARGE:
        delta[n], new_m[n], new_v[n] = _adamw("adamw_" + n, w[n], grad[n], given["m_" + n], given["v_" + n])
    first_small = sum(_rows_of(math.prod(w[n].shape)) for n in LARGE)
    small_shapes = [w[n].shape for n in small]

    def small_slab(arrays):
        rows = _flat_rows(arrays, F32)
        return jnp.pad(rows, ((0, gsum.shape[0] - first_small - rows.shape[0]), (0, 0)))

    res = _adamw("adamw_small", small_slab([w[n] for n in small]), gsum[first_small:],
                 small_slab([given["m_" + n] for n in small]), small_slab([given["v_" + n] for n in small]))
    for out, slab in zip((delta, new_m, new_v), res):
        out.update(zip(small, _unpack(slab, small_shapes)))
    return (loss, grad_x, *[grad[n] for n in names], *[delta[n] for n in names], *[new_m[n] for n in names],
            *[new_v[n] for n in names])


def _full_shape(name, w, kinds):
    shp = w[name].shape
    if kinds[name] == "col":
        return (shp[0], 4 * shp[1])
    if kinds[name] == "row":
        return (4 * shp[0],) + tuple(shp[1:])
    return shp
```

```python
import functools
import math

import jax
import jax.numpy as jnp
from jax import lax
from jax.experimental import pallas as pl
from jax.experimental.pallas import tpu as pltpu

F32 = jnp.float32
_MXU_DTYPE = jnp.bfloat16

D_MODEL = 1024
CHUNK = 64
EPS = 1e-6
RET_HEADS, RET_DK = 4, 128
ROPE_THETA = 10000.0
SSD_HEADS, SSD_HEAD_DIM = 8, 64
GDN_HEADS, GDN_DK = 6, 128
S5_GROUPS, S5_GROUP, S5_STATE = 16, 16, 64
ADAM_LR, ADAM_B1, ADAM_B2, ADAM_EPS, ADAM_WD, ADAM_STEP = 0.001, 0.9, 0.999, 1e-08, 0.01, 10

IN0_PAD = 4096
IN1_PAD = 3584
LANES = 1024
VMEM_LIMIT = 56 * 1024 * 1024
MESH = pl.DeviceIdType.MESH

PARAMS = (
    ("l0_norm_mix", "rep"), ("l0_w_in", "col"), ("ssd_conv_w", "col"), ("ssd_conv_b", "rep"),
    ("ssd_dt_bias", "rep"), ("ssd_A_log", "rep"), ("ssd_D", "rep"), ("ssd_norm_w", "rep"),
    ("l0_w_out", "row"), ("l0_norm_mlp", "rep"), ("l0_w_up", "col"), ("l0_w_down", "row"),
    ("l1_norm_mix", "rep"), ("l1_w_in", "col"), ("gdn_conv_w", "col"), ("gdn_A_log", "rep"),
    ("gdn_dt_bias", "rep"), ("gdn_norm_w", "rep"), ("s5_A_re", "rep"), ("s5_A_im", "rep"),
    ("s5_log_step", "rep"), ("s5_B_re", "rep"), ("s5_B_im", "rep"), ("s5_C_re", "rep"), ("s5_C_im", "rep"),
    ("s5_D", "rep"), ("s5_w_glu", "row"), ("s5_b_glu", "rep"), ("l1_w_out", "row"), ("l1_norm_mlp", "rep"),
    ("l1_w_up", "col"), ("l1_w_down", "row"), ("final_norm", "rep"),
)
GATHER_BF16 = ("l0_w_in", "l0_w_out", "l0_w_up", "l0_w_down", "l1_w_in", "l1_w_out", "l1_w_up", "l1_w_down", "s5_w_glu")
GATHER_F32 = ("ssd_conv_w", "gdn_conv_w")
LARGE = ("l0_w_up", "l0_w_down", "l1_w_up", "l1_w_down", "l0_w_out", "l1_w_out", "l0_w_in", "l1_w_in")
SLAB_DIRECT = 6
SMALL_SHARDED = ("s5_w_glu", "ssd_conv_w", "gdn_conv_w")


def _dg(a, b, ca, cb, prec=None):
    return lax.dot_general(a, b, (((ca,), (cb,)), ((), ())), preferred_element_type=F32, precision=prec)


def _lo(a):
    return a.astype(_MXU_DTYPE)


@jax.custom_vjp
def _mm(a, b):
    return _dg(_lo(a), _lo(b), 1, 0)


def _mm_fwd(a, b):
    return _mm(a, b), (a, b)


def _mm_bwd(res, g):
    a, b = res
    return _dg(_lo(g), _lo(b), 1, 1), _dg(_lo(a), _lo(g), 0, 0)


_mm.defvjp(_mm_fwd, _mm_bwd)


@jax.custom_vjp
def _mm_nt(a, b):
    return _dg(_lo(a), _lo(b), 1, 1)


def _mm_nt_fwd(a, b):
    return _mm_nt(a, b), (a, b)


def _mm_nt_bwd(res, g):
    a, b = res
    return _dg(_lo(g), _lo(b), 1, 0), _dg(_lo(g), _lo(a), 0, 0)


_mm_nt.defvjp(_mm_nt_fwd, _mm_nt_bwd)


@jax.custom_vjp
def _mm_tn(a, b):
    return _dg(_lo(a), _lo(b), 0, 0)


def _mm_tn_fwd(a, b):
    return _mm_tn(a, b), (a, b)


def _mm_tn_bwd(res, g):
    a, b = res
    return _dg(_lo(b), _lo(g), 1, 1), _dg(_lo(a), _lo(g), 1, 0)


_mm_tn.defvjp(_mm_tn_fwd, _mm_tn_bwd)


def _split2(x):
    hi = _lo(x)
    return hi, _lo(x - hi.astype(F32))


def _split3(x):
    h1 = _lo(x)
    r1 = x - h1.astype(F32)
    h2 = _lo(r1)
    return h1, h2, _lo(r1 - h2.astype(F32))


def _tri_cum_dir(m, ca):
    n, w = m.shape
    causal, _ = _tri_masks(n)
    out = _dg(causal.astype(_MXU_DTYPE), jnp.concatenate(_split3(m), axis=1), ca, 0)
    return out[:, :w] + out[:, w:2 * w] + out[:, 2 * w:]


@jax.custom_vjp
def _tri_cum(m):
    return _tri_cum_dir(m, 1)


def _tri_cum_fwd(m):
    return _tri_cum_dir(m, 1), None


def _tri_cum_bwd(_, g):
    return (_tri_cum_dir(g, 0),)


_tri_cum.defvjp(_tri_cum_fwd, _tri_cum_bwd)


@jax.custom_vjp
def _mm_exact_rhs(a, e):
    return _dg(jnp.concatenate(_split3(a), axis=1), jnp.concatenate([_lo(e)] * 3, axis=0), 1, 0)


def _mm_exact_rhs_fwd(a, e):
    return _mm_exact_rhs(a, e), e


def _mm_exact_rhs_bwd(e, g):
    return _dg(jnp.concatenate(_split3(g), axis=1), jnp.concatenate([_lo(e)] * 3, axis=1), 1, 1), jnp.zeros_like(e)


_mm_exact_rhs.defvjp(_mm_exact_rhs_fwd, _mm_exact_rhs_bwd)


def _bd(x):
    left = _iota(x.shape, 1) < (x.shape[1] // 2)
    zero = jnp.zeros_like(x)
    return jnp.concatenate([jnp.where(left, x, zero), jnp.where(left, zero, x)], axis=0)


def _unbd(m):
    half = m.shape[0] // 2
    left = _iota((half, m.shape[1]), 1) < (m.shape[1] // 2)
    return jnp.where(left, m[:half], m[half:])


def _pmm_nn(x, y):
    xh, xl = _split2(x)
    yh, yl = _split2(y)
    return _dg(jnp.concatenate([xh, xl, xh], axis=1), jnp.concatenate([_bd(yh), _bd(yh), _bd(yl)], axis=0), 1, 0)


def _pmm_nt(x, y):
    xh, xl = _split2(x)
    yh, yl = _split2(y)
    return _dg(jnp.concatenate([xh, xl, xh], axis=1), jnp.concatenate([_bd(yh), _bd(yh), _bd(yl)], axis=1), 1, 1)


def _pmm_tn(x, y):
    xh, xl = _split2(x)
    yh, yl = _split2(y)
    return _unbd(_dg(jnp.concatenate([xh, xl, xh], axis=0), jnp.concatenate([yh, yh, yl], axis=0), 0, 0))


@functools.lru_cache(maxsize=None)
def _shift(s, axis):
    @jax.custom_vjp
    def sh(x):
        return pltpu.roll(x, s, axis)

    def fwd(x):
        return sh(x), None

    def bwd(_, g):
        n = g.shape[axis]
        return (pltpu.roll(g, (n - s) % n, axis),)

    sh.defvjp(fwd, bwd)
    return sh


def _iota(shape, axis):
    return lax.broadcasted_iota(jnp.int32, shape, axis)


def _silu(x):
    return x * jax.nn.sigmoid(x)


def _unit_rms(x):
    return x * lax.rsqrt(jnp.mean(x * x, axis=-1, keepdims=True) + EPS)


def _l2norm(x):
    return x * lax.rsqrt(jnp.sum(x * x, axis=-1, keepdims=True) + EPS)


def _tri_masks(n):
    r, c = _iota((n, n), 0), _iota((n, n), 1)
    return r >= c, r > c


def _packed_rc():
    return _iota((CHUNK, 2 * CHUNK), 0), _iota((CHUNK, 2 * CHUNK), 1) & (CHUNK - 1)


def _decay_packed(g_packed):
    r, c = _packed_rc()
    seg = _tri_cum(g_packed * (r > c).astype(F32))
    return jnp.where(r >= c, jnp.exp(jnp.where(r >= c, seg, 0.0)), 0.0)


def _conv(x, tail, w):
    rows, width = x.shape
    row = _iota((rows, width), 0)
    acc = x * w[3:4, :]
    pad = jnp.zeros((rows - 8, width), F32)
    for j in range(3):
        s = 3 - j
        prev = jnp.concatenate([_shift(s, 0)(tail), pad], axis=0)
        acc = acc + w[j:j + 1, :] * jnp.where(row < s, prev, _shift(s, 0)(x))
    return acc


def _tri_inv_impl(mats):
    r, c = _packed_rc()
    eye = (r == c).astype(F32)

    def same_block(b):
        return (r // b) == (c // b)

    a8 = [jnp.where(same_block(8), a, 0.0) for a in mats]
    a2 = [_pmm_nn(t, t) for t in a8]
    a4 = [_pmm_nn(t, t) for t in a2]
    x = [_pmm_nn(eye - p, eye + q) for p, q in zip(a8, a2)]
    x = [_pmm_nn(p, eye + q) for p, q in zip(x, a4)]
    for b in (8, 16, 32):
        off = [jnp.where(same_block(2 * b) & jnp.logical_not(same_block(b)), a, 0.0) for a in mats]
        y = [_pmm_nn(p, q) for p, q in zip(x, off)]
        x = [p - _pmm_nn(q, p) for p, q in zip(x, y)]
    return x


@jax.custom_vjp
def _tri_inv(mats):
    return _tri_inv_impl(mats)


def _tri_inv_fwd(mats):
    t = _tri_inv_impl(mats)
    return t, t


def _tri_inv_bwd(t, g):
    m1 = [_pmm_tn(p, q) for p, q in zip(t, g)]
    return ([-_pmm_nt(p, q) for p, q in zip(m1, t)],)


_tri_inv.defvjp(_tri_inv_fwd, _tri_inv_bwd)


def _f_ret(tabs, consts, xs, xtabs, states):
    dmask, kdec, qdec, cdec = tabs
    q, k, v, gate = xs
    cs, sn = xtabs
    (st,) = states
    swap = _shift(RET_DK // 2, 1)
    heads = range(RET_HEADS)
    sls = [slice(128 * h, 128 * h + 128) for h in heads]
    qh = [(q[:, sl] * cs + swap(q[:, sl]) * sn) * (RET_DK ** -0.5) for sl in sls]
    kh = [k[:, sl] * cs + swap(k[:, sl]) * sn for sl in sls]
    sh = [st[sl, :] for sl in sls]
    scores = [_mm_nt(a, b) * dmask[64 * h:64 * h + 64, :] for h, a, b in zip(heads, qh, kh)]
    y = [_mm(s, v[:, sl]) for s, sl in zip(scores, sls)]
    y = [t + _mm(a * qdec[:, sl], s) for t, a, sl, s in zip(y, qh, sls, sh)]
    new = [s * cdec[:, sl] + _mm_tn(b * kdec[:, sl], v[:, sl]) for s, sl, b in zip(sh, sls, kh)]
    outs = [_silu(gate[:, sl]) * _unit_rms(t) for sl, t in zip(sls, y)]
    return (jnp.concatenate(outs, axis=1),), [jnp.concatenate(new, axis=0)]


def _f_ssd(tabs, consts, xs, xtabs, states):
    (expand,) = tabs
    conv_w, conv_b, dtb, alog, dskip, nw = consts
    z, xr, br, cr, dtr = xs
    tx, tb, tc, st = states
    xc = _silu(_conv(xr, tx, conv_w[:, 0:512]) + conv_b[:, 0:512])
    bc = _silu(_conv(br, tb, conv_w[:, 512:768]) + conv_b[:, 512:768])
    cc = _silu(_conv(cr, tc, conv_w[:, 768:1024]) + conv_b[:, 768:1024])
    dt = jax.nn.softplus(_mm_exact_rhs(dtr, expand) + dtb)
    la = dt * (-jnp.exp(alog))
    lacum = _tri_cum(la)
    total = jnp.sum(la, axis=0, keepdims=True)
    xd = xc * dt
    dte, ecum, cdec = jnp.exp(total - lacum), jnp.exp(lacum), jnp.exp(total)
    pairs = range(SSD_HEADS // 2)
    sls = [slice(128 * p, 128 * p + 128) for p in pairs]
    bg = [bc[:, 128 * g:128 * g + 128] for g in range(2)]
    cg = [cc[:, 128 * g:128 * g + 128] for g in range(2)]
    cb2 = [_mm_nt(c, jnp.concatenate([b, b], axis=0)) for b, c in zip(bg, cg)]
    lm = [_decay_packed(la[:, sl]) for sl in sls]
    sp = [st[sl, :] for sl in sls]
    ys = [_mm(cg[p // 2], sp[p]) * ecum[:, sls[p]] for p in pairs]
    ys = [ys[p] + _mm(cb2[p // 2] * lm[p], _bd(xd[:, sls[p]])) for p in pairs]
    new = [sp[p] * cdec[:, sls[p]] + _mm_tn(bg[p // 2], xd[:, sls[p]] * dte[:, sls[p]]) for p in pairs]
    y = jnp.concatenate(ys, axis=1) + dskip * xc
    yg = y * _silu(z)
    out = jnp.concatenate([_unit_rms(yg[:, 0:256]), _unit_rms(yg[:, 256:512])], axis=1) * nw
    return (out,), [xr[CHUNK - 8:, :], br[CHUNK - 8:, :], cr[CHUNK - 8:, :], jnp.concatenate(new, axis=0)]


def _f_gdn(tabs, consts, xs, xtabs, states):
    conv_w, p_alog, p_dtb, nw = consts
    qr, kr, vr, z, ba = xs
    tq, tk, tv, st = states
    qc = _silu(_conv(qr, tq, conv_w[:, 0:768]))
    kc = _silu(_conv(kr, tk, conv_w[:, 768:1536]))
    vc = _silu(_conv(vr, tv, conv_w[:, 1536:2304]))
    gl = -jnp.exp(p_alog) * jax.nn.softplus(ba + p_dtb)
    bl = jax.nn.sigmoid(ba)
    gcum = _tri_cum(gl)
    left128 = _iota((CHUNK, 128), 1) < 64
    left256 = _iota((CHUNK, 256), 1) < 128
    r, c = _packed_rc()
    diag_blocks = (_iota((256, 256), 0) < 128) == (_iota((256, 256), 1) < 128)

    def norm2(t):
        return jnp.concatenate([_l2norm(t[:, 0:128]), _l2norm(t[:, 128:256])], axis=1)

    def pick(arr, off, left, p):
        return jnp.where(left, arr[:, off + 2 * p:off + 2 * p + 1], arr[:, off + 2 * p + 1:off + 2 * p + 2])

    pairs = range(GDN_HEADS // 2)
    sls = [slice(256 * p, 256 * p + 256) for p in pairs]
    qn = [norm2(qc[:, sl]) * (GDN_DK ** -0.5) for sl in sls]
    kn = [norm2(kc[:, sl]) for sl in sls]
    dec = [_decay_packed(pick(gl, 6, left128, p)) for p in pairs]
    g2 = [pick(gl, 6, left256, p) for p in pairs]
    gc2 = [pick(gcum, 6, left256, p) for p in pairs]
    b2 = [pick(bl, 0, left256, p) for p in pairs]
    tot = [jnp.sum(t, axis=0, keepdims=True) for t in g2]
    eg = [jnp.exp(t) for t in gc2]
    et = [jnp.exp(t - s) for t, s in zip(tot, gc2)]
    cd = [jnp.exp(t) for t in tot]
    kb = [k * b for k, b in zip(kn, b2)]
    vb = [vc[:, sl] * b for sl, b in zip(sls, b2)]
    kbd = [_bd(k) for k in kn]
    tm = _tri_inv([jnp.where(r > c, _mm_nt(a, b) * d, 0.0) for a, b, d in zip(kb, kbd, dec)])
    u = [_mm(t, _bd(v)) for t, v in zip(tm, vb)]
    w = [_mm(t, _bd(k * e)) for t, k, e in zip(tm, kb, eg)]
    attn = [_mm_nt(q, k) * d for q, k, d in zip(qn, kbd, dec)]
    sp = [st[sl, :] for sl in sls]
    vn = [a - _mm(b, s) for a, b, s in zip(u, w, sp)]
    o = [_mm(q * e, s) + _mm(a, _bd(v)) for q, e, s, a, v in zip(qn, eg, sp, attn, vn)]
    new = [s * d + jnp.where(diag_blocks, _mm_tn(k * e, v), 0.0) for s, d, k, e, v in zip(sp, cd, kn, et, vn)]
    outs = []
    for p in pairs:
        for hh in range(2):
            osl = slice(128 * hh, 128 * hh + 128)
            zsl = slice(256 * p + 128 * hh, 256 * p + 128 * hh + 128)
            outs.append(_unit_rms(o[p][:, osl]) * nw * _silu(z[:, zsl]))
    return (jnp.concatenate(outs, axis=1),), [qr[CHUNK - 8:, :], kr[CHUNK - 8:, :], vr[CHUNK - 8:, :],
                                             jnp.concatenate(new, axis=0)]


def _f_s5(tabs, consts, xs, xtabs, states):
    lam_re, lam_im, bblk, c_re, c_im, dskip, wglu, bglu = consts
    (u,) = xs
    s_re, s_im = states
    rows = u.shape[0]
    n = lam_re.shape[1]
    bu = _mm(u, bblk)
    hr, hi = bu[:, 0:n], bu[:, n:2 * n]
    row = _iota((rows, n), 0)
    h0r, h0i = s_re[0:1, :], s_im[0:1, :]
    hr = hr + jnp.where(row == 0, lam_re * h0r - lam_im * h0i, 0.0)
    hi = hi + jnp.where(row == 0, lam_re * h0i + lam_im * h0r, 0.0)
    pr, pi = lam_re, lam_im
    d = 1
    while d < rows:
        sr = jnp.where(row >= d, _shift(d, 0)(hr), 0.0)
        si = jnp.where(row >= d, _shift(d, 0)(hi), 0.0)
        hr, hi = hr + pr * sr - pi * si, hi + pr * si + pi * sr
        pr, pi = pr * pr - pi * pi, 2.0 * pr * pi
        d *= 2
    y = _mm(hr, c_re) - _mm(hi, c_im) + dskip * u
    y = jax.nn.gelu(y)
    out = y * jax.nn.sigmoid(_mm(y, wglu) + bglu)
    last_r = jnp.broadcast_to(hr[rows - 1:rows, :], (8, n))
    last_i = jnp.broadcast_to(hi[rows - 1:rows, :], (8, n))
    return (out,), [last_r, last_i]


def _full_spec(a):
    nd = a.ndim
    return pl.BlockSpec(a.shape, lambda i, _nd=nd: (0,) * _nd)


CHUNKS_PER_STEP = 4


def _chunks_per_step(f, rows, n):
    def g(tabs, consts, xs, xtabs, states):
        ys = []
        for i in range(n):
            sl = slice(rows * i, rows * (i + 1))
            (y,), states = f(tabs, consts, [t[sl] for t in xs], [t[sl] for t in xtabs], states)
            ys.append(y)
        return (jnp.concatenate(ys, axis=0),), states

    return g


def _scan_fwd(name, f, rows, tabs, consts, xs, xtabs, state_shapes, y_total, y_width, y_cb, y_alias=None):
    seq = xs[0][0].shape[0]
    per_step = math.gcd(CHUNKS_PER_STEP, seq // rows)
    f = _chunks_per_step(f, rows, per_step)
    rows = rows * per_step
    nc = seq // rows
    nt, ncst, nx, nxt, ns = len(tabs), len(consts), len(xs), len(xtabs), len(state_shapes)
    alias = y_alias is not None

    def body(*refs):
        p = 0
        tab_r = refs[p:p + nt]; p += nt
        c_r = refs[p:p + ncst]; p += ncst
        x_r = refs[p:p + nx]; p += nx
        xt_r = refs[p:p + nxt]; p += nxt
        if alias:
            p += 1
        y_ref = refs[p]; p += 1
        sv_r = refs[p:p + ns]; p += ns
        st_r = refs[p:p + ns]

        @pl.when(pl.program_id(0) == 0)
        def _():
            for s in st_r:
                s[...] = jnp.zeros(s.shape, F32)

        st = [s[...] for s in st_r]
        for r, v in zip(sv_r, st):
            r[...] = v
        (y,), new = f([r[...] for r in tab_r], [r[...] for r in c_r], [r[...].astype(F32) for r in x_r],
                      [r[...] for r in xt_r], st)
        y_ref[...] = y.astype(y_ref.dtype)
        for s, v in zip(st_r, new):
            s[...] = v

    win = [pl.BlockSpec((rows, w), lambda i, _cb=cb: (i, _cb)) for (_, w, cb) in list(xs) + list(xtabs)]
    in_specs = [_full_spec(a) for a in list(tabs) + list(consts)] + win
    args = list(tabs) + list(consts) + [a for (a, _, _) in list(xs) + list(xtabs)]
    io_alias = {}
    if alias:
        in_specs.append(pl.BlockSpec(memory_space=pl.ANY))
        io_alias = {len(args): 0}
        args.append(y_alias)
    out_shape = [jax.ShapeDtypeStruct((seq, y_total), _MXU_DTYPE)]
    out_specs = [pl.BlockSpec((rows, y_width), lambda i: (i, y_cb))]
    for (r, c) in state_shapes:
        out_shape.append(jax.ShapeDtypeStruct((nc * r, c), F32))
        out_specs.append(pl.BlockSpec((r, c), lambda i: (i, 0)))
    res = pl.pallas_call(
        body, name=name, grid=(nc,), in_specs=in_specs, out_specs=out_specs, out_shape=out_shape,
        scratch_shapes=[pltpu.VMEM(s, F32) for s in state_shapes], input_output_aliases=io_alias,
        compiler_params=pltpu.CompilerParams(dimension_semantics=("arbitrary",), vmem_limit_bytes=VMEM_LIMIT),
    )(*args)
    return res[0], list(res[1:])


def _scan_bwd(name, f, rows, tabs, consts, xs, xtabs, saved, state_shapes, dy, dx_total, dx_width, dx_cb,
              assemble, dx_alias=None):
    seq = xs[0][0].shape[0]
    per_step = math.gcd(CHUNKS_PER_STEP, seq // rows)
    f = _chunks_per_step(f, rows, per_step)
    rows = rows * per_step
    nc = seq // rows
    nt, ncst, nx, nxt, ns = len(tabs), len(consts), len(xs), len(xtabs), len(state_shapes)
    alias = dx_alias is not None

    def body(*refs):
        p = 0
        tab_r = refs[p:p + nt]; p += nt
        c_r = refs[p:p + ncst]; p += ncst
        x_r = refs[p:p + nx]; p += nx
        xt_r = refs[p:p + nxt]; p += nxt
        sv_r = refs[p:p + ns]; p += ns
        dy_ref = refs[p]; p += 1
        if alias:
            p += 1
        dx_ref = refs[p]; p += 1
        dc_r = refs[p:p + ncst]; p += ncst
        ds_r = refs[p:p + ns]

        @pl.when(pl.program_id(0) == 0)
        def _():
            for s in ds_r:
                s[...] = jnp.zeros(s.shape, F32)
            for r in dc_r:
                r[...] = jnp.zeros(r.shape, F32)

        tab_v = [r[...] for r in tab_r]
        xt_v = [r[...] for r in xt_r]

        def g(c, x, s):
            (y,), new = f(tab_v, c, x, xt_v, s)
            return y, new

        _, vjp = jax.vjp(g, [r[...] for r in c_r], [r[...].astype(F32) for r in x_r], [r[...] for r in sv_r])
        dc, dx, ds = vjp((dy_ref[...], [s[...] for s in ds_r]))
        dx_ref[...] = assemble(dx).astype(dx_ref.dtype)
        for r, v in zip(dc_r, dc):
            r[...] += v
        for s, v in zip(ds_r, ds):
            s[...] = v

    win = [pl.BlockSpec((rows, w), lambda j, _cb=cb: (nc - 1 - j, _cb)) for (_, w, cb) in list(xs) + list(xtabs)]
    in_specs = [_full_spec(a) for a in list(tabs) + list(consts)] + win
    args = list(tabs) + list(consts) + [a for (a, _, _) in list(xs) + list(xtabs)]
    for (r, c), sv in zip(state_shapes, saved):
        in_specs.append(pl.BlockSpec((r, c), lambda j: (nc - 1 - j, 0)))
        args.append(sv)
    in_specs.append(pl.BlockSpec((rows, dy[1]), lambda j: (nc - 1 - j, dy[2])))
    args.append(dy[0])
    io_alias = {}
    if alias:
        in_specs.append(pl.BlockSpec(memory_space=pl.ANY))
        io_alias = {len(args): 0}
        args.append(dx_alias)
    out_shape = [jax.ShapeDtypeStruct((seq, dx_total), _MXU_DTYPE)] +[jax.ShapeDtypeStruct(a.shape, F32) for a in consts]
    out_specs = [pl.BlockSpec((rows, dx_width), lambda j: (nc - 1 - j, dx_cb))] + [_full_spec(a) for a in consts]
    res = pl.pallas_call(
        body, name=name, grid=(nc,), in_specs=in_specs, out_specs=out_specs, out_shape=out_shape,
        scratch_shapes=[pltpu.VMEM(s, F32) for s in state_shapes], input_output_aliases=io_alias,
        compiler_params=pltpu.CompilerParams(dimension_semantics=("arbitrary",), vmem_limit_bytes=VMEM_LIMIT),
    )(*args)
    return res[0], list(res[1:])


def _tile(n, want):
    t = min(n, want)
    while n % t:
        t //= 2
    return t


MATMUL_VMEM_BUDGET = 40 * 1024 * 1024


MXU_FLOPS_PER_S = 8.5e14
HBM_BYTES_PER_S = 2.8e12
GRID_STEP_S = 0.35e-6


def _pick_tiles(m, n, k, sa, sb, so, se, whole_rows=False, reduce_rows=False, tn_fixed=None, tm_divides=None):
    best = None
    tns = {tn_fixed} if tn_fixed else ({n} if whole_rows else {_tile(n, t) for t in (4096, 2048, 1024, 512)})
    tms = {_tile(m, t) for t in (2048, 1024, 512)}
    if tm_divides:
        tms = {t for t in (1024, 512, 256) if tm_divides % t == 0 and m % t == 0}
    for tn in tns:
        for tm in tms:
            for tk in {_tile(k, t) for t in (4096, 2048, 1024, 512)}:
                at, bt, ot = tm * tk * sa, tk * tn * sb, tm * tn * so
                need = 2 * (at + bt + ot + tm * tn * se) + 2 * tm * tn * 4 + (at if sa == 4 else 0) + (bt if sb == 4 else 0)
                if need > MATMUL_VMEM_BUDGET:
                    continue
                ni, nj, nk = m // tm, n // tn, k // tk
                b_reads = ni if (reduce_rows or nk > 1) else 1
                moved = m * k * sa * nj + k * n * sb * b_reads + m * n * (so + se)
                cost = max(2 * m * n * k / MXU_FLOPS_PER_S, moved / HBM_BYTES_PER_S) + ni * nj * nk * GRID_STEP_S
                key = (cost, nk, -tm)
                if best is None or key < best[0]:
                    best = (key, (tm, tn, tk))
    assert best is not None, (m, n, k)
    return best[1]


def _matmul(name, a, b, mode, out_dtype=F32, a_pro=None, epi=None, epi_arr=None, norm_w=None, norm_x=None, slab=None):
    if mode == "nn":
        (m, k), (k2, n) = a.shape, b.shape
    elif mode == "nt":
        (m, k), (n, k2) = a.shape, b.shape
    else:
        (k, m), (k2, n) = a.shape, b.shape
    assert k == k2, (name, a.shape, b.shape)
    size = lambda t: jnp.dtype(t).itemsize
    rows_in = [] if epi is None else [epi_arr] + ([norm_x] if epi == "norm_bwd" else [])
    emit_norm = epi == "add" and norm_w is not None
    extra = sum(size(t.dtype) for t in rows_in) + (size(_MXU_DTYPE) if emit_norm else 0)
    if slab is None:
        tm, tn, tk = _pick_tiles(m, n, k, size(a.dtype), size(b.dtype), size(out_dtype), extra,
                                 whole_rows=norm_w is not None, reduce_rows=mode == "tn")
    else:
        prev_slab, slab_rows, first_row, shard_rows = slab
        assert mode == "tn" and epi is None and n % LANES == 0, name
        tm, tn, tk = _pick_tiles(m, n, k, size(a.dtype), size(b.dtype), size(out_dtype), extra, reduce_rows=True,
                                 tn_fixed=LANES, tm_divides=math.gcd(shard_rows or m, first_row or m))
    nk = k // tk
    ca, cb = {"nn": (1, 0), "nt": (1, 1), "tn": (0, 0)}[mode]
    n_in = 2 + len(rows_in) + (norm_w is not None) + (slab is not None and slab[0] is not None)
    n_out = 2 if (emit_norm or epi == "norm_bwd") else 1

    def body(*refs):
        refs = list(refs)
        acc = refs.pop() if nk > 1 else None
        a_ref, b_ref = refs[0], refs[1]
        e_ref = refs[2] if epi is not None else None
        x_ref = refs[3] if epi == "norm_bwd" else None
        w_ref = refs[n_in - 1] if norm_w is not None else None
        o_ref = refs[n_in]
        o2_ref = refs[n_in + 1] if n_out == 2 else None
        kk = pl.program_id(2)

        if epi == "norm_bwd":
            @pl.when((pl.program_id(1) == 0) & (kk == 0))
            def _():
                o2_ref[...] = jnp.zeros(o2_ref.shape, F32)

        av = a_ref[...]
        if a_pro == "relu2":
            r = jnp.maximum(av, 0.0)
            av = r * r
        part = _dg(_lo(av), _lo(b_ref[...]), ca, cb)

        def finish(r):
            if epi == "add":
                r = r + e_ref[...]
                if emit_norm:
                    o2_ref[...] = (_unit_rms(r) * w_ref[...]).astype(_MXU_DTYPE)
            elif epi == "drelu2":
                r = r * (2.0 * jnp.maximum(e_ref[...], 0.0))
            elif epi == "norm_bwd":
                xv = x_ref[...]
                rstd = lax.rsqrt(jnp.mean(xv * xv, axis=-1, keepdims=True) + EPS)
                xh = xv * rstd
                g = r * w_ref[...]
                o2_ref[...] += jnp.sum(r * xh, axis=0, keepdims=True)
                r = e_ref[...] + rstd * (g - xh * jnp.mean(g * xh, axis=-1, keepdims=True))
            o_ref[...] = r.astype(out_dtype).reshape(o_ref.shape)

        if nk == 1:
            finish(part)
        else:
            @pl.when(kk == 0)
            def _():
                acc[...] = part

            @pl.when(kk > 0)
            def _():
                acc[...] += part

            @pl.when(kk == nk - 1)
            def _():
                finish(acc[...])

    if mode == "tn":
        a_spec = pl.BlockSpec((tk, tm), lambda j, i, kk: (kk, i))
    else:
        a_spec = pl.BlockSpec((tm, tk), lambda j, i, kk: (i, kk))
    if mode == "nt":
        b_spec = pl.BlockSpec((tn, tk), lambda j, i, kk: (j, kk))
    else:
        b_spec = pl.BlockSpec((tk, tn), lambda j, i, kk: (kk, j))
    o_spec = pl.BlockSpec((tm, tn), lambda j, i, kk: (i, j))
    vec_spec = pl.BlockSpec((1, tn), lambda j, i, kk: (0, j))
    in_specs, args = [a_spec, b_spec] + [o_spec] * len(rows_in), [a, b] + rows_in
    if norm_w is not None:
        in_specs.append(vec_spec)
        args.append(norm_w)
    out_specs, out_shape = [o_spec], [jax.ShapeDtypeStruct((m, n), out_dtype)]
    io_alias = {}
    if slab is not None:
        first_blk = first_row // tm
        if shard_rows is None:
            out_specs = [pl.BlockSpec((1, tm, tn), lambda j, i, kk: (j, first_blk + i, 0))]
        else:
            per = shard_rows // tm
            out_specs = [pl.BlockSpec((1, tm, tn), lambda j, i, kk: (i // per, first_blk + i % per, 0))]
        out_shape = [jax.ShapeDtypeStruct((4, slab_rows, LANES), out_dtype)]
        if prev_slab is not None:
            in_specs.append(pl.BlockSpec(memory_space=pl.ANY))
            io_alias = {len(args): 0}
            args.append(prev_slab)
    if emit_norm:
        out_specs.append(o_spec)
        out_shape.append(jax.ShapeDtypeStruct((m, n), _MXU_DTYPE))
    elif epi == "norm_bwd":
        out_specs.append(vec_spec)
        out_shape.append(jax.ShapeDtypeStruct((1, n), F32))
    sem = ("parallel", "arbitrary" if epi == "norm_bwd" else "parallel", "arbitrary")
    res = pl.pallas_call(
        body, name=name, grid=(n // tn, m // tm, nk), in_specs=in_specs, out_specs=out_specs, out_shape=out_shape,
        scratch_shapes=[pltpu.VMEM((tm, tn), F32)] if nk > 1 else [], input_output_aliases=io_alias,
        compiler_params=pltpu.CompilerParams(dimension_semantics=sem, vmem_limit_bytes=VMEM_LIMIT),
    )(*args)
    return res[0] if n_out == 1 else res


ROW_TILE = 512


def _rmsnorm_fwd(name, x, w):
    seq, d = x.shape
    tr = _tile(seq, ROW_TILE)

    def body(x_ref, w_ref, o_ref):
        xv = x_ref[...]
        o_ref[...] = (_unit_rms(xv) * w_ref[...]).astype(_MXU_DTYPE)

    return pl.pallas_call(
        body, name=name, grid=(seq // tr,),
        in_specs=[pl.BlockSpec((tr, d), lambda i: (i, 0)), pl.BlockSpec((1, d), lambda i: (0, 0))],
        out_specs=pl.BlockSpec((tr, d), lambda i: (i, 0)), out_shape=jax.ShapeDtypeStruct((seq, d), _MXU_DTYPE),
        compiler_params=pltpu.CompilerParams(dimension_semantics=("parallel",), vmem_limit_bytes=VMEM_LIMIT),
    )(x, w)


def _loss_head(name, x, w, target):
    seq, d = x.shape
    tr = _tile(seq, ROW_TILE)

    def body(x_ref, w_ref, t_ref, loss_ref, dx_ref, dw_ref):
        @pl.when(pl.program_id(0) == 0)
        def _():
            dw_ref[...] = jnp.zeros(dw_ref.shape, F32)
            loss_ref[...] = jnp.zeros(loss_ref.shape, F32)

        xv = x_ref[...]
        rstd = lax.rsqrt(jnp.mean(xv * xv, axis=-1, keepdims=True) + EPS)
        xh = xv * rstd
        err = xh * w_ref[...] - t_ref[...]
        per_row = jnp.mean(err * err, axis=-1, keepdims=True)
        loss_ref[...] += 0.5 * jnp.sum(per_row, axis=0, keepdims=True)
        dy = err * (1.0 / d)
        g = dy * w_ref[...]
        dx_ref[...] = rstd * (g - xh * jnp.mean(g * xh, axis=-1, keepdims=True))
        dw_ref[...] += jnp.sum(dy * xh, axis=0, keepdims=True)

    row = pl.BlockSpec((tr, d), lambda i: (i, 0))
    vec = pl.BlockSpec((1, d), lambda i: (0, 0))
    one = pl.BlockSpec((1, 1), lambda i: (0, 0))
    return pl.pallas_call(
        body, name=name, grid=(seq // tr,), in_specs=[row, vec, row], out_specs=[one, row, vec],
        out_shape=[jax.ShapeDtypeStruct((1, 1), F32), jax.ShapeDtypeStruct((seq, d), F32),
                   jax.ShapeDtypeStruct((1, d), F32)],
        compiler_params=pltpu.CompilerParams(dimension_semantics=("arbitrary",), vmem_limit_bytes=VMEM_LIMIT),
    )(x, w, target)


SLAB_TILE_ROWS = 1024


def _slab_tile(rows, cap=SLAB_TILE_ROWS):
    step = 16 if rows % 16 == 0 else 8
    return max(t for t in range(step, min(rows, cap) + 1, step) if rows % t == 0)


def _adamw(name, w, g, m, v):
    rows, cols = w.shape
    tr = _slab_tile(rows, SLAB_TILE_ROWS // 2) if rows % 8 == 0 else rows

    def body(w_ref, g_ref, m_ref, v_ref, d_ref, nm_ref, nv_ref):
        gv = g_ref[...]
        nm = ADAM_B1 * m_ref[...] + (1.0 - ADAM_B1) * gv
        nv = ADAM_B2 * v_ref[...] + (1.0 - ADAM_B2) * (gv * gv)
        m_hat = nm / (1.0 - ADAM_B1 ** ADAM_STEP)
        v_hat = nv / (1.0 - ADAM_B2 ** ADAM_STEP)
        d_ref[...] = -ADAM_LR * (m_hat / (jnp.sqrt(v_hat) + ADAM_EPS) + ADAM_WD * w_ref[...])
        nm_ref[...] = nm
        nv_ref[...] = nv

    spec = pl.BlockSpec((tr, cols), lambda i: (i, 0))
    sds = jax.ShapeDtypeStruct(w.shape, F32)
    return pl.pallas_call(
        body, name=name, grid=(rows // tr,), in_specs=[spec] * 4, out_specs=[spec] * 3, out_shape=[sds] * 3,
        compiler_params=pltpu.CompilerParams(dimension_semantics=("parallel",), vmem_limit_bytes=VMEM_LIMIT),
    )(w, g, m, v)


def _place_rows(name, slab, tail, first_row):
    nsec, rows, _ = tail.shape
    tr = math.gcd(rows, first_row)
    tr = _slab_tile(tr, SLAB_TILE_ROWS // 2)
    first_blk = first_row // tr

    def body(t_ref, s_ref, o_ref):
        o_ref[...] = t_ref[...]

    return pl.pallas_call(
        body, name=name, grid=(nsec, rows // tr),
        in_specs=[pl.BlockSpec((1, tr, LANES), lambda s, i: (s, i, 0)), pl.BlockSpec(memory_space=pl.ANY)],
        out_specs=pl.BlockSpec((1, tr, LANES), lambda s, i: (s, first_blk + i, 0)),
        out_shape=jax.ShapeDtypeStruct(slab.shape, slab.dtype), input_output_aliases={1: 0},
        compiler_params=pltpu.CompilerParams(dimension_semantics=("parallel", "parallel"),
                                             vmem_limit_bytes=VMEM_LIMIT),
    )(tail, slab)


WIRE_DTYPE = jnp.bfloat16


def _add_halves(name, g, t1, c):
    nsec, rows, _ = g.shape
    rh = rows // 2
    tr = _slab_tile(rh)
    nb = rh // tr

    def body(c_ref, g_ref, t_ref, o_ref):
        o_ref[...] = (g_ref[...] + t_ref[...]).astype(o_ref.dtype)

    gs = pltpu.PrefetchScalarGridSpec(
        num_scalar_prefetch=1, grid=(nsec, nb),
        in_specs=[pl.BlockSpec((1, tr, LANES), lambda s, i, c_ref: (s, c_ref[0] * nb + i, 0)),
                  pl.BlockSpec((1, tr, LANES), lambda s, i, c_ref: (s, i, 0))],
        out_specs=pl.BlockSpec((1, tr, LANES), lambda s, i, c_ref: (s, i, 0)))
    return pl.pallas_call(
        body, name=name, grid_spec=gs, out_shape=jax.ShapeDtypeStruct((nsec, rh, LANES), WIRE_DTYPE),
        compiler_params=pltpu.CompilerParams(dimension_semantics=("parallel", "parallel"),
                                             vmem_limit_bytes=VMEM_LIMIT),
    )(c, g, t1)


ANY = pl.BlockSpec(memory_space=pl.ANY)


def _place():
    return lax.axis_index("x"), lax.axis_index("y"), lax.axis_index("c")


def _all_gather_shards(name, slab):
    rows = slab.shape[0]
    rh = rows // 2
    rq = rh // 2

    def body(x_ref, out_ref, send_sems, recv_sems):
        x, y, c = _place()
        me, sibling = (x, y, c), (x, y, 1 - c)
        xn, yn, dg = (1 - x, y), (x, 1 - y), (1 - x, 1 - y)

        def piece(chip, core, q):
            return out_ref.at[2 * chip[0] + chip[1], pl.ds(core * rh + q * rq, rq), :]

        def copy(k, chip, core, q, to, src=None):
            return pltpu.make_async_remote_copy(
                src_ref=piece(chip, core, q) if src is None else src, dst_ref=piece(chip, core, q),
                send_sem=send_sems.at[k], recv_sem=recv_sems.at[k], device_id=to, device_id_type=MESH)

        own = [x_ref.at[pl.ds(c * rh + q * rq, rq), :] for q in range(2)]
        sends = [copy(0, (x, y), c, 0, (*xn, c), src=own[0]), copy(1, (x, y), c, 1, (*xn, c), src=own[1]),
                 copy(2, (x, y), c, 0, (*yn, c), src=own[0]), copy(3, (x, y), c, 1, (*yn, c), src=own[1])]
        for cp in sends:
            cp.start()
        landed = [(0, xn, 0), (3, yn, 1), (1, xn, 1), (2, yn, 0), (4, dg, 0), (5, dg, 1)]
        onward = {0: (4, (*yn, c)), 3: (5, (*xn, c))}
        for i, (k, chip, q) in enumerate(landed):
            copy(k, chip, c, q, me).wait_recv()
            if k in onward:
                fk, to = onward[k]
                sends.append(copy(fk, chip, c, q, to))
                sends[-1].start()
            sends.append(copy(6 + i, chip, c, q, sibling))
            sends[-1].start()
        for i, (k, chip, q) in enumerate(landed):
            copy(6 + i, chip, 1 - c, q, me).wait_recv()
        for cp in sends:
            cp.wait_send()

    got = pl.pallas_call(
        body, name=name, in_specs=[ANY], out_specs=ANY,
        out_shape=jax.ShapeDtypeStruct((4, rows, LANES), slab.dtype),
        scratch_shapes=[pltpu.SemaphoreType.DMA((12,)), pltpu.SemaphoreType.DMA((12,))],
    )(slab)
    return lax.dynamic_update_slice(got, slab[None], (2 * lax.axis_index("x") + lax.axis_index("y"), 0, 0))


def _swap_halves(name, g):
    nsec, rows, _ = g.shape
    rh = rows // 2

    def body(g_ref, t_ref, send_sem, recv_sem):
        x, y, c = _place()
        cp = pltpu.make_async_remote_copy(
            src_ref=g_ref.at[:, pl.ds((1 - c) * rh, rh), :], dst_ref=t_ref, send_sem=send_sem, recv_sem=recv_sem,
            device_id=(x, y, 1 - c), device_id_type=MESH)
        cp.start()
        cp.wait()

    return pl.pallas_call(
        body, name=name, in_specs=[ANY], out_specs=ANY, out_shape=jax.ShapeDtypeStruct((nsec, rh, LANES), F32),
        scratch_shapes=[pltpu.SemaphoreType.DMA, pltpu.SemaphoreType.DMA],
    )(g)


def _exchange_stage1(name, p):
    _, rh, _ = p.shape
    rq = rh // 2

    def body(p_ref, fx_ref, fy_ref, send_sems, recv_sems):
        x, y, c = _place()
        to_x = pltpu.make_async_remote_copy(
            src_ref=p_ref.at[pl.ds(2 * (1 - x), 2), pl.ds(0, rq), :], dst_ref=fx_ref, send_sem=send_sems.at[0],
            recv_sem=recv_sems.at[0], device_id=(1 - x, y, c), device_id_type=MESH)
        to_y = [pltpu.make_async_remote_copy(
            src_ref=p_ref.at[2 * sx + (1 - y), pl.ds(rq, rq), :], dst_ref=fy_ref.at[sx], send_sem=send_sems.at[1 + sx],
            recv_sem=recv_sems.at[1 + sx], device_id=(x, 1 - y, c), device_id_type=MESH) for sx in range(2)]
        for cp in [to_x] + to_y:
            cp.start()
        for cp in [to_x] + to_y:
            cp.wait_recv()
        for cp in [to_x] + to_y:
            cp.wait_send()

    sds = jax.ShapeDtypeStruct((2, rq, LANES), p.dtype)
    return pl.pallas_call(
        body, name=name, in_specs=[ANY], out_specs=[ANY, ANY], out_shape=[sds, sds],
        scratch_shapes=[pltpu.SemaphoreType.DMA((3,)), pltpu.SemaphoreType.DMA((3,))],
    )(p)


def _exchange_add1(name, p, from_x, from_y, place):
    _, rh, _ = p.shape
    rq = rh // 2
    tr = _slab_tile(rq)
    nb = rq // tr

    def body(xy_ref, pa_s, pa_k, pb_s, pb_k, fx_s, fx_k, fy_s, fy_k, sa, ka, sb, kb):
        for mine, theirs, out in ((pa_s, fx_s, sa), (pa_k, fx_k, ka), (pb_s, fy_s, sb), (pb_k, fy_k, kb)):
            out[...] = (mine[0].astype(F32) + theirs[0].astype(F32)).astype(out.dtype)

    blk = lambda fn: pl.BlockSpec((1, tr, LANES), fn)
    gs = pltpu.PrefetchScalarGridSpec(
        num_scalar_prefetch=1, grid=(nb,),
        in_specs=[blk(lambda i, xy: (2 * xy[0] + 1 - xy[1], i, 0)), blk(lambda i, xy: (2 * xy[0] + xy[1], i, 0)),
                  blk(lambda i, xy: (2 * (1 - xy[0]) + xy[1], nb + i, 0)), blk(lambda i, xy: (2 * xy[0] + xy[1], nb + i, 0)),
                  blk(lambda i, xy: (1 - xy[1], i, 0)), blk(lambda i, xy: (xy[1], i, 0)),
                  blk(lambda i, xy: (1 - xy[0], i, 0)), blk(lambda i, xy: (xy[0], i, 0))],
        out_specs=[pl.BlockSpec((tr, LANES), lambda i, xy: (i, 0))] * 4)
    sds = jax.ShapeDtypeStruct((rq, LANES), p.dtype)
    return pl.pallas_call(
        body, name=name, grid_spec=gs, out_shape=[sds] * 4,
        compiler_params=pltpu.CompilerParams(dimension_semantics=("parallel",), vmem_limit_bytes=VMEM_LIMIT),
    )(place, p, p, p, p, from_x, from_x, from_y, from_y)


def _exchange_stage2(name, send_a, send_b):
    def body(a_ref, b_ref, fa_ref, fb_ref, send_sems, recv_sems):
        x, y, c = _place()
        cps = [pltpu.make_async_remote_copy(src_ref=a_ref, dst_ref=fa_ref, send_sem=send_sems.at[0],
                                            recv_sem=recv_sems.at[0], device_id=(x, 1 - y, c), device_id_type=MESH),
               pltpu.make_async_remote_copy(src_ref=b_ref, dst_ref=fb_ref, send_sem=send_sems.at[1],
                                            recv_sem=recv_sems.at[1], device_id=(1 - x, y, c), device_id_type=MESH)]
        for cp in cps:
            cp.start()
        for cp in cps:
            cp.wait_recv()
        for cp in cps:
            cp.wait_send()

    sds = jax.ShapeDtypeStruct(send_a.shape, send_a.dtype)
    return pl.pallas_call(
        body, name=name, in_specs=[ANY, ANY], out_specs=[ANY, ANY], out_shape=[sds, sds],
        scratch_shapes=[pltpu.SemaphoreType.DMA((2,)), pltpu.SemaphoreType.DMA((2,))],
    )(send_a, send_b)


def _exchange_add2(name, keep_a, got_a, keep_b, got_b, c):
    rq = keep_a.shape[0]
    tr = _slab_tile(rq)

    def body(c_ref, ka, ga, kb, gb, o_ref):
        o_ref[0] = ka[...].astype(F32) + ga[...].astype(F32)
        o_ref[1] = kb[...].astype(F32) + gb[...].astype(F32)

    spec = pl.BlockSpec((tr, LANES), lambda i, c_ref: (i, 0))
    gs = pltpu.PrefetchScalarGridSpec(
        num_scalar_prefetch=1, grid=(rq // tr,), in_specs=[spec] * 4,
        out_specs=pl.BlockSpec((2, tr, LANES), lambda i, c_ref: (c_ref[0], i, 0)))
    out = pl.pallas_call(
        body, name=name, grid_spec=gs, out_shape=jax.ShapeDtypeStruct((4, rq, LANES), F32),
        compiler_params=pltpu.CompilerParams(dimension_semantics=("parallel",), vmem_limit_bytes=VMEM_LIMIT),
    )(c, keep_a, got_a, keep_b, got_b)
    return out.reshape(4 * rq, LANES)


def _join_halves(name, full):
    rh = full.shape[0] // 2

    def body(in_ref, o_ref, send_sem, recv_sem):
        x, y, c = _place()
        cp = pltpu.make_async_remote_copy(
            src_ref=in_ref.at[pl.ds(c * rh, rh), :], dst_ref=o_ref.at[pl.ds(c * rh, rh), :], send_sem=send_sem,
            recv_sem=recv_sem, device_id=(x, y, 1 - c), device_id_type=MESH)
        cp.start()
        pltpu.make_async_remote_copy(
            src_ref=in_ref.at[pl.ds(c * rh, rh), :], dst_ref=o_ref.at[pl.ds((1 - c) * rh, rh), :], send_sem=send_sem,
            recv_sem=recv_sem, device_id=(x, y, 1 - c), device_id_type=MESH).wait_recv()
        cp.wait_send()

    return pl.pallas_call(
        body, name=name, in_specs=[ANY], out_specs=ANY, out_shape=jax.ShapeDtypeStruct(full.shape, full.dtype),
        input_output_aliases={0: 0}, scratch_shapes=[pltpu.SemaphoreType.DMA, pltpu.SemaphoreType.DMA],
    )(full)


def _rows_of(n):
    return -(-n // LANES)


SLAB_ROW_ALIGN = 512


def _flat_rows(arrays, dtype):
    parts = []
    for a in arrays:
        flat = a.reshape(-1).astype(dtype)
        parts.append(jnp.pad(flat, (0, _rows_of(flat.size) * LANES - flat.size)))
    return jnp.concatenate(parts).reshape(-1, LANES)


def _align_rows(slab):
    rows = slab.shape[0]
    return jnp.pad(slab, ((0, -(-rows // SLAB_ROW_ALIGN) * SLAB_ROW_ALIGN - rows), (0, 0)))


def _pack(arrays, dtype):
    return _align_rows(_flat_rows(arrays, dtype))


def _unpack(slab, shapes):
    out, r = [], 0
    for shp in shapes:
        n = math.prod(shp)
        out.append(slab[r:r + _rows_of(n)].reshape(-1)[:n].reshape(shp))
        r += _rows_of(n)
    return out


def _unpack_gathered(g, shapes, kinds):
    out, r = [], 0
    for shp, kind in zip(shapes, kinds):
        n = math.prod(shp)
        blk = g[:, r:r + _rows_of(n)].reshape(4, -1)[:, :n].reshape((4,) + tuple(shp))
        r += _rows_of(n)
        if kind == "col":
            out.append(jnp.moveaxis(blk, 0, 1).reshape(shp[0], 4 * shp[1]))
        else:
            out.append(blk.reshape(4 * shp[0], shp[1]))
    return out


def _sections(g, kind, local_shape):
    if kind == "col":
        blocks = jnp.moveaxis(g.reshape(local_shape[0], 4, local_shape[1]), 1, 0)
    elif kind == "row":
        blocks = g.reshape((4,) + tuple(local_shape))
    else:
        blocks = jnp.broadcast_to(g, (4,) + tuple(g.shape))
    flat = blocks.reshape(4, -1)
    rows = _rows_of(flat.shape[1])
    return jnp.pad(flat, ((0, 0), (0, rows * LANES - flat.shape[1]))).reshape(4, rows, LANES)


def _rotary_tables(seq):
    half = RET_DK // 2
    pos = jnp.arange(seq, dtype=F32)
    inv = ROPE_THETA ** (-jnp.arange(half, dtype=F32) / half)
    ang = pos[:, None] * inv[None, :]
    cos, sin = jnp.cos(ang), jnp.sin(ang)
    return jnp.concatenate([cos, cos], axis=1), jnp.concatenate([-sin, sin], axis=1)


def _retention_tables():
    log_gamma = jnp.log(1.0 - 2.0 ** (-5.0 - jnp.arange(RET_HEADS, dtype=F32)))
    idx = jnp.arange(CHUNK, dtype=F32)
    diff = idx[:, None] - idx[None, :]
    dmask = jnp.exp(jnp.where((diff >= 0)[None], log_gamma[:, None, None] * diff[None], -jnp.inf))
    kdec = jnp.exp(log_gamma[None, :] * (CHUNK - 1.0 - idx)[:, None])
    qdec = jnp.exp(log_gamma[None, :] * (idx + 1.0)[:, None])
    cdec = jnp.exp(log_gamma * CHUNK)[None, :]
    lanes = lambda t: jnp.repeat(t, RET_DK, axis=1)
    return dmask.reshape(RET_HEADS * CHUNK, CHUNK), lanes(kdec), lanes(qdec), lanes(cdec)


def _s5_prep(a_re, a_im, log_step, b_re, b_im, c_re, c_im):
    g, n, c = S5_GROUPS, S5_STATE, S5_GROUP
    lam = lax.complex(a_re, a_im)
    step = jnp.exp(log_step)[:, None]
    lam_bar = jnp.exp(lam * step)
    b_bar = ((lam_bar - 1.0) / lam)[..., None] * lax.complex(b_re, b_im)
    eye = jnp.eye(g, dtype=F32)
    bb_re = (jnp.real(b_bar).transpose(0, 2, 1)[:, :, None, :] * eye[:, None, :, None]).reshape(g * c, g * n)
    bb_im = (jnp.imag(b_bar).transpose(0, 2, 1)[:, :, None, :] * eye[:, None, :, None]).reshape(g * c, g * n)
    cc_re = (c_re.transpose(0, 2, 1)[:, :, None, :] * eye[:, None, :, None]).reshape(g * n, g * c)
    cc_im = (c_im.transpose(0, 2, 1)[:, :, None, :] * eye[:, None, :, None]).reshape(g * n, g * c)
    return (jnp.real(lam_bar).reshape(1, g * n), jnp.imag(lam_bar).reshape(1, g * n),
            jnp.concatenate([bb_re, bb_im], axis=1), cc_re, cc_im)


def kernel(x, l0_norm_mix, l0_w_in, ssd_conv_w, ssd_conv_b, ssd_dt_bias, ssd_A_log, ssd_D, ssd_norm_w, l0_w_out, l0_norm_mlp, l0_w_up, l0_w_down, l1_norm_mix, l1_w_in, gdn_conv_w, gdn_A_log, gdn_dt_bias, gdn_norm_w, s5_A_re, s5_A_im, s5_log_step, s5_B_re, s5_B_im, s5_C_re, s5_C_im, s5_D, s5_w_glu, s5_b_glu, l1_w_out, l1_norm_mlp, l1_w_up, l1_w_down, final_norm, loss_target, m_l0_norm_mix, m_l0_w_in, m_ssd_conv_w, m_ssd_conv_b, m_ssd_dt_bias, m_ssd_A_log, m_ssd_D, m_ssd_norm_w, m_l0_w_out, m_l0_norm_mlp, m_l0_w_up, m_l0_w_down, m_l1_norm_mix, m_l1_w_in, m_gdn_conv_w, m_gdn_A_log, m_gdn_dt_bias, m_gdn_norm_w, m_s5_A_re, m_s5_A_im, m_s5_log_step, m_s5_B_re, m_s5_B_im, m_s5_C_re, m_s5_C_im, m_s5_D, m_s5_w_glu, m_s5_b_glu, m_l1_w_out, m_l1_norm_mlp, m_l1_w_up, m_l1_w_down, m_final_norm, v_l0_norm_mix, v_l0_w_in, v_ssd_conv_w, v_ssd_conv_b, v_ssd_dt_bias, v_ssd_A_log, v_ssd_D, v_ssd_norm_w, v_l0_w_out, v_l0_norm_mlp, v_l0_w_up, v_l0_w_down, v_l1_norm_mix, v_l1_w_in, v_gdn_conv_w, v_gdn_A_log, v_gdn_dt_bias, v_gdn_norm_w, v_s5_A_re, v_s5_A_im, v_s5_log_step, v_s5_B_re, v_s5_B_im, v_s5_C_re, v_s5_C_im, v_s5_D, v_s5_w_glu, v_s5_b_glu, v_l1_w_out, v_l1_norm_mlp, v_l1_w_up, v_l1_w_down, v_final_norm):
    given = dict(locals())
    names = [n for n, _ in PARAMS]
    kinds = dict(PARAMS)
    w = {n: given[n] for n in names}
    seq = x.shape[1]
    x0 = x.reshape(seq, D_MODEL)
    target = loss_target.reshape(seq, D_MODEL)

    gb = _all_gather_shards("gather_weights", _pack([w[n] for n in GATHER_BF16], _MXU_DTYPE))
    full = dict(zip(GATHER_BF16, _unpack_gathered(gb, [w[n].shape for n in GATHER_BF16],
                                                  [kinds[n] for n in GATHER_BF16])))
    gf = _all_gather_shards("gather_conv", _pack([w[n] for n in GATHER_F32], F32))
    full.update(zip(GATHER_F32, _unpack_gathered(gf, [w[n].shape for n in GATHER_F32],
                                                 [kinds[n] for n in GATHER_F32])))
    in0 = full["l0_w_in"].shape[1]
    w_in0 = jnp.pad(full["l0_w_in"], ((0, 0), (0, IN0_PAD - in0)))
    wi1 = full["l1_w_in"]
    in1 = wi1.shape[1]
    w_in1 = jnp.concatenate([wi1[:, :3072], wi1[:, 3084:in1], wi1[:, 3072:3084],
                             jnp.zeros((D_MODEL, IN1_PAD - in1), wi1.dtype)], axis=1)

    row = lambda a: a.reshape(1, -1)
    lanes64 = lambda a: jnp.repeat(a, SSD_HEAD_DIM).reshape(1, -1)

    h0 = _rmsnorm_fwd("norm_mix0", x0, row(w["l0_norm_mix"]))
    proj0 = _matmul("in_proj0", h0, w_in0, "nn")
    cos_t, sin_t = _rotary_tables(seq)
    ret_tabs = list(_retention_tables())
    ret_xs = [(proj0, 512, 0), (proj0, 512, 1), (proj0, 512, 2), (proj0, 512, 3)]
    ret_xt = [(cos_t, 128, 0), (sin_t, 128, 0)]
    ret_states = [(512, 128)]
    mixed0, ret_saved = _scan_fwd("ret_fwd", _f_ret, CHUNK, ret_tabs, [], ret_xs, ret_xt, ret_states, D_MODEL, 512, 0)
    expand = jnp.repeat(jnp.eye(128, SSD_HEADS, dtype=F32), SSD_HEAD_DIM, axis=1)
    ssd_consts = [full["ssd_conv_w"], row(w["ssd_conv_b"]), lanes64(w["ssd_dt_bias"]), lanes64(w["ssd_A_log"]),
                  lanes64(w["ssd_D"]), row(w["ssd_norm_w"])]
    ssd_xs = [(proj0, 512, 4), (proj0, 512, 5), (proj0, 256, 12), (proj0, 256, 13), (proj0, 128, 28)]
    ssd_states = [(8, 512), (8, 256), (8, 256), (512, 128)]
    mixed0, ssd_saved = _scan_fwd("ssd_fwd", _f_ssd, CHUNK, [expand], ssd_consts, ssd_xs, [], ssd_states,
                                  D_MODEL, 512, 1, y_alias=mixed0)
    x1, h1 = _matmul("out_proj0", mixed0, full["l0_w_out"], "nn", epi="add", epi_arr=x0, norm_w=row(w["l0_norm_mlp"]))
    u0 = _matmul("up0", h1, full["l0_w_up"], "nn", out_dtype=_MXU_DTYPE)
    x2, h2 = _matmul("down0", u0, full["l0_w_down"], "nn", a_pro="relu2", epi="add", epi_arr=x1,
                     norm_w=row(w["l1_norm_mix"]))

    proj1 = _matmul("in_proj1", h2, w_in1, "nn")
    p_alog = jnp.zeros((1, 128), F32).at[0, 6:12].set(w["gdn_A_log"])
    p_dtb = jnp.zeros((1, 128), F32).at[0, 6:12].set(w["gdn_dt_bias"])
    gdn_consts = [full["gdn_conv_w"], p_alog, p_dtb, row(w["gdn_norm_w"])]
    gdn_xs = [(proj1, 768, 0), (proj1, 768, 1), (proj1, 768, 2), (proj1, 768, 3), (proj1, 128, 26)]
    gdn_states = [(8, 768), (8, 768), (8, 768), (768, 256)]
    mixed1, gdn_saved = _scan_fwd("gdn_fwd", _f_gdn, CHUNK, [], gdn_consts, gdn_xs, [], gdn_states, D_MODEL, 768, 0)
    s5_args = (w["s5_A_re"], w["s5_A_im"], w["s5_log_step"], w["s5_B_re"], w["s5_B_im"], w["s5_C_re"], w["s5_C_im"])
    (lam_re, lam_im, bblk, cc_re, cc_im), s5_prep_vjp = jax.vjp(_s5_prep, *s5_args)
    s5_consts = [lam_re, lam_im, bblk, cc_re, cc_im, row(w["s5_D"]), full["s5_w_glu"].astype(F32), row(w["s5_b_glu"])]
    s5_xs = [(proj1, 256, 12)]
    s5_states = [(8, 1024), (8, 1024)]
    mixed1, s5_saved = _scan_fwd("s5_fwd", _f_s5, CHUNK, [], s5_consts, s5_xs, [], s5_states, D_MODEL, 256, 3,
                                 y_alias=mixed1)
    x3, h3 = _matmul("out_proj1", mixed1, full["l1_w_out"], "nn", epi="add", epi_arr=x2, norm_w=row(w["l1_norm_mlp"]))
    u1 = _matmul("up1", h3, full["l1_w_up"], "nn", out_dtype=_MXU_DTYPE)
    x4 = _matmul("down1", u1, full["l1_w_down"], "nn", a_pro="relu2", epi="add", epi_arr=x3)

    loss_part, dx4, d_final = _loss_head("loss_head", x4, row(w["final_norm"]), target)
    loss = lax.psum(loss_part[0, 0], ("x", "y", "c"))
    grads = {"final_norm": d_final.reshape(-1)}
    small = SMALL_SHARDED + tuple(n for n in names if kinds[n] == "rep")
    order = LARGE + small
    first_row, slab_rows = {}, 0
    for n in order:
        first_row[n] = slab_rows
        slab_rows += _rows_of(math.prod(w[n].shape))
    slab_rows = -(-slab_rows // SLAB_ROW_ALIGN) * SLAB_ROW_ALIGN

    du1 = _matmul("down1_dx", dx4, full["l1_w_down"], "nt", out_dtype=_MXU_DTYPE, epi="drelu2", epi_arr=u1)
    gslab = _matmul("down1_dw", u1, dx4, "tn", a_pro="relu2", slab=(None, slab_rows, first_row["l1_w_down"], 1024))
    gslab = _matmul("up1_dw", h3, du1, "tn", slab=(gslab, slab_rows, first_row["l1_w_up"], None))
    dx3, dwn = _matmul("up1_dx", du1, full["l1_w_up"], "nt", epi="norm_bwd", epi_arr=dx4, norm_x=x3,
                       norm_w=row(w["l1_norm_mlp"]))
    grads["l1_norm_mlp"] = dwn.reshape(-1)
    gslab = _matmul("out_proj1_dw", mixed1, dx3, "tn", slab=(gslab, slab_rows, first_row["l1_w_out"], 256))
    dmixed1 = _matmul("out_proj1_dx", dx3, full["l1_w_out"], "nt")

    def gdn_assemble(dx):
        dq, dk, dv, dz, dba = dx
        zeros = lambda n: jnp.zeros((dq.shape[0], n), F32)
        return jnp.concatenate([dq, dk, dv, dz, zeros(256), dba, zeros(IN1_PAD - 3456)], axis=1)

    dproj1, gdn_dc = _scan_bwd("gdn_bwd", _f_gdn, CHUNK, [], gdn_consts, gdn_xs, [], gdn_saved, gdn_states,
                               (dmixed1, 768, 0), IN1_PAD, IN1_PAD, 0, gdn_assemble)
    dproj1, s5_dc = _scan_bwd("s5_bwd", _f_s5, CHUNK, [], s5_consts, s5_xs, [], s5_saved, s5_states,
                              (dmixed1, 256, 3), IN1_PAD, 256, 12, lambda dx: dx[0], dx_alias=dproj1)
    grads["gdn_conv_w"] = gdn_dc[0]
    grads["gdn_A_log"] = gdn_dc[1][0, 6:12]
    grads["gdn_dt_bias"] = gdn_dc[2][0, 6:12]
    grads["gdn_norm_w"] = gdn_dc[3].reshape(-1)
    s5_pg = s5_prep_vjp(tuple(s5_dc[:5]))
    for n, gval in zip(("s5_A_re", "s5_A_im", "s5_log_step", "s5_B_re", "s5_B_im", "s5_C_re", "s5_C_im"), s5_pg):
        grads[n] = gval
    grads["s5_D"] = s5_dc[5].reshape(-1)
    grads["s5_w_glu"] = s5_dc[6]
    grads["s5_b_glu"] = s5_dc[7].reshape(-1)
    dwi1 = _matmul("in_proj1_dw", h2, dproj1, "tn")
    grads["l1_w_in"] = jnp.concatenate([dwi1[:, :3072], dwi1[:, 3328:3340], dwi1[:, 3072:3328]], axis=1)
    dx2, dwn = _matmul("in_proj1_dx", dproj1, w_in1, "nt", epi="norm_bwd", epi_arr=dx3, norm_x=x2,
                       norm_w=row(w["l1_norm_mix"]))
    grads["l1_norm_mix"] = dwn.reshape(-1)

    du0 = _matmul("down0_dx", dx2, full["l0_w_down"], "nt", out_dtype=_MXU_DTYPE, epi="drelu2", epi_arr=u0)
    gslab = _matmul("down0_dw", u0, dx2, "tn", a_pro="relu2", slab=(gslab, slab_rows, first_row["l0_w_down"], 1024))
    gslab = _matmul("up0_dw", h1, du0, "tn", slab=(gslab, slab_rows, first_row["l0_w_up"], None))
    dx1, dwn = _matmul("up0_dx", du0, full["l0_w_up"], "nt", epi="norm_bwd", epi_arr=dx2, norm_x=x1,
                       norm_w=row(w["l0_norm_mlp"]))
    grads["l0_norm_mlp"] = dwn.reshape(-1)
    gslab = _matmul("out_proj0_dw", mixed0, dx1, "tn", slab=(gslab, slab_rows, first_row["l0_w_out"], 256))
    dmixed0 = _matmul("out_proj0_dx", dx1, full["l0_w_out"], "nt")
    dproj0, _ = _scan_bwd("ret_bwd", _f_ret, CHUNK, ret_tabs, [], ret_xs, ret_xt, ret_saved, ret_states,
                          (dmixed0, 512, 0), IN0_PAD, 2048, 0, lambda dx: jnp.concatenate(dx, axis=1))

    def ssd_assemble(dx):
        return jnp.concatenate(list(dx) + [jnp.zeros((dx[0].shape[0], 2048 - 1664), F32)], axis=1)

    dproj0, ssd_dc = _scan_bwd("ssd_bwd", _f_ssd, CHUNK, [expand], ssd_consts, ssd_xs, [], ssd_saved, ssd_states,
                               (dmixed0, 512, 1), IN0_PAD, 2048, 1, ssd_assemble, dx_alias=dproj0)
    heads = lambda a: a.reshape(SSD_HEADS, SSD_HEAD_DIM).sum(axis=1)
    grads["ssd_conv_w"] = ssd_dc[0]
    grads["ssd_conv_b"] = ssd_dc[1].reshape(-1)
    grads["ssd_dt_bias"] = heads(ssd_dc[2])
    grads["ssd_A_log"] = heads(ssd_dc[3])
    grads["ssd_D"] = heads(ssd_dc[4])
    grads["ssd_norm_w"] = ssd_dc[5].reshape(-1)
    grads["l0_w_in"] = _matmul("in_proj0_dw", h0, dproj0, "tn")[:, :in0]
    dx0, dwn = _matmul("in_proj0_dx", dproj0, w_in0, "nt", epi="norm_bwd", epi_arr=dx1, norm_x=x0,
                       norm_w=row(w["l0_norm_mix"]))
    grads["l0_norm_mix"] = dwn.reshape(-1)
    grad_x = dx0.reshape(x.shape)

    c_idx = lax.axis_index("c").astype(jnp.int32).reshape(1)
    tail = order[SLAB_DIRECT:]
    parts = [_sections(grads[n].reshape(_full_shape(n, w, kinds)), kinds[n], w[n].shape) for n in tail]
    parts.append(jnp.zeros((4, slab_rows - first_row[tail[0]] - sum(p.shape[1] for p in parts), LANES), F32))
    gslab = _place_rows("grads_place_tail", gslab, jnp.concatenate(parts, axis=1), first_row[tail[0]])
    from_sibling = _swap_halves("grads_swap_halves", gslab)
    chip_sum = _add_halves("grads_add_sibling", gslab, from_sibling, c_idx)
    place = jnp.stack([lax.axis_index("x"), lax.axis_index("y")]).astype(jnp.int32)
    from_x, from_y = _exchange_stage1("grads_stage1", chip_sum)
    send_a, keep_a, send_b, keep_b = _exchange_add1("grads_add1", chip_sum, from_x, from_y, place)
    got_a, got_b = _exchange_stage2("grads_stage2", send_a, send_b)
    my_half = _exchange_add2("grads_add2", keep_a, got_a, keep_b, got_b, c_idx)
    gsum = _join_halves("grads_join_halves", my_half)
    grad = dict(zip(order, _unpack(gsum, [w[n].shape for n in order])))

    delta, new_m, new_v = {}, {}, {}
    for n in LARGE:
        delta[n], new_m[n], new_v[n] = _adamw("adamw_" + n, w[n], grad[n], given["m_" + n], given["v_" + n])
    first_small = sum(_rows_of(math.prod(w[n].shape)) for n in LARGE)
    small_shapes = [w[n].shape for n in small]

    def small_slab(arrays):
        rows = _flat_rows(arrays, F32)
        return jnp.pad(rows, ((0, gsum.shape[0] - first_small - rows.shape[0]), (0, 0)))

    res = _adamw("adamw_small", small_slab([w[n] for n in small]), gsum[first_small:],
                 small_slab([given["m_" + n] for n in small]), small_slab([given["v_" + n] for n in small]))
    for out, slab in zip((delta, new_m, new_v), res):
        out.update(zip(small, _unpack(slab, small_shapes)))
    return (loss, grad_x, *[grad[n] for n in names], *[delta[n] for n in names], *[new_m[n] for n in names],
            *[new_v[n] for n in names])


def _full_shape(name, w, kinds):
    shp = w[name].shape
    if kinds[name] == "col":
        return (shp[0], 4 * shp[1])
    if kinds[name] == "row":
        return (4 * shp[0],) + tuple(shp[1:])
    return shp
```

```python
import functools
import math

import jax
import jax.numpy as jnp
from jax import lax
from jax.experimental import pallas as pl
from jax.experimental.pallas import tpu as pltpu

F32 = jnp.float32
_MXU_DTYPE = jnp.bfloat16

D_MODEL = 1024
CHUNK = 64
EPS = 1e-6
RET_HEADS, RET_DK = 4, 128
ROPE_THETA = 10000.0
SSD_HEADS, SSD_HEAD_DIM = 8, 64
GDN_HEADS, GDN_DK = 6, 128
S5_GROUPS, S5_GROUP, S5_STATE = 16, 16, 64
ADAM_LR, ADAM_B1, ADAM_B2, ADAM_EPS, ADAM_WD, ADAM_STEP = 0.001, 0.9, 0.999, 1e-08, 0.01, 10

IN0_PAD = 4096
IN1_PAD = 3584
LANES = 1024
VMEM_LIMIT = 56 * 1024 * 1024
MESH = pl.DeviceIdType.MESH

PARAMS = (
    ("l0_norm_mix", "rep"), ("l0_w_in", "col"), ("ssd_conv_w", "col"), ("ssd_conv_b", "rep"),
    ("ssd_dt_bias", "rep"), ("ssd_A_log", "rep"), ("ssd_D", "rep"), ("ssd_norm_w", "rep"),
    ("l0_w_out", "row"), ("l0_norm_mlp", "rep"), ("l0_w_up", "col"), ("l0_w_down", "row"),
    ("l1_norm_mix", "rep"), ("l1_w_in", "col"), ("gdn_conv_w", "col"), ("gdn_A_log", "rep"),
    ("gdn_dt_bias", "rep"), ("gdn_norm_w", "rep"), ("s5_A_re", "rep"), ("s5_A_im", "rep"),
    ("s5_log_step", "rep"), ("s5_B_re", "rep"), ("s5_B_im", "rep"), ("s5_C_re", "rep"), ("s5_C_im", "rep"),
    ("s5_D", "rep"), ("s5_w_glu", "row"), ("s5_b_glu", "rep"), ("l1_w_out", "row"), ("l1_norm_mlp", "rep"),
    ("l1_w_up", "col"), ("l1_w_down", "row"), ("final_norm", "rep"),
)
GATHER_BF16 = ("l0_w_in", "l0_w_out", "l0_w_up", "l0_w_down", "l1_w_in", "l1_w_out", "l1_w_up", "l1_w_down", "s5_w_glu")
GATHER_F32 = ("ssd_conv_w", "gdn_conv_w")
LARGE = ("l0_w_up", "l0_w_down", "l1_w_up", "l1_w_down", "l0_w_out", "l1_w_out", "l0_w_in", "l1_w_in")
SLAB_DIRECT = 6
SMALL_SHARDED = ("s5_w_glu", "ssd_conv_w", "gdn_conv_w")


def _dg(a, b, ca, cb, prec=None):
    return lax.dot_general(a, b, (((ca,), (cb,)), ((), ())), preferred_element_type=F32, precision=prec)


def _lo(a):
    return a.astype(_MXU_DTYPE)


@jax.custom_vjp
def _mm(a, b):
    return _dg(_lo(a), _lo(b), 1, 0)


def _mm_fwd(a, b):
    return _mm(a, b), (a, b)


def _mm_bwd(res, g):
    a, b = res
    return _dg(_lo(g), _lo(b), 1, 1), _dg(_lo(a), _lo(g), 0, 0)


_mm.defvjp(_mm_fwd, _mm_bwd)


@jax.custom_vjp
def _mm_nt(a, b):
    return _dg(_lo(a), _lo(b), 1, 1)


def _mm_nt_fwd(a, b):
    return _mm_nt(a, b), (a, b)


def _mm_nt_bwd(res, g):
    a, b = res
    return _dg(_lo(g), _lo(b), 1, 0), _dg(_lo(g), _lo(a), 0, 0)


_mm_nt.defvjp(_mm_nt_fwd, _mm_nt_bwd)


@jax.custom_vjp
def _mm_tn(a, b):
    return _dg(_lo(a), _lo(b), 0, 0)


def _mm_tn_fwd(a, b):
    return _mm_tn(a, b), (a, b)


def _mm_tn_bwd(res, g):
    a, b = res
    return _dg(_lo(b), _lo(g), 1, 1), _dg(_lo(a), _lo(g), 1, 0)


_mm_tn.defvjp(_mm_tn_fwd, _mm_tn_bwd)


def _split2(x):
    hi = _lo(x)
    return hi, _lo(x - hi.astype(F32))


def _split3(x):
    h1 = _lo(x)
    r1 = x - h1.astype(F32)
    h2 = _lo(r1)
    return h1, h2, _lo(r1 - h2.astype(F32))


def _tri_cum_dir(m, ca):
    n, w = m.shape
    causal, _ = _tri_masks(n)
    out = _dg(causal.astype(_MXU_DTYPE), jnp.concatenate(_split3(m), axis=1), ca, 0)
    return out[:, :w] + out[:, w:2 * w] + out[:, 2 * w:]


@jax.custom_vjp
def _tri_cum(m):
    return _tri_cum_dir(m, 1)


def _tri_cum_fwd(m):
    return _tri_cum_dir(m, 1), None


def _tri_cum_bwd(_, g):
    return (_tri_cum_dir(g, 0),)


_tri_cum.defvjp(_tri_cum_fwd, _tri_cum_bwd)


@jax.custom_vjp
def _mm_exact_rhs(a, e):
    return _dg(jnp.concatenate(_split3(a), axis=1), jnp.concatenate([_lo(e)] * 3, axis=0), 1, 0)


def _mm_exact_rhs_fwd(a, e):
    return _mm_exact_rhs(a, e), e


def _mm_exact_rhs_bwd(e, g):
    return _dg(jnp.concatenate(_split3(g), axis=1), jnp.concatenate([_lo(e)] * 3, axis=1), 1, 1), jnp.zeros_like(e)


_mm_exact_rhs.defvjp(_mm_exact_rhs_fwd, _mm_exact_rhs_bwd)


def _bd(x):
    left = _iota(x.shape, 1) < (x.shape[1] // 2)
    zero = jnp.zeros_like(x)
    return jnp.concatenate([jnp.where(left, x, zero), jnp.where(left, zero, x)], axis=0)


def _unbd(m):
    half = m.shape[0] // 2
    left = _iota((half, m.shape[1]), 1) < (m.shape[1] // 2)
    return jnp.where(left, m[:half], m[half:])


def _pmm_nn(x, y):
    xh, xl = _split2(x)
    yh, yl = _split2(y)
    return _dg(jnp.concatenate([xh, xl, xh], axis=1), jnp.concatenate([_bd(yh), _bd(yh), _bd(yl)], axis=0), 1, 0)


def _pmm_nt(x, y):
    xh, xl = _split2(x)
    yh, yl = _split2(y)
    return _dg(jnp.concatenate([xh, xl, xh], axis=1), jnp.concatenate([_bd(yh), _bd(yh), _bd(yl)], axis=1), 1, 1)


def _pmm_tn(x, y):
    xh, xl = _split2(x)
    yh, yl = _split2(y)
    return _unbd(_dg(jnp.concatenate([xh, xl, xh], axis=0), jnp.concatenate([yh, yh, yl], axis=0), 0, 0))


@functools.lru_cache(maxsize=None)
def _shift(s, axis):
    @jax.custom_vjp
    def sh(x):
        return pltpu.roll(x, s, axis)

    def fwd(x):
        return sh(x), None

    def bwd(_, g):
        n = g.shape[axis]
        return (pltpu.roll(g, (n - s) % n, axis),)

    sh.defvjp(fwd, bwd)
    return sh


def _iota(shape, axis):
    return lax.broadcasted_iota(jnp.int32, shape, axis)


def _silu(x):
    return x * jax.nn.sigmoid(x)


def _unit_rms(x):
    return x * lax.rsqrt(jnp.mean(x * x, axis=-1, keepdims=True) + EPS)


def _l2norm(x):
    return x * lax.rsqrt(jnp.sum(x * x, axis=-1, keepdims=True) + EPS)


def _tri_masks(n):
    r, c = _iota((n, n), 0), _iota((n, n), 1)
    return r >= c, r > c


def _packed_rc():
    return _iota((CHUNK, 2 * CHUNK), 0), _iota((CHUNK, 2 * CHUNK), 1) & (CHUNK - 1)


def _decay_packed(g_packed):
    r, c = _packed_rc()
    seg = _tri_cum(g_packed * (r > c).astype(F32))
    return jnp.where(r >= c, jnp.exp(jnp.where(r >= c, seg, 0.0)), 0.0)


def _conv(x, tail, w):
    rows, width = x.shape
    row = _iota((rows, width), 0)
    acc = x * w[3:4, :]
    pad = jnp.zeros((rows - 8, width), F32)
    for j in range(3):
        s = 3 - j
        prev = jnp.concatenate([_shift(s, 0)(tail), pad], axis=0)
        acc = acc + w[j:j + 1, :] * jnp.where(row < s, prev, _shift(s, 0)(x))
    return acc


def _tri_inv_impl(mats):
    r, c = _packed_rc()
    eye = (r == c).astype(F32)

    def same_block(b):
        return (r // b) == (c // b)

    a8 = [jnp.where(same_block(8), a, 0.0) for a in mats]
    a2 = [_pmm_nn(t, t) for t in a8]
    a4 = [_pmm_nn(t, t) for t in a2]
    x = [_pmm_nn(eye - p, eye + q) for p, q in zip(a8, a2)]
    x = [_pmm_nn(p, eye + q) for p, q in zip(x, a4)]
    for b in (8, 16, 32):
        off = [jnp.where(same_block(2 * b) & jnp.logical_not(same_block(b)), a, 0.0) for a in mats]
        y = [_pmm_nn(p, q) for p, q in zip(x, off)]
        x = [p - _pmm_nn(q, p) for p, q in zip(x, y)]
    return x


@jax.custom_vjp
def _tri_inv(mats):
    return _tri_inv_impl(mats)


def _tri_inv_fwd(mats):
    t = _tri_inv_impl(mats)
    return t, t


def _tri_inv_bwd(t, g):
    m1 = [_pmm_tn(p, q) for p, q in zip(t, g)]
    return ([-_pmm_nt(p, q) for p, q in zip(m1, t)],)


_tri_inv.defvjp(_tri_inv_fwd, _tri_inv_bwd)


def _f_ret(tabs, consts, xs, xtabs, states):
    dmask, kdec, qdec, cdec = tabs
    q, k, v, gate = xs
    cs, sn = xtabs
    (st,) = states
    swap = _shift(RET_DK // 2, 1)
    heads = range(RET_HEADS)
    sls = [slice(128 * h, 128 * h + 128) for h in heads]
    qh = [(q[:, sl] * cs + swap(q[:, sl]) * sn) * (RET_DK ** -0.5) for sl in sls]
    kh = [k[:, sl] * cs + swap(k[:, sl]) * sn for sl in sls]
    sh = [st[sl, :] for sl in sls]
    scores = [_mm_nt(a, b) * dmask[64 * h:64 * h + 64, :] for h, a, b in zip(heads, qh, kh)]
    y = [_mm(s, v[:, sl]) for s, sl in zip(scores, sls)]
    y = [t + _mm(a * qdec[:, sl], s) for t, a, sl, s in zip(y, qh, sls, sh)]
    new = [s * cdec[:, sl] + _mm_tn(b * kdec[:, sl], v[:, sl]) for s, sl, b in zip(sh, sls, kh)]
    outs = [_silu(gate[:, sl]) * _unit_rms(t) for sl, t in zip(sls, y)]
    return (jnp.concatenate(outs, axis=1),), [jnp.concatenate(new, axis=0)]


def _f_ssd(tabs, consts, xs, xtabs, states):
    (expand,) = tabs
    conv_w, conv_b, dtb, alog, dskip, nw = consts
    z, xr, br, cr, dtr = xs
    tx, tb, tc, st = states
    xc = _silu(_conv(xr, tx, conv_w[:, 0:512]) + conv_b[:, 0:512])
    bc = _silu(_conv(br, tb, conv_w[:, 512:768]) + conv_b[:, 512:768])
    cc = _silu(_conv(cr, tc, conv_w[:, 768:1024]) + conv_b[:, 768:1024])
    dt = jax.nn.softplus(_mm_exact_rhs(dtr, expand) + dtb)
    la = dt * (-jnp.exp(alog))
    lacum = _tri_cum(la)
    total = jnp.sum(la, axis=0, keepdims=True)
    xd = xc * dt
    dte, ecum, cdec = jnp.exp(total - lacum), jnp.exp(lacum), jnp.exp(total)
    pairs = range(SSD_HEADS // 2)
    sls = [slice(128 * p, 128 * p + 128) for p in pairs]
    bg = [bc[:, 128 * g:128 * g + 128] for g in range(2)]
    cg = [cc[:, 128 * g:128 * g + 128] for g in range(2)]
    cb2 = [_mm_nt(c, jnp.concatenate([b, b], axis=0)) for b, c in zip(bg, cg)]
    lm = [_decay_packed(la[:, sl]) for sl in sls]
    sp = [st[sl, :] for sl in sls]
    ys = [_mm(cg[p // 2], sp[p]) * ecum[:, sls[p]] for p in pairs]
    ys = [ys[p] + _mm(cb2[p // 2] * lm[p], _bd(xd[:, sls[p]])) for p in pairs]
    new = [sp[p] * cdec[:, sls[p]] + _mm_tn(bg[p // 2], xd[:, sls[p]] * dte[:, sls[p]]) for p in pairs]
    y = jnp.concatenate(ys, axis=1) + dskip * xc
    yg = y * _silu(z)
    out = jnp.concatenate([_unit_rms(yg[:, 0:256]), _unit_rms(yg[:, 256:512])], axis=1) * nw
    return (out,), [xr[CHUNK - 8:, :], br[CHUNK - 8:, :], cr[CHUNK - 8:, :], jnp.concatenate(new, axis=0)]


def _f_gdn(tabs, consts, xs, xtabs, states):
    conv_w, p_alog, p_dtb, nw = consts
    qr, kr, vr, z, ba = xs
    tq, tk, tv, st = states
    qc = _silu(_conv(qr, tq, conv_w[:, 0:768]))
    kc = _silu(_conv(kr, tk, conv_w[:, 768:1536]))
    vc = _silu(_conv(vr, tv, conv_w[:, 1536:2304]))
    gl = -jnp.exp(p_alog) * jax.nn.softplus(ba + p_dtb)
    bl = jax.nn.sigmoid(ba)
    gcum = _tri_cum(gl)
    left128 = _iota((CHUNK, 128), 1) < 64
    left256 = _iota((CHUNK, 256), 1) < 128
    r, c = _packed_rc()
    diag_blocks = (_iota((256, 256), 0) < 128) == (_iota((256, 256), 1) < 128)

    def norm2(t):
        return jnp.concatenate([_l2norm(t[:, 0:128]), _l2norm(t[:, 128:256])], axis=1)

    def pick(arr, off, left, p):
        return jnp.where(left, arr[:, off + 2 * p:off + 2 * p + 1], arr[:, off + 2 * p + 1:off + 2 * p + 2])

    pairs = range(GDN_HEADS // 2)
    sls = [slice(256 * p, 256 * p + 256) for p in pairs]
    qn = [norm2(qc[:, sl]) * (GDN_DK ** -0.5) for sl in sls]
    kn = [norm2(kc[:, sl]) for sl in sls]
    dec = [_decay_packed(pick(gl, 6, left128, p)) for p in pairs]
    g2 = [pick(gl, 6, left256, p) for p in pairs]
    gc2 = [pick(gcum, 6, left256, p) for p in pairs]
    b2 = [pick(bl, 0, left256, p) for p in pairs]
    tot = [jnp.sum(t, axis=0, keepdims=True) for t in g2]
    eg = [jnp.exp(t) for t in gc2]
    et = [jnp.exp(t - s) for t, s in zip(tot, gc2)]
    cd = [jnp.exp(t) for t in tot]
    kb = [k * b for k, b in zip(kn, b2)]
    vb = [vc[:, sl] * b for sl, b in zip(sls, b2)]
    kbd = [_bd(k) for k in kn]
    tm = _tri_inv([jnp.where(r > c, _mm_nt(a, b) * d, 0.0) for a, b, d in zip(kb, kbd, dec)])
    u = [_mm(t, _bd(v)) for t, v in zip(tm, vb)]
    w = [_mm(t, _bd(k * e)) for t, k, e in zip(tm, kb, eg)]
    attn = [_mm_nt(q, k) * d for q, k, d in zip(qn, kbd, dec)]
    sp = [st[sl, :] for sl in sls]
    vn = [a - _mm(b, s) for a, b, s in zip(u, w, sp)]
    o = [_mm(q * e, s) + _mm(a, _bd(v)) for q, e, s, a, v in zip(qn, eg, sp, attn, vn)]
    new = [s * d + jnp.where(diag_blocks, _mm_tn(k * e, v), 0.0) for s, d, k, e, v in zip(sp, cd, kn, et, vn)]
    outs = []
    for p in pairs:
        for hh in range(2):
            osl = slice(128 * hh, 128 * hh + 128)
            zsl = slice(256 * p + 128 * hh, 256 * p + 128 * hh + 128)
            outs.append(_unit_rms(o[p][:, osl]) * nw * _silu(z[:, zsl]))
    return (jnp.concatenate(outs, axis=1),), [qr[CHUNK - 8:, :], kr[CHUNK - 8:, :], vr[CHUNK - 8:, :],
                                             jnp.concatenate(new, axis=0)]


def _f_s5(tabs, consts, xs, xtabs, states):
    lam_re, lam_im, bblk, c_re, c_im, dskip, wglu, bglu = consts
    (u,) = xs
    s_re, s_im = states
    rows = u.shape[0]
    n = lam_re.shape[1]
    bu = _mm(u, bblk)
    hr, hi = bu[:, 0:n], bu[:, n:2 * n]
    row = _iota((rows, n), 0)
    h0r, h0i = s_re[0:1, :], s_im[0:1, :]
    hr = hr + jnp.where(row == 0, lam_re * h0r - lam_im * h0i, 0.0)
    hi = hi + jnp.where(row == 0, lam_re * h0i + lam_im * h0r, 0.0)
    pr, pi = lam_re, lam_im
    d = 1
    while d < rows:
        sr = jnp.where(row >= d, _shift(d, 0)(hr), 0.0)
        si = jnp.where(row >= d, _shift(d, 0)(hi), 0.0)
        hr, hi = hr + pr * sr - pi * si, hi + pr * si + pi * sr
        pr, pi = pr * pr - pi * pi, 2.0 * pr * pi
        d *= 2
    y = _mm(hr, c_re) - _mm(hi, c_im) + dskip * u
    y = jax.nn.gelu(y)
    out = y * jax.nn.sigmoid(_mm(y, wglu) + bglu)
    last_r = jnp.broadcast_to(hr[rows - 1:rows, :], (8, n))
    last_i = jnp.broadcast_to(hi[rows - 1:rows, :], (8, n))
    return (out,), [last_r, last_i]


def _full_spec(a):
    nd = a.ndim
    return pl.BlockSpec(a.shape, lambda i, _nd=nd: (0,) * _nd)


CHUNKS_PER_STEP = 4


def _chunks_per_step(f, rows, n):
    def g(tabs, consts, xs, xtabs, states):
        ys = []
        for i in range(n):
            sl = slice(rows * i, rows * (i + 1))
            (y,), states = f(tabs, consts, [t[sl] for t in xs], [t[sl] for t in xtabs], states)
            ys.append(y)
        return (jnp.concatenate(ys, axis=0),), states

    return g


def _scan_fwd(name, f, rows, tabs, consts, xs, xtabs, state_shapes, y_total, y_width, y_cb, y_alias=None):
    seq = xs[0][0].shape[0]
    per_step = math.gcd(CHUNKS_PER_STEP, seq // rows)
    f = _chunks_per_step(f, rows, per_step)
    rows = rows * per_step
    nc = seq // rows
    nt, ncst, nx, nxt, ns = len(tabs), len(consts), len(xs), len(xtabs), len(state_shapes)
    alias = y_alias is not None

    def body(*refs):
        p = 0
        tab_r = refs[p:p + nt]; p += nt
        c_r = refs[p:p + ncst]; p += ncst
        x_r = refs[p:p + nx]; p += nx
        xt_r = refs[p:p + nxt]; p += nxt
        if alias:
            p += 1
        y_ref = refs[p]; p += 1
        sv_r = refs[p:p + ns]; p += ns
        st_r = refs[p:p + ns]

        @pl.when(pl.program_id(0) == 0)
        def _():
            for s in st_r:
                s[...] = jnp.zeros(s.shape, F32)

        st = [s[...] for s in st_r]
        for r, v in zip(sv_r, st):
            r[...] = v
        (y,), new = f([r[...] for r in tab_r], [r[...] for r in c_r], [r[...].astype(F32) for r in x_r],
                      [r[...] for r in xt_r], st)
        y_ref[...] = y.astype(y_ref.dtype)
        for s, v in zip(st_r, new):
            s[...] = v

    win = [pl.BlockSpec((rows, w), lambda i, _cb=cb: (i, _cb)) for (_, w, cb) in list(xs) + list(xtabs)]
    in_specs = [_full_spec(a) for a in list(tabs) + list(consts)] + win
    args = list(tabs) + list(consts) + [a for (a, _, _) in list(xs) + list(xtabs)]
    io_alias = {}
    if alias:
        in_specs.append(pl.BlockSpec(memory_space=pl.ANY))
        io_alias = {len(args): 0}
        args.append(y_alias)
    out_shape = [jax.ShapeDtypeStruct((seq, y_total), _MXU_DTYPE)]
    out_specs = [pl.BlockSpec((rows, y_width), lambda i: (i, y_cb))]
    for (r, c) in state_shapes:
        out_shape.append(jax.ShapeDtypeStruct((nc * r, c), F32))
        out_specs.append(pl.BlockSpec((r, c), lambda i: (i, 0)))
    res = pl.pallas_call(
        body, name=name, grid=(nc,), in_specs=in_specs, out_specs=out_specs, out_shape=out_shape,
        scratch_shapes=[pltpu.VMEM(s, F32) for s in state_shapes], input_output_aliases=io_alias,
        compiler_params=pltpu.CompilerParams(dimension_semantics=("arbitrary",), vmem_limit_bytes=VMEM_LIMIT),
    )(*args)
    return res[0], list(res[1:])


def _scan_bwd(name, f, rows, tabs, consts, xs, xtabs, saved, state_shapes, dy, dx_total, dx_width, dx_cb,
              assemble, dx_alias=None):
    seq = xs[0][0].shape[0]
    per_step = math.gcd(CHUNKS_PER_STEP, seq // rows)
    f = _chunks_per_step(f, rows, per_step)
    rows = rows * per_step
    nc = seq // rows
    nt, ncst, nx, nxt, ns = len(tabs), len(consts), len(xs), len(xtabs), len(state_shapes)
    alias = dx_alias is not None

    def body(*refs):
        p = 0
        tab_r = refs[p:p + nt]; p += nt
        c_r = refs[p:p + ncst]; p += ncst
        x_r = refs[p:p + nx]; p += nx
        xt_r = refs[p:p + nxt]; p += nxt
        sv_r = refs[p:p + ns]; p += ns
        dy_ref = refs[p]; p += 1
        if alias:
            p += 1
        dx_ref = refs[p]; p += 1
        dc_r = refs[p:p + ncst]; p += ncst
        ds_r = refs[p:p + ns]

        @pl.when(pl.program_id(0) == 0)
        def _():
            for s in ds_r:
                s[...] = jnp.zeros(s.shape, F32)
            for r in dc_r:
                r[...] = jnp.zeros(r.shape, F32)

        tab_v = [r[...] for r in tab_r]
        xt_v = [r[...] for r in xt_r]

        def g(c, x, s):
            (y,), new = f(tab_v, c, x, xt_v, s)
            return y, new

        _, vjp = jax.vjp(g, [r[...] for r in c_r], [r[...].astype(F32) for r in x_r], [r[...] for r in sv_r])
        dc, dx, ds = vjp((dy_ref[...], [s[...] for s in ds_r]))
        dx_ref[...] = assemble(dx).astype(dx_ref.dtype)
        for r, v in zip(dc_r, dc):
            r[...] += v
        for s, v in zip(ds_r, ds):
            s[...] = v

    win = [pl.BlockSpec((rows, w), lambda j, _cb=cb: (nc - 1 - j, _cb)) for (_, w, cb) in list(xs) + list(xtabs)]
    in_specs = [_full_spec(a) for a in list(tabs) + list(consts)] + win
    args = list(tabs) + list(consts) + [a for (a, _, _) in list(xs) + list(xtabs)]
    for (r, c), sv in zip(state_shapes, saved):
        in_specs.append(pl.BlockSpec((r, c), lambda j: (nc - 1 - j, 0)))
        args.append(sv)
    in_specs.append(pl.BlockSpec((rows, dy[1]), lambda j: (nc - 1 - j, dy[2])))
    args.append(dy[0])
    io_alias = {}
    if alias:
        in_specs.append(pl.BlockSpec(memory_space=pl.ANY))
        io_alias = {len(args): 0}
        args.append(dx_alias)
    out_shape = [jax.ShapeDtypeStruct((seq, dx_total), _MXU_DTYPE)] +[jax.ShapeDtypeStruct(a.shape, F32) for a in consts]
    out_specs = [pl.BlockSpec((rows, dx_width), lambda j: (nc - 1 - j, dx_cb))] + [_full_spec(a) for a in consts]
    res = pl.pallas_call(
        body, name=name, grid=(nc,), in_specs=in_specs, out_specs=out_specs, out_shape=out_shape,
        scratch_shapes=[pltpu.VMEM(s, F32) for s in state_shapes], input_output_aliases=io_alias,
        compiler_params=pltpu.CompilerParams(dimension_semantics=("arbitrary",), vmem_limit_bytes=VMEM_LIMIT),
    )(*args)
    return res[0], list(res[1:])


def _tile(n, want):
    t = min(n, want)
    while n % t:
        t //= 2
    return t


MATMUL_VMEM_BUDGET = 40 * 1024 * 1024


MXU_FLOPS_PER_S = 8.5e14
HBM_BYTES_PER_S = 2.8e12
GRID_STEP_S = 0.35e-6


def _pick_tiles(m, n, k, sa, sb, so, se, whole_rows=False, reduce_rows=False, tn_fixed=None, tm_divides=None):
    best = None
    tns = {tn_fixed} if tn_fixed else ({n} if whole_rows else {_tile(n, t) for t in (4096, 2048, 1024, 512)})
    tms = {_tile(m, t) for t in (2048, 1024, 512)}
    if tm_divides:
        tms = {t for t in (1024, 512, 256) if tm_divides % t == 0 and m % t == 0}
    for tn in tns:
        for tm in tms:
            for tk in {_tile(k, t) for t in (4096, 2048, 1024, 512)}:
                at, bt, ot = tm * tk * sa, tk * tn * sb, tm * tn * so
                need = 2 * (at + bt + ot + tm * tn * se) + 2 * tm * tn * 4 + (at if sa == 4 else 0) + (bt if sb == 4 else 0)
                if need > MATMUL_VMEM_BUDGET:
                    continue
                ni, nj, nk = m // tm, n // tn, k // tk
                b_reads = ni if (reduce_rows or nk > 1) else 1
                moved = m * k * sa * nj + k * n * sb * b_reads + m * n * (so + se)
                cost = max(2 * m * n * k / MXU_FLOPS_PER_S, moved / HBM_BYTES_PER_S) + ni * nj * nk * GRID_STEP_S
                key = (cost, nk, -tm)
                if best is None or key < best[0]:
                    best = (key, (tm, tn, tk))
    assert best is not None, (m, n, k)
    return best[1]


def _matmul(name, a, b, mode, out_dtype=F32, a_pro=None, epi=None, epi_arr=None, norm_w=None, norm_x=None, slab=None):
    if mode == "nn":
        (m, k), (k2, n) = a.shape, b.shape
    elif mode == "nt":
        (m, k), (n, k2) = a.shape, b.shape
    else:
        (k, m), (k2, n) = a.shape, b.shape
    assert k == k2, (name, a.shape, b.shape)
    size = lambda t: jnp.dtype(t).itemsize
    rows_in = [] if epi is None else [epi_arr] + ([norm_x] if epi == "norm_bwd" else [])
    emit_norm = epi == "add" and norm_w is not None
    extra = sum(size(t.dtype) for t in rows_in) + (size(_MXU_DTYPE) if emit_norm else 0)
    if slab is None:
        tm, tn, tk = _pick_tiles(m, n, k, size(a.dtype), size(b.dtype), size(out_dtype), extra,
                                 whole_rows=norm_w is not None, reduce_rows=mode == "tn")
    else:
        prev_slab, slab_rows, first_row, shard_rows = slab
        assert mode == "tn" and epi is None and n % LANES == 0, name
        tm, tn, tk = _pick_tiles(m, n, k, size(a.dtype), size(b.dtype), size(out_dtype), extra, reduce_rows=True,
                                 tn_fixed=LANES, tm_divides=math.gcd(shard_rows or m, first_row or m))
    nk = k // tk
    ca, cb = {"nn": (1, 0), "nt": (1, 1), "tn": (0, 0)}[mode]
    n_in = 2 + len(rows_in) + (norm_w is not None) + (slab is not None and slab[0] is not None)
    n_out = 2 if (emit_norm or epi == "norm_bwd") else 1

    def body(*refs):
        refs = list(refs)
        acc = refs.pop() if nk > 1 else None
        a_ref, b_ref = refs[0], refs[1]
        e_ref = refs[2] if epi is not None else None
        x_ref = refs[3] if epi == "norm_bwd" else None
        w_ref = refs[n_in - 1] if norm_w is not None else None
        o_ref = refs[n_in]
        o2_ref = refs[n_in + 1] if n_out == 2 else None
        kk = pl.program_id(2)

        if epi == "norm_bwd":
            @pl.when((pl.program_id(1) == 0) & (kk == 0))
            def _():
                o2_ref[...] = jnp.zeros(o2_ref.shape, F32)

        av = a_ref[...]
        if a_pro == "relu2":
            r = jnp.maximum(av, 0.0)
            av = r * r
        part = _dg(_lo(av), _lo(b_ref[...]), ca, cb)

        def finish(r):
            if epi == "add":
                r = r + e_ref[...]
                if emit_norm:
                    o2_ref[...] = (_unit_rms(r) * w_ref[...]).astype(_MXU_DTYPE)
            elif epi == "drelu2":
                r = r * (2.0 * jnp.maximum(e_ref[...], 0.0))
            elif epi == "norm_bwd":
                xv = x_ref[...]
                rstd = lax.rsqrt(jnp.mean(xv * xv, axis=-1, keepdims=True) + EPS)
                xh = xv * rstd
                g = r * w_ref[...]
                o2_ref[...] += jnp.sum(r * xh, axis=0, keepdims=True)
                r = e_ref[...] + rstd * (g - xh * jnp.mean(g * xh, axis=-1, keepdims=True))
            o_ref[...] = r.astype(out_dtype).reshape(o_ref.shape)

        if nk == 1:
            finish(part)
        else:
            @pl.when(kk == 0)
            def _():
                acc[...] = part

            @pl.when(kk > 0)
            def _():
                acc[...] += part

            @pl.when(kk == nk - 1)
            def _():
                finish(acc[...])

    if mode == "tn":
        a_spec = pl.BlockSpec((tk, tm), lambda j, i, kk: (kk, i))
    else:
        a_spec = pl.BlockSpec((tm, tk), lambda j, i, kk: (i, kk))
    if mode == "nt":
        b_spec = pl.BlockSpec((tn, tk), lambda j, i, kk: (j, kk))
    else:
        b_spec = pl.BlockSpec((tk, tn), lambda j, i, kk: (kk, j))
    o_spec = pl.BlockSpec((tm, tn), lambda j, i, kk: (i, j))
    vec_spec = pl.BlockSpec((1, tn), lambda j, i, kk: (0, j))
    in_specs, args = [a_spec, b_spec] + [o_spec] * len(rows_in), [a, b] + rows_in
    if norm_w is not None:
        in_specs.append(vec_spec)
        args.append(norm_w)
    out_specs, out_shape = [o_spec], [jax.ShapeDtypeStruct((m, n), out_dtype)]
    io_alias = {}
    if slab is not None:
        first_blk = first_row // tm
        if shard_rows is None:
            out_specs = [pl.BlockSpec((1, tm, tn), lambda j, i, kk: (j, first_blk + i, 0))]
        else:
            per = shard_rows // tm
            out_specs = [pl.BlockSpec((1, tm, tn), lambda j, i, kk: (i // per, first_blk + i % per, 0))]
        out_shape = [jax.ShapeDtypeStruct((4, slab_rows, LANES), out_dtype)]
        if prev_slab is not None:
            in_specs.append(pl.BlockSpec(memory_space=pl.ANY))
            io_alias = {len(args): 0}
            args.append(prev_slab)
    if emit_norm:
        out_specs.append(o_spec)
        out_shape.append(jax.ShapeDtypeStruct((m, n), _MXU_DTYPE))
    elif epi == "norm_bwd":
        out_specs.append(vec_spec)
        out_shape.append(jax.ShapeDtypeStruct((1, n), F32))
    sem = ("parallel", "arbitrary" if epi == "norm_bwd" else "parallel", "arbitrary")
    res = pl.pallas_call(
        body, name=name, grid=(n // tn, m // tm, nk), in_specs=in_specs, out_specs=out_specs, out_shape=out_shape,
        scratch_shapes=[pltpu.VMEM((tm, tn), F32)] if nk > 1 else [], input_output_aliases=io_alias,
        compiler_params=pltpu.CompilerParams(dimension_semantics=sem, vmem_limit_bytes=VMEM_LIMIT),
    )(*args)
    return res[0] if n_out == 1 else res


ROW_TILE = 512


def _rmsnorm_fwd(name, x, w):
    seq, d = x.shape
    tr = _tile(seq, ROW_TILE)

    def body(x_ref, w_ref, o_ref):
        xv = x_ref[...]
        o_ref[...] = (_unit_rms(xv) * w_ref[...]).astype(_MXU_DTYPE)

    return pl.pallas_call(
        body, name=name, grid=(seq // tr,),
        in_specs=[pl.BlockSpec((tr, d), lambda i: (i, 0)), pl.BlockSpec((1, d), lambda i: (0, 0))],
        out_specs=pl.BlockSpec((tr, d), lambda i: (i, 0)), out_shape=jax.ShapeDtypeStruct((seq, d), _MXU_DTYPE),
        compiler_params=pltpu.CompilerParams(dimension_semantics=("parallel",), vmem_limit_bytes=VMEM_LIMIT),
    )(x, w)


def _loss_head(name, x, w, target):
    seq, d = x.shape
    tr = _tile(seq, ROW_TILE)

    def body(x_ref, w_ref, t_ref, loss_ref, dx_ref, dw_ref):
        @pl.when(pl.program_id(0) == 0)
        def _():
            dw_ref[...] = jnp.zeros(dw_ref.shape, F32)
            loss_ref[...] = jnp.zeros(loss_ref.shape, F32)

        xv = x_ref[...]
        rstd = lax.rsqrt(jnp.mean(xv * xv, axis=-1, keepdims=True) + EPS)
        xh = xv * rstd
        err = xh * w_ref[...] - t_ref[...]
        per_row = jnp.mean(err * err, axis=-1, keepdims=True)
        loss_ref[...] += 0.5 * jnp.sum(per_row, axis=0, keepdims=True)
        dy = err * (1.0 / d)
        g = dy * w_ref[...]
        dx_ref[...] = rstd * (g - xh * jnp.mean(g * xh, axis=-1, keepdims=True))
        dw_ref[...] += jnp.sum(dy * xh, axis=0, keepdims=True)

    row = pl.BlockSpec((tr, d), lambda i: (i, 0))
    vec = pl.BlockSpec((1, d), lambda i: (0, 0))
    one = pl.BlockSpec((1, 1), lambda i: (0, 0))
    return pl.pallas_call(
        body, name=name, grid=(seq // tr,), in_specs=[row, vec, row], out_specs=[one, row, vec],
        out_shape=[jax.ShapeDtypeStruct((1, 1), F32), jax.ShapeDtypeStruct((seq, d), F32),
                   jax.ShapeDtypeStruct((1, d), F32)],
        compiler_params=pltpu.CompilerParams(dimension_semantics=("arbitrary",), vmem_limit_bytes=VMEM_LIMIT),
    )(x, w, target)


SLAB_TILE_ROWS = 1024


def _slab_tile(rows, cap=SLAB_TILE_ROWS):
    step = 16 if rows % 16 == 0 else 8
    return max(t for t in range(step, min(rows, cap) + 1, step) if rows % t == 0)


def _adamw(name, w, g, m, v):
    rows, cols = w.shape
    tr = _slab_tile(rows, SLAB_TILE_ROWS // 2) if rows % 8 == 0 else rows

    def body(w_ref, g_ref, m_ref, v_ref, d_ref, nm_ref, nv_ref):
        gv = g_ref[...]
        nm = ADAM_B1 * m_ref[...] + (1.0 - ADAM_B1) * gv
        nv = ADAM_B2 * v_ref[...] + (1.0 - ADAM_B2) * (gv * gv)
        m_hat = nm / (1.0 - ADAM_B1 ** ADAM_STEP)
        v_hat = nv / (1.0 - ADAM_B2 ** ADAM_STEP)
        d_ref[...] = -ADAM_LR * (m_hat / (jnp.sqrt(v_hat) + ADAM_EPS) + ADAM_WD * w_ref[...])
        nm_ref[...] = nm
        nv_ref[...] = nv

    spec = pl.BlockSpec((tr, cols), lambda i: (i, 0))
    sds = jax.ShapeDtypeStruct(w.shape, F32)
    return pl.pallas_call(
        body, name=name, grid=(rows // tr,), in_specs=[spec] * 4, out_specs=[spec] * 3, out_shape=[sds] * 3,
        compiler_params=pltpu.CompilerParams(dimension_semantics=("parallel",), vmem_limit_bytes=VMEM_LIMIT),
    )(w, g, m, v)


def _place_rows(name, slab, tail, first_row):
    nsec, rows, _ = tail.shape
    tr = math.gcd(rows, first_row)
    tr = _slab_tile(tr, SLAB_TILE_ROWS // 2)
    first_blk = first_row // tr

    def body(t_ref, s_ref, o_ref):
        o_ref[...] = t_ref[...]

    return pl.pallas_call(
        body, name=name, grid=(nsec, rows // tr),
        in_specs=[pl.BlockSpec((1, tr, LANES), lambda s, i: (s, i, 0)), pl.BlockSpec(memory_space=pl.ANY)],
        out_specs=pl.BlockSpec((1, tr, LANES), lambda s, i: (s, first_blk + i, 0)),
        out_shape=jax.ShapeDtypeStruct(slab.shape, slab.dtype), input_output_aliases={1: 0},
        compiler_params=pltpu.CompilerParams(dimension_semantics=("parallel", "parallel"),
                                             vmem_limit_bytes=VMEM_LIMIT),
    )(tail, slab)


WIRE_DTYPE = jnp.bfloat16


def _add_halves(name, g, t1, c):
    nsec, rows, _ = g.shape
    rh = rows // 2
    tr = _slab_tile(rh)
    nb = rh // tr

    def body(c_ref, g_ref, t_ref, o_ref):
        o_ref[...] = (g_ref[...] + t_ref[...]).astype(o_ref.dtype)

    gs = pltpu.PrefetchScalarGridSpec(
        num_scalar_prefetch=1, grid=(nsec, nb),
        in_specs=[pl.BlockSpec((1, tr, LANES), lambda s, i, c_ref: (s, c_ref[0] * nb + i, 0)),
                  pl.BlockSpec((1, tr, LANES), lambda s, i, c_ref: (s, i, 0))],
        out_specs=pl.BlockSpec((1, tr, LANES), lambda s, i, c_ref: (s, i, 0)))
    return pl.pallas_call(
        body, name=name, grid_spec=gs, out_shape=jax.ShapeDtypeStruct((nsec, rh, LANES), WIRE_DTYPE),
        compiler_params=pltpu.CompilerParams(dimension_semantics=("parallel", "parallel"),
                                             vmem_limit_bytes=VMEM_LIMIT),
    )(c, g, t1)


ANY = pl.BlockSpec(memory_space=pl.ANY)


def _place():
    return lax.axis_index("x"), lax.axis_index("y"), lax.axis_index("c")


def _all_gather_shards(name, slab):
    rows = slab.shape[0]
    rh = rows // 2
    rq = rh // 2

    def body(x_ref, out_ref, send_sems, recv_sems):
        x, y, c = _place()
        me, sibling = (x, y, c), (x, y, 1 - c)
        xn, yn, dg = (1 - x, y), (x, 1 - y), (1 - x, 1 - y)

        def piece(chip, core, q):
            return out_ref.at[2 * chip[0] + chip[1], pl.ds(core * rh + q * rq, rq), :]

        def copy(k, chip, core, q, to, src=None):
            return pltpu.make_async_remote_copy(
                src_ref=piece(chip, core, q) if src is None else src, dst_ref=piece(chip, core, q),
                send_sem=send_sems.at[k], recv_sem=recv_sems.at[k], device_id=to, device_id_type=MESH)

        own = [x_ref.at[pl.ds(c * rh + q * rq, rq), :] for q in range(2)]
        sends = [copy(0, (x, y), c, 0, (*xn, c), src=own[0]), copy(1, (x, y), c, 1, (*xn, c), src=own[1]),
                 copy(2, (x, y), c, 0, (*yn, c), src=own[0]), copy(3, (x, y), c, 1, (*yn, c), src=own[1])]
        for cp in sends:
            cp.start()
        landed = [(0, xn, 0), (3, yn, 1), (1, xn, 1), (2, yn, 0), (4, dg, 0), (5, dg, 1)]
        onward = {0: (4, (*yn, c)), 3: (5, (*xn, c))}
        for i, (k, chip, q) in enumerate(landed):
            copy(k, chip, c, q, me).wait_recv()
            if k in onward:
                fk, to = onward[k]
                sends.append(copy(fk, chip, c, q, to))
                sends[-1].start()
            sends.append(copy(6 + i, chip, c, q, sibling))
            sends[-1].start()
        for i, (k, chip, q) in enumerate(landed):
            copy(6 + i, chip, 1 - c, q, me).wait_recv()
        for cp in sends:
            cp.wait_send()

    got = pl.pallas_call(
        body, name=name, in_specs=[ANY], out_specs=ANY,
        out_shape=jax.ShapeDtypeStruct((4, rows, LANES), slab.dtype),
        scratch_shapes=[pltpu.SemaphoreType.DMA((12,)), pltpu.SemaphoreType.DMA((12,))],
    )(slab)
    return lax.dynamic_update_slice(got, slab[None], (2 * lax.axis_index("x") + lax.axis_index("y"), 0, 0))


def _swap_halves(name, g):
    nsec, rows, _ = g.shape
    rh = rows // 2

    def body(g_ref, t_ref, send_sem, recv_sem):
        x, y, c = _place()
        cp = pltpu.make_async_remote_copy(
            src_ref=g_ref.at[:, pl.ds((1 - c) * rh, rh), :], dst_ref=t_ref, send_sem=send_sem, recv_sem=recv_sem,
            device_id=(x, y, 1 - c), device_id_type=MESH)
        cp.start()
        cp.wait()

    return pl.pallas_call(
        body, name=name, in_specs=[ANY], out_specs=ANY, out_shape=jax.ShapeDtypeStruct((nsec, rh, LANES), F32),
        scratch_shapes=[pltpu.SemaphoreType.DMA, pltpu.SemaphoreType.DMA],
    )(g)


def _exchange_stage1(name, p):
    _, rh, _ = p.shape
    rq = rh // 2

    def body(p_ref, fx_ref, fy_ref, send_sems, recv_sems):
        x, y, c = _place()
        to_x = pltpu.make_async_remote_copy(
            src_ref=p_ref.at[pl.ds(2 * (1 - x), 2), pl.ds(0, rq), :], dst_ref=fx_ref, send_sem=send_sems.at[0],
            recv_sem=recv_sems.at[0], device_id=(1 - x, y, c), device_id_type=MESH)
        to_y = [pltpu.make_async_remote_copy(
            src_ref=p_ref.at[2 * sx + (1 - y), pl.ds(rq, rq), :], dst_ref=fy_ref.at[sx], send_sem=send_sems.at[1 + sx],
            recv_sem=recv_sems.at[1 + sx], device_id=(x, 1 - y, c), device_id_type=MESH) for sx in range(2)]
        for cp in [to_x] + to_y:
            cp.start()
        for cp in [to_x] + to_y:
            cp.wait_recv()
        for cp in [to_x] + to_y:
            cp.wait_send()

    sds = jax.ShapeDtypeStruct((2, rq, LANES), p.dtype)
    return pl.pallas_call(
        body, name=name, in_specs=[ANY], out_specs=[ANY, ANY], out_shape=[sds, sds],
        scratch_shapes=[pltpu.SemaphoreType.DMA((3,)), pltpu.SemaphoreType.DMA((3,))],
    )(p)


def _exchange_add1(name, p, from_x, from_y, place):
    _, rh, _ = p.shape
    rq = rh // 2
    tr = _slab_tile(rq)
    nb = rq // tr

    def body(xy_ref, pa_s, pa_k, pb_s, pb_k, fx_s, fx_k, fy_s, fy_k, sa, ka, sb, kb):
        for mine, theirs, out in ((pa_s, fx_s, sa), (pa_k, fx_k, ka), (pb_s, fy_s, sb), (pb_k, fy_k, kb)):
            out[...] = (mine[0].astype(F32) + theirs[0].astype(F32)).astype(out.dtype)

    blk = lambda fn: pl.BlockSpec((1, tr, LANES), fn)
    gs = pltpu.PrefetchScalarGridSpec(
        num_scalar_prefetch=1, grid=(nb,),
        in_specs=[blk(lambda i, xy: (2 * xy[0] + 1 - xy[1], i, 0)), blk(lambda i, xy: (2 * xy[0] + xy[1], i, 0)),
                  blk(lambda i, xy: (2 * (1 - xy[0]) + xy[1], nb + i, 0)), blk(lambda i, xy: (2 * xy[0] + xy[1], nb + i, 0)),
                  blk(lambda i, xy: (1 - xy[1], i, 0)), blk(lambda i, xy: (xy[1], i, 0)),
                  blk(lambda i, xy: (1 - xy[0], i, 0)), blk(lambda i, xy: (xy[0], i, 0))],
        out_specs=[pl.BlockSpec((tr, LANES), lambda i, xy: (i, 0))] * 4)
    sds = jax.ShapeDtypeStruct((rq, LANES), p.dtype)
    return pl.pallas_call(
        body, name=name, grid_spec=gs, out_shape=[sds] * 4,
        compiler_params=pltpu.CompilerParams(dimension_semantics=("parallel",), vmem_limit_bytes=VMEM_LIMIT),
    )(place, p, p, p, p, from_x, from_x, from_y, from_y)


def _exchange_stage2(name, send_a, send_b):
    def body(a_ref, b_ref, fa_ref, fb_ref, send_sems, recv_sems):
        x, y, c = _place()
        cps = [pltpu.make_async_remote_copy(src_ref=a_ref, dst_ref=fa_ref, send_sem=send_sems.at[0],
                                            recv_sem=recv_sems.at[0], device_id=(x, 1 - y, c), device_id_type=MESH),
               pltpu.make_async_remote_copy(src_ref=b_ref, dst_ref=fb_ref, send_sem=send_sems.at[1],
                                            recv_sem=recv_sems.at[1], device_id=(1 - x, y, c), device_id_type=MESH)]
        for cp in cps:
            cp.start()
        for cp in cps:
            cp.wait_recv()
        for cp in cps:
            cp.wait_send()

    sds = jax.ShapeDtypeStruct(send_a.shape, send_a.dtype)
    return pl.pallas_call(
        body, name=name, in_specs=[ANY, ANY], out_specs=[ANY, ANY], out_shape=[sds, sds],
        scratch_shapes=[pltpu.SemaphoreType.DMA((2,)), pltpu.SemaphoreType.DMA((2,))],
    )(send_a, send_b)


def _exchange_add2(name, keep_a, got_a, keep_b, got_b, c):
    rq = keep_a.shape[0]
    tr = _slab_tile(rq)

    def body(c_ref, ka, ga, kb, gb, o_ref):
        o_ref[0] = ka[...].astype(F32) + ga[...].astype(F32)
        o_ref[1] = kb[...].astype(F32) + gb[...].astype(F32)

    spec = pl.BlockSpec((tr, LANES), lambda i, c_ref: (i, 0))
    gs = pltpu.PrefetchScalarGridSpec(
        num_scalar_prefetch=1, grid=(rq // tr,), in_specs=[spec] * 4,
        out_specs=pl.BlockSpec((2, tr, LANES), lambda i, c_ref: (c_ref[0], i, 0)))
    out = pl.pallas_call(
        body, name=name, grid_spec=gs, out_shape=jax.ShapeDtypeStruct((4, rq, LANES), F32),
        compiler_params=pltpu.CompilerParams(dimension_semantics=("parallel",), vmem_limit_bytes=VMEM_LIMIT),
    )(c, keep_a, got_a, keep_b, got_b)
    return out.reshape(4 * rq, LANES)


def _join_halves(name, full):
    rh = full.shape[0] // 2

    def body(in_ref, o_ref, send_sem, recv_sem):
        x, y, c = _place()
        cp = pltpu.make_async_remote_copy(
            src_ref=in_ref.at[pl.ds(c * rh, rh), :], dst_ref=o_ref.at[pl.ds(c * rh, rh), :], send_sem=send_sem,
            recv_sem=recv_sem, device_id=(x, y, 1 - c), device_id_type=MESH)
        cp.start()
        pltpu.make_async_remote_copy(
            src_ref=in_ref.at[pl.ds(c * rh, rh), :], dst_ref=o_ref.at[pl.ds((1 - c) * rh, rh), :], send_sem=send_sem,
            recv_sem=recv_sem, device_id=(x, y, 1 - c), device_id_type=MESH).wait_recv()
        cp.wait_send()

    return pl.pallas_call(
        body, name=name, in_specs=[ANY], out_specs=ANY, out_shape=jax.ShapeDtypeStruct(full.shape, full.dtype),
        input_output_aliases={0: 0}, scratch_shapes=[pltpu.SemaphoreType.DMA, pltpu.SemaphoreType.DMA],
    )(full)


def _rows_of(n):
    return -(-n // LANES)


SLAB_ROW_ALIGN = 512


def _flat_rows(arrays, dtype):
    parts = []
    for a in arrays:
        flat = a.reshape(-1).astype(dtype)
        parts.append(jnp.pad(flat, (0, _rows_of(flat.size) * LANES - flat.size)))
    return jnp.concatenate(parts).reshape(-1, LANES)


def _align_rows(slab):
    rows = slab.shape[0]
    return jnp.pad(slab, ((0, -(-rows // SLAB_ROW_ALIGN) * SLAB_ROW_ALIGN - rows), (0, 0)))


def _pack(arrays, dtype):
    return _align_rows(_flat_rows(arrays, dtype))


def _unpack(slab, shapes):
    out, r = [], 0
    for shp in shapes:
        n = math.prod(shp)
        out.append(slab[r:r + _rows_of(n)].reshape(-1)[:n].reshape(shp))
        r += _rows_of(n)
    return out


def _unpack_gathered(g, shapes, kinds):
    out, r = [], 0
    for shp, kind in zip(shapes, kinds):
        n = math.prod(shp)
        blk = g[:, r:r + _rows_of(n)].reshape(4, -1)[:, :n].reshape((4,) + tuple(shp))
        r += _rows_of(n)
        if kind == "col":
            out.append(jnp.moveaxis(blk, 0, 1).reshape(shp[0], 4 * shp[1]))
        else:
            out.append(blk.reshape(4 * shp[0], shp[1]))
    return out


def _sections(g, kind, local_shape):
    if kind == "col":
        blocks = jnp.moveaxis(g.reshape(local_shape[0], 4, local_shape[1]), 1, 0)
    elif kind == "row":
        blocks = g.reshape((4,) + tuple(local_shape))
    else:
        blocks = jnp.broadcast_to(g, (4,) + tuple(g.shape))
    flat = blocks.reshape(4, -1)
    rows = _rows_of(flat.shape[1])
    return jnp.pad(flat, ((0, 0), (0, rows * LANES - flat.shape[1]))).reshape(4, rows, LANES)


def _rotary_tables(seq):
    half = RET_DK // 2
    pos = jnp.arange(seq, dtype=F32)
    inv = ROPE_THETA ** (-jnp.arange(half, dtype=F32) / half)
    ang = pos[:, None] * inv[None, :]
    cos, sin = jnp.cos(ang), jnp.sin(ang)
    return jnp.concatenate([cos, cos], axis=1), jnp.concatenate([-sin, sin], axis=1)


def _retention_tables():
    log_gamma = jnp.log(1.0 - 2.0 ** (-5.0 - jnp.arange(RET_HEADS, dtype=F32)))
    idx = jnp.arange(CHUNK, dtype=F32)
    diff = idx[:, None] - idx[None, :]
    dmask = jnp.exp(jnp.where((diff >= 0)[None], log_gamma[:, None, None] * diff[None], -jnp.inf))
    kdec = jnp.exp(log_gamma[None, :] * (CHUNK - 1.0 - idx)[:, None])
    qdec = jnp.exp(log_gamma[None, :] * (idx + 1.0)[:, None])
    cdec = jnp.exp(log_gamma * CHUNK)[None, :]
    lanes = lambda t: jnp.repeat(t, RET_DK, axis=1)
    return dmask.reshape(RET_HEADS * CHUNK, CHUNK), lanes(kdec), lanes(qdec), lanes(cdec)


def _s5_prep(a_re, a_im, log_step, b_re, b_im, c_re, c_im):
    g, n, c = S5_GROUPS, S5_STATE, S5_GROUP
    lam = lax.complex(a_re, a_im)
    step = jnp.exp(log_step)[:, None]
    lam_bar = jnp.exp(lam * step)
    b_bar = ((lam_bar - 1.0) / lam)[..., None] * lax.complex(b_re, b_im)
    eye = jnp.eye(g, dtype=F32)
    bb_re = (jnp.real(b_bar).transpose(0, 2, 1)[:, :, None, :] * eye[:, None, :, None]).reshape(g * c, g * n)
    bb_im = (jnp.imag(b_bar).transpose(0, 2, 1)[:, :, None, :] * eye[:, None, :, None]).reshape(g * c, g * n)
    cc_re = (c_re.transpose(0, 2, 1)[:, :, None, :] * eye[:, None, :, None]).reshape(g * n, g * c)
    cc_im = (c_im.transpose(0, 2, 1)[:, :, None, :] * eye[:, None, :, None]).reshape(g * n, g * c)
    return (jnp.real(lam_bar).reshape(1, g * n), jnp.imag(lam_bar).reshape(1, g * n),
            jnp.concatenate([bb_re, bb_im], axis=1), cc_re, cc_im)


def kernel(x, l0_norm_mix, l0_w_in, ssd_conv_w, ssd_conv_b, ssd_dt_bias, ssd_A_log, ssd_D, ssd_norm_w, l0_w_out, l0_norm_mlp, l0_w_up, l0_w_down, l1_norm_mix, l1_w_in, gdn_conv_w, gdn_A_log, gdn_dt_bias, gdn_norm_w, s5_A_re, s5_A_im, s5_log_step, s5_B_re, s5_B_im, s5_C_re, s5_C_im, s5_D, s5_w_glu, s5_b_glu, l1_w_out, l1_norm_mlp, l1_w_up, l1_w_down, final_norm, loss_target, m_l0_norm_mix, m_l0_w_in, m_ssd_conv_w, m_ssd_conv_b, m_ssd_dt_bias, m_ssd_A_log, m_ssd_D, m_ssd_norm_w, m_l0_w_out, m_l0_norm_mlp, m_l0_w_up, m_l0_w_down, m_l1_norm_mix, m_l1_w_in, m_gdn_conv_w, m_gdn_A_log, m_gdn_dt_bias, m_gdn_norm_w, m_s5_A_re, m_s5_A_im, m_s5_log_step, m_s5_B_re, m_s5_B_im, m_s5_C_re, m_s5_C_im, m_s5_D, m_s5_w_glu, m_s5_b_glu, m_l1_w_out, m_l1_norm_mlp, m_l1_w_up, m_l1_w_down, m_final_norm, v_l0_norm_mix, v_l0_w_in, v_ssd_conv_w, v_ssd_conv_b, v_ssd_dt_bias, v_ssd_A_log, v_ssd_D, v_ssd_norm_w, v_l0_w_out, v_l0_norm_mlp, v_l0_w_up, v_l0_w_down, v_l1_norm_mix, v_l1_w_in, v_gdn_conv_w, v_gdn_A_log, v_gdn_dt_bias, v_gdn_norm_w, v_s5_A_re, v_s5_A_im, v_s5_log_step, v_s5_B_re, v_s5_B_im, v_s5_C_re, v_s5_C_im, v_s5_D, v_s5_w_glu, v_s5_b_glu, v_l1_w_out, v_l1_norm_mlp, v_l1_w_up, v_l1_w_down, v_final_norm):
    given = dict(locals())
    names = [n for n, _ in PARAMS]
    kinds = dict(PARAMS)
    w = {n: given[n] for n in names}
    seq = x.shape[1]
    x0 = x.reshape(seq, D_MODEL)
    target = loss_target.reshape(seq, D_MODEL)

    gb = _all_gather_shards("gather_weights", _pack([w[n] for n in GATHER_BF16], _MXU_DTYPE))
    full = dict(zip(GATHER_BF16, _unpack_gathered(gb, [w[n].shape for n in GATHER_BF16],
                                                  [kinds[n] for n in GATHER_BF16])))
    gf = _all_gather_shards("gather_conv", _pack([w[n] for n in GATHER_F32], F32))
    full.update(zip(GATHER_F32, _unpack_gathered(gf, [w[n].shape for n in GATHER_F32],
                                                 [kinds[n] for n in GATHER_F32])))
    in0 = full["l0_w_in"].shape[1]
    w_in0 = jnp.pad(full["l0_w_in"], ((0, 0), (0, IN0_PAD - in0)))
    wi1 = full["l1_w_in"]
    in1 = wi1.shape[1]
    w_in1 = jnp.concatenate([wi1[:, :3072], wi1[:, 3084:in1], wi1[:, 3072:3084],
                             jnp.zeros((D_MODEL, IN1_PAD - in1), wi1.dtype)], axis=1)

    row = lambda a: a.reshape(1, -1)
    lanes64 = lambda a: jnp.repeat(a, SSD_HEAD_DIM).reshape(1, -1)

    h0 = _rmsnorm_fwd("norm_mix0", x0, row(w["l0_norm_mix"]))
    proj0 = _matmul("in_proj0", h0, w_in0, "nn", out_dtype=_MXU_DTYPE)
    cos_t, sin_t = _rotary_tables(seq)
    ret_tabs = list(_retention_tables())
    ret_xs = [(proj0, 512, 0), (proj0, 512, 1), (proj0, 512, 2), (proj0, 512, 3)]
    ret_xt = [(cos_t, 128, 0), (sin_t, 128, 0)]
    ret_states = [(512, 128)]
    mixed0, ret_saved = _scan_fwd("ret_fwd", _f_ret, CHUNK, ret_tabs, [], ret_xs, ret_xt, ret_states, D_MODEL, 512, 0)
    expand = jnp.repeat(jnp.eye(128, SSD_HEADS, dtype=F32), SSD_HEAD_DIM, axis=1)
    ssd_consts = [full["ssd_conv_w"], row(w["ssd_conv_b"]), lanes64(w["ssd_dt_bias"]), lanes64(w["ssd_A_log"]),
                  lanes64(w["ssd_D"]), row(w["ssd_norm_w"])]
    ssd_xs = [(proj0, 512, 4), (proj0, 512, 5), (proj0, 256, 12), (proj0, 256, 13), (proj0, 128, 28)]
    ssd_states = [(8, 512), (8, 256), (8, 256), (512, 128)]
    mixed0, ssd_saved = _scan_fwd("ssd_fwd", _f_ssd, CHUNK, [expand], ssd_consts, ssd_xs, [], ssd_states,
                                  D_MODEL, 512, 1, y_alias=mixed0)
    x1, h1 = _matmul("out_proj0", mixed0, full["l0_w_out"], "nn", epi="add", epi_arr=x0, norm_w=row(w["l0_norm_mlp"]))
    u0 = _matmul("up0", h1, full["l0_w_up"], "nn", out_dtype=_MXU_DTYPE)
    x2, h2 = _matmul("down0", u0, full["l0_w_down"], "nn", a_pro="relu2", epi="add", epi_arr=x1,
                     norm_w=row(w["l1_norm_mix"]))

    proj1 = _matmul("in_proj1", h2, w_in1, "nn", out_dtype=_MXU_DTYPE)
    p_alog = jnp.zeros((1, 128), F32).at[0, 6:12].set(w["gdn_A_log"])
    p_dtb = jnp.zeros((1, 128), F32).at[0, 6:12].set(w["gdn_dt_bias"])
    gdn_consts = [full["gdn_conv_w"], p_alog, p_dtb, row(w["gdn_norm_w"])]
    gdn_xs = [(proj1, 768, 0), (proj1, 768, 1), (proj1, 768, 2), (proj1, 768, 3), (proj1, 128, 26)]
    gdn_states = [(8, 768), (8, 768), (8, 768), (768, 256)]
    mixed1, gdn_saved = _scan_fwd("gdn_fwd", _f_gdn, CHUNK, [], gdn_consts, gdn_xs, [], gdn_states, D_MODEL, 768, 0)
    s5_args = (w["s5_A_re"], w["s5_A_im"], w["s5_log_step"], w["s5_B_re"], w["s5_B_im"], w["s5_C_re"], w["s5_C_im"])
    (lam_re, lam_im, bblk, cc_re, cc_im), s5_prep_vjp = jax.vjp(_s5_prep, *s5_args)
    s5_consts = [lam_re, lam_im, bblk, cc_re, cc_im, row(w["s5_D"]), full["s5_w_glu"].astype(F32), row(w["s5_b_glu"])]
    s5_xs = [(proj1, 256, 12)]
    s5_states = [(8, 1024), (8, 1024)]
    mixed1, s5_saved = _scan_fwd("s5_fwd", _f_s5, CHUNK, [], s5_consts, s5_xs, [], s5_states, D_MODEL, 256, 3,
                                 y_alias=mixed1)
    x3, h3 = _matmul("out_proj1", mixed1, full["l1_w_out"], "nn", epi="add", epi_arr=x2, norm_w=row(w["l1_norm_mlp"]))
    u1 = _matmul("up1", h3, full["l1_w_up"], "nn", out_dtype=_MXU_DTYPE)
    x4 = _matmul("down1", u1, full["l1_w_down"], "nn", a_pro="relu2", epi="add", epi_arr=x3)

    loss_part, dx4, d_final = _loss_head("loss_head", x4, row(w["final_norm"]), target)
    loss = lax.psum(loss_part[0, 0], ("x", "y", "c"))
    grads = {"final_norm": d_final.reshape(-1)}
    small = SMALL_SHARDED + tuple(n for n in names if kinds[n] == "rep")
    order = LARGE + small
    first_row, slab_rows = {}, 0
    for n in order:
        first_row[n] = slab_rows
        slab_rows += _rows_of(math.prod(w[n].shape))
    slab_rows = -(-slab_rows // SLAB_ROW_ALIGN) * SLAB_ROW_ALIGN

    du1 = _matmul("down1_dx", dx4, full["l1_w_down"], "nt", out_dtype=_MXU_DTYPE, epi="drelu2", epi_arr=u1)
    gslab = _matmul("down1_dw", u1, dx4, "tn", a_pro="relu2", slab=(None, slab_rows, first_row["l1_w_down"], 1024))
    gslab = _matmul("up1_dw", h3, du1, "tn", slab=(gslab, slab_rows, first_row["l1_w_up"], None))
    dx3, dwn = _matmul("up1_dx", du1, full["l1_w_up"], "nt", epi="norm_bwd", epi_arr=dx4, norm_x=x3,
                       norm_w=row(w["l1_norm_mlp"]))
    grads["l1_norm_mlp"] = dwn.reshape(-1)
    gslab = _matmul("out_proj1_dw", mixed1, dx3, "tn", slab=(gslab, slab_rows, first_row["l1_w_out"], 256))
    dmixed1 = _matmul("out_proj1_dx", dx3, full["l1_w_out"], "nt")

    def gdn_assemble(dx):
        dq, dk, dv, dz, dba = dx
        zeros = lambda n: jnp.zeros((dq.shape[0], n), F32)
        return jnp.concatenate([dq, dk, dv, dz, zeros(256), dba, zeros(IN1_PAD - 3456)], axis=1)

    dproj1, gdn_dc = _scan_bwd("gdn_bwd", _f_gdn, CHUNK, [], gdn_consts, gdn_xs, [], gdn_saved, gdn_states,
                               (dmixed1, 768, 0), IN1_PAD, IN1_PAD, 0, gdn_assemble)
    dproj1, s5_dc = _scan_bwd("s5_bwd", _f_s5, CHUNK, [], s5_consts, s5_xs, [], s5_saved, s5_states,
                              (dmixed1, 256, 3), IN1_PAD, 256, 12, lambda dx: dx[0], dx_alias=dproj1)
    grads["gdn_conv_w"] = gdn_dc[0]
    grads["gdn_A_log"] = gdn_dc[1][0, 6:12]
    grads["gdn_dt_bias"] = gdn_dc[2][0, 6:12]
    grads["gdn_norm_w"] = gdn_dc[3].reshape(-1)
    s5_pg = s5_prep_vjp(tuple(s5_dc[:5]))
    for n, gval in zip(("s5_A_re", "s5_A_im", "s5_log_step", "s5_B_re", "s5_B_im", "s5_C_re", "s5_C_im"), s5_pg):
        grads[n] = gval
    grads["s5_D"] = s5_dc[5].reshape(-1)
    grads["s5_w_glu"] = s5_dc[6]
    grads["s5_b_glu"] = s5_dc[7].reshape(-1)
    dwi1 = _matmul("in_proj1_dw", h2, dproj1, "tn")
    grads["l1_w_in"] = jnp.concatenate([dwi1[:, :3072], dwi1[:, 3328:3340], dwi1[:, 3072:3328]], axis=1)
    dx2, dwn = _matmul("in_proj1_dx", dproj1, w_in1, "nt", epi="norm_bwd", epi_arr=dx3, norm_x=x2,
                       norm_w=row(w["l1_norm_mix"]))
    grads["l1_norm_mix"] = dwn.reshape(-1)

    du0 = _matmul("down0_dx", dx2, full["l0_w_down"], "nt", out_dtype=_MXU_DTYPE, epi="drelu2", epi_arr=u0)
    gslab = _matmul("down0_dw", u0, dx2, "tn", a_pro="relu2", slab=(gslab, slab_rows, first_row["l0_w_down"], 1024))
    gslab = _matmul("up0_dw", h1, du0, "tn", slab=(gslab, slab_rows, first_row["l0_w_up"], None))
    dx1, dwn = _matmul("up0_dx", du0, full["l0_w_up"], "nt", epi="norm_bwd", epi_arr=dx2, norm_x=x1,
                       norm_w=row(w["l0_norm_mlp"]))
    grads["l0_norm_mlp"] = dwn.reshape(-1)
    gslab = _matmul("out_proj0_dw", mixed0, dx1, "tn", slab=(gslab, slab_rows, first_row["l0_w_out"], 256))
    dmixed0 = _matmul("out_proj0_dx", dx1, full["l0_w_out"], "nt")
    dproj0, _ = _scan_bwd("ret_bwd", _f_ret, CHUNK, ret_tabs, [], ret_xs, ret_xt, ret_saved, ret_states,
                          (dmixed0, 512, 0), IN0_PAD, 2048, 0, lambda dx: jnp.concatenate(dx, axis=1))

    def ssd_assemble(dx):
        return jnp.concatenate(list(dx) + [jnp.zeros((dx[0].shape[0], 2048 - 1664), F32)], axis=1)

    dproj0, ssd_dc = _scan_bwd("ssd_bwd", _f_ssd, CHUNK, [expand], ssd_consts, ssd_xs, [], ssd_saved, ssd_states,
                               (dmixed0, 512, 1), IN0_PAD, 2048, 1, ssd_assemble, dx_alias=dproj0)
    heads = lambda a: a.reshape(SSD_HEADS, SSD_HEAD_DIM).sum(axis=1)
    grads["ssd_conv_w"] = ssd_dc[0]
    grads["ssd_conv_b"] = ssd_dc[1].reshape(-1)
    grads["ssd_dt_bias"] = heads(ssd_dc[2])
    grads["ssd_A_log"] = heads(ssd_dc[3])
    grads["ssd_D"] = heads(ssd_dc[4])
    grads["ssd_norm_w"] = ssd_dc[5].reshape(-1)
    grads["l0_w_in"] = _matmul("in_proj0_dw", h0, dproj0, "tn")[:, :in0]
    dx0, dwn = _matmul("in_proj0_dx", dproj0, w_in0, "nt", epi="norm_bwd", epi_arr=dx1, norm_x=x0,
                       norm_w=row(w["l0_norm_mix"]))
    grads["l0_norm_mix"] = dwn.reshape(-1)
    grad_x = dx0.reshape(x.shape)

    c_idx = lax.axis_index("c").astype(jnp.int32).reshape(1)
    tail = order[SLAB_DIRECT:]
    parts = [_sections(grads[n].reshape(_full_shape(n, w, kinds)), kinds[n], w[n].shape) for n in tail]
    parts.append(jnp.zeros((4, slab_rows - first_row[tail[0]] - sum(p.shape[1] for p in parts), LANES), F32))
    gslab = _place_rows("grads_place_tail", gslab, jnp.concatenate(parts, axis=1), first_row[tail[0]])
    from_sibling = _swap_halves("grads_swap_halves", gslab)
    chip_sum = _add_halves("grads_add_sibling", gslab, from_sibling, c_idx)
    place = jnp.stack([lax.axis_index("x"), lax.axis_index("y")]).astype(jnp.int32)
    from_x, from_y = _exchange_stage1("grads_stage1", chip_sum)
    send_a, keep_a, send_b, keep_b = _exchange_add1("grads_add1", chip_sum, from_x, from_y, place)
    got_a, got_b = _exchange_stage2("grads_stage2", send_a, send_b)
    my_half = _exchange_add2("grads_add2", keep_a, got_a, keep_b, got_b, c_idx)
    gsum = _join_halves("grads_join_halves", my_half)
    grad = dict(zip(order, _unpack(gsum, [w[n].shape for n in order])))

    delta, new_m, new_v = {}, {}, {}
    for n in LARGE:
        delta[n], new_m[n], new_v[n] = _adamw("adamw_" + n, w[n], grad[n], given["m_" + n], given["v_" + n])
    first_small = sum(_rows_of(math.prod(w[n].shape)) for n in LARGE)
    small_shapes = [w[n].shape for n in small]

    def small_slab(arrays):
        rows = _flat_rows(arrays, F32)
        return jnp.pad(rows, ((0, gsum.shape[0] - first_small - rows.shape[0]), (0, 0)))

    res = _adamw("adamw_small", small_slab([w[n] for n in small]), gsum[first_small:],
                 small_slab([given["m_" + n] for n in small]), small_slab([given["v_" + n] for n in small]))
    for out, slab in zip((delta, new_m, new_v), res):
        out.update(zip(small, _unpack(slab, small_shapes)))
    return (loss, grad_x, *[grad[n] for n in names], *[delta[n] for n in names], *[new_m[n] for n in names],
            *[new_v[n] for n in names])


def _full_shape(name, w, kinds):
    shp = w[name].shape
    if kinds[name] == "col":
        return (shp[0], 4 * shp[1])
    if kinds[name] == "row":
        return (4 * shp[0],) + tuple(shp[1:])
    return shp
```

```python
import functools
import math

import jax
import jax.numpy as jnp
from jax import lax
from jax.experimental import pallas as pl
from jax.experimental.pallas import tpu as pltpu

F32 = jnp.float32
_MXU_DTYPE = jnp.bfloat16

D_MODEL = 1024
CHUNK = 64
EPS = 1e-6
RET_HEADS, RET_DK = 4, 128
ROPE_THETA = 10000.0
SSD_HEADS, SSD_HEAD_DIM = 8, 64
GDN_HEADS, GDN_DK = 6, 128
S5_GROUPS, S5_GROUP, S5_STATE = 16, 16, 64
ADAM_LR, ADAM_B1, ADAM_B2, ADAM_EPS, ADAM_WD, ADAM_STEP = 0.001, 0.9, 0.999, 1e-08, 0.01, 10

IN0_PAD = 4096
IN1_PAD = 3584
LANES = 1024
VMEM_LIMIT = 56 * 1024 * 1024
MESH = pl.DeviceIdType.MESH

PARAMS = (
    ("l0_norm_mix", "rep"), ("l0_w_in", "col"), ("ssd_conv_w", "col"), ("ssd_conv_b", "rep"),
    ("ssd_dt_bias", "rep"), ("ssd_A_log", "rep"), ("ssd_D", "rep"), ("ssd_norm_w", "rep"),
    ("l0_w_out", "row"), ("l0_norm_mlp", "rep"), ("l0_w_up", "col"), ("l0_w_down", "row"),
    ("l1_norm_mix", "rep"), ("l1_w_in", "col"), ("gdn_conv_w", "col"), ("gdn_A_log", "rep"),
    ("gdn_dt_bias", "rep"), ("gdn_norm_w", "rep"), ("s5_A_re", "rep"), ("s5_A_im", "rep"),
    ("s5_log_step", "rep"), ("s5_B_re", "rep"), ("s5_B_im", "rep"), ("s5_C_re", "rep"), ("s5_C_im", "rep"),
    ("s5_D", "rep"), ("s5_w_glu", "row"), ("s5_b_glu", "rep"), ("l1_w_out", "row"), ("l1_norm_mlp", "rep"),
    ("l1_w_up", "col"), ("l1_w_down", "row"), ("final_norm", "rep"),
)
GATHER_BF16 = ("l0_w_in", "l0_w_out", "l0_w_up", "l0_w_down", "l1_w_in", "l1_w_out", "l1_w_up", "l1_w_down", "s5_w_glu")
GATHER_F32 = ("ssd_conv_w", "gdn_conv_w")
LARGE = ("l0_w_up", "l0_w_down", "l1_w_up", "l1_w_down", "l0_w_out", "l1_w_out", "l0_w_in", "l1_w_in")
SLAB_DIRECT = 4
SMALL_SHARDED = ("s5_w_glu", "ssd_conv_w", "gdn_conv_w")


def _dg(a, b, ca, cb, prec=None):
    return lax.dot_general(a, b, (((ca,), (cb,)), ((), ())), preferred_element_type=F32, precision=prec)


def _lo(a):
    return a.astype(_MXU_DTYPE)


@jax.custom_vjp
def _mm(a, b):
    return _dg(_lo(a), _lo(b), 1, 0)


def _mm_fwd(a, b):
    return _mm(a, b), (a, b)


def _mm_bwd(res, g):
    a, b = res
    return _dg(_lo(g), _lo(b), 1, 1), _dg(_lo(a), _lo(g), 0, 0)


_mm.defvjp(_mm_fwd, _mm_bwd)


@jax.custom_vjp
def _mm_nt(a, b):
    return _dg(_lo(a), _lo(b), 1, 1)


def _mm_nt_fwd(a, b):
    return _mm_nt(a, b), (a, b)


def _mm_nt_bwd(res, g):
    a, b = res
    return _dg(_lo(g), _lo(b), 1, 0), _dg(_lo(g), _lo(a), 0, 0)


_mm_nt.defvjp(_mm_nt_fwd, _mm_nt_bwd)


@jax.custom_vjp
def _mm_tn(a, b):
    return _dg(_lo(a), _lo(b), 0, 0)


def _mm_tn_fwd(a, b):
    return _mm_tn(a, b), (a, b)


def _mm_tn_bwd(res, g):
    a, b = res
    return _dg(_lo(b), _lo(g), 1, 1), _dg(_lo(a), _lo(g), 1, 0)


_mm_tn.defvjp(_mm_tn_fwd, _mm_tn_bwd)


def _split2(x):
    hi = _lo(x)
    return hi, _lo(x - hi.astype(F32))


def _split3(x):
    h1 = _lo(x)
    r1 = x - h1.astype(F32)
    h2 = _lo(r1)
    return h1, h2, _lo(r1 - h2.astype(F32))


def _tri_cum_dir(m, ca):
    n, w = m.shape
    causal, _ = _tri_masks(n)
    out = _dg(causal.astype(_MXU_DTYPE), jnp.concatenate(_split3(m), axis=1), ca, 0)
    return out[:, :w] + out[:, w:2 * w] + out[:, 2 * w:]


@jax.custom_vjp
def _tri_cum(m):
    return _tri_cum_dir(m, 1)


def _tri_cum_fwd(m):
    return _tri_cum_dir(m, 1), None


def _tri_cum_bwd(_, g):
    return (_tri_cum_dir(g, 0),)


_tri_cum.defvjp(_tri_cum_fwd, _tri_cum_bwd)


@jax.custom_vjp
def _mm_exact_rhs(a, e):
    return _dg(jnp.concatenate(_split3(a), axis=1), jnp.concatenate([_lo(e)] * 3, axis=0), 1, 0)


def _mm_exact_rhs_fwd(a, e):
    return _mm_exact_rhs(a, e), e


def _mm_exact_rhs_bwd(e, g):
    return _dg(jnp.concatenate(_split3(g), axis=1), jnp.concatenate([_lo(e)] * 3, axis=1), 1, 1), jnp.zeros_like(e)


_mm_exact_rhs.defvjp(_mm_exact_rhs_fwd, _mm_exact_rhs_bwd)


def _bd(x):
    left = _iota(x.shape, 1) < (x.shape[1] // 2)
    zero = jnp.zeros_like(x)
    return jnp.concatenate([jnp.where(left, x, zero), jnp.where(left, zero, x)], axis=0)


def _unbd(m):
    half = m.shape[0] // 2
    left = _iota((half, m.shape[1]), 1) < (m.shape[1] // 2)
    return jnp.where(left, m[:half], m[half:])


def _pmm_nn(x, y):
    xh, xl = _split2(x)
    yh, yl = _split2(y)
    return _dg(jnp.concatenate([xh, xl, xh], axis=1), jnp.concatenate([_bd(yh), _bd(yh), _bd(yl)], axis=0), 1, 0)


def _pmm_nt(x, y):
    xh, xl = _split2(x)
    yh, yl = _split2(y)
    return _dg(jnp.concatenate([xh, xl, xh], axis=1), jnp.concatenate([_bd(yh), _bd(yh), _bd(yl)], axis=1), 1, 1)


def _pmm_tn(x, y):
    xh, xl = _split2(x)
    yh, yl = _split2(y)
    return _unbd(_dg(jnp.concatenate([xh, xl, xh], axis=0), jnp.concatenate([yh, yh, yl], axis=0), 0, 0))


@functools.lru_cache(maxsize=None)
def _shift(s, axis):
    @jax.custom_vjp
    def sh(x):
        return pltpu.roll(x, s, axis)

    def fwd(x):
        return sh(x), None

    def bwd(_, g):
        n = g.shape[axis]
        return (pltpu.roll(g, (n - s) % n, axis),)

    sh.defvjp(fwd, bwd)
    return sh


def _iota(shape, axis):
    return lax.broadcasted_iota(jnp.int32, shape, axis)


def _silu(x):
    return x * jax.nn.sigmoid(x)


def _unit_rms(x):
    return x * lax.rsqrt(jnp.mean(x * x, axis=-1, keepdims=True) + EPS)


def _l2norm(x):
    return x * lax.rsqrt(jnp.sum(x * x, axis=-1, keepdims=True) + EPS)


def _tri_masks(n):
    r, c = _iota((n, n), 0), _iota((n, n), 1)
    return r >= c, r > c


def _packed_rc():
    return _iota((CHUNK, 2 * CHUNK), 0), _iota((CHUNK, 2 * CHUNK), 1) & (CHUNK - 1)


def _decay_packed(g_packed):
    r, c = _packed_rc()
    seg = _tri_cum(g_packed * (r > c).astype(F32))
    return jnp.where(r >= c, jnp.exp(jnp.where(r >= c, seg, 0.0)), 0.0)


def _conv(x, tail, w):
    rows, width = x.shape
    row = _iota((rows, width), 0)
    acc = x * w[3:4, :]
    pad = jnp.zeros((rows - 8, width), F32)
    for j in range(3):
        s = 3 - j
        prev = jnp.concatenate([_shift(s, 0)(tail), pad], axis=0)
        acc = acc + w[j:j + 1, :] * jnp.where(row < s, prev, _shift(s, 0)(x))
    return acc


def _tri_inv_impl(mats):
    r, c = _packed_rc()
    eye = (r == c).astype(F32)

    def same_block(b):
        return (r // b) == (c // b)

    a8 = [jnp.where(same_block(8), a, 0.0) for a in mats]
    a2 = [_pmm_nn(t, t) for t in a8]
    a4 = [_pmm_nn(t, t) for t in a2]
    x = [_pmm_nn(eye - p, eye + q) for p, q in zip(a8, a2)]
    x = [_pmm_nn(p, eye + q) for p, q in zip(x, a4)]
    for b in (8, 16, 32):
        off = [jnp.where(same_block(2 * b) & jnp.logical_not(same_block(b)), a, 0.0) for a in mats]
        y = [_pmm_nn(p, q) for p, q in zip(x, off)]
        x = [p - _pmm_nn(q, p) for p, q in zip(x, y)]
    return x


@jax.custom_vjp
def _tri_inv(mats):
    return _tri_inv_impl(mats)


def _tri_inv_fwd(mats):
    t = _tri_inv_impl(mats)
    return t, t


def _tri_inv_bwd(t, g):
    m1 = [_pmm_tn(p, q) for p, q in zip(t, g)]
    return ([-_pmm_nt(p, q) for p, q in zip(m1, t)],)


_tri_inv.defvjp(_tri_inv_fwd, _tri_inv_bwd)


def _f_ret(tabs, consts, xs, xtabs, states):
    dmask, kdec, qdec, cdec = tabs
    q, k, v, gate = xs
    cs, sn = xtabs
    (st,) = states
    swap = _shift(RET_DK // 2, 1)
    heads = range(RET_HEADS)
    sls = [slice(128 * h, 128 * h + 128) for h in heads]
    qh = [(q[:, sl] * cs + swap(q[:, sl]) * sn) * (RET_DK ** -0.5) for sl in sls]
    kh = [k[:, sl] * cs + swap(k[:, sl]) * sn for sl in sls]
    sh = [st[sl, :] for sl in sls]
    scores = [_mm_nt(a, b) * dmask[64 * h:64 * h + 64, :] for h, a, b in zip(heads, qh, kh)]
    y = [_mm(s, v[:, sl]) for s, sl in zip(scores, sls)]
    y = [t + _mm(a * qdec[:, sl], s) for t, a, sl, s in zip(y, qh, sls, sh)]
    new = [s * cdec[:, sl] + _mm_tn(b * kdec[:, sl], v[:, sl]) for s, sl, b in zip(sh, sls, kh)]
    outs = [_silu(gate[:, sl]) * _unit_rms(t) for sl, t in zip(sls, y)]
    return (jnp.concatenate(outs, axis=1),), [jnp.concatenate(new, axis=0)]


def _f_ssd(tabs, consts, xs, xtabs, states):
    (expand,) = tabs
    conv_w, conv_b, dtb, alog, dskip, nw = consts
    z, xr, br, cr, dtr = xs
    tx, tb, tc, st = states
    xc = _silu(_conv(xr, tx, conv_w[:, 0:512]) + conv_b[:, 0:512])
    bc = _silu(_conv(br, tb, conv_w[:, 512:768]) + conv_b[:, 512:768])
    cc = _silu(_conv(cr, tc, conv_w[:, 768:1024]) + conv_b[:, 768:1024])
    dt = jax.nn.softplus(_mm_exact_rhs(dtr, expand) + dtb)
    la = dt * (-jnp.exp(alog))
    lacum = _tri_cum(la)
    total = jnp.sum(la, axis=0, keepdims=True)
    xd = xc * dt
    dte, ecum, cdec = jnp.exp(total - lacum), jnp.exp(lacum), jnp.exp(total)
    pairs = range(SSD_HEADS // 2)
    sls = [slice(128 * p, 128 * p + 128) for p in pairs]
    bg = [bc[:, 128 * g:128 * g + 128] for g in range(2)]
    cg = [cc[:, 128 * g:128 * g + 128] for g in range(2)]
    cb2 = [_mm_nt(c, jnp.concatenate([b, b], axis=0)) for b, c in zip(bg, cg)]
    lm = [_decay_packed(la[:, sl]) for sl in sls]
    sp = [st[sl, :] for sl in sls]
    ys = [_mm(cg[p // 2], sp[p]) * ecum[:, sls[p]] for p in pairs]
    ys = [ys[p] + _mm(cb2[p // 2] * lm[p], _bd(xd[:, sls[p]])) for p in pairs]
    new = [sp[p] * cdec[:, sls[p]] + _mm_tn(bg[p // 2], xd[:, sls[p]] * dte[:, sls[p]]) for p in pairs]
    y = jnp.concatenate(ys, axis=1) + dskip * xc
    yg = y * _silu(z)
    out = jnp.concatenate([_unit_rms(yg[:, 0:256]), _unit_rms(yg[:, 256:512])], axis=1) * nw
    return (out,), [xr[CHUNK - 8:, :], br[CHUNK - 8:, :], cr[CHUNK - 8:, :], jnp.concatenate(new, axis=0)]


def _f_gdn(tabs, consts, xs, xtabs, states):
    conv_w, p_alog, p_dtb, nw = consts
    qr, kr, vr, z, ba = xs
    tq, tk, tv, st = states
    qc = _silu(_conv(qr, tq, conv_w[:, 0:768]))
    kc = _silu(_conv(kr, tk, conv_w[:, 768:1536]))
    vc = _silu(_conv(vr, tv, conv_w[:, 1536:2304]))
    gl = -jnp.exp(p_alog) * jax.nn.softplus(ba + p_dtb)
    bl = jax.nn.sigmoid(ba)
    gcum = _tri_cum(gl)
    left128 = _iota((CHUNK, 128), 1) < 64
    left256 = _iota((CHUNK, 256), 1) < 128
    r, c = _packed_rc()
    diag_blocks = (_iota((256, 256), 0) < 128) == (_iota((256, 256), 1) < 128)

    def norm2(t):
        return jnp.concatenate([_l2norm(t[:, 0:128]), _l2norm(t[:, 128:256])], axis=1)

    def pick(arr, off, left, p):
        return jnp.where(left, arr[:, off + 2 * p:off + 2 * p + 1], arr[:, off + 2 * p + 1:off + 2 * p + 2])

    pairs = range(GDN_HEADS // 2)
    sls = [slice(256 * p, 256 * p + 256) for p in pairs]
    qn = [norm2(qc[:, sl]) * (GDN_DK ** -0.5) for sl in sls]
    kn = [norm2(kc[:, sl]) for sl in sls]
    dec = [_decay_packed(pick(gl, 6, left128, p)) for p in pairs]
    g2 = [pick(gl, 6, left256, p) for p in pairs]
    gc2 = [pick(gcum, 6, left256, p) for p in pairs]
    b2 = [pick(bl, 0, left256, p) for p in pairs]
    tot = [jnp.sum(t, axis=0, keepdims=True) for t in g2]
    eg = [jnp.exp(t) for t in gc2]
    et = [jnp.exp(t - s) for t, s in zip(tot, gc2)]
    cd = [jnp.exp(t) for t in tot]
    kb = [k * b for k, b in zip(kn, b2)]
    vb = [vc[:, sl] * b for sl, b in zip(sls, b2)]
    kbd = [_bd(k) for k in kn]
    tm = _tri_inv([jnp.where(r > c, _mm_nt(a, b) * d, 0.0) for a, b, d in zip(kb, kbd, dec)])
    u = [_mm(t, _bd(v)) for t, v in zip(tm, vb)]
    w = [_mm(t, _bd(k * e)) for t, k, e in zip(tm, kb, eg)]
    attn = [_mm_nt(q, k) * d for q, k, d in zip(qn, kbd, dec)]
    sp = [st[sl, :] for sl in sls]
    vn = [a - _mm(b, s) for a, b, s in zip(u, w, sp)]
    o = [_mm(q * e, s) + _mm(a, _bd(v)) for q, e, s, a, v in zip(qn, eg, sp, attn, vn)]
    new = [s * d + jnp.where(diag_blocks, _mm_tn(k * e, v), 0.0) for s, d, k, e, v in zip(sp, cd, kn, et, vn)]
    outs = []
    for p in pairs:
        for hh in range(2):
            osl = slice(128 * hh, 128 * hh + 128)
            zsl = slice(256 * p + 128 * hh, 256 * p + 128 * hh + 128)
            outs.append(_unit_rms(o[p][:, osl]) * nw * _silu(z[:, zsl]))
    return (jnp.concatenate(outs, axis=1),), [qr[CHUNK - 8:, :], kr[CHUNK - 8:, :], vr[CHUNK - 8:, :],
                                             jnp.concatenate(new, axis=0)]


def _f_s5(tabs, consts, xs, xtabs, states):
    lam_re, lam_im, bblk, c_re, c_im, dskip, wglu, bglu = consts
    (u,) = xs
    s_re, s_im = states
    rows = u.shape[0]
    n = lam_re.shape[1]
    bu = _mm(u, bblk)
    hr, hi = bu[:, 0:n], bu[:, n:2 * n]
    row = _iota((rows, n), 0)
    h0r, h0i = s_re[0:1, :], s_im[0:1, :]
    hr = hr + jnp.where(row == 0, lam_re * h0r - lam_im * h0i, 0.0)
    hi = hi + jnp.where(row == 0, lam_re * h0i + lam_im * h0r, 0.0)
    pr, pi = lam_re, lam_im
    d = 1
    while d < rows:
        sr = jnp.where(row >= d, _shift(d, 0)(hr), 0.0)
        si = jnp.where(row >= d, _shift(d, 0)(hi), 0.0)
        hr, hi = hr + pr * sr - pi * si, hi + pr * si + pi * sr
        pr, pi = pr * pr - pi * pi, 2.0 * pr * pi
        d *= 2
    y = _mm(hr, c_re) - _mm(hi, c_im) + dskip * u
    y = jax.nn.gelu(y)
    out = y * jax.nn.sigmoid(_mm(y, wglu) + bglu)
    last_r = jnp.broadcast_to(hr[rows - 1:rows, :], (8, n))
    last_i = jnp.broadcast_to(hi[rows - 1:rows, :], (8, n))
    return (out,), [last_r, last_i]


def _full_spec(a):
    nd = a.ndim
    return pl.BlockSpec(a.shape, lambda i, _nd=nd: (0,) * _nd)


CHUNKS_PER_STEP = 4


def _chunks_per_step(f, rows, n):
    def g(tabs, consts, xs, xtabs, states):
        ys = []
        for i in range(n):
            sl = slice(rows * i, rows * (i + 1))
            (y,), states = f(tabs, consts, [t[sl] for t in xs], [t[sl] for t in xtabs], states)
            ys.append(y)
        return (jnp.concatenate(ys, axis=0),), states

    return g


def _scan_fwd(name, f, rows, tabs, consts, xs, xtabs, state_shapes, y_total, y_width, y_cb, y_alias=None):
    seq = xs[0][0].shape[0]
    per_step = math.gcd(CHUNKS_PER_STEP, seq // rows)
    f = _chunks_per_step(f, rows, per_step)
    rows = rows * per_step
    nc = seq // rows
    nt, ncst, nx, nxt, ns = len(tabs), len(consts), len(xs), len(xtabs), len(state_shapes)
    alias = y_alias is not None

    def body(*refs):
        p = 0
        tab_r = refs[p:p + nt]; p += nt
        c_r = refs[p:p + ncst]; p += ncst
        x_r = refs[p:p + nx]; p += nx
        xt_r = refs[p:p + nxt]; p += nxt
        if alias:
            p += 1
        y_ref = refs[p]; p += 1
        sv_r = refs[p:p + ns]; p += ns
        st_r = refs[p:p + ns]

        @pl.when(pl.program_id(0) == 0)
        def _():
            for s in st_r:
                s[...] = jnp.zeros(s.shape, F32)

        st = [s[...] for s in st_r]
        for r, v in zip(sv_r, st):
            r[...] = v
        (y,), new = f([r[...] for r in tab_r], [r[...] for r in c_r], [r[...].astype(F32) for r in x_r],
                      [r[...] for r in xt_r], st)
        y_ref[...] = y.astype(y_ref.dtype)
        for s, v in zip(st_r, new):
            s[...] = v

    win = [pl.BlockSpec((rows, w), lambda i, _cb=cb: (i, _cb)) for (_, w, cb) in list(xs) + list(xtabs)]
    in_specs = [_full_spec(a) for a in list(tabs) + list(consts)] + win
    args = list(tabs) + list(consts) + [a for (a, _, _) in list(xs) + list(xtabs)]
    io_alias = {}
    if alias:
        in_specs.append(pl.BlockSpec(memory_space=pl.ANY))
        io_alias = {len(args): 0}
        args.append(y_alias)
    out_shape = [jax.ShapeDtypeStruct((seq, y_total), _MXU_DTYPE)]
    out_specs = [pl.BlockSpec((rows, y_width), lambda i: (i, y_cb))]
    for (r, c) in state_shapes:
        out_shape.append(jax.ShapeDtypeStruct((nc * r, c), F32))
        out_specs.append(pl.BlockSpec((r, c), lambda i: (i, 0)))
    res = pl.pallas_call(
        body, name=name, grid=(nc,), in_specs=in_specs, out_specs=out_specs, out_shape=out_shape,
        scratch_shapes=[pltpu.VMEM(s, F32) for s in state_shapes], input_output_aliases=io_alias,
        compiler_params=pltpu.CompilerParams(dimension_semantics=("arbitrary",), vmem_limit_bytes=VMEM_LIMIT),
    )(*args)
    return res[0], list(res[1:])


def _scan_bwd(name, f, rows, tabs, consts, xs, xtabs, saved, state_shapes, dy, dx_total, dx_width, dx_cb,
              assemble, dx_alias=None):
    seq = xs[0][0].shape[0]
    per_step = math.gcd(CHUNKS_PER_STEP, seq // rows)
    f = _chunks_per_step(f, rows, per_step)
    rows = rows * per_step
    nc = seq // rows
    nt, ncst, nx, nxt, ns = len(tabs), len(consts), len(xs), len(xtabs), len(state_shapes)
    alias = dx_alias is not None

    def body(*refs):
        p = 0
        tab_r = refs[p:p + nt]; p += nt
        c_r = refs[p:p + ncst]; p += ncst
        x_r = refs[p:p + nx]; p += nx
        xt_r = refs[p:p + nxt]; p += nxt
        sv_r = refs[p:p + ns]; p += ns
        dy_ref = refs[p]; p += 1
        if alias:
            p += 1
        dx_ref = refs[p]; p += 1
        dc_r = refs[p:p + ncst]; p += ncst
        ds_r = refs[p:p + ns]

        @pl.when(pl.program_id(0) == 0)
        def _():
            for s in ds_r:
                s[...] = jnp.zeros(s.shape, F32)
            for r in dc_r:
                r[...] = jnp.zeros(r.shape, F32)

        tab_v = [r[...] for r in tab_r]
        xt_v = [r[...] for r in xt_r]

        def g(c, x, s):
            (y,), new = f(tab_v, c, x, xt_v, s)
            return y, new

        _, vjp = jax.vjp(g, [r[...] for r in c_r], [r[...].astype(F32) for r in x_r], [r[...] for r in sv_r])
        dc, dx, ds = vjp((dy_ref[...], [s[...] for s in ds_r]))
        dx_ref[...] = assemble(dx).astype(dx_ref.dtype)
        for r, v in zip(dc_r, dc):
            r[...] += v
        for s, v in zip(ds_r, ds):
            s[...] = v

    win = [pl.BlockSpec((rows, w), lambda j, _cb=cb: (nc - 1 - j, _cb)) for (_, w, cb) in list(xs) + list(xtabs)]
    in_specs = [_full_spec(a) for a in list(tabs) + list(consts)] + win
    args = list(tabs) + list(consts) + [a for (a, _, _) in list(xs) + list(xtabs)]
    for (r, c), sv in zip(state_shapes, saved):
        in_specs.append(pl.BlockSpec((r, c), lambda j: (nc - 1 - j, 0)))
        args.append(sv)
    in_specs.append(pl.BlockSpec((rows, dy[1]), lambda j: (nc - 1 - j, dy[2])))
    args.append(dy[0])
    io_alias = {}
    if alias:
        in_specs.append(pl.BlockSpec(memory_space=pl.ANY))
        io_alias = {len(args): 0}
        args.append(dx_alias)
    out_shape = [jax.ShapeDtypeStruct((seq, dx_total), _MXU_DTYPE)] +[jax.ShapeDtypeStruct(a.shape, F32) for a in consts]
    out_specs = [pl.BlockSpec((rows, dx_width), lambda j: (nc - 1 - j, dx_cb))] + [_full_spec(a) for a in consts]
    res = pl.pallas_call(
        body, name=name, grid=(nc,), in_specs=in_specs, out_specs=out_specs, out_shape=out_shape,
        scratch_shapes=[pltpu.VMEM(s, F32) for s in state_shapes], input_output_aliases=io_alias,
        compiler_params=pltpu.CompilerParams(dimension_semantics=("arbitrary",), vmem_limit_bytes=VMEM_LIMIT),
    )(*args)
    return res[0], list(res[1:])


def _tile(n, want):
    t = min(n, want)
    while n % t:
        t //= 2
    return t


MATMUL_VMEM_BUDGET = 40 * 1024 * 1024


MXU_FLOPS_PER_S = 8.5e14
HBM_BYTES_PER_S = 2.8e12
GRID_STEP_S = 0.35e-6


def _pick_tiles(m, n, k, sa, sb, so, se, whole_rows=False, reduce_rows=False, tn_fixed=None, tm_divides=None):
    best = None
    tns = {tn_fixed} if tn_fixed else ({n} if whole_rows else {_tile(n, t) for t in (4096, 2048, 1024, 512)})
    tms = {_tile(m, t) for t in (2048, 1024, 512)}
    if tm_divides:
        tms = {t for t in (1024, 512, 256) if tm_divides % t == 0 and m % t == 0}
    for tn in tns:
        for tm in tms:
            for tk in {_tile(k, t) for t in (4096, 2048, 1024, 512)}:
                at, bt, ot = tm * tk * sa, tk * tn * sb, tm * tn * so
                need = 2 * (at + bt + ot + tm * tn * se) + 2 * tm * tn * 4 + (at if sa == 4 else 0) + (bt if sb == 4 else 0)
                if need > MATMUL_VMEM_BUDGET:
                    continue
                ni, nj, nk = m // tm, n // tn, k // tk
                b_reads = ni if (reduce_rows or nk > 1) else 1
                moved = m * k * sa * nj + k * n * sb * b_reads + m * n * (so + se)
                cost = max(2 * m * n * k / MXU_FLOPS_PER_S, moved / HBM_BYTES_PER_S) + ni * nj * nk * GRID_STEP_S
                key = (cost, nk, -tm)
                if best is None or key < best[0]:
                    best = (key, (tm, tn, tk))
    assert best is not None, (m, n, k)
    return best[1]


def _matmul(name, a, b, mode, out_dtype=F32, a_pro=None, epi=None, epi_arr=None, norm_w=None, norm_x=None, slab=None):
    if mode == "nn":
        (m, k), (k2, n) = a.shape, b.shape
    elif mode == "nt":
        (m, k), (n, k2) = a.shape, b.shape
    else:
        (k, m), (k2, n) = a.shape, b.shape
    assert k == k2, (name, a.shape, b.shape)
    size = lambda t: jnp.dtype(t).itemsize
    rows_in = [] if epi is None else [epi_arr] + ([norm_x] if epi == "norm_bwd" else [])
    emit_norm = epi == "add" and norm_w is not None
    extra = sum(size(t.dtype) for t in rows_in) + (size(_MXU_DTYPE) if emit_norm else 0)
    if slab is None:
        tm, tn, tk = _pick_tiles(m, n, k, size(a.dtype), size(b.dtype), size(out_dtype), extra,
                                 whole_rows=norm_w is not None, reduce_rows=mode == "tn")
    else:
        prev_slab, slab_rows, first_row, shard_rows = slab
        assert mode == "tn" and epi is None and n % LANES == 0, name
        tm, tn, tk = _pick_tiles(m, n, k, size(a.dtype), size(b.dtype), size(out_dtype), extra, reduce_rows=True,
                                 tn_fixed=LANES, tm_divides=math.gcd(shard_rows or m, first_row or m))
    nk = k // tk
    ca, cb = {"nn": (1, 0), "nt": (1, 1), "tn": (0, 0)}[mode]
    n_in = 2 + len(rows_in) + (norm_w is not None) + (slab is not None and slab[0] is not None)
    n_out = 2 if (emit_norm or epi == "norm_bwd") else 1

    def body(*refs):
        refs = list(refs)
        acc = refs.pop() if nk > 1 else None
        a_ref, b_ref = refs[0], refs[1]
        e_ref = refs[2] if epi is not None else None
        x_ref = refs[3] if epi == "norm_bwd" else None
        w_ref = refs[n_in - 1] if norm_w is not None else None
        o_ref = refs[n_in]
        o2_ref = refs[n_in + 1] if n_out == 2 else None
        kk = pl.program_id(2)

        if epi == "norm_bwd":
            @pl.when((pl.program_id(1) == 0) & (kk == 0))
            def _():
                o2_ref[...] = jnp.zeros(o2_ref.shape, F32)

        av = a_ref[...]
        if a_pro == "relu2":
            r = jnp.maximum(av, 0.0)
            av = r * r
        part = _dg(_lo(av), _lo(b_ref[...]), ca, cb)

        def finish(r):
            if epi == "add":
                r = r + e_ref[...]
                if emit_norm:
                    o2_ref[...] = (_unit_rms(r) * w_ref[...]).astype(_MXU_DTYPE)
            elif epi == "drelu2":
                r = r * (2.0 * jnp.maximum(e_ref[...], 0.0))
            elif epi == "norm_bwd":
                xv = x_ref[...]
                rstd = lax.rsqrt(jnp.mean(xv * xv, axis=-1, keepdims=True) + EPS)
                xh = xv * rstd
                g = r * w_ref[...]
                o2_ref[...] += jnp.sum(r * xh, axis=0, keepdims=True)
                r = e_ref[...] + rstd * (g - xh * jnp.mean(g * xh, axis=-1, keepdims=True))
            o_ref[...] = r.astype(out_dtype).reshape(o_ref.shape)

        if nk == 1:
            finish(part)
        else:
            @pl.when(kk == 0)
            def _():
                acc[...] = part

            @pl.when(kk > 0)
            def _():
                acc[...] += part

            @pl.when(kk == nk - 1)
            def _():
                finish(acc[...])

    if mode == "tn":
        a_spec = pl.BlockSpec((tk, tm), lambda j, i, kk: (kk, i))
    else:
        a_spec = pl.BlockSpec((tm, tk), lambda j, i, kk: (i, kk))
    if mode == "nt":
        b_spec = pl.BlockSpec((tn, tk), lambda j, i, kk: (j, kk))
    else:
        b_spec = pl.BlockSpec((tk, tn), lambda j, i, kk: (kk, j))
    o_spec = pl.BlockSpec((tm, tn), lambda j, i, kk: (i, j))
    vec_spec = pl.BlockSpec((1, tn), lambda j, i, kk: (0, j))
    in_specs, args = [a_spec, b_spec] + [o_spec] * len(rows_in), [a, b] + rows_in
    if norm_w is not None:
        in_specs.append(vec_spec)
        args.append(norm_w)
    out_specs, out_shape = [o_spec], [jax.ShapeDtypeStruct((m, n), out_dtype)]
    io_alias = {}
    if slab is not None:
        first_blk = first_row // tm
        if shard_rows is None:
            out_specs = [pl.BlockSpec((1, tm, tn), lambda j, i, kk: (j, first_blk + i, 0))]
        else:
            per = shard_rows // tm
            out_specs = [pl.BlockSpec((1, tm, tn), lambda j, i, kk: (i // per, first_blk + i % per, 0))]
        out_shape = [jax.ShapeDtypeStruct((4, slab_rows, LANES), out_dtype)]
        if prev_slab is not None:
            in_specs.append(pl.BlockSpec(memory_space=pl.ANY))
            io_alias = {len(args): 0}
            args.append(prev_slab)
    if emit_norm:
        out_specs.append(o_spec)
        out_shape.append(jax.ShapeDtypeStruct((m, n), _MXU_DTYPE))
    elif epi == "norm_bwd":
        out_specs.append(vec_spec)
        out_shape.append(jax.ShapeDtypeStruct((1, n), F32))
    sem = ("parallel", "arbitrary" if epi == "norm_bwd" else "parallel", "arbitrary")
    res = pl.pallas_call(
        body, name=name, grid=(n // tn, m // tm, nk), in_specs=in_specs, out_specs=out_specs, out_shape=out_shape,
        scratch_shapes=[pltpu.VMEM((tm, tn), F32)] if nk > 1 else [], input_output_aliases=io_alias,
        compiler_params=pltpu.CompilerParams(dimension_semantics=sem, vmem_limit_bytes=VMEM_LIMIT),
    )(*args)
    return res[0] if n_out == 1 else res


ROW_TILE = 512


def _rmsnorm_fwd(name, x, w):
    seq, d = x.shape
    tr = _tile(seq, ROW_TILE)

    def body(x_ref, w_ref, o_ref):
        xv = x_ref[...]
        o_ref[...] = (_unit_rms(xv) * w_ref[...]).astype(_MXU_DTYPE)

    return pl.pallas_call(
        body, name=name, grid=(seq // tr,),
        in_specs=[pl.BlockSpec((tr, d), lambda i: (i, 0)), pl.BlockSpec((1, d), lambda i: (0, 0))],
        out_specs=pl.BlockSpec((tr, d), lambda i: (i, 0)), out_shape=jax.ShapeDtypeStruct((seq, d), _MXU_DTYPE),
        compiler_params=pltpu.CompilerParams(dimension_semantics=("parallel",), vmem_limit_bytes=VMEM_LIMIT),
    )(x, w)


def _loss_head(name, x, w, target):
    seq, d = x.shape
    tr = _tile(seq, ROW_TILE)

    def body(x_ref, w_ref, t_ref, loss_ref, dx_ref, dw_ref):
        @pl.when(pl.program_id(0) == 0)
        def _():
            dw_ref[...] = jnp.zeros(dw_ref.shape, F32)
            loss_ref[...] = jnp.zeros(loss_ref.shape, F32)

        xv = x_ref[...]
        rstd = lax.rsqrt(jnp.mean(xv * xv, axis=-1, keepdims=True) + EPS)
        xh = xv * rstd
        err = xh * w_ref[...] - t_ref[...]
        per_row = jnp.mean(err * err, axis=-1, keepdims=True)
        loss_ref[...] += 0.5 * jnp.sum(per_row, axis=0, keepdims=True)
        dy = err * (1.0 / d)
        g = dy * w_ref[...]
        dx_ref[...] = rstd * (g - xh * jnp.mean(g * xh, axis=-1, keepdims=True))
        dw_ref[...] += jnp.sum(dy * xh, axis=0, keepdims=True)

    row = pl.BlockSpec((tr, d), lambda i: (i, 0))
    vec = pl.BlockSpec((1, d), lambda i: (0, 0))
    one = pl.BlockSpec((1, 1), lambda i: (0, 0))
    return pl.pallas_call(
        body, name=name, grid=(seq // tr,), in_specs=[row, vec, row], out_specs=[one, row, vec],
        out_shape=[jax.ShapeDtypeStruct((1, 1), F32), jax.ShapeDtypeStruct((seq, d), F32),
                   jax.ShapeDtypeStruct((1, d), F32)],
        compiler_params=pltpu.CompilerParams(dimension_semantics=("arbitrary",), vmem_limit_bytes=VMEM_LIMIT),
    )(x, w, target)


SLAB_TILE_ROWS = 1024


def _slab_tile(rows, cap=SLAB_TILE_ROWS):
    step = 16 if rows % 16 == 0 else 8
    return max(t for t in range(step, min(rows, cap) + 1, step) if rows % t == 0)


def _adamw(name, w, g, m, v):
    rows, cols = w.shape
    tr = _slab_tile(rows, SLAB_TILE_ROWS // 2) if rows % 8 == 0 else rows

    def body(w_ref, g_ref, m_ref, v_ref, d_ref, nm_ref, nv_ref):
        gv = g_ref[...]
        nm = ADAM_B1 * m_ref[...] + (1.0 - ADAM_B1) * gv
        nv = ADAM_B2 * v_ref[...] + (1.0 - ADAM_B2) * (gv * gv)
        m_hat = nm / (1.0 - ADAM_B1 ** ADAM_STEP)
        v_hat = nv / (1.0 - ADAM_B2 ** ADAM_STEP)
        d_ref[...] = -ADAM_LR * (m_hat / (jnp.sqrt(v_hat) + ADAM_EPS) + ADAM_WD * w_ref[...])
        nm_ref[...] = nm
        nv_ref[...] = nv

    spec = pl.BlockSpec((tr, cols), lambda i: (i, 0))
    sds = jax.ShapeDtypeStruct(w.shape, F32)
    return pl.pallas_call(
        body, name=name, grid=(rows // tr,), in_specs=[spec] * 4, out_specs=[spec] * 3, out_shape=[sds] * 3,
        compiler_params=pltpu.CompilerParams(dimension_semantics=("parallel",), vmem_limit_bytes=VMEM_LIMIT),
    )(w, g, m, v)


def _place_rows(name, slab, tail, first_row):
    nsec, rows, _ = tail.shape
    tr = math.gcd(rows, first_row)
    tr = _slab_tile(tr, SLAB_TILE_ROWS // 2)
    first_blk = first_row // tr

    def body(t_ref, s_ref, o_ref):
        o_ref[...] = t_ref[...]

    return pl.pallas_call(
        body, name=name, grid=(nsec, rows // tr),
        in_specs=[pl.BlockSpec((1, tr, LANES), lambda s, i: (s, i, 0)), pl.BlockSpec(memory_space=pl.ANY)],
        out_specs=pl.BlockSpec((1, tr, LANES), lambda s, i: (s, first_blk + i, 0)),
        out_shape=jax.ShapeDtypeStruct(slab.shape, slab.dtype), input_output_aliases={1: 0},
        compiler_params=pltpu.CompilerParams(dimension_semantics=("parallel", "parallel"),
                                             vmem_limit_bytes=VMEM_LIMIT),
    )(tail, slab)


WIRE_DTYPE = jnp.bfloat16


def _add_halves(name, g, t1, c):
    nsec, rows, _ = g.shape
    rh = rows // 2
    tr = _slab_tile(rh)
    nb = rh // tr

    def body(c_ref, g_ref, t_ref, o_ref):
        o_ref[...] = (g_ref[...] + t_ref[...]).astype(o_ref.dtype)

    gs = pltpu.PrefetchScalarGridSpec(
        num_scalar_prefetch=1, grid=(nsec, nb),
        in_specs=[pl.BlockSpec((1, tr, LANES), lambda s, i, c_ref: (s, c_ref[0] * nb + i, 0)),
                  pl.BlockSpec((1, tr, LANES), lambda s, i, c_ref: (s, i, 0))],
        out_specs=pl.BlockSpec((1, tr, LANES), lambda s, i, c_ref: (s, i, 0)))
    return pl.pallas_call(
        body, name=name, grid_spec=gs, out_shape=jax.ShapeDtypeStruct((nsec, rh, LANES), WIRE_DTYPE),
        compiler_params=pltpu.CompilerParams(dimension_semantics=("parallel", "parallel"),
                                             vmem_limit_bytes=VMEM_LIMIT),
    )(c, g, t1)


ANY = pl.BlockSpec(memory_space=pl.ANY)


def _place():
    return lax.axis_index("x"), lax.axis_index("y"), lax.axis_index("c")


def _all_gather_shards(name, slab):
    rows = slab.shape[0]
    rh = rows // 2
    rq = rh // 2

    def body(x_ref, out_ref, send_sems, recv_sems):
        x, y, c = _place()
        me, sibling = (x, y, c), (x, y, 1 - c)
        xn, yn, dg = (1 - x, y), (x, 1 - y), (1 - x, 1 - y)

        def piece(chip, core, q):
            return out_ref.at[2 * chip[0] + chip[1], pl.ds(core * rh + q * rq, rq), :]

        def copy(k, chip, core, q, to, src=None):
            return pltpu.make_async_remote_copy(
                src_ref=piece(chip, core, q) if src is None else src, dst_ref=piece(chip, core, q),
                send_sem=send_sems.at[k], recv_sem=recv_sems.at[k], device_id=to, device_id_type=MESH)

        own = [x_ref.at[pl.ds(c * rh + q * rq, rq), :] for q in range(2)]
        sends = [copy(0, (x, y), c, 0, (*xn, c), src=own[0]), copy(1, (x, y), c, 1, (*xn, c), src=own[1]),
                 copy(2, (x, y), c, 0, (*yn, c), src=own[0]), copy(3, (x, y), c, 1, (*yn, c), src=own[1])]
        for cp in sends:
            cp.start()
        landed = [(0, xn, 0), (3, yn, 1), (1, xn, 1), (2, yn, 0), (4, dg, 0), (5, dg, 1)]
        onward = {0: (4, (*yn, c)), 3: (5, (*xn, c))}
        for i, (k, chip, q) in enumerate(landed):
            copy(k, chip, c, q, me).wait_recv()
            if k in onward:
                fk, to = onward[k]
                sends.append(copy(fk, chip, c, q, to))
                sends[-1].start()
            sends.append(copy(6 + i, chip, c, q, sibling))
            sends[-1].start()
        for i, (k, chip, q) in enumerate(landed):
            copy(6 + i, chip, 1 - c, q, me).wait_recv()
        for cp in sends:
            cp.wait_send()

    got = pl.pallas_call(
        body, name=name, in_specs=[ANY], out_specs=ANY,
        out_shape=jax.ShapeDtypeStruct((4, rows, LANES), slab.dtype),
        scratch_shapes=[pltpu.SemaphoreType.DMA((12,)), pltpu.SemaphoreType.DMA((12,))],
    )(slab)
    return lax.dynamic_update_slice(got, slab[None], (2 * lax.axis_index("x") + lax.axis_index("y"), 0, 0))


def _swap_halves(name, g):
    nsec, rows, _ = g.shape
    rh = rows // 2

    def body(g_ref, t_ref, send_sem, recv_sem):
        x, y, c = _place()
        cp = pltpu.make_async_remote_copy(
            src_ref=g_ref.at[:, pl.ds((1 - c) * rh, rh), :], dst_ref=t_ref, send_sem=send_sem, recv_sem=recv_sem,
            device_id=(x, y, 1 - c), device_id_type=MESH)
        cp.start()
        cp.wait()

    return pl.pallas_call(
        body, name=name, in_specs=[ANY], out_specs=ANY, out_shape=jax.ShapeDtypeStruct((nsec, rh, LANES), F32),
        scratch_shapes=[pltpu.SemaphoreType.DMA, pltpu.SemaphoreType.DMA],
    )(g)


def _exchange_stage1(name, p):
    _, rh, _ = p.shape
    rq = rh // 2

    def body(p_ref, fx_ref, fy_ref, send_sems, recv_sems):
        x, y, c = _place()
        to_x = pltpu.make_async_remote_copy(
            src_ref=p_ref.at[pl.ds(2 * (1 - x), 2), pl.ds(0, rq), :], dst_ref=fx_ref, send_sem=send_sems.at[0],
            recv_sem=recv_sems.at[0], device_id=(1 - x, y, c), device_id_type=MESH)
        to_y = [pltpu.make_async_remote_copy(
            src_ref=p_ref.at[2 * sx + (1 - y), pl.ds(rq, rq), :], dst_ref=fy_ref.at[sx], send_sem=send_sems.at[1 + sx],
            recv_sem=recv_sems.at[1 + sx], device_id=(x, 1 - y, c), device_id_type=MESH) for sx in range(2)]
        for cp in [to_x] + to_y:
            cp.start()
        for cp in [to_x] + to_y:
            cp.wait_recv()
        for cp in [to_x] + to_y:
            cp.wait_send()

    sds = jax.ShapeDtypeStruct((2, rq, LANES), p.dtype)
    return pl.pallas_call(
        body, name=name, in_specs=[ANY], out_specs=[ANY, ANY], out_shape=[sds, sds],
        scratch_shapes=[pltpu.SemaphoreType.DMA((3,)), pltpu.SemaphoreType.DMA((3,))],
    )(p)


def _exchange_add1(name, p, from_x, from_y, place):
    _, rh, _ = p.shape
    rq = rh // 2
    tr = _slab_tile(rq)
    nb = rq // tr

    def body(xy_ref, pa_s, pa_k, pb_s, pb_k, fx_s, fx_k, fy_s, fy_k, sa, ka, sb, kb):
        for mine, theirs, out in ((pa_s, fx_s, sa), (pa_k, fx_k, ka), (pb_s, fy_s, sb), (pb_k, fy_k, kb)):
            out[...] = (mine[0].astype(F32) + theirs[0].astype(F32)).astype(out.dtype)

    blk = lambda fn: pl.BlockSpec((1, tr, LANES), fn)
    gs = pltpu.PrefetchScalarGridSpec(
        num_scalar_prefetch=1, grid=(nb,),
        in_specs=[blk(lambda i, xy: (2 * xy[0] + 1 - xy[1], i, 0)), blk(lambda i, xy: (2 * xy[0] + xy[1], i, 0)),
                  blk(lambda i, xy: (2 * (1 - xy[0]) + xy[1], nb + i, 0)), blk(lambda i, xy: (2 * xy[0] + xy[1], nb + i, 0)),
                  blk(lambda i, xy: (1 - xy[1], i, 0)), blk(lambda i, xy: (xy[1], i, 0)),
                  blk(lambda i, xy: (1 - xy[0], i, 0)), blk(lambda i, xy: (xy[0], i, 0))],
        out_specs=[pl.BlockSpec((tr, LANES), lambda i, xy: (i, 0))] * 4)
    sds = jax.ShapeDtypeStruct((rq, LANES), p.dtype)
    return pl.pallas_call(
        body, name=name, grid_spec=gs, out_shape=[sds] * 4,
        compiler_params=pltpu.CompilerParams(dimension_semantics=("parallel",), vmem_limit_bytes=VMEM_LIMIT),
    )(place, p, p, p, p, from_x, from_x, from_y, from_y)


def _exchange_stage2(name, send_a, send_b):
    def body(a_ref, b_ref, fa_ref, fb_ref, send_sems, recv_sems):
        x, y, c = _place()
        cps = [pltpu.make_async_remote_copy(src_ref=a_ref, dst_ref=fa_ref, send_sem=send_sems.at[0],
                                            recv_sem=recv_sems.at[0], device_id=(x, 1 - y, c), device_id_type=MESH),
               pltpu.make_async_remote_copy(src_ref=b_ref, dst_ref=fb_ref, send_sem=send_sems.at[1],
                                            recv_sem=recv_sems.at[1], device_id=(1 - x, y, c), device_id_type=MESH)]
        for cp in cps:
            cp.start()
        for cp in cps:
            cp.wait_recv()
        for cp in cps:
            cp.wait_send()

    sds = jax.ShapeDtypeStruct(send_a.shape, send_a.dtype)
    return pl.pallas_call(
        body, name=name, in_specs=[ANY, ANY], out_specs=[ANY, ANY], out_shape=[sds, sds],
        scratch_shapes=[pltpu.SemaphoreType.DMA((2,)), pltpu.SemaphoreType.DMA((2,))],
    )(send_a, send_b)


def _exchange_add2(name, keep_a, got_a, keep_b, got_b, c):
    rq = keep_a.shape[0]
    tr = _slab_tile(rq)

    def body(c_ref, ka, ga, kb, gb, o_ref):
        o_ref[0] = ka[...].astype(F32) + ga[...].astype(F32)
        o_ref[1] = kb[...].astype(F32) + gb[...].astype(F32)

    spec = pl.BlockSpec((tr, LANES), lambda i, c_ref: (i, 0))
    gs = pltpu.PrefetchScalarGridSpec(
        num_scalar_prefetch=1, grid=(rq // tr,), in_specs=[spec] * 4,
        out_specs=pl.BlockSpec((2, tr, LANES), lambda i, c_ref: (c_ref[0], i, 0)))
    out = pl.pallas_call(
        body, name=name, grid_spec=gs, out_shape=jax.ShapeDtypeStruct((4, rq, LANES), F32),
        compiler_params=pltpu.CompilerParams(dimension_semantics=("parallel",), vmem_limit_bytes=VMEM_LIMIT),
    )(c, keep_a, got_a, keep_b, got_b)
    return out.reshape(4 * rq, LANES)


def _join_halves(name, full):
    rh = full.shape[0] // 2

    def body(in_ref, o_ref, send_sem, recv_sem):
        x, y, c = _place()
        cp = pltpu.make_async_remote_copy(
            src_ref=in_ref.at[pl.ds(c * rh, rh), :], dst_ref=o_ref.at[pl.ds(c * rh, rh), :], send_sem=send_sem,
            recv_sem=recv_sem, device_id=(x, y, 1 - c), device_id_type=MESH)
        cp.start()
        pltpu.make_async_remote_copy(
            src_ref=in_ref.at[pl.ds(c * rh, rh), :], dst_ref=o_ref.at[pl.ds((1 - c) * rh, rh), :], send_sem=send_sem,
            recv_sem=recv_sem, device_id=(x, y, 1 - c), device_id_type=MESH).wait_recv()
        cp.wait_send()

    return pl.pallas_call(
        body, name=name, in_specs=[ANY], out_specs=ANY, out_shape=jax.ShapeDtypeStruct(full.shape, full.dtype),
        input_output_aliases={0: 0}, scratch_shapes=[pltpu.SemaphoreType.DMA, pltpu.SemaphoreType.DMA],
    )(full)


def _rows_of(n):
    return -(-n // LANES)


SLAB_ROW_ALIGN = 512


def _flat_rows(arrays, dtype):
    parts = []
    for a in arrays:
        flat = a.reshape(-1).astype(dtype)
        parts.append(jnp.pad(flat, (0, _rows_of(flat.size) * LANES - flat.size)))
    return jnp.concatenate(parts).reshape(-1, LANES)


def _align_rows(slab):
    rows = slab.shape[0]
    return jnp.pad(slab, ((0, -(-rows // SLAB_ROW_ALIGN) * SLAB_ROW_ALIGN - rows), (0, 0)))


def _pack(arrays, dtype):
    return _align_rows(_flat_rows(arrays, dtype))


def _unpack(slab, shapes):
    out, r = [], 0
    for shp in shapes:
        n = math.prod(shp)
        out.append(slab[r:r + _rows_of(n)].reshape(-1)[:n].reshape(shp))
        r += _rows_of(n)
    return out


def _unpack_gathered(g, shapes, kinds):
    out, r = [], 0
    for shp, kind in zip(shapes, kinds):
        n = math.prod(shp)
        blk = g[:, r:r + _rows_of(n)].reshape(4, -1)[:, :n].reshape((4,) + tuple(shp))
        r += _rows_of(n)
        if kind == "col":
            out.append(jnp.moveaxis(blk, 0, 1).reshape(shp[0], 4 * shp[1]))
        else:
            out.append(blk.reshape(4 * shp[0], shp[1]))
    return out


def _sections(g, kind, local_shape):
    if kind == "col":
        blocks = jnp.moveaxis(g.reshape(local_shape[0], 4, local_shape[1]), 1, 0)
    elif kind == "row":
        blocks = g.reshape((4,) + tuple(local_shape))
    else:
        blocks = jnp.broadcast_to(g, (4,) + tuple(g.shape))
    flat = blocks.reshape(4, -1)
    rows = _rows_of(flat.shape[1])
    return jnp.pad(flat, ((0, 0), (0, rows * LANES - flat.shape[1]))).reshape(4, rows, LANES)


def _rotary_tables(seq):
    half = RET_DK // 2
    pos = jnp.arange(seq, dtype=F32)
    inv = ROPE_THETA ** (-jnp.arange(half, dtype=F32) / half)
    ang = pos[:, None] * inv[None, :]
    cos, sin = jnp.cos(ang), jnp.sin(ang)
    return jnp.concatenate([cos, cos], axis=1), jnp.concatenate([-sin, sin], axis=1)


def _retention_tables():
    log_gamma = jnp.log(1.0 - 2.0 ** (-5.0 - jnp.arange(RET_HEADS, dtype=F32)))
    idx = jnp.arange(CHUNK, dtype=F32)
    diff = idx[:, None] - idx[None, :]
    dmask = jnp.exp(jnp.where((diff >= 0)[None], log_gamma[:, None, None] * diff[None], -jnp.inf))
    kdec = jnp.exp(log_gamma[None, :] * (CHUNK - 1.0 - idx)[:, None])
    qdec = jnp.exp(log_gamma[None, :] * (idx + 1.0)[:, None])
    cdec = jnp.exp(log_gamma * CHUNK)[None, :]
    lanes = lambda t: jnp.repeat(t, RET_DK, axis=1)
    return dmask.reshape(RET_HEADS * CHUNK, CHUNK), lanes(kdec), lanes(qdec), lanes(cdec)


def _s5_prep(a_re, a_im, log_step, b_re, b_im, c_re, c_im):
    g, n, c = S5_GROUPS, S5_STATE, S5_GROUP
    lam = lax.complex(a_re, a_im)
    step = jnp.exp(log_step)[:, None]
    lam_bar = jnp.exp(lam * step)
    b_bar = ((lam_bar - 1.0) / lam)[..., None] * lax.complex(b_re, b_im)
    eye = jnp.eye(g, dtype=F32)
    bb_re = (jnp.real(b_bar).transpose(0, 2, 1)[:, :, None, :] * eye[:, None, :, None]).reshape(g * c, g * n)
    bb_im = (jnp.imag(b_bar).transpose(0, 2, 1)[:, :, None, :] * eye[:, None, :, None]).reshape(g * c, g * n)
    cc_re = (c_re.transpose(0, 2, 1)[:, :, None, :] * eye[:, None, :, None]).reshape(g * n, g * c)
    cc_im = (c_im.transpose(0, 2, 1)[:, :, None, :] * eye[:, None, :, None]).reshape(g * n, g * c)
    return (jnp.real(lam_bar).reshape(1, g * n), jnp.imag(lam_bar).reshape(1, g * n),
            jnp.concatenate([bb_re, bb_im], axis=1), cc_re, cc_im)


def kernel(x, l0_norm_mix, l0_w_in, ssd_conv_w, ssd_conv_b, ssd_dt_bias, ssd_A_log, ssd_D, ssd_norm_w, l0_w_out, l0_norm_mlp, l0_w_up, l0_w_down, l1_norm_mix, l1_w_in, gdn_conv_w, gdn_A_log, gdn_dt_bias, gdn_norm_w, s5_A_re, s5_A_im, s5_log_step, s5_B_re, s5_B_im, s5_C_re, s5_C_im, s5_D, s5_w_glu, s5_b_glu, l1_w_out, l1_norm_mlp, l1_w_up, l1_w_down, final_norm, loss_target, m_l0_norm_mix, m_l0_w_in, m_ssd_conv_w, m_ssd_conv_b, m_ssd_dt_bias, m_ssd_A_log, m_ssd_D, m_ssd_norm_w, m_l0_w_out, m_l0_norm_mlp, m_l0_w_up, m_l0_w_down, m_l1_norm_mix, m_l1_w_in, m_gdn_conv_w, m_gdn_A_log, m_gdn_dt_bias, m_gdn_norm_w, m_s5_A_re, m_s5_A_im, m_s5_log_step, m_s5_B_re, m_s5_B_im, m_s5_C_re, m_s5_C_im, m_s5_D, m_s5_w_glu, m_s5_b_glu, m_l1_w_out, m_l1_norm_mlp, m_l1_w_up, m_l1_w_down, m_final_norm, v_l0_norm_mix, v_l0_w_in, v_ssd_conv_w, v_ssd_conv_b, v_ssd_dt_bias, v_ssd_A_log, v_ssd_D, v_ssd_norm_w, v_l0_w_out, v_l0_norm_mlp, v_l0_w_up, v_l0_w_down, v_l1_norm_mix, v_l1_w_in, v_gdn_conv_w, v_gdn_A_log, v_gdn_dt_bias, v_gdn_norm_w, v_s5_A_re, v_s5_A_im, v_s5_log_step, v_s5_B_re, v_s5_B_im, v_s5_C_re, v_s5_C_im, v_s5_D, v_s5_w_glu, v_s5_b_glu, v_l1_w_out, v_l1_norm_mlp, v_l1_w_up, v_l1_w_down, v_final_norm):
    given = dict(locals())
    names = [n for n, _ in PARAMS]
    kinds = dict(PARAMS)
    w = {n: given[n] for n in names}
    seq = x.shape[1]
    x0 = x.reshape(seq, D_MODEL)
    target = loss_target.reshape(seq, D_MODEL)

    gb = _all_gather_shards("gather_weights", _pack([w[n] for n in GATHER_BF16], _MXU_DTYPE))
    full = dict(zip(GATHER_BF16, _unpack_gathered(gb, [w[n].shape for n in GATHER_BF16],
                                                  [kinds[n] for n in GATHER_BF16])))
    gf = _all_gather_shards("gather_conv", _pack([w[n] for n in GATHER_F32], F32))
    full.update(zip(GATHER_F32, _unpack_gathered(gf, [w[n].shape for n in GATHER_F32],
                                                 [kinds[n] for n in GATHER_F32])))
    in0 = full["l0_w_in"].shape[1]
    w_in0 = jnp.pad(full["l0_w_in"], ((0, 0), (0, IN0_PAD - in0)))
    wi1 = full["l1_w_in"]
    in1 = wi1.shape[1]
    w_in1 = jnp.concatenate([wi1[:, :3072], wi1[:, 3084:in1], wi1[:, 3072:3084],
                             jnp.zeros((D_MODEL, IN1_PAD - in1), wi1.dtype)], axis=1)

    row = lambda a: a.reshape(1, -1)
    lanes64 = lambda a: jnp.repeat(a, SSD_HEAD_DIM).reshape(1, -1)

    h0 = _rmsnorm_fwd("norm_mix0", x0, row(w["l0_norm_mix"]))
    proj0 = _matmul("in_proj0", h0, w_in0, "nn")
    cos_t, sin_t = _rotary_tables(seq)
    ret_tabs = list(_retention_tables())
    ret_xs = [(proj0, 512, 0), (proj0, 512, 1), (proj0, 512, 2), (proj0, 512, 3)]
    ret_xt = [(cos_t, 128, 0), (sin_t, 128, 0)]
    ret_states = [(512, 128)]
    mixed0, ret_saved = _scan_fwd("ret_fwd", _f_ret, CHUNK, ret_tabs, [], ret_xs, ret_xt, ret_states, D_MODEL, 512, 0)
    expand = jnp.repeat(jnp.eye(128, SSD_HEADS, dtype=F32), SSD_HEAD_DIM, axis=1)
    ssd_consts = [full["ssd_conv_w"], row(w["ssd_conv_b"]), lanes64(w["ssd_dt_bias"]), lanes64(w["ssd_A_log"]),
                  lanes64(w["ssd_D"]), row(w["ssd_norm_w"])]
    ssd_xs = [(proj0, 512, 4), (proj0, 512, 5), (proj0, 256, 12), (proj0, 256, 13), (proj0, 128, 28)]
    ssd_states = [(8, 512), (8, 256), (8, 256), (512, 128)]
    mixed0, ssd_saved = _scan_fwd("ssd_fwd", _f_ssd, CHUNK, [expand], ssd_consts, ssd_xs, [], ssd_states,
                                  D_MODEL, 512, 1, y_alias=mixed0)
    x1, h1 = _matmul("out_proj0", mixed0, full["l0_w_out"], "nn", epi="add", epi_arr=x0, norm_w=row(w["l0_norm_mlp"]))
    u0 = _matmul("up0", h1, full["l0_w_up"], "nn", out_dtype=_MXU_DTYPE)
    x2, h2 = _matmul("down0", u0, full["l0_w_down"], "nn", a_pro="relu2", epi="add", epi_arr=x1,
                     norm_w=row(w["l1_norm_mix"]))

    proj1 = _matmul("in_proj1", h2, w_in1, "nn")
    p_alog = jnp.zeros((1, 128), F32).at[0, 6:12].set(w["gdn_A_log"])
    p_dtb = jnp.zeros((1, 128), F32).at[0, 6:12].set(w["gdn_dt_bias"])
    gdn_consts = [full["gdn_conv_w"], p_alog, p_dtb, row(w["gdn_norm_w"])]
    gdn_xs = [(proj1, 768, 0), (proj1, 768, 1), (proj1, 768, 2), (proj1, 768, 3), (proj1, 128, 26)]
    gdn_states = [(8, 768), (8, 768), (8, 768), (768, 256)]
    mixed1, gdn_saved = _scan_fwd("gdn_fwd", _f_gdn, CHUNK, [], gdn_consts, gdn_xs, [], gdn_states, D_MODEL, 768, 0)
    s5_args = (w["s5_A_re"], w["s5_A_im"], w["s5_log_step"], w["s5_B_re"], w["s5_B_im"], w["s5_C_re"], w["s5_C_im"])
    (lam_re, lam_im, bblk, cc_re, cc_im), s5_prep_vjp = jax.vjp(_s5_prep, *s5_args)
    s5_consts = [lam_re, lam_im, bblk, cc_re, cc_im, row(w["s5_D"]), full["s5_w_glu"].astype(F32), row(w["s5_b_glu"])]
    s5_xs = [(proj1, 256, 12)]
    s5_states = [(8, 1024), (8, 1024)]
    mixed1, s5_saved = _scan_fwd("s5_fwd", _f_s5, CHUNK, [], s5_consts, s5_xs, [], s5_states, D_MODEL, 256, 3,
                                 y_alias=mixed1)
    x3, h3 = _matmul("out_proj1", mixed1, full["l1_w_out"], "nn", epi="add", epi_arr=x2, norm_w=row(w["l1_norm_mlp"]))
    u1 = _matmul("up1", h3, full["l1_w_up"], "nn", out_dtype=_MXU_DTYPE)
    x4 = _matmul("down1", u1, full["l1_w_down"], "nn", a_pro="relu2", epi="add", epi_arr=x3)

    loss_part, dx4, d_final = _loss_head("loss_head", x4, row(w["final_norm"]), target)
    loss = lax.psum(loss_part[0, 0], ("x", "y", "c"))
    grads = {"final_norm": d_final.reshape(-1)}
    small = SMALL_SHARDED + tuple(n for n in names if kinds[n] == "rep")
    order = LARGE + small
    first_row, slab_rows = {}, 0
    for n in order:
        first_row[n] = slab_rows
        slab_rows += _rows_of(math.prod(w[n].shape))
    slab_rows = -(-slab_rows // SLAB_ROW_ALIGN) * SLAB_ROW_ALIGN

    du1 = _matmul("down1_dx", dx4, full["l1_w_down"], "nt", out_dtype=_MXU_DTYPE, epi="drelu2", epi_arr=u1)
    gslab = _matmul("down1_dw", u1, dx4, "tn", a_pro="relu2", slab=(None, slab_rows, first_row["l1_w_down"], 1024))
    gslab = _matmul("up1_dw", h3, du1, "tn", slab=(gslab, slab_rows, first_row["l1_w_up"], None))
    dx3, dwn = _matmul("up1_dx", du1, full["l1_w_up"], "nt", epi="norm_bwd", epi_arr=dx4, norm_x=x3,
                       norm_w=row(w["l1_norm_mlp"]))
    grads["l1_norm_mlp"] = dwn.reshape(-1)
    grads["l1_w_out"] = _matmul("out_proj1_dw", mixed1, dx3, "tn")
    dmixed1 = _matmul("out_proj1_dx", dx3, full["l1_w_out"], "nt")

    def gdn_assemble(dx):
        dq, dk, dv, dz, dba = dx
        zeros = lambda n: jnp.zeros((dq.shape[0], n), F32)
        return jnp.concatenate([dq, dk, dv, dz, zeros(256), dba, zeros(IN1_PAD - 3456)], axis=1)

    dproj1, gdn_dc = _scan_bwd("gdn_bwd", _f_gdn, CHUNK, [], gdn_consts, gdn_xs, [], gdn_saved, gdn_states,
                               (dmixed1, 768, 0), IN1_PAD, IN1_PAD, 0, gdn_assemble)
    dproj1, s5_dc = _scan_bwd("s5_bwd", _f_s5, CHUNK, [], s5_consts, s5_xs, [], s5_saved, s5_states,
                              (dmixed1, 256, 3), IN1_PAD, 256, 12, lambda dx: dx[0], dx_alias=dproj1)
    grads["gdn_conv_w"] = gdn_dc[0]
    grads["gdn_A_log"] = gdn_dc[1][0, 6:12]
    grads["gdn_dt_bias"] = gdn_dc[2][0, 6:12]
    grads["gdn_norm_w"] = gdn_dc[3].reshape(-1)
    s5_pg = s5_prep_vjp(tuple(s5_dc[:5]))
    for n, gval in zip(("s5_A_re", "s5_A_im", "s5_log_step", "s5_B_re", "s5_B_im", "s5_C_re", "s5_C_im"), s5_pg):
        grads[n] = gval
    grads["s5_D"] = s5_dc[5].reshape(-1)
    grads["s5_w_glu"] = s5_dc[6]
    grads["s5_b_glu"] = s5_dc[7].reshape(-1)
    dwi1 = _matmul("in_proj1_dw", h2, dproj1, "tn")
    grads["l1_w_in"] = jnp.concatenate([dwi1[:, :3072], dwi1[:, 3328:3340], dwi1[:, 3072:3328]], axis=1)
    dx2, dwn = _matmul("in_proj1_dx", dproj1, w_in1, "nt", epi="norm_bwd", epi_arr=dx3, norm_x=x2,
                       norm_w=row(w["l1_norm_mix"]))
    grads["l1_norm_mix"] = dwn.reshape(-1)

    du0 = _matmul("down0_dx", dx2, full["l0_w_down"], "nt", out_dtype=_MXU_DTYPE, epi="drelu2", epi_arr=u0)
    gslab = _matmul("down0_dw", u0, dx2, "tn", a_pro="relu2", slab=(gslab, slab_rows, first_row["l0_w_down"], 1024))
    gslab = _matmul("up0_dw", h1, du0, "tn", slab=(gslab, slab_rows, first_row["l0_w_up"], None))
    dx1, dwn = _matmul("up0_dx", du0, full["l0_w_up"], "nt", epi="norm_bwd", epi_arr=dx2, norm_x=x1,
                       norm_w=row(w["l0_norm_mlp"]))
    grads["l0_norm_mlp"] = dwn.reshape(-1)
    grads["l0_w_out"] = _matmul("out_proj0_dw", mixed0, dx1, "tn")
    dmixed0 = _matmul("out_proj0_dx", dx1, full["l0_w_out"], "nt")
    dproj0, _ = _scan_bwd("ret_bwd", _f_ret, CHUNK, ret_tabs, [], ret_xs, ret_xt, ret_saved, ret_states,
                          (dmixed0, 512, 0), IN0_PAD, 2048, 0, lambda dx: jnp.concatenate(dx, axis=1))

    def ssd_assemble(dx):
        return jnp.concatenate(list(dx) + [jnp.zeros((dx[0].shape[0], 2048 - 1664), F32)], axis=1)

    dproj0, ssd_dc = _scan_bwd("ssd_bwd", _f_ssd, CHUNK, [expand], ssd_consts, ssd_xs, [], ssd_saved, ssd_states,
                               (dmixed0, 512, 1), IN0_PAD, 2048, 1, ssd_assemble, dx_alias=dproj0)
    heads = lambda a: a.reshape(SSD_HEADS, SSD_HEAD_DIM).sum(axis=1)
    grads["ssd_conv_w"] = ssd_dc[0]
    grads["ssd_conv_b"] = ssd_dc[1].reshape(-1)
    grads["ssd_dt_bias"] = heads(ssd_dc[2])
    grads["ssd_A_log"] = heads(ssd_dc[3])
    grads["ssd_D"] = heads(ssd_dc[4])
    grads["ssd_norm_w"] = ssd_dc[5].reshape(-1)
    grads["l0_w_in"] = _matmul("in_proj0_dw", h0, dproj0, "tn")[:, :in0]
    dx0, dwn = _matmul("in_proj0_dx", dproj0, w_in0, "nt", epi="norm_bwd", epi_arr=dx1, norm_x=x0,
                       norm_w=row(w["l0_norm_mix"]))
    grads["l0_norm_mix"] = dwn.reshape(-1)
    grad_x = dx0.reshape(x.shape)

    c_idx = lax.axis_index("c").astype(jnp.int32).reshape(1)
    tail = order[SLAB_DIRECT:]
    parts = [_sections(grads[n].reshape(_full_shape(n, w, kinds)), kinds[n], w[n].shape) for n in tail]
    parts.append(jnp.zeros((4, slab_rows - first_row[tail[0]] - sum(p.shape[1] for p in parts), LANES), F32))
    gslab = _place_rows("grads_place_tail", gslab, jnp.concatenate(parts, axis=1), first_row[tail[0]])
    from_sibling = _swap_halves("grads_swap_halves", gslab)
    chip_sum = _add_halves("grads_add_sibling", gslab, from_sibling, c_idx)
    place = jnp.stack([lax.axis_index("x"), lax.axis_index("y")]).astype(jnp.int32)
    from_x, from_y = _exchange_stage1("grads_stage1", chip_sum)
    send_a, keep_a, send_b, keep_b = _exchange_add1("grads_add1", chip_sum, from_x, from_y, place)
    got_a, got_b = _exchange_stage2("grads_stage2", send_a, send_b)
    my_half = _exchange_add2("grads_add2", keep_a, got_a, keep_b, got_b, c_idx)
    gsum = _join_halves("grads_join_halves", my_half)
    grad = dict(zip(order, _unpack(gsum, [w[n].shape for n in order])))

    delta, new_m, new_v = {}, {}, {}
    for n in LARGE:
        delta[n], new_m[n], new_v[n] = _adamw("adamw_" + n, w[n], grad[n], given["m_" + n], given["v_" + n])
    first_small = sum(_rows_of(math.prod(w[n].shape)) for n in LARGE)
    small_shapes = [w[n].shape for n in small]

    def small_slab(arrays):
        rows = _flat_rows(arrays, F32)
        return jnp.pad(rows, ((0, gsum.shape[0] - first_small - rows.shape[0]), (0, 0)))

    res = _adamw("adamw_small", small_slab([w[n] for n in small]), gsum[first_small:],
                 small_slab([given["m_" + n] for n in small]), small_slab([given["v_" + n] for n in small]))
    for out, slab in zip((delta, new_m, new_v), res):
        out.update(zip(small, _unpack(slab, small_shapes)))
    return (loss, grad_x, *[grad[n] for n in names], *[delta[n] for n in names], *[new_m[n] for n in names],
            *[new_v[n] for n in names])


def _full_shape(name, w, kinds):
    shp = w[name].shape
    if kinds[name] == "col":
        return (shp[0], 4 * shp[1])
    if kinds[name] == "row":
        return (4 * shp[0],) + tuple(shp[1:])
    return shp
```

```python
import functools
import math

import jax
import jax.numpy as jnp
from jax import lax
from jax.experimental import pallas as pl
from jax.experimental.pallas import tpu as pltpu

F32 = jnp.float32
_MXU_DTYPE = jnp.bfloat16

D_MODEL = 1024
CHUNK = 64
EPS = 1e-6
RET_HEADS, RET_DK = 4, 128
ROPE_THETA = 10000.0
SSD_HEADS, SSD_HEAD_DIM = 8, 64
GDN_HEADS, GDN_DK = 6, 128
S5_GROUPS, S5_GROUP, S5_STATE = 16, 16, 64
ADAM_LR, ADAM_B1, ADAM_B2, ADAM_EPS, ADAM_WD, ADAM_STEP = 0.001, 0.9, 0.999, 1e-08, 0.01, 10

IN0_PAD = 4096
IN1_PAD = 3584
LANES = 1024
VMEM_LIMIT = 56 * 1024 * 1024
MESH = pl.DeviceIdType.MESH

PARAMS = (
    ("l0_norm_mix", "rep"), ("l0_w_in", "col"), ("ssd_conv_w", "col"), ("ssd_conv_b", "rep"),
    ("ssd_dt_bias", "rep"), ("ssd_A_log", "rep"), ("ssd_D", "rep"), ("ssd_norm_w", "rep"),
    ("l0_w_out", "row"), ("l0_norm_mlp", "rep"), ("l0_w_up", "col"), ("l0_w_down", "row"),
    ("l1_norm_mix", "rep"), ("l1_w_in", "col"), ("gdn_conv_w", "col"), ("gdn_A_log", "rep"),
    ("gdn_dt_bias", "rep"), ("gdn_norm_w", "rep"), ("s5_A_re", "rep"), ("s5_A_im", "rep"),
    ("s5_log_step", "rep"), ("s5_B_re", "rep"), ("s5_B_im", "rep"), ("s5_C_re", "rep"), ("s5_C_im", "rep"),
    ("s5_D", "rep"), ("s5_w_glu", "row"), ("s5_b_glu", "rep"), ("l1_w_out", "row"), ("l1_norm_mlp", "rep"),
    ("l1_w_up", "col"), ("l1_w_down", "row"), ("final_norm", "rep"),
)
GATHER_BF16 = ("l0_w_in", "l0_w_out", "l0_w_up", "l0_w_down", "l1_w_in", "l1_w_out", "l1_w_up", "l1_w_down", "s5_w_glu")
GATHER_F32 = ("ssd_conv_w", "gdn_conv_w")
LARGE = ("l0_w_up", "l0_w_down", "l1_w_up", "l1_w_down", "l0_w_out", "l1_w_out", "l0_w_in", "l1_w_in")
SLAB_DIRECT = 4
SMALL_SHARDED = ("s5_w_glu", "ssd_conv_w", "gdn_conv_w")


def _dg(a, b, ca, cb, prec=None):
    return lax.dot_general(a, b, (((ca,), (cb,)), ((), ())), preferred_element_type=F32, precision=prec)


def _lo(a):
    return a.astype(_MXU_DTYPE)


@jax.custom_vjp
def _mm(a, b):
    return _dg(_lo(a), _lo(b), 1, 0)


def _mm_fwd(a, b):
    return _mm(a, b), (a, b)


def _mm_bwd(res, g):
    a, b = res
    return _dg(_lo(g), _lo(b), 1, 1), _dg(_lo(a), _lo(g), 0, 0)


_mm.defvjp(_mm_fwd, _mm_bwd)


@jax.custom_vjp
def _mm_nt(a, b):
    return _dg(_lo(a), _lo(b), 1, 1)


def _mm_nt_fwd(a, b):
    return _mm_nt(a, b), (a, b)


def _mm_nt_bwd(res, g):
    a, b = res
    return _dg(_lo(g), _lo(b), 1, 0), _dg(_lo(g), _lo(a), 0, 0)


_mm_nt.defvjp(_mm_nt_fwd, _mm_nt_bwd)


@jax.custom_vjp
def _mm_tn(a, b):
    return _dg(_lo(a), _lo(b), 0, 0)


def _mm_tn_fwd(a, b):
    return _mm_tn(a, b), (a, b)


def _mm_tn_bwd(res, g):
    a, b = res
    return _dg(_lo(b), _lo(g), 1, 1), _dg(_lo(a), _lo(g), 1, 0)


_mm_tn.defvjp(_mm_tn_fwd, _mm_tn_bwd)


def _split2(x):
    hi = _lo(x)
    return hi, _lo(x - hi.astype(F32))


def _split3(x):
    h1 = _lo(x)
    r1 = x - h1.astype(F32)
    h2 = _lo(r1)
    return h1, h2, _lo(r1 - h2.astype(F32))


def _tri_cum_dir(m, ca):
    n, w = m.shape
    causal, _ = _tri_masks(n)
    out = _dg(causal.astype(_MXU_DTYPE), jnp.concatenate(_split3(m), axis=1), ca, 0)
    return out[:, :w] + out[:, w:2 * w] + out[:, 2 * w:]


@jax.custom_vjp
def _tri_cum(m):
    return _tri_cum_dir(m, 1)


def _tri_cum_fwd(m):
    return _tri_cum_dir(m, 1), None


def _tri_cum_bwd(_, g):
    return (_tri_cum_dir(g, 0),)


_tri_cum.defvjp(_tri_cum_fwd, _tri_cum_bwd)


@jax.custom_vjp
def _mm_exact_rhs(a, e):
    return _dg(jnp.concatenate(_split3(a), axis=1), jnp.concatenate([_lo(e)] * 3, axis=0), 1, 0)


def _mm_exact_rhs_fwd(a, e):
    return _mm_exact_rhs(a, e), e


def _mm_exact_rhs_bwd(e, g):
    return _dg(jnp.concatenate(_split3(g), axis=1), jnp.concatenate([_lo(e)] * 3, axis=1), 1, 1), jnp.zeros_like(e)


_mm_exact_rhs.defvjp(_mm_exact_rhs_fwd, _mm_exact_rhs_bwd)


def _bd(x):
    left = _iota(x.shape, 1) < (x.shape[1] // 2)
    zero = jnp.zeros_like(x)
    return jnp.concatenate([jnp.where(left, x, zero), jnp.where(left, zero, x)], axis=0)


def _unbd(m):
    half = m.shape[0] // 2
    left = _iota((half, m.shape[1]), 1) < (m.shape[1] // 2)
    return jnp.where(left, m[:half], m[half:])


def _pmm_nn(x, y):
    xh, xl = _split2(x)
    yh, yl = _split2(y)
    return _dg(jnp.concatenate([xh, xl, xh], axis=1), jnp.concatenate([_bd(yh), _bd(yh), _bd(yl)], axis=0), 1, 0)


def _pmm_nt(x, y):
    xh, xl = _split2(x)
    yh, yl = _split2(y)
    return _dg(jnp.concatenate([xh, xl, xh], axis=1), jnp.concatenate([_bd(yh), _bd(yh), _bd(yl)], axis=1), 1, 1)


def _pmm_tn(x, y):
    xh, xl = _split2(x)
    yh, yl = _split2(y)
    return _unbd(_dg(jnp.concatenate([xh, xl, xh], axis=0), jnp.concatenate([yh, yh, yl], axis=0), 0, 0))


@functools.lru_cache(maxsize=None)
def _shift(s, axis):
    @jax.custom_vjp
    def sh(x):
        return pltpu.roll(x, s, axis)

    def fwd(x):
        return sh(x), None

    def bwd(_, g):
        n = g.shape[axis]
        return (pltpu.roll(g, (n - s) % n, axis),)

    sh.defvjp(fwd, bwd)
    return sh


def _iota(shape, axis):
    return lax.broadcasted_iota(jnp.int32, shape, axis)


def _silu(x):
    return x * jax.nn.sigmoid(x)


def _unit_rms(x):
    return x * lax.rsqrt(jnp.mean(x * x, axis=-1, keepdims=True) + EPS)


def _l2norm(x):
    return x * lax.rsqrt(jnp.sum(x * x, axis=-1, keepdims=True) + EPS)


def _tri_masks(n):
    r, c = _iota((n, n), 0), _iota((n, n), 1)
    return r >= c, r > c


def _packed_rc():
    return _iota((CHUNK, 2 * CHUNK), 0), _iota((CHUNK, 2 * CHUNK), 1) & (CHUNK - 1)


def _decay_packed(g_packed):
    r, c = _packed_rc()
    seg = _tri_cum(g_packed * (r > c).astype(F32))
    return jnp.where(r >= c, jnp.exp(jnp.where(r >= c, seg, 0.0)), 0.0)


def _conv(x, tail, w):
    rows, width = x.shape
    row = _iota((rows, width), 0)
    acc = x * w[3:4, :]
    pad = jnp.zeros((rows - 8, width), F32)
    for j in range(3):
        s = 3 - j
        prev = jnp.concatenate([_shift(s, 0)(tail), pad], axis=0)
        acc = acc + w[j:j + 1, :] * jnp.where(row < s, prev, _shift(s, 0)(x))
    return acc


def _tri_inv_impl(mats):
    r, c = _packed_rc()
    eye = (r == c).astype(F32)

    def same_block(b):
        return (r // b) == (c // b)

    a8 = [jnp.where(same_block(8), a, 0.0) for a in mats]
    a2 = [_pmm_nn(t, t) for t in a8]
    a4 = [_pmm_nn(t, t) for t in a2]
    x = [_pmm_nn(eye - p, eye + q) for p, q in zip(a8, a2)]
    x = [_pmm_nn(p, eye + q) for p, q in zip(x, a4)]
    for b in (8, 16, 32):
        off = [jnp.where(same_block(2 * b) & jnp.logical_not(same_block(b)), a, 0.0) for a in mats]
        y = [_pmm_nn(p, q) for p, q in zip(x, off)]
        x = [p - _pmm_nn(q, p) for p, q in zip(x, y)]
    return x


@jax.custom_vjp
def _tri_inv(mats):
    return _tri_inv_impl(mats)


def _tri_inv_fwd(mats):
    t = _tri_inv_impl(mats)
    return t, t


def _tri_inv_bwd(t, g):
    m1 = [_pmm_tn(p, q) for p, q in zip(t, g)]
    return ([-_pmm_nt(p, q) for p, q in zip(m1, t)],)


_tri_inv.defvjp(_tri_inv_fwd, _tri_inv_bwd)


def _f_ret(tabs, consts, xs, xtabs, states):
    dmask, kdec, qdec, cdec = tabs
    q, k, v, gate = xs
    cs, sn = xtabs
    (st,) = states
    swap = _shift(RET_DK // 2, 1)
    heads = range(RET_HEADS)
    sls = [slice(128 * h, 128 * h + 128) for h in heads]
    qh = [(q[:, sl] * cs + swap(q[:, sl]) * sn) * (RET_DK ** -0.5) for sl in sls]
    kh = [k[:, sl] * cs + swap(k[:, sl]) * sn for sl in sls]
    sh = [st[sl, :] for sl in sls]
    scores = [_mm_nt(a, b) * dmask[64 * h:64 * h + 64, :] for h, a, b in zip(heads, qh, kh)]
    y = [_mm(s, v[:, sl]) for s, sl in zip(scores, sls)]
    y = [t + _mm(a * qdec[:, sl], s) for t, a, sl, s in zip(y, qh, sls, sh)]
    new = [s * cdec[:, sl] + _mm_tn(b * kdec[:, sl], v[:, sl]) for s, sl, b in zip(sh, sls, kh)]
    outs = [_silu(gate[:, sl]) * _unit_rms(t) for sl, t in zip(sls, y)]
    return (jnp.concatenate(outs, axis=1),), [jnp.concatenate(new, axis=0)]


def _f_ssd(tabs, consts, xs, xtabs, states):
    (expand,) = tabs
    conv_w, conv_b, dtb, alog, dskip, nw = consts
    z, xr, br, cr, dtr = xs
    tx, tb, tc, st = states
    xc = _silu(_conv(xr, tx, conv_w[:, 0:512]) + conv_b[:, 0:512])
    bc = _silu(_conv(br, tb, conv_w[:, 512:768]) + conv_b[:, 512:768])
    cc = _silu(_conv(cr, tc, conv_w[:, 768:1024]) + conv_b[:, 768:1024])
    dt = jax.nn.softplus(_mm_exact_rhs(dtr, expand) + dtb)
    la = dt * (-jnp.exp(alog))
    lacum = _tri_cum(la)
    total = jnp.sum(la, axis=0, keepdims=True)
    xd = xc * dt
    dte, ecum, cdec = jnp.exp(total - lacum), jnp.exp(lacum), jnp.exp(total)
    pairs = range(SSD_HEADS // 2)
    sls = [slice(128 * p, 128 * p + 128) for p in pairs]
    bg = [bc[:, 128 * g:128 * g + 128] for g in range(2)]
    cg = [cc[:, 128 * g:128 * g + 128] for g in range(2)]
    cb2 = [_mm_nt(c, jnp.concatenate([b, b], axis=0)) for b, c in zip(bg, cg)]
    lm = [_decay_packed(la[:, sl]) for sl in sls]
    sp = [st[sl, :] for sl in sls]
    ys = [_mm(cg[p // 2], sp[p]) * ecum[:, sls[p]] for p in pairs]
    ys = [ys[p] + _mm(cb2[p // 2] * lm[p], _bd(xd[:, sls[p]])) for p in pairs]
    new = [sp[p] * cdec[:, sls[p]] + _mm_tn(bg[p // 2], xd[:, sls[p]] * dte[:, sls[p]]) for p in pairs]
    y = jnp.concatenate(ys, axis=1) + dskip * xc
    yg = y * _silu(z)
    out = jnp.concatenate([_unit_rms(yg[:, 0:256]), _unit_rms(yg[:, 256:512])], axis=1) * nw
    return (out,), [xr[CHUNK - 8:, :], br[CHUNK - 8:, :], cr[CHUNK - 8:, :], jnp.concatenate(new, axis=0)]


def _f_layer0(tabs, consts, xs, xtabs, states):
    (y_ret,), st_ret = _f_ret(tabs[:4], [], xs[:4], xtabs, states[:1])
    (y_ssd,), st_ssd = _f_ssd(tabs[4:], consts, xs[4:], [], states[1:])
    return (jnp.concatenate([y_ret, y_ssd], axis=1),), st_ret + st_ssd


def _f_gdn(tabs, consts, xs, xtabs, states):
    conv_w, p_alog, p_dtb, nw = consts
    qr, kr, vr, z, ba = xs
    tq, tk, tv, st = states
    qc = _silu(_conv(qr, tq, conv_w[:, 0:768]))
    kc = _silu(_conv(kr, tk, conv_w[:, 768:1536]))
    vc = _silu(_conv(vr, tv, conv_w[:, 1536:2304]))
    gl = -jnp.exp(p_alog) * jax.nn.softplus(ba + p_dtb)
    bl = jax.nn.sigmoid(ba)
    gcum = _tri_cum(gl)
    left128 = _iota((CHUNK, 128), 1) < 64
    left256 = _iota((CHUNK, 256), 1) < 128
    r, c = _packed_rc()
    diag_blocks = (_iota((256, 256), 0) < 128) == (_iota((256, 256), 1) < 128)

    def norm2(t):
        return jnp.concatenate([_l2norm(t[:, 0:128]), _l2norm(t[:, 128:256])], axis=1)

    def pick(arr, off, left, p):
        return jnp.where(left, arr[:, off + 2 * p:off + 2 * p + 1], arr[:, off + 2 * p + 1:off + 2 * p + 2])

    pairs = range(GDN_HEADS // 2)
    sls = [slice(256 * p, 256 * p + 256) for p in pairs]
    qn = [norm2(qc[:, sl]) * (GDN_DK ** -0.5) for sl in sls]
    kn = [norm2(kc[:, sl]) for sl in sls]
    dec = [_decay_packed(pick(gl, 6, left128, p)) for p in pairs]
    g2 = [pick(gl, 6, left256, p) for p in pairs]
    gc2 = [pick(gcum, 6, left256, p) for p in pairs]
    b2 = [pick(bl, 0, left256, p) for p in pairs]
    tot = [jnp.sum(t, axis=0, keepdims=True) for t in g2]
    eg = [jnp.exp(t) for t in gc2]
    et = [jnp.exp(t - s) for t, s in zip(tot, gc2)]
    cd = [jnp.exp(t) for t in tot]
    kb = [k * b for k, b in zip(kn, b2)]
    vb = [vc[:, sl] * b for sl, b in zip(sls, b2)]
    kbd = [_bd(k) for k in kn]
    tm = _tri_inv([jnp.where(r > c, _mm_nt(a, b) * d, 0.0) for a, b, d in zip(kb, kbd, dec)])
    u = [_mm(t, _bd(v)) for t, v in zip(tm, vb)]
    w = [_mm(t, _bd(k * e)) for t, k, e in zip(tm, kb, eg)]
    attn = [_mm_nt(q, k) * d for q, k, d in zip(qn, kbd, dec)]
    sp = [st[sl, :] for sl in sls]
    vn = [a - _mm(b, s) for a, b, s in zip(u, w, sp)]
    o = [_mm(q * e, s) + _mm(a, _bd(v)) for q, e, s, a, v in zip(qn, eg, sp, attn, vn)]
    new = [s * d + jnp.where(diag_blocks, _mm_tn(k * e, v), 0.0) for s, d, k, e, v in zip(sp, cd, kn, et, vn)]
    outs = []
    for p in pairs:
        for hh in range(2):
            osl = slice(128 * hh, 128 * hh + 128)
            zsl = slice(256 * p + 128 * hh, 256 * p + 128 * hh + 128)
            outs.append(_unit_rms(o[p][:, osl]) * nw * _silu(z[:, zsl]))
    return (jnp.concatenate(outs, axis=1),), [qr[CHUNK - 8:, :], kr[CHUNK - 8:, :], vr[CHUNK - 8:, :],
                                             jnp.concatenate(new, axis=0)]


def _f_s5(tabs, consts, xs, xtabs, states):
    lam_re, lam_im, bblk, c_re, c_im, dskip, wglu, bglu = consts
    (u,) = xs
    s_re, s_im = states
    rows = u.shape[0]
    n = lam_re.shape[1]
    bu = _mm(u, bblk)
    hr, hi = bu[:, 0:n], bu[:, n:2 * n]
    row = _iota((rows, n), 0)
    h0r, h0i = s_re[0:1, :], s_im[0:1, :]
    hr = hr + jnp.where(row == 0, lam_re * h0r - lam_im * h0i, 0.0)
    hi = hi + jnp.where(row == 0, lam_re * h0i + lam_im * h0r, 0.0)
    pr, pi = lam_re, lam_im
    d = 1
    while d < rows:
        sr = jnp.where(row >= d, _shift(d, 0)(hr), 0.0)
        si = jnp.where(row >= d, _shift(d, 0)(hi), 0.0)
        hr, hi = hr + pr * sr - pi * si, hi + pr * si + pi * sr
        pr, pi = pr * pr - pi * pi, 2.0 * pr * pi
        d *= 2
    y = _mm(hr, c_re) - _mm(hi, c_im) + dskip * u
    y = jax.nn.gelu(y)
    out = y * jax.nn.sigmoid(_mm(y, wglu) + bglu)
    last_r = jnp.broadcast_to(hr[rows - 1:rows, :], (8, n))
    last_i = jnp.broadcast_to(hi[rows - 1:rows, :], (8, n))
    return (out,), [last_r, last_i]


def _full_spec(a):
    nd = a.ndim
    return pl.BlockSpec(a.shape, lambda i, _nd=nd: (0,) * _nd)


CHUNKS_PER_STEP = 4


def _chunks_per_step(f, rows, n):
    def g(tabs, consts, xs, xtabs, states):
        ys = []
        for i in range(n):
            sl = slice(rows * i, rows * (i + 1))
            (y,), states = f(tabs, consts, [t[sl] for t in xs], [t[sl] for t in xtabs], states)
            ys.append(y)
        return (jnp.concatenate(ys, axis=0),), states

    return g


def _scan_fwd(name, f, rows, tabs, consts, xs, xtabs, state_shapes, y_total, y_width, y_cb, y_alias=None):
    seq = xs[0][0].shape[0]
    per_step = math.gcd(CHUNKS_PER_STEP, seq // rows)
    f = _chunks_per_step(f, rows, per_step)
    rows = rows * per_step
    nc = seq // rows
    nt, ncst, nx, nxt, ns = len(tabs), len(consts), len(xs), len(xtabs), len(state_shapes)
    alias = y_alias is not None

    def body(*refs):
        p = 0
        tab_r = refs[p:p + nt]; p += nt
        c_r = refs[p:p + ncst]; p += ncst
        x_r = refs[p:p + nx]; p += nx
        xt_r = refs[p:p + nxt]; p += nxt
        if alias:
            p += 1
        y_ref = refs[p]; p += 1
        sv_r = refs[p:p + ns]; p += ns
        st_r = refs[p:p + ns]

        @pl.when(pl.program_id(0) == 0)
        def _():
            for s in st_r:
                s[...] = jnp.zeros(s.shape, F32)

        st = [s[...] for s in st_r]
        for r, v in zip(sv_r, st):
            r[...] = v
        (y,), new = f([r[...] for r in tab_r], [r[...] for r in c_r], [r[...].astype(F32) for r in x_r],
                      [r[...] for r in xt_r], st)
        y_ref[...] = y.astype(y_ref.dtype)
        for s, v in zip(st_r, new):
            s[...] = v

    win = [pl.BlockSpec((rows, w), lambda i, _cb=cb: (i, _cb)) for (_, w, cb) in list(xs) + list(xtabs)]
    in_specs = [_full_spec(a) for a in list(tabs) + list(consts)] + win
    args = list(tabs) + list(consts) + [a for (a, _, _) in list(xs) + list(xtabs)]
    io_alias = {}
    if alias:
        in_specs.append(pl.BlockSpec(memory_space=pl.ANY))
        io_alias = {len(args): 0}
        args.append(y_alias)
    out_shape = [jax.ShapeDtypeStruct((seq, y_total), _MXU_DTYPE)]
    out_specs = [pl.BlockSpec((rows, y_width), lambda i: (i, y_cb))]
    for (r, c) in state_shapes:
        out_shape.append(jax.ShapeDtypeStruct((nc * r, c), F32))
        out_specs.append(pl.BlockSpec((r, c), lambda i: (i, 0)))
    res = pl.pallas_call(
        body, name=name, grid=(nc,), in_specs=in_specs, out_specs=out_specs, out_shape=out_shape,
        scratch_shapes=[pltpu.VMEM(s, F32) for s in state_shapes], input_output_aliases=io_alias,
        compiler_params=pltpu.CompilerParams(dimension_semantics=("arbitrary",), vmem_limit_bytes=VMEM_LIMIT),
    )(*args)
    return res[0], list(res[1:])


def _scan_bwd(name, f, rows, tabs, consts, xs, xtabs, saved, state_shapes, dy, dx_total, dx_width, dx_cb,
              assemble, dx_alias=None):
    seq = xs[0][0].shape[0]
    per_step = math.gcd(CHUNKS_PER_STEP, seq // rows)
    f = _chunks_per_step(f, rows, per_step)
    rows = rows * per_step
    nc = seq // rows
    nt, ncst, nx, nxt, ns = len(tabs), len(consts), len(xs), len(xtabs), len(state_shapes)
    alias = dx_alias is not None

    def body(*refs):
        p = 0
        tab_r = refs[p:p + nt]; p += nt
        c_r = refs[p:p + ncst]; p += ncst
        x_r = refs[p:p + nx]; p += nx
        xt_r = refs[p:p + nxt]; p += nxt
        sv_r = refs[p:p + ns]; p += ns
        dy_ref = refs[p]; p += 1
        if alias:
            p += 1
        dx_ref = refs[p]; p += 1
        dc_r = refs[p:p + ncst]; p += ncst
        ds_r = refs[p:p + ns]

        @pl.when(pl.program_id(0) == 0)
        def _():
            for s in ds_r:
                s[...] = jnp.zeros(s.shape, F32)
            for r in dc_r:
                r[...] = jnp.zeros(r.shape, F32)

        tab_v = [r[...] for r in tab_r]
        xt_v = [r[...] for r in xt_r]

        def g(c, x, s):
            (y,), new = f(tab_v, c, x, xt_v, s)
            return y, new

        _, vjp = jax.vjp(g, [r[...] for r in c_r], [r[...].astype(F32) for r in x_r], [r[...] for r in sv_r])
        dc, dx, ds = vjp((dy_ref[...], [s[...] for s in ds_r]))
        dx_ref[...] = assemble(dx).astype(dx_ref.dtype)
        for r, v in zip(dc_r, dc):
            r[...] += v
        for s, v in zip(ds_r, ds):
            s[...] = v

    win = [pl.BlockSpec((rows, w), lambda j, _cb=cb: (nc - 1 - j, _cb)) for (_, w, cb) in list(xs) + list(xtabs)]
    in_specs = [_full_spec(a) for a in list(tabs) + list(consts)] + win
    args = list(tabs) + list(consts) + [a for (a, _, _) in list(xs) + list(xtabs)]
    for (r, c), sv in zip(state_shapes, saved):
        in_specs.append(pl.BlockSpec((r, c), lambda j: (nc - 1 - j, 0)))
        args.append(sv)
    in_specs.append(pl.BlockSpec((rows, dy[1]), lambda j: (nc - 1 - j, dy[2])))
    args.append(dy[0])
    io_alias = {}
    if alias:
        in_specs.append(pl.BlockSpec(memory_space=pl.ANY))
        io_alias = {len(args): 0}
        args.append(dx_alias)
    out_shape = [jax.ShapeDtypeStruct((seq, dx_total), _MXU_DTYPE)] +[jax.ShapeDtypeStruct(a.shape, F32) for a in consts]
    out_specs = [pl.BlockSpec((rows, dx_width), lambda j: (nc - 1 - j, dx_cb))] + [_full_spec(a) for a in consts]
    res = pl.pallas_call(
        body, name=name, grid=(nc,), in_specs=in_specs, out_specs=out_specs, out_shape=out_shape,
        scratch_shapes=[pltpu.VMEM(s, F32) for s in state_shapes], input_output_aliases=io_alias,
        compiler_params=pltpu.CompilerParams(dimension_semantics=("arbitrary",), vmem_limit_bytes=VMEM_LIMIT),
    )(*args)
    return res[0], list(res[1:])


def _tile(n, want):
    t = min(n, want)
    while n % t:
        t //= 2
    return t


MATMUL_VMEM_BUDGET = 40 * 1024 * 1024


MXU_FLOPS_PER_S = 8.5e14
HBM_BYTES_PER_S = 2.8e12
GRID_STEP_S = 0.35e-6


def _pick_tiles(m, n, k, sa, sb, so, se, whole_rows=False, reduce_rows=False, tn_fixed=None, tm_divides=None):
    best = None
    tns = {tn_fixed} if tn_fixed else ({n} if whole_rows else {_tile(n, t) for t in (4096, 2048, 1024, 512)})
    tms = {_tile(m, t) for t in (2048, 1024, 512)}
    if tm_divides:
        tms = {t for t in (1024, 512, 256) if tm_divides % t == 0 and m % t == 0}
    for tn in tns:
        for tm in tms:
            for tk in {_tile(k, t) for t in (4096, 2048, 1024, 512)}:
                at, bt, ot = tm * tk * sa, tk * tn * sb, tm * tn * so
                need = 2 * (at + bt + ot + tm * tn * se) + 2 * tm * tn * 4 + (at if sa == 4 else 0) + (bt if sb == 4 else 0)
                if need > MATMUL_VMEM_BUDGET:
                    continue
                ni, nj, nk = m // tm, n // tn, k // tk
                b_reads = ni if (reduce_rows or nk > 1) else 1
                moved = m * k * sa * nj + k * n * sb * b_reads + m * n * (so + se)
                cost = max(2 * m * n * k / MXU_FLOPS_PER_S, moved / HBM_BYTES_PER_S) + ni * nj * nk * GRID_STEP_S
                key = (cost, nk, -tm)
                if best is None or key < best[0]:
                    best = (key, (tm, tn, tk))
    assert best is not None, (m, n, k)
    return best[1]


def _matmul(name, a, b, mode, out_dtype=F32, a_pro=None, epi=None, epi_arr=None, norm_w=None, norm_x=None, slab=None):
    if mode == "nn":
        (m, k), (k2, n) = a.shape, b.shape
    elif mode == "nt":
        (m, k), (n, k2) = a.shape, b.shape
    else:
        (k, m), (k2, n) = a.shape, b.shape
    assert k == k2, (name, a.shape, b.shape)
    size = lambda t: jnp.dtype(t).itemsize
    rows_in = [] if epi is None else [epi_arr] + ([norm_x] if epi == "norm_bwd" else [])
    emit_norm = epi == "add" and norm_w is not None
    extra = sum(size(t.dtype) for t in rows_in) + (size(_MXU_DTYPE) if emit_norm else 0)
    if slab is None:
        tm, tn, tk = _pick_tiles(m, n, k, size(a.dtype), size(b.dtype), size(out_dtype), extra,
                                 whole_rows=norm_w is not None, reduce_rows=mode == "tn")
    else:
        prev_slab, slab_rows, first_row, shard_rows = slab
        assert mode == "tn" and epi is None and n % LANES == 0, name
        tm, tn, tk = _pick_tiles(m, n, k, size(a.dtype), size(b.dtype), size(out_dtype), extra, reduce_rows=True,
                                 tn_fixed=LANES, tm_divides=math.gcd(shard_rows or m, first_row or m))
    nk = k // tk
    ca, cb = {"nn": (1, 0), "nt": (1, 1), "tn": (0, 0)}[mode]
    n_in = 2 + len(rows_in) + (norm_w is not None) + (slab is not None and slab[0] is not None)
    n_out = 2 if (emit_norm or epi == "norm_bwd") else 1

    def body(*refs):
        refs = list(refs)
        acc = refs.pop() if nk > 1 else None
        a_ref, b_ref = refs[0], refs[1]
        e_ref = refs[2] if epi is not None else None
        x_ref = refs[3] if epi == "norm_bwd" else None
        w_ref = refs[n_in - 1] if norm_w is not None else None
        o_ref = refs[n_in]
        o2_ref = refs[n_in + 1] if n_out == 2 else None
        kk = pl.program_id(2)

        if epi == "norm_bwd":
            @pl.when((pl.program_id(1) == 0) & (kk == 0))
            def _():
                o2_ref[...] = jnp.zeros(o2_ref.shape, F32)

        av = a_ref[...]
        if a_pro == "relu2":
            r = jnp.maximum(av, 0.0)
            av = r * r
        part = _dg(_lo(av), _lo(b_ref[...]), ca, cb)

        def finish(r):
            if epi == "add":
                r = r + e_ref[...]
                if emit_norm:
                    o2_ref[...] = (_unit_rms(r) * w_ref[...]).astype(_MXU_DTYPE)
            elif epi == "drelu2":
                r = r * (2.0 * jnp.maximum(e_ref[...], 0.0))
            elif epi == "norm_bwd":
                xv = x_ref[...]
                rstd = lax.rsqrt(jnp.mean(xv * xv, axis=-1, keepdims=True) + EPS)
                xh = xv * rstd
                g = r * w_ref[...]
                o2_ref[...] += jnp.sum(r * xh, axis=0, keepdims=True)
                r = e_ref[...] + rstd * (g - xh * jnp.mean(g * xh, axis=-1, keepdims=True))
            o_ref[...] = r.astype(out_dtype).reshape(o_ref.shape)

        if nk == 1:
            finish(part)
        else:
            @pl.when(kk == 0)
            def _():
                acc[...] = part

            @pl.when(kk > 0)
            def _():
                acc[...] += part

            @pl.when(kk == nk - 1)
            def _():
                finish(acc[...])

    if mode == "tn":
        a_spec = pl.BlockSpec((tk, tm), lambda j, i, kk: (kk, i))
    else:
        a_spec = pl.BlockSpec((tm, tk), lambda j, i, kk: (i, kk))
    if mode == "nt":
        b_spec = pl.BlockSpec((tn, tk), lambda j, i, kk: (j, kk))
    else:
        b_spec = pl.BlockSpec((tk, tn), lambda j, i, kk: (kk, j))
    o_spec = pl.BlockSpec((tm, tn), lambda j, i, kk: (i, j))
    vec_spec = pl.BlockSpec((1, tn), lambda j, i, kk: (0, j))
    in_specs, args = [a_spec, b_spec] + [o_spec] * len(rows_in), [a, b] + rows_in
    if norm_w is not None:
        in_specs.append(vec_spec)
        args.append(norm_w)
    out_specs, out_shape = [o_spec], [jax.ShapeDtypeStruct((m, n), out_dtype)]
    io_alias = {}
    if slab is not None:
        first_blk = first_row // tm
        if shard_rows is None:
            out_specs = [pl.BlockSpec((1, tm, tn), lambda j, i, kk: (j, first_blk + i, 0))]
        else:
            per = shard_rows // tm
            out_specs = [pl.BlockSpec((1, tm, tn), lambda j, i, kk: (i // per, first_blk + i % per, 0))]
        out_shape = [jax.ShapeDtypeStruct((4, slab_rows, LANES), out_dtype)]
        if prev_slab is not None:
            in_specs.append(pl.BlockSpec(memory_space=pl.ANY))
            io_alias = {len(args): 0}
            args.append(prev_slab)
    if emit_norm:
        out_specs.append(o_spec)
        out_shape.append(jax.ShapeDtypeStruct((m, n), _MXU_DTYPE))
    elif epi == "norm_bwd":
        out_specs.append(vec_spec)
        out_shape.append(jax.ShapeDtypeStruct((1, n), F32))
    sem = ("parallel", "arbitrary" if epi == "norm_bwd" else "parallel", "arbitrary")
    res = pl.pallas_call(
        body, name=name, grid=(n // tn, m // tm, nk), in_specs=in_specs, out_specs=out_specs, out_shape=out_shape,
        scratch_shapes=[pltpu.VMEM((tm, tn), F32)] if nk > 1 else [], input_output_aliases=io_alias,
        compiler_params=pltpu.CompilerParams(dimension_semantics=sem, vmem_limit_bytes=VMEM_LIMIT),
    )(*args)
    return res[0] if n_out == 1 else res


ROW_TILE = 512


def _rmsnorm_fwd(name, x, w):
    seq, d = x.shape
    tr = _tile(seq, ROW_TILE)

    def body(x_ref, w_ref, o_ref):
        xv = x_ref[...]
        o_ref[...] = (_unit_rms(xv) * w_ref[...]).astype(_MXU_DTYPE)

    return pl.pallas_call(
        body, name=name, grid=(seq // tr,),
        in_specs=[pl.BlockSpec((tr, d), lambda i: (i, 0)), pl.BlockSpec((1, d), lambda i: (0, 0))],
        out_specs=pl.BlockSpec((tr, d), lambda i: (i, 0)), out_shape=jax.ShapeDtypeStruct((seq, d), _MXU_DTYPE),
        compiler_params=pltpu.CompilerParams(dimension_semantics=("parallel",), vmem_limit_bytes=VMEM_LIMIT),
    )(x, w)


def _loss_head(name, x, w, target):
    seq, d = x.shape
    tr = _tile(seq, ROW_TILE)

    def body(x_ref, w_ref, t_ref, loss_ref, dx_ref, dw_ref):
        @pl.when(pl.program_id(0) == 0)
        def _():
            dw_ref[...] = jnp.zeros(dw_ref.shape, F32)
            loss_ref[...] = jnp.zeros(loss_ref.shape, F32)

        xv = x_ref[...]
        rstd = lax.rsqrt(jnp.mean(xv * xv, axis=-1, keepdims=True) + EPS)
        xh = xv * rstd
        err = xh * w_ref[...] - t_ref[...]
        per_row = jnp.mean(err * err, axis=-1, keepdims=True)
        loss_ref[...] += 0.5 * jnp.sum(per_row, axis=0, keepdims=True)
        dy = err * (1.0 / d)
        g = dy * w_ref[...]
        dx_ref[...] = rstd * (g - xh * jnp.mean(g * xh, axis=-1, keepdims=True))
        dw_ref[...] += jnp.sum(dy * xh, axis=0, keepdims=True)

    row = pl.BlockSpec((tr, d), lambda i: (i, 0))
    vec = pl.BlockSpec((1, d), lambda i: (0, 0))
    one = pl.BlockSpec((1, 1), lambda i: (0, 0))
    return pl.pallas_call(
        body, name=name, grid=(seq // tr,), in_specs=[row, vec, row], out_specs=[one, row, vec],
        out_shape=[jax.ShapeDtypeStruct((1, 1), F32), jax.ShapeDtypeStruct((seq, d), F32),
                   jax.ShapeDtypeStruct((1, d), F32)],
        compiler_params=pltpu.CompilerParams(dimension_semantics=("arbitrary",), vmem_limit_bytes=VMEM_LIMIT),
    )(x, w, target)


SLAB_TILE_ROWS = 1024


def _slab_tile(rows, cap=SLAB_TILE_ROWS):
    step = 16 if rows % 16 == 0 else 8
    return max(t for t in range(step, min(rows, cap) + 1, step) if rows % t == 0)


def _adamw(name, w, g, m, v):
    rows, cols = w.shape
    tr = _slab_tile(rows, SLAB_TILE_ROWS // 2) if rows % 8 == 0 else rows

    def body(w_ref, g_ref, m_ref, v_ref, d_ref, nm_ref, nv_ref):
        gv = g_ref[...]
        nm = ADAM_B1 * m_ref[...] + (1.0 - ADAM_B1) * gv
        nv = ADAM_B2 * v_ref[...] + (1.0 - ADAM_B2) * (gv * gv)
        m_hat = nm / (1.0 - ADAM_B1 ** ADAM_STEP)
        v_hat = nv / (1.0 - ADAM_B2 ** ADAM_STEP)
        d_ref[...] = -ADAM_LR * (m_hat / (jnp.sqrt(v_hat) + ADAM_EPS) + ADAM_WD * w_ref[...])
        nm_ref[...] = nm
        nv_ref[...] = nv

    spec = pl.BlockSpec((tr, cols), lambda i: (i, 0))
    sds = jax.ShapeDtypeStruct(w.shape, F32)
    return pl.pallas_call(
        body, name=name, grid=(rows // tr,), in_specs=[spec] * 4, out_specs=[spec] * 3, out_shape=[sds] * 3,
        compiler_params=pltpu.CompilerParams(dimension_semantics=("parallel",), vmem_limit_bytes=VMEM_LIMIT),
    )(w, g, m, v)


def _place_rows(name, slab, tail, first_row):
    nsec, rows, _ = tail.shape
    tr = math.gcd(rows, first_row)
    tr = _slab_tile(tr, SLAB_TILE_ROWS // 2)
    first_blk = first_row // tr

    def body(t_ref, s_ref, o_ref):
        o_ref[...] = t_ref[...]

    return pl.pallas_call(
        body, name=name, grid=(nsec, rows // tr),
        in_specs=[pl.BlockSpec((1, tr, LANES), lambda s, i: (s, i, 0)), pl.BlockSpec(memory_space=pl.ANY)],
        out_specs=pl.BlockSpec((1, tr, LANES), lambda s, i: (s, first_blk + i, 0)),
        out_shape=jax.ShapeDtypeStruct(slab.shape, slab.dtype), input_output_aliases={1: 0},
        compiler_params=pltpu.CompilerParams(dimension_semantics=("parallel", "parallel"),
                                             vmem_limit_bytes=VMEM_LIMIT),
    )(tail, slab)


WIRE_DTYPE = jnp.bfloat16


def _add_halves(name, g, t1, c):
    nsec, rows, _ = g.shape
    rh = rows // 2
    tr = _slab_tile(rh)
    nb = rh // tr

    def body(c_ref, g_ref, t_ref, o_ref):
        o_ref[...] = (g_ref[...] + t_ref[...]).astype(o_ref.dtype)

    gs = pltpu.PrefetchScalarGridSpec(
        num_scalar_prefetch=1, grid=(nsec, nb),
        in_specs=[pl.BlockSpec((1, tr, LANES), lambda s, i, c_ref: (s, c_ref[0] * nb + i, 0)),
                  pl.BlockSpec((1, tr, LANES), lambda s, i, c_ref: (s, i, 0))],
        out_specs=pl.BlockSpec((1, tr, LANES), lambda s, i, c_ref: (s, i, 0)))
    return pl.pallas_call(
        body, name=name, grid_spec=gs, out_shape=jax.ShapeDtypeStruct((nsec, rh, LANES), WIRE_DTYPE),
        compiler_params=pltpu.CompilerParams(dimension_semantics=("parallel", "parallel"),
                                             vmem_limit_bytes=VMEM_LIMIT),
    )(c, g, t1)


ANY = pl.BlockSpec(memory_space=pl.ANY)


def _place():
    return lax.axis_index("x"), lax.axis_index("y"), lax.axis_index("c")


def _all_gather_shards(name, slab):
    rows = slab.shape[0]
    rh = rows // 2
    rq = rh // 2

    def body(x_ref, out_ref, send_sems, recv_sems):
        x, y, c = _place()
        me, sibling = (x, y, c), (x, y, 1 - c)
        xn, yn, dg = (1 - x, y), (x, 1 - y), (1 - x, 1 - y)

        def piece(chip, core, q):
            return out_ref.at[2 * chip[0] + chip[1], pl.ds(core * rh + q * rq, rq), :]

        def copy(k, chip, core, q, to, src=None):
            return pltpu.make_async_remote_copy(
                src_ref=piece(chip, core, q) if src is None else src, dst_ref=piece(chip, core, q),
                send_sem=send_sems.at[k], recv_sem=recv_sems.at[k], device_id=to, device_id_type=MESH)

        own = [x_ref.at[pl.ds(c * rh + q * rq, rq), :] for q in range(2)]
        sends = [copy(0, (x, y), c, 0, (*xn, c), src=own[0]), copy(1, (x, y), c, 1, (*xn, c), src=own[1]),
                 copy(2, (x, y), c, 0, (*yn, c), src=own[0]), copy(3, (x, y), c, 1, (*yn, c), src=own[1])]
        for cp in sends:
            cp.start()
        landed = [(0, xn, 0), (3, yn, 1), (1, xn, 1), (2, yn, 0), (4, dg, 0), (5, dg, 1)]
        onward = {0: (4, (*yn, c)), 3: (5, (*xn, c))}
        for i, (k, chip, q) in enumerate(landed):
            copy(k, chip, c, q, me).wait_recv()
            if k in onward:
                fk, to = onward[k]
                sends.append(copy(fk, chip, c, q, to))
                sends[-1].start()
            sends.append(copy(6 + i, chip, c, q, sibling))
            sends[-1].start()
        for i, (k, chip, q) in enumerate(landed):
            copy(6 + i, chip, 1 - c, q, me).wait_recv()
        for cp in sends:
            cp.wait_send()

    got = pl.pallas_call(
        body, name=name, in_specs=[ANY], out_specs=ANY,
        out_shape=jax.ShapeDtypeStruct((4, rows, LANES), slab.dtype),
        scratch_shapes=[pltpu.SemaphoreType.DMA((12,)), pltpu.SemaphoreType.DMA((12,))],
    )(slab)
    return lax.dynamic_update_slice(got, slab[None], (2 * lax.axis_index("x") + lax.axis_index("y"), 0, 0))


def _swap_halves(name, g):
    nsec, rows, _ = g.shape
    rh = rows // 2

    def body(g_ref, t_ref, send_sem, recv_sem):
        x, y, c = _place()
        cp = pltpu.make_async_remote_copy(
            src_ref=g_ref.at[:, pl.ds((1 - c) * rh, rh), :], dst_ref=t_ref, send_sem=send_sem, recv_sem=recv_sem,
            device_id=(x, y, 1 - c), device_id_type=MESH)
        cp.start()
        cp.wait()

    return pl.pallas_call(
        body, name=name, in_specs=[ANY], out_specs=ANY, out_shape=jax.ShapeDtypeStruct((nsec, rh, LANES), F32),
        scratch_shapes=[pltpu.SemaphoreType.DMA, pltpu.SemaphoreType.DMA],
    )(g)


def _exchange_stage1(name, p):
    _, rh, _ = p.shape
    rq = rh // 2

    def body(p_ref, fx_ref, fy_ref, send_sems, recv_sems):
        x, y, c = _place()
        to_x = pltpu.make_async_remote_copy(
            src_ref=p_ref.at[pl.ds(2 * (1 - x), 2), pl.ds(0, rq), :], dst_ref=fx_ref, send_sem=send_sems.at[0],
            recv_sem=recv_sems.at[0], device_id=(1 - x, y, c), device_id_type=MESH)
        to_y = [pltpu.make_async_remote_copy(
            src_ref=p_ref.at[2 * sx + (1 - y), pl.ds(rq, rq), :], dst_ref=fy_ref.at[sx], send_sem=send_sems.at[1 + sx],
            recv_sem=recv_sems.at[1 + sx], device_id=(x, 1 - y, c), device_id_type=MESH) for sx in range(2)]
        for cp in [to_x] + to_y:
            cp.start()
        for cp in [to_x] + to_y:
            cp.wait_recv()
        for cp in [to_x] + to_y:
            cp.wait_send()

    sds = jax.ShapeDtypeStruct((2, rq, LANES), p.dtype)
    return pl.pallas_call(
        body, name=name, in_specs=[ANY], out_specs=[ANY, ANY], out_shape=[sds, sds],
        scratch_shapes=[pltpu.SemaphoreType.DMA((3,)), pltpu.SemaphoreType.DMA((3,))],
    )(p)


def _exchange_add1(name, p, from_x, from_y, place):
    _, rh, _ = p.shape
    rq = rh // 2
    tr = _slab_tile(rq)
    nb = rq // tr

    def body(xy_ref, pa_s, pa_k, pb_s, pb_k, fx_s, fx_k, fy_s, fy_k, sa, ka, sb, kb):
        for mine, theirs, out in ((pa_s, fx_s, sa), (pa_k, fx_k, ka), (pb_s, fy_s, sb), (pb_k, fy_k, kb)):
            out[...] = (mine[0].astype(F32) + theirs[0].astype(F32)).astype(out.dtype)

    blk = lambda fn: pl.BlockSpec((1, tr, LANES), fn)
    gs = pltpu.PrefetchScalarGridSpec(
        num_scalar_prefetch=1, grid=(nb,),
        in_specs=[blk(lambda i, xy: (2 * xy[0] + 1 - xy[1], i, 0)), blk(lambda i, xy: (2 * xy[0] + xy[1], i, 0)),
                  blk(lambda i, xy: (2 * (1 - xy[0]) + xy[1], nb + i, 0)), blk(lambda i, xy: (2 * xy[0] + xy[1], nb + i, 0)),
                  blk(lambda i, xy: (1 - xy[1], i, 0)), blk(lambda i, xy: (xy[1], i, 0)),
                  blk(lambda i, xy: (1 - xy[0], i, 0)), blk(lambda i, xy: (xy[0], i, 0))],
        out_specs=[pl.BlockSpec((tr, LANES), lambda i, xy: (i, 0))] * 4)
    sds = jax.ShapeDtypeStruct((rq, LANES), p.dtype)
    return pl.pallas_call(
        body, name=name, grid_spec=gs, out_shape=[sds] * 4,
        compiler_params=pltpu.CompilerParams(dimension_semantics=("parallel",), vmem_limit_bytes=VMEM_LIMIT),
    )(place, p, p, p, p, from_x, from_x, from_y, from_y)


def _exchange_stage2(name, send_a, send_b):
    def body(a_ref, b_ref, fa_ref, fb_ref, send_sems, recv_sems):
        x, y, c = _place()
        cps = [pltpu.make_async_remote_copy(src_ref=a_ref, dst_ref=fa_ref, send_sem=send_sems.at[0],
                                            recv_sem=recv_sems.at[0], device_id=(x, 1 - y, c), device_id_type=MESH),
               pltpu.make_async_remote_copy(src_ref=b_ref, dst_ref=fb_ref, send_sem=send_sems.at[1],
                                            recv_sem=recv_sems.at[1], device_id=(1 - x, y, c), device_id_type=MESH)]
        for cp in cps:
            cp.start()
        for cp in cps:
            cp.wait_recv()
        for cp in cps:
            cp.wait_send()

    sds = jax.ShapeDtypeStruct(send_a.shape, send_a.dtype)
    return pl.pallas_call(
        body, name=name, in_specs=[ANY, ANY], out_specs=[ANY, ANY], out_shape=[sds, sds],
        scratch_shapes=[pltpu.SemaphoreType.DMA((2,)), pltpu.SemaphoreType.DMA((2,))],
    )(send_a, send_b)


def _exchange_add2(name, keep_a, got_a, keep_b, got_b, c):
    rq = keep_a.shape[0]
    tr = _slab_tile(rq)

    def body(c_ref, ka, ga, kb, gb, o_ref):
        o_ref[0] = ka[...].astype(F32) + ga[...].astype(F32)
        o_ref[1] = kb[...].astype(F32) + gb[...].astype(F32)

    spec = pl.BlockSpec((tr, LANES), lambda i, c_ref: (i, 0))
    gs = pltpu.PrefetchScalarGridSpec(
        num_scalar_prefetch=1, grid=(rq // tr,), in_specs=[spec] * 4,
        out_specs=pl.BlockSpec((2, tr, LANES), lambda i, c_ref: (c_ref[0], i, 0)))
    out = pl.pallas_call(
        body, name=name, grid_spec=gs, out_shape=jax.ShapeDtypeStruct((4, rq, LANES), F32),
        compiler_params=pltpu.CompilerParams(dimension_semantics=("parallel",), vmem_limit_bytes=VMEM_LIMIT),
    )(c, keep_a, got_a, keep_b, got_b)
    return out.reshape(4 * rq, LANES)


def _join_halves(name, full):
    rh = full.shape[0] // 2

    def body(in_ref, o_ref, send_sem, recv_sem):
        x, y, c = _place()
        cp = pltpu.make_async_remote_copy(
            src_ref=in_ref.at[pl.ds(c * rh, rh), :], dst_ref=o_ref.at[pl.ds(c * rh, rh), :], send_sem=send_sem,
            recv_sem=recv_sem, device_id=(x, y, 1 - c), device_id_type=MESH)
        cp.start()
        pltpu.make_async_remote_copy(
            src_ref=in_ref.at[pl.ds(c * rh, rh), :], dst_ref=o_ref.at[pl.ds((1 - c) * rh, rh), :], send_sem=send_sem,
            recv_sem=recv_sem, device_id=(x, y, 1 - c), device_id_type=MESH).wait_recv()
        cp.wait_send()

    return pl.pallas_call(
        body, name=name, in_specs=[ANY], out_specs=ANY, out_shape=jax.ShapeDtypeStruct(full.shape, full.dtype),
        input_output_aliases={0: 0}, scratch_shapes=[pltpu.SemaphoreType.DMA, pltpu.SemaphoreType.DMA],
    )(full)


def _rows_of(n):
    return -(-n // LANES)


SLAB_ROW_ALIGN = 512


def _flat_rows(arrays, dtype):
    parts = []
    for a in arrays:
        flat = a.reshape(-1).astype(dtype)
        parts.append(jnp.pad(flat, (0, _rows_of(flat.size) * LANES - flat.size)))
    return jnp.concatenate(parts).reshape(-1, LANES)


def _align_rows(slab):
    rows = slab.shape[0]
    return jnp.pad(slab, ((0, -(-rows // SLAB_ROW_ALIGN) * SLAB_ROW_ALIGN - rows), (0, 0)))


def _pack(arrays, dtype):
    return _align_rows(_flat_rows(arrays, dtype))


def _unpack(slab, shapes):
    out, r = [], 0
    for shp in shapes:
        n = math.prod(shp)
        out.append(slab[r:r + _rows_of(n)].reshape(-1)[:n].reshape(shp))
        r += _rows_of(n)
    return out


def _unpack_gathered(g, shapes, kinds):
    out, r = [], 0
    for shp, kind in zip(shapes, kinds):
        n = math.prod(shp)
        blk = g[:, r:r + _rows_of(n)].reshape(4, -1)[:, :n].reshape((4,) + tuple(shp))
        r += _rows_of(n)
        if kind == "col":
            out.append(jnp.moveaxis(blk, 0, 1).reshape(shp[0], 4 * shp[1]))
        else:
            out.append(blk.reshape(4 * shp[0], shp[1]))
    return out


def _sections(g, kind, local_shape):
    if kind == "col":
        blocks = jnp.moveaxis(g.reshape(local_shape[0], 4, local_shape[1]), 1, 0)
    elif kind == "row":
        blocks = g.reshape((4,) + tuple(local_shape))
    else:
        blocks = jnp.broadcast_to(g, (4,) + tuple(g.shape))
    flat = blocks.reshape(4, -1)
    rows = _rows_of(flat.shape[1])
    return jnp.pad(flat, ((0, 0), (0, rows * LANES - flat.shape[1]))).reshape(4, rows, LANES)


def _rotary_tables(seq):
    half = RET_DK // 2
    pos = jnp.arange(seq, dtype=F32)
    inv = ROPE_THETA ** (-jnp.arange(half, dtype=F32) / half)
    ang = pos[:, None] * inv[None, :]
    cos, sin = jnp.cos(ang), jnp.sin(ang)
    return jnp.concatenate([cos, cos], axis=1), jnp.concatenate([-sin, sin], axis=1)


def _retention_tables():
    log_gamma = jnp.log(1.0 - 2.0 ** (-5.0 - jnp.arange(RET_HEADS, dtype=F32)))
    idx = jnp.arange(CHUNK, dtype=F32)
    diff = idx[:, None] - idx[None, :]
    dmask = jnp.exp(jnp.where((diff >= 0)[None], log_gamma[:, None, None] * diff[None], -jnp.inf))
    kdec = jnp.exp(log_gamma[None, :] * (CHUNK - 1.0 - idx)[:, None])
    qdec = jnp.exp(log_gamma[None, :] * (idx + 1.0)[:, None])
    cdec = jnp.exp(log_gamma * CHUNK)[None, :]
    lanes = lambda t: jnp.repeat(t, RET_DK, axis=1)
    return dmask.reshape(RET_HEADS * CHUNK, CHUNK), lanes(kdec), lanes(qdec), lanes(cdec)


def _s5_prep(a_re, a_im, log_step, b_re, b_im, c_re, c_im):
    g, n, c = S5_GROUPS, S5_STATE, S5_GROUP
    lam = lax.complex(a_re, a_im)
    step = jnp.exp(log_step)[:, None]
    lam_bar = jnp.exp(lam * step)
    b_bar = ((lam_bar - 1.0) / lam)[..., None] * lax.complex(b_re, b_im)
    eye = jnp.eye(g, dtype=F32)
    bb_re = (jnp.real(b_bar).transpose(0, 2, 1)[:, :, None, :] * eye[:, None, :, None]).reshape(g * c, g * n)
    bb_im = (jnp.imag(b_bar).transpose(0, 2, 1)[:, :, None, :] * eye[:, None, :, None]).reshape(g * c, g * n)
    cc_re = (c_re.transpose(0, 2, 1)[:, :, None, :] * eye[:, None, :, None]).reshape(g * n, g * c)
    cc_im = (c_im.transpose(0, 2, 1)[:, :, None, :] * eye[:, None, :, None]).reshape(g * n, g * c)
    return (jnp.real(lam_bar).reshape(1, g * n), jnp.imag(lam_bar).reshape(1, g * n),
            jnp.concatenate([bb_re, bb_im], axis=1), cc_re, cc_im)


def kernel(x, l0_norm_mix, l0_w_in, ssd_conv_w, ssd_conv_b, ssd_dt_bias, ssd_A_log, ssd_D, ssd_norm_w, l0_w_out, l0_norm_mlp, l0_w_up, l0_w_down, l1_norm_mix, l1_w_in, gdn_conv_w, gdn_A_log, gdn_dt_bias, gdn_norm_w, s5_A_re, s5_A_im, s5_log_step, s5_B_re, s5_B_im, s5_C_re, s5_C_im, s5_D, s5_w_glu, s5_b_glu, l1_w_out, l1_norm_mlp, l1_w_up, l1_w_down, final_norm, loss_target, m_l0_norm_mix, m_l0_w_in, m_ssd_conv_w, m_ssd_conv_b, m_ssd_dt_bias, m_ssd_A_log, m_ssd_D, m_ssd_norm_w, m_l0_w_out, m_l0_norm_mlp, m_l0_w_up, m_l0_w_down, m_l1_norm_mix, m_l1_w_in, m_gdn_conv_w, m_gdn_A_log, m_gdn_dt_bias, m_gdn_norm_w, m_s5_A_re, m_s5_A_im, m_s5_log_step, m_s5_B_re, m_s5_B_im, m_s5_C_re, m_s5_C_im, m_s5_D, m_s5_w_glu, m_s5_b_glu, m_l1_w_out, m_l1_norm_mlp, m_l1_w_up, m_l1_w_down, m_final_norm, v_l0_norm_mix, v_l0_w_in, v_ssd_conv_w, v_ssd_conv_b, v_ssd_dt_bias, v_ssd_A_log, v_ssd_D, v_ssd_norm_w, v_l0_w_out, v_l0_norm_mlp, v_l0_w_up, v_l0_w_down, v_l1_norm_mix, v_l1_w_in, v_gdn_conv_w, v_gdn_A_log, v_gdn_dt_bias, v_gdn_norm_w, v_s5_A_re, v_s5_A_im, v_s5_log_step, v_s5_B_re, v_s5_B_im, v_s5_C_re, v_s5_C_im, v_s5_D, v_s5_w_glu, v_s5_b_glu, v_l1_w_out, v_l1_norm_mlp, v_l1_w_up, v_l1_w_down, v_final_norm):
    given = dict(locals())
    names = [n for n, _ in PARAMS]
    kinds = dict(PARAMS)
    w = {n: given[n] for n in names}
    seq = x.shape[1]
    x0 = x.reshape(seq, D_MODEL)
    target = loss_target.reshape(seq, D_MODEL)

    gb = _all_gather_shards("gather_weights", _pack([w[n] for n in GATHER_BF16], _MXU_DTYPE))
    full = dict(zip(GATHER_BF16, _unpack_gathered(gb, [w[n].shape for n in GATHER_BF16],
                                                  [kinds[n] for n in GATHER_BF16])))
    gf = _all_gather_shards("gather_conv", _pack([w[n] for n in GATHER_F32], F32))
    full.update(zip(GATHER_F32, _unpack_gathered(gf, [w[n].shape for n in GATHER_F32],
                                                 [kinds[n] for n in GATHER_F32])))
    in0 = full["l0_w_in"].shape[1]
    w_in0 = jnp.pad(full["l0_w_in"], ((0, 0), (0, IN0_PAD - in0)))
    wi1 = full["l1_w_in"]
    in1 = wi1.shape[1]
    w_in1 = jnp.concatenate([wi1[:, :3072], wi1[:, 3084:in1], wi1[:, 3072:3084],
                             jnp.zeros((D_MODEL, IN1_PAD - in1), wi1.dtype)], axis=1)

    row = lambda a: a.reshape(1, -1)
    lanes64 = lambda a: jnp.repeat(a, SSD_HEAD_DIM).reshape(1, -1)

    h0 = _rmsnorm_fwd("norm_mix0", x0, row(w["l0_norm_mix"]))
    proj0 = _matmul("in_proj0", h0, w_in0, "nn")
    cos_t, sin_t = _rotary_tables(seq)
    ret_tabs = list(_retention_tables())
    ret_xs = [(proj0, 512, 0), (proj0, 512, 1), (proj0, 512, 2), (proj0, 512, 3)]
    ret_xt = [(cos_t, 128, 0), (sin_t, 128, 0)]
    ret_states = [(512, 128)]
    expand = jnp.repeat(jnp.eye(128, SSD_HEADS, dtype=F32), SSD_HEAD_DIM, axis=1)
    ssd_consts = [full["ssd_conv_w"], row(w["ssd_conv_b"]), lanes64(w["ssd_dt_bias"]), lanes64(w["ssd_A_log"]),
                  lanes64(w["ssd_D"]), row(w["ssd_norm_w"])]
    ssd_xs = [(proj0, 512, 4), (proj0, 512, 5), (proj0, 256, 12), (proj0, 256, 13), (proj0, 128, 28)]
    ssd_states = [(8, 512), (8, 256), (8, 256), (512, 128)]
    l0_tabs, l0_xs, l0_states = ret_tabs + [expand], ret_xs + ssd_xs, ret_states + ssd_states
    mixed0, l0_saved = _scan_fwd("mix0_fwd", _f_layer0, CHUNK, l0_tabs, ssd_consts, l0_xs, ret_xt, l0_states,
                                 D_MODEL, D_MODEL, 0)
    x1, h1 = _matmul("out_proj0", mixed0, full["l0_w_out"], "nn", epi="add", epi_arr=x0, norm_w=row(w["l0_norm_mlp"]))
    u0 = _matmul("up0", h1, full["l0_w_up"], "nn", out_dtype=_MXU_DTYPE)
    x2, h2 = _matmul("down0", u0, full["l0_w_down"], "nn", a_pro="relu2", epi="add", epi_arr=x1,
                     norm_w=row(w["l1_norm_mix"]))

    proj1 = _matmul("in_proj1", h2, w_in1, "nn")
    p_alog = jnp.zeros((1, 128), F32).at[0, 6:12].set(w["gdn_A_log"])
    p_dtb = jnp.zeros((1, 128), F32).at[0, 6:12].set(w["gdn_dt_bias"])
    gdn_consts = [full["gdn_conv_w"], p_alog, p_dtb, row(w["gdn_norm_w"])]
    gdn_xs = [(proj1, 768, 0), (proj1, 768, 1), (proj1, 768, 2), (proj1, 768, 3), (proj1, 128, 26)]
    gdn_states = [(8, 768), (8, 768), (8, 768), (768, 256)]
    mixed1, gdn_saved = _scan_fwd("gdn_fwd", _f_gdn, CHUNK, [], gdn_consts, gdn_xs, [], gdn_states, D_MODEL, 768, 0)
    s5_args = (w["s5_A_re"], w["s5_A_im"], w["s5_log_step"], w["s5_B_re"], w["s5_B_im"], w["s5_C_re"], w["s5_C_im"])
    (lam_re, lam_im, bblk, cc_re, cc_im), s5_prep_vjp = jax.vjp(_s5_prep, *s5_args)
    s5_consts = [lam_re, lam_im, bblk, cc_re, cc_im, row(w["s5_D"]), full["s5_w_glu"].astype(F32), row(w["s5_b_glu"])]
    s5_xs = [(proj1, 256, 12)]
    s5_states = [(8, 1024), (8, 1024)]
    mixed1, s5_saved = _scan_fwd("s5_fwd", _f_s5, CHUNK, [], s5_consts, s5_xs, [], s5_states, D_MODEL, 256, 3,
                                 y_alias=mixed1)
    x3, h3 = _matmul("out_proj1", mixed1, full["l1_w_out"], "nn", epi="add", epi_arr=x2, norm_w=row(w["l1_norm_mlp"]))
    u1 = _matmul("up1", h3, full["l1_w_up"], "nn", out_dtype=_MXU_DTYPE)
    x4 = _matmul("down1", u1, full["l1_w_down"], "nn", a_pro="relu2", epi="add", epi_arr=x3)

    loss_part, dx4, d_final = _loss_head("loss_head", x4, row(w["final_norm"]), target)
    loss = lax.psum(loss_part[0, 0], ("x", "y", "c"))
    grads = {"final_norm": d_final.reshape(-1)}
    small = SMALL_SHARDED + tuple(n for n in names if kinds[n] == "rep")
    order = LARGE + small
    first_row, slab_rows = {}, 0
    for n in order:
        first_row[n] = slab_rows
        slab_rows += _rows_of(math.prod(w[n].shape))
    slab_rows = -(-slab_rows // SLAB_ROW_ALIGN) * SLAB_ROW_ALIGN

    du1 = _matmul("down1_dx", dx4, full["l1_w_down"], "nt", out_dtype=_MXU_DTYPE, epi="drelu2", epi_arr=u1)
    gslab = _matmul("down1_dw", u1, dx4, "tn", a_pro="relu2", slab=(None, slab_rows, first_row["l1_w_down"], 1024))
    gslab = _matmul("up1_dw", h3, du1, "tn", slab=(gslab, slab_rows, first_row["l1_w_up"], None))
    dx3, dwn = _matmul("up1_dx", du1, full["l1_w_up"], "nt", epi="norm_bwd", epi_arr=dx4, norm_x=x3,
                       norm_w=row(w["l1_norm_mlp"]))
    grads["l1_norm_mlp"] = dwn.reshape(-1)
    grads["l1_w_out"] = _matmul("out_proj1_dw", mixed1, dx3, "tn")
    dmixed1 = _matmul("out_proj1_dx", dx3, full["l1_w_out"], "nt")

    def gdn_assemble(dx):
        dq, dk, dv, dz, dba = dx
        zeros = lambda n: jnp.zeros((dq.shape[0], n), F32)
        return jnp.concatenate([dq, dk, dv, dz, zeros(256), dba, zeros(IN1_PAD - 3456)], axis=1)

    dproj1, gdn_dc = _scan_bwd("gdn_bwd", _f_gdn, CHUNK, [], gdn_consts, gdn_xs, [], gdn_saved, gdn_states,
                               (dmixed1, 768, 0), IN1_PAD, IN1_PAD, 0, gdn_assemble)
    dproj1, s5_dc = _scan_bwd("s5_bwd", _f_s5, CHUNK, [], s5_consts, s5_xs, [], s5_saved, s5_states,
                              (dmixed1, 256, 3), IN1_PAD, 256, 12, lambda dx: dx[0], dx_alias=dproj1)
    grads["gdn_conv_w"] = gdn_dc[0]
    grads["gdn_A_log"] = gdn_dc[1][0, 6:12]
    grads["gdn_dt_bias"] = gdn_dc[2][0, 6:12]
    grads["gdn_norm_w"] = gdn_dc[3].reshape(-1)
    s5_pg = s5_prep_vjp(tuple(s5_dc[:5]))
    for n, gval in zip(("s5_A_re", "s5_A_im", "s5_log_step", "s5_B_re", "s5_B_im", "s5_C_re", "s5_C_im"), s5_pg):
        grads[n] = gval
    grads["s5_D"] = s5_dc[5].reshape(-1)
    grads["s5_w_glu"] = s5_dc[6]
    grads["s5_b_glu"] = s5_dc[7].reshape(-1)
    dwi1 = _matmul("in_proj1_dw", h2, dproj1, "tn")
    grads["l1_w_in"] = jnp.concatenate([dwi1[:, :3072], dwi1[:, 3328:3340], dwi1[:, 3072:3328]], axis=1)
    dx2, dwn = _matmul("in_proj1_dx", dproj1, w_in1, "nt", epi="norm_bwd", epi_arr=dx3, norm_x=x2,
                       norm_w=row(w["l1_norm_mix"]))
    grads["l1_norm_mix"] = dwn.reshape(-1)

    du0 = _matmul("down0_dx", dx2, full["l0_w_down"], "nt", out_dtype=_MXU_DTYPE, epi="drelu2", epi_arr=u0)
    gslab = _matmul("down0_dw", u0, dx2, "tn", a_pro="relu2", slab=(gslab, slab_rows, first_row["l0_w_down"], 1024))
    gslab = _matmul("up0_dw", h1, du0, "tn", slab=(gslab, slab_rows, first_row["l0_w_up"], None))
    dx1, dwn = _matmul("up0_dx", du0, full["l0_w_up"], "nt", epi="norm_bwd", epi_arr=dx2, norm_x=x1,
                       norm_w=row(w["l0_norm_mlp"]))
    grads["l0_norm_mlp"] = dwn.reshape(-1)
    grads["l0_w_out"] = _matmul("out_proj0_dw", mixed0, dx1, "tn")
    dmixed0 = _matmul("out_proj0_dx", dx1, full["l0_w_out"], "nt")
    def l0_assemble(dx):
        return jnp.concatenate(list(dx) + [jnp.zeros((dx[0].shape[0], IN0_PAD - 3712), F32)], axis=1)

    dproj0, ssd_dc = _scan_bwd("mix0_bwd", _f_layer0, CHUNK, l0_tabs, ssd_consts, l0_xs, ret_xt, l0_saved, l0_states,
                               (dmixed0, D_MODEL, 0), IN0_PAD, IN0_PAD, 0, l0_assemble)
    heads = lambda a: a.reshape(SSD_HEADS, SSD_HEAD_DIM).sum(axis=1)
    grads["ssd_conv_w"] = ssd_dc[0]
    grads["ssd_conv_b"] = ssd_dc[1].reshape(-1)
    grads["ssd_dt_bias"] = heads(ssd_dc[2])
    grads["ssd_A_log"] = heads(ssd_dc[3])
    grads["ssd_D"] = heads(ssd_dc[4])
    grads["ssd_norm_w"] = ssd_dc[5].reshape(-1)
    grads["l0_w_in"] = _matmul("in_proj0_dw", h0, dproj0, "tn")[:, :in0]
    dx0, dwn = _matmul("in_proj0_dx", dproj0, w_in0, "nt", epi="norm_bwd", epi_arr=dx1, norm_x=x0,
                       norm_w=row(w["l0_norm_mix"]))
    grads["l0_norm_mix"] = dwn.reshape(-1)
    grad_x = dx0.reshape(x.shape)

    c_idx = lax.axis_index("c").astype(jnp.int32).reshape(1)
    tail = order[SLAB_DIRECT:]
    parts = [_sections(grads[n].reshape(_full_shape(n, w, kinds)), kinds[n], w[n].shape) for n in tail]
    parts.append(jnp.zeros((4, slab_rows - first_row[tail[0]] - sum(p.shape[1] for p in parts), LANES), F32))
    gslab = _place_rows("grads_place_tail", gslab, jnp.concatenate(parts, axis=1), first_row[tail[0]])
    from_sibling = _swap_halves("grads_swap_halves", gslab)
    chip_sum = _add_halves("grads_add_sibling", gslab, from_sibling, c_idx)
    place = jnp.stack([lax.axis_index("x"), lax.axis_index("y")]).astype(jnp.int32)
    from_x, from_y = _exchange_stage1("grads_stage1", chip_sum)
    send_a, keep_a, send_b, keep_b = _exchange_add1("grads_add1", chip_sum, from_x, from_y, place)
    got_a, got_b = _exchange_stage2("grads_stage2", send_a, send_b)
    my_half = _exchange_add2("grads_add2", keep_a, got_a, keep_b, got_b, c_idx)
    gsum = _join_halves("grads_join_halves", my_half)
    grad = dict(zip(order, _unpack(gsum, [w[n].shape for n in order])))

    delta, new_m, new_v = {}, {}, {}
    for n in LARGE:
        delta[n], new_m[n], new_v[n] = _adamw("adamw_" + n, w[n], grad[n], given["m_" + n], given["v_" + n])
    first_small = sum(_rows_of(math.prod(w[n].shape)) for n in LARGE)
    small_shapes = [w[n].shape for n in small]

    def small_slab(arrays):
        rows = _flat_rows(arrays, F32)
        return jnp.pad(rows, ((0, gsum.shape[0] - first_small - rows.shape[0]), (0, 0)))

    res = _adamw("adamw_small", small_slab([w[n] for n in small]), gsum[first_small:],
                 small_slab([given["m_" + n] for n in small]), small_slab([given["v_" + n] for n in small]))
    for out, slab in zip((delta, new_m, new_v), res):
        out.update(zip(small, _unpack(slab, small_shapes)))
    return (loss, grad_x, *[grad[n] for n in names], *[delta[n] for n in names], *[new_m[n] for n in names],
            *[new_v[n] for n in names])


def _full_shape(name, w, kinds):
    shp = w[name].shape
    if kinds[name] == "col":
        return (shp[0], 4 * shp[1])
    if kinds[name] == "row":
        return (4 * shp[0],) + tuple(shp[1:])
    return shp
```
